```python
import jax, jax.numpy as jnp
from jax import lax
import numpy as np

D_MODEL = 1024
BATCH = 8
SEQ = 4096
DEPTH = 1

D_PLE = 256
D_MIX = 2 * D_MODEL
D_SSM = D_MIX // 2
D_CONF = D_MIX - D_SSM
SSM_HEAD_DIM = 64
SSM_HEADS = D_SSM // SSM_HEAD_DIM
SSM_GROUPS = 2
SSM_HPG = SSM_HEADS // SSM_GROUPS
SSM_STATE = 128
SSM_CONV_K = 4
CHUNK = 128
CONF_K = 31
LN_EPS = 1e-5
RMS_EPS = 1e-5

COLS = (D_SSM,
        D_SSM,
        SSM_GROUPS * SSM_STATE,
        SSM_GROUPS * SSM_STATE,
        SSM_HEADS,
        2 * D_CONF,
        D_CONF)
D_IN_PROJ = sum(COLS)
SPLITS = tuple(int(s) for s in np.cumsum(COLS)[:-1])

DEEPNORM_ALPHA = (2.0 * DEPTH) ** 0.25
DEEPNORM_BETA = (8.0 * DEPTH) ** -0.25

kernel_name = "hybrid_ssd_conformer_deepnorm_block"


def layer_norm(x, g, b):
    xf = x.astype(jnp.float32)
    mu = jnp.mean(xf, axis=-1, keepdims=True)
    var = jnp.mean(jnp.square(xf - mu), axis=-1, keepdims=True)
    return ((xf - mu) * lax.rsqrt(var + LN_EPS)).astype(x.dtype) * g + b


def causal_depthwise_conv(x, w, b):
    k = w.shape[0]
    xp = jnp.pad(x, ((0, 0), (k - 1, 0), (0, 0)))
    y = lax.conv_general_dilated(xp, w[:, None, :], window_strides=(1,), padding='VALID',
                                 dimension_numbers=('NWC', 'WIO', 'NWC'),
                                 feature_group_count=x.shape[-1])
    return y + b


def ssd_chunked(xh, dt, a, bm, cm):
    bsz, l, g, j, p = xh.shape
    n = bm.shape[-1]
    c = l // CHUNK
    xdt = (xh * dt[..., None]).reshape(bsz, c, CHUNK, g, j, p)
    adt = jnp.moveaxis((dt * a).reshape(bsz, c, CHUNK, g, j), 2, -1)
    a_cs = jnp.cumsum(adt, axis=-1)
    bc = bm.reshape(bsz, c, CHUNK, g, n)
    cc = cm.reshape(bsz, c, CHUNK, g, n)
    causal = jnp.tril(jnp.ones((CHUNK, CHUNK), dtype=bool))
    seg = a_cs[..., :, None] - a_cs[..., None, :]
    ldec = jnp.exp(jnp.where(causal, seg, -jnp.inf))
    cb = jnp.einsum('bclgn,bcsgn->bcgls', cc, bc)
    y_diag = jnp.einsum('bcgjls,bcsgjp->bclgjp', cb[:, :, :, None] * ldec, xdt)
    decay_states = jnp.exp(a_cs[..., -1:] - a_cs)
    states = jnp.einsum('bclgn,bcgjl,bclgjp->bcgjpn', bc, decay_states, xdt)
    chunk_decay = jnp.exp(a_cs[..., -1])

    def step(h, inp):
        s, d = inp
        return h * d[..., None, None] + s, h

    h0 = jnp.zeros((bsz, g, j, p, n), dtype=xdt.dtype)
    _, prev = lax.scan(step, h0, (jnp.moveaxis(states, 1, 0), jnp.moveaxis(chunk_decay, 1, 0)))
    prev = jnp.moveaxis(prev, 0, 1)
    y_off = jnp.einsum('bclgn,bcgjpn,bcgjl->bclgjp', cc, prev, jnp.exp(a_cs))
    return (y_diag + y_off).reshape(bsz, l, g, j, p)


def _fwd_setup_inputs(seed: int = 0) -> dict:
    key = jax.random.key(seed)
    ks = jax.random.split(key, 26)
    nrm = jax.random.normal
    f32 = jnp.float32
    x = nrm(ks[0], (BATCH, SEQ, D_MODEL), f32)
    p = nrm(ks[1], (DEPTH, BATCH, SEQ, D_PLE), f32)
    ln_emb_g = 1.0 + 0.02 * nrm(ks[2], (D_MODEL,), f32)
    ln_emb_b = 0.02 * nrm(ks[3], (D_MODEL,), f32)
    w_in = nrm(ks[4], (DEPTH, D_MODEL, D_IN_PROJ), f32) * D_MODEL ** -0.5
    d_xbc = D_SSM + 2 * SSM_GROUPS * SSM_STATE
    ssm_conv_w = nrm(ks[5], (DEPTH, SSM_CONV_K, d_xbc), f32) * SSM_CONV_K ** -0.5
    ssm_conv_b = 0.02 * nrm(ks[6], (DEPTH, d_xbc), f32)
    dt0 = jnp.exp(jax.random.uniform(ks[7], (DEPTH, SSM_HEADS), f32, np.log(1e-3), np.log(1e-1)))
    dt_bias = dt0 + jnp.log(-jnp.expm1(-dt0))
    a_log = jnp.log(jax.random.uniform(ks[8], (DEPTH, SSM_HEADS), f32, 1.0, 16.0))
    d_skip = 1.0 + 0.1 * nrm(ks[9], (DEPTH, SSM_HEADS), f32)
    ssm_norm_g = 1.0 + 0.02 * nrm(ks[10], (DEPTH, D_SSM), f32)
    b_glu = 0.02 * nrm(ks[11], (DEPTH, 2 * D_CONF), f32)
    conf_conv_w = nrm(ks[12], (DEPTH, CONF_K, D_CONF), f32) * CONF_K ** -0.5
    conf_conv_b = 0.02 * nrm(ks[13], (DEPTH, D_CONF), f32)
    conf_ln_g = 1.0 + 0.02 * nrm(ks[14], (DEPTH, D_CONF), f32)
    conf_ln_b = 0.02 * nrm(ks[15], (DEPTH, D_CONF), f32)
    w_out = nrm(ks[16], (DEPTH, D_MIX, D_MODEL), f32) * (DEEPNORM_BETA * (2.0 / (D_MIX + D_MODEL)) ** 0.5)
    b_out = 0.02 * nrm(ks[17], (DEPTH, D_MODEL), f32)
    ln1_g = 1.0 + 0.02 * nrm(ks[18], (DEPTH, D_MODEL), f32)
    ln1_b = 0.02 * nrm(ks[19], (DEPTH, D_MODEL), f32)
    w_ple_gate = nrm(ks[20], (DEPTH, D_MODEL, D_MODEL), f32) * D_MODEL ** -0.5
    w_ple_proj = nrm(ks[21], (DEPTH, D_PLE, D_MODEL), f32) * (DEEPNORM_BETA * (2.0 / (D_PLE + D_MODEL)) ** 0.5)
    ln2_g = 1.0 + 0.02 * nrm(ks[22], (DEPTH, D_MODEL), f32)
    ln2_b = 0.02 * nrm(ks[23], (DEPTH, D_MODEL), f32)
    return {"x": x, "p": p, "ln_emb_g": ln_emb_g, "ln_emb_b": ln_emb_b, "w_in": w_in,
            "ssm_conv_w": ssm_conv_w, "ssm_conv_b": ssm_conv_b, "dt_bias": dt_bias,
            "a_log": a_log, "d_skip": d_skip, "ssm_norm_g": ssm_norm_g, "b_glu": b_glu,
            "conf_conv_w": conf_conv_w, "conf_conv_b": conf_conv_b, "conf_ln_g": conf_ln_g,
            "conf_ln_b": conf_ln_b, "w_out": w_out, "b_out": b_out, "ln1_g": ln1_g,
            "ln1_b": ln1_b, "w_ple_gate": w_ple_gate, "w_ple_proj": w_ple_proj,
            "ln2_g": ln2_g, "ln2_b": ln2_b}


def _fwd_reference(x, p, ln_emb_g, ln_emb_b, w_in, ssm_conv_w, ssm_conv_b, dt_bias, a_log,
              d_skip, ssm_norm_g, b_glu, conf_conv_w, conf_conv_b, conf_ln_g, conf_ln_b,
              w_out, b_out, ln1_g, ln1_b, w_ple_gate, w_ple_proj, ln2_g, ln2_b):
    bsz, l, _ = x.shape
    f32 = jnp.float32
    gn = SSM_GROUPS * SSM_STATE
    h = layer_norm(x, ln_emb_g, ln_emb_b)
    for i in range(DEPTH):
        proj = jnp.einsum('bld,de->ble', h, w_in[i])
        xs, z, bm, cm, dt_raw, glu, cgate = jnp.split(proj, SPLITS, axis=-1)

        xbc = jnp.concatenate([xs, bm, cm], axis=-1)
        xbc = jax.nn.silu(causal_depthwise_conv(xbc, ssm_conv_w[i], ssm_conv_b[i]))
        xs_c = xbc[..., :D_SSM]
        bm_c = xbc[..., D_SSM:D_SSM + gn].reshape(bsz, l, SSM_GROUPS, SSM_STATE).astype(f32)
        cm_c = xbc[..., D_SSM + gn:].reshape(bsz, l, SSM_GROUPS, SSM_STATE).astype(f32)
        xh = xs_c.reshape(bsz, l, SSM_GROUPS, SSM_HPG, SSM_HEAD_DIM).astype(f32)
        dt = jax.nn.softplus((dt_raw + dt_bias[i]).astype(f32)).reshape(bsz, l, SSM_GROUPS, SSM_HPG)
        a = -jnp.exp(a_log[i].astype(f32)).reshape(SSM_GROUPS, SSM_HPG)
        y = ssd_chunked(xh, dt, a, bm_c, cm_c)
        y = y + d_skip[i].astype(f32).reshape(SSM_GROUPS, SSM_HPG)[:, :, None] * xh
        yz = y.reshape(bsz, l, SSM_GROUPS, D_SSM // SSM_GROUPS) * \
            jax.nn.silu(z.astype(f32)).reshape(bsz, l, SSM_GROUPS, D_SSM // SSM_GROUPS)
        yz = yz * lax.rsqrt(jnp.mean(jnp.square(yz), axis=-1, keepdims=True) + RMS_EPS)
        y_ssm = yz.reshape(bsz, l, D_SSM).astype(x.dtype) * ssm_norm_g[i]

        glu = glu + b_glu[i]
        u = glu[..., :D_CONF] * jax.nn.sigmoid(glu[..., D_CONF:])
        u = causal_depthwise_conv(u, conf_conv_w[i], conf_conv_b[i])
        u = jax.nn.silu(layer_norm(u, conf_ln_g[i], conf_ln_b[i]))
        y_conf = u * jax.nn.silu(cgate)

        mix = jnp.concatenate([y_ssm, y_conf], axis=-1)
        out = jnp.einsum('ble,ed->bld', mix, w_out[i]) + b_out[i]
        h = layer_norm(DEEPNORM_ALPHA * h + out, ln1_g[i], ln1_b[i])

        gate = jax.nn.sigmoid(jnp.einsum('bld,de->ble', h, w_ple_gate[i]))
        ple = jnp.einsum('blq,qd->bld', p[i], w_ple_proj[i])
        h = layer_norm(DEEPNORM_ALPHA * h + gate * ple, ln2_g[i], ln2_b[i])
    return h


import jax as _jax
import jax.numpy as _jnp

TWIN_FORMAT = 'train_step'
FWD_PARAMS = ['x', 'p', 'ln_emb_g', 'ln_emb_b', 'w_in', 'ssm_conv_w', 'ssm_conv_b', 'dt_bias', 'a_log', 'd_skip', 'ssm_norm_g', 'b_glu', 'conf_conv_w', 'conf_conv_b', 'conf_ln_g', 'conf_ln_b', 'w_out', 'b_out', 'ln1_g', 'ln1_b', 'w_ple_gate', 'w_ple_proj', 'ln2_g', 'ln2_b']
TWIN_WEIGHTS = ['ln_emb_g', 'ln_emb_b', 'w_in', 'ssm_conv_w', 'ssm_conv_b', 'dt_bias', 'a_log', 'd_skip', 'ssm_norm_g', 'b_glu', 'conf_conv_w', 'conf_conv_b', 'conf_ln_g', 'conf_ln_b', 'w_out', 'b_out', 'ln1_g', 'ln1_b', 'w_ple_gate', 'w_ple_proj', 'ln2_g', 'ln2_b']
TWIN_DIFF_INPUT = 'x'
TWIN_INPUTS = ['x', 'p', 'ln_emb_g', 'ln_emb_b', 'w_in', 'ssm_conv_w', 'ssm_conv_b', 'dt_bias', 'a_log', 'd_skip', 'ssm_norm_g', 'b_glu', 'conf_conv_w', 'conf_conv_b', 'conf_ln_g', 'conf_ln_b', 'w_out', 'b_out', 'ln1_g', 'ln1_b', 'w_ple_gate', 'w_ple_proj', 'ln2_g', 'ln2_b', 'loss_target', 'm_ln_emb_g', 'm_ln_emb_b', 'm_w_in', 'm_ssm_conv_w', 'm_ssm_conv_b', 'm_dt_bias', 'm_a_log', 'm_d_skip', 'm_ssm_norm_g', 'm_b_glu', 'm_conf_conv_w', 'm_conf_conv_b', 'm_conf_ln_g', 'm_conf_ln_b', 'm_w_out', 'm_b_out', 'm_ln1_g', 'm_ln1_b', 'm_w_ple_gate', 'm_w_ple_proj', 'm_ln2_g', 'm_ln2_b', 'v_ln_emb_g', 'v_ln_emb_b', 'v_w_in', 'v_ssm_conv_w', 'v_ssm_conv_b', 'v_dt_bias', 'v_a_log', 'v_d_skip', 'v_ssm_norm_g', 'v_b_glu', 'v_conf_conv_w', 'v_conf_conv_b', 'v_conf_ln_g', 'v_conf_ln_b', 'v_w_out', 'v_b_out', 'v_ln1_g', 'v_ln1_b', 'v_w_ple_gate', 'v_w_ple_proj', 'v_ln2_g', 'v_ln2_b']
TWIN_OUTPUTS = ['loss', 'grad_x', 'grad_ln_emb_g', 'grad_ln_emb_b', 'grad_w_in', 'grad_ssm_conv_w', 'grad_ssm_conv_b', 'grad_dt_bias', 'grad_a_log', 'grad_d_skip', 'grad_ssm_norm_g', 'grad_b_glu', 'grad_conf_conv_w', 'grad_conf_conv_b', 'grad_conf_ln_g', 'grad_conf_ln_b', 'grad_w_out', 'grad_b_out', 'grad_ln1_g', 'grad_ln1_b', 'grad_w_ple_gate', 'grad_w_ple_proj', 'grad_ln2_g', 'grad_ln2_b', 'delta_ln_emb_g', 'delta_ln_emb_b', 'delta_w_in', 'delta_ssm_conv_w', 'delta_ssm_conv_b', 'delta_dt_bias', 'delta_a_log', 'delta_d_skip', 'delta_ssm_norm_g', 'delta_b_glu', 'delta_conf_conv_w', 'delta_conf_conv_b', 'delta_conf_ln_g', 'delta_conf_ln_b', 'delta_w_out', 'delta_b_out', 'delta_ln1_g', 'delta_ln1_b', 'delta_w_ple_gate', 'delta_w_ple_proj', 'delta_ln2_g', 'delta_ln2_b', 'new_m_ln_emb_g', 'new_m_ln_emb_b', 'new_m_w_in', 'new_m_ssm_conv_w', 'new_m_ssm_conv_b', 'new_m_dt_bias', 'new_m_a_log', 'new_m_d_skip', 'new_m_ssm_norm_g', 'new_m_b_glu', 'new_m_conf_conv_w', 'new_m_conf_conv_b', 'new_m_conf_ln_g', 'new_m_conf_ln_b', 'new_m_w_out', 'new_m_b_out', 'new_m_ln1_g', 'new_m_ln1_b', 'new_m_w_ple_gate', 'new_m_w_ple_proj', 'new_m_ln2_g', 'new_m_ln2_b', 'new_v_ln_emb_g', 'new_v_ln_emb_b', 'new_v_w_in', 'new_v_ssm_conv_w', 'new_v_ssm_conv_b', 'new_v_dt_bias', 'new_v_a_log', 'new_v_d_skip', 'new_v_ssm_norm_g', 'new_v_b_glu', 'new_v_conf_conv_w', 'new_v_conf_conv_b', 'new_v_conf_ln_g', 'new_v_conf_ln_b', 'new_v_w_out', 'new_v_b_out', 'new_v_ln1_g', 'new_v_ln1_b', 'new_v_w_ple_gate', 'new_v_w_ple_proj', 'new_v_ln2_g', 'new_v_ln2_b']
TWIN_LEAF_KINDS = {'loss': 'loss', 'grad_x': 'grad_x', 'grad_ln_emb_g': 'grad_w', 'grad_ln_emb_b': 'grad_w', 'grad_w_in': 'grad_w', 'grad_ssm_conv_w': 'grad_w', 'grad_ssm_conv_b': 'grad_w', 'grad_dt_bias': 'grad_w', 'grad_a_log': 'grad_w', 'grad_d_skip': 'grad_w', 'grad_ssm_norm_g': 'grad_w', 'grad_b_glu': 'grad_w', 'grad_conf_conv_w': 'grad_w', 'grad_conf_conv_b': 'grad_w', 'grad_conf_ln_g': 'grad_w', 'grad_conf_ln_b': 'grad_w', 'grad_w_out': 'grad_w', 'grad_b_out': 'grad_w', 'grad_ln1_g': 'grad_w', 'grad_ln1_b': 'grad_w', 'grad_w_ple_gate': 'grad_w', 'grad_w_ple_proj': 'grad_w', 'grad_ln2_g': 'grad_w', 'grad_ln2_b': 'grad_w', 'delta_ln_emb_g': 'delta_w', 'delta_ln_emb_b': 'delta_w', 'delta_w_in': 'delta_w', 'delta_ssm_conv_w': 'delta_w', 'delta_ssm_conv_b': 'delta_w', 'delta_dt_bias': 'delta_w', 'delta_a_log': 'delta_w', 'delta_d_skip': 'delta_w', 'delta_ssm_norm_g': 'delta_w', 'delta_b_glu': 'delta_w', 'delta_conf_conv_w': 'delta_w', 'delta_conf_conv_b': 'delta_w', 'delta_conf_ln_g': 'delta_w', 'delta_conf_ln_b': 'delta_w', 'delta_w_out': 'delta_w', 'delta_b_out': 'delta_w', 'delta_ln1_g': 'delta_w', 'delta_ln1_b': 'delta_w', 'delta_w_ple_gate': 'delta_w', 'delta_w_ple_proj': 'delta_w', 'delta_ln2_g': 'delta_w', 'delta_ln2_b': 'delta_w', 'new_m_ln_emb_g': 'new_m', 'new_m_ln_emb_b': 'new_m', 'new_m_w_in': 'new_m', 'new_m_ssm_conv_w': 'new_m', 'new_m_ssm_conv_b': 'new_m', 'new_m_dt_bias': 'new_m', 'new_m_a_log': 'new_m', 'new_m_d_skip': 'new_m', 'new_m_ssm_norm_g': 'new_m', 'new_m_b_glu': 'new_m', 'new_m_conf_conv_w': 'new_m', 'new_m_conf_conv_b': 'new_m', 'new_m_conf_ln_g': 'new_m', 'new_m_conf_ln_b': 'new_m', 'new_m_w_out': 'new_m', 'new_m_b_out': 'new_m', 'new_m_ln1_g': 'new_m', 'new_m_ln1_b': 'new_m', 'new_m_w_ple_gate': 'new_m', 'new_m_w_ple_proj': 'new_m', 'new_m_ln2_g': 'new_m', 'new_m_ln2_b': 'new_m', 'new_v_ln_emb_g': 'new_v', 'new_v_ln_emb_b': 'new_v', 'new_v_w_in': 'new_v', 'new_v_ssm_conv_w': 'new_v', 'new_v_ssm_conv_b': 'new_v', 'new_v_dt_bias': 'new_v', 'new_v_a_log': 'new_v', 'new_v_d_skip': 'new_v', 'new_v_ssm_norm_g': 'new_v', 'new_v_b_glu': 'new_v', 'new_v_conf_conv_w': 'new_v', 'new_v_conf_conv_b': 'new_v', 'new_v_conf_ln_g': 'new_v', 'new_v_conf_ln_b': 'new_v', 'new_v_w_out': 'new_v', 'new_v_b_out': 'new_v', 'new_v_ln1_g': 'new_v', 'new_v_ln1_b': 'new_v', 'new_v_w_ple_gate': 'new_v', 'new_v_w_ple_proj': 'new_v', 'new_v_ln2_g': 'new_v', 'new_v_ln2_b': 'new_v'}


def _forward(args):
    return _fwd_reference(*[args[k] for k in FWD_PARAMS])


def _output_shape():
    def fwd():
        inp = _fwd_setup_inputs(0)
        return _fwd_reference(*[inp[k] for k in FWD_PARAMS])
    out = _jax.eval_shape(fwd)
    return out.shape, out.dtype

N_MICROBATCH = 1
ADAM_LR = 0.001
ADAM_B1 = 0.9
ADAM_B2 = 0.999
ADAM_EPS = 1e-08
ADAM_WD = 0.01
ADAM_STEP = 10
PER_EXAMPLE_BATCH_AXIS = {'x': 0, 'p': 1, 'loss_target': 0}
SHARED_INPUTS = []
_WEIGHT_DTYPES = {'ln_emb_g': _jnp.float32, 'ln_emb_b': _jnp.float32, 'w_in': _jnp.float32, 'ssm_conv_w': _jnp.float32, 'ssm_conv_b': _jnp.float32, 'dt_bias': _jnp.float32, 'a_log': _jnp.float32, 'd_skip': _jnp.float32, 'ssm_norm_g': _jnp.float32, 'b_glu': _jnp.float32, 'conf_conv_w': _jnp.float32, 'conf_conv_b': _jnp.float32, 'conf_ln_g': _jnp.float32, 'conf_ln_b': _jnp.float32, 'w_out': _jnp.float32, 'b_out': _jnp.float32, 'ln1_g': _jnp.float32, 'ln1_b': _jnp.float32, 'w_ple_gate': _jnp.float32, 'w_ple_proj': _jnp.float32, 'ln2_g': _jnp.float32, 'ln2_b': _jnp.float32}
MOMENT_SCALE = {'ln_emb_g': 9.752694e-01, 'ln_emb_b': 5.131035e-01, 'w_in': 4.350612e-02, 'ssm_conv_w': 5.776453e-02, 'ssm_conv_b': 1.077792e-01, 'dt_bias': 1.605881e-01, 'a_log': 1.977489e-01, 'd_skip': 3.554798e-01, 'ssm_norm_g': 7.355355e-02, 'b_glu': 2.096712e-02, 'conf_conv_w': 2.416220e-02, 'conf_conv_b': 5.327358e-02, 'conf_ln_g': 2.996892e-02, 'conf_ln_b': 2.764113e-02, 'w_out': 1.131605e-01, 'b_out': 4.304673e-01, 'ln1_g': 1.194448e+00, 'ln1_b': 5.583519e-01, 'w_ple_gate': 1.165787e-02, 'w_ple_proj': 7.949512e-02, 'ln2_g': 3.205309e+01, 'ln2_b': 2.784253e+00}


def _to_microbatches(a, axis):
    t = _jnp.moveaxis(a, axis, 0)
    t = t.reshape((N_MICROBATCH, t.shape[0] // N_MICROBATCH) + t.shape[1:])
    return _jnp.moveaxis(t, 1, axis + 1)


def setup_inputs(seed: int = 0) -> dict:
    inp = _fwd_setup_inputs(seed)
    key = _jax.random.fold_in(_jax.random.key(seed), 7919)
    shape, _ = _output_shape()
    out = dict(inp)
    out["loss_target"] = _jax.random.normal(_jax.random.fold_in(key, 0), shape, _jnp.float32)
    for i, name in enumerate(TWIN_WEIGHTS):
        w = inp[name].astype(_jnp.float32)
        if MOMENT_SCALE is None:
            s = _jnp.sqrt(_jnp.mean(_jnp.square(w)) + 1e-30)
        else:
            s = MOMENT_SCALE[name]
        km, kv = _jax.random.split(_jax.random.fold_in(key, i + 1))
        out[name] = w
        out["m_" + name] = s * _jax.random.normal(km, w.shape, _jnp.float32)
        out["v_" + name] = (s * s) * _jax.random.uniform(kv, w.shape, _jnp.float32, 0.5, 1.5)
    if N_MICROBATCH > 1:
        for name, axis in PER_EXAMPLE_BATCH_AXIS.items():
            out[name] = _to_microbatches(out[name], axis)
    return {'x': out['x'], 'p': out['p'], 'ln_emb_g': out['ln_emb_g'], 'ln_emb_b': out['ln_emb_b'], 'w_in': out['w_in'], 'ssm_conv_w': out['ssm_conv_w'], 'ssm_conv_b': out['ssm_conv_b'], 'dt_bias': out['dt_bias'], 'a_log': out['a_log'], 'd_skip': out['d_skip'], 'ssm_norm_g': out['ssm_norm_g'], 'b_glu': out['b_glu'], 'conf_conv_w': out['conf_conv_w'], 'conf_conv_b': out['conf_conv_b'], 'conf_ln_g': out['conf_ln_g'], 'conf_ln_b': out['conf_ln_b'], 'w_out': out['w_out'], 'b_out': out['b_out'], 'ln1_g': out['ln1_g'], 'ln1_b': out['ln1_b'], 'w_ple_gate': out['w_ple_gate'], 'w_ple_proj': out['w_ple_proj'], 'ln2_g': out['ln2_g'], 'ln2_b': out['ln2_b'], 'loss_target': out['loss_target'], 'm_ln_emb_g': out['m_ln_emb_g'], 'm_ln_emb_b': out['m_ln_emb_b'], 'm_w_in': out['m_w_in'], 'm_ssm_conv_w': out['m_ssm_conv_w'], 'm_ssm_conv_b': out['m_ssm_conv_b'], 'm_dt_bias': out['m_dt_bias'], 'm_a_log': out['m_a_log'], 'm_d_skip': out['m_d_skip'], 'm_ssm_norm_g': out['m_ssm_norm_g'], 'm_b_glu': out['m_b_glu'], 'm_conf_conv_w': out['m_conf_conv_w'], 'm_conf_conv_b': out['m_conf_conv_b'], 'm_conf_ln_g': out['m_conf_ln_g'], 'm_conf_ln_b': out['m_conf_ln_b'], 'm_w_out': out['m_w_out'], 'm_b_out': out['m_b_out'], 'm_ln1_g': out['m_ln1_g'], 'm_ln1_b': out['m_ln1_b'], 'm_w_ple_gate': out['m_w_ple_gate'], 'm_w_ple_proj': out['m_w_ple_proj'], 'm_ln2_g': out['m_ln2_g'], 'm_ln2_b': out['m_ln2_b'], 'v_ln_emb_g': out['v_ln_emb_g'], 'v_ln_emb_b': out['v_ln_emb_b'], 'v_w_in': out['v_w_in'], 'v_ssm_conv_w': out['v_ssm_conv_w'], 'v_ssm_conv_b': out['v_ssm_conv_b'], 'v_dt_bias': out['v_dt_bias'], 'v_a_log': out['v_a_log'], 'v_d_skip': out['v_d_skip'], 'v_ssm_norm_g': out['v_ssm_norm_g'], 'v_b_glu': out['v_b_glu'], 'v_conf_conv_w': out['v_conf_conv_w'], 'v_conf_conv_b': out['v_conf_conv_b'], 'v_conf_ln_g': out['v_conf_ln_g'], 'v_conf_ln_b': out['v_conf_ln_b'], 'v_w_out': out['v_w_out'], 'v_b_out': out['v_b_out'], 'v_ln1_g': out['v_ln1_g'], 'v_ln1_b': out['v_ln1_b'], 'v_w_ple_gate': out['v_w_ple_gate'], 'v_w_ple_proj': out['v_w_ple_proj'], 'v_ln2_g': out['v_ln2_g'], 'v_ln2_b': out['v_ln2_b']}


def _loss(weights, diff, rest, loss_target):
    with _jax.named_scope("forward"):
        args = {**rest, TWIN_DIFF_INPUT: diff, **{k: w.astype(_WEIGHT_DTYPES[k]) for k, w in weights.items()}}
        y = _forward(args)
    with _jax.named_scope("loss_head"):
        err = _jnp.square(y.astype(_jnp.float32) - loss_target)
        return 0.5 * _jnp.sum(_jnp.mean(err, axis=-1)) if err.ndim else 0.5 * err


def _adamw(w, g, m, v):
    m = ADAM_B1 * m + (1.0 - ADAM_B1) * g
    v = ADAM_B2 * v + (1.0 - ADAM_B2) * _jnp.square(g)
    m_hat = m / (1.0 - ADAM_B1 ** ADAM_STEP)
    v_hat = v / (1.0 - ADAM_B2 ** ADAM_STEP)
    delta = -ADAM_LR * (m_hat / (_jnp.sqrt(v_hat) + ADAM_EPS) + ADAM_WD * w)
    return delta, m, v


def reference(x, p, ln_emb_g, ln_emb_b, w_in, ssm_conv_w, ssm_conv_b, dt_bias, a_log, d_skip, ssm_norm_g, b_glu, conf_conv_w, conf_conv_b, conf_ln_g, conf_ln_b, w_out, b_out, ln1_g, ln1_b, w_ple_gate, w_ple_proj, ln2_g, ln2_b, loss_target, m_ln_emb_g, m_ln_emb_b, m_w_in, m_ssm_conv_w, m_ssm_conv_b, m_dt_bias, m_a_log, m_d_skip, m_ssm_norm_g, m_b_glu, m_conf_conv_w, m_conf_conv_b, m_conf_ln_g, m_conf_ln_b, m_w_out, m_b_out, m_ln1_g, m_ln1_b, m_w_ple_gate, m_w_ple_proj, m_ln2_g, m_ln2_b, v_ln_emb_g, v_ln_emb_b, v_w_in, v_ssm_conv_w, v_ssm_conv_b, v_dt_bias, v_a_log, v_d_skip, v_ssm_norm_g, v_b_glu, v_conf_conv_w, v_conf_conv_b, v_conf_ln_g, v_conf_ln_b, v_w_out, v_b_out, v_ln1_g, v_ln1_b, v_w_ple_gate, v_w_ple_proj, v_ln2_g, v_ln2_b):
    given = dict(x=x, p=p, ln_emb_g=ln_emb_g, ln_emb_b=ln_emb_b, w_in=w_in, ssm_conv_w=ssm_conv_w, ssm_conv_b=ssm_conv_b, dt_bias=dt_bias, a_log=a_log, d_skip=d_skip, ssm_norm_g=ssm_norm_g, b_glu=b_glu, conf_conv_w=conf_conv_w, conf_conv_b=conf_conv_b, conf_ln_g=conf_ln_g, conf_ln_b=conf_ln_b, w_out=w_out, b_out=b_out, ln1_g=ln1_g, ln1_b=ln1_b, w_ple_gate=w_ple_gate, w_ple_proj=w_ple_proj, ln2_g=ln2_g, ln2_b=ln2_b, loss_target=loss_target, m_ln_emb_g=m_ln_emb_g, m_ln_emb_b=m_ln_emb_b, m_w_in=m_w_in, m_ssm_conv_w=m_ssm_conv_w, m_ssm_conv_b=m_ssm_conv_b, m_dt_bias=m_dt_bias, m_a_log=m_a_log, m_d_skip=m_d_skip, m_ssm_norm_g=m_ssm_norm_g, m_b_glu=m_b_glu, m_conf_conv_w=m_conf_conv_w, m_conf_conv_b=m_conf_conv_b, m_conf_ln_g=m_conf_ln_g, m_conf_ln_b=m_conf_ln_b, m_w_out=m_w_out, m_b_out=m_b_out, m_ln1_g=m_ln1_g, m_ln1_b=m_ln1_b, m_w_ple_gate=m_w_ple_gate, m_w_ple_proj=m_w_ple_proj, m_ln2_g=m_ln2_g, m_ln2_b=m_ln2_b, v_ln_emb_g=v_ln_emb_g, v_ln_emb_b=v_ln_emb_b, v_w_in=v_w_in, v_ssm_conv_w=v_ssm_conv_w, v_ssm_conv_b=v_ssm_conv_b, v_dt_bias=v_dt_bias, v_a_log=v_a_log, v_d_skip=v_d_skip, v_ssm_norm_g=v_ssm_norm_g, v_b_glu=v_b_glu, v_conf_conv_w=v_conf_conv_w, v_conf_conv_b=v_conf_conv_b, v_conf_ln_g=v_conf_ln_g, v_conf_ln_b=v_conf_ln_b, v_w_out=v_w_out, v_b_out=v_b_out, v_ln1_g=v_ln1_g, v_ln1_b=v_ln1_b, v_w_ple_gate=v_w_ple_gate, v_w_ple_proj=v_w_ple_proj, v_ln2_g=v_ln2_g, v_ln2_b=v_ln2_b)
    weights = {n: given[n] for n in TWIN_WEIGHTS}
    shared = {n: given[n] for n in SHARED_INPUTS}
    per_example = {n: given[n] for n in ['x', 'p']}
    grad_fn = _jax.value_and_grad(_loss, argnums=(0, 1))

    def one_microbatch(ex, loss_target):
        ex = dict(ex)
        diff = ex.pop(TWIN_DIFF_INPUT)
        return grad_fn(weights, diff, {**shared, **ex}, loss_target)

    if N_MICROBATCH == 1:
        loss, (grad_w, grad_x) = one_microbatch(per_example, given["loss_target"])
    else:
        def body(carry, xs):
            loss_sum, grad_sum = carry
            l_k, (gw_k, gx_k) = one_microbatch(xs[0], xs[1])
            with _jax.named_scope("update"):
                return (loss_sum + l_k, _jax.tree.map(_jnp.add, grad_sum, gw_k)), gx_k

        init = (_jnp.zeros((), _jnp.float32), _jax.tree.map(_jnp.zeros_like, weights))
        (loss, grad_w), grad_x = _jax.lax.scan(body, init, (per_example, given["loss_target"]))
    with _jax.named_scope("update"):
        delta_w, new_m, new_v = {}, {}, {}
        for n in TWIN_WEIGHTS:
            delta_w[n], new_m[n], new_v[n] = _adamw(weights[n], grad_w[n], given["m_" + n], given["v_" + n])
    return (loss, grad_x, *[grad_w[n] for n in TWIN_WEIGHTS], *[delta_w[n] for n in TWIN_WEIGHTS],
            *[new_m[n] for n in TWIN_WEIGHTS], *[new_v[n] for n in TWIN_WEIGHTS])
```

```python
import functools

import numpy as np
import jax
import jax.numpy as jnp
from jax import lax
from jax.experimental import pallas as pl
from jax.experimental.pallas import tpu as pltpu

F32, BF16 = jnp.float32, jnp.bfloat16
S = jax.ShapeDtypeStruct

N_DEV = 8
D = 1024
D_PLE = 256
D_SSM = 1024
D_CONF = 1024
N_HEADS = 16
HEAD = 64
N_STATE = 128
CHUNK = 128
K_SSM = 4
K_CONF = 31
D_IN = 5648
COLS_PER_DEV = D_IN // N_DEV
LN_EPS = 1e-5
RMS_EPS = 1e-5
ALPHA = 2.0 ** 0.25
LANES = 128
TB = 256
RG = 32
HALO_SSM = 8
HALO_CONF = 32
VMEM_LIMIT = 56 * 1024 * 1024

ADAM_LR, ADAM_B1, ADAM_B2, ADAM_EPS, ADAM_WD, ADAM_STEP = 0.001, 0.9, 0.999, 1e-08, 0.01, 10

C_GLUA, C_GLUB, C_XS, C_Z, C_CG = 0, 1, 2, 3, 4
C_BC = 10
N_MAIN = 5632


def _params(sem, vmem=VMEM_LIMIT):
    return pltpu.CompilerParams(dimension_semantics=sem, vmem_limit_bytes=vmem)


def _row(tb, n, col=0):
    return pl.BlockSpec((tb, n), lambda i: (i, col))


def _full(shape):
    return pl.BlockSpec(shape, lambda i: (0,) * len(shape))


def _prev(tb, halo, n, col=0):
    r = tb // halo
    return pl.BlockSpec((halo, n), lambda i: (jnp.maximum(i * r - 1, 0), col))


def _next(tb, halo, n, nt, col=0):
    r = tb // halo
    return pl.BlockSpec((halo, n), lambda i: (jnp.minimum((i + 1) * r, nt * r - 1), col))


def _row_loop(tb, fn):
    def it(r, c):
        fn(pl.ds(pl.multiple_of(r * RG, RG), RG))
        return c
    lax.fori_loop(0, tb // RG, it, 0)


def _col_loop(n, fn):
    def it(j, c):
        fn(pl.ds(pl.multiple_of(j * LANES, LANES), LANES))
        return c
    lax.fori_loop(0, n // LANES, it, 0)


def _sigmoid(x):
    return 1.0 / (1.0 + jnp.exp(-x))


def _dsilu(x, s):
    return s * (1.0 + x * (1.0 - s))


def _ln_stats(v):
    mu = jnp.mean(v, axis=-1, keepdims=True)
    c = v - mu
    r = lax.rsqrt(jnp.mean(c * c, axis=-1, keepdims=True) + LN_EPS)
    return c * r, r


def _ln_bwd(dy, xhat, r, g):
    dxh = dy * g
    dv = r * (dxh - jnp.mean(dxh, axis=-1, keepdims=True) - xhat * jnp.mean(dxh * xhat, axis=-1, keepdims=True))
    return dv, jnp.sum(dy * xhat, axis=0, keepdims=True), jnp.sum(dy, axis=0, keepdims=True)


def _dot(a, b, dims=((1,), (0,))):
    return lax.dot_general(a.astype(BF16), b.astype(BF16), (dims, ((), ())), preferred_element_type=F32)


_NT = ((1,), (1,))
_TN = ((0,), (0,))


def _dot_hi(a, b, dims=((1,), (0,))):
    return lax.dot_general(a, b, (dims, ((), ())), precision=lax.Precision.HIGHEST, preferred_element_type=F32)


def _mm(a, b, mode, name, out_dtype=F32, add=None, tm=512, tn=512, tk=512):
    if mode == "nn":
        (M, K), N = a.shape, b.shape[1]
    elif mode == "tn":
        (K, M), N = a.shape, b.shape[1]
    else:
        (M, K), N = a.shape, b.shape[0]
    tm, tn, tk = min(tm, M), min(tn, N), min(tk, K)
    assert M % tm == 0 and N % tn == 0 and K % tk == 0, (name, M, N, K)
    nk = K // tk
    dims = {"nn": ((1,), (0,)), "tn": _TN, "nt": _NT}[mode]

    def body(*refs):
        if add is None:
            a_ref, b_ref, o_ref, acc = refs
        else:
            a_ref, b_ref, add_ref, o_ref, acc = refs
        k = pl.program_id(2)

        @pl.when(k == 0)
        def _():
            acc[...] = jnp.zeros_like(acc)

        acc[...] += _dot(a_ref[...], b_ref[...], dims)

        @pl.when(k == nk - 1)
        def _():
            r = acc[...]
            if add is not None:
                r = r + add_ref[...]
            o_ref[...] = r.astype(out_dtype)

    a_spec = pl.BlockSpec((tk, tm), lambda i, j, k: (k, i)) if mode == "tn" else pl.BlockSpec((tm, tk), lambda i, j, k: (i, k))
    b_spec = pl.BlockSpec((tn, tk), lambda i, j, k: (j, k)) if mode == "nt" else pl.BlockSpec((tk, tn), lambda i, j, k: (k, j))
    o_spec = pl.BlockSpec((tm, tn), lambda i, j, k: (i, j))
    ins, specs = [a, b], [a_spec, b_spec]
    if add is not None:
        ins.append(add)
        specs.append(o_spec)
    return pl.pallas_call(
        body, name=name, grid=(M // tm, N // tn, nk), in_specs=specs, out_specs=o_spec,
        out_shape=S((M, N), out_dtype), scratch_shapes=[pltpu.VMEM((tm, tn), F32)],
        compiler_params=_params(("parallel", "parallel", "arbitrary")))(*ins)


def _ln_emb_fwd(x, g, b):
    T = x.shape[0]

    def body(x_ref, g_ref, b_ref, h_ref, hb_ref):
        def rows(rs):
            xh, _ = _ln_stats(x_ref[rs, :])
            h = xh * g_ref[...] + b_ref[...]
            h_ref[rs, :] = h
            hb_ref[rs, :] = h.astype(BF16)
        _row_loop(TB, rows)

    return pl.pallas_call(
        body, name="ln_emb_fwd", grid=(T // TB,),
        in_specs=[_row(TB, D), _full((1, D)), _full((1, D))], out_specs=[_row(TB, D), _row(TB, D)],
        out_shape=[S((T, D), F32), S((T, D), BF16)], compiler_params=_params(("parallel",)))(x, g, b)


def _post1_fwd(h0, out, b_out, g, b):
    T = h0.shape[0]

    def body(h0_ref, out_ref, bo_ref, g_ref, b_ref, h_ref, hb_ref):
        def rows(rs):
            xh, _ = _ln_stats(ALPHA * h0_ref[rs, :] + out_ref[rs, :] + bo_ref[...])
            h = xh * g_ref[...] + b_ref[...]
            h_ref[rs, :] = h
            hb_ref[rs, :] = h.astype(BF16)
        _row_loop(TB, rows)

    return pl.pallas_call(
        body, name="post1_fwd", grid=(T // TB,),
        in_specs=[_row(TB, D), _row(TB, D)] + [_full((1, D))] * 3, out_specs=[_row(TB, D), _row(TB, D)],
        out_shape=[S((T, D), F32), S((T, D), BF16)], compiler_params=_params(("parallel",)))(h0, out, b_out, g, b)


def _post2(h1, gpre, ple, tgt, g, b):
    T = h1.shape[0]

    def body(h1_ref, gp_ref, ple_ref, tgt_ref, g_ref, b_ref, dh1_ref, dgp_ref, dple_ref, loss_ref, dg_ref, db_ref):
        @pl.when(pl.program_id(0) == 0)
        def _():
            loss_ref[...] = jnp.zeros_like(loss_ref)
            dg_ref[...] = jnp.zeros_like(dg_ref)
            db_ref[...] = jnp.zeros_like(db_ref)

        def rows(rs):
            gate = _sigmoid(gp_ref[rs, :])
            ple = ple_ref[rs, :]
            xh, r = _ln_stats(ALPHA * h1_ref[rs, :] + gate * ple)
            err = xh * g_ref[...] + b_ref[...] - tgt_ref[rs, :]
            loss_ref[...] += 0.5 * jnp.sum(jnp.mean(err * err, axis=-1, keepdims=True), axis=0, keepdims=True)
            dv, dg, db = _ln_bwd(err * (1.0 / D), xh, r, g_ref[...])
            dg_ref[...] += dg
            db_ref[...] += db
            dh1_ref[rs, :] = ALPHA * dv
            dgp_ref[rs, :] = (dv * ple * gate * (1.0 - gate)).astype(BF16)
            dple_ref[rs, :] = (dv * gate).astype(BF16)
        _row_loop(TB, rows)

    return pl.pallas_call(
        body, name="post2", grid=(T // TB,),
        in_specs=[_row(TB, D)] * 4 + [_full((1, D))] * 2,
        out_specs=[_row(TB, D)] * 3 + [_full((8, LANES)), _full((1, D)), _full((1, D))],
        out_shape=[S((T, D), F32), S((T, D), BF16), S((T, D), BF16), S((8, LANES), F32), S((1, D), F32), S((1, D), F32)],
        compiler_params=_params(("arbitrary",)))(h1, gpre, ple, tgt, g, b)


def _post1_bwd(dh1a, dh1b, h0, out, b_out, g):
    T = h0.shape[0]

    def body(da_ref, db2_ref, h0_ref, out_ref, bo_ref, g_ref, dout_ref, dh0_ref, dg_ref, db_ref, dbo_ref):
        @pl.when(pl.program_id(0) == 0)
        def _():
            dg_ref[...] = jnp.zeros_like(dg_ref)
            db_ref[...] = jnp.zeros_like(db_ref)
            dbo_ref[...] = jnp.zeros_like(dbo_ref)

        def rows(rs):
            xh, r = _ln_stats(ALPHA * h0_ref[rs, :] + out_ref[rs, :] + bo_ref[...])
            dv, dg, db = _ln_bwd(da_ref[rs, :] + db2_ref[rs, :], xh, r, g_ref[...])
            dg_ref[...] += dg
            db_ref[...] += db
            dbo_ref[...] += jnp.sum(dv, axis=0, keepdims=True)
            dout_ref[rs, :] = dv.astype(BF16)
            dh0_ref[rs, :] = ALPHA * dv
        _row_loop(TB, rows)

    return pl.pallas_call(
        body, name="post1_bwd", grid=(T // TB,),
        in_specs=[_row(TB, D)] * 4 + [_full((1, D))] * 2,
        out_specs=[_row(TB, D)] * 2 + [_full((1, D))] * 3,
        out_shape=[S((T, D), BF16), S((T, D), F32)] + [S((1, D), F32)] * 3,
        compiler_params=_params(("arbitrary",)))(dh1a, dh1b, h0, out, b_out, g)


def _ln_emb_bwd(x, dh0, g):
    T = x.shape[0]

    def body(x_ref, dh_ref, g_ref, dx_ref, dg_ref, db_ref):
        @pl.when(pl.program_id(0) == 0)
        def _():
            dg_ref[...] = jnp.zeros_like(dg_ref)
            db_ref[...] = jnp.zeros_like(db_ref)

        def rows(rs):
            xh, r = _ln_stats(x_ref[rs, :])
            dv, dg, db = _ln_bwd(dh_ref[rs, :], xh, r, g_ref[...])
            dg_ref[...] += dg
            db_ref[...] += db
            dx_ref[rs, :] = dv
        _row_loop(TB, rows)

    return pl.pallas_call(
        body, name="ln_emb_bwd", grid=(T // TB,),
        in_specs=[_row(TB, D), _row(TB, D), _full((1, D))], out_specs=[_row(TB, D), _full((1, D)), _full((1, D))],
        out_shape=[S((T, D), F32), S((1, D), F32), S((1, D), F32)],
        compiler_params=_params(("arbitrary",)))(x, dh0, g)


def _softplus(x):
    return jnp.maximum(x, 0.0) + jnp.log1p(jnp.exp(-jnp.abs(x)))


def _ssd_pre_fwd(proj, dt_raw, wx, wb, bx, bb, dt_bias):
    T = proj.shape[0]
    H = HALO_SSM

    def body(xs_ref, xsp_ref, bc_ref, bcp_ref, dtr_ref, wx_ref, wb_ref, bx_ref, bb_ref, dtb_ref,
             xso_ref, bco_ref, dto_ref, extx, extb):
        first = pl.program_id(0) == 0

        def conv(t_ref, p_ref, w_ref, b_ref, o_ref, ext, n):
            def blk(cols):
                ext[0:H, cols] = jnp.where(first, 0.0, p_ref[:, cols])
                ext[H:, cols] = t_ref[:, cols]
                for r0 in range(0, TB, 64):
                    acc = jnp.broadcast_to(b_ref[:, cols], (64, LANES))
                    for k in range(K_SSM):
                        acc = acc + w_ref[k:k + 1, cols] * ext[pl.ds(r0 + H - (K_SSM - 1) + k, 64), cols]
                    o_ref[pl.ds(r0, 64), cols] = acc * _sigmoid(acc)
            _col_loop(n, blk)

        conv(xs_ref, xsp_ref, wx_ref, bx_ref, xso_ref, extx, D_SSM)
        conv(bc_ref, bcp_ref, wb_ref, bb_ref, bco_ref, extb, 512)
        dto_ref[...] = _softplus(dtr_ref[...] + dtb_ref[...])

    return pl.pallas_call(
        body, name="ssd_pre_fwd", grid=(T // TB,),
        in_specs=[_row(TB, 1024, C_XS), _prev(TB, H, 1024, C_XS), _row(TB, 512, C_BC), _prev(TB, H, 512, C_BC),
                  _row(TB, LANES), _full((K_SSM, 1024)), _full((K_SSM, 512)), _full((1, 1024)), _full((1, 512)),
                  _full((1, LANES))],
        out_specs=[_row(TB, 1024), _row(TB, 512), _row(TB, LANES)],
        out_shape=[S((T, 1024), F32), S((T, 512), F32), S((T, LANES), F32)],
        scratch_shapes=[pltpu.VMEM((H + TB, 1024), F32), pltpu.VMEM((H + TB, 512), F32)],
        compiler_params=_params(("parallel",)))(proj, proj, proj, proj, dt_raw, wx, wb, bx, bb, dt_bias)


def _ssd_pre_bwd(proj, dxs_c, dbc_c, ddt, dt_raw, wx, wb, bx, bb, dt_bias):
    T = proj.shape[0]
    nt = T // TB
    H = HALO_SSM
    R = TB + H

    def body(xs_ref, xsp_ref, xsn_ref, bc_ref, bcp_ref, bcn_ref, dx_ref, dxn_ref, db_ref, dbn_ref, ddt_ref, dtr_ref,
             wx_ref, wb_ref, bx_ref, bb_ref, dtb_ref,
             dxo_ref, dbo_ref, ddto_ref, dwx_ref, dwb_ref, dbx_ref, dbb_ref, ddtb_ref, extx, extb, dpx, dpb):
        i = pl.program_id(0)
        first, last = i == 0, i == nt - 1

        @pl.when(first)
        def _():
            for r in (dwx_ref, dwb_ref, dbx_ref, dbb_ref, ddtb_ref):
                r[...] = jnp.zeros_like(r)

        def conv_bwd(t_ref, p_ref, n_ref, d_ref, dn_ref, w_ref, b_ref, o_ref, dw_ref, dbias_ref, ext, dp, n):
            def blk(cols):
                ext[0:H, cols] = jnp.where(first, 0.0, p_ref[:, cols])
                ext[H:H + TB, cols] = t_ref[:, cols]
                ext[H + TB:, cols] = n_ref[:, cols]
                pre = jnp.broadcast_to(b_ref[:, cols], (R, LANES))
                for k in range(K_SSM):
                    pre = pre + w_ref[k:k + 1, cols] * ext[pl.ds(H - (K_SSM - 1) + k, R), cols]
                s = _sigmoid(pre)
                ds = _dsilu(pre, s)
                dp[0:TB, cols] = d_ref[:, cols] * ds[0:TB]
                dp[TB:, cols] = jnp.where(last, 0.0, dn_ref[:, cols] * ds[TB:])
                dpt = dp[0:TB, cols]
                dbias_ref[:, cols] += jnp.sum(dpt, axis=0, keepdims=True)
                acc = jnp.zeros((TB, LANES), F32)
                for k in range(K_SSM):
                    dw_ref[k:k + 1, cols] += jnp.sum(dpt * ext[pl.ds(H - (K_SSM - 1) + k, TB), cols], axis=0, keepdims=True)
                    acc = acc + w_ref[k:k + 1, cols] * dp[pl.ds(K_SSM - 1 - k, TB), cols]
                o_ref[:, cols] = acc.astype(BF16)
            _col_loop(n, blk)

        conv_bwd(xs_ref, xsp_ref, xsn_ref, dx_ref, dxn_ref, wx_ref, bx_ref, dxo_ref, dwx_ref, dbx_ref, extx, dpx, D_SSM)
        conv_bwd(bc_ref, bcp_ref, bcn_ref, db_ref, dbn_ref, wb_ref, bb_ref, dbo_ref, dwb_ref, dbb_ref, extb, dpb, 512)
        g = ddt_ref[...] * _sigmoid(dtr_ref[...] + dtb_ref[...])
        ddto_ref[...] = g.astype(BF16)
        ddtb_ref[...] += jnp.sum(g, axis=0, keepdims=True)

    return pl.pallas_call(
        body, name="ssd_pre_bwd", grid=(nt,),
        in_specs=[_row(TB, 1024, C_XS), _prev(TB, H, 1024, C_XS), _next(TB, H, 1024, nt, C_XS),
                  _row(TB, 512, C_BC), _prev(TB, H, 512, C_BC), _next(TB, H, 512, nt, C_BC),
                  _row(TB, 1024), _next(TB, H, 1024, nt), _row(TB, 512), _next(TB, H, 512, nt),
                  _row(TB, LANES), _row(TB, LANES),
                  _full((K_SSM, 1024)), _full((K_SSM, 512)), _full((1, 1024)), _full((1, 512)), _full((1, LANES))],
        out_specs=[_row(TB, 1024), _row(TB, 512), _row(TB, LANES),
                   _full((K_SSM, 1024)), _full((K_SSM, 512)), _full((1, 1024)), _full((1, 512)), _full((1, LANES))],
        out_shape=[S((T, 1024), BF16), S((T, 512), BF16), S((T, LANES), BF16),
                   S((K_SSM, 1024), F32), S((K_SSM, 512), F32), S((1, 1024), F32), S((1, 512), F32), S((1, LANES), F32)],
        scratch_shapes=[pltpu.VMEM((H + TB + H, 1024), F32), pltpu.VMEM((H + TB + H, 512), F32),
                        pltpu.VMEM((R, 1024), F32), pltpu.VMEM((R, 512), F32)],
        compiler_params=_params(("arbitrary",)))(
            proj, proj, proj, proj, proj, proj, dxs_c, dxs_c, dbc_c, dbc_c, ddt, dt_raw, wx, wb, bx, bb, dt_bias)


def _ssd_consts():
    ex = np.zeros((LANES, D_SSM), np.float32)
    for h in range(N_HEADS):
        ex[h, h * HEAD:(h + 1) * HEAD] = 1.0
    tri = np.tril(np.ones((CHUNK, CHUNK), np.float32))
    return jnp.asarray(ex), jnp.asarray(ex.T.copy()), jnp.asarray(tri), jnp.asarray(tri.T.copy())


def _ssd_common(xs, dt, alog_ref, ex_ref, tri_ref):
    lane = lax.broadcasted_iota(jnp.int32, (1, LANES), 1)
    a = jnp.where(lane < N_HEADS, -jnp.exp(alog_ref[...]), 0.0)
    A = _dot_hi(tri_ref[...], dt * a)
    ex = ex_ref[...]
    Aex = _dot_hi(A, ex)
    dtex = _dot_hi(dt, ex)
    expA = jnp.exp(Aex)
    dec = jnp.exp(Aex[CHUNK - 1:CHUNK, :] - Aex)
    cd = _dot_hi(ex, jnp.broadcast_to(jnp.exp(A.T[:, CHUNK - 1:CHUNK]), (LANES, LANES)), _TN)
    return a, A, dtex, expA, dec, cd


def _decay_mask():
    sub = lax.broadcasted_iota(jnp.int32, (CHUNK, CHUNK), 0)
    lane = lax.broadcasted_iota(jnp.int32, (CHUNK, CHUNK), 1)
    return sub, lane, sub >= lane


def _ssd_fwd(xs_c, bc_c, dt, proj, alog, dskip_row, norm_g):
    T = xs_c.shape[0]
    nc = T // CHUNK
    ex, _, tri, _ = _ssd_consts()

    def body(xs_ref, bc_ref, dt_ref, z_ref, alog_ref, dsk_ref, ng_ref, ex_ref, tri_ref,
             ys_ref, ypre_ref, hprev_ref, Hs, ybuf):
        @pl.when(pl.program_id(0) == 0)
        def _():
            Hs[...] = jnp.zeros_like(Hs)

        hprev_ref[0] = Hs[...]
        xs, dt = xs_ref[...], dt_ref[...]
        a, A, dtex, expA, dec, cd = _ssd_common(xs, dt, alog_ref, ex_ref, tri_ref)
        AT = A.T
        xdt = xs * dtex
        xdec = xdt * dec
        _, _, causal = _decay_mask()
        for g in range(2):
            gs = slice(g * 512, (g + 1) * 512)
            B = bc_ref[:, g * N_STATE:(g + 1) * N_STATE]
            C = bc_ref[:, 256 + g * N_STATE:256 + (g + 1) * N_STATE]
            cb = _dot(C, B, _NT)
            Hg = Hs[gs, :]
            yoff = _dot(C, Hg, _NT) * expA[:, gs]
            for j in range(8):
                h = g * 8 + j
                hs = slice(h * HEAD, (h + 1) * HEAD)
                L = jnp.exp(jnp.where(causal, A[:, h:h + 1] - AT[h:h + 1, :], -1e30))
                ybuf[:, hs] = _dot(cb * L, xdt[:, hs]) + yoff[:, j * HEAD:(j + 1) * HEAD]
            Hs[gs, :] = cd[gs, :] * Hg + _dot(xdec[:, gs], B, _TN)
        ypre = ybuf[...] + dsk_ref[...] * xs
        ypre_ref[...] = ypre
        z = z_ref[...]
        yz = ypre * (z * _sigmoid(z))
        for g in range(2):
            gs = slice(g * 512, (g + 1) * 512)
            v = yz[:, gs]
            r = lax.rsqrt(jnp.mean(v * v, axis=-1, keepdims=True) + RMS_EPS)
            ys_ref[:, gs] = (v * r * ng_ref[:, gs]).astype(BF16)

    return pl.pallas_call(
        body, name="ssd_fwd", grid=(nc,),
        in_specs=[_row(CHUNK, 1024), _row(CHUNK, 512), _row(CHUNK, LANES), _row(CHUNK, 1024, C_Z),
                  _full((1, LANES)), _full((1, 1024)), _full((1, 1024)), _full((LANES, 1024)), _full((CHUNK, CHUNK))],
        out_specs=[_row(CHUNK, 1024), _row(CHUNK, 1024), pl.BlockSpec((1, 1024, N_STATE), lambda c: (c, 0, 0))],
        out_shape=[S((T, 1024), BF16), S((T, 1024), F32), S((nc, 1024, N_STATE), F32)],
        scratch_shapes=[pltpu.VMEM((1024, N_STATE), F32), pltpu.VMEM((CHUNK, 1024), F32)],
        compiler_params=_params(("arbitrary",)))(xs_c, bc_c, dt, proj, alog, dskip_row, norm_g, ex, tri)


def _ssd_bwd(xs_c, bc_c, dt, proj, ypre, hprev, dmix, alog, dskip_row, norm_g):
    T = xs_c.shape[0]
    nc = T // CHUNK
    ex, ext, tri, triu = _ssd_consts()
    rev = lambda n, col=0: pl.BlockSpec((CHUNK, n), lambda c: (nc - 1 - c, col))

    def body(xs_ref, bc_ref, dt_ref, z_ref, ypre_ref, hprev_ref, dys_ref, alog_ref, dsk_ref, ng_ref,
             ex_ref, ext_ref, tri_ref, triu_ref,
             dxs_ref, dbc_ref, ddt_ref, dz_ref, dng_ref, ddsk_ref, dalog_ref, dHs, dxbuf, dskacc):
        c = pl.program_id(0)

        @pl.when(c == 0)
        def _():
            dHs[...] = jnp.zeros_like(dHs)
            dng_ref[...] = jnp.zeros_like(dng_ref)
            dalog_ref[...] = jnp.zeros_like(dalog_ref)
            dskacc[...] = jnp.zeros_like(dskacc)

        xs, dt, z, ypre, dys = xs_ref[...], dt_ref[...], z_ref[...], ypre_ref[...], dys_ref[...]
        sg = _sigmoid(z)
        sz = z * sg
        yz = ypre * sz
        dyz_parts = []
        for g in range(2):
            gs = slice(g * 512, (g + 1) * 512)
            v = yz[:, gs]
            r = lax.rsqrt(jnp.mean(v * v, axis=-1, keepdims=True) + RMS_EPS)
            vn = v * r
            dng_ref[:, gs] += jnp.sum(dys[:, gs] * vn, axis=0, keepdims=True)
            dvn = dys[:, gs] * ng_ref[:, gs]
            dyz_parts.append(r * (dvn - vn * jnp.mean(dvn * vn, axis=-1, keepdims=True)))
        dyz = jnp.concatenate(dyz_parts, axis=1)
        dy = dyz * sz
        dz_ref[...] = (dyz * ypre * _dsilu(z, sg)).astype(BF16)
        dskacc[...] += jnp.sum(dy * xs, axis=0, keepdims=True)

        a, A, dtex, expA, dec, cd = _ssd_common(xs, dt, alog_ref, ex_ref, tri_ref)
        AT = A.T
        xdt = xs * dtex
        xdec = xdt * dec
        dye = dy * expA
        H = hprev_ref[0]
        dHn = dHs[...]
        sub, lane, causal = _decay_mask()
        dAc = jnp.zeros((CHUNK, LANES), F32)
        Rm = jnp.zeros((CHUNK, LANES), F32)
        yoff_parts, q_parts = [], []
        for g in range(2):
            gs = slice(g * 512, (g + 1) * 512)
            B = bc_ref[:, g * N_STATE:(g + 1) * N_STATE]
            C = bc_ref[:, 256 + g * N_STATE:256 + (g + 1) * N_STATE]
            cb = _dot(C, B, _NT)
            Hg, dHg = H[gs, :], dHn[gs, :]
            Q = _dot(B, dHg, _NT)
            yoff_parts.append(_dot(C, Hg, _NT) * expA[:, gs])
            q_parts.append(Q)
            dcb = jnp.zeros((CHUNK, CHUNK), F32)
            for j in range(8):
                h = g * 8 + j
                hs = slice(h * HEAD, (h + 1) * HEAD)
                L = jnp.exp(jnp.where(causal, A[:, h:h + 1] - AT[h:h + 1, :], -1e30))
                M = cb * L
                G = _dot(dy[:, hs], xdt[:, hs], _NT)
                dxbuf[:, hs] = _dot(M, dy[:, hs], _TN)
                dcb = dcb + G * L
                E = G * M
                dAc = jnp.where(lane == h, jnp.sum(E, axis=1, keepdims=True), dAc)
                Rm = jnp.where(sub == h, jnp.sum(E, axis=0, keepdims=True), Rm)
            dbc_ref[:, g * N_STATE:(g + 1) * N_STATE] = _dot(dcb, C, _TN) + _dot(xdec[:, gs], dHg)
            dbc_ref[:, 256 + g * N_STATE:256 + (g + 1) * N_STATE] = _dot(dcb, B) + _dot(dye[:, gs], Hg)
            dHs[gs, :] = cd[gs, :] * dHg + _dot(dye[:, gs], C, _TN)
        yoff = jnp.concatenate(yoff_parts, axis=1)
        Qd = jnp.concatenate(q_parts, axis=1) * dec
        dxdt = dxbuf[...] + Qd
        extm = ext_ref[...]
        red_s = _dot_hi(xdt * Qd, extm)
        dA = dAc - Rm.T + _dot_hi(dy * yoff, extm) - red_s
        hd = jnp.sum(_dot_hi(H * dHn, extm, _TN), axis=0, keepdims=True)
        last_add = jnp.sum(red_s, axis=0, keepdims=True) + jnp.exp(A[CHUNK - 1:CHUNK, :]) * hd
        dA = dA + jnp.where(sub == CHUNK - 1, last_add, 0.0)
        dadt = _dot_hi(triu_ref[...], dA)
        ddt_ref[...] = dadt * a + _dot_hi(dxdt * xs, extm)
        dalog_ref[...] += jnp.sum(dadt * dt, axis=0, keepdims=True) * a
        dxs_ref[...] = dxdt * dtex + dsk_ref[...] * dy

        @pl.when(c == nc - 1)
        def _():
            ddsk_ref[...] = _dot_hi(jnp.broadcast_to(dskacc[...], (8, 1024)), extm)[0:1, :]

    return pl.pallas_call(
        body, name="ssd_bwd", grid=(nc,),
        in_specs=[rev(1024), rev(512), rev(LANES), rev(1024, C_Z), rev(1024),
                  pl.BlockSpec((1, 1024, N_STATE), lambda c: (nc - 1 - c, 0, 0)), rev(1024, 0),
                  _full((1, LANES)), _full((1, 1024)), _full((1, 1024)),
                  _full((LANES, 1024)), _full((1024, LANES)), _full((CHUNK, CHUNK)), _full((CHUNK, CHUNK))],
        out_specs=[rev(1024), rev(512), rev(LANES), rev(1024), _full((1, 1024)), _full((1, LANES)), _full((1, LANES))],
        out_shape=[S((T, 1024), F32), S((T, 512), F32), S((T, LANES), F32), S((T, 1024), BF16),
                   S((1, 1024), F32), S((1, LANES), F32), S((1, LANES), F32)],
        scratch_shapes=[pltpu.VMEM((1024, N_STATE), F32), pltpu.VMEM((CHUNK, 1024), F32), pltpu.VMEM((1, 1024), F32)],
        compiler_params=_params(("arbitrary",)))(
            xs_c, bc_c, dt, proj, ypre, hprev, dmix, alog, dskip_row, norm_g, ex, ext, tri, triu)


def _conf_fwd(proj, w, cb, lg, lb, ba, bb):
    T = proj.shape[0]
    H = HALO_CONF

    def body(ga_ref, gap_ref, gb_ref, gbp_ref, cg_ref, w_ref, cb_ref, lg_ref, lb_ref, ba_ref, bb_ref,
             u1_ref, yc_ref, ext):
        first = pl.program_id(0) == 0

        def blk(cols):
            up = (gap_ref[:, cols] + ba_ref[:, cols]) * _sigmoid(gbp_ref[:, cols] + bb_ref[:, cols])
            ext[0:H, cols] = jnp.where(first, 0.0, up)
            ext[H:, cols] = (ga_ref[:, cols] + ba_ref[:, cols]) * _sigmoid(gb_ref[:, cols] + bb_ref[:, cols])
            for r0 in range(0, TB, 64):
                acc = jnp.broadcast_to(cb_ref[:, cols], (64, LANES))
                for k in range(K_CONF):
                    acc = acc + w_ref[k:k + 1, cols] * ext[pl.ds(r0 + H - (K_CONF - 1) + k, 64), cols]
                u1_ref[pl.ds(r0, 64), cols] = acc
        _col_loop(D_CONF, blk)

        def rows(rs):
            xh, _ = _ln_stats(u1_ref[rs, :])
            u2 = xh * lg_ref[...] + lb_ref[...]
            cg = cg_ref[rs, :]
            yc_ref[rs, :] = (u2 * _sigmoid(u2) * cg * _sigmoid(cg)).astype(BF16)
        _row_loop(TB, rows)

    return pl.pallas_call(
        body, name="conf_fwd", grid=(T // TB,),
        in_specs=[_row(TB, 1024, C_GLUA), _prev(TB, H, 1024, C_GLUA), _row(TB, 1024, C_GLUB), _prev(TB, H, 1024, C_GLUB),
                  _row(TB, 1024, C_CG), _full((K_CONF, 1024))] + [_full((1, 1024))] * 5,
        out_specs=[_row(TB, 1024), _row(TB, 1024)],
        out_shape=[S((T, 1024), F32), S((T, 1024), BF16)],
        scratch_shapes=[pltpu.VMEM((H + TB, 1024), F32)],
        compiler_params=_params(("parallel",)))(proj, proj, proj, proj, proj, w, cb, lg, lb, ba, bb)


def _conf_bwd1(dmix, u1, proj, lg, lb):
    T = u1.shape[0]

    def body(dy_ref, u1_ref, cg_ref, lg_ref, lb_ref, du1_ref, dcg_ref, dg_ref, db_ref):
        @pl.when(pl.program_id(0) == 0)
        def _():
            dg_ref[...] = jnp.zeros_like(dg_ref)
            db_ref[...] = jnp.zeros_like(db_ref)

        def rows(rs):
            xh, r = _ln_stats(u1_ref[rs, :])
            u2 = xh * lg_ref[...] + lb_ref[...]
            s2 = _sigmoid(u2)
            cg = cg_ref[rs, :]
            sc = _sigmoid(cg)
            dy = dy_ref[rs, :]
            dcg_ref[rs, :] = (dy * u2 * s2 * _dsilu(cg, sc)).astype(BF16)
            dv, dg, db = _ln_bwd(dy * cg * sc * _dsilu(u2, s2), xh, r, lg_ref[...])
            dg_ref[...] += dg
            db_ref[...] += db
            du1_ref[rs, :] = dv
        _row_loop(TB, rows)

    return pl.pallas_call(
        body, name="conf_bwd1", grid=(T // TB,),
        in_specs=[_row(TB, 1024, 1), _row(TB, 1024), _row(TB, 1024, C_CG), _full((1, 1024)), _full((1, 1024))],
        out_specs=[_row(TB, 1024), _row(TB, 1024), _full((1, 1024)), _full((1, 1024))],
        out_shape=[S((T, 1024), F32), S((T, 1024), BF16), S((1, 1024), F32), S((1, 1024), F32)],
        compiler_params=_params(("arbitrary",)))(dmix, u1, proj, lg, lb)


def _conf_bwd2(proj, du1, w, ba, bb):
    T = du1.shape[0]
    nt = T // TB
    H = HALO_CONF

    def body(ga_ref, gap_ref, gb_ref, gbp_ref, du_ref, dun_ref, w_ref, ba_ref, bb_ref,
             dga_ref, dgb_ref, dw_ref, dcb_ref, dba_ref, dbb_ref, ext, dext):
        i = pl.program_id(0)
        first, last = i == 0, i == nt - 1

        @pl.when(first)
        def _():
            for r in (dw_ref, dcb_ref, dba_ref, dbb_ref):
                r[...] = jnp.zeros_like(r)

        def blk(cols):
            up = (gap_ref[:, cols] + ba_ref[:, cols]) * _sigmoid(gbp_ref[:, cols] + bb_ref[:, cols])
            ext[0:H, cols] = jnp.where(first, 0.0, up)
            a = ga_ref[:, cols] + ba_ref[:, cols]
            sb = _sigmoid(gb_ref[:, cols] + bb_ref[:, cols])
            ext[H:, cols] = a * sb
            du = du_ref[:, cols]
            dext[0:TB, cols] = du
            dext[TB:, cols] = jnp.where(last, 0.0, dun_ref[:, cols])
            dcb_ref[:, cols] += jnp.sum(du, axis=0, keepdims=True)
            for r0 in range(0, TB, 64):
                dur = du_ref[pl.ds(r0, 64), cols]
                acc = jnp.zeros((64, LANES), F32)
                for k in range(K_CONF):
                    dw_ref[k:k + 1, cols] += jnp.sum(
                        dur * ext[pl.ds(r0 + H - (K_CONF - 1) + k, 64), cols], axis=0, keepdims=True)
                    acc = acc + w_ref[k:k + 1, cols] * dext[pl.ds(r0 + K_CONF - 1 - k, 64), cols]
                ar, sr = a[r0:r0 + 64], sb[r0:r0 + 64]
                da = acc * sr
                dbv = acc * ar * sr * (1.0 - sr)
                dga_ref[pl.ds(r0, 64), cols] = da.astype(BF16)
                dgb_ref[pl.ds(r0, 64), cols] = dbv.astype(BF16)
                dba_ref[:, cols] += jnp.sum(da, axis=0, keepdims=True)
                dbb_ref[:, cols] += jnp.sum(dbv, axis=0, keepdims=True)
        _col_loop(D_CONF, blk)

    return pl.pallas_call(
        body, name="conf_bwd2", grid=(nt,),
        in_specs=[_row(TB, 1024, C_GLUA), _prev(TB, H, 1024, C_GLUA), _row(TB, 1024, C_GLUB), _prev(TB, H, 1024, C_GLUB),
                  _row(TB, 1024), _next(TB, H, 1024, nt), _full((K_CONF, 1024)), _full((1, 1024)), _full((1, 1024))],
        out_specs=[_row(TB, 1024), _row(TB, 1024), _full((K_CONF, 1024)), _full((1, 1024)), _full((1, 1024)), _full((1, 1024))],
        out_shape=[S((T, 1024), BF16), S((T, 1024), BF16), S((K_CONF, 1024), F32)] + [S((1, 1024), F32)] * 3,
        scratch_shapes=[pltpu.VMEM((H + TB, 1024), F32), pltpu.VMEM((TB + H, 1024), F32)],
        compiler_params=_params(("arbitrary",)))(proj, proj, proj, proj, du1, du1, w, ba, bb)


def _mesh_pos():
    x, y, c = lax.axis_index("x"), lax.axis_index("y"), lax.axis_index("c")
    return x, y, c, 4 * x + 2 * y + c


def _peer(x, y, c, k):
    return (x ^ ((k >> 2) & 1), y ^ ((k >> 1) & 1), c ^ (k & 1))


def _all_gather(bufs):
    nb = len(bufs)

    def body(*refs):
        ins, outs = refs[:nb], refs[nb:2 * nb]
        send, recv, loc = refs[2 * nb:]
        x, y, c, me = _mesh_pos()
        locs = [pltpu.make_async_copy(ins[b], outs[b].at[me], loc.at[b]) for b in range(nb)]
        for cp in locs:
            cp.start()
        copies = []
        for k in range(1, N_DEV):
            for b in range(nb):
                s = (k - 1) * nb + b
                copies.append(pltpu.make_async_remote_copy(
                    src_ref=ins[b], dst_ref=outs[b].at[me], send_sem=send.at[s], recv_sem=recv.at[s],
                    device_id=_peer(x, y, c, k), device_id_type=pl.DeviceIdType.MESH))
        for cp in copies:
            cp.start()
        for cp in copies:
            cp.wait()
        for cp in locs:
            cp.wait()

    any_spec = pl.BlockSpec(memory_space=pl.ANY)
    n = (N_DEV - 1) * nb
    return pl.pallas_call(
        body, name="all_gather", in_specs=[any_spec] * nb, out_specs=[any_spec] * nb,
        out_shape=[S((N_DEV,) + b.shape, b.dtype) for b in bufs],
        scratch_shapes=[pltpu.SemaphoreType.DMA((n,)), pltpu.SemaphoreType.DMA((n,)), pltpu.SemaphoreType.DMA((nb,))],
    )(*bufs)


def _grad_exchange(slabs, small):
    def body(slab_ref, small_ref, rs_ref, rm_ref, send, recv, loc):
        x, y, c, me = _mesh_pos()
        locs = [pltpu.make_async_copy(slab_ref.at[me], rs_ref.at[me], loc.at[0]),
                pltpu.make_async_copy(small_ref, rm_ref.at[me], loc.at[1])]
        for cp in locs:
            cp.start()
        copies = []
        for k in range(1, N_DEV):
            px, py, pc = _peer(x, y, c, k)
            pid = 4 * px + 2 * py + pc
            for b, (src, dst) in enumerate(((slab_ref.at[pid], rs_ref.at[me]), (small_ref, rm_ref.at[me]))):
                s = (k - 1) * 2 + b
                copies.append(pltpu.make_async_remote_copy(
                    src_ref=src, dst_ref=dst, send_sem=send.at[s], recv_sem=recv.at[s],
                    device_id=(px, py, pc), device_id_type=pl.DeviceIdType.MESH))
        for cp in copies:
            cp.start()
        for cp in copies:
            cp.wait()
        for cp in locs:
            cp.wait()

    any_spec = pl.BlockSpec(memory_space=pl.ANY)
    n = (N_DEV - 1) * 2
    return pl.pallas_call(
        body, name="grad_exchange", in_specs=[any_spec] * 2, out_specs=[any_spec] * 2,
        out_shape=[S(slabs.shape, slabs.dtype), S((N_DEV,) + small.shape, small.dtype)],
        scratch_shapes=[pltpu.SemaphoreType.DMA((n,)), pltpu.SemaphoreType.DMA((n,)), pltpu.SemaphoreType.DMA((2,))],
    )(slabs, small)


def _row_block(rows):
    for tb in (1024, 528, 512, 256, 128, 64, 32, 16, 8):
        if rows % tb == 0:
            return tb
    return rows


def _sum8(parts):
    _, R, n = parts.shape
    tb = _row_block(R)

    def body(p_ref, o_ref):
        acc = p_ref[0]
        for d in range(1, N_DEV):
            acc = acc + p_ref[d]
        o_ref[...] = acc

    return pl.pallas_call(
        body, name="sum8", grid=(R // tb,),
        in_specs=[pl.BlockSpec((N_DEV, tb, n), lambda i: (0, i, 0))], out_specs=_row(tb, n),
        out_shape=S((R, n), F32), compiler_params=_params(("parallel",)))(parts)


def _adamw(g, w, m, v):
    R, n = g.shape
    tb = _row_block(R)

    def body(g_ref, w_ref, m_ref, v_ref, d_ref, mo_ref, vo_ref):
        g = g_ref[...]
        m = ADAM_B1 * m_ref[...] + (1.0 - ADAM_B1) * g
        v = ADAM_B2 * v_ref[...] + (1.0 - ADAM_B2) * (g * g)
        m_hat = m / (1.0 - ADAM_B1 ** ADAM_STEP)
        v_hat = v / (1.0 - ADAM_B2 ** ADAM_STEP)
        d_ref[...] = -ADAM_LR * (m_hat / (jnp.sqrt(v_hat) + ADAM_EPS) + ADAM_WD * w_ref[...])
        mo_ref[...] = m
        vo_ref[...] = v

    return pl.pallas_call(
        body, name="adamw", grid=(R // tb,), in_specs=[_row(tb, n)] * 4, out_specs=[_row(tb, n)] * 3,
        out_shape=[S((R, n), F32)] * 3, compiler_params=_params(("parallel",)))(g, w, m, v)


def _pack(parts, dtype):
    flat = []
    for p in parts:
        f = p.reshape(-1).astype(dtype)
        flat.append(jnp.pad(f, (0, (-f.shape[0]) % LANES)))
    f = jnp.concatenate(flat)
    return jnp.pad(f, (0, (-f.shape[0]) % (16 * LANES))).reshape(-1, LANES)


def _unpack(buf, shapes):
    f = buf.reshape(-1)
    out, off = [], 0
    for s in shapes:
        n = int(np.prod(s))
        out.append(f[off:off + n].reshape(s))
        off += n + (-n) % LANES
    return out


def _local_step(x, p, tgt, W):
    r1 = lambda v: v.reshape(1, -1).astype(F32)
    pad_l = lambda v: jnp.pad(r1(v), ((0, 0), (0, LANES - v.size)))
    w_in = W["w_in"]
    w_main = jnp.concatenate([w_in[:, 2576:4624], w_in[:, 0:2048], w_in[:, 4624:5648], w_in[:, 2048:2560]], axis=1)
    w_dt = jnp.pad(w_in[:, 2560:2576], ((0, 0), (0, LANES - N_HEADS)))
    scw, scb = W["ssm_conv_w"], r1(W["ssm_conv_b"])
    wx, wb, bx, bb = scw[:, :1024], scw[:, 1024:], scb[:, :1024], scb[:, 1024:]
    dt_bias, alog = pad_l(W["dt_bias"]), pad_l(W["a_log"])
    dskip_row = jnp.repeat(W["d_skip"].reshape(-1), HEAD).reshape(1, -1)
    norm_g = r1(W["ssm_norm_g"])
    bglu = r1(W["b_glu"])
    ba, bbg = bglu[:, :1024], bglu[:, 1024:]
    ccw, ccb, clg, clb = W["conf_conv_w"], r1(W["conf_conv_b"]), r1(W["conf_ln_g"]), r1(W["conf_ln_b"])

    h0, h0b = _ln_emb_fwd(x, r1(W["ln_emb_g"]), r1(W["ln_emb_b"]))
    proj = _mm(h0b, w_main, "nn", "in_proj", tk=1024)
    dt_raw = _mm(h0b, w_dt, "nn", "in_proj_dt", tk=1024)
    xs_c, bc_c, dt = _ssd_pre_fwd(proj, dt_raw, wx, wb, bx, bb, dt_bias)
    y_ssm, ypre, hprev = _ssd_fwd(xs_c, bc_c, dt, proj, alog, dskip_row, norm_g)
    u1, y_conf = _conf_fwd(proj, ccw, ccb, clg, clb, ba, bbg)
    mix = jnp.concatenate([y_ssm, y_conf], axis=1)
    out = _mm(mix, W["w_out"], "nn", "out_proj", tk=1024)
    h1, h1b = _post1_fwd(h0, out, r1(W["b_out"]), r1(W["ln1_g"]), r1(W["ln1_b"]))
    gpre = _mm(h1b, W["w_ple_gate"], "nn", "ple_gate", tk=1024)
    pb = p.astype(BF16)
    ple = _mm(pb, W["w_ple_proj"], "nn", "ple_proj")
    dh1a, dgp, dple, loss, dln2g, dln2b = _post2(h1, gpre, ple, tgt, r1(W["ln2_g"]), r1(W["ln2_b"]))

    g = {"ln2_g": dln2g, "ln2_b": dln2b}
    g["w_ple_proj"] = _mm(pb, dple, "tn", "d_ple_proj")
    g["w_ple_gate"] = _mm(h1b, dgp, "tn", "d_ple_gate")
    dh1b = _mm(dgp, W["w_ple_gate"], "nt", "d_h1", tk=1024)
    dout, dh0a, g["ln1_g"], g["ln1_b"], g["b_out"] = _post1_bwd(dh1a, dh1b, h0, out, r1(W["b_out"]), r1(W["ln1_g"]))
    g["w_out"] = _mm(mix, dout, "tn", "d_w_out")
    dmix = _mm(dout, W["w_out"], "nt", "d_mix", tk=1024)
    du1, dcg, g["conf_ln_g"], g["conf_ln_b"] = _conf_bwd1(dmix, u1, proj, clg, clb)
    dga, dgb, g["conf_conv_w"], g["conf_conv_b"], dba, dbb = _conf_bwd2(proj, du1, ccw, ba, bbg)
    g["b_glu"] = jnp.concatenate([dba, dbb], axis=1)
    dxs_c, dbc_c, ddt, dz, g["ssm_norm_g"], ddsk, dalog = _ssd_bwd(
        xs_c, bc_c, dt, proj, ypre, hprev, dmix, alog, dskip_row, norm_g)
    g["d_skip"], g["a_log"] = ddsk[:, :N_HEADS], dalog[:, :N_HEADS]
    dxs, dbc, ddtr, dwx, dwb, dbx, dbb2, ddtb = _ssd_pre_bwd(proj, dxs_c, dbc_c, ddt, dt_raw, wx, wb, bx, bb, dt_bias)
    g["ssm_conv_w"] = jnp.concatenate([dwx, dwb], axis=1)
    g["ssm_conv_b"] = jnp.concatenate([dbx, dbb2], axis=1)
    g["dt_bias"] = ddtb[:, :N_HEADS]
    dproj = jnp.concatenate([dga, dgb, dxs, dz, dcg, dbc], axis=1)
    dw_main = _mm(h0b, dproj, "tn", "d_w_in")
    dw_dt = _mm(h0b, ddtr, "tn", "d_w_dt")
    g["w_in"] = jnp.concatenate([dw_main[:, 2048:4096], dw_main[:, 5120:5632], dw_dt[:, :N_HEADS],
                                 dw_main[:, 0:2048], dw_main[:, 4096:5120]], axis=1)
    dh0 = _mm(dproj, w_main, "nt", "d_h0", add=dh0a, tk=512)
    dh0 = _mm(ddtr, w_dt, "nt", "d_h0_dt", add=dh0)
    grad_x, g["ln_emb_g"], g["ln_emb_b"] = _ln_emb_bwd(x, dh0, r1(W["ln_emb_g"]))
    return loss[0, 0], grad_x, g


WEIGHTS = ['ln_emb_g', 'ln_emb_b', 'w_in', 'ssm_conv_w', 'ssm_conv_b', 'dt_bias', 'a_log', 'd_skip', 'ssm_norm_g',
           'b_glu', 'conf_conv_w', 'conf_conv_b', 'conf_ln_g', 'conf_ln_b', 'w_out', 'b_out', 'ln1_g', 'ln1_b',
           'w_ple_gate', 'w_ple_proj', 'ln2_g', 'ln2_b']
BIG = ['w_in', 'w_out', 'w_ple_gate', 'w_ple_proj']
CONV = ['ssm_conv_w', 'conf_conv_w']
SMALL = [n for n in WEIGHTS if n not in BIG]


def kernel(x, p, ln_emb_g, ln_emb_b, w_in, ssm_conv_w, ssm_conv_b, dt_bias, a_log, d_skip, ssm_norm_g, b_glu, conf_conv_w, conf_conv_b, conf_ln_g, conf_ln_b, w_out, b_out, ln1_g, ln1_b, w_ple_gate, w_ple_proj, ln2_g, ln2_b, loss_target, m_ln_emb_g, m_ln_emb_b, m_w_in, m_ssm_conv_w, m_ssm_conv_b, m_dt_bias, m_a_log, m_d_skip, m_ssm_norm_g, m_b_glu, m_conf_conv_w, m_conf_conv_b, m_conf_ln_g, m_conf_ln_b, m_w_out, m_b_out, m_ln1_g, m_ln1_b, m_w_ple_gate, m_w_ple_proj, m_ln2_g, m_ln2_b, v_ln_emb_g, v_ln_emb_b, v_w_in, v_ssm_conv_w, v_ssm_conv_b, v_dt_bias, v_a_log, v_d_skip, v_ssm_norm_g, v_b_glu, v_conf_conv_w, v_conf_conv_b, v_conf_ln_g, v_conf_ln_b, v_w_out, v_b_out, v_ln1_g, v_ln1_b, v_w_ple_gate, v_w_ple_proj, v_ln2_g, v_ln2_b):
    loc = dict(locals())
    w = {n: loc[n] for n in WEIGHTS}
    m = {n: loc["m_" + n] for n in WEIGHTS}
    v = {n: loc["v_" + n] for n in WEIGHTS}
    me = 4 * lax.axis_index("x") + 2 * lax.axis_index("y") + lax.axis_index("c")

    big_shapes = [w[n].shape for n in BIG]
    conv_shapes = [w[n].shape for n in CONV]
    gb, gc = _all_gather([_pack([w[n] for n in BIG], BF16), _pack([w[n] for n in CONV], F32)])
    bigs = [_unpack(gb[d], big_shapes) for d in range(N_DEV)]
    convs = [_unpack(gc[d], conv_shapes) for d in range(N_DEV)]
    W = {n: w[n][0] if w[n].ndim > 1 else w[n] for n in SMALL if n not in CONV}
    W["w_in"] = jnp.concatenate([bigs[d][0][0] for d in range(N_DEV)], axis=1)
    W["w_out"] = jnp.concatenate([bigs[d][1][0] for d in range(N_DEV)], axis=0)
    W["w_ple_gate"] = jnp.concatenate([bigs[d][2][0] for d in range(N_DEV)], axis=0)
    W["w_ple_proj"] = jnp.concatenate([bigs[d][3][0] for d in range(N_DEV)], axis=1)
    W["ssm_conv_w"] = jnp.concatenate([convs[d][0][0] for d in range(N_DEV)], axis=1)
    W["conf_conv_w"] = jnp.concatenate([convs[d][1][0] for d in range(N_DEV)], axis=1)

    loss, grad_x, g = _local_step(x[0], p[0, 0], loss_target[0], W)
    loss = lax.psum(loss, ("x", "y", "c"))

    def shard(n, gw, d):
        if n in ("w_in", "w_ple_proj"):
            c = gw.shape[1] // N_DEV
            return gw[:, d * c:(d + 1) * c]
        r = gw.shape[0] // N_DEV
        return gw[d * r:(d + 1) * r]
    slabs = jnp.stack([_pack([shard(n, g[n], d) for n in BIG], F32) for d in range(N_DEV)])
    small_shapes = [g[n].shape for n in SMALL]
    recv_slabs, recv_small = _grad_exchange(slabs, _pack([g[n] for n in SMALL], F32))
    g_big = _unpack(_sum8(recv_slabs), big_shapes)
    g_small = dict(zip(SMALL, _unpack(_sum8(recv_small), small_shapes)))
    grads = dict(zip(BIG, g_big))
    for n in SMALL:
        gs = g_small[n]
        if n in CONV:
            c = w[n].shape[-1]
            gs = lax.dynamic_slice_in_dim(gs, me * c, c, axis=1)
        grads[n] = gs.reshape(w[n].shape)

    d_big, m_big, v_big = _adamw(_pack([grads[n] for n in BIG], F32), _pack([w[n] for n in BIG], F32),
                                 _pack([m[n] for n in BIG], F32), _pack([v[n] for n in BIG], F32))
    d_sm, m_sm, v_sm = _adamw(_pack([grads[n] for n in SMALL], F32), _pack([w[n] for n in SMALL], F32),
                              _pack([m[n] for n in SMALL], F32), _pack([v[n] for n in SMALL], F32))
    sm_shapes = [w[n].shape for n in SMALL]
    delta, new_m, new_v = {}, {}, {}
    for dst, b, s in ((delta, d_big, d_sm), (new_m, m_big, m_sm), (new_v, v_big, v_sm)):
        dst.update(zip(BIG, _unpack(b, big_shapes)))
        dst.update(zip(SMALL, _unpack(s, sm_shapes)))
    return (loss, grad_x[None], *[grads[n] for n in WEIGHTS], *[delta[n] for n in WEIGHTS],
            *[new_m[n] for n in WEIGHTS], *[new_v[n] for n in WEIGHTS])
```

```python
import functools

import numpy as np
import jax
import jax.numpy as jnp
from jax import lax
from jax.experimental import pallas as pl
from jax.experimental.pallas import tpu as pltpu

F32, BF16 = jnp.float32, jnp.bfloat16
S = jax.ShapeDtypeStruct

N_DEV = 8
D = 1024
D_PLE = 256
D_SSM = 1024
D_CONF = 1024
N_HEADS = 16
HEAD = 64
N_STATE = 128
CHUNK = 128
K_SSM = 4
K_CONF = 31
D_IN = 5648
COLS_PER_DEV = D_IN // N_DEV
LN_EPS = 1e-5
RMS_EPS = 1e-5
ALPHA = 2.0 ** 0.25
LANES = 128
TB = 256
RG = 32
HALO_SSM = 8
HALO_CONF = 32
VMEM_LIMIT = 56 * 1024 * 1024

ADAM_LR, ADAM_B1, ADAM_B2, ADAM_EPS, ADAM_WD, ADAM_STEP = 0.001, 0.9, 0.999, 1e-08, 0.01, 10

C_GLUA, C_GLUB, C_XS, C_Z, C_CG = 0, 1, 2, 3, 4
C_BC = 10
N_MAIN = 5632


def _params(sem, vmem=VMEM_LIMIT):
    return pltpu.CompilerParams(dimension_semantics=sem, vmem_limit_bytes=vmem)


def _row(tb, n, col=0):
    return pl.BlockSpec((tb, n), lambda i: (i, col))


def _full(shape):
    return pl.BlockSpec(shape, lambda i: (0,) * len(shape))


_ANY = pl.BlockSpec(memory_space=pl.ANY)


def _prev(tb, halo, n, col=0):
    r = tb // halo
    return pl.BlockSpec((halo, n), lambda i: (jnp.maximum(i * r - 1, 0), col))


def _next(tb, halo, n, nt, col=0):
    r = tb // halo
    return pl.BlockSpec((halo, n), lambda i: (jnp.minimum((i + 1) * r, nt * r - 1), col))


def _row_loop(tb, fn):
    def it(r, c):
        fn(pl.ds(pl.multiple_of(r * RG, RG), RG))
        return c
    lax.fori_loop(0, tb // RG, it, 0)


def _col_loop(n, fn):
    def it(j, c):
        fn(pl.ds(pl.multiple_of(j * LANES, LANES), LANES))
        return c
    lax.fori_loop(0, n // LANES, it, 0)


def _sigmoid(x):
    return 1.0 / (1.0 + jnp.exp(-x))


def _dsilu(x, s):
    return s * (1.0 + x * (1.0 - s))


def _ln_stats(v):
    mu = jnp.mean(v, axis=-1, keepdims=True)
    c = v - mu
    r = lax.rsqrt(jnp.mean(c * c, axis=-1, keepdims=True) + LN_EPS)
    return c * r, r


def _ln_bwd(dy, xhat, r, g):
    dxh = dy * g
    dv = r * (dxh - jnp.mean(dxh, axis=-1, keepdims=True) - xhat * jnp.mean(dxh * xhat, axis=-1, keepdims=True))
    return dv, jnp.sum(dy * xhat, axis=0, keepdims=True), jnp.sum(dy, axis=0, keepdims=True)


def _dot(a, b, dims=((1,), (0,))):
    return lax.dot_general(a.astype(BF16), b.astype(BF16), (dims, ((), ())), preferred_element_type=F32)


_NT = ((1,), (1,))
_TN = ((0,), (0,))


def _dot_hi(a, b, dims=((1,), (0,))):
    return lax.dot_general(a, b, (dims, ((), ())), precision=lax.Precision.HIGHEST, preferred_element_type=F32)


def _mm(a, b, mode, name, out_dtype=F32, add=None, tm=1024, tn=512, tk=1024):
    if mode == "nn":
        (M, K), N = a.shape, b.shape[1]
    elif mode == "tn":
        (K, M), N = a.shape, b.shape[1]
    else:
        (M, K), N = a.shape, b.shape[0]
    tm, tn, tk = min(tm, M), min(tn, N), min(tk, K)
    assert M % tm == 0 and N % tn == 0 and K % tk == 0, (name, M, N, K)
    nk = K // tk
    dims = {"nn": ((1,), (0,)), "tn": _TN, "nt": _NT}[mode]

    def body(*refs):
        if add is None:
            a_ref, b_ref, o_ref, acc = refs
        else:
            a_ref, b_ref, add_ref, o_ref, acc = refs
        k = pl.program_id(2)

        @pl.when(k == 0)
        def _():
            acc[...] = jnp.zeros_like(acc)

        acc[...] += _dot(a_ref[...], b_ref[...], dims)

        @pl.when(k == nk - 1)
        def _():
            r = acc[...]
            if add is not None:
                r = r + add_ref[...]
            o_ref[...] = r.astype(out_dtype)

    a_spec = pl.BlockSpec((tk, tm), lambda i, j, k: (k, i)) if mode == "tn" else pl.BlockSpec((tm, tk), lambda i, j, k: (i, k))
    b_spec = pl.BlockSpec((tn, tk), lambda i, j, k: (j, k)) if mode == "nt" else pl.BlockSpec((tk, tn), lambda i, j, k: (k, j))
    o_spec = pl.BlockSpec((tm, tn), lambda i, j, k: (i, j))
    ins, specs = [a, b], [a_spec, b_spec]
    if add is not None:
        ins.append(add)
        specs.append(o_spec)
    return pl.pallas_call(
        body, name=name, grid=(M // tm, N // tn, nk), in_specs=specs, out_specs=o_spec,
        out_shape=S((M, N), out_dtype), scratch_shapes=[pltpu.VMEM((tm, tn), F32)],
        compiler_params=_params(("parallel", "parallel", "arbitrary")))(*ins)


def _ln_emb_fwd(x, g, b):
    T = x.shape[0]

    def body(x_ref, g_ref, b_ref, h_ref, hb_ref):
        def rows(rs):
            xh, _ = _ln_stats(x_ref[rs, :])
            h = xh * g_ref[...] + b_ref[...]
            h_ref[rs, :] = h
            hb_ref[rs, :] = h.astype(BF16)
        _row_loop(TB, rows)

    return pl.pallas_call(
        body, name="ln_emb_fwd", grid=(T // TB,),
        in_specs=[_row(TB, D), _full((1, D)), _full((1, D))], out_specs=[_row(TB, D), _row(TB, D)],
        out_shape=[S((T, D), F32), S((T, D), BF16)], compiler_params=_params(("parallel",)))(x, g, b)


def _post1_fwd(h0, out, b_out, g, b):
    T = h0.shape[0]

    def body(h0_ref, out_ref, bo_ref, g_ref, b_ref, h_ref, hb_ref):
        def rows(rs):
            xh, _ = _ln_stats(ALPHA * h0_ref[rs, :] + out_ref[rs, :] + bo_ref[...])
            h = xh * g_ref[...] + b_ref[...]
            h_ref[rs, :] = h
            hb_ref[rs, :] = h.astype(BF16)
        _row_loop(TB, rows)

    return pl.pallas_call(
        body, name="post1_fwd", grid=(T // TB,),
        in_specs=[_row(TB, D), _row(TB, D)] + [_full((1, D))] * 3, out_specs=[_row(TB, D), _row(TB, D)],
        out_shape=[S((T, D), F32), S((T, D), BF16)], compiler_params=_params(("parallel",)))(h0, out, b_out, g, b)


def _post2(h1, gpre, ple, tgt, g, b):
    T = h1.shape[0]

    def body(h1_ref, gp_ref, ple_ref, tgt_ref, g_ref, b_ref, dh1_ref, dgp_ref, dple_ref, loss_ref, dg_ref, db_ref):
        @pl.when(pl.program_id(0) == 0)
        def _():
            loss_ref[...] = jnp.zeros_like(loss_ref)
            dg_ref[...] = jnp.zeros_like(dg_ref)
            db_ref[...] = jnp.zeros_like(db_ref)

        def rows(rs):
            gate = _sigmoid(gp_ref[rs, :])
            ple = ple_ref[rs, :]
            xh, r = _ln_stats(ALPHA * h1_ref[rs, :] + gate * ple)
            err = xh * g_ref[...] + b_ref[...] - tgt_ref[rs, :]
            loss_ref[...] += 0.5 * jnp.sum(jnp.mean(err * err, axis=-1, keepdims=True), axis=0, keepdims=True)
            dv, dg, db = _ln_bwd(err * (1.0 / D), xh, r, g_ref[...])
            dg_ref[...] += dg
            db_ref[...] += db
            dh1_ref[rs, :] = ALPHA * dv
            dgp_ref[rs, :] = (dv * ple * gate * (1.0 - gate)).astype(BF16)
            dple_ref[rs, :] = (dv * gate).astype(BF16)
        _row_loop(TB, rows)

    return pl.pallas_call(
        body, name="post2", grid=(T // TB,),
        in_specs=[_row(TB, D)] * 4 + [_full((1, D))] * 2,
        out_specs=[_row(TB, D)] * 3 + [_full((8, LANES)), _full((1, D)), _full((1, D))],
        out_shape=[S((T, D), F32), S((T, D), BF16), S((T, D), BF16), S((8, LANES), F32), S((1, D), F32), S((1, D), F32)],
        compiler_params=_params(("arbitrary",)))(h1, gpre, ple, tgt, g, b)


def _post1_bwd(dh1a, dh1b, h0, out, b_out, g):
    T = h0.shape[0]

    def body(da_ref, db2_ref, h0_ref, out_ref, bo_ref, g_ref, dout_ref, dh0_ref, dg_ref, db_ref, dbo_ref):
        @pl.when(pl.program_id(0) == 0)
        def _():
            dg_ref[...] = jnp.zeros_like(dg_ref)
            db_ref[...] = jnp.zeros_like(db_ref)
            dbo_ref[...] = jnp.zeros_like(dbo_ref)

        def rows(rs):
            xh, r = _ln_stats(ALPHA * h0_ref[rs, :] + out_ref[rs, :] + bo_ref[...])
            dv, dg, db = _ln_bwd(da_ref[rs, :] + db2_ref[rs, :], xh, r, g_ref[...])
            dg_ref[...] += dg
            db_ref[...] += db
            dbo_ref[...] += jnp.sum(dv, axis=0, keepdims=True)
            dout_ref[rs, :] = dv.astype(BF16)
            dh0_ref[rs, :] = ALPHA * dv
        _row_loop(TB, rows)

    return pl.pallas_call(
        body, name="post1_bwd", grid=(T // TB,),
        in_specs=[_row(TB, D)] * 4 + [_full((1, D))] * 2,
        out_specs=[_row(TB, D)] * 2 + [_full((1, D))] * 3,
        out_shape=[S((T, D), BF16), S((T, D), F32)] + [S((1, D), F32)] * 3,
        compiler_params=_params(("arbitrary",)))(dh1a, dh1b, h0, out, b_out, g)


def _ln_emb_bwd(x, dh0, g):
    T = x.shape[0]

    def body(x_ref, dh_ref, g_ref, dx_ref, dg_ref, db_ref):
        @pl.when(pl.program_id(0) == 0)
        def _():
            dg_ref[...] = jnp.zeros_like(dg_ref)
            db_ref[...] = jnp.zeros_like(db_ref)

        def rows(rs):
            xh, r = _ln_stats(x_ref[rs, :])
            dv, dg, db = _ln_bwd(dh_ref[rs, :], xh, r, g_ref[...])
            dg_ref[...] += dg
            db_ref[...] += db
            dx_ref[rs, :] = dv
        _row_loop(TB, rows)

    return pl.pallas_call(
        body, name="ln_emb_bwd", grid=(T // TB,),
        in_specs=[_row(TB, D), _row(TB, D), _full((1, D))], out_specs=[_row(TB, D), _full((1, D)), _full((1, D))],
        out_shape=[S((T, D), F32), S((1, D), F32), S((1, D), F32)],
        compiler_params=_params(("arbitrary",)))(x, dh0, g)


def _softplus(x):
    return jnp.maximum(x, 0.0) + jnp.log1p(jnp.exp(-jnp.abs(x)))


def _ssd_pre_fwd(proj, dt_raw, wx, wb, bx, bb, dt_bias):
    T = proj.shape[0]
    H = HALO_SSM

    def body(xs_ref, xsp_ref, bc_ref, bcp_ref, dtr_ref, wx_ref, wb_ref, bx_ref, bb_ref, dtb_ref,
             xso_ref, bco_ref, dto_ref, extx, extb):
        first = pl.program_id(0) == 0

        def conv(t_ref, p_ref, w_ref, b_ref, o_ref, ext, n):
            def blk(cols):
                ext[0:H, cols] = jnp.where(first, 0.0, p_ref[:, cols])
                ext[H:, cols] = t_ref[:, cols]
                for r0 in range(0, TB, 64):
                    acc = jnp.broadcast_to(b_ref[:, cols], (64, LANES))
                    for k in range(K_SSM):
                        acc = acc + w_ref[k:k + 1, cols] * ext[pl.ds(r0 + H - (K_SSM - 1) + k, 64), cols]
                    o_ref[pl.ds(r0, 64), cols] = acc * _sigmoid(acc)
            _col_loop(n, blk)

        conv(xs_ref, xsp_ref, wx_ref, bx_ref, xso_ref, extx, D_SSM)
        conv(bc_ref, bcp_ref, wb_ref, bb_ref, bco_ref, extb, 512)
        dto_ref[...] = _softplus(dtr_ref[...] + dtb_ref[...])

    return pl.pallas_call(
        body, name="ssd_pre_fwd", grid=(T // TB,),
        in_specs=[_row(TB, 1024, C_XS), _prev(TB, H, 1024, C_XS), _row(TB, 512, C_BC), _prev(TB, H, 512, C_BC),
                  _row(TB, LANES), _full((K_SSM, 1024)), _full((K_SSM, 512)), _full((1, 1024)), _full((1, 512)),
                  _full((1, LANES))],
        out_specs=[_row(TB, 1024), _row(TB, 512), _row(TB, LANES)],
        out_shape=[S((T, 1024), F32), S((T, 512), F32), S((T, LANES), F32)],
        scratch_shapes=[pltpu.VMEM((H + TB, 1024), F32), pltpu.VMEM((H + TB, 512), F32)],
        compiler_params=_params(("parallel",)))(proj, proj, proj, proj, dt_raw, wx, wb, bx, bb, dt_bias)


def _ssd_conv_bwd(dproj, proj, d_c, w, b, n, col, name):
    T = proj.shape[0]
    nt = T // TB
    H = HALO_SSM
    R = TB + H

    def body(dproj_ref, t_ref, p_ref, n_ref, d_ref, dn_ref, w_ref, b_ref, o_ref, dw_ref, dbias_ref, ext, dp):
        i = pl.program_id(0)
        first, last = i == 0, i == nt - 1

        @pl.when(first)
        def _():
            dw_ref[...] = jnp.zeros_like(dw_ref)
            dbias_ref[...] = jnp.zeros_like(dbias_ref)

        def blk(cols):
            ext[0:H, cols] = jnp.where(first, 0.0, p_ref[:, cols])
            ext[H:H + TB, cols] = t_ref[:, cols]
            ext[H + TB:, cols] = n_ref[:, cols]
            pre = jnp.broadcast_to(b_ref[:, cols], (R, LANES))
            for k in range(K_SSM):
                pre = pre + w_ref[k:k + 1, cols] * ext[pl.ds(H - (K_SSM - 1) + k, R), cols]
            s = _sigmoid(pre)
            ds = _dsilu(pre, s)
            dp[0:TB, cols] = d_ref[:, cols] * ds[0:TB]
            dp[TB:, cols] = jnp.where(last, 0.0, dn_ref[:, cols] * ds[TB:])
            dpt = dp[0:TB, cols]
            dbias_ref[:, cols] += jnp.sum(dpt, axis=0, keepdims=True)
            acc = jnp.zeros((TB, LANES), F32)
            for k in range(K_SSM):
                dw_ref[k:k + 1, cols] += jnp.sum(dpt * ext[pl.ds(H - (K_SSM - 1) + k, TB), cols], axis=0, keepdims=True)
                acc = acc + w_ref[k:k + 1, cols] * dp[pl.ds(K_SSM - 1 - k, TB), cols]
            o_ref[:, cols] = acc.astype(BF16)
        _col_loop(n, blk)

    return pl.pallas_call(
        body, name=name, grid=(nt,),
        in_specs=[_ANY, _row(TB, n, col), _prev(TB, H, n, col), _next(TB, H, n, nt, col),
                  _row(TB, n), _next(TB, H, n, nt), _full((K_SSM, n)), _full((1, n))],
        out_specs=[_row(TB, n, col), _full((K_SSM, n)), _full((1, n))],
        out_shape=[S(dproj.shape, BF16), S((K_SSM, n), F32), S((1, n), F32)],
        input_output_aliases={0: 0},
        scratch_shapes=[pltpu.VMEM((H + TB + H, n), F32), pltpu.VMEM((R, n), F32)],
        compiler_params=_params(("arbitrary",)))(dproj, proj, proj, proj, d_c, d_c, w, b)


def _dt_bwd(ddt, dt_raw, dt_bias):
    T = ddt.shape[0]

    def body(ddt_ref, dtr_ref, dtb_ref, o_ref, db_ref):
        @pl.when(pl.program_id(0) == 0)
        def _():
            db_ref[...] = jnp.zeros_like(db_ref)

        g = ddt_ref[...] * _sigmoid(dtr_ref[...] + dtb_ref[...])
        o_ref[...] = g.astype(BF16)
        db_ref[...] += jnp.sum(g, axis=0, keepdims=True)

    return pl.pallas_call(
        body, name="dt_bwd", grid=(T // TB,),
        in_specs=[_row(TB, LANES), _row(TB, LANES), _full((1, LANES))], out_specs=[_row(TB, LANES), _full((1, LANES))],
        out_shape=[S((T, LANES), BF16), S((1, LANES), F32)],
        compiler_params=_params(("arbitrary",)))(ddt, dt_raw, dt_bias)


def _ssd_consts():
    ex = np.zeros((LANES, D_SSM), np.float32)
    for h in range(N_HEADS):
        ex[h, h * HEAD:(h + 1) * HEAD] = 1.0
    tri = np.tril(np.ones((CHUNK, CHUNK), np.float32))
    return jnp.asarray(ex), jnp.asarray(ex.T.copy()), jnp.asarray(tri), jnp.asarray(tri.T.copy())


def _ssd_common(xs, dt, alog_ref, ex_ref, tri_ref):
    lane = lax.broadcasted_iota(jnp.int32, (1, LANES), 1)
    a = jnp.where(lane < N_HEADS, -jnp.exp(alog_ref[...]), 0.0)
    A = _dot_hi(tri_ref[...], dt * a)
    ex = ex_ref[...]
    Aex = _dot_hi(A, ex)
    dtex = _dot_hi(dt, ex)
    expA = jnp.exp(Aex)
    dec = jnp.exp(Aex[CHUNK - 1:CHUNK, :] - Aex)
    cd = _dot_hi(ex, jnp.broadcast_to(jnp.exp(A.T[:, CHUNK - 1:CHUNK]), (LANES, LANES)), _TN)
    return a, A, dtex, expA, dec, cd


def _decay_mask():
    sub = lax.broadcasted_iota(jnp.int32, (CHUNK, CHUNK), 0)
    lane = lax.broadcasted_iota(jnp.int32, (CHUNK, CHUNK), 1)
    return sub, lane, sub >= lane


def _ssd_fwd(xs_c, bc_c, dt, proj, alog, dskip_row, norm_g):
    T = xs_c.shape[0]
    nc = T // CHUNK
    ex, _, tri, _ = _ssd_consts()

    def body(xs_ref, bc_ref, dt_ref, z_ref, alog_ref, dsk_ref, ng_ref, ex_ref, tri_ref,
             ys_ref, ypre_ref, hprev_ref, Hs, ybuf):
        @pl.when(pl.program_id(0) == 0)
        def _():
            Hs[...] = jnp.zeros_like(Hs)

        hprev_ref[0] = Hs[...]
        xs, dt = xs_ref[...], dt_ref[...]
        a, A, dtex, expA, dec, cd = _ssd_common(xs, dt, alog_ref, ex_ref, tri_ref)
        AT = A.T
        xdt = xs * dtex
        xdec = xdt * dec
        _, _, causal = _decay_mask()
        for g in range(2):
            gs = slice(g * 512, (g + 1) * 512)
            B = bc_ref[:, g * N_STATE:(g + 1) * N_STATE]
            C = bc_ref[:, 256 + g * N_STATE:256 + (g + 1) * N_STATE]
            cb = _dot(C, B, _NT)
            Hg = Hs[gs, :]
            yoff = _dot(C, Hg, _NT) * expA[:, gs]
            for j in range(8):
                h = g * 8 + j
                hs = slice(h * HEAD, (h + 1) * HEAD)
                L = jnp.exp(jnp.where(causal, A[:, h:h + 1] - AT[h:h + 1, :], -1e30))
                ybuf[:, hs] = _dot(cb * L, xdt[:, hs]) + yoff[:, j * HEAD:(j + 1) * HEAD]
            Hs[gs, :] = cd[gs, :] * Hg + _dot(xdec[:, gs], B, _TN)
        ypre = ybuf[...] + dsk_ref[...] * xs
        ypre_ref[...] = ypre
        z = z_ref[...]
        yz = ypre * (z * _sigmoid(z))
        for g in range(2):
            gs = slice(g * 512, (g + 1) * 512)
            v = yz[:, gs]
            r = lax.rsqrt(jnp.mean(v * v, axis=-1, keepdims=True) + RMS_EPS)
            ys_ref[:, gs] = (v * r * ng_ref[:, gs]).astype(BF16)

    return pl.pallas_call(
        body, name="ssd_fwd", grid=(nc,),
        in_specs=[_row(CHUNK, 1024), _row(CHUNK, 512), _row(CHUNK, LANES), _row(CHUNK, 1024, C_Z),
                  _full((1, LANES)), _full((1, 1024)), _full((1, 1024)), _full((LANES, 1024)), _full((CHUNK, CHUNK))],
        out_specs=[_row(CHUNK, 1024), _row(CHUNK, 1024), pl.BlockSpec((1, 1024, N_STATE), lambda c: (c, 0, 0))],
        out_shape=[S((T, 2048), BF16), S((T, 1024), F32), S((nc, 1024, N_STATE), F32)],
        scratch_shapes=[pltpu.VMEM((1024, N_STATE), F32), pltpu.VMEM((CHUNK, 1024), F32)],
        compiler_params=_params(("arbitrary",)))(xs_c, bc_c, dt, proj, alog, dskip_row, norm_g, ex, tri)


def _ssd_bwd(dproj, xs_c, bc_c, dt, proj, ypre, hprev, dmix, alog, dskip_row, norm_g):
    T = xs_c.shape[0]
    nc = T // CHUNK
    ex, ext, tri, triu = _ssd_consts()
    rev = lambda n, col=0: pl.BlockSpec((CHUNK, n), lambda c: (nc - 1 - c, col))

    def body(dproj_ref, xs_ref, bc_ref, dt_ref, z_ref, ypre_ref, hprev_ref, dys_ref, alog_ref, dsk_ref, ng_ref,
             ex_ref, ext_ref, tri_ref, triu_ref,
             dxs_ref, dbc_ref, ddt_ref, dz_ref, dng_ref, ddsk_ref, dalog_ref, dHs, dxbuf, dskacc):
        c = pl.program_id(0)

        @pl.when(c == 0)
        def _():
            dHs[...] = jnp.zeros_like(dHs)
            dng_ref[...] = jnp.zeros_like(dng_ref)
            dalog_ref[...] = jnp.zeros_like(dalog_ref)
            dskacc[...] = jnp.zeros_like(dskacc)

        xs, dt, z, ypre, dys = xs_ref[...], dt_ref[...], z_ref[...], ypre_ref[...], dys_ref[...]
        sg = _sigmoid(z)
        sz = z * sg
        yz = ypre * sz
        dyz_parts = []
        for g in range(2):
            gs = slice(g * 512, (g + 1) * 512)
            v = yz[:, gs]
            r = lax.rsqrt(jnp.mean(v * v, axis=-1, keepdims=True) + RMS_EPS)
            vn = v * r
            dng_ref[:, gs] += jnp.sum(dys[:, gs] * vn, axis=0, keepdims=True)
            dvn = dys[:, gs] * ng_ref[:, gs]
            dyz_parts.append(r * (dvn - vn * jnp.mean(dvn * vn, axis=-1, keepdims=True)))
        dyz = jnp.concatenate(dyz_parts, axis=1)
        dy = dyz * sz
        dz_ref[...] = (dyz * ypre * _dsilu(z, sg)).astype(BF16)
        dskacc[...] += jnp.sum(dy * xs, axis=0, keepdims=True)

        a, A, dtex, expA, dec, cd = _ssd_common(xs, dt, alog_ref, ex_ref, tri_ref)
        AT = A.T
        xdt = xs * dtex
        xdec = xdt * dec
        dye = dy * expA
        H = hprev_ref[0]
        dHn = dHs[...]
        sub, lane, causal = _decay_mask()
        dAc = jnp.zeros((CHUNK, LANES), F32)
        Rm = jnp.zeros((CHUNK, LANES), F32)
        yoff_parts, q_parts = [], []
        for g in range(2):
            gs = slice(g * 512, (g + 1) * 512)
            B = bc_ref[:, g * N_STATE:(g + 1) * N_STATE]
            C = bc_ref[:, 256 + g * N_STATE:256 + (g + 1) * N_STATE]
            cb = _dot(C, B, _NT)
            Hg, dHg = H[gs, :], dHn[gs, :]
            Q = _dot(B, dHg, _NT)
            yoff_parts.append(_dot(C, Hg, _NT) * expA[:, gs])
            q_parts.append(Q)
            dcb = jnp.zeros((CHUNK, CHUNK), F32)
            for j in range(8):
                h = g * 8 + j
                hs = slice(h * HEAD, (h + 1) * HEAD)
                L = jnp.exp(jnp.where(causal, A[:, h:h + 1] - AT[h:h + 1, :], -1e30))
                M = cb * L
                G = _dot(dy[:, hs], xdt[:, hs], _NT)
                dxbuf[:, hs] = _dot(M, dy[:, hs], _TN)
                dcb = dcb + G * L
                E = G * M
                dAc = jnp.where(lane == h, jnp.sum(E, axis=1, keepdims=True), dAc)
                Rm = jnp.where(sub == h, jnp.sum(E, axis=0, keepdims=True), Rm)
            dbc_ref[:, g * N_STATE:(g + 1) * N_STATE] = _dot(dcb, C, _TN) + _dot(xdec[:, gs], dHg)
            dbc_ref[:, 256 + g * N_STATE:256 + (g + 1) * N_STATE] = _dot(dcb, B) + _dot(dye[:, gs], Hg)
            dHs[gs, :] = cd[gs, :] * dHg + _dot(dye[:, gs], C, _TN)
        yoff = jnp.concatenate(yoff_parts, axis=1)
        Qd = jnp.concatenate(q_parts, axis=1) * dec
        dxdt = dxbuf[...] + Qd
        extm = ext_ref[...]
        red_s = _dot_hi(xdt * Qd, extm)
        dA = dAc - Rm.T + _dot_hi(dy * yoff, extm) - red_s
        hd = jnp.sum(_dot_hi(H * dHn, extm, _TN), axis=0, keepdims=True)
        last_add = jnp.sum(red_s, axis=0, keepdims=True) + jnp.exp(A[CHUNK - 1:CHUNK, :]) * hd
        dA = dA + jnp.where(sub == CHUNK - 1, last_add, 0.0)
        dadt = _dot_hi(triu_ref[...], dA)
        ddt_ref[...] = dadt * a + _dot_hi(dxdt * xs, extm)
        dalog_ref[...] += jnp.sum(dadt * dt, axis=0, keepdims=True) * a
        dxs_ref[...] = dxdt * dtex + dsk_ref[...] * dy

        @pl.when(c == nc - 1)
        def _():
            ddsk_ref[...] = _dot_hi(jnp.broadcast_to(dskacc[...], (8, 1024)), extm)[0:1, :]

    return pl.pallas_call(
        body, name="ssd_bwd", grid=(nc,),
        in_specs=[_ANY, rev(1024), rev(512), rev(LANES), rev(1024, C_Z), rev(1024),
                  pl.BlockSpec((1, 1024, N_STATE), lambda c: (nc - 1 - c, 0, 0)), rev(1024, 0),
                  _full((1, LANES)), _full((1, 1024)), _full((1, 1024)),
                  _full((LANES, 1024)), _full((1024, LANES)), _full((CHUNK, CHUNK)), _full((CHUNK, CHUNK))],
        out_specs=[rev(1024), rev(512), rev(LANES), rev(1024, C_Z), _full((1, 1024)), _full((1, LANES)), _full((1, LANES))],
        out_shape=[S((T, 1024), F32), S((T, 512), F32), S((T, LANES), F32), S(dproj.shape, BF16),
                   S((1, 1024), F32), S((1, LANES), F32), S((1, LANES), F32)],
        input_output_aliases={0: 3},
        scratch_shapes=[pltpu.VMEM((1024, N_STATE), F32), pltpu.VMEM((CHUNK, 1024), F32), pltpu.VMEM((1, 1024), F32)],
        compiler_params=_params(("arbitrary",)))(
            dproj, xs_c, bc_c, dt, proj, ypre, hprev, dmix, alog, dskip_row, norm_g, ex, ext, tri, triu)


def _conf_fwd(mix, proj, w, cb, lg, lb, ba, bb):
    T = proj.shape[0]
    H = HALO_CONF

    def body(mix_ref, ga_ref, gap_ref, gb_ref, gbp_ref, cg_ref, w_ref, cb_ref, lg_ref, lb_ref, ba_ref, bb_ref,
             u1_ref, yc_ref, ext):
        first = pl.program_id(0) == 0

        def blk(cols):
            up = (gap_ref[:, cols] + ba_ref[:, cols]) * _sigmoid(gbp_ref[:, cols] + bb_ref[:, cols])
            ext[0:H, cols] = jnp.where(first, 0.0, up)
            ext[H:, cols] = (ga_ref[:, cols] + ba_ref[:, cols]) * _sigmoid(gb_ref[:, cols] + bb_ref[:, cols])
            for r0 in range(0, TB, 64):
                acc = jnp.broadcast_to(cb_ref[:, cols], (64, LANES))
                for k in range(K_CONF):
                    acc = acc + w_ref[k:k + 1, cols] * ext[pl.ds(r0 + H - (K_CONF - 1) + k, 64), cols]
                u1_ref[pl.ds(r0, 64), cols] = acc
        _col_loop(D_CONF, blk)

        def rows(rs):
            xh, _ = _ln_stats(u1_ref[rs, :])
            u2 = xh * lg_ref[...] + lb_ref[...]
            cg = cg_ref[rs, :]
            yc_ref[rs, :] = (u2 * _sigmoid(u2) * cg * _sigmoid(cg)).astype(BF16)
        _row_loop(TB, rows)

    return pl.pallas_call(
        body, name="conf_fwd", grid=(T // TB,),
        in_specs=[_ANY, _row(TB, 1024, C_GLUA), _prev(TB, H, 1024, C_GLUA), _row(TB, 1024, C_GLUB),
                  _prev(TB, H, 1024, C_GLUB), _row(TB, 1024, C_CG), _full((K_CONF, 1024))] + [_full((1, 1024))] * 5,
        out_specs=[_row(TB, 1024), _row(TB, 1024, 1)],
        out_shape=[S((T, 1024), F32), S((T, 2048), BF16)],
        input_output_aliases={0: 1},
        scratch_shapes=[pltpu.VMEM((H + TB, 1024), F32)],
        compiler_params=_params(("parallel",)))(mix, proj, proj, proj, proj, proj, w, cb, lg, lb, ba, bb)


def _conf_bwd1(dmix, u1, proj, lg, lb):
    T = u1.shape[0]

    def body(dy_ref, u1_ref, cg_ref, lg_ref, lb_ref, du1_ref, dcg_ref, dg_ref, db_ref):
        @pl.when(pl.program_id(0) == 0)
        def _():
            dg_ref[...] = jnp.zeros_like(dg_ref)
            db_ref[...] = jnp.zeros_like(db_ref)

        def rows(rs):
            xh, r = _ln_stats(u1_ref[rs, :])
            u2 = xh * lg_ref[...] + lb_ref[...]
            s2 = _sigmoid(u2)
            cg = cg_ref[rs, :]
            sc = _sigmoid(cg)
            dy = dy_ref[rs, :]
            dcg_ref[rs, :] = (dy * u2 * s2 * _dsilu(cg, sc)).astype(BF16)
            dv, dg, db = _ln_bwd(dy * cg * sc * _dsilu(u2, s2), xh, r, lg_ref[...])
            dg_ref[...] += dg
            db_ref[...] += db
            du1_ref[rs, :] = dv
        _row_loop(TB, rows)

    return pl.pallas_call(
        body, name="conf_bwd1", grid=(T // TB,),
        in_specs=[_row(TB, 1024, 1), _row(TB, 1024), _row(TB, 1024, C_CG), _full((1, 1024)), _full((1, 1024))],
        out_specs=[_row(TB, 1024), _row(TB, 1024, C_CG), _full((1, 1024)), _full((1, 1024))],
        out_shape=[S((T, 1024), F32), S((T, N_MAIN), BF16), S((1, 1024), F32), S((1, 1024), F32)],
        compiler_params=_params(("arbitrary",)))(dmix, u1, proj, lg, lb)


def _conf_bwd2(dproj, proj, du1, w, ba, bb):
    T = du1.shape[0]
    nt = T // TB
    H = HALO_CONF

    def body(dproj_ref, ga_ref, gap_ref, gb_ref, gbp_ref, du_ref, dun_ref, w_ref, ba_ref, bb_ref,
             dg_ref, dw_ref, dcb_ref, dba_ref, dbb_ref, ext, dext):
        i = pl.program_id(0)
        first, last = i == 0, i == nt - 1

        @pl.when(first)
        def _():
            for r in (dw_ref, dcb_ref, dba_ref, dbb_ref):
                r[...] = jnp.zeros_like(r)

        def blk(cols):
            cols_b = pl.ds(pl.multiple_of(cols.start + D_CONF, LANES), LANES)
            up = (gap_ref[:, cols] + ba_ref[:, cols]) * _sigmoid(gbp_ref[:, cols] + bb_ref[:, cols])
            ext[0:H, cols] = jnp.where(first, 0.0, up)
            a = ga_ref[:, cols] + ba_ref[:, cols]
            sb = _sigmoid(gb_ref[:, cols] + bb_ref[:, cols])
            ext[H:, cols] = a * sb
            du = du_ref[:, cols]
            dext[0:TB, cols] = du
            dext[TB:, cols] = jnp.where(last, 0.0, dun_ref[:, cols])
            dcb_ref[:, cols] += jnp.sum(du, axis=0, keepdims=True)
            for r0 in range(0, TB, 64):
                dur = du_ref[pl.ds(r0, 64), cols]
                acc = jnp.zeros((64, LANES), F32)
                for k in range(K_CONF):
                    dw_ref[k:k + 1, cols] += jnp.sum(
                        dur * ext[pl.ds(r0 + H - (K_CONF - 1) + k, 64), cols], axis=0, keepdims=True)
                    acc = acc + w_ref[k:k + 1, cols] * dext[pl.ds(r0 + K_CONF - 1 - k, 64), cols]
                ar, sr = a[r0:r0 + 64], sb[r0:r0 + 64]
                da = acc * sr
                dbv = acc * ar * sr * (1.0 - sr)
                dg_ref[pl.ds(r0, 64), cols] = da.astype(BF16)
                dg_ref[pl.ds(r0, 64), cols_b] = dbv.astype(BF16)
                dba_ref[:, cols] += jnp.sum(da, axis=0, keepdims=True)
                dbb_ref[:, cols] += jnp.sum(dbv, axis=0, keepdims=True)
        _col_loop(D_CONF, blk)

    return pl.pallas_call(
        body, name="conf_bwd2", grid=(nt,),
        in_specs=[_ANY, _row(TB, 1024, C_GLUA), _prev(TB, H, 1024, C_GLUA), _row(TB, 1024, C_GLUB),
                  _prev(TB, H, 1024, C_GLUB), _row(TB, 1024), _next(TB, H, 1024, nt), _full((K_CONF, 1024)),
                  _full((1, 1024)), _full((1, 1024))],
        out_specs=[_row(TB, 2048), _full((K_CONF, 1024)), _full((1, 1024)), _full((1, 1024)), _full((1, 1024))],
        out_shape=[S(dproj.shape, BF16), S((K_CONF, 1024), F32)] + [S((1, 1024), F32)] * 3,
        input_output_aliases={0: 0},
        scratch_shapes=[pltpu.VMEM((H + TB, 1024), F32), pltpu.VMEM((TB + H, 1024), F32)],
        compiler_params=_params(("arbitrary",)))(dproj, proj, proj, proj, proj, du1, du1, w, ba, bb)


def _mesh_pos():
    x, y, c = lax.axis_index("x"), lax.axis_index("y"), lax.axis_index("c")
    return x, y, c, 4 * x + 2 * y + c


def _peer(x, y, c, k):
    return (x ^ ((k >> 2) & 1), y ^ ((k >> 1) & 1), c ^ (k & 1))


def _all_gather(bufs):
    nb = len(bufs)

    def body(*refs):
        ins, outs = refs[:nb], refs[nb:2 * nb]
        send, recv, loc = refs[2 * nb:]
        x, y, c, me = _mesh_pos()
        locs = [pltpu.make_async_copy(ins[b], outs[b].at[me], loc.at[b]) for b in range(nb)]
        for cp in locs:
            cp.start()
        copies = []
        for k in range(1, N_DEV):
            for b in range(nb):
                s = (k - 1) * nb + b
                copies.append(pltpu.make_async_remote_copy(
                    src_ref=ins[b], dst_ref=outs[b].at[me], send_sem=send.at[s], recv_sem=recv.at[s],
                    device_id=_peer(x, y, c, k), device_id_type=pl.DeviceIdType.MESH))
        for cp in copies:
            cp.start()
        for cp in copies:
            cp.wait()
        for cp in locs:
            cp.wait()

    any_spec = pl.BlockSpec(memory_space=pl.ANY)
    n = (N_DEV - 1) * nb
    return pl.pallas_call(
        body, name="all_gather", in_specs=[any_spec] * nb, out_specs=[any_spec] * nb,
        out_shape=[S((N_DEV,) + b.shape, b.dtype) for b in bufs],
        scratch_shapes=[pltpu.SemaphoreType.DMA((n,)), pltpu.SemaphoreType.DMA((n,)), pltpu.SemaphoreType.DMA((nb,))],
    )(*bufs)


def _grad_exchange(slabs, small):
    ns = len(slabs)
    nb = ns + 1

    def body(*refs):
        ins, outs = refs[:nb], refs[nb:2 * nb]
        send, recv, loc = refs[2 * nb:]
        x, y, c, me = _mesh_pos()
        locs = [pltpu.make_async_copy(ins[b].at[me] if b < ns else ins[b], outs[b].at[me], loc.at[b]) for b in range(nb)]
        for cp in locs:
            cp.start()
        copies = []
        for k in range(1, N_DEV):
            px, py, pc = _peer(x, y, c, k)
            pid = 4 * px + 2 * py + pc
            for b in range(nb):
                s = (k - 1) * nb + b
                copies.append(pltpu.make_async_remote_copy(
                    src_ref=ins[b].at[pid] if b < ns else ins[b], dst_ref=outs[b].at[me],
                    send_sem=send.at[s], recv_sem=recv.at[s],
                    device_id=(px, py, pc), device_id_type=pl.DeviceIdType.MESH))
        for cp in copies:
            cp.start()
        for cp in copies:
            cp.wait()
        for cp in locs:
            cp.wait()

    n = (N_DEV - 1) * nb
    return pl.pallas_call(
        body, name="grad_exchange", in_specs=[_ANY] * nb, out_specs=[_ANY] * nb,
        out_shape=[S(b.shape, b.dtype) for b in slabs] + [S((N_DEV,) + small.shape, small.dtype)],
        scratch_shapes=[pltpu.SemaphoreType.DMA((n,)), pltpu.SemaphoreType.DMA((n,)), pltpu.SemaphoreType.DMA((nb,))],
    )(*slabs, small)


def _sum_parts(p_ref):
    acc = p_ref[0].astype(F32)
    for d in range(1, N_DEV):
        acc = acc + p_ref[d].astype(F32)
    return acc


def _adamw_math(g, w, m, v):
    m = ADAM_B1 * m + (1.0 - ADAM_B1) * g
    v = ADAM_B2 * v + (1.0 - ADAM_B2) * (g * g)
    m_hat = m / (1.0 - ADAM_B1 ** ADAM_STEP)
    v_hat = v / (1.0 - ADAM_B2 ** ADAM_STEP)
    return -ADAM_LR * (m_hat / (jnp.sqrt(v_hat) + ADAM_EPS) + ADAM_WD * w), m, v


def _sum8_adamw(parts, w, m, v, name):
    _, R, C = parts.shape
    tb = 256 if R % 256 == 0 else R

    def body(p_ref, w_ref, m_ref, v_ref, g_ref, d_ref, mo_ref, vo_ref):
        g = _sum_parts(p_ref)
        g_ref[...] = g
        d_ref[...], mo_ref[...], vo_ref[...] = _adamw_math(g, w_ref[...], m_ref[...], v_ref[...])

    return pl.pallas_call(
        body, name=name, grid=(R // tb,),
        in_specs=[pl.BlockSpec((N_DEV, tb, C), lambda i: (0, i, 0))] + [_row(tb, C)] * 3, out_specs=[_row(tb, C)] * 4,
        out_shape=[S((R, C), F32)] * 4, compiler_params=_params(("parallel",)))(parts, w, m, v)


SMALL_LAYOUT = (
    ("ln_emb_g", 0, 1024), ("ln_emb_b", 0, 1024), ("ssm_conv_b", 0, 1024), ("ssm_conv_b", 1024, 512),
    ("dt_bias", 0, N_HEADS), ("a_log", 0, N_HEADS), ("d_skip", 0, N_HEADS), ("ssm_norm_g", 0, 1024),
    ("b_glu", 0, 1024), ("b_glu", 1024, 1024), ("conf_conv_b", 0, 1024), ("conf_ln_g", 0, 1024),
    ("conf_ln_b", 0, 1024), ("b_out", 0, 1024), ("ln1_g", 0, 1024), ("ln1_b", 0, 1024), ("ln2_g", 0, 1024),
    ("ln2_b", 0, 1024))
SMALL_ROWS = 24
SMALL = tuple(dict.fromkeys(n for n, _, _ in SMALL_LAYOUT))


def _pack_small(rows):
    def body(*refs):
        o_ref = refs[-1]
        o_ref[...] = jnp.zeros_like(o_ref)
        for r, ref in enumerate(refs[:-1]):
            o_ref[r:r + 1, 0:ref.shape[1]] = ref[...]

    return pl.pallas_call(body, name="pack_small", out_shape=S((SMALL_ROWS, 1024), F32))(*rows)


def _small_update(parts, w, m, v):
    def body(*refs):
        p_ref = refs[0]
        ins = {n: refs[1 + 3 * i:4 + 3 * i] for i, n in enumerate(SMALL)}
        o0 = 1 + 3 * len(SMALL)
        outs = {n: refs[o0 + 4 * i:o0 + 4 * i + 4] for i, n in enumerate(SMALL)}
        gsum = refs[-1]
        gsum[...] = _sum_parts(p_ref)
        for r, (n, off, wd) in enumerate(SMALL_LAYOUT):
            cs = slice(off, off + wd)
            g = gsum[r:r + 1, 0:wd]
            w_ref, m_ref, v_ref = ins[n]
            g_ref, d_ref, mo_ref, vo_ref = outs[n]
            g_ref[:, cs] = g
            d_ref[:, cs], mo_ref[:, cs], vo_ref[:, cs] = _adamw_math(g, w_ref[:, cs], m_ref[:, cs], v_ref[:, cs])

    args = [parts] + [a for n in SMALL for a in (w[n], m[n], v[n])]
    res = pl.pallas_call(
        body, name="small_update", out_shape=[S(w[n].shape, F32) for n in SMALL for _ in range(4)],
        scratch_shapes=[pltpu.VMEM((SMALL_ROWS, 1024), F32)])(*args)
    return tuple({n: res[4 * i + j] for i, n in enumerate(SMALL)} for j in range(4))


def _local_step(x, p, tgt, W):
    r1 = lambda v: v.reshape(1, -1).astype(F32)
    pad_l = lambda v: jnp.pad(r1(v), ((0, 0), (0, LANES - v.size)))
    w_in = W["w_in"]
    w_main = jnp.concatenate([w_in[:, 2576:4624], w_in[:, 0:2048], w_in[:, 4624:5648], w_in[:, 2048:2560]], axis=1)
    w_dt = jnp.pad(w_in[:, 2560:2576], ((0, 0), (0, LANES - N_HEADS)))
    scw, scb = W["ssm_conv_w"], r1(W["ssm_conv_b"])
    wx, wb, bx, bb = scw[:, :1024], scw[:, 1024:], scb[:, :1024], scb[:, 1024:]
    dt_bias, alog = pad_l(W["dt_bias"]), pad_l(W["a_log"])
    dskip_row = jnp.repeat(W["d_skip"].reshape(-1), HEAD).reshape(1, -1)
    norm_g = r1(W["ssm_norm_g"])
    bglu = r1(W["b_glu"])
    ba, bbg = bglu[:, :1024], bglu[:, 1024:]
    ccw, ccb, clg, clb = W["conf_conv_w"], r1(W["conf_conv_b"]), r1(W["conf_ln_g"]), r1(W["conf_ln_b"])

    h0, h0b = _ln_emb_fwd(x, r1(W["ln_emb_g"]), r1(W["ln_emb_b"]))
    proj = _mm(h0b, w_main, "nn", "in_proj")
    dt_raw = _mm(h0b, w_dt, "nn", "in_proj_dt")
    xs_c, bc_c, dt = _ssd_pre_fwd(proj, dt_raw, wx, wb, bx, bb, dt_bias)
    mix, ypre, hprev = _ssd_fwd(xs_c, bc_c, dt, proj, alog, dskip_row, norm_g)
    u1, mix = _conf_fwd(mix, proj, ccw, ccb, clg, clb, ba, bbg)
    out = _mm(mix, W["w_out"], "nn", "out_proj")
    h1, h1b = _post1_fwd(h0, out, r1(W["b_out"]), r1(W["ln1_g"]), r1(W["ln1_b"]))
    gpre = _mm(h1b, W["w_ple_gate"], "nn", "ple_gate")
    pb = p.astype(BF16)
    ple = _mm(pb, W["w_ple_proj"], "nn", "ple_proj")
    dh1a, dgp, dple, loss, dln2g, dln2b = _post2(h1, gpre, ple, tgt, r1(W["ln2_g"]), r1(W["ln2_b"]))

    g = {}
    g["w_ple_proj"] = _mm(pb.T, dple, "nn", "d_ple_proj", out_dtype=BF16)
    g["w_ple_gate"] = _mm(h1b.T, dgp, "nn", "d_ple_gate", out_dtype=BF16)
    dh1b = _mm(dgp, W["w_ple_gate"], "nt", "d_h1")
    dout, dh0a, dln1g, dln1b, dbout = _post1_bwd(dh1a, dh1b, h0, out, r1(W["b_out"]), r1(W["ln1_g"]))
    g["w_out"] = _mm(mix.T, dout, "nn", "d_w_out", out_dtype=BF16)
    dmix = _mm(dout, W["w_out"], "nt", "d_mix")
    du1, dproj, dclg, dclb = _conf_bwd1(dmix, u1, proj, clg, clb)
    dproj, g["conf_conv_w"], dccb, dba, dbb = _conf_bwd2(dproj, proj, du1, ccw, ba, bbg)
    dxs_c, dbc_c, ddt, dproj, dng, ddsk, dalog = _ssd_bwd(
        dproj, xs_c, bc_c, dt, proj, ypre, hprev, dmix, alog, dskip_row, norm_g)
    dproj, dwx, dbx = _ssd_conv_bwd(dproj, proj, dxs_c, wx, bx, 1024, C_XS, "ssd_conv_bwd_x")
    dproj, dwb, dbb2 = _ssd_conv_bwd(dproj, proj, dbc_c, wb, bb, 512, C_BC, "ssd_conv_bwd_bc")
    ddtr, ddtb = _dt_bwd(ddt, dt_raw, dt_bias)
    g["ssm_conv_w"] = (dwx, dwb)
    h0bt = h0b.T
    g["w_in"] = (_mm(h0bt, dproj, "nn", "d_w_in", out_dtype=BF16), _mm(h0bt, ddtr, "nn", "d_w_dt", out_dtype=BF16))
    dh0 = _mm(dproj, w_main, "nt", "d_h0", add=dh0a, tk=1408)
    dh0 = _mm(ddtr, w_dt, "nt", "d_h0_dt", add=dh0)
    grad_x, dlng, dlnb = _ln_emb_bwd(x, dh0, r1(W["ln_emb_g"]))
    g["rows"] = [dlng, dlnb, dbx, dbb2, ddtb, dalog, ddsk, dng, dba, dbb, dccb, dclg, dclb, dbout, dln1g, dln1b,
                 dln2g, dln2b]
    return loss[0, 0], grad_x, g


def _w_in_columns(dw_main, dw_dt):
    return jnp.concatenate([dw_main[:, 2048:4096], dw_main[:, 5120:5632], dw_dt[:, :N_HEADS],
                            dw_main[:, 0:2048], dw_main[:, 4096:5120]], axis=1)


WEIGHTS = ['ln_emb_g', 'ln_emb_b', 'w_in', 'ssm_conv_w', 'ssm_conv_b', 'dt_bias', 'a_log', 'd_skip', 'ssm_norm_g',
           'b_glu', 'conf_conv_w', 'conf_conv_b', 'conf_ln_g', 'conf_ln_b', 'w_out', 'b_out', 'ln1_g', 'ln1_b',
           'w_ple_gate', 'w_ple_proj', 'ln2_g', 'ln2_b']
SHARDED = (("w_in", True), ("w_out", False), ("w_ple_gate", False), ("w_ple_proj", True), ("ssm_conv_w", True),
           ("conf_conv_w", True))


def _stack_shards(a, by_cols):
    if by_cols:
        return a.reshape(a.shape[0], N_DEV, a.shape[1] // N_DEV).transpose(1, 0, 2)
    return a.reshape(N_DEV, a.shape[0] // N_DEV, a.shape[1])


def _unstack_shards(a, by_cols):
    if by_cols:
        return a.transpose(1, 0, 2).reshape(a.shape[1], N_DEV * a.shape[2])
    return a.reshape(N_DEV * a.shape[1], a.shape[2])


def kernel(x, p, ln_emb_g, ln_emb_b, w_in, ssm_conv_w, ssm_conv_b, dt_bias, a_log, d_skip, ssm_norm_g, b_glu, conf_conv_w, conf_conv_b, conf_ln_g, conf_ln_b, w_out, b_out, ln1_g, ln1_b, w_ple_gate, w_ple_proj, ln2_g, ln2_b, loss_target, m_ln_emb_g, m_ln_emb_b, m_w_in, m_ssm_conv_w, m_ssm_conv_b, m_dt_bias, m_a_log, m_d_skip, m_ssm_norm_g, m_b_glu, m_conf_conv_w, m_conf_conv_b, m_conf_ln_g, m_conf_ln_b, m_w_out, m_b_out, m_ln1_g, m_ln1_b, m_w_ple_gate, m_w_ple_proj, m_ln2_g, m_ln2_b, v_ln_emb_g, v_ln_emb_b, v_w_in, v_ssm_conv_w, v_ssm_conv_b, v_dt_bias, v_a_log, v_d_skip, v_ssm_norm_g, v_b_glu, v_conf_conv_w, v_conf_conv_b, v_conf_ln_g, v_conf_ln_b, v_w_out, v_b_out, v_ln1_g, v_ln1_b, v_w_ple_gate, v_w_ple_proj, v_ln2_g, v_ln2_b):
    loc = dict(locals())
    w = {n: loc[n] for n in WEIGHTS}
    m = {n: loc["m_" + n] for n in WEIGHTS}
    v = {n: loc["v_" + n] for n in WEIGHTS}
    sharded = [n for n, _ in SHARDED]

    local = [w[n][0].astype(BF16) if n.startswith("w_") else w[n][0] for n in sharded]
    gathered = _all_gather(local)
    W = {n: w[n].reshape(-1) for n in SMALL}
    for (n, by_cols), a in zip(SHARDED, gathered):
        W[n] = _unstack_shards(a, by_cols)

    loss, grad_x, g = _local_step(x[0], p[0, 0], loss_target[0], W)
    loss = lax.psum(loss, ("x", "y", "c"))

    g["w_in"] = _w_in_columns(*g["w_in"])
    g["ssm_conv_w"] = jnp.concatenate(g["ssm_conv_w"], axis=1)
    slabs = [_stack_shards(g[n], by_cols) for n, by_cols in SHARDED]
    *recv, recv_small = _grad_exchange(slabs, _pack_small(g["rows"]))

    grads, delta, new_m, new_v = {}, {}, {}, {}
    for n, parts in zip(sharded, recv):
        res = _sum8_adamw(parts, w[n][0], m[n][0], v[n][0], "adamw_" + n)
        grads[n], delta[n], new_m[n], new_v[n] = (r[None] for r in res)
    two_d = lambda d: {n: d[n].reshape(1, -1) for n in SMALL}
    for dst, res in zip((grads, delta, new_m, new_v), _small_update(recv_small, two_d(w), two_d(m), two_d(v))):
        for n in SMALL:
            dst[n] = res[n].reshape(w[n].shape)
    return (loss, grad_x[None], *[grads[n] for n in WEIGHTS], *[delta[n] for n in WEIGHTS],
            *[new_m[n] for n in WEIGHTS], *[new_v[n] for n in WEIGHTS])
```

```python
import functools

import numpy as np
import jax
import jax.numpy as jnp
from jax import lax
from jax.experimental import pallas as pl
from jax.experimental.pallas import tpu as pltpu

F32, BF16 = jnp.float32, jnp.bfloat16
S = jax.ShapeDtypeStruct

N_DEV = 8
D = 1024
D_PLE = 256
D_SSM = 1024
D_CONF = 1024
N_HEADS = 16
HEAD = 64
N_STATE = 128
CHUNK = 128
K_SSM = 4
K_CONF = 31
D_IN = 5648
COLS_PER_DEV = D_IN // N_DEV
LN_EPS = 1e-5
RMS_EPS = 1e-5
ALPHA = 2.0 ** 0.25
LANES = 128
TB = 256
RG = 32
HALO_SSM = 8
HALO_CONF = 32
VMEM_LIMIT = 56 * 1024 * 1024

ADAM_LR, ADAM_B1, ADAM_B2, ADAM_EPS, ADAM_WD, ADAM_STEP = 0.001, 0.9, 0.999, 1e-08, 0.01, 10

C_GLUA, C_GLUB, C_XS, C_Z, C_CG = 0, 1, 2, 3, 4
C_BC = 10
N_MAIN = 5632


def _params(sem, vmem=VMEM_LIMIT):
    return pltpu.CompilerParams(dimension_semantics=sem, vmem_limit_bytes=vmem)


def _row(tb, n, col=0):
    return pl.BlockSpec((tb, n), lambda i: (i, col))


def _full(shape):
    return pl.BlockSpec(shape, lambda i: (0,) * len(shape))


_ANY = pl.BlockSpec(memory_space=pl.ANY)


def _prev(tb, halo, n, col=0):
    r = tb // halo
    return pl.BlockSpec((halo, n), lambda i: (jnp.maximum(i * r - 1, 0), col))


def _next(tb, halo, n, nt, col=0):
    r = tb // halo
    return pl.BlockSpec((halo, n), lambda i: (jnp.minimum((i + 1) * r, nt * r - 1), col))


def _row_loop(tb, fn):
    def it(r, c):
        fn(pl.ds(pl.multiple_of(r * RG, RG), RG))
        return c
    lax.fori_loop(0, tb // RG, it, 0)


def _col_loop(n, fn):
    def it(j, c):
        fn(pl.ds(pl.multiple_of(j * LANES, LANES), LANES))
        return c
    lax.fori_loop(0, n // LANES, it, 0)


def _sigmoid(x):
    return 1.0 / (1.0 + jnp.exp(-x))


def _dsilu(x, s):
    return s * (1.0 + x * (1.0 - s))


def _ln_stats(v):
    mu = jnp.mean(v, axis=-1, keepdims=True)
    c = v - mu
    r = lax.rsqrt(jnp.mean(c * c, axis=-1, keepdims=True) + LN_EPS)
    return c * r, r


def _ln_bwd(dy, xhat, r, g):
    dxh = dy * g
    dv = r * (dxh - jnp.mean(dxh, axis=-1, keepdims=True) - xhat * jnp.mean(dxh * xhat, axis=-1, keepdims=True))
    return dv, jnp.sum(dy * xhat, axis=0, keepdims=True), jnp.sum(dy, axis=0, keepdims=True)


def _dot(a, b, dims=((1,), (0,))):
    return lax.dot_general(a.astype(BF16), b.astype(BF16), (dims, ((), ())), preferred_element_type=F32)


_NT = ((1,), (1,))
_TN = ((0,), (0,))


def _dot_hi(a, b, dims=((1,), (0,))):
    return lax.dot_general(a, b, (dims, ((), ())), precision=lax.Precision.HIGHEST, preferred_element_type=F32)


def _mm(a, b, mode, name, out_dtype=F32, add=None, tm=1024, tn=512, tk=1024, exchange=None):
    if mode == "nn":
        (M, K), N = a.shape, b.shape[1]
    elif mode == "tn":
        (K, M), N = a.shape, b.shape[1]
    else:
        (M, K), N = a.shape, b.shape[0]
    tm, tn, tk = min(tm, M), min(tn, N), min(tk, K)
    assert M % tm == 0 and N % tn == 0 and K % tk == 0, (name, M, N, K)
    grid = (M // tm, N // tn, K // tk)
    nk = grid[2]
    dims = {"nn": ((1,), (0,)), "tn": _TN, "nt": _NT}[mode]
    n_in = 2 + (add is not None)
    xbufs, kinds = exchange if exchange is not None else ((), ())
    nx = len(xbufs)

    def body(*refs):
        a_ref, b_ref = refs[:2]
        o_ref = refs[n_in + nx]
        acc = refs[n_in + 2 * nx + 1]
        i, j, k = pl.program_id(0), pl.program_id(1), pl.program_id(2)
        if nx:
            copies = _exchange_copies(refs[n_in:n_in + nx], refs[n_in + nx + 1:n_in + 2 * nx + 1], kinds,
                                      *refs[n_in + 2 * nx + 2:])

            @pl.when((i == 0) & (j == 0) & (k == 0))
            def _():
                for cp in copies:
                    cp.start()

        @pl.when(k == 0)
        def _():
            acc[...] = jnp.zeros_like(acc)

        acc[...] += _dot(a_ref[...], b_ref[...], dims)

        @pl.when(k == nk - 1)
        def _():
            r = acc[...]
            if add is not None:
                r = r + refs[2][...]
            o_ref[...] = r.astype(out_dtype)

        if nx:
            @pl.when((i == grid[0] - 1) & (j == grid[1] - 1) & (k == nk - 1))
            def _():
                for cp in copies:
                    cp.wait()

    a_spec = pl.BlockSpec((tk, tm), lambda i, j, k: (k, i)) if mode == "tn" else pl.BlockSpec((tm, tk), lambda i, j, k: (i, k))
    b_spec = pl.BlockSpec((tn, tk), lambda i, j, k: (j, k)) if mode == "nt" else pl.BlockSpec((tk, tn), lambda i, j, k: (k, j))
    o_spec = pl.BlockSpec((tm, tn), lambda i, j, k: (i, j))
    ins, specs = [a, b], [a_spec, b_spec]
    if add is not None:
        ins.append(add)
        specs.append(o_spec)
    if not nx:
        return pl.pallas_call(
            body, name=name, grid=grid, in_specs=specs, out_specs=o_spec,
            out_shape=S((M, N), out_dtype), scratch_shapes=[pltpu.VMEM((tm, tn), F32)],
            compiler_params=_params(("parallel", "parallel", "arbitrary")))(*ins)
    return pl.pallas_call(
        body, name=name, grid=grid, in_specs=specs + [_ANY] * nx, out_specs=[o_spec] + [_ANY] * nx,
        out_shape=[S((M, N), out_dtype)] + _exchange_shapes(xbufs, kinds),
        scratch_shapes=[pltpu.VMEM((tm, tn), F32)] + _exchange_sems(nx),
        compiler_params=_params(("arbitrary", "arbitrary", "arbitrary")))(*ins, *xbufs)


def _ln_emb_fwd(x, g, b):
    T = x.shape[0]

    def body(x_ref, g_ref, b_ref, h_ref, hb_ref):
        def rows(rs):
            xh, _ = _ln_stats(x_ref[rs, :])
            h = xh * g_ref[...] + b_ref[...]
            h_ref[rs, :] = h
            hb_ref[rs, :] = h.astype(BF16)
        _row_loop(TB, rows)

    return pl.pallas_call(
        body, name="ln_emb_fwd", grid=(T // TB,),
        in_specs=[_row(TB, D), _full((1, D)), _full((1, D))], out_specs=[_row(TB, D), _row(TB, D)],
        out_shape=[S((T, D), F32), S((T, D), BF16)], compiler_params=_params(("parallel",)))(x, g, b)


def _post1_fwd(h0, out, b_out, g, b):
    T = h0.shape[0]

    def body(h0_ref, out_ref, bo_ref, g_ref, b_ref, h_ref, hb_ref):
        def rows(rs):
            xh, _ = _ln_stats(ALPHA * h0_ref[rs, :] + out_ref[rs, :] + bo_ref[...])
            h = xh * g_ref[...] + b_ref[...]
            h_ref[rs, :] = h
            hb_ref[rs, :] = h.astype(BF16)
        _row_loop(TB, rows)

    return pl.pallas_call(
        body, name="post1_fwd", grid=(T // TB,),
        in_specs=[_row(TB, D), _row(TB, D)] + [_full((1, D))] * 3, out_specs=[_row(TB, D), _row(TB, D)],
        out_shape=[S((T, D), F32), S((T, D), BF16)], compiler_params=_params(("parallel",)))(h0, out, b_out, g, b)


def _post2(h1, gpre, ple, tgt, g, b):
    T = h1.shape[0]

    def body(h1_ref, gp_ref, ple_ref, tgt_ref, g_ref, b_ref, dh1_ref, dgp_ref, dple_ref, loss_ref, dg_ref, db_ref):
        @pl.when(pl.program_id(0) == 0)
        def _():
            loss_ref[...] = jnp.zeros_like(loss_ref)
            dg_ref[...] = jnp.zeros_like(dg_ref)
            db_ref[...] = jnp.zeros_like(db_ref)

        def rows(rs):
            gate = _sigmoid(gp_ref[rs, :])
            ple = ple_ref[rs, :]
            xh, r = _ln_stats(ALPHA * h1_ref[rs, :] + gate * ple)
            err = xh * g_ref[...] + b_ref[...] - tgt_ref[rs, :]
            loss_ref[...] += 0.5 * jnp.sum(jnp.mean(err * err, axis=-1, keepdims=True), axis=0, keepdims=True)
            dv, dg, db = _ln_bwd(err * (1.0 / D), xh, r, g_ref[...])
            dg_ref[...] += dg
            db_ref[...] += db
            dh1_ref[rs, :] = ALPHA * dv
            dgp_ref[rs, :] = (dv * ple * gate * (1.0 - gate)).astype(BF16)
            dple_ref[rs, :] = (dv * gate).astype(BF16)
        _row_loop(TB, rows)

    return pl.pallas_call(
        body, name="post2", grid=(T // TB,),
        in_specs=[_row(TB, D)] * 4 + [_full((1, D))] * 2,
        out_specs=[_row(TB, D)] * 3 + [_full((8, LANES)), _full((1, D)), _full((1, D))],
        out_shape=[S((T, D), F32), S((T, D), BF16), S((T, D), BF16), S((8, LANES), F32), S((1, D), F32), S((1, D), F32)],
        compiler_params=_params(("arbitrary",)))(h1, gpre, ple, tgt, g, b)


def _post1_bwd(dh1a, dh1b, h0, out, b_out, g):
    T = h0.shape[0]

    def body(da_ref, db2_ref, h0_ref, out_ref, bo_ref, g_ref, dout_ref, dh0_ref, dg_ref, db_ref, dbo_ref):
        @pl.when(pl.program_id(0) == 0)
        def _():
            dg_ref[...] = jnp.zeros_like(dg_ref)
            db_ref[...] = jnp.zeros_like(db_ref)
            dbo_ref[...] = jnp.zeros_like(dbo_ref)

        def rows(rs):
            xh, r = _ln_stats(ALPHA * h0_ref[rs, :] + out_ref[rs, :] + bo_ref[...])
            dv, dg, db = _ln_bwd(da_ref[rs, :] + db2_ref[rs, :], xh, r, g_ref[...])
            dg_ref[...] += dg
            db_ref[...] += db
            dbo_ref[...] += jnp.sum(dv, axis=0, keepdims=True)
            dout_ref[rs, :] = dv.astype(BF16)
            dh0_ref[rs, :] = ALPHA * dv
        _row_loop(TB, rows)

    return pl.pallas_call(
        body, name="post1_bwd", grid=(T // TB,),
        in_specs=[_row(TB, D)] * 4 + [_full((1, D))] * 2,
        out_specs=[_row(TB, D)] * 2 + [_full((1, D))] * 3,
        out_shape=[S((T, D), BF16), S((T, D), F32)] + [S((1, D), F32)] * 3,
        compiler_params=_params(("arbitrary",)))(dh1a, dh1b, h0, out, b_out, g)


def _ln_emb_bwd(x, dh0, g):
    T = x.shape[0]

    def body(x_ref, dh_ref, g_ref, dx_ref, dg_ref, db_ref):
        @pl.when(pl.program_id(0) == 0)
        def _():
            dg_ref[...] = jnp.zeros_like(dg_ref)
            db_ref[...] = jnp.zeros_like(db_ref)

        def rows(rs):
            xh, r = _ln_stats(x_ref[rs, :])
            dv, dg, db = _ln_bwd(dh_ref[rs, :], xh, r, g_ref[...])
            dg_ref[...] += dg
            db_ref[...] += db
            dx_ref[rs, :] = dv
        _row_loop(TB, rows)

    return pl.pallas_call(
        body, name="ln_emb_bwd", grid=(T // TB,),
        in_specs=[_row(TB, D), _row(TB, D), _full((1, D))], out_specs=[_row(TB, D), _full((1, D)), _full((1, D))],
        out_shape=[S((T, D), F32), S((1, D), F32), S((1, D), F32)],
        compiler_params=_params(("arbitrary",)))(x, dh0, g)


def _softplus(x):
    return jnp.maximum(x, 0.0) + jnp.log1p(jnp.exp(-jnp.abs(x)))


def _ssd_pre_fwd(proj, dt_raw, wx, wb, bx, bb, dt_bias):
    T = proj.shape[0]
    H = HALO_SSM

    def body(xs_ref, xsp_ref, bc_ref, bcp_ref, dtr_ref, wx_ref, wb_ref, bx_ref, bb_ref, dtb_ref,
             xso_ref, bco_ref, dto_ref, extx, extb):
        first = pl.program_id(0) == 0

        def conv(t_ref, p_ref, w_ref, b_ref, o_ref, ext, n):
            def blk(cols):
                ext[0:H, cols] = jnp.where(first, 0.0, p_ref[:, cols])
                ext[H:, cols] = t_ref[:, cols]
                for r0 in range(0, TB, 64):
                    acc = jnp.broadcast_to(b_ref[:, cols], (64, LANES))
                    for k in range(K_SSM):
                        acc = acc + w_ref[k:k + 1, cols] * ext[pl.ds(r0 + H - (K_SSM - 1) + k, 64), cols]
                    o_ref[pl.ds(r0, 64), cols] = acc * _sigmoid(acc)
            _col_loop(n, blk)

        conv(xs_ref, xsp_ref, wx_ref, bx_ref, xso_ref, extx, D_SSM)
        conv(bc_ref, bcp_ref, wb_ref, bb_ref, bco_ref, extb, 512)
        dto_ref[...] = _softplus(dtr_ref[...] + dtb_ref[...])

    return pl.pallas_call(
        body, name="ssd_pre_fwd", grid=(T // TB,),
        in_specs=[_row(TB, 1024, C_XS), _prev(TB, H, 1024, C_XS), _row(TB, 512, C_BC), _prev(TB, H, 512, C_BC),
                  _row(TB, LANES), _full((K_SSM, 1024)), _full((K_SSM, 512)), _full((1, 1024)), _full((1, 512)),
                  _full((1, LANES))],
        out_specs=[_row(TB, 1024), _row(TB, 512), _row(TB, LANES)],
        out_shape=[S((T, 1024), F32), S((T, 512), F32), S((T, LANES), F32)],
        scratch_shapes=[pltpu.VMEM((H + TB, 1024), F32), pltpu.VMEM((H + TB, 512), F32)],
        compiler_params=_params(("parallel",)))(proj, proj, proj, proj, dt_raw, wx, wb, bx, bb, dt_bias)


def _ssd_conv_bwd(dproj, proj, d_c, w, b, n, col, name):
    T = proj.shape[0]
    nt = T // TB
    H = HALO_SSM
    R = TB + H

    def body(dproj_ref, t_ref, p_ref, n_ref, d_ref, dn_ref, w_ref, b_ref, o_ref, dw_ref, dbias_ref, ext, dp):
        i = pl.program_id(0)
        first, last = i == 0, i == nt - 1

        @pl.when(first)
        def _():
            dw_ref[...] = jnp.zeros_like(dw_ref)
            dbias_ref[...] = jnp.zeros_like(dbias_ref)

        def blk(cols):
            ext[0:H, cols] = jnp.where(first, 0.0, p_ref[:, cols])
            ext[H:H + TB, cols] = t_ref[:, cols]
            ext[H + TB:, cols] = n_ref[:, cols]
            pre = jnp.broadcast_to(b_ref[:, cols], (R, LANES))
            for k in range(K_SSM):
                pre = pre + w_ref[k:k + 1, cols] * ext[pl.ds(H - (K_SSM - 1) + k, R), cols]
            s = _sigmoid(pre)
            ds = _dsilu(pre, s)
            dp[0:TB, cols] = d_ref[:, cols] * ds[0:TB]
            dp[TB:, cols] = jnp.where(last, 0.0, dn_ref[:, cols] * ds[TB:])
            dpt = dp[0:TB, cols]
            dbias_ref[:, cols] += jnp.sum(dpt, axis=0, keepdims=True)
            acc = jnp.zeros((TB, LANES), F32)
            for k in range(K_SSM):
                dw_ref[k:k + 1, cols] += jnp.sum(dpt * ext[pl.ds(H - (K_SSM - 1) + k, TB), cols], axis=0, keepdims=True)
                acc = acc + w_ref[k:k + 1, cols] * dp[pl.ds(K_SSM - 1 - k, TB), cols]
            o_ref[:, cols] = acc.astype(BF16)
        _col_loop(n, blk)

    return pl.pallas_call(
        body, name=name, grid=(nt,),
        in_specs=[_ANY, _row(TB, n, col), _prev(TB, H, n, col), _next(TB, H, n, nt, col),
                  _row(TB, n), _next(TB, H, n, nt), _full((K_SSM, n)), _full((1, n))],
        out_specs=[_row(TB, n, col), _full((K_SSM, n)), _full((1, n))],
        out_shape=[S(dproj.shape, BF16), S((K_SSM, n), F32), S((1, n), F32)],
        input_output_aliases={0: 0},
        scratch_shapes=[pltpu.VMEM((H + TB + H, n), F32), pltpu.VMEM((R, n), F32)],
        compiler_params=_params(("arbitrary",)))(dproj, proj, proj, proj, d_c, d_c, w, b)


def _dt_bwd(ddt, dt_raw, dt_bias):
    T = ddt.shape[0]

    def body(ddt_ref, dtr_ref, dtb_ref, o_ref, db_ref):
        @pl.when(pl.program_id(0) == 0)
        def _():
            db_ref[...] = jnp.zeros_like(db_ref)

        g = ddt_ref[...] * _sigmoid(dtr_ref[...] + dtb_ref[...])
        o_ref[...] = g.astype(BF16)
        db_ref[...] += jnp.sum(g, axis=0, keepdims=True)

    return pl.pallas_call(
        body, name="dt_bwd", grid=(T // TB,),
        in_specs=[_row(TB, LANES), _row(TB, LANES), _full((1, LANES))], out_specs=[_row(TB, LANES), _full((1, LANES))],
        out_shape=[S((T, LANES), BF16), S((1, LANES), F32)],
        compiler_params=_params(("arbitrary",)))(ddt, dt_raw, dt_bias)


def _ssd_consts():
    ex = np.zeros((LANES, D_SSM), np.float32)
    for h in range(N_HEADS):
        ex[h, h * HEAD:(h + 1) * HEAD] = 1.0
    tri = np.tril(np.ones((CHUNK, CHUNK), np.float32))
    return jnp.asarray(ex), jnp.asarray(ex.T.copy()), jnp.asarray(tri), jnp.asarray(tri.T.copy())


def _ssd_common(xs, dt, alog_ref, ex_ref, tri_ref):
    lane = lax.broadcasted_iota(jnp.int32, (1, LANES), 1)
    a = jnp.where(lane < N_HEADS, -jnp.exp(alog_ref[...]), 0.0)
    A = _dot_hi(tri_ref[...], dt * a)
    ex = ex_ref[...]
    Aex = _dot_hi(A, ex)
    dtex = _dot_hi(dt, ex)
    expA = jnp.exp(Aex)
    dec = jnp.exp(Aex[CHUNK - 1:CHUNK, :] - Aex)
    cd = _dot_hi(ex, jnp.broadcast_to(jnp.exp(A.T[:, CHUNK - 1:CHUNK]), (LANES, LANES)), _TN)
    return a, A, dtex, expA, dec, cd


def _decay_mask():
    sub = lax.broadcasted_iota(jnp.int32, (CHUNK, CHUNK), 0)
    lane = lax.broadcasted_iota(jnp.int32, (CHUNK, CHUNK), 1)
    return sub, lane, sub >= lane


def _ssd_fwd(xs_c, bc_c, dt, proj, alog, dskip_row, norm_g):
    T = xs_c.shape[0]
    nc = T // CHUNK
    ex, _, tri, _ = _ssd_consts()

    def body(xs_ref, bc_ref, dt_ref, z_ref, alog_ref, dsk_ref, ng_ref, ex_ref, tri_ref,
             ys_ref, ypre_ref, hprev_ref, Hs, ybuf):
        @pl.when(pl.program_id(0) == 0)
        def _():
            Hs[...] = jnp.zeros_like(Hs)

        hprev_ref[0] = Hs[...]
        xs, dt = xs_ref[...], dt_ref[...]
        a, A, dtex, expA, dec, cd = _ssd_common(xs, dt, alog_ref, ex_ref, tri_ref)
        AT = A.T
        xdt = xs * dtex
        xdec = xdt * dec
        _, _, causal = _decay_mask()
        for g in range(2):
            gs = slice(g * 512, (g + 1) * 512)
            B = bc_ref[:, g * N_STATE:(g + 1) * N_STATE]
            C = bc_ref[:, 256 + g * N_STATE:256 + (g + 1) * N_STATE]
            cb = _dot(C, B, _NT)
            Hg = Hs[gs, :]
            yoff = _dot(C, Hg, _NT) * expA[:, gs]
            for j in range(8):
                h = g * 8 + j
                hs = slice(h * HEAD, (h + 1) * HEAD)
                L = jnp.exp(jnp.where(causal, A[:, h:h + 1] - AT[h:h + 1, :], -1e30))
                ybuf[:, hs] = _dot(cb * L, xdt[:, hs]) + yoff[:, j * HEAD:(j + 1) * HEAD]
            Hs[gs, :] = cd[gs, :] * Hg + _dot(xdec[:, gs], B, _TN)
        ypre = ybuf[...] + dsk_ref[...] * xs
        ypre_ref[...] = ypre
        z = z_ref[...]
        yz = ypre * (z * _sigmoid(z))
        for g in range(2):
            gs = slice(g * 512, (g + 1) * 512)
            v = yz[:, gs]
            r = lax.rsqrt(jnp.mean(v * v, axis=-1, keepdims=True) + RMS_EPS)
            ys_ref[:, gs] = (v * r * ng_ref[:, gs]).astype(BF16)

    return pl.pallas_call(
        body, name="ssd_fwd", grid=(nc,),
        in_specs=[_row(CHUNK, 1024), _row(CHUNK, 512), _row(CHUNK, LANES), _row(CHUNK, 1024, C_Z),
                  _full((1, LANES)), _full((1, 1024)), _full((1, 1024)), _full((LANES, 1024)), _full((CHUNK, CHUNK))],
        out_specs=[_row(CHUNK, 1024), _row(CHUNK, 1024), pl.BlockSpec((1, 1024, N_STATE), lambda c: (c, 0, 0))],
        out_shape=[S((T, 2048), BF16), S((T, 1024), F32), S((nc, 1024, N_STATE), F32)],
        scratch_shapes=[pltpu.VMEM((1024, N_STATE), F32), pltpu.VMEM((CHUNK, 1024), F32)],
        compiler_params=_params(("arbitrary",)))(xs_c, bc_c, dt, proj, alog, dskip_row, norm_g, ex, tri)


def _ssd_bwd(dproj, xs_c, bc_c, dt, proj, ypre, hprev, dmix, alog, dskip_row, norm_g):
    T = xs_c.shape[0]
    nc = T // CHUNK
    ex, ext, tri, triu = _ssd_consts()
    rev = lambda n, col=0: pl.BlockSpec((CHUNK, n), lambda c: (nc - 1 - c, col))

    def body(dproj_ref, xs_ref, bc_ref, dt_ref, z_ref, ypre_ref, hprev_ref, dys_ref, alog_ref, dsk_ref, ng_ref,
             ex_ref, ext_ref, tri_ref, triu_ref,
             dxs_ref, dbc_ref, ddt_ref, dz_ref, dng_ref, ddsk_ref, dalog_ref, dHs, dxbuf, dskacc):
        c = pl.program_id(0)

        @pl.when(c == 0)
        def _():
            dHs[...] = jnp.zeros_like(dHs)
            dng_ref[...] = jnp.zeros_like(dng_ref)
            dalog_ref[...] = jnp.zeros_like(dalog_ref)
            dskacc[...] = jnp.zeros_like(dskacc)

        xs, dt, z, ypre, dys = xs_ref[...], dt_ref[...], z_ref[...], ypre_ref[...], dys_ref[...]
        sg = _sigmoid(z)
        sz = z * sg
        yz = ypre * sz
        dyz_parts = []
        for g in range(2):
            gs = slice(g * 512, (g + 1) * 512)
            v = yz[:, gs]
            r = lax.rsqrt(jnp.mean(v * v, axis=-1, keepdims=True) + RMS_EPS)
            vn = v * r
            dng_ref[:, gs] += jnp.sum(dys[:, gs] * vn, axis=0, keepdims=True)
            dvn = dys[:, gs] * ng_ref[:, gs]
            dyz_parts.append(r * (dvn - vn * jnp.mean(dvn * vn, axis=-1, keepdims=True)))
        dyz = jnp.concatenate(dyz_parts, axis=1)
        dy = dyz * sz
        dz_ref[...] = (dyz * ypre * _dsilu(z, sg)).astype(BF16)
        dskacc[...] += jnp.sum(dy * xs, axis=0, keepdims=True)

        a, A, dtex, expA, dec, cd = _ssd_common(xs, dt, alog_ref, ex_ref, tri_ref)
        AT = A.T
        xdt = xs * dtex
        xdec = xdt * dec
        dye = dy * expA
        H = hprev_ref[0]
        dHn = dHs[...]
        sub, lane, causal = _decay_mask()
        dAc = jnp.zeros((CHUNK, LANES), F32)
        Rm = jnp.zeros((CHUNK, LANES), F32)
        yoff_parts, q_parts = [], []
        for g in range(2):
            gs = slice(g * 512, (g + 1) * 512)
            B = bc_ref[:, g * N_STATE:(g + 1) * N_STATE]
            C = bc_ref[:, 256 + g * N_STATE:256 + (g + 1) * N_STATE]
            cb = _dot(C, B, _NT)
            Hg, dHg = H[gs, :], dHn[gs, :]
            Q = _dot(B, dHg, _NT)
            yoff_parts.append(_dot(C, Hg, _NT) * expA[:, gs])
            q_parts.append(Q)
            dcb = jnp.zeros((CHUNK, CHUNK), F32)
            for j in range(8):
                h = g * 8 + j
                hs = slice(h * HEAD, (h + 1) * HEAD)
                L = jnp.exp(jnp.where(causal, A[:, h:h + 1] - AT[h:h + 1, :], -1e30))
                M = cb * L
                G = _dot(dy[:, hs], xdt[:, hs], _NT)
                dxbuf[:, hs] = _dot(M, dy[:, hs], _TN)
                dcb = dcb + G * L
                E = G * M
                dAc = jnp.where(lane == h, jnp.sum(E, axis=1, keepdims=True), dAc)
                Rm = jnp.where(sub == h, jnp.sum(E, axis=0, keepdims=True), Rm)
            dbc_ref[:, g * N_STATE:(g + 1) * N_STATE] = _dot(dcb, C, _TN) + _dot(xdec[:, gs], dHg)
            dbc_ref[:, 256 + g * N_STATE:256 + (g + 1) * N_STATE] = _dot(dcb, B) + _dot(dye[:, gs], Hg)
            dHs[gs, :] = cd[gs, :] * dHg + _dot(dye[:, gs], C, _TN)
        yoff = jnp.concatenate(yoff_parts, axis=1)
        Qd = jnp.concatenate(q_parts, axis=1) * dec
        dxdt = dxbuf[...] + Qd
        extm = ext_ref[...]
        red_s = _dot_hi(xdt * Qd, extm)
        dA = dAc - Rm.T + _dot_hi(dy * yoff, extm) - red_s
        hd = jnp.sum(_dot_hi(H * dHn, extm, _TN), axis=0, keepdims=True)
        last_add = jnp.sum(red_s, axis=0, keepdims=True) + jnp.exp(A[CHUNK - 1:CHUNK, :]) * hd
        dA = dA + jnp.where(sub == CHUNK - 1, last_add, 0.0)
        dadt = _dot_hi(triu_ref[...], dA)
        ddt_ref[...] = dadt * a + _dot_hi(dxdt * xs, extm)
        dalog_ref[...] += jnp.sum(dadt * dt, axis=0, keepdims=True) * a
        dxs_ref[...] = dxdt * dtex + dsk_ref[...] * dy

        @pl.when(c == nc - 1)
        def _():
            ddsk_ref[...] = _dot_hi(jnp.broadcast_to(dskacc[...], (8, 1024)), extm)[0:1, :]

    return pl.pallas_call(
        body, name="ssd_bwd", grid=(nc,),
        in_specs=[_ANY, rev(1024), rev(512), rev(LANES), rev(1024, C_Z), rev(1024),
                  pl.BlockSpec((1, 1024, N_STATE), lambda c: (nc - 1 - c, 0, 0)), rev(1024, 0),
                  _full((1, LANES)), _full((1, 1024)), _full((1, 1024)),
                  _full((LANES, 1024)), _full((1024, LANES)), _full((CHUNK, CHUNK)), _full((CHUNK, CHUNK))],
        out_specs=[rev(1024), rev(512), rev(LANES), rev(1024, C_Z), _full((1, 1024)), _full((1, LANES)), _full((1, LANES))],
        out_shape=[S((T, 1024), F32), S((T, 512), F32), S((T, LANES), F32), S(dproj.shape, BF16),
                   S((1, 1024), F32), S((1, LANES), F32), S((1, LANES), F32)],
        input_output_aliases={0: 3},
        scratch_shapes=[pltpu.VMEM((1024, N_STATE), F32), pltpu.VMEM((CHUNK, 1024), F32), pltpu.VMEM((1, 1024), F32)],
        compiler_params=_params(("arbitrary",)))(
            dproj, xs_c, bc_c, dt, proj, ypre, hprev, dmix, alog, dskip_row, norm_g, ex, ext, tri, triu)


def _shifted_copies(ext, ext8):
    n = ext8.shape[1]
    for r in range(8):
        ext8[r] = ext[pl.ds(r, n), :]


def _shifted(ext8, off, rows):
    return ext8[off % 8, pl.ds(off - off % 8, rows), :]


def _conf_fwd(mix, proj, w, cb, lg, lb, ba, bb):
    T = proj.shape[0]
    H = HALO_CONF

    def body(mix_ref, ga_ref, gap_ref, gb_ref, gbp_ref, cg_ref, w_ref, cb_ref, lg_ref, lb_ref, ba_ref, bb_ref,
             u1_ref, yc_ref, ext, ext8):
        first = pl.program_id(0) == 0
        ext[H + TB:, :] = jnp.zeros((8, LANES), F32)

        def blk(cols):
            up = (gap_ref[:, cols] + ba_ref[:, cols]) * _sigmoid(gbp_ref[:, cols] + bb_ref[:, cols])
            ext[0:H, :] = jnp.where(first, 0.0, up)
            ext[H:H + TB, :] = (ga_ref[:, cols] + ba_ref[:, cols]) * _sigmoid(gb_ref[:, cols] + bb_ref[:, cols])
            _shifted_copies(ext, ext8)
            for r0 in range(0, TB, 64):
                acc = jnp.broadcast_to(cb_ref[:, cols], (64, LANES))
                for k in range(K_CONF):
                    acc = acc + w_ref[k:k + 1, cols] * _shifted(ext8, r0 + H - (K_CONF - 1) + k, 64)
                u1_ref[pl.ds(r0, 64), cols] = acc
        _col_loop(D_CONF, blk)

        def rows(rs):
            xh, _ = _ln_stats(u1_ref[rs, :])
            u2 = xh * lg_ref[...] + lb_ref[...]
            cg = cg_ref[rs, :]
            yc_ref[rs, :] = (u2 * _sigmoid(u2) * cg * _sigmoid(cg)).astype(BF16)
        _row_loop(TB, rows)

    return pl.pallas_call(
        body, name="conf_fwd", grid=(T // TB,),
        in_specs=[_ANY, _row(TB, 1024, C_GLUA), _prev(TB, H, 1024, C_GLUA), _row(TB, 1024, C_GLUB),
                  _prev(TB, H, 1024, C_GLUB), _row(TB, 1024, C_CG), _full((K_CONF, 1024))] + [_full((1, 1024))] * 5,
        out_specs=[_row(TB, 1024), _row(TB, 1024, 1)],
        out_shape=[S((T, 1024), F32), S((T, 2048), BF16)],
        input_output_aliases={0: 1},
        scratch_shapes=[pltpu.VMEM((H + TB + 8, LANES), F32), pltpu.VMEM((8, H + TB, LANES), F32)],
        compiler_params=_params(("parallel",)))(mix, proj, proj, proj, proj, proj, w, cb, lg, lb, ba, bb)


def _conf_bwd1(dmix, u1, proj, lg, lb):
    T = u1.shape[0]

    def body(dy_ref, u1_ref, cg_ref, lg_ref, lb_ref, du1_ref, dcg_ref, dg_ref, db_ref):
        @pl.when(pl.program_id(0) == 0)
        def _():
            dg_ref[...] = jnp.zeros_like(dg_ref)
            db_ref[...] = jnp.zeros_like(db_ref)

        def rows(rs):
            xh, r = _ln_stats(u1_ref[rs, :])
            u2 = xh * lg_ref[...] + lb_ref[...]
            s2 = _sigmoid(u2)
            cg = cg_ref[rs, :]
            sc = _sigmoid(cg)
            dy = dy_ref[rs, :]
            dcg_ref[rs, :] = (dy * u2 * s2 * _dsilu(cg, sc)).astype(BF16)
            dv, dg, db = _ln_bwd(dy * cg * sc * _dsilu(u2, s2), xh, r, lg_ref[...])
            dg_ref[...] += dg
            db_ref[...] += db
            du1_ref[rs, :] = dv
        _row_loop(TB, rows)

    return pl.pallas_call(
        body, name="conf_bwd1", grid=(T // TB,),
        in_specs=[_row(TB, 1024, 1), _row(TB, 1024), _row(TB, 1024, C_CG), _full((1, 1024)), _full((1, 1024))],
        out_specs=[_row(TB, 1024), _row(TB, 1024, C_CG), _full((1, 1024)), _full((1, 1024))],
        out_shape=[S((T, 1024), F32), S((T, N_MAIN), BF16), S((1, 1024), F32), S((1, 1024), F32)],
        compiler_params=_params(("arbitrary",)))(dmix, u1, proj, lg, lb)


def _conf_bwd2(dproj, proj, du1, w, ba, bb):
    T = du1.shape[0]
    nt = T // TB
    H = HALO_CONF

    def body(dproj_ref, ga_ref, gap_ref, gb_ref, gbp_ref, du_ref, dun_ref, w_ref, ba_ref, bb_ref,
             dg_ref, dw_ref, dcb_ref, dba_ref, dbb_ref, ext, dext, ext8, dext8, dwacc):
        i = pl.program_id(0)
        first, last = i == 0, i == nt - 1

        @pl.when(first)
        def _():
            for r in (dcb_ref, dba_ref, dbb_ref, dwacc):
                r[...] = jnp.zeros_like(r)

        ext[H + TB:, :] = jnp.zeros((8, LANES), F32)
        dext[H + TB:, :] = jnp.zeros((8, LANES), F32)

        def blk(cols):
            cols_b = pl.ds(pl.multiple_of(cols.start + D_CONF, LANES), LANES)
            up = (gap_ref[:, cols] + ba_ref[:, cols]) * _sigmoid(gbp_ref[:, cols] + bb_ref[:, cols])
            ext[0:H, :] = jnp.where(first, 0.0, up)
            a = ga_ref[:, cols] + ba_ref[:, cols]
            sb = _sigmoid(gb_ref[:, cols] + bb_ref[:, cols])
            ext[H:H + TB, :] = a * sb
            du = du_ref[:, cols]
            dext[0:TB, :] = du
            dext[TB:TB + H, :] = jnp.where(last, 0.0, dun_ref[:, cols])
            _shifted_copies(ext, ext8)
            _shifted_copies(dext, dext8)
            dcb_ref[:, cols] += jnp.sum(du, axis=0, keepdims=True)
            for r0 in range(0, TB, 64):
                dur = du_ref[pl.ds(r0, 64), cols]
                acc = jnp.zeros((64, LANES), F32)
                for k in range(K_CONF):
                    prod = dur * _shifted(ext8, r0 + H - (K_CONF - 1) + k, 64)
                    dwacc[k * 8:(k + 1) * 8, cols] += prod.reshape(8, 8, LANES).sum(axis=0)
                    acc = acc + w_ref[k:k + 1, cols] * _shifted(dext8, r0 + K_CONF - 1 - k, 64)
                ar, sr = a[r0:r0 + 64], sb[r0:r0 + 64]
                da = acc * sr
                dbv = acc * ar * sr * (1.0 - sr)
                dg_ref[pl.ds(r0, 64), cols] = da.astype(BF16)
                dg_ref[pl.ds(r0, 64), cols_b] = dbv.astype(BF16)
                dba_ref[:, cols] += jnp.sum(da, axis=0, keepdims=True)
                dbb_ref[:, cols] += jnp.sum(dbv, axis=0, keepdims=True)
        _col_loop(D_CONF, blk)

        @pl.when(last)
        def _():
            dw_ref[...] = jnp.sum(dwacc[...].reshape(K_CONF, 8, D_CONF), axis=1)

    return pl.pallas_call(
        body, name="conf_bwd2", grid=(nt,),
        in_specs=[_ANY, _row(TB, 1024, C_GLUA), _prev(TB, H, 1024, C_GLUA), _row(TB, 1024, C_GLUB),
                  _prev(TB, H, 1024, C_GLUB), _row(TB, 1024), _next(TB, H, 1024, nt), _full((K_CONF, 1024)),
                  _full((1, 1024)), _full((1, 1024))],
        out_specs=[_row(TB, 2048), _full((K_CONF, 1024)), _full((1, 1024)), _full((1, 1024)), _full((1, 1024))],
        out_shape=[S(dproj.shape, BF16), S((K_CONF, 1024), F32)] + [S((1, 1024), F32)] * 3,
        input_output_aliases={0: 0},
        scratch_shapes=[pltpu.VMEM((H + TB + 8, LANES), F32), pltpu.VMEM((TB + H + 8, LANES), F32),
                        pltpu.VMEM((8, H + TB, LANES), F32), pltpu.VMEM((8, TB + H, LANES), F32),
                        pltpu.VMEM((K_CONF * 8, D_CONF), F32)],
        compiler_params=_params(("arbitrary",)))(dproj, proj, proj, proj, proj, du1, du1, w, ba, bb)


def _mesh_pos():
    x, y, c = lax.axis_index("x"), lax.axis_index("y"), lax.axis_index("c")
    return x, y, c, 4 * x + 2 * y + c


def _peer(x, y, c, k):
    return (x ^ ((k >> 2) & 1), y ^ ((k >> 1) & 1), c ^ (k & 1))


def _exchange_copies(ins, outs, kinds, send, recv, loc):
    nb = len(ins)
    x, y, c, me = _mesh_pos()
    src = lambda b, d: ins[b].at[d] if kinds[b] == "blocks" else ins[b]
    copies = [pltpu.make_async_copy(src(b, me), outs[b].at[me], loc.at[b]) for b in range(nb)]
    for k in range(1, N_DEV):
        px, py, pc = _peer(x, y, c, k)
        for b in range(nb):
            s = (k - 1) * nb + b
            copies.append(pltpu.make_async_remote_copy(
                src_ref=src(b, 4 * px + 2 * py + pc), dst_ref=outs[b].at[me], send_sem=send.at[s], recv_sem=recv.at[s],
                device_id=(px, py, pc), device_id_type=pl.DeviceIdType.MESH))
    return copies


def _exchange_shapes(bufs, kinds):
    return [S(b.shape if kd == "blocks" else (N_DEV,) + b.shape, b.dtype) for b, kd in zip(bufs, kinds)]


def _exchange_sems(nb):
    n = (N_DEV - 1) * nb
    return [pltpu.SemaphoreType.DMA((n,)), pltpu.SemaphoreType.DMA((n,)), pltpu.SemaphoreType.DMA((nb,))]


def _exchange(bufs, kinds, name):
    nb = len(bufs)

    def body(*refs):
        copies = _exchange_copies(refs[:nb], refs[nb:2 * nb], kinds, *refs[2 * nb:])
        for cp in copies:
            cp.start()
        for cp in copies:
            cp.wait()

    return pl.pallas_call(
        body, name=name, in_specs=[_ANY] * nb, out_specs=[_ANY] * nb,
        out_shape=_exchange_shapes(bufs, kinds), scratch_shapes=_exchange_sems(nb))(*bufs)


def _sum_parts(p_ref):
    acc = p_ref[0].astype(F32)
    for d in range(1, N_DEV):
        acc = acc + p_ref[d].astype(F32)
    return acc


def _adamw_math(g, w, m, v):
    m = ADAM_B1 * m + (1.0 - ADAM_B1) * g
    v = ADAM_B2 * v + (1.0 - ADAM_B2) * (g * g)
    m_hat = m / (1.0 - ADAM_B1 ** ADAM_STEP)
    v_hat = v / (1.0 - ADAM_B2 ** ADAM_STEP)
    return -ADAM_LR * (m_hat / (jnp.sqrt(v_hat) + ADAM_EPS) + ADAM_WD * w), m, v


def _sum8_adamw(parts, w, m, v, name):
    _, R, C = parts.shape
    tb = 256 if R % 256 == 0 else R

    def body(p_ref, w_ref, m_ref, v_ref, g_ref, d_ref, mo_ref, vo_ref):
        g = _sum_parts(p_ref)
        g_ref[...] = g
        d_ref[...], mo_ref[...], vo_ref[...] = _adamw_math(g, w_ref[...], m_ref[...], v_ref[...])

    return pl.pallas_call(
        body, name=name, grid=(R // tb,),
        in_specs=[pl.BlockSpec((N_DEV, tb, C), lambda i: (0, i, 0))] + [_row(tb, C)] * 3, out_specs=[_row(tb, C)] * 4,
        out_shape=[S((R, C), F32)] * 4, compiler_params=_params(("parallel",)))(parts, w, m, v)


SMALL_LAYOUT = (
    ("ln_emb_g", 0, 1024), ("ln_emb_b", 0, 1024), ("ssm_conv_b", 0, 1024), ("ssm_conv_b", 1024, 512),
    ("dt_bias", 0, N_HEADS), ("a_log", 0, N_HEADS), ("d_skip", 0, N_HEADS), ("ssm_norm_g", 0, 1024),
    ("b_glu", 0, 1024), ("b_glu", 1024, 1024), ("conf_conv_b", 0, 1024), ("conf_ln_g", 0, 1024),
    ("conf_ln_b", 0, 1024), ("b_out", 0, 1024), ("ln1_g", 0, 1024), ("ln1_b", 0, 1024), ("ln2_g", 0, 1024),
    ("ln2_b", 0, 1024))
SMALL_ROWS = 24
SMALL = tuple(dict.fromkeys(n for n, _, _ in SMALL_LAYOUT))


def _pack_small(rows):
    def body(*refs):
        o_ref = refs[-1]
        o_ref[...] = jnp.zeros_like(o_ref)
        for r, ref in enumerate(refs[:-1]):
            o_ref[r:r + 1, 0:ref.shape[1]] = ref[...]

    return pl.pallas_call(body, name="pack_small", out_shape=S((SMALL_ROWS, 1024), F32))(*rows)


def _small_update(parts, w, m, v):
    def body(*refs):
        p_ref = refs[0]
        ins = {n: refs[1 + 3 * i:4 + 3 * i] for i, n in enumerate(SMALL)}
        o0 = 1 + 3 * len(SMALL)
        outs = {n: refs[o0 + 4 * i:o0 + 4 * i + 4] for i, n in enumerate(SMALL)}
        gsum = refs[-1]
        gsum[...] = _sum_parts(p_ref)
        for r, (n, off, wd) in enumerate(SMALL_LAYOUT):
            cs = slice(off, off + wd)
            g = gsum[r:r + 1, 0:wd]
            w_ref, m_ref, v_ref = ins[n]
            g_ref, d_ref, mo_ref, vo_ref = outs[n]
            g_ref[:, cs] = g
            d_ref[:, cs], mo_ref[:, cs], vo_ref[:, cs] = _adamw_math(g, w_ref[:, cs], m_ref[:, cs], v_ref[:, cs])

    args = [parts] + [a for n in SMALL for a in (w[n], m[n], v[n])]
    res = pl.pallas_call(
        body, name="small_update", out_shape=[S(w[n].shape, F32) for n in SMALL for _ in range(4)],
        scratch_shapes=[pltpu.VMEM((SMALL_ROWS, 1024), F32)])(*args)
    return tuple({n: res[4 * i + j] for i, n in enumerate(SMALL)} for j in range(4))


LATE = ("w_out", "w_ple_gate", "w_ple_proj")


def _local_step(x, p, tgt, W, late=None):
    r1 = lambda v: v.reshape(1, -1).astype(F32)
    pad_l = lambda v: jnp.pad(r1(v), ((0, 0), (0, LANES - v.size)))
    w_in = W["w_in"]
    w_main = jnp.concatenate([w_in[:, 2576:4624], w_in[:, 0:2048], w_in[:, 4624:5648], w_in[:, 2048:2560]], axis=1)
    w_dt = jnp.pad(w_in[:, 2560:2576], ((0, 0), (0, LANES - N_HEADS)))
    scw, scb = W["ssm_conv_w"], r1(W["ssm_conv_b"])
    wx, wb, bx, bb = scw[:, :1024], scw[:, 1024:], scb[:, :1024], scb[:, 1024:]
    dt_bias, alog = pad_l(W["dt_bias"]), pad_l(W["a_log"])
    dskip_row = jnp.repeat(W["d_skip"].reshape(-1), HEAD).reshape(1, -1)
    norm_g = r1(W["ssm_norm_g"])
    bglu = r1(W["b_glu"])
    ba, bbg = bglu[:, :1024], bglu[:, 1024:]
    ccw, ccb, clg, clb = W["conf_conv_w"], r1(W["conf_conv_b"]), r1(W["conf_ln_g"]), r1(W["conf_ln_b"])

    h0, h0b = _ln_emb_fwd(x, r1(W["ln_emb_g"]), r1(W["ln_emb_b"]))
    if late is None:
        proj = _mm(h0b, w_main, "nn", "in_proj")
    else:
        proj, *gathered = _mm(h0b, w_main, "nn", "in_proj", exchange=(late, ("all",) * len(LATE)))
        W = dict(W, **{n: _unstack_shards(a, BY_COLS[n]) for n, a in zip(LATE, gathered)})
    dt_raw = _mm(h0b, w_dt, "nn", "in_proj_dt")
    xs_c, bc_c, dt = _ssd_pre_fwd(proj, dt_raw, wx, wb, bx, bb, dt_bias)
    mix, ypre, hprev = _ssd_fwd(xs_c, bc_c, dt, proj, alog, dskip_row, norm_g)
    u1, mix = _conf_fwd(mix, proj, ccw, ccb, clg, clb, ba, bbg)
    out = _mm(mix, W["w_out"], "nn", "out_proj")
    h1, h1b = _post1_fwd(h0, out, r1(W["b_out"]), r1(W["ln1_g"]), r1(W["ln1_b"]))
    gpre = _mm(h1b, W["w_ple_gate"], "nn", "ple_gate")
    pb = p.astype(BF16)
    ple = _mm(pb, W["w_ple_proj"], "nn", "ple_proj")
    dh1a, dgp, dple, loss, dln2g, dln2b = _post2(h1, gpre, ple, tgt, r1(W["ln2_g"]), r1(W["ln2_b"]))

    g = {}
    g["w_ple_proj"] = _mm(pb.T, dple, "nn", "d_ple_proj", out_dtype=BF16)
    g["w_ple_gate"] = _mm(h1b.T, dgp, "nn", "d_ple_gate", out_dtype=BF16)
    dh1b = _mm(dgp, W["w_ple_gate"], "nt", "d_h1")
    dout, dh0a, dln1g, dln1b, dbout = _post1_bwd(dh1a, dh1b, h0, out, r1(W["b_out"]), r1(W["ln1_g"]))
    g["w_out"] = _mm(mix.T, dout, "nn", "d_w_out", out_dtype=BF16)
    dmix = _mm(dout, W["w_out"], "nt", "d_mix")
    du1, dproj, dclg, dclb = _conf_bwd1(dmix, u1, proj, clg, clb)
    dproj, g["conf_conv_w"], dccb, dba, dbb = _conf_bwd2(dproj, proj, du1, ccw, ba, bbg)
    dxs_c, dbc_c, ddt, dproj, dng, ddsk, dalog = _ssd_bwd(
        dproj, xs_c, bc_c, dt, proj, ypre, hprev, dmix, alog, dskip_row, norm_g)
    dproj, dwx, dbx = _ssd_conv_bwd(dproj, proj, dxs_c, wx, bx, 1024, C_XS, "ssd_conv_bwd_x")
    dproj, dwb, dbb2 = _ssd_conv_bwd(dproj, proj, dbc_c, wb, bb, 512, C_BC, "ssd_conv_bwd_bc")
    ddtr, ddtb = _dt_bwd(ddt, dt_raw, dt_bias)
    g["ssm_conv_w"] = jnp.concatenate([dwx, dwb], axis=1)
    h0bt = h0b.T
    dw_dt = _mm(h0bt, ddtr, "nn", "d_w_dt", out_dtype=BF16)
    stack = lambda names: [_stack_shards(g[n], BY_COLS[n]) for n in names]
    if late is None:
        g["w_in"] = _w_in_columns(_mm(h0bt, dproj, "nn", "d_w_in", out_dtype=BF16), dw_dt)
        dh0 = _mm(dproj, w_main, "nt", "d_h0", add=dh0a, tk=1408)
    else:
        last = ("w_in", "ssm_conv_w", "conf_conv_w")
        dw_main, *recv_a = _mm(h0bt, dproj, "nn", "d_w_in", out_dtype=BF16,
                               exchange=(stack(LATE), ("blocks",) * len(LATE)))
        g["w_in"] = _w_in_columns(dw_main, dw_dt)
        dh0, *recv_b = _mm(dproj, w_main, "nt", "d_h0", add=dh0a, tk=1408, exchange=(stack(last), ("blocks",) * 3))
        g["recv"] = dict(zip(LATE + last, recv_a + recv_b))
    dh0 = _mm(ddtr, w_dt, "nt", "d_h0_dt", add=dh0)
    grad_x, dlng, dlnb = _ln_emb_bwd(x, dh0, r1(W["ln_emb_g"]))
    g["rows"] = [dlng, dlnb, dbx, dbb2, ddtb, dalog, ddsk, dng, dba, dbb, dccb, dclg, dclb, dbout, dln1g, dln1b,
                 dln2g, dln2b]
    return loss[0, 0], grad_x, g


def _w_in_columns(dw_main, dw_dt):
    return jnp.concatenate([dw_main[:, 2048:4096], dw_main[:, 5120:5632], dw_dt[:, :N_HEADS],
                            dw_main[:, 0:2048], dw_main[:, 4096:5120]], axis=1)


WEIGHTS = ['ln_emb_g', 'ln_emb_b', 'w_in', 'ssm_conv_w', 'ssm_conv_b', 'dt_bias', 'a_log', 'd_skip', 'ssm_norm_g',
           'b_glu', 'conf_conv_w', 'conf_conv_b', 'conf_ln_g', 'conf_ln_b', 'w_out', 'b_out', 'ln1_g', 'ln1_b',
           'w_ple_gate', 'w_ple_proj', 'ln2_g', 'ln2_b']
SHARDED = (("w_in", True), ("w_out", False), ("w_ple_gate", False), ("w_ple_proj", True), ("ssm_conv_w", True),
           ("conf_conv_w", True))
BY_COLS = dict(SHARDED)


def _stack_shards(a, by_cols):
    if by_cols:
        return a.reshape(a.shape[0], N_DEV, a.shape[1] // N_DEV).transpose(1, 0, 2)
    return a.reshape(N_DEV, a.shape[0] // N_DEV, a.shape[1])


def _unstack_shards(a, by_cols):
    if by_cols:
        return a.transpose(1, 0, 2).reshape(a.shape[1], N_DEV * a.shape[2])
    return a.reshape(N_DEV * a.shape[1], a.shape[2])


def kernel(x, p, ln_emb_g, ln_emb_b, w_in, ssm_conv_w, ssm_conv_b, dt_bias, a_log, d_skip, ssm_norm_g, b_glu, conf_conv_w, conf_conv_b, conf_ln_g, conf_ln_b, w_out, b_out, ln1_g, ln1_b, w_ple_gate, w_ple_proj, ln2_g, ln2_b, loss_target, m_ln_emb_g, m_ln_emb_b, m_w_in, m_ssm_conv_w, m_ssm_conv_b, m_dt_bias, m_a_log, m_d_skip, m_ssm_norm_g, m_b_glu, m_conf_conv_w, m_conf_conv_b, m_conf_ln_g, m_conf_ln_b, m_w_out, m_b_out, m_ln1_g, m_ln1_b, m_w_ple_gate, m_w_ple_proj, m_ln2_g, m_ln2_b, v_ln_emb_g, v_ln_emb_b, v_w_in, v_ssm_conv_w, v_ssm_conv_b, v_dt_bias, v_a_log, v_d_skip, v_ssm_norm_g, v_b_glu, v_conf_conv_w, v_conf_conv_b, v_conf_ln_g, v_conf_ln_b, v_w_out, v_b_out, v_ln1_g, v_ln1_b, v_w_ple_gate, v_w_ple_proj, v_ln2_g, v_ln2_b):
    loc = dict(locals())
    w = {n: loc[n] for n in WEIGHTS}
    m = {n: loc["m_" + n] for n in WEIGHTS}
    v = {n: loc["v_" + n] for n in WEIGHTS}
    sharded = [n for n, _ in SHARDED]

    local = {n: w[n][0].astype(BF16) if n.startswith("w_") else w[n][0] for n in sharded}
    first = [n for n in sharded if n not in LATE]
    W = {n: w[n].reshape(-1) for n in SMALL}
    for n, a in zip(first, _exchange([local[n] for n in first], ("all",) * len(first), "gather_first")):
        W[n] = _unstack_shards(a, BY_COLS[n])

    loss, grad_x, g = _local_step(x[0], p[0, 0], loss_target[0], W, late=[local[n] for n in LATE])
    loss = lax.psum(loss, ("x", "y", "c"))
    (recv_small,) = _exchange([_pack_small(g["rows"])], ("all",), "small_exchange")

    grads, delta, new_m, new_v = {}, {}, {}, {}
    for n in sharded:
        res = _sum8_adamw(g["recv"][n], w[n][0], m[n][0], v[n][0], "adamw_" + n)
        grads[n], delta[n], new_m[n], new_v[n] = (r[None] for r in res)
    two_d = lambda d: {n: d[n].reshape(1, -1) for n in SMALL}
    for dst, res in zip((grads, delta, new_m, new_v), _small_update(recv_small, two_d(w), two_d(m), two_d(v))):
        for n in SMALL:
            dst[n] = res[n].reshape(w[n].shape)
    return (loss, grad_x[None], *[grads[n] for n in WEIGHTS], *[delta[n] for n in WEIGHTS],
            *[new_m[n] for n in WEIGHTS], *[new_v[n] for n in WEIGHTS])
```

```python
import functools

import numpy as np
import jax
import jax.numpy as jnp
from jax import lax
from jax.experimental import pallas as pl
from jax.experimental.pallas import tpu as pltpu

F32, BF16 = jnp.float32, jnp.bfloat16
S = jax.ShapeDtypeStruct

N_DEV = 8
D = 1024
D_PLE = 256
D_SSM = 1024
D_CONF = 1024
N_HEADS = 16
HEAD = 64
N_STATE = 128
CHUNK = 128
K_SSM = 4
K_CONF = 31
D_IN = 5648
COLS_PER_DEV = D_IN // N_DEV
LN_EPS = 1e-5
RMS_EPS = 1e-5
ALPHA = 2.0 ** 0.25
LANES = 128
TB = 256
RG = 32
HALO_SSM = 8
HALO_CONF = 32
VMEM_LIMIT = 56 * 1024 * 1024

ADAM_LR, ADAM_B1, ADAM_B2, ADAM_EPS, ADAM_WD, ADAM_STEP = 0.001, 0.9, 0.999, 1e-08, 0.01, 10

C_GLUA, C_GLUB, C_XS, C_Z, C_CG = 0, 1, 2, 3, 4
C_BC = 10
N_MAIN = 5632


def _params(sem, vmem=VMEM_LIMIT):
    return pltpu.CompilerParams(dimension_semantics=sem, vmem_limit_bytes=vmem)


def _row(tb, n, col=0):
    return pl.BlockSpec((tb, n), lambda i: (i, col))


def _full(shape):
    return pl.BlockSpec(shape, lambda i: (0,) * len(shape))


_ANY = pl.BlockSpec(memory_space=pl.ANY)


def _prev(tb, halo, n, col=0):
    r = tb // halo
    return pl.BlockSpec((halo, n), lambda i: (jnp.maximum(i * r - 1, 0), col))


def _next(tb, halo, n, nt, col=0):
    r = tb // halo
    return pl.BlockSpec((halo, n), lambda i: (jnp.minimum((i + 1) * r, nt * r - 1), col))


def _row_loop(tb, fn):
    def it(r, c):
        fn(pl.ds(pl.multiple_of(r * RG, RG), RG))
        return c
    lax.fori_loop(0, tb // RG, it, 0)


def _col_loop(n, fn):
    def it(j, c):
        fn(pl.ds(pl.multiple_of(j * LANES, LANES), LANES))
        return c
    lax.fori_loop(0, n // LANES, it, 0)


def _sigmoid(x):
    return 1.0 / (1.0 + jnp.exp(-x))


def _dsilu(x, s):
    return s * (1.0 + x * (1.0 - s))


def _ln_stats(v):
    mu = jnp.mean(v, axis=-1, keepdims=True)
    c = v - mu
    r = lax.rsqrt(jnp.mean(c * c, axis=-1, keepdims=True) + LN_EPS)
    return c * r, r


def _ln_bwd(dy, xhat, r, g):
    dxh = dy * g
    dv = r * (dxh - jnp.mean(dxh, axis=-1, keepdims=True) - xhat * jnp.mean(dxh * xhat, axis=-1, keepdims=True))
    return dv, jnp.sum(dy * xhat, axis=0, keepdims=True), jnp.sum(dy, axis=0, keepdims=True)


def _dot(a, b, dims=((1,), (0,))):
    return lax.dot_general(a.astype(BF16), b.astype(BF16), (dims, ((), ())), preferred_element_type=F32)


_NT = ((1,), (1,))
_TN = ((0,), (0,))


def _split3(x):
    hi = x.astype(BF16)
    r = x - hi.astype(F32)
    mid = r.astype(BF16)
    return hi, mid, (r - mid.astype(F32)).astype(BF16)


def _dot_sel_b(a, b, dims=((1,), (0,))):
    hi, mid, lo = _split3(a)
    return (_dot(lo, b, dims) + _dot(mid, b, dims)) + _dot(hi, b, dims)


def _dot_sel_a(a, b, dims=((1,), (0,))):
    hi, mid, lo = _split3(b)
    return (_dot(a, lo, dims) + _dot(a, mid, dims)) + _dot(a, hi, dims)


def _mm(a, b, mode, name, out_dtype=F32, add=None, tm=1024, tn=512, tk=1024, exchange=None):
    if mode == "nn":
        (M, K), N = a.shape, b.shape[1]
    elif mode == "tn":
        (K, M), N = a.shape, b.shape[1]
    else:
        (M, K), N = a.shape, b.shape[0]
    tm, tn, tk = min(tm, M), min(tn, N), min(tk, K)
    assert M % tm == 0 and N % tn == 0 and K % tk == 0, (name, M, N, K)
    grid = (M // tm, N // tn, K // tk)
    nk = grid[2]
    dims = {"nn": ((1,), (0,)), "tn": _TN, "nt": _NT}[mode]
    n_in = 2 + (add is not None)
    xbufs, kinds = exchange if exchange is not None else ((), ())
    nx = len(xbufs)

    def body(*refs):
        a_ref, b_ref = refs[:2]
        o_ref = refs[n_in + nx]
        acc = refs[n_in + 2 * nx + 1]
        i, j, k = pl.program_id(0), pl.program_id(1), pl.program_id(2)
        if nx:
            copies = _exchange_copies(refs[n_in:n_in + nx], refs[n_in + nx + 1:n_in + 2 * nx + 1], kinds,
                                      *refs[n_in + 2 * nx + 2:])

            @pl.when((i == 0) & (j == 0) & (k == 0))
            def _():
                for cp in copies:
                    cp.start()

        @pl.when(k == 0)
        def _():
            acc[...] = jnp.zeros_like(acc)

        acc[...] += _dot(a_ref[...], b_ref[...], dims)

        @pl.when(k == nk - 1)
        def _():
            r = acc[...]
            if add is not None:
                r = r + refs[2][...]
            o_ref[...] = r.astype(out_dtype)

        if nx:
            @pl.when((i == grid[0] - 1) & (j == grid[1] - 1) & (k == nk - 1))
            def _():
                for cp in copies:
                    cp.wait()

    a_spec = pl.BlockSpec((tk, tm), lambda i, j, k: (k, i)) if mode == "tn" else pl.BlockSpec((tm, tk), lambda i, j, k: (i, k))
    b_spec = pl.BlockSpec((tn, tk), lambda i, j, k: (j, k)) if mode == "nt" else pl.BlockSpec((tk, tn), lambda i, j, k: (k, j))
    o_spec = pl.BlockSpec((tm, tn), lambda i, j, k: (i, j))
    ins, specs = [a, b], [a_spec, b_spec]
    if add is not None:
        ins.append(add)
        specs.append(o_spec)
    if not nx:
        return pl.pallas_call(
            body, name=name, grid=grid, in_specs=specs, out_specs=o_spec,
            out_shape=S((M, N), out_dtype), scratch_shapes=[pltpu.VMEM((tm, tn), F32)],
            compiler_params=_params(("parallel", "parallel", "arbitrary")))(*ins)
    return pl.pallas_call(
        body, name=name, grid=grid, in_specs=specs + [_ANY] * nx, out_specs=[o_spec] + [_ANY] * nx,
        out_shape=[S((M, N), out_dtype)] + _exchange_shapes(xbufs, kinds),
        scratch_shapes=[pltpu.VMEM((tm, tn), F32)] + _exchange_sems(nx),
        compiler_params=_params(("arbitrary", "arbitrary", "arbitrary")))(*ins, *xbufs)


def _ln_emb_fwd(x, g, b):
    T = x.shape[0]

    def body(x_ref, g_ref, b_ref, h_ref, hb_ref):
        def rows(rs):
            xh, _ = _ln_stats(x_ref[rs, :])
            h = xh * g_ref[...] + b_ref[...]
            h_ref[rs, :] = h
            hb_ref[rs, :] = h.astype(BF16)
        _row_loop(TB, rows)

    return pl.pallas_call(
        body, name="ln_emb_fwd", grid=(T // TB,),
        in_specs=[_row(TB, D), _full((1, D)), _full((1, D))], out_specs=[_row(TB, D), _row(TB, D)],
        out_shape=[S((T, D), F32), S((T, D), BF16)], compiler_params=_params(("parallel",)))(x, g, b)


def _post1_fwd(h0, out, b_out, g, b):
    T = h0.shape[0]

    def body(h0_ref, out_ref, bo_ref, g_ref, b_ref, h_ref, hb_ref):
        def rows(rs):
            xh, _ = _ln_stats(ALPHA * h0_ref[rs, :] + out_ref[rs, :] + bo_ref[...])
            h = xh * g_ref[...] + b_ref[...]
            h_ref[rs, :] = h
            hb_ref[rs, :] = h.astype(BF16)
        _row_loop(TB, rows)

    return pl.pallas_call(
        body, name="post1_fwd", grid=(T // TB,),
        in_specs=[_row(TB, D), _row(TB, D)] + [_full((1, D))] * 3, out_specs=[_row(TB, D), _row(TB, D)],
        out_shape=[S((T, D), F32), S((T, D), BF16)], compiler_params=_params(("parallel",)))(h0, out, b_out, g, b)


def _post2(h1, gpre, ple, tgt, g, b):
    T = h1.shape[0]

    def body(h1_ref, gp_ref, ple_ref, tgt_ref, g_ref, b_ref, dh1_ref, dgp_ref, dple_ref, loss_ref, dg_ref, db_ref):
        @pl.when(pl.program_id(0) == 0)
        def _():
            loss_ref[...] = jnp.zeros_like(loss_ref)
            dg_ref[...] = jnp.zeros_like(dg_ref)
            db_ref[...] = jnp.zeros_like(db_ref)

        def rows(rs):
            gate = _sigmoid(gp_ref[rs, :])
            ple = ple_ref[rs, :]
            xh, r = _ln_stats(ALPHA * h1_ref[rs, :] + gate * ple)
            err = xh * g_ref[...] + b_ref[...] - tgt_ref[rs, :]
            loss_ref[...] += 0.5 * jnp.sum(jnp.mean(err * err, axis=-1, keepdims=True), axis=0, keepdims=True)
            dv, dg, db = _ln_bwd(err * (1.0 / D), xh, r, g_ref[...])
            dg_ref[...] += dg
            db_ref[...] += db
            dh1_ref[rs, :] = ALPHA * dv
            dgp_ref[rs, :] = (dv * ple * gate * (1.0 - gate)).astype(BF16)
            dple_ref[rs, :] = (dv * gate).astype(BF16)
        _row_loop(TB, rows)

    return pl.pallas_call(
        body, name="post2", grid=(T // TB,),
        in_specs=[_row(TB, D)] * 4 + [_full((1, D))] * 2,
        out_specs=[_row(TB, D)] * 3 + [_full((8, LANES)), _full((1, D)), _full((1, D))],
        out_shape=[S((T, D), F32), S((T, D), BF16), S((T, D), BF16), S((8, LANES), F32), S((1, D), F32), S((1, D), F32)],
        compiler_params=_params(("arbitrary",)))(h1, gpre, ple, tgt, g, b)


def _post1_bwd(dh1a, dh1b, h0, out, b_out, g):
    T = h0.shape[0]

    def body(da_ref, db2_ref, h0_ref, out_ref, bo_ref, g_ref, dout_ref, dh0_ref, dg_ref, db_ref, dbo_ref):
        @pl.when(pl.program_id(0) == 0)
        def _():
            dg_ref[...] = jnp.zeros_like(dg_ref)
            db_ref[...] = jnp.zeros_like(db_ref)
            dbo_ref[...] = jnp.zeros_like(dbo_ref)

        def rows(rs):
            xh, r = _ln_stats(ALPHA * h0_ref[rs, :] + out_ref[rs, :] + bo_ref[...])
            dv, dg, db = _ln_bwd(da_ref[rs, :] + db2_ref[rs, :], xh, r, g_ref[...])
            dg_ref[...] += dg
            db_ref[...] += db
            dbo_ref[...] += jnp.sum(dv, axis=0, keepdims=True)
            dout_ref[rs, :] = dv.astype(BF16)
            dh0_ref[rs, :] = ALPHA * dv
        _row_loop(TB, rows)

    return pl.pallas_call(
        body, name="post1_bwd", grid=(T // TB,),
        in_specs=[_row(TB, D)] * 4 + [_full((1, D))] * 2,
        out_specs=[_row(TB, D)] * 2 + [_full((1, D))] * 3,
        out_shape=[S((T, D), BF16), S((T, D), F32)] + [S((1, D), F32)] * 3,
        compiler_params=_params(("arbitrary",)))(dh1a, dh1b, h0, out, b_out, g)


def _d_h0_ln_bwd(dproj, w_main, ddtr, w_dt, dh0a, x, g, exchange=None, tm=1024, tk=1408):
    T, K = dproj.shape
    tm = min(tm, T)
    assert T % tm == 0 and K % tk == 0
    ni, nk = T // tm, K // tk
    xbufs, kinds = exchange if exchange is not None else ((), ())
    nx = len(xbufs)

    def body(*refs):
        dp_ref, w_ref, dt_ref, wdt_ref, da_ref, x_ref, g_ref = refs[:7]
        dx_ref, dg_ref, db_ref = refs[7 + nx:10 + nx]
        acc = refs[10 + 2 * nx]
        i, k = pl.program_id(0), pl.program_id(1)
        if nx:
            copies = _exchange_copies(refs[7:7 + nx], refs[10 + nx:10 + 2 * nx], kinds, *refs[11 + 2 * nx:])

            @pl.when((i == 0) & (k == 0))
            def _():
                for cp in copies:
                    cp.start()

        @pl.when((i == 0) & (k == 0))
        def _():
            dg_ref[...] = jnp.zeros_like(dg_ref)
            db_ref[...] = jnp.zeros_like(db_ref)

        @pl.when(k == 0)
        def _():
            acc[...] = da_ref[...] + _dot(dt_ref[...], wdt_ref[...], _NT)

        acc[...] += _dot(dp_ref[...], w_ref[...], _NT)

        @pl.when(k == nk - 1)
        def _():
            def rows(rs):
                xh, r = _ln_stats(x_ref[rs, :])
                dv, dg, db = _ln_bwd(acc[rs, :], xh, r, g_ref[...])
                dg_ref[...] += dg
                db_ref[...] += db
                dx_ref[rs, :] = dv
            _row_loop(tm, rows)

        if nx:
            @pl.when((i == ni - 1) & (k == nk - 1))
            def _():
                for cp in copies:
                    cp.wait()

    rowt = lambda n: pl.BlockSpec((tm, n), lambda i, k: (i, 0))
    const = lambda shape: pl.BlockSpec(shape, lambda i, k: (0, 0))
    return pl.pallas_call(
        body, name="d_h0_ln_bwd", grid=(ni, nk),
        in_specs=[pl.BlockSpec((tm, tk), lambda i, k: (i, k)), pl.BlockSpec((D, tk), lambda i, k: (0, k)),
                  rowt(LANES), const((D, LANES)), rowt(D), rowt(D), const((1, D))] + [_ANY] * nx,
        out_specs=[rowt(D), const((1, D)), const((1, D))] + [_ANY] * nx,
        out_shape=[S((T, D), F32), S((1, D), F32), S((1, D), F32)] + _exchange_shapes(xbufs, kinds),
        scratch_shapes=[pltpu.VMEM((tm, D), F32)] + (_exchange_sems(nx) if nx else []),
        compiler_params=_params(("arbitrary", "arbitrary")))(dproj, w_main, ddtr, w_dt, dh0a, x, g, *xbufs)


def _softplus(x):
    return jnp.maximum(x, 0.0) + jnp.log1p(jnp.exp(-jnp.abs(x)))


def _ssd_pre_fwd(proj, dt_raw, wx, wb, bx, bb, dt_bias):
    T = proj.shape[0]
    H = HALO_SSM

    def body(xs_ref, xsp_ref, bc_ref, bcp_ref, dtr_ref, wx_ref, wb_ref, bx_ref, bb_ref, dtb_ref,
             xso_ref, bco_ref, dto_ref, extx, extb):
        first = pl.program_id(0) == 0

        def conv(t_ref, p_ref, w_ref, b_ref, o_ref, ext, n):
            def blk(cols):
                ext[0:H, cols] = jnp.where(first, 0.0, p_ref[:, cols])
                ext[H:, cols] = t_ref[:, cols]
                for r0 in range(0, TB, 64):
                    acc = jnp.broadcast_to(b_ref[:, cols], (64, LANES))
                    for k in range(K_SSM):
                        acc = acc + w_ref[k:k + 1, cols] * ext[pl.ds(r0 + H - (K_SSM - 1) + k, 64), cols]
                    o_ref[pl.ds(r0, 64), cols] = acc * _sigmoid(acc)
            _col_loop(n, blk)

        conv(xs_ref, xsp_ref, wx_ref, bx_ref, xso_ref, extx, D_SSM)
        conv(bc_ref, bcp_ref, wb_ref, bb_ref, bco_ref, extb, 512)
        dto_ref[...] = _softplus(dtr_ref[...] + dtb_ref[...])

    return pl.pallas_call(
        body, name="ssd_pre_fwd", grid=(T // TB,),
        in_specs=[_row(TB, 1024, C_XS), _prev(TB, H, 1024, C_XS), _row(TB, 512, C_BC), _prev(TB, H, 512, C_BC),
                  _row(TB, LANES), _full((K_SSM, 1024)), _full((K_SSM, 512)), _full((1, 1024)), _full((1, 512)),
                  _full((1, LANES))],
        out_specs=[_row(TB, 1024), _row(TB, 512), _row(TB, LANES)],
        out_shape=[S((T, 1024), F32), S((T, 512), F32), S((T, LANES), F32)],
        scratch_shapes=[pltpu.VMEM((H + TB, 1024), F32), pltpu.VMEM((H + TB, 512), F32)],
        compiler_params=_params(("parallel",)))(proj, proj, proj, proj, dt_raw, wx, wb, bx, bb, dt_bias)


def _ssd_conv_bwd(dproj, proj, d_c, w, b, n, col, name):
    T = proj.shape[0]
    nt = T // TB
    H = HALO_SSM
    R = TB + H

    def body(dproj_ref, t_ref, p_ref, n_ref, d_ref, dn_ref, w_ref, b_ref, o_ref, dw_ref, dbias_ref, ext, dp):
        i = pl.program_id(0)
        first, last = i == 0, i == nt - 1

        @pl.when(first)
        def _():
            dw_ref[...] = jnp.zeros_like(dw_ref)
            dbias_ref[...] = jnp.zeros_like(dbias_ref)

        def blk(cols):
            ext[0:H, cols] = jnp.where(first, 0.0, p_ref[:, cols])
            ext[H:H + TB, cols] = t_ref[:, cols]
            ext[H + TB:, cols] = n_ref[:, cols]
            pre = jnp.broadcast_to(b_ref[:, cols], (R, LANES))
            for k in range(K_SSM):
                pre = pre + w_ref[k:k + 1, cols] * ext[pl.ds(H - (K_SSM - 1) + k, R), cols]
            s = _sigmoid(pre)
            ds = _dsilu(pre, s)
            dp[0:TB, cols] = d_ref[:, cols] * ds[0:TB]
            dp[TB:, cols] = jnp.where(last, 0.0, dn_ref[:, cols] * ds[TB:])
            dpt = dp[0:TB, cols]
            dbias_ref[:, cols] += jnp.sum(dpt, axis=0, keepdims=True)
            acc = jnp.zeros((TB, LANES), F32)
            for k in range(K_SSM):
                dw_ref[k:k + 1, cols] += jnp.sum(dpt * ext[pl.ds(H - (K_SSM - 1) + k, TB), cols], axis=0, keepdims=True)
                acc = acc + w_ref[k:k + 1, cols] * dp[pl.ds(K_SSM - 1 - k, TB), cols]
            o_ref[:, cols] = acc.astype(BF16)
        _col_loop(n, blk)

    return pl.pallas_call(
        body, name=name, grid=(nt,),
        in_specs=[_ANY, _row(TB, n, col), _prev(TB, H, n, col), _next(TB, H, n, nt, col),
                  _row(TB, n), _next(TB, H, n, nt), _full((K_SSM, n)), _full((1, n))],
        out_specs=[_row(TB, n, col), _full((K_SSM, n)), _full((1, n))],
        out_shape=[S(dproj.shape, BF16), S((K_SSM, n), F32), S((1, n), F32)],
        input_output_aliases={0: 0},
        scratch_shapes=[pltpu.VMEM((H + TB + H, n), F32), pltpu.VMEM((R, n), F32)],
        compiler_params=_params(("arbitrary",)))(dproj, proj, proj, proj, d_c, d_c, w, b)


def _dt_bwd(ddt, dt_raw, dt_bias):
    T = ddt.shape[0]

    def body(ddt_ref, dtr_ref, dtb_ref, o_ref, db_ref):
        @pl.when(pl.program_id(0) == 0)
        def _():
            db_ref[...] = jnp.zeros_like(db_ref)

        g = ddt_ref[...] * _sigmoid(dtr_ref[...] + dtb_ref[...])
        o_ref[...] = g.astype(BF16)
        db_ref[...] += jnp.sum(g, axis=0, keepdims=True)

    return pl.pallas_call(
        body, name="dt_bwd", grid=(T // TB,),
        in_specs=[_row(TB, LANES), _row(TB, LANES), _full((1, LANES))], out_specs=[_row(TB, LANES), _full((1, LANES))],
        out_shape=[S((T, LANES), BF16), S((1, LANES), F32)],
        compiler_params=_params(("arbitrary",)))(ddt, dt_raw, dt_bias)


def _ssd_consts():
    ex = np.zeros((LANES, D_SSM), np.float32)
    for h in range(N_HEADS):
        ex[h, h * HEAD:(h + 1) * HEAD] = 1.0
    tri = np.tril(np.ones((CHUNK, CHUNK), np.float32))
    return jnp.asarray(ex), jnp.asarray(ex.T.copy()), jnp.asarray(tri), jnp.asarray(tri.T.copy())


def _ssd_common(xs, dt, alog_ref, ex_ref, tri_ref):
    lane = lax.broadcasted_iota(jnp.int32, (1, LANES), 1)
    a = jnp.where(lane < N_HEADS, -jnp.exp(alog_ref[...]), 0.0)
    A = _dot_sel_a(tri_ref[...], dt * a)
    ex = ex_ref[...]
    Aex = _dot_sel_b(A, ex)
    dtex = _dot_sel_b(dt, ex)
    expA = jnp.exp(Aex)
    dec = jnp.exp(Aex[CHUNK - 1:CHUNK, :] - Aex)
    cd = _dot_sel_a(ex, jnp.broadcast_to(jnp.exp(A.T[:, CHUNK - 1:CHUNK]), (LANES, LANES)), _TN)
    return a, A, dtex, expA, dec, cd


def _decay_mask():
    sub = lax.broadcasted_iota(jnp.int32, (CHUNK, CHUNK), 0)
    lane = lax.broadcasted_iota(jnp.int32, (CHUNK, CHUNK), 1)
    return sub, lane, sub >= lane


def _ssd_fwd(xs_c, bc_c, dt, proj, alog, dskip_row, norm_g):
    T = xs_c.shape[0]
    nc = T // CHUNK
    ex, _, tri, _ = _ssd_consts()

    def body(xs_ref, bc_ref, dt_ref, z_ref, alog_ref, dsk_ref, ng_ref, ex_ref, tri_ref,
             ys_ref, ypre_ref, hprev_ref, Hs, ybuf):
        @pl.when(pl.program_id(0) == 0)
        def _():
            Hs[...] = jnp.zeros_like(Hs)

        hprev_ref[0] = Hs[...]
        xs, dt = xs_ref[...], dt_ref[...]
        a, A, dtex, expA, dec, cd = _ssd_common(xs, dt, alog_ref, ex_ref, tri_ref)
        AT = A.T
        xdt = xs * dtex
        xdec = xdt * dec
        _, _, causal = _decay_mask()
        for g in range(2):
            gs = slice(g * 512, (g + 1) * 512)
            B = bc_ref[:, g * N_STATE:(g + 1) * N_STATE]
            C = bc_ref[:, 256 + g * N_STATE:256 + (g + 1) * N_STATE]
            cb = _dot(C, B, _NT)
            Hg = Hs[gs, :]
            yoff = _dot(C, Hg, _NT) * expA[:, gs]
            for j in range(8):
                h = g * 8 + j
                hs = slice(h * HEAD, (h + 1) * HEAD)
                L = jnp.exp(jnp.where(causal, A[:, h:h + 1] - AT[h:h + 1, :], -1e30))
                ybuf[:, hs] = _dot(cb * L, xdt[:, hs]) + yoff[:, j * HEAD:(j + 1) * HEAD]
            Hs[gs, :] = cd[gs, :] * Hg + _dot(xdec[:, gs], B, _TN)
        ypre = ybuf[...] + dsk_ref[...] * xs
        ypre_ref[...] = ypre
        z = z_ref[...]
        yz = ypre * (z * _sigmoid(z))
        for g in range(2):
            gs = slice(g * 512, (g + 1) * 512)
            v = yz[:, gs]
            r = lax.rsqrt(jnp.mean(v * v, axis=-1, keepdims=True) + RMS_EPS)
            ys_ref[:, gs] = (v * r * ng_ref[:, gs]).astype(BF16)

    return pl.pallas_call(
        body, name="ssd_fwd", grid=(nc,),
        in_specs=[_row(CHUNK, 1024), _row(CHUNK, 512), _row(CHUNK, LANES), _row(CHUNK, 1024, C_Z),
                  _full((1, LANES)), _full((1, 1024)), _full((1, 1024)), _full((LANES, 1024)), _full((CHUNK, CHUNK))],
        out_specs=[_row(CHUNK, 1024), _row(CHUNK, 1024), pl.BlockSpec((1, 1024, N_STATE), lambda c: (c, 0, 0))],
        out_shape=[S((T, 2048), BF16), S((T, 1024), F32), S((nc, 1024, N_STATE), F32)],
        scratch_shapes=[pltpu.VMEM((1024, N_STATE), F32), pltpu.VMEM((CHUNK, 1024), F32)],
        compiler_params=_params(("arbitrary",)))(xs_c, bc_c, dt, proj, alog, dskip_row, norm_g, ex, tri)


def _ssd_bwd(dproj, xs_c, bc_c, dt, proj, ypre, hprev, dmix, alog, dskip_row, norm_g):
    T = xs_c.shape[0]
    nc = T // CHUNK
    ex, ext, tri, triu = _ssd_consts()
    rev = lambda n, col=0: pl.BlockSpec((CHUNK, n), lambda c: (nc - 1 - c, col))

    def body(dproj_ref, xs_ref, bc_ref, dt_ref, z_ref, ypre_ref, hprev_ref, dys_ref, alog_ref, dsk_ref, ng_ref,
             ex_ref, ext_ref, tri_ref, triu_ref,
             dxs_ref, dbc_ref, ddt_ref, dz_ref, dng_ref, ddsk_ref, dalog_ref, dHs, dxbuf, dskacc):
        c = pl.program_id(0)

        @pl.when(c == 0)
        def _():
            dHs[...] = jnp.zeros_like(dHs)
            dng_ref[...] = jnp.zeros_like(dng_ref)
            dalog_ref[...] = jnp.zeros_like(dalog_ref)
            dskacc[...] = jnp.zeros_like(dskacc)

        xs, dt, z, ypre, dys = xs_ref[...], dt_ref[...], z_ref[...], ypre_ref[...], dys_ref[...]
        sg = _sigmoid(z)
        sz = z * sg
        yz = ypre * sz
        dyz_parts = []
        for g in range(2):
            gs = slice(g * 512, (g + 1) * 512)
            v = yz[:, gs]
            r = lax.rsqrt(jnp.mean(v * v, axis=-1, keepdims=True) + RMS_EPS)
            vn = v * r
            dng_ref[:, gs] += jnp.sum(dys[:, gs] * vn, axis=0, keepdims=True)
            dvn = dys[:, gs] * ng_ref[:, gs]
            dyz_parts.append(r * (dvn - vn * jnp.mean(dvn * vn, axis=-1, keepdims=True)))
        dyz = jnp.concatenate(dyz_parts, axis=1)
        dy = dyz * sz
        dz_ref[...] = (dyz * ypre * _dsilu(z, sg)).astype(BF16)
        dskacc[...] += jnp.sum(dy * xs, axis=0, keepdims=True)

        a, A, dtex, expA, dec, cd = _ssd_common(xs, dt, alog_ref, ex_ref, tri_ref)
        AT = A.T
        xdt = xs * dtex
        xdec = xdt * dec
        dye = dy * expA
        H = hprev_ref[0]
        dHn = dHs[...]
        sub, lane, causal = _decay_mask()
        dAc = jnp.zeros((CHUNK, LANES), F32)
        Rm = jnp.zeros((CHUNK, LANES), F32)
        yoff_parts, q_parts = [], []
        for g in range(2):
            gs = slice(g * 512, (g + 1) * 512)
            B = bc_ref[:, g * N_STATE:(g + 1) * N_STATE]
            C = bc_ref[:, 256 + g * N_STATE:256 + (g + 1) * N_STATE]
            cb = _dot(C, B, _NT)
            Hg, dHg = H[gs, :], dHn[gs, :]
            Q = _dot(B, dHg, _NT)
            yoff_parts.append(_dot(C, Hg, _NT) * expA[:, gs])
            q_parts.append(Q)
            dcb = jnp.zeros((CHUNK, CHUNK), F32)
            for j in range(8):
                h = g * 8 + j
                hs = slice(h * HEAD, (h + 1) * HEAD)
                L = jnp.exp(jnp.where(causal, A[:, h:h + 1] - AT[h:h + 1, :], -1e30))
                M = cb * L
                G = _dot(dy[:, hs], xdt[:, hs], _NT)
                dxbuf[:, hs] = _dot(M, dy[:, hs], _TN)
                dcb = dcb + G * L
                E = G * M
                dAc = jnp.where(lane == h, jnp.sum(E, axis=1, keepdims=True), dAc)
                Rm = jnp.where(sub == h, jnp.sum(E, axis=0, keepdims=True), Rm)
            dbc_ref[:, g * N_STATE:(g + 1) * N_STATE] = _dot(dcb, C, _TN) + _dot(xdec[:, gs], dHg)
            dbc_ref[:, 256 + g * N_STATE:256 + (g + 1) * N_STATE] = _dot(dcb, B) + _dot(dye[:, gs], Hg)
            dHs[gs, :] = cd[gs, :] * dHg + _dot(dye[:, gs], C, _TN)
        yoff = jnp.concatenate(yoff_parts, axis=1)
        Qd = jnp.concatenate(q_parts, axis=1) * dec
        dxdt = dxbuf[...] + Qd
        extm = ext_ref[...]
        red_s = _dot_sel_b(xdt * Qd, extm)
        dA = dAc - Rm.T + _dot_sel_b(dy * yoff, extm) - red_s
        hd = jnp.sum(_dot_sel_b(H * dHn, extm, _TN), axis=0, keepdims=True)
        last_add = jnp.sum(red_s, axis=0, keepdims=True) + jnp.exp(A[CHUNK - 1:CHUNK, :]) * hd
        dA = dA + jnp.where(sub == CHUNK - 1, last_add, 0.0)
        dadt = _dot_sel_a(triu_ref[...], dA)
        ddt_ref[...] = dadt * a + _dot_sel_b(dxdt * xs, extm)
        dalog_ref[...] += jnp.sum(dadt * dt, axis=0, keepdims=True) * a
        dxs_ref[...] = dxdt * dtex + dsk_ref[...] * dy

        @pl.when(c == nc - 1)
        def _():
            ddsk_ref[...] = _dot_sel_b(jnp.broadcast_to(dskacc[...], (8, 1024)), extm)[0:1, :]

    return pl.pallas_call(
        body, name="ssd_bwd", grid=(nc,),
        in_specs=[_ANY, rev(1024), rev(512), rev(LANES), rev(1024, C_Z), rev(1024),
                  pl.BlockSpec((1, 1024, N_STATE), lambda c: (nc - 1 - c, 0, 0)), rev(1024, 0),
                  _full((1, LANES)), _full((1, 1024)), _full((1, 1024)),
                  _full((LANES, 1024)), _full((1024, LANES)), _full((CHUNK, CHUNK)), _full((CHUNK, CHUNK))],
        out_specs=[rev(1024), rev(512), rev(LANES), rev(1024, C_Z), _full((1, 1024)), _full((1, LANES)), _full((1, LANES))],
        out_shape=[S((T, 1024), F32), S((T, 512), F32), S((T, LANES), F32), S(dproj.shape, BF16),
                   S((1, 1024), F32), S((1, LANES), F32), S((1, LANES), F32)],
        input_output_aliases={0: 3},
        scratch_shapes=[pltpu.VMEM((1024, N_STATE), F32), pltpu.VMEM((CHUNK, 1024), F32), pltpu.VMEM((1, 1024), F32)],
        compiler_params=_params(("arbitrary",)))(
            dproj, xs_c, bc_c, dt, proj, ypre, hprev, dmix, alog, dskip_row, norm_g, ex, ext, tri, triu)


def _shifted_copies(ext, ext8):
    n = ext8.shape[1]
    for r in range(8):
        ext8[r] = ext[pl.ds(r, n), :]


def _shifted(ext8, off, rows):
    return ext8[off % 8, pl.ds(off - off % 8, rows), :]


def _conf_fwd(mix, proj, w, cb, lg, lb, ba, bb):
    T = proj.shape[0]
    H = HALO_CONF

    def body(mix_ref, ga_ref, gap_ref, gb_ref, gbp_ref, cg_ref, w_ref, cb_ref, lg_ref, lb_ref, ba_ref, bb_ref,
             u1_ref, yc_ref, ext, ext8):
        first = pl.program_id(0) == 0
        ext[H + TB:, :] = jnp.zeros((8, LANES), F32)

        def blk(cols):
            up = (gap_ref[:, cols] + ba_ref[:, cols]) * _sigmoid(gbp_ref[:, cols] + bb_ref[:, cols])
            ext[0:H, :] = jnp.where(first, 0.0, up)
            ext[H:H + TB, :] = (ga_ref[:, cols] + ba_ref[:, cols]) * _sigmoid(gb_ref[:, cols] + bb_ref[:, cols])
            _shifted_copies(ext, ext8)
            for r0 in range(0, TB, 64):
                acc = jnp.broadcast_to(cb_ref[:, cols], (64, LANES))
                for k in range(K_CONF):
                    acc = acc + w_ref[k:k + 1, cols] * _shifted(ext8, r0 + H - (K_CONF - 1) + k, 64)
                u1_ref[pl.ds(r0, 64), cols] = acc
        _col_loop(D_CONF, blk)

        def rows(rs):
            xh, _ = _ln_stats(u1_ref[rs, :])
            u2 = xh * lg_ref[...] + lb_ref[...]
            cg = cg_ref[rs, :]
            yc_ref[rs, :] = (u2 * _sigmoid(u2) * cg * _sigmoid(cg)).astype(BF16)
        _row_loop(TB, rows)

    return pl.pallas_call(
        body, name="conf_fwd", grid=(T // TB,),
        in_specs=[_ANY, _row(TB, 1024, C_GLUA), _prev(TB, H, 1024, C_GLUA), _row(TB, 1024, C_GLUB),
                  _prev(TB, H, 1024, C_GLUB), _row(TB, 1024, C_CG), _full((K_CONF, 1024))] + [_full((1, 1024))] * 5,
        out_specs=[_row(TB, 1024), _row(TB, 1024, 1)],
        out_shape=[S((T, 1024), F32), S((T, 2048), BF16)],
        input_output_aliases={0: 1},
        scratch_shapes=[pltpu.VMEM((H + TB + 8, LANES), F32), pltpu.VMEM((8, H + TB, LANES), F32)],
        compiler_params=_params(("parallel",)))(mix, proj, proj, proj, proj, proj, w, cb, lg, lb, ba, bb)


def _conf_bwd1(dmix, u1, proj, lg, lb):
    T = u1.shape[0]

    def body(dy_ref, u1_ref, cg_ref, lg_ref, lb_ref, du1_ref, dcg_ref, dg_ref, db_ref):
        @pl.when(pl.program_id(0) == 0)
        def _():
            dg_ref[...] = jnp.zeros_like(dg_ref)
            db_ref[...] = jnp.zeros_like(db_ref)

        def rows(rs):
            xh, r = _ln_stats(u1_ref[rs, :])
            u2 = xh * lg_ref[...] + lb_ref[...]
            s2 = _sigmoid(u2)
            cg = cg_ref[rs, :]
            sc = _sigmoid(cg)
            dy = dy_ref[rs, :]
            dcg_ref[rs, :] = (dy * u2 * s2 * _dsilu(cg, sc)).astype(BF16)
            dv, dg, db = _ln_bwd(dy * cg * sc * _dsilu(u2, s2), xh, r, lg_ref[...])
            dg_ref[...] += dg
            db_ref[...] += db
            du1_ref[rs, :] = dv
        _row_loop(TB, rows)

    return pl.pallas_call(
        body, name="conf_bwd1", grid=(T // TB,),
        in_specs=[_row(TB, 1024, 1), _row(TB, 1024), _row(TB, 1024, C_CG), _full((1, 1024)), _full((1, 1024))],
        out_specs=[_row(TB, 1024), _row(TB, 1024, C_CG), _full((1, 1024)), _full((1, 1024))],
        out_shape=[S((T, 1024), F32), S((T, N_MAIN), BF16), S((1, 1024), F32), S((1, 1024), F32)],
        compiler_params=_params(("arbitrary",)))(dmix, u1, proj, lg, lb)


def _conf_bwd2(dproj, proj, du1, w, ba, bb):
    T = du1.shape[0]
    nt = T // TB
    H = HALO_CONF

    def body(dproj_ref, ga_ref, gap_ref, gb_ref, gbp_ref, du_ref, dun_ref, w_ref, ba_ref, bb_ref,
             dg_ref, dw_ref, dcb_ref, dba_ref, dbb_ref, ext, dext, ext8, dext8, dwacc):
        i = pl.program_id(0)
        first, last = i == 0, i == nt - 1

        @pl.when(first)
        def _():
            for r in (dcb_ref, dba_ref, dbb_ref, dwacc):
                r[...] = jnp.zeros_like(r)

        ext[H + TB:, :] = jnp.zeros((8, LANES), F32)
        dext[H + TB:, :] = jnp.zeros((8, LANES), F32)

        def blk(cols):
            cols_b = pl.ds(pl.multiple_of(cols.start + D_CONF, LANES), LANES)
            up = (gap_ref[:, cols] + ba_ref[:, cols]) * _sigmoid(gbp_ref[:, cols] + bb_ref[:, cols])
            ext[0:H, :] = jnp.where(first, 0.0, up)
            a = ga_ref[:, cols] + ba_ref[:, cols]
            sb = _sigmoid(gb_ref[:, cols] + bb_ref[:, cols])
            ext[H:H + TB, :] = a * sb
            du = du_ref[:, cols]
            dext[0:TB, :] = du
            dext[TB:TB + H, :] = jnp.where(last, 0.0, dun_ref[:, cols])
            _shifted_copies(ext, ext8)
            _shifted_copies(dext, dext8)
            dcb_ref[:, cols] += jnp.sum(du, axis=0, keepdims=True)
            for r0 in range(0, TB, 64):
                dur = du_ref[pl.ds(r0, 64), cols]
                acc = jnp.zeros((64, LANES), F32)
                for k in range(K_CONF):
                    prod = dur * _shifted(ext8, r0 + H - (K_CONF - 1) + k, 64)
                    dwacc[k * 8:(k + 1) * 8, cols] += prod.reshape(8, 8, LANES).sum(axis=0)
                    acc = acc + w_ref[k:k + 1, cols] * _shifted(dext8, r0 + K_CONF - 1 - k, 64)
                ar, sr = a[r0:r0 + 64], sb[r0:r0 + 64]
                da = acc * sr
                dbv = acc * ar * sr * (1.0 - sr)
                dg_ref[pl.ds(r0, 64), cols] = da.astype(BF16)
                dg_ref[pl.ds(r0, 64), cols_b] = dbv.astype(BF16)
                dba_ref[:, cols] += jnp.sum(da, axis=0, keepdims=True)
                dbb_ref[:, cols] += jnp.sum(dbv, axis=0, keepdims=True)
        _col_loop(D_CONF, blk)

        @pl.when(last)
        def _():
            dw_ref[...] = jnp.sum(dwacc[...].reshape(K_CONF, 8, D_CONF), axis=1)

    return pl.pallas_call(
        body, name="conf_bwd2", grid=(nt,),
        in_specs=[_ANY, _row(TB, 1024, C_GLUA), _prev(TB, H, 1024, C_GLUA), _row(TB, 1024, C_GLUB),
                  _prev(TB, H, 1024, C_GLUB), _row(TB, 1024), _next(TB, H, 1024, nt), _full((K_CONF, 1024)),
                  _full((1, 1024)), _full((1, 1024))],
        out_specs=[_row(TB, 2048), _full((K_CONF, 1024)), _full((1, 1024)), _full((1, 1024)), _full((1, 1024))],
        out_shape=[S(dproj.shape, BF16), S((K_CONF, 1024), F32)] + [S((1, 1024), F32)] * 3,
        input_output_aliases={0: 0},
        scratch_shapes=[pltpu.VMEM((H + TB + 8, LANES), F32), pltpu.VMEM((TB + H + 8, LANES), F32),
                        pltpu.VMEM((8, H + TB, LANES), F32), pltpu.VMEM((8, TB + H, LANES), F32),
                        pltpu.VMEM((K_CONF * 8, D_CONF), F32)],
        compiler_params=_params(("arbitrary",)))(dproj, proj, proj, proj, proj, du1, du1, w, ba, bb)


def _mesh_pos():
    x, y, c = lax.axis_index("x"), lax.axis_index("y"), lax.axis_index("c")
    return x, y, c, 4 * x + 2 * y + c


def _peer(x, y, c, k):
    return (x ^ ((k >> 2) & 1), y ^ ((k >> 1) & 1), c ^ (k & 1))


def _exchange_copies(ins, outs, kinds, send, recv, loc):
    nb = len(ins)
    x, y, c, me = _mesh_pos()
    src = lambda b, d: ins[b].at[d] if kinds[b] == "blocks" else ins[b]
    copies = [pltpu.make_async_copy(src(b, me), outs[b].at[me], loc.at[b]) for b in range(nb)]
    for k in range(1, N_DEV):
        px, py, pc = _peer(x, y, c, k)
        for b in range(nb):
            s = (k - 1) * nb + b
            copies.append(pltpu.make_async_remote_copy(
                src_ref=src(b, 4 * px + 2 * py + pc), dst_ref=outs[b].at[me], send_sem=send.at[s], recv_sem=recv.at[s],
                device_id=(px, py, pc), device_id_type=pl.DeviceIdType.MESH))
    return copies


def _exchange_shapes(bufs, kinds):
    return [S(b.shape if kd == "blocks" else (N_DEV,) + b.shape, b.dtype) for b, kd in zip(bufs, kinds)]


def _exchange_sems(nb):
    n = (N_DEV - 1) * nb
    return [pltpu.SemaphoreType.DMA((n,)), pltpu.SemaphoreType.DMA((n,)), pltpu.SemaphoreType.DMA((nb,))]


def _exchange(bufs, kinds, name):
    nb = len(bufs)

    def body(*refs):
        copies = _exchange_copies(refs[:nb], refs[nb:2 * nb], kinds, *refs[2 * nb:])
        for cp in copies:
            cp.start()
        for cp in copies:
            cp.wait()

    return pl.pallas_call(
        body, name=name, in_specs=[_ANY] * nb, out_specs=[_ANY] * nb,
        out_shape=_exchange_shapes(bufs, kinds), scratch_shapes=_exchange_sems(nb))(*bufs)


def _sum_parts(p_ref):
    acc = p_ref[0].astype(F32)
    for d in range(1, N_DEV):
        acc = acc + p_ref[d].astype(F32)
    return acc


def _adamw_math(g, w, m, v):
    m = ADAM_B1 * m + (1.0 - ADAM_B1) * g
    v = ADAM_B2 * v + (1.0 - ADAM_B2) * (g * g)
    m_hat = m / (1.0 - ADAM_B1 ** ADAM_STEP)
    v_hat = v / (1.0 - ADAM_B2 ** ADAM_STEP)
    return -ADAM_LR * (m_hat / (jnp.sqrt(v_hat) + ADAM_EPS) + ADAM_WD * w), m, v


def _sum8_adamw(parts, w, m, v, name):
    _, R, C = parts.shape
    tb = 256 if R % 256 == 0 else R

    def body(p_ref, w_ref, m_ref, v_ref, g_ref, d_ref, mo_ref, vo_ref):
        g = _sum_parts(p_ref)
        g_ref[...] = g
        d_ref[...], mo_ref[...], vo_ref[...] = _adamw_math(g, w_ref[...], m_ref[...], v_ref[...])

    return pl.pallas_call(
        body, name=name, grid=(R // tb,),
        in_specs=[pl.BlockSpec((N_DEV, tb, C), lambda i: (0, i, 0))] + [_row(tb, C)] * 3, out_specs=[_row(tb, C)] * 4,
        out_shape=[S((R, C), F32)] * 4, compiler_params=_params(("parallel",)))(parts, w, m, v)


SMALL_LAYOUT = (
    ("ln_emb_g", 0, 1024), ("ln_emb_b", 0, 1024), ("ssm_conv_b", 0, 1024), ("ssm_conv_b", 1024, 512),
    ("dt_bias", 0, N_HEADS), ("a_log", 0, N_HEADS), ("d_skip", 0, N_HEADS), ("ssm_norm_g", 0, 1024),
    ("b_glu", 0, 1024), ("b_glu", 1024, 1024), ("conf_conv_b", 0, 1024), ("conf_ln_g", 0, 1024),
    ("conf_ln_b", 0, 1024), ("b_out", 0, 1024), ("ln1_g", 0, 1024), ("ln1_b", 0, 1024), ("ln2_g", 0, 1024),
    ("ln2_b", 0, 1024))
SMALL_ROWS = 24
SMALL = tuple(dict.fromkeys(n for n, _, _ in SMALL_LAYOUT))


def _pack_small(rows):
    def body(*refs):
        o_ref = refs[-1]
        o_ref[...] = jnp.zeros_like(o_ref)
        for r, ref in enumerate(refs[:-1]):
            o_ref[r:r + 1, 0:ref.shape[1]] = ref[...]

    return pl.pallas_call(body, name="pack_small", out_shape=S((SMALL_ROWS, 1024), F32))(*rows)


def _small_update(parts, w, m, v):
    def body(*refs):
        p_ref = refs[0]
        ins = {n: refs[1 + 3 * i:4 + 3 * i] for i, n in enumerate(SMALL)}
        o0 = 1 + 3 * len(SMALL)
        outs = {n: refs[o0 + 4 * i:o0 + 4 * i + 4] for i, n in enumerate(SMALL)}
        gsum = refs[-1]
        gsum[...] = _sum_parts(p_ref)
        for r, (n, off, wd) in enumerate(SMALL_LAYOUT):
            cs = slice(off, off + wd)
            g = gsum[r:r + 1, 0:wd]
            w_ref, m_ref, v_ref = ins[n]
            g_ref, d_ref, mo_ref, vo_ref = outs[n]
            g_ref[:, cs] = g
            d_ref[:, cs], mo_ref[:, cs], vo_ref[:, cs] = _adamw_math(g, w_ref[:, cs], m_ref[:, cs], v_ref[:, cs])

    args = [parts] + [a for n in SMALL for a in (w[n], m[n], v[n])]
    res = pl.pallas_call(
        body, name="small_update", out_shape=[S(w[n].shape, F32) for n in SMALL for _ in range(4)],
        scratch_shapes=[pltpu.VMEM((SMALL_ROWS, 1024), F32)])(*args)
    return tuple({n: res[4 * i + j] for i, n in enumerate(SMALL)} for j in range(4))


LATE = ("w_out", "w_ple_gate", "w_ple_proj")


def _local_step(x, p, tgt, W, late=None):
    r1 = lambda v: v.reshape(1, -1).astype(F32)
    pad_l = lambda v: jnp.pad(r1(v), ((0, 0), (0, LANES - v.size)))
    w_main, w_dt = _w_in_to_main(W["w_in"])
    scw, scb = W["ssm_conv_w"], r1(W["ssm_conv_b"])
    wx, wb, bx, bb = scw[:, :1024], scw[:, 1024:], scb[:, :1024], scb[:, 1024:]
    dt_bias, alog = pad_l(W["dt_bias"]), pad_l(W["a_log"])
    dskip_row = jnp.repeat(W["d_skip"].reshape(-1), HEAD).reshape(1, -1)
    norm_g = r1(W["ssm_norm_g"])
    bglu = r1(W["b_glu"])
    ba, bbg = bglu[:, :1024], bglu[:, 1024:]
    ccw, ccb, clg, clb = W["conf_conv_w"], r1(W["conf_conv_b"]), r1(W["conf_ln_g"]), r1(W["conf_ln_b"])

    h0, h0b = _ln_emb_fwd(x, r1(W["ln_emb_g"]), r1(W["ln_emb_b"]))
    if late is None:
        proj = _mm(h0b, w_main, "nn", "in_proj")
    else:
        proj, *gathered = _mm(h0b, w_main, "nn", "in_proj", exchange=(late, ("all",) * len(LATE)))
        W = dict(W, **{n: _unstack_shards(a, BY_COLS[n]) for n, a in zip(LATE, gathered)})
    dt_raw = _mm(h0b, w_dt, "nn", "in_proj_dt")
    xs_c, bc_c, dt = _ssd_pre_fwd(proj, dt_raw, wx, wb, bx, bb, dt_bias)
    mix, ypre, hprev = _ssd_fwd(xs_c, bc_c, dt, proj, alog, dskip_row, norm_g)
    u1, mix = _conf_fwd(mix, proj, ccw, ccb, clg, clb, ba, bbg)
    out = _mm(mix, W["w_out"], "nn", "out_proj")
    h1, h1b = _post1_fwd(h0, out, r1(W["b_out"]), r1(W["ln1_g"]), r1(W["ln1_b"]))
    gpre = _mm(h1b, W["w_ple_gate"], "nn", "ple_gate")
    pb = p.astype(BF16)
    ple = _mm(pb, W["w_ple_proj"], "nn", "ple_proj")
    dh1a, dgp, dple, loss, dln2g, dln2b = _post2(h1, gpre, ple, tgt, r1(W["ln2_g"]), r1(W["ln2_b"]))

    g = {}
    g["w_ple_proj"] = _mm(pb.T, dple, "nn", "d_ple_proj", out_dtype=BF16)
    g["w_ple_gate"] = _mm(h1b.T, dgp, "nn", "d_ple_gate", out_dtype=BF16)
    dh1b = _mm(dgp, W["w_ple_gate"], "nt", "d_h1")
    dout, dh0a, dln1g, dln1b, dbout = _post1_bwd(dh1a, dh1b, h0, out, r1(W["b_out"]), r1(W["ln1_g"]))
    g["w_out"] = _mm(mix.T, dout, "nn", "d_w_out", out_dtype=BF16)
    dmix = _mm(dout, W["w_out"], "nt", "d_mix")
    du1, dproj, dclg, dclb = _conf_bwd1(dmix, u1, proj, clg, clb)
    dproj, g["conf_conv_w"], dccb, dba, dbb = _conf_bwd2(dproj, proj, du1, ccw, ba, bbg)
    dxs_c, dbc_c, ddt, dproj, dng, ddsk, dalog = _ssd_bwd(
        dproj, xs_c, bc_c, dt, proj, ypre, hprev, dmix, alog, dskip_row, norm_g)
    dproj, dwx, dbx = _ssd_conv_bwd(dproj, proj, dxs_c, wx, bx, 1024, C_XS, "ssd_conv_bwd_x")
    dproj, dwb, dbb2 = _ssd_conv_bwd(dproj, proj, dbc_c, wb, bb, 512, C_BC, "ssd_conv_bwd_bc")
    ddtr, ddtb = _dt_bwd(ddt, dt_raw, dt_bias)
    g["ssm_conv_w"] = jnp.concatenate([dwx, dwb], axis=1)
    h0bt = h0b.T
    dw_dt = _mm(h0bt, ddtr, "nn", "d_w_dt", out_dtype=BF16)
    stack = lambda names: [_stack_shards(g[n], BY_COLS[n]) for n in names]
    last_args = (dproj, w_main, ddtr, w_dt, dh0a, x, r1(W["ln_emb_g"]))
    if late is None:
        g["w_in"] = _w_in_blocks(_mm(h0bt, dproj, "nn", "d_w_in", out_dtype=BF16), dw_dt)
        grad_x, dlng, dlnb = _d_h0_ln_bwd(*last_args)
    else:
        dw_main, *recv_a = _mm(h0bt, dproj, "nn", "d_w_in", out_dtype=BF16,
                               exchange=(stack(LATE), ("blocks",) * len(LATE)))
        last = ("ssm_conv_w", "conf_conv_w")
        blocks = [_w_in_blocks(dw_main, dw_dt)] + stack(last)
        grad_x, dlng, dlnb, *recv_b = _d_h0_ln_bwd(*last_args, exchange=(blocks, ("blocks",) * 3))
        g["recv"] = dict(zip(LATE + ("w_in",) + last, recv_a + recv_b))
    g["rows"] = [dlng, dlnb, dbx, dbb2, ddtb, dalog, ddsk, dng, dba, dbb, dccb, dclg, dclb, dbout, dln1g, dln1b,
                 dln2g, dln2b]
    return loss[0, 0], grad_x, g


W_IN_SEGMENTS = ((0, 2048, 2048), (2048, 5120, 512), (2560, None, N_HEADS), (2576, 0, 2048), (4624, 4096, 1024))


def _w_in_to_main(shards):
    def pieces(p0, width):
        out, p = [], p0
        while p < p0 + width:
            d = p // COLS_PER_DEV
            hi = min(p0 + width, (d + 1) * COLS_PER_DEV)
            out.append(shards[d][:, p - d * COLS_PER_DEV:hi - d * COLS_PER_DEV])
            p = hi
        return out
    main = [s for s in sorted(W_IN_SEGMENTS, key=lambda s: -1 if s[1] is None else s[1]) if s[1] is not None]
    w_main = jnp.concatenate([q for p0, _, width in main for q in pieces(p0, width)], axis=1)
    w_dt = jnp.concatenate(pieces(2560, N_HEADS), axis=1)
    return w_main, jnp.pad(w_dt, ((0, 0), (0, LANES - N_HEADS)))


def _w_in_blocks(dw_main, dw_dt):
    blocks = []
    for d in range(N_DEV):
        lo_d, hi_d = d * COLS_PER_DEV, (d + 1) * COLS_PER_DEV
        parts = []
        for p0, m0, width in W_IN_SEGMENTS:
            lo, hi = max(lo_d, p0), min(hi_d, p0 + width)
            if lo < hi:
                parts.append(dw_dt[:, lo - p0:hi - p0] if m0 is None else dw_main[:, m0 + lo - p0:m0 + hi - p0])
        blocks.append(jnp.concatenate(parts, axis=1))
    return jnp.stack(blocks)


WEIGHTS = ['ln_emb_g', 'ln_emb_b', 'w_in', 'ssm_conv_w', 'ssm_conv_b', 'dt_bias', 'a_log', 'd_skip', 'ssm_norm_g',
           'b_glu', 'conf_conv_w', 'conf_conv_b', 'conf_ln_g', 'conf_ln_b', 'w_out', 'b_out', 'ln1_g', 'ln1_b',
           'w_ple_gate', 'w_ple_proj', 'ln2_g', 'ln2_b']
SHARDED = (("w_in", True), ("w_out", False), ("w_ple_gate", False), ("w_ple_proj", True), ("ssm_conv_w", True),
           ("conf_conv_w", True))
BY_COLS = dict(SHARDED)


def _stack_shards(a, by_cols):
    if by_cols:
        return a.reshape(a.shape[0], N_DEV, a.shape[1] // N_DEV).transpose(1, 0, 2)
    return a.reshape(N_DEV, a.shape[0] // N_DEV, a.shape[1])


def _unstack_shards(a, by_cols):
    if by_cols:
        return a.transpose(1, 0, 2).reshape(a.shape[1], N_DEV * a.shape[2])
    return a.reshape(N_DEV * a.shape[1], a.shape[2])


def kernel(x, p, ln_emb_g, ln_emb_b, w_in, ssm_conv_w, ssm_conv_b, dt_bias, a_log, d_skip, ssm_norm_g, b_glu, conf_conv_w, conf_conv_b, conf_ln_g, conf_ln_b, w_out, b_out, ln1_g, ln1_b, w_ple_gate, w_ple_proj, ln2_g, ln2_b, loss_target, m_ln_emb_g, m_ln_emb_b, m_w_in, m_ssm_conv_w, m_ssm_conv_b, m_dt_bias, m_a_log, m_d_skip, m_ssm_norm_g, m_b_glu, m_conf_conv_w, m_conf_conv_b, m_conf_ln_g, m_conf_ln_b, m_w_out, m_b_out, m_ln1_g, m_ln1_b, m_w_ple_gate, m_w_ple_proj, m_ln2_g, m_ln2_b, v_ln_emb_g, v_ln_emb_b, v_w_in, v_ssm_conv_w, v_ssm_conv_b, v_dt_bias, v_a_log, v_d_skip, v_ssm_norm_g, v_b_glu, v_conf_conv_w, v_conf_conv_b, v_conf_ln_g, v_conf_ln_b, v_w_out, v_b_out, v_ln1_g, v_ln1_b, v_w_ple_gate, v_w_ple_proj, v_ln2_g, v_ln2_b):
    loc = dict(locals())
    w = {n: loc[n] for n in WEIGHTS}
    m = {n: loc["m_" + n] for n in WEIGHTS}
    v = {n: loc["v_" + n] for n in WEIGHTS}
    sharded = [n for n, _ in SHARDED]

    local = {n: w[n][0].astype(BF16) if n.startswith("w_") else w[n][0] for n in sharded}
    first = [n for n in sharded if n not in LATE]
    W = {n: w[n].reshape(-1) for n in SMALL}
    for n, a in zip(first, _exchange([local[n] for n in first], ("all",) * len(first), "gather_first")):
        W[n] = a if n == "w_in" else _unstack_shards(a, BY_COLS[n])

    loss, grad_x, g = _local_step(x[0], p[0, 0], loss_target[0], W, late=[local[n] for n in LATE])
    loss = lax.psum(loss, ("x", "y", "c"))
    (recv_small,) = _exchange([_pack_small(g["rows"])], ("all",), "small_exchange")

    grads, delta, new_m, new_v = {}, {}, {}, {}
    for n in sharded:
        res = _sum8_adamw(g["recv"][n], w[n][0], m[n][0], v[n][0], "adamw_" + n)
        grads[n], delta[n], new_m[n], new_v[n] = (r[None] for r in res)
    two_d = lambda d: {n: d[n].reshape(1, -1) for n in SMALL}
    for dst, res in zip((grads, delta, new_m, new_v), _small_update(recv_small, two_d(w), two_d(m), two_d(v))):
        for n in SMALL:
            dst[n] = res[n].reshape(w[n].shape)
    return (loss, grad_x[None], *[grads[n] for n in WEIGHTS], *[delta[n] for n in WEIGHTS],
            *[new_m[n] for n in WEIGHTS], *[new_v[n] for n in WEIGHTS])
```

```python
import functools

import numpy as np
import jax
import jax.numpy as jnp
from jax import lax
from jax.experimental import pallas as pl
from jax.experimental.pallas import tpu as pltpu

F32, BF16 = jnp.float32, jnp.bfloat16
S = jax.ShapeDtypeStruct

N_DEV = 8
D = 1024
D_PLE = 256
D_SSM = 1024
D_CONF = 1024
N_HEADS = 16
HEAD = 64
N_STATE = 128
CHUNK = 128
K_SSM = 4
K_CONF = 31
D_IN = 5648
COLS_PER_DEV = D_IN // N_DEV
LN_EPS = 1e-5
RMS_EPS = 1e-5
ALPHA = 2.0 ** 0.25
LANES = 128
TB = 256
RG = 32
ROW_UNROLL = 4
HALO_SSM = 8
HALO_CONF = 32
VMEM_LIMIT = 56 * 1024 * 1024

ADAM_LR, ADAM_B1, ADAM_B2, ADAM_EPS, ADAM_WD, ADAM_STEP = 0.001, 0.9, 0.999, 1e-08, 0.01, 10

C_GLUA, C_GLUB, C_XS, C_Z, C_CG = 0, 1, 2, 3, 4
C_BC = 10
N_MAIN = 5632


def _params(sem, vmem=VMEM_LIMIT):
    return pltpu.CompilerParams(dimension_semantics=sem, vmem_limit_bytes=vmem)


def _row(tb, n, col=0):
    return pl.BlockSpec((tb, n), lambda i: (i, col))


def _full(shape):
    return pl.BlockSpec(shape, lambda i: (0,) * len(shape))


_ANY = pl.BlockSpec(memory_space=pl.ANY)


def _prev(tb, halo, n, col=0):
    r = tb // halo
    return pl.BlockSpec((halo, n), lambda i: (jnp.maximum(i * r - 1, 0), col))


def _next(tb, halo, n, nt, col=0):
    r = tb // halo
    return pl.BlockSpec((halo, n), lambda i: (jnp.minimum((i + 1) * r, nt * r - 1), col))


def _row_loop(tb, fn):
    def it(r, c):
        fn(pl.ds(pl.multiple_of(r * RG, RG), RG))
        return c
    lax.fori_loop(0, tb // RG, it, 0, unroll=ROW_UNROLL)


def _col_loop(n, fn):
    def it(j, c):
        fn(pl.ds(pl.multiple_of(j * LANES, LANES), LANES))
        return c
    lax.fori_loop(0, n // LANES, it, 0)


def _sigmoid(x):
    return 1.0 / (1.0 + jnp.exp(-x))


def _dsilu(x, s):
    return s * (1.0 + x * (1.0 - s))


def _ln_stats(v):
    mu = jnp.mean(v, axis=-1, keepdims=True)
    c = v - mu
    r = lax.rsqrt(jnp.mean(c * c, axis=-1, keepdims=True) + LN_EPS)
    return c * r, r


def _ln_bwd(dy, xhat, r, g):
    dxh = dy * g
    dv = r * (dxh - jnp.mean(dxh, axis=-1, keepdims=True) - xhat * jnp.mean(dxh * xhat, axis=-1, keepdims=True))
    return dv, jnp.sum(dy * xhat, axis=0, keepdims=True), jnp.sum(dy, axis=0, keepdims=True)


def _dot(a, b, dims=((1,), (0,))):
    return lax.dot_general(a.astype(BF16), b.astype(BF16), (dims, ((), ())), preferred_element_type=F32)


_NT = ((1,), (1,))
_TN = ((0,), (0,))


def _split3(x):
    hi = x.astype(BF16)
    r = x - hi.astype(F32)
    mid = r.astype(BF16)
    return hi, mid, (r - mid.astype(F32)).astype(BF16)


def _dot_sel_b(a, b, dims=((1,), (0,))):
    hi, mid, lo = _split3(a)
    return (_dot(lo, b, dims) + _dot(mid, b, dims)) + _dot(hi, b, dims)


def _dot_sel_a(a, b, dims=((1,), (0,))):
    hi, mid, lo = _split3(b)
    return (_dot(a, lo, dims) + _dot(a, mid, dims)) + _dot(a, hi, dims)


def _mm(a, b, mode, name, out_dtype=F32, add=None, tm=1024, tn=512, tk=1024, exchange=None):
    if mode == "nn":
        (M, K), N = a.shape, b.shape[1]
    elif mode == "tn":
        (K, M), N = a.shape, b.shape[1]
    else:
        (M, K), N = a.shape, b.shape[0]
    tm, tn, tk = min(tm, M), min(tn, N), min(tk, K)
    assert M % tm == 0 and N % tn == 0 and K % tk == 0, (name, M, N, K)
    grid = (M // tm, N // tn, K // tk)
    nk = grid[2]
    dims = {"nn": ((1,), (0,)), "tn": _TN, "nt": _NT}[mode]
    n_in = 2 + (add is not None)
    xbufs, kinds = exchange if exchange is not None else ((), ())
    nx = len(xbufs)

    def body(*refs):
        a_ref, b_ref = refs[:2]
        o_ref = refs[n_in + nx]
        acc = refs[n_in + 2 * nx + 1]
        i, j, k = pl.program_id(0), pl.program_id(1), pl.program_id(2)
        if nx:
            copies = _exchange_copies(refs[n_in:n_in + nx], refs[n_in + nx + 1:n_in + 2 * nx + 1], kinds,
                                      *refs[n_in + 2 * nx + 2:])

            @pl.when((i == 0) & (j == 0) & (k == 0))
            def _():
                for cp in copies:
                    cp.start()

        @pl.when(k == 0)
        def _():
            acc[...] = jnp.zeros_like(acc)

        acc[...] += _dot(a_ref[...], b_ref[...], dims)

        @pl.when(k == nk - 1)
        def _():
            r = acc[...]
            if add is not None:
                r = r + refs[2][...]
            o_ref[...] = r.astype(out_dtype)

        if nx:
            @pl.when((i == grid[0] - 1) & (j == grid[1] - 1) & (k == nk - 1))
            def _():
                for cp in copies:
                    cp.wait()

    a_spec = pl.BlockSpec((tk, tm), lambda i, j, k: (k, i)) if mode == "tn" else pl.BlockSpec((tm, tk), lambda i, j, k: (i, k))
    b_spec = pl.BlockSpec((tn, tk), lambda i, j, k: (j, k)) if mode == "nt" else pl.BlockSpec((tk, tn), lambda i, j, k: (k, j))
    o_spec = pl.BlockSpec((tm, tn), lambda i, j, k: (i, j))
    ins, specs = [a, b], [a_spec, b_spec]
    if add is not None:
        ins.append(add)
        specs.append(o_spec)
    if not nx:
        return pl.pallas_call(
            body, name=name, grid=grid, in_specs=specs, out_specs=o_spec,
            out_shape=S((M, N), out_dtype), scratch_shapes=[pltpu.VMEM((tm, tn), F32)],
            compiler_params=_params(("parallel", "parallel", "arbitrary")))(*ins)
    return pl.pallas_call(
        body, name=name, grid=grid, in_specs=specs + [_ANY] * nx, out_specs=[o_spec] + [_ANY] * nx,
        out_shape=[S((M, N), out_dtype)] + _exchange_shapes(xbufs, kinds),
        scratch_shapes=[pltpu.VMEM((tm, tn), F32)] + _exchange_sems(nx),
        compiler_params=_params(("arbitrary", "arbitrary", "arbitrary")))(*ins, *xbufs)


def _ln_emb_fwd(x, g, b):
    T = x.shape[0]

    def body(x_ref, g_ref, b_ref, h_ref, hb_ref):
        def rows(rs):
            xh, _ = _ln_stats(x_ref[rs, :])
            h = xh * g_ref[...] + b_ref[...]
            h_ref[rs, :] = h
            hb_ref[rs, :] = h.astype(BF16)
        _row_loop(TB, rows)

    return pl.pallas_call(
        body, name="ln_emb_fwd", grid=(T // TB,),
        in_specs=[_row(TB, D), _full((1, D)), _full((1, D))], out_specs=[_row(TB, D), _row(TB, D)],
        out_shape=[S((T, D), F32), S((T, D), BF16)], compiler_params=_params(("parallel",)))(x, g, b)


def _post1_fwd(h0, out, b_out, g, b):
    T = h0.shape[0]

    def body(h0_ref, out_ref, bo_ref, g_ref, b_ref, h_ref, hb_ref):
        def rows(rs):
            xh, _ = _ln_stats(ALPHA * h0_ref[rs, :] + out_ref[rs, :] + bo_ref[...])
            h = xh * g_ref[...] + b_ref[...]
            h_ref[rs, :] = h
            hb_ref[rs, :] = h.astype(BF16)
        _row_loop(TB, rows)

    return pl.pallas_call(
        body, name="post1_fwd", grid=(T // TB,),
        in_specs=[_row(TB, D), _row(TB, D)] + [_full((1, D))] * 3, out_specs=[_row(TB, D), _row(TB, D)],
        out_shape=[S((T, D), F32), S((T, D), BF16)], compiler_params=_params(("parallel",)))(h0, out, b_out, g, b)


def _post2(h1, gpre, ple, tgt, g, b):
    T = h1.shape[0]

    def body(h1_ref, gp_ref, ple_ref, tgt_ref, g_ref, b_ref, dh1_ref, dgp_ref, dple_ref, loss_ref, dg_ref, db_ref):
        @pl.when(pl.program_id(0) == 0)
        def _():
            loss_ref[...] = jnp.zeros_like(loss_ref)
            dg_ref[...] = jnp.zeros_like(dg_ref)
            db_ref[...] = jnp.zeros_like(db_ref)

        def rows(rs):
            gate = _sigmoid(gp_ref[rs, :])
            ple = ple_ref[rs, :]
            xh, r = _ln_stats(ALPHA * h1_ref[rs, :] + gate * ple)
            err = xh * g_ref[...] + b_ref[...] - tgt_ref[rs, :]
            loss_ref[...] += 0.5 * jnp.sum(jnp.mean(err * err, axis=-1, keepdims=True), axis=0, keepdims=True)
            dv, dg, db = _ln_bwd(err * (1.0 / D), xh, r, g_ref[...])
            dg_ref[...] += dg
            db_ref[...] += db
            dh1_ref[rs, :] = ALPHA * dv
            dgp_ref[rs, :] = (dv * ple * gate * (1.0 - gate)).astype(BF16)
            dple_ref[rs, :] = (dv * gate).astype(BF16)
        _row_loop(TB, rows)

    return pl.pallas_call(
        body, name="post2", grid=(T // TB,),
        in_specs=[_row(TB, D)] * 4 + [_full((1, D))] * 2,
        out_specs=[_row(TB, D)] * 3 + [_full((8, LANES)), _full((1, D)), _full((1, D))],
        out_shape=[S((T, D), F32), S((T, D), BF16), S((T, D), BF16), S((8, LANES), F32), S((1, D), F32), S((1, D), F32)],
        compiler_params=_params(("arbitrary",)))(h1, gpre, ple, tgt, g, b)


def _post1_bwd(dh1a, dh1b, h0, out, b_out, g):
    T = h0.shape[0]

    def body(da_ref, db2_ref, h0_ref, out_ref, bo_ref, g_ref, dout_ref, dh0_ref, dg_ref, db_ref, dbo_ref):
        @pl.when(pl.program_id(0) == 0)
        def _():
            dg_ref[...] = jnp.zeros_like(dg_ref)
            db_ref[...] = jnp.zeros_like(db_ref)
            dbo_ref[...] = jnp.zeros_like(dbo_ref)

        def rows(rs):
            xh, r = _ln_stats(ALPHA * h0_ref[rs, :] + out_ref[rs, :] + bo_ref[...])
            dv, dg, db = _ln_bwd(da_ref[rs, :] + db2_ref[rs, :], xh, r, g_ref[...])
            dg_ref[...] += dg
            db_ref[...] += db
            dbo_ref[...] += jnp.sum(dv, axis=0, keepdims=True)
            dout_ref[rs, :] = dv.astype(BF16)
            dh0_ref[rs, :] = ALPHA * dv
        _row_loop(TB, rows)

    return pl.pallas_call(
        body, name="post1_bwd", grid=(T // TB,),
        in_specs=[_row(TB, D)] * 4 + [_full((1, D))] * 2,
        out_specs=[_row(TB, D)] * 2 + [_full((1, D))] * 3,
        out_shape=[S((T, D), BF16), S((T, D), F32)] + [S((1, D), F32)] * 3,
        compiler_params=_params(("arbitrary",)))(dh1a, dh1b, h0, out, b_out, g)


def _d_h0_ln_bwd(dproj, w_main, ddtr, w_dt, dh0a, x, g, exchange=None, tm=1024, tk=1408):
    T, K = dproj.shape
    tm = min(tm, T)
    assert T % tm == 0 and K % tk == 0
    ni, nk = T // tm, K // tk
    xbufs, kinds = exchange if exchange is not None else ((), ())
    nx = len(xbufs)

    def body(*refs):
        dp_ref, w_ref, dt_ref, wdt_ref, da_ref, x_ref, g_ref = refs[:7]
        dx_ref, dg_ref, db_ref = refs[7 + nx:10 + nx]
        acc = refs[10 + 2 * nx]
        i, k = pl.program_id(0), pl.program_id(1)
        if nx:
            copies = _exchange_copies(refs[7:7 + nx], refs[10 + nx:10 + 2 * nx], kinds, *refs[11 + 2 * nx:])

            @pl.when((i == 0) & (k == 0))
            def _():
                for cp in copies:
                    cp.start()

        @pl.when((i == 0) & (k == 0))
        def _():
            dg_ref[...] = jnp.zeros_like(dg_ref)
            db_ref[...] = jnp.zeros_like(db_ref)

        @pl.when(k == 0)
        def _():
            acc[...] = da_ref[...] + _dot(dt_ref[...], wdt_ref[...], _NT)

        acc[...] += _dot(dp_ref[...], w_ref[...], _NT)

        @pl.when(k == nk - 1)
        def _():
            def rows(rs):
                xh, r = _ln_stats(x_ref[rs, :])
                dv, dg, db = _ln_bwd(acc[rs, :], xh, r, g_ref[...])
                dg_ref[...] += dg
                db_ref[...] += db
                dx_ref[rs, :] = dv
            _row_loop(tm, rows)

        if nx:
            @pl.when((i == ni - 1) & (k == nk - 1))
            def _():
                for cp in copies:
                    cp.wait()

    rowt = lambda n: pl.BlockSpec((tm, n), lambda i, k: (i, 0))
    const = lambda shape: pl.BlockSpec(shape, lambda i, k: (0, 0))
    return pl.pallas_call(
        body, name="d_h0_ln_bwd", grid=(ni, nk),
        in_specs=[pl.BlockSpec((tm, tk), lambda i, k: (i, k)), pl.BlockSpec((D, tk), lambda i, k: (0, k)),
                  rowt(LANES), const((D, LANES)), rowt(D), rowt(D), const((1, D))] + [_ANY] * nx,
        out_specs=[rowt(D), const((1, D)), const((1, D))] + [_ANY] * nx,
        out_shape=[S((T, D), F32), S((1, D), F32), S((1, D), F32)] + _exchange_shapes(xbufs, kinds),
        scratch_shapes=[pltpu.VMEM((tm, D), F32)] + (_exchange_sems(nx) if nx else []),
        compiler_params=_params(("arbitrary", "arbitrary")))(dproj, w_main, ddtr, w_dt, dh0a, x, g, *xbufs)


def _softplus(x):
    return jnp.maximum(x, 0.0) + jnp.log1p(jnp.exp(-jnp.abs(x)))


def _ssd_pre_fwd(proj, dt_raw, wx, wb, bx, bb, dt_bias):
    T = proj.shape[0]
    H = HALO_SSM

    def body(xs_ref, xsp_ref, bc_ref, bcp_ref, dtr_ref, wx_ref, wb_ref, bx_ref, bb_ref, dtb_ref,
             xso_ref, bco_ref, dto_ref, extx, extb):
        first = pl.program_id(0) == 0

        def conv(t_ref, p_ref, w_ref, b_ref, o_ref, ext, n):
            def blk(cols):
                ext[0:H, cols] = jnp.where(first, 0.0, p_ref[:, cols])
                ext[H:, cols] = t_ref[:, cols]
                for r0 in range(0, TB, 64):
                    acc = jnp.broadcast_to(b_ref[:, cols], (64, LANES))
                    for k in range(K_SSM):
                        acc = acc + w_ref[k:k + 1, cols] * ext[pl.ds(r0 + H - (K_SSM - 1) + k, 64), cols]
                    o_ref[pl.ds(r0, 64), cols] = acc * _sigmoid(acc)
            _col_loop(n, blk)

        conv(xs_ref, xsp_ref, wx_ref, bx_ref, xso_ref, extx, D_SSM)
        conv(bc_ref, bcp_ref, wb_ref, bb_ref, bco_ref, extb, 512)
        dto_ref[...] = _softplus(dtr_ref[...] + dtb_ref[...])

    return pl.pallas_call(
        body, name="ssd_pre_fwd", grid=(T // TB,),
        in_specs=[_row(TB, 1024, C_XS), _prev(TB, H, 1024, C_XS), _row(TB, 512, C_BC), _prev(TB, H, 512, C_BC),
                  _row(TB, LANES), _full((K_SSM, 1024)), _full((K_SSM, 512)), _full((1, 1024)), _full((1, 512)),
                  _full((1, LANES))],
        out_specs=[_row(TB, 1024), _row(TB, 512), _row(TB, LANES)],
        out_shape=[S((T, 1024), F32), S((T, 512), F32), S((T, LANES), F32)],
        scratch_shapes=[pltpu.VMEM((H + TB, 1024), F32), pltpu.VMEM((H + TB, 512), F32)],
        compiler_params=_params(("parallel",)))(proj, proj, proj, proj, dt_raw, wx, wb, bx, bb, dt_bias)


def _ssd_conv_bwd(dproj, proj, d_c, w, b, n, col, name):
    T = proj.shape[0]
    nt = T // TB
    H = HALO_SSM
    R = TB + H

    def body(dproj_ref, t_ref, p_ref, n_ref, d_ref, dn_ref, w_ref, b_ref, o_ref, dw_ref, dbias_ref, ext, dp):
        i = pl.program_id(0)
        first, last = i == 0, i == nt - 1

        @pl.when(first)
        def _():
            dw_ref[...] = jnp.zeros_like(dw_ref)
            dbias_ref[...] = jnp.zeros_like(dbias_ref)

        def blk(cols):
            ext[0:H, cols] = jnp.where(first, 0.0, p_ref[:, cols])
            ext[H:H + TB, cols] = t_ref[:, cols]
            ext[H + TB:, cols] = n_ref[:, cols]
            pre = jnp.broadcast_to(b_ref[:, cols], (R, LANES))
            for k in range(K_SSM):
                pre = pre + w_ref[k:k + 1, cols] * ext[pl.ds(H - (K_SSM - 1) + k, R), cols]
            s = _sigmoid(pre)
            ds = _dsilu(pre, s)
            dp[0:TB, cols] = d_ref[:, cols] * ds[0:TB]
            dp[TB:, cols] = jnp.where(last, 0.0, dn_ref[:, cols] * ds[TB:])
            dpt = dp[0:TB, cols]
            dbias_ref[:, cols] += jnp.sum(dpt, axis=0, keepdims=True)
            acc = jnp.zeros((TB, LANES), F32)
            for k in range(K_SSM):
                dw_ref[k:k + 1, cols] += jnp.sum(dpt * ext[pl.ds(H - (K_SSM - 1) + k, TB), cols], axis=0, keepdims=True)
                acc = acc + w_ref[k:k + 1, cols] * dp[pl.ds(K_SSM - 1 - k, TB), cols]
            o_ref[:, cols] = acc.astype(BF16)
        _col_loop(n, blk)

    return pl.pallas_call(
        body, name=name, grid=(nt,),
        in_specs=[_ANY, _row(TB, n, col), _prev(TB, H, n, col), _next(TB, H, n, nt, col),
                  _row(TB, n), _next(TB, H, n, nt), _full((K_SSM, n)), _full((1, n))],
        out_specs=[_row(TB, n, col), _full((K_SSM, n)), _full((1, n))],
        out_shape=[S(dproj.shape, BF16), S((K_SSM, n), F32), S((1, n), F32)],
        input_output_aliases={0: 0},
        scratch_shapes=[pltpu.VMEM((H + TB + H, n), F32), pltpu.VMEM((R, n), F32)],
        compiler_params=_params(("arbitrary",)))(dproj, proj, proj, proj, d_c, d_c, w, b)


def _dt_bwd(ddt, dt_raw, dt_bias):
    T = ddt.shape[0]

    def body(ddt_ref, dtr_ref, dtb_ref, o_ref, db_ref):
        @pl.when(pl.program_id(0) == 0)
        def _():
            db_ref[...] = jnp.zeros_like(db_ref)

        g = ddt_ref[...] * _sigmoid(dtr_ref[...] + dtb_ref[...])
        o_ref[...] = g.astype(BF16)
        db_ref[...] += jnp.sum(g, axis=0, keepdims=True)

    return pl.pallas_call(
        body, name="dt_bwd", grid=(T // TB,),
        in_specs=[_row(TB, LANES), _row(TB, LANES), _full((1, LANES))], out_specs=[_row(TB, LANES), _full((1, LANES))],
        out_shape=[S((T, LANES), BF16), S((1, LANES), F32)],
        compiler_params=_params(("arbitrary",)))(ddt, dt_raw, dt_bias)


def _ssd_consts():
    ex = np.zeros((LANES, D_SSM), np.float32)
    for h in range(N_HEADS):
        ex[h, h * HEAD:(h + 1) * HEAD] = 1.0
    tri = np.tril(np.ones((CHUNK, CHUNK), np.float32))
    return jnp.asarray(ex), jnp.asarray(ex.T.copy()), jnp.asarray(tri), jnp.asarray(tri.T.copy())


def _ssd_common(xs, dt, alog_ref, ex_ref, tri_ref):
    lane = lax.broadcasted_iota(jnp.int32, (1, LANES), 1)
    a = jnp.where(lane < N_HEADS, -jnp.exp(alog_ref[...]), 0.0)
    A = _dot_sel_a(tri_ref[...], dt * a)
    ex = ex_ref[...]
    Aex = _dot_sel_b(A, ex)
    dtex = _dot_sel_b(dt, ex)
    expA = jnp.exp(Aex)
    dec = jnp.exp(Aex[CHUNK - 1:CHUNK, :] - Aex)
    cd = _dot_sel_a(ex, jnp.broadcast_to(jnp.exp(A.T[:, CHUNK - 1:CHUNK]), (LANES, LANES)), _TN)
    return a, A, dtex, expA, dec, cd


def _decay_mask():
    sub = lax.broadcasted_iota(jnp.int32, (CHUNK, CHUNK), 0)
    lane = lax.broadcasted_iota(jnp.int32, (CHUNK, CHUNK), 1)
    return sub, lane, sub >= lane


def _ssd_fwd(xs_c, bc_c, dt, proj, alog, dskip_row, norm_g):
    T = xs_c.shape[0]
    nc = T // CHUNK
    ex, _, tri, _ = _ssd_consts()

    def body(xs_ref, bc_ref, dt_ref, z_ref, alog_ref, dsk_ref, ng_ref, ex_ref, tri_ref,
             ys_ref, ypre_ref, hprev_ref, Hs, ybuf):
        @pl.when(pl.program_id(0) == 0)
        def _():
            Hs[...] = jnp.zeros_like(Hs)

        hprev_ref[0] = Hs[...]
        xs, dt = xs_ref[...], dt_ref[...]
        a, A, dtex, expA, dec, cd = _ssd_common(xs, dt, alog_ref, ex_ref, tri_ref)
        AT = A.T
        xdt = xs * dtex
        xdec = xdt * dec
        _, _, causal = _decay_mask()
        for g in range(2):
            gs = slice(g * 512, (g + 1) * 512)
            B = bc_ref[:, g * N_STATE:(g + 1) * N_STATE]
            C = bc_ref[:, 256 + g * N_STATE:256 + (g + 1) * N_STATE]
            cb = _dot(C, B, _NT)
            Hg = Hs[gs, :]
            yoff = _dot(C, Hg, _NT) * expA[:, gs]
            for j in range(8):
                h = g * 8 + j
                hs = slice(h * HEAD, (h + 1) * HEAD)
                L = jnp.exp(jnp.where(causal, A[:, h:h + 1] - AT[h:h + 1, :], -1e30))
                ybuf[:, hs] = _dot(cb * L, xdt[:, hs]) + yoff[:, j * HEAD:(j + 1) * HEAD]
            Hs[gs, :] = cd[gs, :] * Hg + _dot(xdec[:, gs], B, _TN)
        ypre = ybuf[...] + dsk_ref[...] * xs
        ypre_ref[...] = ypre
        z = z_ref[...]
        yz = ypre * (z * _sigmoid(z))
        for g in range(2):
            gs = slice(g * 512, (g + 1) * 512)
            v = yz[:, gs]
            r = lax.rsqrt(jnp.mean(v * v, axis=-1, keepdims=True) + RMS_EPS)
            ys_ref[:, gs] = (v * r * ng_ref[:, gs]).astype(BF16)

    return pl.pallas_call(
        body, name="ssd_fwd", grid=(nc,),
        in_specs=[_row(CHUNK, 1024), _row(CHUNK, 512), _row(CHUNK, LANES), _row(CHUNK, 1024, C_Z),
                  _full((1, LANES)), _full((1, 1024)), _full((1, 1024)), _full((LANES, 1024)), _full((CHUNK, CHUNK))],
        out_specs=[_row(CHUNK, 1024), _row(CHUNK, 1024), pl.BlockSpec((1, 1024, N_STATE), lambda c: (c, 0, 0))],
        out_shape=[S((T, 2048), BF16), S((T, 1024), F32), S((nc, 1024, N_STATE), F32)],
        scratch_shapes=[pltpu.VMEM((1024, N_STATE), F32), pltpu.VMEM((CHUNK, 1024), F32)],
        compiler_params=_params(("arbitrary",)))(xs_c, bc_c, dt, proj, alog, dskip_row, norm_g, ex, tri)


def _ssd_bwd(dproj, xs_c, bc_c, dt, proj, ypre, hprev, dmix, alog, dskip_row, norm_g):
    T = xs_c.shape[0]
    nc = T // CHUNK
    ex, ext, tri, triu = _ssd_consts()
    rev = lambda n, col=0: pl.BlockSpec((CHUNK, n), lambda c: (nc - 1 - c, col))

    def body(dproj_ref, xs_ref, bc_ref, dt_ref, z_ref, ypre_ref, hprev_ref, dys_ref, alog_ref, dsk_ref, ng_ref,
             ex_ref, ext_ref, tri_ref, triu_ref,
             dxs_ref, dbc_ref, ddt_ref, dz_ref, dng_ref, ddsk_ref, dalog_ref, dHs, dxbuf, dskacc):
        c = pl.program_id(0)

        @pl.when(c == 0)
        def _():
            dHs[...] = jnp.zeros_like(dHs)
            dng_ref[...] = jnp.zeros_like(dng_ref)
            dalog_ref[...] = jnp.zeros_like(dalog_ref)
            dskacc[...] = jnp.zeros_like(dskacc)

        xs, dt, z, ypre, dys = xs_ref[...], dt_ref[...], z_ref[...], ypre_ref[...], dys_ref[...]
        sg = _sigmoid(z)
        sz = z * sg
        yz = ypre * sz
        dyz_parts = []
        for g in range(2):
            gs = slice(g * 512, (g + 1) * 512)
            v = yz[:, gs]
            r = lax.rsqrt(jnp.mean(v * v, axis=-1, keepdims=True) + RMS_EPS)
            vn = v * r
            dng_ref[:, gs] += jnp.sum(dys[:, gs] * vn, axis=0, keepdims=True)
            dvn = dys[:, gs] * ng_ref[:, gs]
            dyz_parts.append(r * (dvn - vn * jnp.mean(dvn * vn, axis=-1, keepdims=True)))
        dyz = jnp.concatenate(dyz_parts, axis=1)
        dy = dyz * sz
        dz_ref[...] = (dyz * ypre * _dsilu(z, sg)).astype(BF16)
        dskacc[...] += jnp.sum(dy * xs, axis=0, keepdims=True)

        a, A, dtex, expA, dec, cd = _ssd_common(xs, dt, alog_ref, ex_ref, tri_ref)
        AT = A.T
        xdt = xs * dtex
        xdec = xdt * dec
        dye = dy * expA
        H = hprev_ref[0]
        dHn = dHs[...]
        sub, lane, causal = _decay_mask()
        dAc = jnp.zeros((CHUNK, LANES), F32)
        Rm = jnp.zeros((CHUNK, LANES), F32)
        yoff_parts, q_parts = [], []
        for g in range(2):
            gs = slice(g * 512, (g + 1) * 512)
            B = bc_ref[:, g * N_STATE:(g + 1) * N_STATE]
            C = bc_ref[:, 256 + g * N_STATE:256 + (g + 1) * N_STATE]
            cb = _dot(C, B, _NT)
            Hg, dHg = H[gs, :], dHn[gs, :]
            Q = _dot(B, dHg, _NT)
            yoff_parts.append(_dot(C, Hg, _NT) * expA[:, gs])
            q_parts.append(Q)
            dcb = jnp.zeros((CHUNK, CHUNK), F32)
            for j in range(8):
                h = g * 8 + j
                hs = slice(h * HEAD, (h + 1) * HEAD)
                L = jnp.exp(jnp.where(causal, A[:, h:h + 1] - AT[h:h + 1, :], -1e30))
                M = cb * L
                G = _dot(dy[:, hs], xdt[:, hs], _NT)
                dxbuf[:, hs] = _dot(M, dy[:, hs], _TN)
                dcb = dcb + G * L
                E = G * M
                dAc = jnp.where(lane == h, jnp.sum(E, axis=1, keepdims=True), dAc)
                Rm = jnp.where(sub == h, jnp.sum(E, axis=0, keepdims=True), Rm)
            dbc_ref[:, g * N_STATE:(g + 1) * N_STATE] = _dot(dcb, C, _TN) + _dot(xdec[:, gs], dHg)
            dbc_ref[:, 256 + g * N_STATE:256 + (g + 1) * N_STATE] = _dot(dcb, B) + _dot(dye[:, gs], Hg)
            dHs[gs, :] = cd[gs, :] * dHg + _dot(dye[:, gs], C, _TN)
        yoff = jnp.concatenate(yoff_parts, axis=1)
        Qd = jnp.concatenate(q_parts, axis=1) * dec
        dxdt = dxbuf[...] + Qd
        extm = ext_ref[...]
        red_s = _dot_sel_b(xdt * Qd, extm)
        dA = dAc - Rm.T + _dot_sel_b(dy * yoff, extm) - red_s
        hd = jnp.sum(_dot_sel_b(H * dHn, extm, _TN), axis=0, keepdims=True)
        last_add = jnp.sum(red_s, axis=0, keepdims=True) + jnp.exp(A[CHUNK - 1:CHUNK, :]) * hd
        dA = dA + jnp.where(sub == CHUNK - 1, last_add, 0.0)
        dadt = _dot_sel_a(triu_ref[...], dA)
        ddt_ref[...] = dadt * a + _dot_sel_b(dxdt * xs, extm)
        dalog_ref[...] += jnp.sum(dadt * dt, axis=0, keepdims=True) * a
        dxs_ref[...] = dxdt * dtex + dsk_ref[...] * dy

        @pl.when(c == nc - 1)
        def _():
            ddsk_ref[...] = _dot_sel_b(jnp.broadcast_to(dskacc[...], (8, 1024)), extm)[0:1, :]

    return pl.pallas_call(
        body, name="ssd_bwd", grid=(nc,),
        in_specs=[_ANY, rev(1024), rev(512), rev(LANES), rev(1024, C_Z), rev(1024),
                  pl.BlockSpec((1, 1024, N_STATE), lambda c: (nc - 1 - c, 0, 0)), rev(1024, 0),
                  _full((1, LANES)), _full((1, 1024)), _full((1, 1024)),
                  _full((LANES, 1024)), _full((1024, LANES)), _full((CHUNK, CHUNK)), _full((CHUNK, CHUNK))],
        out_specs=[rev(1024), rev(512), rev(LANES), rev(1024, C_Z), _full((1, 1024)), _full((1, LANES)), _full((1, LANES))],
        out_shape=[S((T, 1024), F32), S((T, 512), F32), S((T, LANES), F32), S(dproj.shape, BF16),
                   S((1, 1024), F32), S((1, LANES), F32), S((1, LANES), F32)],
        input_output_aliases={0: 3},
        scratch_shapes=[pltpu.VMEM((1024, N_STATE), F32), pltpu.VMEM((CHUNK, 1024), F32), pltpu.VMEM((1, 1024), F32)],
        compiler_params=_params(("arbitrary",)))(
            dproj, xs_c, bc_c, dt, proj, ypre, hprev, dmix, alog, dskip_row, norm_g, ex, ext, tri, triu)


def _shifted_copies(ext, ext8):
    n = ext8.shape[1]
    for r in range(8):
        ext8[r] = ext[pl.ds(r, n), :]


def _shifted(ext8, off, rows):
    return ext8[off % 8, pl.ds(off - off % 8, rows), :]


def _conf_fwd(mix, proj, w, cb, lg, lb, ba, bb):
    T = proj.shape[0]
    H = HALO_CONF

    def body(mix_ref, ga_ref, gap_ref, gb_ref, gbp_ref, cg_ref, w_ref, cb_ref, lg_ref, lb_ref, ba_ref, bb_ref,
             u1_ref, yc_ref, ext, ext8):
        first = pl.program_id(0) == 0
        ext[H + TB:, :] = jnp.zeros((8, LANES), F32)

        def blk(cols):
            up = (gap_ref[:, cols] + ba_ref[:, cols]) * _sigmoid(gbp_ref[:, cols] + bb_ref[:, cols])
            ext[0:H, :] = jnp.where(first, 0.0, up)
            ext[H:H + TB, :] = (ga_ref[:, cols] + ba_ref[:, cols]) * _sigmoid(gb_ref[:, cols] + bb_ref[:, cols])
            _shifted_copies(ext, ext8)
            for r0 in range(0, TB, 64):
                acc = jnp.broadcast_to(cb_ref[:, cols], (64, LANES))
                for k in range(K_CONF):
                    acc = acc + w_ref[k:k + 1, cols] * _shifted(ext8, r0 + H - (K_CONF - 1) + k, 64)
                u1_ref[pl.ds(r0, 64), cols] = acc
        _col_loop(D_CONF, blk)

        def rows(rs):
            xh, _ = _ln_stats(u1_ref[rs, :])
            u2 = xh * lg_ref[...] + lb_ref[...]
            cg = cg_ref[rs, :]
            yc_ref[rs, :] = (u2 * _sigmoid(u2) * cg * _sigmoid(cg)).astype(BF16)
        _row_loop(TB, rows)

    return pl.pallas_call(
        body, name="conf_fwd", grid=(T // TB,),
        in_specs=[_ANY, _row(TB, 1024, C_GLUA), _prev(TB, H, 1024, C_GLUA), _row(TB, 1024, C_GLUB),
                  _prev(TB, H, 1024, C_GLUB), _row(TB, 1024, C_CG), _full((K_CONF, 1024))] + [_full((1, 1024))] * 5,
        out_specs=[_row(TB, 1024), _row(TB, 1024, 1)],
        out_shape=[S((T, 1024), F32), S((T, 2048), BF16)],
        input_output_aliases={0: 1},
        scratch_shapes=[pltpu.VMEM((H + TB + 8, LANES), F32), pltpu.VMEM((8, H + TB, LANES), F32)],
        compiler_params=_params(("parallel",)))(mix, proj, proj, proj, proj, proj, w, cb, lg, lb, ba, bb)


def _conf_bwd1(dmix, u1, proj, lg, lb):
    T = u1.shape[0]

    def body(dy_ref, u1_ref, cg_ref, lg_ref, lb_ref, du1_ref, dcg_ref, dg_ref, db_ref):
        @pl.when(pl.program_id(0) == 0)
        def _():
            dg_ref[...] = jnp.zeros_like(dg_ref)
            db_ref[...] = jnp.zeros_like(db_ref)

        def rows(rs):
            xh, r = _ln_stats(u1_ref[rs, :])
            u2 = xh * lg_ref[...] + lb_ref[...]
            s2 = _sigmoid(u2)
            cg = cg_ref[rs, :]
            sc = _sigmoid(cg)
            dy = dy_ref[rs, :]
            dcg_ref[rs, :] = (dy * u2 * s2 * _dsilu(cg, sc)).astype(BF16)
            dv, dg, db = _ln_bwd(dy * cg * sc * _dsilu(u2, s2), xh, r, lg_ref[...])
            dg_ref[...] += dg
            db_ref[...] += db
            du1_ref[rs, :] = dv
        _row_loop(TB, rows)

    return pl.pallas_call(
        body, name="conf_bwd1", grid=(T // TB,),
        in_specs=[_row(TB, 1024, 1), _row(TB, 1024), _row(TB, 1024, C_CG), _full((1, 1024)), _full((1, 1024))],
        out_specs=[_row(TB, 1024), _row(TB, 1024, C_CG), _full((1, 1024)), _full((1, 1024))],
        out_shape=[S((T, 1024), F32), S((T, N_MAIN), BF16), S((1, 1024), F32), S((1, 1024), F32)],
        compiler_params=_params(("arbitrary",)))(dmix, u1, proj, lg, lb)


def _conf_bwd2(dproj, proj, du1, w, ba, bb):
    T = du1.shape[0]
    nt = T // TB
    H = HALO_CONF

    def body(dproj_ref, ga_ref, gap_ref, gb_ref, gbp_ref, du_ref, dun_ref, w_ref, ba_ref, bb_ref,
             dg_ref, dw_ref, dcb_ref, dba_ref, dbb_ref, ext, dext, ext8, dext8, dwacc):
        i = pl.program_id(0)
        first, last = i == 0, i == nt - 1

        @pl.when(first)
        def _():
            for r in (dcb_ref, dba_ref, dbb_ref, dwacc):
                r[...] = jnp.zeros_like(r)

        ext[H + TB:, :] = jnp.zeros((8, LANES), F32)
        dext[H + TB:, :] = jnp.zeros((8, LANES), F32)

        def blk(cols):
            cols_b = pl.ds(pl.multiple_of(cols.start + D_CONF, LANES), LANES)
            up = (gap_ref[:, cols] + ba_ref[:, cols]) * _sigmoid(gbp_ref[:, cols] + bb_ref[:, cols])
            ext[0:H, :] = jnp.where(first, 0.0, up)
            a = ga_ref[:, cols] + ba_ref[:, cols]
            sb = _sigmoid(gb_ref[:, cols] + bb_ref[:, cols])
            ext[H:H + TB, :] = a * sb
            du = du_ref[:, cols]
            dext[0:TB, :] = du
            dext[TB:TB + H, :] = jnp.where(last, 0.0, dun_ref[:, cols])
            _shifted_copies(ext, ext8)
            _shifted_copies(dext, dext8)
            dcb_ref[:, cols] += jnp.sum(du, axis=0, keepdims=True)
            for r0 in range(0, TB, 64):
                dur = du_ref[pl.ds(r0, 64), cols]
                acc = jnp.zeros((64, LANES), F32)
                for k in range(K_CONF):
                    prod = dur * _shifted(ext8, r0 + H - (K_CONF - 1) + k, 64)
                    dwacc[k * 8:(k + 1) * 8, cols] += prod.reshape(8, 8, LANES).sum(axis=0)
                    acc = acc + w_ref[k:k + 1, cols] * _shifted(dext8, r0 + K_CONF - 1 - k, 64)
                ar, sr = a[r0:r0 + 64], sb[r0:r0 + 64]
                da = acc * sr
                dbv = acc * ar * sr * (1.0 - sr)
                dg_ref[pl.ds(r0, 64), cols] = da.astype(BF16)
                dg_ref[pl.ds(r0, 64), cols_b] = dbv.astype(BF16)
                dba_ref[:, cols] += jnp.sum(da, axis=0, keepdims=True)
                dbb_ref[:, cols] += jnp.sum(dbv, axis=0, keepdims=True)
        _col_loop(D_CONF, blk)

        @pl.when(last)
        def _():
            dw_ref[...] = jnp.sum(dwacc[...].reshape(K_CONF, 8, D_CONF), axis=1)

    return pl.pallas_call(
        body, name="conf_bwd2", grid=(nt,),
        in_specs=[_ANY, _row(TB, 1024, C_GLUA), _prev(TB, H, 1024, C_GLUA), _row(TB, 1024, C_GLUB),
                  _prev(TB, H, 1024, C_GLUB), _row(TB, 1024), _next(TB, H, 1024, nt), _full((K_CONF, 1024)),
                  _full((1, 1024)), _full((1, 1024))],
        out_specs=[_row(TB, 2048), _full((K_CONF, 1024)), _full((1, 1024)), _full((1, 1024)), _full((1, 1024))],
        out_shape=[S(dproj.shape, BF16), S((K_CONF, 1024), F32)] + [S((1, 1024), F32)] * 3,
        input_output_aliases={0: 0},
        scratch_shapes=[pltpu.VMEM((H + TB + 8, LANES), F32), pltpu.VMEM((TB + H + 8, LANES), F32),
                        pltpu.VMEM((8, H + TB, LANES), F32), pltpu.VMEM((8, TB + H, LANES), F32),
                        pltpu.VMEM((K_CONF * 8, D_CONF), F32)],
        compiler_params=_params(("arbitrary",)))(dproj, proj, proj, proj, proj, du1, du1, w, ba, bb)


def _mesh_pos():
    x, y, c = lax.axis_index("x"), lax.axis_index("y"), lax.axis_index("c")
    return x, y, c, 4 * x + 2 * y + c


def _peer(x, y, c, k):
    return (x ^ ((k >> 2) & 1), y ^ ((k >> 1) & 1), c ^ (k & 1))


def _exchange_copies(ins, outs, kinds, send, recv, loc):
    nb = len(ins)
    x, y, c, me = _mesh_pos()
    src = lambda b, d: ins[b].at[d] if kinds[b] == "blocks" else ins[b]
    copies = [pltpu.make_async_copy(src(b, me), outs[b].at[me], loc.at[b]) for b in range(nb)]
    for k in range(1, N_DEV):
        px, py, pc = _peer(x, y, c, k)
        for b in range(nb):
            s = (k - 1) * nb + b
            copies.append(pltpu.make_async_remote_copy(
                src_ref=src(b, 4 * px + 2 * py + pc), dst_ref=outs[b].at[me], send_sem=send.at[s], recv_sem=recv.at[s],
                device_id=(px, py, pc), device_id_type=pl.DeviceIdType.MESH))
    return copies


def _exchange_shapes(bufs, kinds):
    return [S(b.shape if kd == "blocks" else (N_DEV,) + b.shape, b.dtype) for b, kd in zip(bufs, kinds)]


def _exchange_sems(nb):
    n = (N_DEV - 1) * nb
    return [pltpu.SemaphoreType.DMA((n,)), pltpu.SemaphoreType.DMA((n,)), pltpu.SemaphoreType.DMA((nb,))]


def _exchange(bufs, kinds, name):
    nb = len(bufs)

    def body(*refs):
        copies = _exchange_copies(refs[:nb], refs[nb:2 * nb], kinds, *refs[2 * nb:])
        for cp in copies:
            cp.start()
        for cp in copies:
            cp.wait()

    return pl.pallas_call(
        body, name=name, in_specs=[_ANY] * nb, out_specs=[_ANY] * nb,
        out_shape=_exchange_shapes(bufs, kinds), scratch_shapes=_exchange_sems(nb))(*bufs)


def _gather_two_level(bufs, name):
    nb = len(bufs)

    def body(*refs):
        ins, outs = refs[:nb], refs[nb:2 * nb]
        send, recv, loc = refs[2 * nb:]
        x, y, c, me = _mesh_pos()
        here, sibling = (x, y, c), (x, y, 1 - c)
        chips = [(1 - x, y), (x, 1 - y), (1 - x, 1 - y)]

        def copy(slot, b, block, to, src=None):
            d = 4 * block[0] + 2 * block[1] + block[2]
            return pltpu.make_async_remote_copy(
                src_ref=outs[b].at[d] if src is None else src, dst_ref=outs[b].at[d],
                send_sem=send.at[slot * nb + b], recv_sem=recv.at[slot * nb + b],
                device_id=to, device_id_type=pl.DeviceIdType.MESH)

        mine = [pltpu.make_async_copy(ins[b], outs[b].at[me], loc.at[b]) for b in range(nb)]
        first = [copy(0, b, here, sibling, src=ins[b]) for b in range(nb)]
        first += [copy(1 + j, b, here, (*chip, c), src=ins[b]) for j, chip in enumerate(chips) for b in range(nb)]
        for cp in mine + first:
            cp.start()
        passed = []
        for j, chip in enumerate(chips):
            for b in range(nb):
                copy(1 + j, b, (*chip, c), here).wait_recv()
            onward = [copy(4 + j, b, (*chip, c), sibling) for b in range(nb)]
            for cp in onward:
                cp.start()
            passed += onward
        for b in range(nb):
            copy(0, b, sibling, here).wait_recv()
        for j, chip in enumerate(chips):
            for b in range(nb):
                copy(4 + j, b, (*chip, 1 - c), here).wait_recv()
        for cp in first + passed:
            cp.wait_send()
        for cp in mine:
            cp.wait()

    return pl.pallas_call(
        body, name=name, in_specs=[_ANY] * nb, out_specs=[_ANY] * nb,
        out_shape=_exchange_shapes(bufs, ("all",) * nb), scratch_shapes=_exchange_sems(nb))(*bufs)


def _sum_parts(p_ref):
    acc = p_ref[0].astype(F32)
    for d in range(1, N_DEV):
        acc = acc + p_ref[d].astype(F32)
    return acc


def _adamw_math(g, w, m, v):
    m = ADAM_B1 * m + (1.0 - ADAM_B1) * g
    v = ADAM_B2 * v + (1.0 - ADAM_B2) * (g * g)
    m_hat = m / (1.0 - ADAM_B1 ** ADAM_STEP)
    v_hat = v / (1.0 - ADAM_B2 ** ADAM_STEP)
    return -ADAM_LR * (m_hat / (jnp.sqrt(v_hat) + ADAM_EPS) + ADAM_WD * w), m, v


def _sum8_adamw(parts, w, m, v, name):
    _, R, C = parts.shape
    tb = 256 if R % 256 == 0 else R

    def body(p_ref, w_ref, m_ref, v_ref, g_ref, d_ref, mo_ref, vo_ref):
        g = _sum_parts(p_ref)
        g_ref[...] = g
        d_ref[...], mo_ref[...], vo_ref[...] = _adamw_math(g, w_ref[...], m_ref[...], v_ref[...])

    return pl.pallas_call(
        body, name=name, grid=(R // tb,),
        in_specs=[pl.BlockSpec((N_DEV, tb, C), lambda i: (0, i, 0))] + [_row(tb, C)] * 3, out_specs=[_row(tb, C)] * 4,
        out_shape=[S((R, C), F32)] * 4, compiler_params=_params(("parallel",)))(parts, w, m, v)


SMALL_LAYOUT = (
    ("ln_emb_g", 0, 1024), ("ln_emb_b", 0, 1024), ("ssm_conv_b", 0, 1024), ("ssm_conv_b", 1024, 512),
    ("dt_bias", 0, N_HEADS), ("a_log", 0, N_HEADS), ("d_skip", 0, N_HEADS), ("ssm_norm_g", 0, 1024),
    ("b_glu", 0, 1024), ("b_glu", 1024, 1024), ("conf_conv_b", 0, 1024), ("conf_ln_g", 0, 1024),
    ("conf_ln_b", 0, 1024), ("b_out", 0, 1024), ("ln1_g", 0, 1024), ("ln1_b", 0, 1024), ("ln2_g", 0, 1024),
    ("ln2_b", 0, 1024))
SMALL_ROWS = 24
SMALL = tuple(dict.fromkeys(n for n, _, _ in SMALL_LAYOUT))


LOSS_ROW = len(SMALL_LAYOUT)


def _pack_small(rows, loss):
    def body(*refs):
        o_ref = refs[-1]
        o_ref[...] = jnp.zeros_like(o_ref)
        for r, ref in enumerate(refs[:-2]):
            o_ref[r:r + 1, 0:ref.shape[1]] = ref[...]
        o_ref[LOSS_ROW:LOSS_ROW + 1, 0:LANES] = refs[-2][0:1, :]

    return pl.pallas_call(body, name="pack_small", out_shape=S((SMALL_ROWS, 1024), F32))(*rows, loss)


def _small_update(parts, w, m, v):
    def body(*refs):
        p_ref = refs[0]
        ins = {n: refs[1 + 3 * i:4 + 3 * i] for i, n in enumerate(SMALL)}
        o0 = 1 + 3 * len(SMALL)
        outs = {n: refs[o0 + 4 * i:o0 + 4 * i + 4] for i, n in enumerate(SMALL)}
        gsum = refs[-1]
        gsum[...] = _sum_parts(p_ref)
        refs[-2][...] = gsum[LOSS_ROW:LOSS_ROW + 1, 0:LANES]
        for r, (n, off, wd) in enumerate(SMALL_LAYOUT):
            cs = slice(off, off + wd)
            g = gsum[r:r + 1, 0:wd]
            w_ref, m_ref, v_ref = ins[n]
            g_ref, d_ref, mo_ref, vo_ref = outs[n]
            g_ref[:, cs] = g
            d_ref[:, cs], mo_ref[:, cs], vo_ref[:, cs] = _adamw_math(g, w_ref[:, cs], m_ref[:, cs], v_ref[:, cs])

    args = [parts] + [a for n in SMALL for a in (w[n], m[n], v[n])]
    res = pl.pallas_call(
        body, name="small_update",
        out_shape=[S(w[n].shape, F32) for n in SMALL for _ in range(4)] + [S((1, LANES), F32)],
        scratch_shapes=[pltpu.VMEM((SMALL_ROWS, 1024), F32)])(*args)
    return tuple({n: res[4 * i + j] for i, n in enumerate(SMALL)} for j in range(4)) + (res[-1],)


LATE = ("w_out", "w_ple_gate", "w_ple_proj")


def _local_step(x, p, tgt, W, late=None):
    r1 = lambda v: v.reshape(1, -1).astype(F32)
    pad_l = lambda v: jnp.pad(r1(v), ((0, 0), (0, LANES - v.size)))
    w_main, w_dt = _w_in_to_main(W["w_in"])
    scw, scb = W["ssm_conv_w"], r1(W["ssm_conv_b"])
    wx, wb, bx, bb = scw[:, :1024], scw[:, 1024:], scb[:, :1024], scb[:, 1024:]
    dt_bias, alog = pad_l(W["dt_bias"]), pad_l(W["a_log"])
    dskip_row = jnp.repeat(W["d_skip"].reshape(-1), HEAD).reshape(1, -1)
    norm_g = r1(W["ssm_norm_g"])
    bglu = r1(W["b_glu"])
    ba, bbg = bglu[:, :1024], bglu[:, 1024:]
    ccw, ccb, clg, clb = W["conf_conv_w"], r1(W["conf_conv_b"]), r1(W["conf_ln_g"]), r1(W["conf_ln_b"])

    h0, h0b = _ln_emb_fwd(x, r1(W["ln_emb_g"]), r1(W["ln_emb_b"]))
    if late is None:
        proj = _mm(h0b, w_main, "nn", "in_proj")
    else:
        proj, *gathered = _mm(h0b, w_main, "nn", "in_proj", exchange=(late, ("all",) * len(LATE)))
        W = dict(W, **{n: _unstack_shards(a, BY_COLS[n]) for n, a in zip(LATE, gathered)})
    dt_raw = _mm(h0b, w_dt, "nn", "in_proj_dt")
    xs_c, bc_c, dt = _ssd_pre_fwd(proj, dt_raw, wx, wb, bx, bb, dt_bias)
    mix, ypre, hprev = _ssd_fwd(xs_c, bc_c, dt, proj, alog, dskip_row, norm_g)
    u1, mix = _conf_fwd(mix, proj, ccw, ccb, clg, clb, ba, bbg)
    out = _mm(mix, W["w_out"], "nn", "out_proj")
    h1, h1b = _post1_fwd(h0, out, r1(W["b_out"]), r1(W["ln1_g"]), r1(W["ln1_b"]))
    gpre = _mm(h1b, W["w_ple_gate"], "nn", "ple_gate")
    pb = p.astype(BF16)
    ple = _mm(pb, W["w_ple_proj"], "nn", "ple_proj")
    dh1a, dgp, dple, loss, dln2g, dln2b = _post2(h1, gpre, ple, tgt, r1(W["ln2_g"]), r1(W["ln2_b"]))

    g = {}
    g["w_ple_proj"] = _mm(pb.T, dple, "nn", "d_ple_proj", out_dtype=BF16)
    g["w_ple_gate"] = _mm(h1b.T, dgp, "nn", "d_ple_gate", out_dtype=BF16)
    dh1b = _mm(dgp, W["w_ple_gate"], "nt", "d_h1")
    dout, dh0a, dln1g, dln1b, dbout = _post1_bwd(dh1a, dh1b, h0, out, r1(W["b_out"]), r1(W["ln1_g"]))
    g["w_out"] = _mm(mix.T, dout, "nn", "d_w_out", out_dtype=BF16)
    dmix = _mm(dout, W["w_out"], "nt", "d_mix")
    du1, dproj, dclg, dclb = _conf_bwd1(dmix, u1, proj, clg, clb)
    dproj, g["conf_conv_w"], dccb, dba, dbb = _conf_bwd2(dproj, proj, du1, ccw, ba, bbg)
    dxs_c, dbc_c, ddt, dproj, dng, ddsk, dalog = _ssd_bwd(
        dproj, xs_c, bc_c, dt, proj, ypre, hprev, dmix, alog, dskip_row, norm_g)
    dproj, dwx, dbx = _ssd_conv_bwd(dproj, proj, dxs_c, wx, bx, 1024, C_XS, "ssd_conv_bwd_x")
    dproj, dwb, dbb2 = _ssd_conv_bwd(dproj, proj, dbc_c, wb, bb, 512, C_BC, "ssd_conv_bwd_bc")
    ddtr, ddtb = _dt_bwd(ddt, dt_raw, dt_bias)
    g["ssm_conv_w"] = jnp.concatenate([dwx, dwb], axis=1)
    h0bt = h0b.T
    dw_dt = _mm(h0bt, ddtr, "nn", "d_w_dt", out_dtype=BF16)
    stack = lambda names: [_stack_shards(g[n], BY_COLS[n]) for n in names]
    last_args = (dproj, w_main, ddtr, w_dt, dh0a, x, r1(W["ln_emb_g"]))
    if late is None:
        g["w_in"] = _w_in_blocks(_mm(h0bt, dproj, "nn", "d_w_in", out_dtype=BF16), dw_dt)
        grad_x, dlng, dlnb = _d_h0_ln_bwd(*last_args)
    else:
        dw_main, *recv_a = _mm(h0bt, dproj, "nn", "d_w_in", out_dtype=BF16,
                               exchange=(stack(LATE), ("blocks",) * len(LATE)))
        last = ("ssm_conv_w", "conf_conv_w")
        blocks = [_w_in_blocks(dw_main, dw_dt)] + stack(last)
        grad_x, dlng, dlnb, *recv_b = _d_h0_ln_bwd(*last_args, exchange=(blocks, ("blocks",) * 3))
        g["recv"] = dict(zip(LATE + ("w_in",) + last, recv_a + recv_b))
    g["rows"] = [dlng, dlnb, dbx, dbb2, ddtb, dalog, ddsk, dng, dba, dbb, dccb, dclg, dclb, dbout, dln1g, dln1b,
                 dln2g, dln2b]
    return loss, grad_x, g


W_IN_SEGMENTS = ((0, 2048, 2048), (2048, 5120, 512), (2560, None, N_HEADS), (2576, 0, 2048), (4624, 4096, 1024))


def _w_in_to_main(shards):
    def pieces(p0, width):
        out, p = [], p0
        while p < p0 + width:
            d = p // COLS_PER_DEV
            hi = min(p0 + width, (d + 1) * COLS_PER_DEV)
            out.append(shards[d][:, p - d * COLS_PER_DEV:hi - d * COLS_PER_DEV])
            p = hi
        return out
    main = [s for s in sorted(W_IN_SEGMENTS, key=lambda s: -1 if s[1] is None else s[1]) if s[1] is not None]
    w_main = jnp.concatenate([q for p0, _, width in main for q in pieces(p0, width)], axis=1)
    w_dt = jnp.concatenate(pieces(2560, N_HEADS), axis=1)
    return w_main, jnp.pad(w_dt, ((0, 0), (0, LANES - N_HEADS)))


def _w_in_blocks(dw_main, dw_dt):
    blocks = []
    for d in range(N_DEV):
        lo_d, hi_d = d * COLS_PER_DEV, (d + 1) * COLS_PER_DEV
        parts = []
        for p0, m0, width in W_IN_SEGMENTS:
            lo, hi = max(lo_d, p0), min(hi_d, p0 + width)
            if lo < hi:
                parts.append(dw_dt[:, lo - p0:hi - p0] if m0 is None else dw_main[:, m0 + lo - p0:m0 + hi - p0])
        blocks.append(jnp.concatenate(parts, axis=1))
    return jnp.stack(blocks)


WEIGHTS = ['ln_emb_g', 'ln_emb_b', 'w_in', 'ssm_conv_w', 'ssm_conv_b', 'dt_bias', 'a_log', 'd_skip', 'ssm_norm_g',
           'b_glu', 'conf_conv_w', 'conf_conv_b', 'conf_ln_g', 'conf_ln_b', 'w_out', 'b_out', 'ln1_g', 'ln1_b',
           'w_ple_gate', 'w_ple_proj', 'ln2_g', 'ln2_b']
SHARDED = (("w_in", True), ("w_out", False), ("w_ple_gate", False), ("w_ple_proj", True), ("ssm_conv_w", True),
           ("conf_conv_w", True))
BY_COLS = dict(SHARDED)


def _stack_shards(a, by_cols):
    if by_cols:
        return a.reshape(a.shape[0], N_DEV, a.shape[1] // N_DEV).transpose(1, 0, 2)
    return a.reshape(N_DEV, a.shape[0] // N_DEV, a.shape[1])


def _unstack_shards(a, by_cols):
    if by_cols:
        return a.transpose(1, 0, 2).reshape(a.shape[1], N_DEV * a.shape[2])
    return a.reshape(N_DEV * a.shape[1], a.shape[2])


def kernel(x, p, ln_emb_g, ln_emb_b, w_in, ssm_conv_w, ssm_conv_b, dt_bias, a_log, d_skip, ssm_norm_g, b_glu, conf_conv_w, conf_conv_b, conf_ln_g, conf_ln_b, w_out, b_out, ln1_g, ln1_b, w_ple_gate, w_ple_proj, ln2_g, ln2_b, loss_target, m_ln_emb_g, m_ln_emb_b, m_w_in, m_ssm_conv_w, m_ssm_conv_b, m_dt_bias, m_a_log, m_d_skip, m_ssm_norm_g, m_b_glu, m_conf_conv_w, m_conf_conv_b, m_conf_ln_g, m_conf_ln_b, m_w_out, m_b_out, m_ln1_g, m_ln1_b, m_w_ple_gate, m_w_ple_proj, m_ln2_g, m_ln2_b, v_ln_emb_g, v_ln_emb_b, v_w_in, v_ssm_conv_w, v_ssm_conv_b, v_dt_bias, v_a_log, v_d_skip, v_ssm_norm_g, v_b_glu, v_conf_conv_w, v_conf_conv_b, v_conf_ln_g, v_conf_ln_b, v_w_out, v_b_out, v_ln1_g, v_ln1_b, v_w_ple_gate, v_w_ple_proj, v_ln2_g, v_ln2_b):
    loc = dict(locals())
    w = {n: loc[n] for n in WEIGHTS}
    m = {n: loc["m_" + n] for n in WEIGHTS}
    v = {n: loc["v_" + n] for n in WEIGHTS}
    sharded = [n for n, _ in SHARDED]

    local = {n: w[n][0].astype(BF16) if n.startswith("w_") else w[n][0] for n in sharded}
    first = [n for n in sharded if n not in LATE]
    W = {n: w[n].reshape(-1) for n in SMALL}
    for n, a in zip(first, _gather_two_level([local[n] for n in first], "gather_first")):
        W[n] = a if n == "w_in" else _unstack_shards(a, BY_COLS[n])

    loss, grad_x, g = _local_step(x[0], p[0, 0], loss_target[0], W, late=[local[n] for n in LATE])
    (recv_small,) = _exchange([_pack_small(g["rows"], loss)], ("all",), "small_exchange")

    grads, delta, new_m, new_v = {}, {}, {}, {}
    for n in sharded:
        res = _sum8_adamw(g["recv"][n], w[n][0], m[n][0], v[n][0], "adamw_" + n)
        grads[n], delta[n], new_m[n], new_v[n] = (r[None] for r in res)
    two_d = lambda d: {n: d[n].reshape(1, -1) for n in SMALL}
    *small, loss = _small_update(recv_small, two_d(w), two_d(m), two_d(v))
    for dst, res in zip((grads, delta, new_m, new_v), small):
        for n in SMALL:
            dst[n] = res[n].reshape(w[n].shape)
    return (loss[0, 0], grad_x[None], *[grads[n] for n in WEIGHTS], *[delta[n] for n in WEIGHTS],
            *[new_m[n] for n in WEIGHTS], *[new_v[n] for n in WEIGHTS])
```

```python
import functools

import numpy as np
import jax
import jax.numpy as jnp
from jax import lax
from jax.experimental import pallas as pl
from jax.experimental.pallas import tpu as pltpu

F32, BF16 = jnp.float32, jnp.bfloat16
S = jax.ShapeDtypeStruct

N_DEV = 8
D = 1024
D_PLE = 256
D_SSM = 1024
D_CONF = 1024
N_HEADS = 16
HEAD = 64
N_STATE = 128
CHUNK = 128
K_SSM = 4
K_CONF = 31
D_IN = 5648
COLS_PER_DEV = D_IN // N_DEV
LN_EPS = 1e-5
RMS_EPS = 1e-5
ALPHA = 2.0 ** 0.25
LANES = 128
TB = 256
RG = 32
ROW_UNROLL = 4
HALO_SSM = 8
HALO_CONF = 32
VMEM_LIMIT = 56 * 1024 * 1024

ADAM_LR, ADAM_B1, ADAM_B2, ADAM_EPS, ADAM_WD, ADAM_STEP = 0.001, 0.9, 0.999, 1e-08, 0.01, 10

C_GLUA, C_GLUB, C_XS, C_Z, C_CG = 0, 1, 2, 3, 4
C_BC = 10
N_MAIN = 5632


def _params(sem, vmem=VMEM_LIMIT):
    return pltpu.CompilerParams(dimension_semantics=sem, vmem_limit_bytes=vmem)


def _row(tb, n, col=0):
    return pl.BlockSpec((tb, n), lambda i: (i, col))


def _colt(n, tb, row=0):
    return pl.BlockSpec((n, tb), lambda i: (row, i))


def _full(shape):
    return pl.BlockSpec(shape, lambda i: (0,) * len(shape))


_ANY = pl.BlockSpec(memory_space=pl.ANY)


def _prev(tb, halo, n, col=0):
    r = tb // halo
    return pl.BlockSpec((halo, n), lambda i: (jnp.maximum(i * r - 1, 0), col))


def _next(tb, halo, n, nt, col=0):
    r = tb // halo
    return pl.BlockSpec((halo, n), lambda i: (jnp.minimum((i + 1) * r, nt * r - 1), col))


def _row_loop(tb, fn):
    def it(r, c):
        fn(pl.ds(pl.multiple_of(r * RG, RG), RG))
        return c
    lax.fori_loop(0, tb // RG, it, 0, unroll=ROW_UNROLL)


def _col_loop(n, fn):
    def it(j, c):
        fn(pl.ds(pl.multiple_of(j * LANES, LANES), LANES))
        return c
    lax.fori_loop(0, n // LANES, it, 0)


def _sigmoid(x):
    return 1.0 / (1.0 + jnp.exp(-x))


def _dsilu(x, s):
    return s * (1.0 + x * (1.0 - s))


def _ln_stats(v):
    mu = jnp.mean(v, axis=-1, keepdims=True)
    c = v - mu
    r = lax.rsqrt(jnp.mean(c * c, axis=-1, keepdims=True) + LN_EPS)
    return c * r, r


def _ln_bwd(dy, xhat, r, g):
    dxh = dy * g
    dv = r * (dxh - jnp.mean(dxh, axis=-1, keepdims=True) - xhat * jnp.mean(dxh * xhat, axis=-1, keepdims=True))
    return dv, jnp.sum(dy * xhat, axis=0, keepdims=True), jnp.sum(dy, axis=0, keepdims=True)


def _dot(a, b, dims=((1,), (0,))):
    return lax.dot_general(a.astype(BF16), b.astype(BF16), (dims, ((), ())), preferred_element_type=F32)


_NT = ((1,), (1,))
_TN = ((0,), (0,))


def _split3(x):
    hi = x.astype(BF16)
    r = x - hi.astype(F32)
    mid = r.astype(BF16)
    return hi, mid, (r - mid.astype(F32)).astype(BF16)


def _dot_sel_b(a, b, dims=((1,), (0,))):
    hi, mid, lo = _split3(a)
    return (_dot(lo, b, dims) + _dot(mid, b, dims)) + _dot(hi, b, dims)


def _dot_sel_a(a, b, dims=((1,), (0,))):
    hi, mid, lo = _split3(b)
    return (_dot(a, lo, dims) + _dot(a, mid, dims)) + _dot(a, hi, dims)


def _mm(a, b, mode, name, out_dtype=F32, add=None, tm=1024, tn=None, tk=1024, exchange=None):
    if mode == "nn":
        (M, K), N = a.shape, b.shape[1]
    elif mode == "tn":
        (K, M), N = a.shape, b.shape[1]
    else:
        (M, K), N = a.shape, b.shape[0]
    if tn is None:
        tn = next(t for t in (1024, 1408, 512, 256, LANES) if N % t == 0)
    tm, tn, tk = min(tm, M), min(tn, N), min(tk, K)
    assert M % tm == 0 and N % tn == 0 and K % tk == 0, (name, M, N, K)
    grid = (M // tm, N // tn, K // tk)
    nk = grid[2]
    dims = {"nn": ((1,), (0,)), "tn": _TN, "nt": _NT}[mode]
    n_in = 2 + (add is not None)
    xbufs, kinds = exchange if exchange is not None else ((), ())
    nx = len(xbufs)

    def body(*refs):
        a_ref, b_ref = refs[:2]
        o_ref = refs[n_in + nx]
        acc = refs[n_in + 2 * nx + 1]
        i, j, k = pl.program_id(0), pl.program_id(1), pl.program_id(2)
        if nx:
            copies = _exchange_copies(refs[n_in:n_in + nx], refs[n_in + nx + 1:n_in + 2 * nx + 1], kinds,
                                      *refs[n_in + 2 * nx + 2:])

            @pl.when((i == 0) & (j == 0) & (k == 0))
            def _():
                for cp in copies:
                    cp.start()

        d = _dot(a_ref[...], b_ref[...], dims)

        def finish(r):
            if add is not None:
                r = r + refs[2][...]
            o_ref[...] = r.astype(out_dtype)

        if nk == 1:
            finish(d)
        else:
            @pl.when(k == 0)
            def _():
                acc[...] = d

            @pl.when((k > 0) & (k < nk - 1))
            def _():
                acc[...] += d

            @pl.when(k == nk - 1)
            def _():
                finish(acc[...] + d)

        if nx:
            @pl.when((i == grid[0] - 1) & (j == grid[1] - 1) & (k == nk - 1))
            def _():
                for cp in copies:
                    cp.wait()

    a_spec = pl.BlockSpec((tk, tm), lambda i, j, k: (k, i)) if mode == "tn" else pl.BlockSpec((tm, tk), lambda i, j, k: (i, k))
    b_spec = pl.BlockSpec((tn, tk), lambda i, j, k: (j, k)) if mode == "nt" else pl.BlockSpec((tk, tn), lambda i, j, k: (k, j))
    o_spec = pl.BlockSpec((tm, tn), lambda i, j, k: (i, j))
    ins, specs = [a, b], [a_spec, b_spec]
    if add is not None:
        ins.append(add)
        specs.append(o_spec)
    acc_spec = pltpu.VMEM((tm, tn) if nk > 1 else (8, LANES), F32)
    if not nx:
        return pl.pallas_call(
            body, name=name, grid=grid, in_specs=specs, out_specs=o_spec,
            out_shape=S((M, N), out_dtype), scratch_shapes=[acc_spec],
            compiler_params=_params(("parallel", "parallel", "arbitrary")))(*ins)
    return pl.pallas_call(
        body, name=name, grid=grid, in_specs=specs + [_ANY] * nx, out_specs=[o_spec] + [_ANY] * nx,
        out_shape=[S((M, N), out_dtype)] + _exchange_shapes(xbufs, kinds),
        scratch_shapes=[acc_spec] + _exchange_sems(nx),
        compiler_params=_params(("arbitrary", "arbitrary", "arbitrary")))(*ins, *xbufs)


def _ln_emb_fwd(x, g, b):
    T = x.shape[0]

    def body(x_ref, g_ref, b_ref, h_ref, hb_ref, hbt_ref):
        def rows(rs):
            xh, _ = _ln_stats(x_ref[rs, :])
            h = xh * g_ref[...] + b_ref[...]
            h_ref[rs, :] = h
            hb_ref[rs, :] = h.astype(BF16)
        _row_loop(TB, rows)
        hbt_ref[...] = hb_ref[...].T

    return pl.pallas_call(
        body, name="ln_emb_fwd", grid=(T // TB,),
        in_specs=[_row(TB, D), _full((1, D)), _full((1, D))], out_specs=[_row(TB, D), _row(TB, D), _colt(D, TB)],
        out_shape=[S((T, D), F32), S((T, D), BF16), S((D, T), BF16)], compiler_params=_params(("parallel",)))(x, g, b)


def _post1_fwd(h0, out, b_out, g, b):
    T = h0.shape[0]

    def body(h0_ref, out_ref, bo_ref, g_ref, b_ref, h_ref, hb_ref, hbt_ref):
        def rows(rs):
            xh, _ = _ln_stats(ALPHA * h0_ref[rs, :] + out_ref[rs, :] + bo_ref[...])
            h = xh * g_ref[...] + b_ref[...]
            h_ref[rs, :] = h
            hb_ref[rs, :] = h.astype(BF16)
        _row_loop(TB, rows)
        hbt_ref[...] = hb_ref[...].T

    return pl.pallas_call(
        body, name="post1_fwd", grid=(T // TB,),
        in_specs=[_row(TB, D), _row(TB, D)] + [_full((1, D))] * 3, out_specs=[_row(TB, D), _row(TB, D), _colt(D, TB)],
        out_shape=[S((T, D), F32), S((T, D), BF16), S((D, T), BF16)],
        compiler_params=_params(("parallel",)))(h0, out, b_out, g, b)


def _post2(h1, gpre, ple, tgt, g, b):
    T = h1.shape[0]

    def body(h1_ref, gp_ref, ple_ref, tgt_ref, g_ref, b_ref, dh1_ref, dgp_ref, dple_ref, loss_ref, dg_ref, db_ref):
        @pl.when(pl.program_id(0) == 0)
        def _():
            loss_ref[...] = jnp.zeros_like(loss_ref)
            dg_ref[...] = jnp.zeros_like(dg_ref)
            db_ref[...] = jnp.zeros_like(db_ref)

        def rows(rs):
            gate = _sigmoid(gp_ref[rs, :])
            ple = ple_ref[rs, :]
            xh, r = _ln_stats(ALPHA * h1_ref[rs, :] + gate * ple)
            err = xh * g_ref[...] + b_ref[...] - tgt_ref[rs, :]
            loss_ref[...] += 0.5 * jnp.sum(jnp.mean(err * err, axis=-1, keepdims=True), axis=0, keepdims=True)
            dv, dg, db = _ln_bwd(err * (1.0 / D), xh, r, g_ref[...])
            dg_ref[...] += dg
            db_ref[...] += db
            dh1_ref[rs, :] = ALPHA * dv
            dgp_ref[rs, :] = (dv * ple * gate * (1.0 - gate)).astype(BF16)
            dple_ref[rs, :] = (dv * gate).astype(BF16)
        _row_loop(TB, rows)

    return pl.pallas_call(
        body, name="post2", grid=(T // TB,),
        in_specs=[_row(TB, D)] * 4 + [_full((1, D))] * 2,
        out_specs=[_row(TB, D)] * 3 + [_full((8, LANES)), _full((1, D)), _full((1, D))],
        out_shape=[S((T, D), F32), S((T, D), BF16), S((T, D), BF16), S((8, LANES), F32), S((1, D), F32), S((1, D), F32)],
        compiler_params=_params(("arbitrary",)))(h1, gpre, ple, tgt, g, b)


def _post1_bwd(dh1a, dh1b, h0, out, b_out, g):
    T = h0.shape[0]

    def body(da_ref, db2_ref, h0_ref, out_ref, bo_ref, g_ref, dout_ref, dh0_ref, dg_ref, db_ref, dbo_ref):
        @pl.when(pl.program_id(0) == 0)
        def _():
            dg_ref[...] = jnp.zeros_like(dg_ref)
            db_ref[...] = jnp.zeros_like(db_ref)
            dbo_ref[...] = jnp.zeros_like(dbo_ref)

        def rows(rs):
            xh, r = _ln_stats(ALPHA * h0_ref[rs, :] + out_ref[rs, :] + bo_ref[...])
            dv, dg, db = _ln_bwd(da_ref[rs, :] + db2_ref[rs, :], xh, r, g_ref[...])
            dg_ref[...] += dg
            db_ref[...] += db
            dbo_ref[...] += jnp.sum(dv, axis=0, keepdims=True)
            dout_ref[rs, :] = dv.astype(BF16)
            dh0_ref[rs, :] = ALPHA * dv
        _row_loop(TB, rows)

    return pl.pallas_call(
        body, name="post1_bwd", grid=(T // TB,),
        in_specs=[_row(TB, D)] * 4 + [_full((1, D))] * 2,
        out_specs=[_row(TB, D)] * 2 + [_full((1, D))] * 3,
        out_shape=[S((T, D), BF16), S((T, D), F32)] + [S((1, D), F32)] * 3,
        compiler_params=_params(("arbitrary",)))(dh1a, dh1b, h0, out, b_out, g)


def _d_h0_ln_bwd(dproj, w_main, ddtr, w_dt, dh0a, x, g, exchange=None, tm=1024, tk=1408):
    T, K = dproj.shape
    tm = min(tm, T)
    assert T % tm == 0 and K % tk == 0
    ni, nk = T // tm, K // tk
    xbufs, kinds = exchange if exchange is not None else ((), ())
    nx = len(xbufs)

    def body(*refs):
        dp_ref, w_ref, dt_ref, wdt_ref, da_ref, x_ref, g_ref = refs[:7]
        dx_ref, dg_ref, db_ref = refs[7 + nx:10 + nx]
        acc = refs[10 + 2 * nx]
        i, k = pl.program_id(0), pl.program_id(1)
        if nx:
            copies = _exchange_copies(refs[7:7 + nx], refs[10 + nx:10 + 2 * nx], kinds, *refs[11 + 2 * nx:])

            @pl.when((i == 0) & (k == 0))
            def _():
                for cp in copies:
                    cp.start()

        @pl.when((i == 0) & (k == 0))
        def _():
            dg_ref[...] = jnp.zeros_like(dg_ref)
            db_ref[...] = jnp.zeros_like(db_ref)

        d = _dot(dp_ref[...], w_ref[...], _NT)

        @pl.when(k == 0)
        def _():
            acc[...] = da_ref[...] + _dot(dt_ref[...], wdt_ref[...], _NT) + d

        @pl.when(k > 0)
        def _():
            acc[...] += d

        @pl.when(k == nk - 1)
        def _():
            def rows(rs):
                xh, r = _ln_stats(x_ref[rs, :])
                dv, dg, db = _ln_bwd(acc[rs, :], xh, r, g_ref[...])
                dg_ref[...] += dg
                db_ref[...] += db
                dx_ref[rs, :] = dv
            _row_loop(tm, rows)

        if nx:
            @pl.when((i == ni - 1) & (k == nk - 1))
            def _():
                for cp in copies:
                    cp.wait()

    rowt = lambda n: pl.BlockSpec((tm, n), lambda i, k: (i, 0))
    const = lambda shape: pl.BlockSpec(shape, lambda i, k: (0, 0))
    return pl.pallas_call(
        body, name="d_h0_ln_bwd", grid=(ni, nk),
        in_specs=[pl.BlockSpec((tm, tk), lambda i, k: (i, k)), pl.BlockSpec((D, tk), lambda i, k: (0, k)),
                  rowt(LANES), const((D, LANES)), rowt(D), rowt(D), const((1, D))] + [_ANY] * nx,
        out_specs=[rowt(D), const((1, D)), const((1, D))] + [_ANY] * nx,
        out_shape=[S((T, D), F32), S((1, D), F32), S((1, D), F32)] + _exchange_shapes(xbufs, kinds),
        scratch_shapes=[pltpu.VMEM((tm, D), F32)] + (_exchange_sems(nx) if nx else []),
        compiler_params=_params(("arbitrary", "arbitrary")))(dproj, w_main, ddtr, w_dt, dh0a, x, g, *xbufs)


def _softplus(x):
    return jnp.maximum(x, 0.0) + jnp.log1p(jnp.exp(-jnp.abs(x)))


def _ssd_pre_fwd(proj, dt_raw, wx, wb, bx, bb, dt_bias):
    T = proj.shape[0]
    H = HALO_SSM

    def body(xs_ref, xsp_ref, bc_ref, bcp_ref, dtr_ref, wx_ref, wb_ref, bx_ref, bb_ref, dtb_ref,
             xso_ref, bco_ref, dto_ref, extx, extb):
        first = pl.program_id(0) == 0

        def conv(t_ref, p_ref, w_ref, b_ref, o_ref, ext, n):
            def blk(cols):
                ext[0:H, cols] = jnp.where(first, 0.0, p_ref[:, cols])
                ext[H:, cols] = t_ref[:, cols]
                for r0 in range(0, TB, 64):
                    acc = jnp.broadcast_to(b_ref[:, cols], (64, LANES))
                    for k in range(K_SSM):
                        acc = acc + w_ref[k:k + 1, cols] * ext[pl.ds(r0 + H - (K_SSM - 1) + k, 64), cols]
                    o_ref[pl.ds(r0, 64), cols] = acc * _sigmoid(acc)
            _col_loop(n, blk)

        conv(xs_ref, xsp_ref, wx_ref, bx_ref, xso_ref, extx, D_SSM)
        conv(bc_ref, bcp_ref, wb_ref, bb_ref, bco_ref, extb, 512)
        dto_ref[...] = _softplus(dtr_ref[...] + dtb_ref[...])

    return pl.pallas_call(
        body, name="ssd_pre_fwd", grid=(T // TB,),
        in_specs=[_row(TB, 1024, C_XS), _prev(TB, H, 1024, C_XS), _row(TB, 512, C_BC), _prev(TB, H, 512, C_BC),
                  _row(TB, LANES), _full((K_SSM, 1024)), _full((K_SSM, 512)), _full((1, 1024)), _full((1, 512)),
                  _full((1, LANES))],
        out_specs=[_row(TB, 1024), _row(TB, 512), _row(TB, LANES)],
        out_shape=[S((T, 1024), F32), S((T, 512), F32), S((T, LANES), F32)],
        scratch_shapes=[pltpu.VMEM((H + TB, 1024), F32), pltpu.VMEM((H + TB, 512), F32)],
        compiler_params=_params(("parallel",)))(proj, proj, proj, proj, dt_raw, wx, wb, bx, bb, dt_bias)


def _ssd_conv_bwd(dproj, proj, d_c, w, b, n, col, name):
    T = proj.shape[0]
    nt = T // TB
    H = HALO_SSM
    R = TB + H

    def body(dproj_ref, t_ref, p_ref, n_ref, d_ref, dn_ref, w_ref, b_ref, o_ref, dw_ref, dbias_ref, ext, dp):
        i = pl.program_id(0)
        first, last = i == 0, i == nt - 1

        @pl.when(first)
        def _():
            dw_ref[...] = jnp.zeros_like(dw_ref)
            dbias_ref[...] = jnp.zeros_like(dbias_ref)

        def blk(cols):
            ext[0:H, cols] = jnp.where(first, 0.0, p_ref[:, cols])
            ext[H:H + TB, cols] = t_ref[:, cols]
            ext[H + TB:, cols] = n_ref[:, cols]
            pre = jnp.broadcast_to(b_ref[:, cols], (R, LANES))
            for k in range(K_SSM):
                pre = pre + w_ref[k:k + 1, cols] * ext[pl.ds(H - (K_SSM - 1) + k, R), cols]
            s = _sigmoid(pre)
            ds = _dsilu(pre, s)
            dp[0:TB, cols] = d_ref[:, cols] * ds[0:TB]
            dp[TB:, cols] = jnp.where(last, 0.0, dn_ref[:, cols] * ds[TB:])
            dpt = dp[0:TB, cols]
            dbias_ref[:, cols] += jnp.sum(dpt, axis=0, keepdims=True)
            acc = jnp.zeros((TB, LANES), F32)
            for k in range(K_SSM):
                dw_ref[k:k + 1, cols] += jnp.sum(dpt * ext[pl.ds(H - (K_SSM - 1) + k, TB), cols], axis=0, keepdims=True)
                acc = acc + w_ref[k:k + 1, cols] * dp[pl.ds(K_SSM - 1 - k, TB), cols]
            o_ref[:, cols] = acc.astype(BF16)
        _col_loop(n, blk)

    return pl.pallas_call(
        body, name=name, grid=(nt,),
        in_specs=[_ANY, _row(TB, n, col), _prev(TB, H, n, col), _next(TB, H, n, nt, col),
                  _row(TB, n), _next(TB, H, n, nt), _full((K_SSM, n)), _full((1, n))],
        out_specs=[_row(TB, n, col), _full((K_SSM, n)), _full((1, n))],
        out_shape=[S(dproj.shape, BF16), S((K_SSM, n), F32), S((1, n), F32)],
        input_output_aliases={0: 0},
        scratch_shapes=[pltpu.VMEM((H + TB + H, n), F32), pltpu.VMEM((R, n), F32)],
        compiler_params=_params(("arbitrary",)))(dproj, proj, proj, proj, d_c, d_c, w, b)


def _dt_bwd(ddt, dt_raw, dt_bias):
    T = ddt.shape[0]

    def body(ddt_ref, dtr_ref, dtb_ref, o_ref, db_ref):
        @pl.when(pl.program_id(0) == 0)
        def _():
            db_ref[...] = jnp.zeros_like(db_ref)

        g = ddt_ref[...] * _sigmoid(dtr_ref[...] + dtb_ref[...])
        o_ref[...] = g.astype(BF16)
        db_ref[...] += jnp.sum(g, axis=0, keepdims=True)

    return pl.pallas_call(
        body, name="dt_bwd", grid=(T // TB,),
        in_specs=[_row(TB, LANES), _row(TB, LANES), _full((1, LANES))], out_specs=[_row(TB, LANES), _full((1, LANES))],
        out_shape=[S((T, LANES), BF16), S((1, LANES), F32)],
        compiler_params=_params(("arbitrary",)))(ddt, dt_raw, dt_bias)


def _ssd_consts():
    ex = np.zeros((LANES, D_SSM), np.float32)
    for h in range(N_HEADS):
        ex[h, h * HEAD:(h + 1) * HEAD] = 1.0
    tri = np.tril(np.ones((CHUNK, CHUNK), np.float32))
    return jnp.asarray(ex), jnp.asarray(ex.T.copy()), jnp.asarray(tri), jnp.asarray(tri.T.copy())


def _ssd_common(xs, dt, alog_ref, ex_ref, tri_ref):
    lane = lax.broadcasted_iota(jnp.int32, (1, LANES), 1)
    a = jnp.where(lane < N_HEADS, -jnp.exp(alog_ref[...]), 0.0)
    A = _dot_sel_a(tri_ref[...], dt * a)
    ex = ex_ref[...]
    Aex = _dot_sel_b(A, ex)
    dtex = _dot_sel_b(dt, ex)
    expA = jnp.exp(Aex)
    dec = jnp.exp(Aex[CHUNK - 1:CHUNK, :] - Aex)
    cd = _dot_sel_a(ex, jnp.broadcast_to(jnp.exp(A.T[:, CHUNK - 1:CHUNK]), (LANES, LANES)), _TN)
    return a, A, dtex, expA, dec, cd


def _decay_mask():
    sub = lax.broadcasted_iota(jnp.int32, (CHUNK, CHUNK), 0)
    lane = lax.broadcasted_iota(jnp.int32, (CHUNK, CHUNK), 1)
    return sub, lane, sub >= lane


def _ssd_fwd(xs_c, bc_c, dt, proj, alog, dskip_row, norm_g):
    T = xs_c.shape[0]
    nc = T // CHUNK
    ex, _, tri, _ = _ssd_consts()

    def body(xs_ref, bc_ref, dt_ref, z_ref, alog_ref, dsk_ref, ng_ref, ex_ref, tri_ref,
             ys_ref, ypre_ref, hprev_ref, yst_ref, Hs, ybuf):
        @pl.when(pl.program_id(0) == 0)
        def _():
            Hs[...] = jnp.zeros_like(Hs)

        hprev_ref[0] = Hs[...]
        xs, dt = xs_ref[...], dt_ref[...]
        a, A, dtex, expA, dec, cd = _ssd_common(xs, dt, alog_ref, ex_ref, tri_ref)
        AT = A.T
        xdt = xs * dtex
        xdec = xdt * dec
        _, _, causal = _decay_mask()
        for g in range(2):
            gs = slice(g * 512, (g + 1) * 512)
            B = bc_ref[:, g * N_STATE:(g + 1) * N_STATE]
            C = bc_ref[:, 256 + g * N_STATE:256 + (g + 1) * N_STATE]
            cb = _dot(C, B, _NT)
            Hg = Hs[gs, :]
            yoff = _dot(C, Hg, _NT) * expA[:, gs]
            for j in range(8):
                h = g * 8 + j
                hs = slice(h * HEAD, (h + 1) * HEAD)
                L = jnp.exp(jnp.where(causal, A[:, h:h + 1] - AT[h:h + 1, :], -1e30))
                ybuf[:, hs] = _dot(cb * L, xdt[:, hs]) + yoff[:, j * HEAD:(j + 1) * HEAD]
            Hs[gs, :] = cd[gs, :] * Hg + _dot(xdec[:, gs], B, _TN)
        ypre = ybuf[...] + dsk_ref[...] * xs
        ypre_ref[...] = ypre
        z = z_ref[...]
        yz = ypre * (z * _sigmoid(z))
        for g in range(2):
            gs = slice(g * 512, (g + 1) * 512)
            v = yz[:, gs]
            r = lax.rsqrt(jnp.mean(v * v, axis=-1, keepdims=True) + RMS_EPS)
            ys_ref[:, gs] = (v * r * ng_ref[:, gs]).astype(BF16)
        yst_ref[...] = ys_ref[...].T

    return pl.pallas_call(
        body, name="ssd_fwd", grid=(nc,),
        in_specs=[_row(CHUNK, 1024), _row(CHUNK, 512), _row(CHUNK, LANES), _row(CHUNK, 1024, C_Z),
                  _full((1, LANES)), _full((1, 1024)), _full((1, 1024)), _full((LANES, 1024)), _full((CHUNK, CHUNK))],
        out_specs=[_row(CHUNK, 1024), _row(CHUNK, 1024), pl.BlockSpec((1, 1024, N_STATE), lambda c: (c, 0, 0)),
                   _colt(1024, CHUNK)],
        out_shape=[S((T, 2048), BF16), S((T, 1024), F32), S((nc, 1024, N_STATE), F32), S((2048, T), BF16)],
        scratch_shapes=[pltpu.VMEM((1024, N_STATE), F32), pltpu.VMEM((CHUNK, 1024), F32)],
        compiler_params=_params(("arbitrary",)))(xs_c, bc_c, dt, proj, alog, dskip_row, norm_g, ex, tri)


def _ssd_bwd(dproj, xs_c, bc_c, dt, proj, ypre, hprev, dmix, alog, dskip_row, norm_g):
    T = xs_c.shape[0]
    nc = T // CHUNK
    ex, ext, tri, triu = _ssd_consts()
    rev = lambda n, col=0: pl.BlockSpec((CHUNK, n), lambda c: (nc - 1 - c, col))

    def body(dproj_ref, xs_ref, bc_ref, dt_ref, z_ref, ypre_ref, hprev_ref, dys_ref, alog_ref, dsk_ref, ng_ref,
             ex_ref, ext_ref, tri_ref, triu_ref,
             dxs_ref, dbc_ref, ddt_ref, dz_ref, dng_ref, ddsk_ref, dalog_ref, dHs, dxbuf, dskacc):
        c = pl.program_id(0)

        @pl.when(c == 0)
        def _():
            dHs[...] = jnp.zeros_like(dHs)
            dng_ref[...] = jnp.zeros_like(dng_ref)
            dalog_ref[...] = jnp.zeros_like(dalog_ref)
            dskacc[...] = jnp.zeros_like(dskacc)

        xs, dt, z, ypre, dys = xs_ref[...], dt_ref[...], z_ref[...], ypre_ref[...], dys_ref[...]
        sg = _sigmoid(z)
        sz = z * sg
        yz = ypre * sz
        dyz_parts = []
        for g in range(2):
            gs = slice(g * 512, (g + 1) * 512)
            v = yz[:, gs]
            r = lax.rsqrt(jnp.mean(v * v, axis=-1, keepdims=True) + RMS_EPS)
            vn = v * r
            dng_ref[:, gs] += jnp.sum(dys[:, gs] * vn, axis=0, keepdims=True)
            dvn = dys[:, gs] * ng_ref[:, gs]
            dyz_parts.append(r * (dvn - vn * jnp.mean(dvn * vn, axis=-1, keepdims=True)))
        dyz = jnp.concatenate(dyz_parts, axis=1)
        dy = dyz * sz
        dz_ref[...] = (dyz * ypre * _dsilu(z, sg)).astype(BF16)
        dskacc[...] += jnp.sum(dy * xs, axis=0, keepdims=True)

        a, A, dtex, expA, dec, cd = _ssd_common(xs, dt, alog_ref, ex_ref, tri_ref)
        AT = A.T
        xdt = xs * dtex
        xdec = xdt * dec
        dye = dy * expA
        H = hprev_ref[0]
        dHn = dHs[...]
        sub, lane, causal = _decay_mask()
        dAc = jnp.zeros((CHUNK, LANES), F32)
        Rm = jnp.zeros((CHUNK, LANES), F32)
        yoff_parts, q_parts = [], []
        for g in range(2):
            gs = slice(g * 512, (g + 1) * 512)
            B = bc_ref[:, g * N_STATE:(g + 1) * N_STATE]
            C = bc_ref[:, 256 + g * N_STATE:256 + (g + 1) * N_STATE]
            cb = _dot(C, B, _NT)
            Hg, dHg = H[gs, :], dHn[gs, :]
            Q = _dot(B, dHg, _NT)
            yoff_parts.append(_dot(C, Hg, _NT) * expA[:, gs])
            q_parts.append(Q)
            dcb = jnp.zeros((CHUNK, CHUNK), F32)
            for j in range(8):
                h = g * 8 + j
                hs = slice(h * HEAD, (h + 1) * HEAD)
                L = jnp.exp(jnp.where(causal, A[:, h:h + 1] - AT[h:h + 1, :], -1e30))
                M = cb * L
                G = _dot(dy[:, hs], xdt[:, hs], _NT)
                dxbuf[:, hs] = _dot(M, dy[:, hs], _TN)
                dcb = dcb + G * L
                E = G * M
                dAc = jnp.where(lane == h, jnp.sum(E, axis=1, keepdims=True), dAc)
                Rm = jnp.where(sub == h, jnp.sum(E, axis=0, keepdims=True), Rm)
            dbc_ref[:, g * N_STATE:(g + 1) * N_STATE] = _dot(dcb, C, _TN) + _dot(xdec[:, gs], dHg)
            dbc_ref[:, 256 + g * N_STATE:256 + (g + 1) * N_STATE] = _dot(dcb, B) + _dot(dye[:, gs], Hg)
            dHs[gs, :] = cd[gs, :] * dHg + _dot(dye[:, gs], C, _TN)
        yoff = jnp.concatenate(yoff_parts, axis=1)
        Qd = jnp.concatenate(q_parts, axis=1) * dec
        dxdt = dxbuf[...] + Qd
        extm = ext_ref[...]
        red_s = _dot_sel_b(xdt * Qd, extm)
        dA = dAc - Rm.T + _dot_sel_b(dy * yoff, extm) - red_s
        hd = jnp.sum(_dot_sel_b(H * dHn, extm, _TN), axis=0, keepdims=True)
        last_add = jnp.sum(red_s, axis=0, keepdims=True) + jnp.exp(A[CHUNK - 1:CHUNK, :]) * hd
        dA = dA + jnp.where(sub == CHUNK - 1, last_add, 0.0)
        dadt = _dot_sel_a(triu_ref[...], dA)
        ddt_ref[...] = dadt * a + _dot_sel_b(dxdt * xs, extm)
        dalog_ref[...] += jnp.sum(dadt * dt, axis=0, keepdims=True) * a
        dxs_ref[...] = dxdt * dtex + dsk_ref[...] * dy

        @pl.when(c == nc - 1)
        def _():
            ddsk_ref[...] = _dot_sel_b(jnp.broadcast_to(dskacc[...], (8, 1024)), extm)[0:1, :]

    return pl.pallas_call(
        body, name="ssd_bwd", grid=(nc,),
        in_specs=[_ANY, rev(1024), rev(512), rev(LANES), rev(1024, C_Z), rev(1024),
                  pl.BlockSpec((1, 1024, N_STATE), lambda c: (nc - 1 - c, 0, 0)), rev(1024, 0),
                  _full((1, LANES)), _full((1, 1024)), _full((1, 1024)),
                  _full((LANES, 1024)), _full((1024, LANES)), _full((CHUNK, CHUNK)), _full((CHUNK, CHUNK))],
        out_specs=[rev(1024), rev(512), rev(LANES), rev(1024, C_Z), _full((1, 1024)), _full((1, LANES)), _full((1, LANES))],
        out_shape=[S((T, 1024), F32), S((T, 512), F32), S((T, LANES), F32), S(dproj.shape, BF16),
                   S((1, 1024), F32), S((1, LANES), F32), S((1, LANES), F32)],
        input_output_aliases={0: 3},
        scratch_shapes=[pltpu.VMEM((1024, N_STATE), F32), pltpu.VMEM((CHUNK, 1024), F32), pltpu.VMEM((1, 1024), F32)],
        compiler_params=_params(("arbitrary",)))(
            dproj, xs_c, bc_c, dt, proj, ypre, hprev, dmix, alog, dskip_row, norm_g, ex, ext, tri, triu)


def _shifted_copies(ext, ext8):
    n = ext8.shape[1]
    for r in range(8):
        ext8[r] = ext[pl.ds(r, n), :]


def _shifted(ext8, off, rows):
    return ext8[off % 8, pl.ds(off - off % 8, rows), :]


def _conf_fwd(mix, mixt, proj, w, cb, lg, lb, ba, bb):
    T = proj.shape[0]
    H = HALO_CONF

    def body(mix_ref, mixt_ref, ga_ref, gap_ref, gb_ref, gbp_ref, cg_ref, w_ref, cb_ref, lg_ref, lb_ref, ba_ref,
             bb_ref, u1_ref, yc_ref, yct_ref, ext, ext8):
        first = pl.program_id(0) == 0
        ext[H + TB:, :] = jnp.zeros((8, LANES), F32)

        def blk(cols):
            up = (gap_ref[:, cols] + ba_ref[:, cols]) * _sigmoid(gbp_ref[:, cols] + bb_ref[:, cols])
            ext[0:H, :] = jnp.where(first, 0.0, up)
            ext[H:H + TB, :] = (ga_ref[:, cols] + ba_ref[:, cols]) * _sigmoid(gb_ref[:, cols] + bb_ref[:, cols])
            _shifted_copies(ext, ext8)
            for r0 in range(0, TB, 64):
                acc = jnp.broadcast_to(cb_ref[:, cols], (64, LANES))
                for k in range(K_CONF):
                    acc = acc + w_ref[k:k + 1, cols] * _shifted(ext8, r0 + H - (K_CONF - 1) + k, 64)
                u1_ref[pl.ds(r0, 64), cols] = acc
        _col_loop(D_CONF, blk)

        def rows(rs):
            xh, _ = _ln_stats(u1_ref[rs, :])
            u2 = xh * lg_ref[...] + lb_ref[...]
            cg = cg_ref[rs, :]
            yc_ref[rs, :] = (u2 * _sigmoid(u2) * cg * _sigmoid(cg)).astype(BF16)
        _row_loop(TB, rows)
        yct_ref[...] = yc_ref[...].T

    return pl.pallas_call(
        body, name="conf_fwd", grid=(T // TB,),
        in_specs=[_ANY, _ANY, _row(TB, 1024, C_GLUA), _prev(TB, H, 1024, C_GLUA), _row(TB, 1024, C_GLUB),
                  _prev(TB, H, 1024, C_GLUB), _row(TB, 1024, C_CG), _full((K_CONF, 1024))] + [_full((1, 1024))] * 5,
        out_specs=[_row(TB, 1024), _row(TB, 1024, 1), _colt(1024, TB, 1)],
        out_shape=[S((T, 1024), F32), S((T, 2048), BF16), S((2048, T), BF16)],
        input_output_aliases={0: 1, 1: 2},
        scratch_shapes=[pltpu.VMEM((H + TB + 8, LANES), F32), pltpu.VMEM((8, H + TB, LANES), F32)],
        compiler_params=_params(("parallel",)))(mix, mixt, proj, proj, proj, proj, proj, w, cb, lg, lb, ba, bb)


def _conf_bwd1(dmix, u1, proj, lg, lb):
    T = u1.shape[0]

    def body(dy_ref, u1_ref, cg_ref, lg_ref, lb_ref, du1_ref, dcg_ref, dg_ref, db_ref):
        @pl.when(pl.program_id(0) == 0)
        def _():
            dg_ref[...] = jnp.zeros_like(dg_ref)
            db_ref[...] = jnp.zeros_like(db_ref)

        def rows(rs):
            xh, r = _ln_stats(u1_ref[rs, :])
            u2 = xh * lg_ref[...] + lb_ref[...]
            s2 = _sigmoid(u2)
            cg = cg_ref[rs, :]
            sc = _sigmoid(cg)
            dy = dy_ref[rs, :]
            dcg_ref[rs, :] = (dy * u2 * s2 * _dsilu(cg, sc)).astype(BF16)
            dv, dg, db = _ln_bwd(dy * cg * sc * _dsilu(u2, s2), xh, r, lg_ref[...])
            dg_ref[...] += dg
            db_ref[...] += db
            du1_ref[rs, :] = dv
        _row_loop(TB, rows)

    return pl.pallas_call(
        body, name="conf_bwd1", grid=(T // TB,),
        in_specs=[_row(TB, 1024, 1), _row(TB, 1024), _row(TB, 1024, C_CG), _full((1, 1024)), _full((1, 1024))],
        out_specs=[_row(TB, 1024), _row(TB, 1024, C_CG), _full((1, 1024)), _full((1, 1024))],
        out_shape=[S((T, 1024), F32), S((T, N_MAIN), BF16), S((1, 1024), F32), S((1, 1024), F32)],
        compiler_params=_params(("arbitrary",)))(dmix, u1, proj, lg, lb)


def _conf_bwd2(dproj, proj, du1, w, ba, bb):
    T = du1.shape[0]
    nt = T // TB
    H = HALO_CONF

    def body(dproj_ref, ga_ref, gap_ref, gb_ref, gbp_ref, du_ref, dun_ref, w_ref, ba_ref, bb_ref,
             dg_ref, dw_ref, dcb_ref, dba_ref, dbb_ref, ext, dext, ext8, dext8, dwacc):
        i = pl.program_id(0)
        first, last = i == 0, i == nt - 1

        @pl.when(first)
        def _():
            for r in (dcb_ref, dba_ref, dbb_ref, dwacc):
                r[...] = jnp.zeros_like(r)

        ext[H + TB:, :] = jnp.zeros((8, LANES), F32)
        dext[H + TB:, :] = jnp.zeros((8, LANES), F32)

        def blk(cols):
            cols_b = pl.ds(pl.multiple_of(cols.start + D_CONF, LANES), LANES)
            up = (gap_ref[:, cols] + ba_ref[:, cols]) * _sigmoid(gbp_ref[:, cols] + bb_ref[:, cols])
            ext[0:H, :] = jnp.where(first, 0.0, up)
            a = ga_ref[:, cols] + ba_ref[:, cols]
            sb = _sigmoid(gb_ref[:, cols] + bb_ref[:, cols])
            ext[H:H + TB, :] = a * sb
            du = du_ref[:, cols]
            dext[0:TB, :] = du
            dext[TB:TB + H, :] = jnp.where(last, 0.0, dun_ref[:, cols])
            _shifted_copies(ext, ext8)
            _shifted_copies(dext, dext8)
            dcb_ref[:, cols] += jnp.sum(du, axis=0, keepdims=True)
            for r0 in range(0, TB, 64):
                dur = du_ref[pl.ds(r0, 64), cols]
                acc = jnp.zeros((64, LANES), F32)
                for k in range(K_CONF):
                    prod = dur * _shifted(ext8, r0 + H - (K_CONF - 1) + k, 64)
                    dwacc[k * 8:(k + 1) * 8, cols] += prod.reshape(8, 8, LANES).sum(axis=0)
                    acc = acc + w_ref[k:k + 1, cols] * _shifted(dext8, r0 + K_CONF - 1 - k, 64)
                ar, sr = a[r0:r0 + 64], sb[r0:r0 + 64]
                da = acc * sr
                dbv = acc * ar * sr * (1.0 - sr)
                dg_ref[pl.ds(r0, 64), cols] = da.astype(BF16)
                dg_ref[pl.ds(r0, 64), cols_b] = dbv.astype(BF16)
                dba_ref[:, cols] += jnp.sum(da, axis=0, keepdims=True)
                dbb_ref[:, cols] += jnp.sum(dbv, axis=0, keepdims=True)
        _col_loop(D_CONF, blk)

        @pl.when(last)
        def _():
            dw_ref[...] = jnp.sum(dwacc[...].reshape(K_CONF, 8, D_CONF), axis=1)

    return pl.pallas_call(
        body, name="conf_bwd2", grid=(nt,),
        in_specs=[_ANY, _row(TB, 1024, C_GLUA), _prev(TB, H, 1024, C_GLUA), _row(TB, 1024, C_GLUB),
                  _prev(TB, H, 1024, C_GLUB), _row(TB, 1024), _next(TB, H, 1024, nt), _full((K_CONF, 1024)),
                  _full((1, 1024)), _full((1, 1024))],
        out_specs=[_row(TB, 2048), _full((K_CONF, 1024)), _full((1, 1024)), _full((1, 1024)), _full((1, 1024))],
        out_shape=[S(dproj.shape, BF16), S((K_CONF, 1024), F32)] + [S((1, 1024), F32)] * 3,
        input_output_aliases={0: 0},
        scratch_shapes=[pltpu.VMEM((H + TB + 8, LANES), F32), pltpu.VMEM((TB + H + 8, LANES), F32),
                        pltpu.VMEM((8, H + TB, LANES), F32), pltpu.VMEM((8, TB + H, LANES), F32),
                        pltpu.VMEM((K_CONF * 8, D_CONF), F32)],
        compiler_params=_params(("arbitrary",)))(dproj, proj, proj, proj, proj, du1, du1, w, ba, bb)


def _mesh_pos():
    x, y, c = lax.axis_index("x"), lax.axis_index("y"), lax.axis_index("c")
    return x, y, c, 4 * x + 2 * y + c


def _peer(x, y, c, k):
    return (x ^ ((k >> 2) & 1), y ^ ((k >> 1) & 1), c ^ (k & 1))


def _exchange_copies(ins, outs, kinds, send, recv, loc):
    nb = len(ins)
    x, y, c, me = _mesh_pos()
    src = lambda b, d: ins[b].at[d] if kinds[b] == "blocks" else ins[b]
    copies = [pltpu.make_async_copy(src(b, me), outs[b].at[me], loc.at[b]) for b in range(nb)]
    for k in range(1, N_DEV):
        px, py, pc = _peer(x, y, c, k)
        for b in range(nb):
            s = (k - 1) * nb + b
            copies.append(pltpu.make_async_remote_copy(
                src_ref=src(b, 4 * px + 2 * py + pc), dst_ref=outs[b].at[me], send_sem=send.at[s], recv_sem=recv.at[s],
                device_id=(px, py, pc), device_id_type=pl.DeviceIdType.MESH))
    return copies


def _exchange_shapes(bufs, kinds):
    return [S(b.shape if kd == "blocks" else (N_DEV,) + b.shape, b.dtype) for b, kd in zip(bufs, kinds)]


def _exchange_sems(nb):
    n = (N_DEV - 1) * nb
    return [pltpu.SemaphoreType.DMA((n,)), pltpu.SemaphoreType.DMA((n,)), pltpu.SemaphoreType.DMA((nb,))]


def _exchange(bufs, kinds, name):
    nb = len(bufs)

    def body(*refs):
        copies = _exchange_copies(refs[:nb], refs[nb:2 * nb], kinds, *refs[2 * nb:])
        for cp in copies:
            cp.start()
        for cp in copies:
            cp.wait()

    return pl.pallas_call(
        body, name=name, in_specs=[_ANY] * nb, out_specs=[_ANY] * nb,
        out_shape=_exchange_shapes(bufs, kinds), scratch_shapes=_exchange_sems(nb))(*bufs)


def _gather_two_level(bufs, name):
    nb = len(bufs)

    def body(*refs):
        ins, outs = refs[:nb], refs[nb:2 * nb]
        send, recv, loc = refs[2 * nb:]
        x, y, c, me = _mesh_pos()
        here, sibling = (x, y, c), (x, y, 1 - c)
        chips = [(1 - x, y), (x, 1 - y), (1 - x, 1 - y)]

        def copy(slot, b, block, to, src=None):
            d = 4 * block[0] + 2 * block[1] + block[2]
            return pltpu.make_async_remote_copy(
                src_ref=outs[b].at[d] if src is None else src, dst_ref=outs[b].at[d],
                send_sem=send.at[slot * nb + b], recv_sem=recv.at[slot * nb + b],
                device_id=to, device_id_type=pl.DeviceIdType.MESH)

        mine = [pltpu.make_async_copy(ins[b], outs[b].at[me], loc.at[b]) for b in range(nb)]
        first = [copy(0, b, here, sibling, src=ins[b]) for b in range(nb)]
        first += [copy(1 + j, b, here, (*chip, c), src=ins[b]) for j, chip in enumerate(chips) for b in range(nb)]
        for cp in mine + first:
            cp.start()
        passed = []
        for j, chip in enumerate(chips):
            for b in range(nb):
                copy(1 + j, b, (*chip, c), here).wait_recv()
            onward = [copy(4 + j, b, (*chip, c), sibling) for b in range(nb)]
            for cp in onward:
                cp.start()
            passed += onward
        for b in range(nb):
            copy(0, b, sibling, here).wait_recv()
        for j, chip in enumerate(chips):
            for b in range(nb):
                copy(4 + j, b, (*chip, 1 - c), here).wait_recv()
        for cp in first + passed:
            cp.wait_send()
        for cp in mine:
            cp.wait()

    return pl.pallas_call(
        body, name=name, in_specs=[_ANY] * nb, out_specs=[_ANY] * nb,
        out_shape=_exchange_shapes(bufs, ("all",) * nb), scratch_shapes=_exchange_sems(nb))(*bufs)


def _sum_parts(p_ref):
    acc = p_ref[0].astype(F32)
    for d in range(1, N_DEV):
        acc = acc + p_ref[d].astype(F32)
    return acc


def _adamw_math(g, w, m, v):
    m = ADAM_B1 * m + (1.0 - ADAM_B1) * g
    v = ADAM_B2 * v + (1.0 - ADAM_B2) * (g * g)
    m_hat = m / (1.0 - ADAM_B1 ** ADAM_STEP)
    v_hat = v / (1.0 - ADAM_B2 ** ADAM_STEP)
    return -ADAM_LR * (m_hat / (jnp.sqrt(v_hat) + ADAM_EPS) + ADAM_WD * w), m, v


def _sum8_adamw(parts, w, m, v, name):
    _, R, C = parts.shape
    tb = 256 if R % 256 == 0 else R

    def body(p_ref, w_ref, m_ref, v_ref, g_ref, d_ref, mo_ref, vo_ref):
        g = _sum_parts(p_ref)
        g_ref[...] = g
        d_ref[...], mo_ref[...], vo_ref[...] = _adamw_math(g, w_ref[...], m_ref[...], v_ref[...])

    return pl.pallas_call(
        body, name=name, grid=(R // tb,),
        in_specs=[pl.BlockSpec((N_DEV, tb, C), lambda i: (0, i, 0))] + [_row(tb, C)] * 3, out_specs=[_row(tb, C)] * 4,
        out_shape=[S((R, C), F32)] * 4, compiler_params=_params(("parallel",)))(parts, w, m, v)


SMALL_LAYOUT = (
    ("ln_emb_g", 0, 1024), ("ln_emb_b", 0, 1024), ("ssm_conv_b", 0, 1024), ("ssm_conv_b", 1024, 512),
    ("dt_bias", 0, N_HEADS), ("a_log", 0, N_HEADS), ("d_skip", 0, N_HEADS), ("ssm_norm_g", 0, 1024),
    ("b_glu", 0, 1024), ("b_glu", 1024, 1024), ("conf_conv_b", 0, 1024), ("conf_ln_g", 0, 1024),
    ("conf_ln_b", 0, 1024), ("b_out", 0, 1024), ("ln1_g", 0, 1024), ("ln1_b", 0, 1024), ("ln2_g", 0, 1024),
    ("ln2_b", 0, 1024))
SMALL_ROWS = 24
SMALL = tuple(dict.fromkeys(n for n, _, _ in SMALL_LAYOUT))


LOSS_ROW = len(SMALL_LAYOUT)


def _pack_small(rows, loss):
    def body(*refs):
        o_ref = refs[-1]
        o_ref[...] = jnp.zeros_like(o_ref)
        for r, ref in enumerate(refs[:-2]):
            o_ref[r:r + 1, 0:ref.shape[1]] = ref[...]
        o_ref[LOSS_ROW:LOSS_ROW + 1, 0:LANES] = refs[-2][0:1, :]

    return pl.pallas_call(body, name="pack_small", out_shape=S((SMALL_ROWS, 1024), F32))(*rows, loss)


def _small_update(parts, w, m, v):
    def body(*refs):
        p_ref = refs[0]
        ins = {n: refs[1 + 3 * i:4 + 3 * i] for i, n in enumerate(SMALL)}
        o0 = 1 + 3 * len(SMALL)
        outs = {n: refs[o0 + 4 * i:o0 + 4 * i + 4] for i, n in enumerate(SMALL)}
        gsum = refs[-1]
        gsum[...] = _sum_parts(p_ref)
        refs[-2][...] = gsum[LOSS_ROW:LOSS_ROW + 1, 0:LANES]
        for r, (n, off, wd) in enumerate(SMALL_LAYOUT):
            cs = slice(off, off + wd)
            g = gsum[r:r + 1, 0:wd]
            w_ref, m_ref, v_ref = ins[n]
            g_ref, d_ref, mo_ref, vo_ref = outs[n]
            g_ref[:, cs] = g
            d_ref[:, cs], mo_ref[:, cs], vo_ref[:, cs] = _adamw_math(g, w_ref[:, cs], m_ref[:, cs], v_ref[:, cs])

    args = [parts] + [a for n in SMALL for a in (w[n], m[n], v[n])]
    res = pl.pallas_call(
        body, name="small_update",
        out_shape=[S(w[n].shape, F32) for n in SMALL for _ in range(4)] + [S((1, LANES), F32)],
        scratch_shapes=[pltpu.VMEM((SMALL_ROWS, 1024), F32)])(*args)
    return tuple({n: res[4 * i + j] for i, n in enumerate(SMALL)} for j in range(4)) + (res[-1],)


LATE = ("w_out", "w_ple_gate", "w_ple_proj")


def _local_step(x, p, tgt, W, late=None):
    r1 = lambda v: v.reshape(1, -1).astype(F32)
    pad_l = lambda v: jnp.pad(r1(v), ((0, 0), (0, LANES - v.size)))
    w_main, w_dt = _w_in_to_main(W["w_in"])
    scw, scb = W["ssm_conv_w"], r1(W["ssm_conv_b"])
    wx, wb, bx, bb = scw[:, :1024], scw[:, 1024:], scb[:, :1024], scb[:, 1024:]
    dt_bias, alog = pad_l(W["dt_bias"]), pad_l(W["a_log"])
    dskip_row = jnp.repeat(W["d_skip"].reshape(-1), HEAD).reshape(1, -1)
    norm_g = r1(W["ssm_norm_g"])
    bglu = r1(W["b_glu"])
    ba, bbg = bglu[:, :1024], bglu[:, 1024:]
    ccw, ccb, clg, clb = W["conf_conv_w"], r1(W["conf_conv_b"]), r1(W["conf_ln_g"]), r1(W["conf_ln_b"])

    h0, h0b, h0bt = _ln_emb_fwd(x, r1(W["ln_emb_g"]), r1(W["ln_emb_b"]))
    if late is None:
        proj = _mm(h0b, w_main, "nn", "in_proj")
    else:
        proj, *gathered = _mm(h0b, w_main, "nn", "in_proj", exchange=(late, ("all",) * len(LATE)))
        W = dict(W, **{n: _unstack_shards(a, BY_COLS[n]) for n, a in zip(LATE, gathered)})
    dt_raw = _mm(h0b, w_dt, "nn", "in_proj_dt")
    xs_c, bc_c, dt = _ssd_pre_fwd(proj, dt_raw, wx, wb, bx, bb, dt_bias)
    mix, ypre, hprev, mixt = _ssd_fwd(xs_c, bc_c, dt, proj, alog, dskip_row, norm_g)
    u1, mix, mixt = _conf_fwd(mix, mixt, proj, ccw, ccb, clg, clb, ba, bbg)
    out = _mm(mix, W["w_out"], "nn", "out_proj")
    h1, h1b, h1bt = _post1_fwd(h0, out, r1(W["b_out"]), r1(W["ln1_g"]), r1(W["ln1_b"]))
    gpre = _mm(h1b, W["w_ple_gate"], "nn", "ple_gate")
    pb = p.astype(BF16)
    ple = _mm(pb, W["w_ple_proj"], "nn", "ple_proj")
    dh1a, dgp, dple, loss, dln2g, dln2b = _post2(h1, gpre, ple, tgt, r1(W["ln2_g"]), r1(W["ln2_b"]))

    g = {}
    g["w_ple_proj"] = _mm(pb.T, dple, "nn", "d_ple_proj", out_dtype=BF16)
    g["w_ple_gate"] = _mm(h1bt, dgp, "nn", "d_ple_gate", out_dtype=BF16)
    dh1b = _mm(dgp, W["w_ple_gate"], "nt", "d_h1")
    dout, dh0a, dln1g, dln1b, dbout = _post1_bwd(dh1a, dh1b, h0, out, r1(W["b_out"]), r1(W["ln1_g"]))
    g["w_out"] = _mm(mixt, dout, "nn", "d_w_out", out_dtype=BF16)
    dmix = _mm(dout, W["w_out"], "nt", "d_mix")
    du1, dproj, dclg, dclb = _conf_bwd1(dmix, u1, proj, clg, clb)
    dproj, g["conf_conv_w"], dccb, dba, dbb = _conf_bwd2(dproj, proj, du1, ccw, ba, bbg)
    dxs_c, dbc_c, ddt, dproj, dng, ddsk, dalog = _ssd_bwd(
        dproj, xs_c, bc_c, dt, proj, ypre, hprev, dmix, alog, dskip_row, norm_g)
    dproj, dwx, dbx = _ssd_conv_bwd(dproj, proj, dxs_c, wx, bx, 1024, C_XS, "ssd_conv_bwd_x")
    dproj, dwb, dbb2 = _ssd_conv_bwd(dproj, proj, dbc_c, wb, bb, 512, C_BC, "ssd_conv_bwd_bc")
    ddtr, ddtb = _dt_bwd(ddt, dt_raw, dt_bias)
    g["ssm_conv_w"] = jnp.concatenate([dwx, dwb], axis=1)
    dw_dt = _mm(h0bt, ddtr, "nn", "d_w_dt", out_dtype=BF16)
    stack = lambda names: [_stack_shards(g[n], BY_COLS[n]) for n in names]
    last_args = (dproj, w_main, ddtr, w_dt, dh0a, x, r1(W["ln_emb_g"]))
    if late is None:
        g["w_in"] = _w_in_blocks(_mm(h0bt, dproj, "nn", "d_w_in", out_dtype=BF16), dw_dt)
        grad_x, dlng, dlnb = _d_h0_ln_bwd(*last_args)
    else:
        dw_main, *recv_a = _mm(h0bt, dproj, "nn", "d_w_in", out_dtype=BF16,
                               exchange=(stack(LATE), ("blocks",) * len(LATE)))
        last = ("ssm_conv_w", "conf_conv_w")
        blocks = [_w_in_blocks(dw_main, dw_dt)] + stack(last)
        grad_x, dlng, dlnb, *recv_b = _d_h0_ln_bwd(*last_args, exchange=(blocks, ("blocks",) * 3))
        g["recv"] = dict(zip(LATE + ("w_in",) + last, recv_a + recv_b))
    g["rows"] = [dlng, dlnb, dbx, dbb2, ddtb, dalog, ddsk, dng, dba, dbb, dccb, dclg, dclb, dbout, dln1g, dln1b,
                 dln2g, dln2b]
    return loss, grad_x, g


W_IN_SEGMENTS = ((0, 2048, 2048), (2048, 5120, 512), (2560, None, N_HEADS), (2576, 0, 2048), (4624, 4096, 1024))


def _w_in_to_main(shards):
    def pieces(p0, width):
        out, p = [], p0
        while p < p0 + width:
            d = p // COLS_PER_DEV
            hi = min(p0 + width, (d + 1) * COLS_PER_DEV)
            out.append(shards[d][:, p - d * COLS_PER_DEV:hi - d * COLS_PER_DEV])
            p = hi
        return out
    main = [s for s in sorted(W_IN_SEGMENTS, key=lambda s: -1 if s[1] is None else s[1]) if s[1] is not None]
    w_main = jnp.concatenate([q for p0, _, width in main for q in pieces(p0, width)], axis=1)
    w_dt = jnp.concatenate(pieces(2560, N_HEADS), axis=1)
    return w_main, jnp.pad(w_dt, ((0, 0), (0, LANES - N_HEADS)))


def _w_in_blocks(dw_main, dw_dt):
    blocks = []
    for d in range(N_DEV):
        lo_d, hi_d = d * COLS_PER_DEV, (d + 1) * COLS_PER_DEV
        parts = []
        for p0, m0, width in W_IN_SEGMENTS:
            lo, hi = max(lo_d, p0), min(hi_d, p0 + width)
            if lo < hi:
                parts.append(dw_dt[:, lo - p0:hi - p0] if m0 is None else dw_main[:, m0 + lo - p0:m0 + hi - p0])
        blocks.append(jnp.concatenate(parts, axis=1))
    return jnp.stack(blocks)


WEIGHTS = ['ln_emb_g', 'ln_emb_b', 'w_in', 'ssm_conv_w', 'ssm_conv_b', 'dt_bias', 'a_log', 'd_skip', 'ssm_norm_g',
           'b_glu', 'conf_conv_w', 'conf_conv_b', 'conf_ln_g', 'conf_ln_b', 'w_out', 'b_out', 'ln1_g', 'ln1_b',
           'w_ple_gate', 'w_ple_proj', 'ln2_g', 'ln2_b']
SHARDED = (("w_in", True), ("w_out", False), ("w_ple_gate", False), ("w_ple_proj", True), ("ssm_conv_w", True),
           ("conf_conv_w", True))
BY_COLS = dict(SHARDED)


def _stack_shards(a, by_cols):
    if by_cols:
        return a.reshape(a.shape[0], N_DEV, a.shape[1] // N_DEV).transpose(1, 0, 2)
    return a.reshape(N_DEV, a.shape[0] // N_DEV, a.shape[1])


def _unstack_shards(a, by_cols):
    if by_cols:
        return a.transpose(1, 0, 2).reshape(a.shape[1], N_DEV * a.shape[2])
    return a.reshape(N_DEV * a.shape[1], a.shape[2])


def kernel(x, p, ln_emb_g, ln_emb_b, w_in, ssm_conv_w, ssm_conv_b, dt_bias, a_log, d_skip, ssm_norm_g, b_glu, conf_conv_w, conf_conv_b, conf_ln_g, conf_ln_b, w_out, b_out, ln1_g, ln1_b, w_ple_gate, w_ple_proj, ln2_g, ln2_b, loss_target, m_ln_emb_g, m_ln_emb_b, m_w_in, m_ssm_conv_w, m_ssm_conv_b, m_dt_bias, m_a_log, m_d_skip, m_ssm_norm_g, m_b_glu, m_conf_conv_w, m_conf_conv_b, m_conf_ln_g, m_conf_ln_b, m_w_out, m_b_out, m_ln1_g, m_ln1_b, m_w_ple_gate, m_w_ple_proj, m_ln2_g, m_ln2_b, v_ln_emb_g, v_ln_emb_b, v_w_in, v_ssm_conv_w, v_ssm_conv_b, v_dt_bias, v_a_log, v_d_skip, v_ssm_norm_g, v_b_glu, v_conf_conv_w, v_conf_conv_b, v_conf_ln_g, v_conf_ln_b, v_w_out, v_b_out, v_ln1_g, v_ln1_b, v_w_ple_gate, v_w_ple_proj, v_ln2_g, v_ln2_b):
    loc = dict(locals())
    w = {n: loc[n] for n in WEIGHTS}
    m = {n: loc["m_" + n] for n in WEIGHTS}
    v = {n: loc["v_" + n] for n in WEIGHTS}
    sharded = [n for n, _ in SHARDED]

    local = {n: w[n][0].astype(BF16) if n.startswith("w_") else w[n][0] for n in sharded}
    first = [n for n in sharded if n not in LATE]
    W = {n: w[n].reshape(-1) for n in SMALL}
    for n, a in zip(first, _gather_two_level([local[n] for n in first], "gather_first")):
        W[n] = a if n == "w_in" else _unstack_shards(a, BY_COLS[n])

    loss, grad_x, g = _local_step(x[0], p[0, 0], loss_target[0], W, late=[local[n] for n in LATE])
    (recv_small,) = _exchange([_pack_small(g["rows"], loss)], ("all",), "small_exchange")

    grads, delta, new_m, new_v = {}, {}, {}, {}
    for n in sharded:
        res = _sum8_adamw(g["recv"][n], w[n][0], m[n][0], v[n][0], "adamw_" + n)
        grads[n], delta[n], new_m[n], new_v[n] = (r[None] for r in res)
    two_d = lambda d: {n: d[n].reshape(1, -1) for n in SMALL}
    *small, loss = _small_update(recv_small, two_d(w), two_d(m), two_d(v))
    for dst, res in zip((grads, delta, new_m, new_v), small):
        for n in SMALL:
            dst[n] = res[n].reshape(w[n].shape)
    return (loss[0, 0], grad_x[None], *[grads[n] for n in WEIGHTS], *[delta[n] for n in WEIGHTS],
            *[new_m[n] for n in WEIGHTS], *[new_v[n] for n in WEIGHTS])
```

```python
import functools

import numpy as np
import jax
import jax.numpy as jnp
from jax import lax
from jax.experimental import pallas as pl
from jax.experimental.pallas import tpu as pltpu

F32, BF16 = jnp.float32, jnp.bfloat16
S = jax.ShapeDtypeStruct

N_DEV = 8
D = 1024
D_PLE = 256
D_SSM = 1024
D_CONF = 1024
N_HEADS = 16
HEAD = 64
N_STATE = 128
CHUNK = 128
K_SSM = 4
K_CONF = 31
D_IN = 5648
COLS_PER_DEV = D_IN // N_DEV
LN_EPS = 1e-5
RMS_EPS = 1e-5
ALPHA = 2.0 ** 0.25
LANES = 128
TB = 256
RG = 32
ROW_UNROLL = 4
HALO_SSM = 8
HALO_CONF = 32
VMEM_LIMIT = 56 * 1024 * 1024

ADAM_LR, ADAM_B1, ADAM_B2, ADAM_EPS, ADAM_WD, ADAM_STEP = 0.001, 0.9, 0.999, 1e-08, 0.01, 10

C_GLUA, C_GLUB, C_XS, C_Z, C_CG = 0, 1, 2, 3, 4
C_BC = 10
N_MAIN = 5632


def _params(sem, vmem=VMEM_LIMIT):
    return pltpu.CompilerParams(dimension_semantics=sem, vmem_limit_bytes=vmem)


def _row(tb, n, col=0):
    return pl.BlockSpec((tb, n), lambda i: (i, col))


def _colt(n, tb, row=0):
    return pl.BlockSpec((n, tb), lambda i: (row, i))


def _full(shape):
    return pl.BlockSpec(shape, lambda i: (0,) * len(shape))


_ANY = pl.BlockSpec(memory_space=pl.ANY)


def _prev(tb, halo, n, col=0):
    r = tb // halo
    return pl.BlockSpec((halo, n), lambda i: (jnp.maximum(i * r - 1, 0), col))


def _next(tb, halo, n, nt, col=0):
    r = tb // halo
    return pl.BlockSpec((halo, n), lambda i: (jnp.minimum((i + 1) * r, nt * r - 1), col))


def _row_loop(tb, fn):
    def it(r, c):
        fn(pl.ds(pl.multiple_of(r * RG, RG), RG))
        return c
    lax.fori_loop(0, tb // RG, it, 0, unroll=ROW_UNROLL)


def _col_loop(n, fn):
    def it(j, c):
        fn(pl.ds(pl.multiple_of(j * LANES, LANES), LANES))
        return c
    lax.fori_loop(0, n // LANES, it, 0)


def _sigmoid(x):
    return 1.0 / (1.0 + jnp.exp(-x))


def _dsilu(x, s):
    return s * (1.0 + x * (1.0 - s))


def _ln_stats(v):
    mu = jnp.mean(v, axis=-1, keepdims=True)
    c = v - mu
    r = lax.rsqrt(jnp.mean(c * c, axis=-1, keepdims=True) + LN_EPS)
    return c * r, r


def _ln_bwd(dy, xhat, r, g):
    dxh = dy * g
    dv = r * (dxh - jnp.mean(dxh, axis=-1, keepdims=True) - xhat * jnp.mean(dxh * xhat, axis=-1, keepdims=True))
    return dv, jnp.sum(dy * xhat, axis=0, keepdims=True), jnp.sum(dy, axis=0, keepdims=True)


def _dot(a, b, dims=((1,), (0,))):
    return lax.dot_general(a.astype(BF16), b.astype(BF16), (dims, ((), ())), preferred_element_type=F32)


_NT = ((1,), (1,))
_TN = ((0,), (0,))


def _split3(x):
    hi = x.astype(BF16)
    r = x - hi.astype(F32)
    mid = r.astype(BF16)
    return hi, mid, (r - mid.astype(F32)).astype(BF16)


def _dot_sel_b(a, b, dims=((1,), (0,))):
    hi, mid, lo = _split3(a)
    return (_dot(lo, b, dims) + _dot(mid, b, dims)) + _dot(hi, b, dims)


def _dot_sel_a(a, b, dims=((1,), (0,))):
    hi, mid, lo = _split3(b)
    return (_dot(a, lo, dims) + _dot(a, mid, dims)) + _dot(a, hi, dims)


def _mm(a, b, mode, name, out_dtype=F32, add=None, tm=1024, tn=None, tk=1024, exchange=None):
    if mode == "nn":
        (M, K), N = a.shape, b.shape[1]
    elif mode == "tn":
        (K, M), N = a.shape, b.shape[1]
    else:
        (M, K), N = a.shape, b.shape[0]
    if tn is None:
        tn = next(t for t in (1024, 1408, 512, 256, LANES) if N % t == 0)
    tm, tn, tk = min(tm, M), min(tn, N), min(tk, K)
    assert M % tm == 0 and N % tn == 0 and K % tk == 0, (name, M, N, K)
    grid = (M // tm, N // tn, K // tk)
    nk = grid[2]
    dims = {"nn": ((1,), (0,)), "tn": _TN, "nt": _NT}[mode]
    n_in = 2 + (add is not None)
    xbufs, kinds = exchange if exchange is not None else ((), ())
    nx = len(xbufs)

    def body(*refs):
        a_ref, b_ref = refs[:2]
        o_ref = refs[n_in + nx]
        acc = refs[n_in + 2 * nx + 1]
        i, j, k = pl.program_id(0), pl.program_id(1), pl.program_id(2)
        if nx:
            start, finish = _exchange_plan(refs[n_in:n_in + nx], refs[n_in + nx + 1:n_in + 2 * nx + 1], kinds,
                                           *refs[n_in + 2 * nx + 2:])
            pl.when((i == 0) & (j == 0) & (k == 0))(start)

        d = _dot(a_ref[...], b_ref[...], dims)

        def write_out(r):
            if add is not None:
                r = r + refs[2][...]
            o_ref[...] = r.astype(out_dtype)

        if nk == 1:
            write_out(d)
        else:
            @pl.when(k == 0)
            def _():
                acc[...] = d

            @pl.when((k > 0) & (k < nk - 1))
            def _():
                acc[...] += d

            @pl.when(k == nk - 1)
            def _():
                write_out(acc[...] + d)

        if nx:
            pl.when((i == grid[0] - 1) & (j == grid[1] - 1) & (k == nk - 1))(finish)

    a_spec = pl.BlockSpec((tk, tm), lambda i, j, k: (k, i)) if mode == "tn" else pl.BlockSpec((tm, tk), lambda i, j, k: (i, k))
    b_spec = pl.BlockSpec((tn, tk), lambda i, j, k: (j, k)) if mode == "nt" else pl.BlockSpec((tk, tn), lambda i, j, k: (k, j))
    o_spec = pl.BlockSpec((tm, tn), lambda i, j, k: (i, j))
    ins, specs = [a, b], [a_spec, b_spec]
    if add is not None:
        ins.append(add)
        specs.append(o_spec)
    acc_spec = pltpu.VMEM((tm, tn) if nk > 1 else (8, LANES), F32)
    if not nx:
        return pl.pallas_call(
            body, name=name, grid=grid, in_specs=specs, out_specs=o_spec,
            out_shape=S((M, N), out_dtype), scratch_shapes=[acc_spec],
            compiler_params=_params(("parallel", "parallel", "arbitrary")))(*ins)
    return pl.pallas_call(
        body, name=name, grid=grid, in_specs=specs + [_ANY] * nx, out_specs=[o_spec] + [_ANY] * nx,
        out_shape=[S((M, N), out_dtype)] + _exchange_shapes(xbufs, kinds),
        scratch_shapes=[acc_spec] + _exchange_sems(nx),
        compiler_params=_params(("arbitrary", "arbitrary", "arbitrary")))(*ins, *xbufs)


def _ln_emb_fwd(x, g, b, exchange=None):
    T = x.shape[0]

    nt = T // TB
    xbufs, kinds = exchange if exchange is not None else ((), ())
    nx = len(xbufs)

    def body(*refs):
        x_ref, g_ref, b_ref = refs[:3]
        h_ref, hb_ref, hbt_ref = refs[3 + nx:6 + nx]
        i = pl.program_id(0)
        if nx:
            start, finish = _exchange_plan(refs[3:3 + nx], refs[6 + nx:6 + 2 * nx], kinds, *refs[6 + 2 * nx:])
            pl.when(i == 0)(start)

        def rows(rs):
            xh, _ = _ln_stats(x_ref[rs, :])
            h = xh * g_ref[...] + b_ref[...]
            h_ref[rs, :] = h
            hb_ref[rs, :] = h.astype(BF16)
        _row_loop(TB, rows)
        hbt_ref[...] = hb_ref[...].T
        if nx:
            pl.when(i == nt - 1)(finish)

    return pl.pallas_call(
        body, name="ln_emb_fwd", grid=(nt,),
        in_specs=[_row(TB, D), _full((1, D)), _full((1, D))] + [_ANY] * nx,
        out_specs=[_row(TB, D), _row(TB, D), _colt(D, TB)] + [_ANY] * nx,
        out_shape=[S((T, D), F32), S((T, D), BF16), S((D, T), BF16)] + _exchange_shapes(xbufs, kinds),
        scratch_shapes=_exchange_sems(nx) if nx else [],
        compiler_params=_params(("arbitrary",)))(x, g, b, *xbufs)


def _post1_fwd(h0, out, b_out, g, b):
    T = h0.shape[0]

    def body(h0_ref, out_ref, bo_ref, g_ref, b_ref, h_ref, hb_ref, hbt_ref):
        def rows(rs):
            xh, _ = _ln_stats(ALPHA * h0_ref[rs, :] + out_ref[rs, :] + bo_ref[...])
            h = xh * g_ref[...] + b_ref[...]
            h_ref[rs, :] = h
            hb_ref[rs, :] = h.astype(BF16)
        _row_loop(TB, rows)
        hbt_ref[...] = hb_ref[...].T

    return pl.pallas_call(
        body, name="post1_fwd", grid=(T // TB,),
        in_specs=[_row(TB, D), _row(TB, D)] + [_full((1, D))] * 3, out_specs=[_row(TB, D), _row(TB, D), _colt(D, TB)],
        out_shape=[S((T, D), F32), S((T, D), BF16), S((D, T), BF16)],
        compiler_params=_params(("parallel",)))(h0, out, b_out, g, b)


def _post2(h1, gpre, ple, tgt, g, b):
    T = h1.shape[0]

    def body(h1_ref, gp_ref, ple_ref, tgt_ref, g_ref, b_ref, dh1_ref, dgp_ref, dple_ref, loss_ref, dg_ref, db_ref):
        @pl.when(pl.program_id(0) == 0)
        def _():
            loss_ref[...] = jnp.zeros_like(loss_ref)
            dg_ref[...] = jnp.zeros_like(dg_ref)
            db_ref[...] = jnp.zeros_like(db_ref)

        def rows(rs):
            gate = _sigmoid(gp_ref[rs, :])
            ple = ple_ref[rs, :]
            xh, r = _ln_stats(ALPHA * h1_ref[rs, :] + gate * ple)
            err = xh * g_ref[...] + b_ref[...] - tgt_ref[rs, :]
            loss_ref[...] += 0.5 * jnp.sum(jnp.mean(err * err, axis=-1, keepdims=True), axis=0, keepdims=True)
            dv, dg, db = _ln_bwd(err * (1.0 / D), xh, r, g_ref[...])
            dg_ref[...] += dg
            db_ref[...] += db
            dh1_ref[rs, :] = ALPHA * dv
            dgp_ref[rs, :] = (dv * ple * gate * (1.0 - gate)).astype(BF16)
            dple_ref[rs, :] = (dv * gate).astype(BF16)
        _row_loop(TB, rows)

    return pl.pallas_call(
        body, name="post2", grid=(T // TB,),
        in_specs=[_row(TB, D)] * 4 + [_full((1, D))] * 2,
        out_specs=[_row(TB, D)] * 3 + [_full((8, LANES)), _full((1, D)), _full((1, D))],
        out_shape=[S((T, D), F32), S((T, D), BF16), S((T, D), BF16), S((8, LANES), F32), S((1, D), F32), S((1, D), F32)],
        compiler_params=_params(("arbitrary",)))(h1, gpre, ple, tgt, g, b)


def _post1_bwd(dh1a, dh1b, h0, out, b_out, g):
    T = h0.shape[0]

    def body(da_ref, db2_ref, h0_ref, out_ref, bo_ref, g_ref, dout_ref, dh0_ref, dg_ref, db_ref, dbo_ref):
        @pl.when(pl.program_id(0) == 0)
        def _():
            dg_ref[...] = jnp.zeros_like(dg_ref)
            db_ref[...] = jnp.zeros_like(db_ref)
            dbo_ref[...] = jnp.zeros_like(dbo_ref)

        def rows(rs):
            xh, r = _ln_stats(ALPHA * h0_ref[rs, :] + out_ref[rs, :] + bo_ref[...])
            dv, dg, db = _ln_bwd(da_ref[rs, :] + db2_ref[rs, :], xh, r, g_ref[...])
            dg_ref[...] += dg
            db_ref[...] += db
            dbo_ref[...] += jnp.sum(dv, axis=0, keepdims=True)
            dout_ref[rs, :] = dv.astype(BF16)
            dh0_ref[rs, :] = ALPHA * dv
        _row_loop(TB, rows)

    return pl.pallas_call(
        body, name="post1_bwd", grid=(T // TB,),
        in_specs=[_row(TB, D)] * 4 + [_full((1, D))] * 2,
        out_specs=[_row(TB, D)] * 2 + [_full((1, D))] * 3,
        out_shape=[S((T, D), BF16), S((T, D), F32)] + [S((1, D), F32)] * 3,
        compiler_params=_params(("arbitrary",)))(dh1a, dh1b, h0, out, b_out, g)


def _d_h0_ln_bwd(dproj, w_main, ddtr, w_dt, dh0a, x, g, exchange=None, tm=1024, tk=1408):
    T, K = dproj.shape
    tm = min(tm, T)
    assert T % tm == 0 and K % tk == 0
    ni, nk = T // tm, K // tk
    xbufs, kinds = exchange if exchange is not None else ((), ())
    nx = len(xbufs)

    def body(*refs):
        dp_ref, w_ref, dt_ref, wdt_ref, da_ref, x_ref, g_ref = refs[:7]
        dx_ref, dg_ref, db_ref = refs[7 + nx:10 + nx]
        acc = refs[10 + 2 * nx]
        i, k = pl.program_id(0), pl.program_id(1)
        if nx:
            start, finish = _exchange_plan(refs[7:7 + nx], refs[10 + nx:10 + 2 * nx], kinds, *refs[11 + 2 * nx:])
            pl.when((i == 0) & (k == 0))(start)

        @pl.when((i == 0) & (k == 0))
        def _():
            dg_ref[...] = jnp.zeros_like(dg_ref)
            db_ref[...] = jnp.zeros_like(db_ref)

        d = _dot(dp_ref[...], w_ref[...], _NT)

        @pl.when(k == 0)
        def _():
            acc[...] = da_ref[...] + _dot(dt_ref[...], wdt_ref[...], _NT) + d

        @pl.when(k > 0)
        def _():
            acc[...] += d

        @pl.when(k == nk - 1)
        def _():
            def rows(rs):
                xh, r = _ln_stats(x_ref[rs, :])
                dv, dg, db = _ln_bwd(acc[rs, :], xh, r, g_ref[...])
                dg_ref[...] += dg
                db_ref[...] += db
                dx_ref[rs, :] = dv
            _row_loop(tm, rows)

        if nx:
            pl.when((i == ni - 1) & (k == nk - 1))(finish)

    rowt = lambda n: pl.BlockSpec((tm, n), lambda i, k: (i, 0))
    const = lambda shape: pl.BlockSpec(shape, lambda i, k: (0, 0))
    return pl.pallas_call(
        body, name="d_h0_ln_bwd", grid=(ni, nk),
        in_specs=[pl.BlockSpec((tm, tk), lambda i, k: (i, k)), pl.BlockSpec((D, tk), lambda i, k: (0, k)),
                  rowt(LANES), const((D, LANES)), rowt(D), rowt(D), const((1, D))] + [_ANY] * nx,
        out_specs=[rowt(D), const((1, D)), const((1, D))] + [_ANY] * nx,
        out_shape=[S((T, D), F32), S((1, D), F32), S((1, D), F32)] + _exchange_shapes(xbufs, kinds),
        scratch_shapes=[pltpu.VMEM((tm, D), F32)] + (_exchange_sems(nx) if nx else []),
        compiler_params=_params(("arbitrary", "arbitrary")))(dproj, w_main, ddtr, w_dt, dh0a, x, g, *xbufs)


def _softplus(x):
    return jnp.maximum(x, 0.0) + jnp.log1p(jnp.exp(-jnp.abs(x)))


def _ssd_pre_fwd(proj, dt_raw, wx, wb, bx, bb, dt_bias):
    T = proj.shape[0]
    H = HALO_SSM

    def body(xs_ref, xsp_ref, bc_ref, bcp_ref, dtr_ref, wx_ref, wb_ref, bx_ref, bb_ref, dtb_ref,
             xso_ref, bco_ref, dto_ref, extx, extb):
        first = pl.program_id(0) == 0

        def conv(t_ref, p_ref, w_ref, b_ref, o_ref, ext, n):
            def blk(cols):
                ext[0:H, cols] = jnp.where(first, 0.0, p_ref[:, cols])
                ext[H:, cols] = t_ref[:, cols]
                for r0 in range(0, TB, 64):
                    acc = jnp.broadcast_to(b_ref[:, cols], (64, LANES))
                    for k in range(K_SSM):
                        acc = acc + w_ref[k:k + 1, cols] * ext[pl.ds(r0 + H - (K_SSM - 1) + k, 64), cols]
                    o_ref[pl.ds(r0, 64), cols] = acc * _sigmoid(acc)
            _col_loop(n, blk)

        conv(xs_ref, xsp_ref, wx_ref, bx_ref, xso_ref, extx, D_SSM)
        conv(bc_ref, bcp_ref, wb_ref, bb_ref, bco_ref, extb, 512)
        dto_ref[...] = _softplus(dtr_ref[...] + dtb_ref[...])

    return pl.pallas_call(
        body, name="ssd_pre_fwd", grid=(T // TB,),
        in_specs=[_row(TB, 1024, C_XS), _prev(TB, H, 1024, C_XS), _row(TB, 512, C_BC), _prev(TB, H, 512, C_BC),
                  _row(TB, LANES), _full((K_SSM, 1024)), _full((K_SSM, 512)), _full((1, 1024)), _full((1, 512)),
                  _full((1, LANES))],
        out_specs=[_row(TB, 1024), _row(TB, 512), _row(TB, LANES)],
        out_shape=[S((T, 1024), F32), S((T, 512), F32), S((T, LANES), F32)],
        scratch_shapes=[pltpu.VMEM((H + TB, 1024), F32), pltpu.VMEM((H + TB, 512), F32)],
        compiler_params=_params(("parallel",)))(proj, proj, proj, proj, dt_raw, wx, wb, bx, bb, dt_bias)


def _ssd_conv_bwd(dproj, proj, d_c, w, b, n, col, name):
    T = proj.shape[0]
    nt = T // TB
    H = HALO_SSM
    R = TB + H

    def body(dproj_ref, t_ref, p_ref, n_ref, d_ref, dn_ref, w_ref, b_ref, o_ref, dw_ref, dbias_ref, ext, dp):
        i = pl.program_id(0)
        first, last = i == 0, i == nt - 1

        @pl.when(first)
        def _():
            dw_ref[...] = jnp.zeros_like(dw_ref)
            dbias_ref[...] = jnp.zeros_like(dbias_ref)

        def blk(cols):
            ext[0:H, cols] = jnp.where(first, 0.0, p_ref[:, cols])
            ext[H:H + TB, cols] = t_ref[:, cols]
            ext[H + TB:, cols] = n_ref[:, cols]
            pre = jnp.broadcast_to(b_ref[:, cols], (R, LANES))
            for k in range(K_SSM):
                pre = pre + w_ref[k:k + 1, cols] * ext[pl.ds(H - (K_SSM - 1) + k, R), cols]
            s = _sigmoid(pre)
            ds = _dsilu(pre, s)
            dp[0:TB, cols] = d_ref[:, cols] * ds[0:TB]
            dp[TB:, cols] = jnp.where(last, 0.0, dn_ref[:, cols] * ds[TB:])
            dpt = dp[0:TB, cols]
            dbias_ref[:, cols] += jnp.sum(dpt, axis=0, keepdims=True)
            acc = jnp.zeros((TB, LANES), F32)
            for k in range(K_SSM):
                dw_ref[k:k + 1, cols] += jnp.sum(dpt * ext[pl.ds(H - (K_SSM - 1) + k, TB), cols], axis=0, keepdims=True)
                acc = acc + w_ref[k:k + 1, cols] * dp[pl.ds(K_SSM - 1 - k, TB), cols]
            o_ref[:, cols] = acc.astype(BF16)
        _col_loop(n, blk)

    return pl.pallas_call(
        body, name=name, grid=(nt,),
        in_specs=[_ANY, _row(TB, n, col), _prev(TB, H, n, col), _next(TB, H, n, nt, col),
                  _row(TB, n), _next(TB, H, n, nt), _full((K_SSM, n)), _full((1, n))],
        out_specs=[_row(TB, n, col), _full((K_SSM, n)), _full((1, n))],
        out_shape=[S(dproj.shape, BF16), S((K_SSM, n), F32), S((1, n), F32)],
        input_output_aliases={0: 0},
        scratch_shapes=[pltpu.VMEM((H + TB + H, n), F32), pltpu.VMEM((R, n), F32)],
        compiler_params=_params(("arbitrary",)))(dproj, proj, proj, proj, d_c, d_c, w, b)


def _dt_bwd(ddt, dt_raw, dt_bias):
    T = ddt.shape[0]

    def body(ddt_ref, dtr_ref, dtb_ref, o_ref, db_ref):
        @pl.when(pl.program_id(0) == 0)
        def _():
            db_ref[...] = jnp.zeros_like(db_ref)

        g = ddt_ref[...] * _sigmoid(dtr_ref[...] + dtb_ref[...])
        o_ref[...] = g.astype(BF16)
        db_ref[...] += jnp.sum(g, axis=0, keepdims=True)

    return pl.pallas_call(
        body, name="dt_bwd", grid=(T // TB,),
        in_specs=[_row(TB, LANES), _row(TB, LANES), _full((1, LANES))], out_specs=[_row(TB, LANES), _full((1, LANES))],
        out_shape=[S((T, LANES), BF16), S((1, LANES), F32)],
        compiler_params=_params(("arbitrary",)))(ddt, dt_raw, dt_bias)


def _ssd_consts():
    ex = np.zeros((LANES, D_SSM), np.float32)
    for h in range(N_HEADS):
        ex[h, h * HEAD:(h + 1) * HEAD] = 1.0
    tri = np.tril(np.ones((CHUNK, CHUNK), np.float32))
    return jnp.asarray(ex), jnp.asarray(ex.T.copy()), jnp.asarray(tri), jnp.asarray(tri.T.copy())


def _ssd_common(xs, dt, alog_ref, ex_ref, tri_ref):
    lane = lax.broadcasted_iota(jnp.int32, (1, LANES), 1)
    a = jnp.where(lane < N_HEADS, -jnp.exp(alog_ref[...]), 0.0)
    A = _dot_sel_a(tri_ref[...], dt * a)
    ex = ex_ref[...]
    Aex = _dot_sel_b(A, ex)
    dtex = _dot_sel_b(dt, ex)
    expA = jnp.exp(Aex)
    dec = jnp.exp(Aex[CHUNK - 1:CHUNK, :] - Aex)
    cd = _dot_sel_a(ex, jnp.broadcast_to(jnp.exp(A.T[:, CHUNK - 1:CHUNK]), (LANES, LANES)), _TN)
    return a, A, dtex, expA, dec, cd


def _decay_mask():
    sub = lax.broadcasted_iota(jnp.int32, (CHUNK, CHUNK), 0)
    lane = lax.broadcasted_iota(jnp.int32, (CHUNK, CHUNK), 1)
    return sub, lane, sub >= lane


def _ssd_fwd(xs_c, bc_c, dt, proj, alog, dskip_row, norm_g):
    T = xs_c.shape[0]
    nc = T // CHUNK
    ex, _, tri, _ = _ssd_consts()

    def body(xs_ref, bc_ref, dt_ref, z_ref, alog_ref, dsk_ref, ng_ref, ex_ref, tri_ref,
             ys_ref, ypre_ref, hprev_ref, yst_ref, Hs, ybuf):
        @pl.when(pl.program_id(0) == 0)
        def _():
            Hs[...] = jnp.zeros_like(Hs)

        hprev_ref[0] = Hs[...]
        xs, dt = xs_ref[...], dt_ref[...]
        a, A, dtex, expA, dec, cd = _ssd_common(xs, dt, alog_ref, ex_ref, tri_ref)
        AT = A.T
        xdt = xs * dtex
        xdec = xdt * dec
        _, _, causal = _decay_mask()
        for g in range(2):
            gs = slice(g * 512, (g + 1) * 512)
            B = bc_ref[:, g * N_STATE:(g + 1) * N_STATE]
            C = bc_ref[:, 256 + g * N_STATE:256 + (g + 1) * N_STATE]
            cb = _dot(C, B, _NT)
            Hg = Hs[gs, :]
            yoff = _dot(C, Hg, _NT) * expA[:, gs]
            for j in range(8):
                h = g * 8 + j
                hs = slice(h * HEAD, (h + 1) * HEAD)
                L = jnp.exp(jnp.where(causal, A[:, h:h + 1] - AT[h:h + 1, :], -1e30))
                ybuf[:, hs] = _dot(cb * L, xdt[:, hs]) + yoff[:, j * HEAD:(j + 1) * HEAD]
            Hs[gs, :] = cd[gs, :] * Hg + _dot(xdec[:, gs], B, _TN)
        ypre = ybuf[...] + dsk_ref[...] * xs
        ypre_ref[...] = ypre
        z = z_ref[...]
        yz = ypre * (z * _sigmoid(z))
        for g in range(2):
            gs = slice(g * 512, (g + 1) * 512)
            v = yz[:, gs]
            r = lax.rsqrt(jnp.mean(v * v, axis=-1, keepdims=True) + RMS_EPS)
            ys_ref[:, gs] = (v * r * ng_ref[:, gs]).astype(BF16)
        yst_ref[...] = ys_ref[...].T

    return pl.pallas_call(
        body, name="ssd_fwd", grid=(nc,),
        in_specs=[_row(CHUNK, 1024), _row(CHUNK, 512), _row(CHUNK, LANES), _row(CHUNK, 1024, C_Z),
                  _full((1, LANES)), _full((1, 1024)), _full((1, 1024)), _full((LANES, 1024)), _full((CHUNK, CHUNK))],
        out_specs=[_row(CHUNK, 1024), _row(CHUNK, 1024), pl.BlockSpec((1, 1024, N_STATE), lambda c: (c, 0, 0)),
                   _colt(1024, CHUNK)],
        out_shape=[S((T, 2048), BF16), S((T, 1024), F32), S((nc, 1024, N_STATE), F32), S((2048, T), BF16)],
        scratch_shapes=[pltpu.VMEM((1024, N_STATE), F32), pltpu.VMEM((CHUNK, 1024), F32)],
        compiler_params=_params(("arbitrary",)))(xs_c, bc_c, dt, proj, alog, dskip_row, norm_g, ex, tri)


def _ssd_bwd(dproj, xs_c, bc_c, dt, proj, ypre, hprev, dmix, alog, dskip_row, norm_g):
    T = xs_c.shape[0]
    nc = T // CHUNK
    ex, ext, tri, triu = _ssd_consts()
    rev = lambda n, col=0: pl.BlockSpec((CHUNK, n), lambda c: (nc - 1 - c, col))

    def body(dproj_ref, xs_ref, bc_ref, dt_ref, z_ref, ypre_ref, hprev_ref, dys_ref, alog_ref, dsk_ref, ng_ref,
             ex_ref, ext_ref, tri_ref, triu_ref,
             dxs_ref, dbc_ref, ddt_ref, dz_ref, dng_ref, ddsk_ref, dalog_ref, dHs, dxbuf, dskacc):
        c = pl.program_id(0)

        @pl.when(c == 0)
        def _():
            dHs[...] = jnp.zeros_like(dHs)
            dng_ref[...] = jnp.zeros_like(dng_ref)
            dalog_ref[...] = jnp.zeros_like(dalog_ref)
            dskacc[...] = jnp.zeros_like(dskacc)

        xs, dt, z, ypre, dys = xs_ref[...], dt_ref[...], z_ref[...], ypre_ref[...], dys_ref[...]
        sg = _sigmoid(z)
        sz = z * sg
        yz = ypre * sz
        dyz_parts = []
        for g in range(2):
            gs = slice(g * 512, (g + 1) * 512)
            v = yz[:, gs]
            r = lax.rsqrt(jnp.mean(v * v, axis=-1, keepdims=True) + RMS_EPS)
            vn = v * r
            dng_ref[:, gs] += jnp.sum(dys[:, gs] * vn, axis=0, keepdims=True)
            dvn = dys[:, gs] * ng_ref[:, gs]
            dyz_parts.append(r * (dvn - vn * jnp.mean(dvn * vn, axis=-1, keepdims=True)))
        dyz = jnp.concatenate(dyz_parts, axis=1)
        dy = dyz * sz
        dz_ref[...] = (dyz * ypre * _dsilu(z, sg)).astype(BF16)
        dskacc[...] += jnp.sum(dy * xs, axis=0, keepdims=True)

        a, A, dtex, expA, dec, cd = _ssd_common(xs, dt, alog_ref, ex_ref, tri_ref)
        AT = A.T
        xdt = xs * dtex
        xdec = xdt * dec
        dye = dy * expA
        H = hprev_ref[0]
        dHn = dHs[...]
        sub, lane, causal = _decay_mask()
        dAc = jnp.zeros((CHUNK, LANES), F32)
        Rm = jnp.zeros((CHUNK, LANES), F32)
        yoff_parts, q_parts = [], []
        for g in range(2):
            gs = slice(g * 512, (g + 1) * 512)
            B = bc_ref[:, g * N_STATE:(g + 1) * N_STATE]
            C = bc_ref[:, 256 + g * N_STATE:256 + (g + 1) * N_STATE]
            cb = _dot(C, B, _NT)
            Hg, dHg = H[gs, :], dHn[gs, :]
            Q = _dot(B, dHg, _NT)
            yoff_parts.append(_dot(C, Hg, _NT) * expA[:, gs])
            q_parts.append(Q)
            dcb = jnp.zeros((CHUNK, CHUNK), F32)
            for j in range(8):
                h = g * 8 + j
                hs = slice(h * HEAD, (h + 1) * HEAD)
                L = jnp.exp(jnp.where(causal, A[:, h:h + 1] - AT[h:h + 1, :], -1e30))
                M = cb * L
                G = _dot(dy[:, hs], xdt[:, hs], _NT)
                dxbuf[:, hs] = _dot(M, dy[:, hs], _TN)
                dcb = dcb + G * L
                E = G * M
                dAc = jnp.where(lane == h, jnp.sum(E, axis=1, keepdims=True), dAc)
                Rm = jnp.where(sub == h, jnp.sum(E, axis=0, keepdims=True), Rm)
            dbc_ref[:, g * N_STATE:(g + 1) * N_STATE] = _dot(dcb, C, _TN) + _dot(xdec[:, gs], dHg)
            dbc_ref[:, 256 + g * N_STATE:256 + (g + 1) * N_STATE] = _dot(dcb, B) + _dot(dye[:, gs], Hg)
            dHs[gs, :] = cd[gs, :] * dHg + _dot(dye[:, gs], C, _TN)
        yoff = jnp.concatenate(yoff_parts, axis=1)
        Qd = jnp.concatenate(q_parts, axis=1) * dec
        dxdt = dxbuf[...] + Qd
        extm = ext_ref[...]
        red_s = _dot_sel_b(xdt * Qd, extm)
        dA = dAc - Rm.T + _dot_sel_b(dy * yoff, extm) - red_s
        hd = jnp.sum(_dot_sel_b(H * dHn, extm, _TN), axis=0, keepdims=True)
        last_add = jnp.sum(red_s, axis=0, keepdims=True) + jnp.exp(A[CHUNK - 1:CHUNK, :]) * hd
        dA = dA + jnp.where(sub == CHUNK - 1, last_add, 0.0)
        dadt = _dot_sel_a(triu_ref[...], dA)
        ddt_ref[...] = dadt * a + _dot_sel_b(dxdt * xs, extm)
        dalog_ref[...] += jnp.sum(dadt * dt, axis=0, keepdims=True) * a
        dxs_ref[...] = dxdt * dtex + dsk_ref[...] * dy

        @pl.when(c == nc - 1)
        def _():
            ddsk_ref[...] = _dot_sel_b(jnp.broadcast_to(dskacc[...], (8, 1024)), extm)[0:1, :]

    return pl.pallas_call(
        body, name="ssd_bwd", grid=(nc,),
        in_specs=[_ANY, rev(1024), rev(512), rev(LANES), rev(1024, C_Z), rev(1024),
                  pl.BlockSpec((1, 1024, N_STATE), lambda c: (nc - 1 - c, 0, 0)), rev(1024, 0),
                  _full((1, LANES)), _full((1, 1024)), _full((1, 1024)),
                  _full((LANES, 1024)), _full((1024, LANES)), _full((CHUNK, CHUNK)), _full((CHUNK, CHUNK))],
        out_specs=[rev(1024), rev(512), rev(LANES), rev(1024, C_Z), _full((1, 1024)), _full((1, LANES)), _full((1, LANES))],
        out_shape=[S((T, 1024), F32), S((T, 512), F32), S((T, LANES), F32), S(dproj.shape, BF16),
                   S((1, 1024), F32), S((1, LANES), F32), S((1, LANES), F32)],
        input_output_aliases={0: 3},
        scratch_shapes=[pltpu.VMEM((1024, N_STATE), F32), pltpu.VMEM((CHUNK, 1024), F32), pltpu.VMEM((1, 1024), F32)],
        compiler_params=_params(("arbitrary",)))(
            dproj, xs_c, bc_c, dt, proj, ypre, hprev, dmix, alog, dskip_row, norm_g, ex, ext, tri, triu)


def _shifted_copies(ext, ext8):
    n = ext8.shape[1]
    for r in range(8):
        ext8[r] = ext[pl.ds(r, n), :]


def _shifted(ext8, off, rows):
    return ext8[off % 8, pl.ds(off - off % 8, rows), :]


def _conf_fwd(mix, mixt, proj, w, cb, lg, lb, ba, bb):
    T = proj.shape[0]
    H = HALO_CONF

    def body(mix_ref, mixt_ref, ga_ref, gap_ref, gb_ref, gbp_ref, cg_ref, w_ref, cb_ref, lg_ref, lb_ref, ba_ref,
             bb_ref, u1_ref, yc_ref, yct_ref, ext, ext8):
        first = pl.program_id(0) == 0
        ext[H + TB:, :] = jnp.zeros((8, LANES), F32)

        def blk(cols):
            up = (gap_ref[:, cols] + ba_ref[:, cols]) * _sigmoid(gbp_ref[:, cols] + bb_ref[:, cols])
            ext[0:H, :] = jnp.where(first, 0.0, up)
            ext[H:H + TB, :] = (ga_ref[:, cols] + ba_ref[:, cols]) * _sigmoid(gb_ref[:, cols] + bb_ref[:, cols])
            _shifted_copies(ext, ext8)
            for r0 in range(0, TB, 64):
                acc = jnp.broadcast_to(cb_ref[:, cols], (64, LANES))
                for k in range(K_CONF):
                    acc = acc + w_ref[k:k + 1, cols] * _shifted(ext8, r0 + H - (K_CONF - 1) + k, 64)
                u1_ref[pl.ds(r0, 64), cols] = acc
        _col_loop(D_CONF, blk)

        def rows(rs):
            xh, _ = _ln_stats(u1_ref[rs, :])
            u2 = xh * lg_ref[...] + lb_ref[...]
            cg = cg_ref[rs, :]
            yc_ref[rs, :] = (u2 * _sigmoid(u2) * cg * _sigmoid(cg)).astype(BF16)
        _row_loop(TB, rows)
        yct_ref[...] = yc_ref[...].T

    return pl.pallas_call(
        body, name="conf_fwd", grid=(T // TB,),
        in_specs=[_ANY, _ANY, _row(TB, 1024, C_GLUA), _prev(TB, H, 1024, C_GLUA), _row(TB, 1024, C_GLUB),
                  _prev(TB, H, 1024, C_GLUB), _row(TB, 1024, C_CG), _full((K_CONF, 1024))] + [_full((1, 1024))] * 5,
        out_specs=[_row(TB, 1024), _row(TB, 1024, 1), _colt(1024, TB, 1)],
        out_shape=[S((T, 1024), F32), S((T, 2048), BF16), S((2048, T), BF16)],
        input_output_aliases={0: 1, 1: 2},
        scratch_shapes=[pltpu.VMEM((H + TB + 8, LANES), F32), pltpu.VMEM((8, H + TB, LANES), F32)],
        compiler_params=_params(("parallel",)))(mix, mixt, proj, proj, proj, proj, proj, w, cb, lg, lb, ba, bb)


def _conf_bwd1(dmix, u1, proj, lg, lb):
    T = u1.shape[0]

    def body(dy_ref, u1_ref, cg_ref, lg_ref, lb_ref, du1_ref, dcg_ref, dg_ref, db_ref):
        @pl.when(pl.program_id(0) == 0)
        def _():
            dg_ref[...] = jnp.zeros_like(dg_ref)
            db_ref[...] = jnp.zeros_like(db_ref)

        def rows(rs):
            xh, r = _ln_stats(u1_ref[rs, :])
            u2 = xh * lg_ref[...] + lb_ref[...]
            s2 = _sigmoid(u2)
            cg = cg_ref[rs, :]
            sc = _sigmoid(cg)
            dy = dy_ref[rs, :]
            dcg_ref[rs, :] = (dy * u2 * s2 * _dsilu(cg, sc)).astype(BF16)
            dv, dg, db = _ln_bwd(dy * cg * sc * _dsilu(u2, s2), xh, r, lg_ref[...])
            dg_ref[...] += dg
            db_ref[...] += db
            du1_ref[rs, :] = dv
        _row_loop(TB, rows)

    return pl.pallas_call(
        body, name="conf_bwd1", grid=(T // TB,),
        in_specs=[_row(TB, 1024, 1), _row(TB, 1024), _row(TB, 1024, C_CG), _full((1, 1024)), _full((1, 1024))],
        out_specs=[_row(TB, 1024), _row(TB, 1024, C_CG), _full((1, 1024)), _full((1, 1024))],
        out_shape=[S((T, 1024), F32), S((T, N_MAIN), BF16), S((1, 1024), F32), S((1, 1024), F32)],
        compiler_params=_params(("arbitrary",)))(dmix, u1, proj, lg, lb)


def _conf_bwd2(dproj, proj, du1, w, ba, bb):
    T = du1.shape[0]
    nt = T // TB
    H = HALO_CONF

    def body(dproj_ref, ga_ref, gap_ref, gb_ref, gbp_ref, du_ref, dun_ref, w_ref, ba_ref, bb_ref,
             dg_ref, dw_ref, dcb_ref, dba_ref, dbb_ref, ext, dext, ext8, dext8, dwacc):
        i = pl.program_id(0)
        first, last = i == 0, i == nt - 1

        @pl.when(first)
        def _():
            for r in (dcb_ref, dba_ref, dbb_ref, dwacc):
                r[...] = jnp.zeros_like(r)

        ext[H + TB:, :] = jnp.zeros((8, LANES), F32)
        dext[H + TB:, :] = jnp.zeros((8, LANES), F32)

        def blk(cols):
            cols_b = pl.ds(pl.multiple_of(cols.start + D_CONF, LANES), LANES)
            up = (gap_ref[:, cols] + ba_ref[:, cols]) * _sigmoid(gbp_ref[:, cols] + bb_ref[:, cols])
            ext[0:H, :] = jnp.where(first, 0.0, up)
            a = ga_ref[:, cols] + ba_ref[:, cols]
            sb = _sigmoid(gb_ref[:, cols] + bb_ref[:, cols])
            ext[H:H + TB, :] = a * sb
            du = du_ref[:, cols]
            dext[0:TB, :] = du
            dext[TB:TB + H, :] = jnp.where(last, 0.0, dun_ref[:, cols])
            _shifted_copies(ext, ext8)
            _shifted_copies(dext, dext8)
            dcb_ref[:, cols] += jnp.sum(du, axis=0, keepdims=True)
            for r0 in range(0, TB, 64):
                dur = du_ref[pl.ds(r0, 64), cols]
                acc = jnp.zeros((64, LANES), F32)
                for k in range(K_CONF):
                    prod = dur * _shifted(ext8, r0 + H - (K_CONF - 1) + k, 64)
                    dwacc[k * 8:(k + 1) * 8, cols] += prod.reshape(8, 8, LANES).sum(axis=0)
                    acc = acc + w_ref[k:k + 1, cols] * _shifted(dext8, r0 + K_CONF - 1 - k, 64)
                ar, sr = a[r0:r0 + 64], sb[r0:r0 + 64]
                da = acc * sr
                dbv = acc * ar * sr * (1.0 - sr)
                dg_ref[pl.ds(r0, 64), cols] = da.astype(BF16)
                dg_ref[pl.ds(r0, 64), cols_b] = dbv.astype(BF16)
                dba_ref[:, cols] += jnp.sum(da, axis=0, keepdims=True)
                dbb_ref[:, cols] += jnp.sum(dbv, axis=0, keepdims=True)
        _col_loop(D_CONF, blk)

        @pl.when(last)
        def _():
            dw_ref[...] = jnp.sum(dwacc[...].reshape(K_CONF, 8, D_CONF), axis=1)

    return pl.pallas_call(
        body, name="conf_bwd2", grid=(nt,),
        in_specs=[_ANY, _row(TB, 1024, C_GLUA), _prev(TB, H, 1024, C_GLUA), _row(TB, 1024, C_GLUB),
                  _prev(TB, H, 1024, C_GLUB), _row(TB, 1024), _next(TB, H, 1024, nt), _full((K_CONF, 1024)),
                  _full((1, 1024)), _full((1, 1024))],
        out_specs=[_row(TB, 2048), _full((K_CONF, 1024)), _full((1, 1024)), _full((1, 1024)), _full((1, 1024))],
        out_shape=[S(dproj.shape, BF16), S((K_CONF, 1024), F32)] + [S((1, 1024), F32)] * 3,
        input_output_aliases={0: 0},
        scratch_shapes=[pltpu.VMEM((H + TB + 8, LANES), F32), pltpu.VMEM((TB + H + 8, LANES), F32),
                        pltpu.VMEM((8, H + TB, LANES), F32), pltpu.VMEM((8, TB + H, LANES), F32),
                        pltpu.VMEM((K_CONF * 8, D_CONF), F32)],
        compiler_params=_params(("arbitrary",)))(dproj, proj, proj, proj, proj, du1, du1, w, ba, bb)


def _mesh_pos():
    x, y, c = lax.axis_index("x"), lax.axis_index("y"), lax.axis_index("c")
    return x, y, c, 4 * x + 2 * y + c


def _peer(x, y, c, k):
    return (x ^ ((k >> 2) & 1), y ^ ((k >> 1) & 1), c ^ (k & 1))


def _exchange_copies(ins, outs, kinds, send, recv, loc):
    nb = len(ins)
    x, y, c, me = _mesh_pos()
    src = lambda b, d: ins[b].at[d] if kinds[b] == "blocks" else ins[b]
    copies = [pltpu.make_async_copy(src(b, me), outs[b].at[me], loc.at[b]) for b in range(nb)]
    for k in range(1, N_DEV):
        px, py, pc = _peer(x, y, c, k)
        for b in range(nb):
            s = (k - 1) * nb + b
            copies.append(pltpu.make_async_remote_copy(
                src_ref=src(b, 4 * px + 2 * py + pc), dst_ref=outs[b].at[me], send_sem=send.at[s], recv_sem=recv.at[s],
                device_id=(px, py, pc), device_id_type=pl.DeviceIdType.MESH))
    return copies


def _exchange_shapes(bufs, kinds):
    return [S(b.shape if kd == "blocks" else (N_DEV,) + b.shape, b.dtype) for b, kd in zip(bufs, kinds)]


def _exchange_sems(nb):
    n = (N_DEV - 1) * nb
    return [pltpu.SemaphoreType.DMA((n,)), pltpu.SemaphoreType.DMA((n,)), pltpu.SemaphoreType.DMA((nb,))]


def _two_level_gather(ins, outs, send, recv, loc):
    nb = len(ins)
    x, y, c, me = _mesh_pos()
    here, sibling = (x, y, c), (x, y, 1 - c)
    chips = [(1 - x, y), (x, 1 - y), (1 - x, 1 - y)]

    def copy(slot, b, block, to, src=None):
        d = 4 * block[0] + 2 * block[1] + block[2]
        return pltpu.make_async_remote_copy(
            src_ref=outs[b].at[d] if src is None else src, dst_ref=outs[b].at[d],
            send_sem=send.at[slot * nb + b], recv_sem=recv.at[slot * nb + b],
            device_id=to, device_id_type=pl.DeviceIdType.MESH)

    mine = [pltpu.make_async_copy(ins[b], outs[b].at[me], loc.at[b]) for b in range(nb)]
    first = [copy(0, b, here, sibling, src=ins[b]) for b in range(nb)]
    first += [copy(1 + j, b, here, (*chip, c), src=ins[b]) for j, chip in enumerate(chips) for b in range(nb)]

    def start():
        for cp in mine + first:
            cp.start()

    def finish():
        passed = []
        for j, chip in enumerate(chips):
            for b in range(nb):
                copy(1 + j, b, (*chip, c), here).wait_recv()
            onward = [copy(4 + j, b, (*chip, c), sibling) for b in range(nb)]
            for cp in onward:
                cp.start()
            passed += onward
        for b in range(nb):
            copy(0, b, sibling, here).wait_recv()
        for j, chip in enumerate(chips):
            for b in range(nb):
                copy(4 + j, b, (*chip, 1 - c), here).wait_recv()
        for cp in first + passed:
            cp.wait_send()
        for cp in mine:
            cp.wait()

    return start, finish


def _exchange_plan(ins, outs, kinds, send, recv, loc):
    if all(kd == "gather" for kd in kinds):
        return _two_level_gather(ins, outs, send, recv, loc)
    copies = _exchange_copies(ins, outs, kinds, send, recv, loc)

    def start():
        for cp in copies:
            cp.start()

    def finish():
        for cp in copies:
            cp.wait()

    return start, finish


def _exchange(bufs, kinds, name):
    nb = len(bufs)

    def body(*refs):
        start, finish = _exchange_plan(refs[:nb], refs[nb:2 * nb], kinds, *refs[2 * nb:])
        start()
        finish()

    return pl.pallas_call(
        body, name=name, in_specs=[_ANY] * nb, out_specs=[_ANY] * nb,
        out_shape=_exchange_shapes(bufs, kinds), scratch_shapes=_exchange_sems(nb))(*bufs)


def _sum_parts(p_ref):
    acc = p_ref[0].astype(F32)
    for d in range(1, N_DEV):
        acc = acc + p_ref[d].astype(F32)
    return acc


def _adamw_math(g, w, m, v):
    m = ADAM_B1 * m + (1.0 - ADAM_B1) * g
    v = ADAM_B2 * v + (1.0 - ADAM_B2) * (g * g)
    m_hat = m / (1.0 - ADAM_B1 ** ADAM_STEP)
    v_hat = v / (1.0 - ADAM_B2 ** ADAM_STEP)
    return -ADAM_LR * (m_hat / (jnp.sqrt(v_hat) + ADAM_EPS) + ADAM_WD * w), m, v


def _sum8_adamw(parts, w, m, v, name):
    _, R, C = parts.shape
    tb = 256 if R % 256 == 0 else R

    def body(p_ref, w_ref, m_ref, v_ref, g_ref, d_ref, mo_ref, vo_ref):
        g = _sum_parts(p_ref)
        g_ref[...] = g
        d_ref[...], mo_ref[...], vo_ref[...] = _adamw_math(g, w_ref[...], m_ref[...], v_ref[...])

    return pl.pallas_call(
        body, name=name, grid=(R // tb,),
        in_specs=[pl.BlockSpec((N_DEV, tb, C), lambda i: (0, i, 0))] + [_row(tb, C)] * 3, out_specs=[_row(tb, C)] * 4,
        out_shape=[S((R, C), F32)] * 4, compiler_params=_params(("parallel",)))(parts, w, m, v)


SMALL_LAYOUT = (
    ("ln_emb_g", 0, 1024), ("ln_emb_b", 0, 1024), ("ssm_conv_b", 0, 1024), ("ssm_conv_b", 1024, 512),
    ("dt_bias", 0, N_HEADS), ("a_log", 0, N_HEADS), ("d_skip", 0, N_HEADS), ("ssm_norm_g", 0, 1024),
    ("b_glu", 0, 1024), ("b_glu", 1024, 1024), ("conf_conv_b", 0, 1024), ("conf_ln_g", 0, 1024),
    ("conf_ln_b", 0, 1024), ("b_out", 0, 1024), ("ln1_g", 0, 1024), ("ln1_b", 0, 1024), ("ln2_g", 0, 1024),
    ("ln2_b", 0, 1024))
SMALL_ROWS = 24
SMALL = tuple(dict.fromkeys(n for n, _, _ in SMALL_LAYOUT))


LOSS_ROW = len(SMALL_LAYOUT)


def _pack_small(rows, loss):
    def body(*refs):
        o_ref = refs[-1]
        o_ref[...] = jnp.zeros_like(o_ref)
        for r, ref in enumerate(refs[:-2]):
            o_ref[r:r + 1, 0:ref.shape[1]] = ref[...]
        o_ref[LOSS_ROW:LOSS_ROW + 1, 0:LANES] = refs[-2][0:1, :]

    return pl.pallas_call(body, name="pack_small", out_shape=S((SMALL_ROWS, 1024), F32))(*rows, loss)


def _small_update(parts, w, m, v):
    def body(*refs):
        p_ref = refs[0]
        ins = {n: refs[1 + 3 * i:4 + 3 * i] for i, n in enumerate(SMALL)}
        o0 = 1 + 3 * len(SMALL)
        outs = {n: refs[o0 + 4 * i:o0 + 4 * i + 4] for i, n in enumerate(SMALL)}
        gsum = refs[-1]
        gsum[...] = _sum_parts(p_ref)
        refs[-2][...] = gsum[LOSS_ROW:LOSS_ROW + 1, 0:LANES]
        for r, (n, off, wd) in enumerate(SMALL_LAYOUT):
            cs = slice(off, off + wd)
            g = gsum[r:r + 1, 0:wd]
            w_ref, m_ref, v_ref = ins[n]
            g_ref, d_ref, mo_ref, vo_ref = outs[n]
            g_ref[:, cs] = g
            d_ref[:, cs], mo_ref[:, cs], vo_ref[:, cs] = _adamw_math(g, w_ref[:, cs], m_ref[:, cs], v_ref[:, cs])

    args = [parts] + [a for n in SMALL for a in (w[n], m[n], v[n])]
    res = pl.pallas_call(
        body, name="small_update",
        out_shape=[S(w[n].shape, F32) for n in SMALL for _ in range(4)] + [S((1, LANES), F32)],
        scratch_shapes=[pltpu.VMEM((SMALL_ROWS, 1024), F32)])(*args)
    return tuple({n: res[4 * i + j] for i, n in enumerate(SMALL)} for j in range(4)) + (res[-1],)


EARLY = ("w_in", "ssm_conv_w", "conf_conv_w")
LATE = ("w_out", "w_ple_gate", "w_ple_proj")


def _local_step(x, p, tgt, W, shards=None):
    r1 = lambda v: v.reshape(1, -1).astype(F32)
    pad_l = lambda v: jnp.pad(r1(v), ((0, 0), (0, LANES - v.size)))
    late = None if shards is None else [shards[n] for n in LATE]
    if shards is None:
        h0, h0b, h0bt = _ln_emb_fwd(x, r1(W["ln_emb_g"]), r1(W["ln_emb_b"]))
    else:
        h0, h0b, h0bt, *gathered = _ln_emb_fwd(x, r1(W["ln_emb_g"]), r1(W["ln_emb_b"]),
                                               exchange=([shards[n] for n in EARLY], ("gather",) * len(EARLY)))
        W = dict(W, **{n: a if n == "w_in" else _unstack_shards(a, BY_COLS[n]) for n, a in zip(EARLY, gathered)})
    w_main, w_dt = _w_in_to_main(W["w_in"])
    scw, scb = W["ssm_conv_w"], r1(W["ssm_conv_b"])
    wx, wb, bx, bb = scw[:, :1024], scw[:, 1024:], scb[:, :1024], scb[:, 1024:]
    dt_bias, alog = pad_l(W["dt_bias"]), pad_l(W["a_log"])
    dskip_row = jnp.repeat(W["d_skip"].reshape(-1), HEAD).reshape(1, -1)
    norm_g = r1(W["ssm_norm_g"])
    bglu = r1(W["b_glu"])
    ba, bbg = bglu[:, :1024], bglu[:, 1024:]
    ccw, ccb, clg, clb = W["conf_conv_w"], r1(W["conf_conv_b"]), r1(W["conf_ln_g"]), r1(W["conf_ln_b"])

    if late is None:
        proj = _mm(h0b, w_main, "nn", "in_proj")
    else:
        proj, *gathered = _mm(h0b, w_main, "nn", "in_proj", exchange=(late, ("gather",) * len(LATE)))
        W = dict(W, **{n: _unstack_shards(a, BY_COLS[n]) for n, a in zip(LATE, gathered)})
    dt_raw = _mm(h0b, w_dt, "nn", "in_proj_dt")
    xs_c, bc_c, dt = _ssd_pre_fwd(proj, dt_raw, wx, wb, bx, bb, dt_bias)
    mix, ypre, hprev, mixt = _ssd_fwd(xs_c, bc_c, dt, proj, alog, dskip_row, norm_g)
    u1, mix, mixt = _conf_fwd(mix, mixt, proj, ccw, ccb, clg, clb, ba, bbg)
    out = _mm(mix, W["w_out"], "nn", "out_proj")
    h1, h1b, h1bt = _post1_fwd(h0, out, r1(W["b_out"]), r1(W["ln1_g"]), r1(W["ln1_b"]))
    gpre = _mm(h1b, W["w_ple_gate"], "nn", "ple_gate")
    pb = p.astype(BF16)
    ple = _mm(pb, W["w_ple_proj"], "nn", "ple_proj")
    dh1a, dgp, dple, loss, dln2g, dln2b = _post2(h1, gpre, ple, tgt, r1(W["ln2_g"]), r1(W["ln2_b"]))

    g = {}
    g["w_ple_proj"] = _mm(pb.T, dple, "nn", "d_ple_proj", out_dtype=BF16)
    g["w_ple_gate"] = _mm(h1bt, dgp, "nn", "d_ple_gate", out_dtype=BF16)
    dh1b = _mm(dgp, W["w_ple_gate"], "nt", "d_h1")
    dout, dh0a, dln1g, dln1b, dbout = _post1_bwd(dh1a, dh1b, h0, out, r1(W["b_out"]), r1(W["ln1_g"]))
    g["w_out"] = _mm(mixt, dout, "nn", "d_w_out", out_dtype=BF16)
    dmix = _mm(dout, W["w_out"], "nt", "d_mix")
    du1, dproj, dclg, dclb = _conf_bwd1(dmix, u1, proj, clg, clb)
    dproj, g["conf_conv_w"], dccb, dba, dbb = _conf_bwd2(dproj, proj, du1, ccw, ba, bbg)
    dxs_c, dbc_c, ddt, dproj, dng, ddsk, dalog = _ssd_bwd(
        dproj, xs_c, bc_c, dt, proj, ypre, hprev, dmix, alog, dskip_row, norm_g)
    dproj, dwx, dbx = _ssd_conv_bwd(dproj, proj, dxs_c, wx, bx, 1024, C_XS, "ssd_conv_bwd_x")
    dproj, dwb, dbb2 = _ssd_conv_bwd(dproj, proj, dbc_c, wb, bb, 512, C_BC, "ssd_conv_bwd_bc")
    ddtr, ddtb = _dt_bwd(ddt, dt_raw, dt_bias)
    g["ssm_conv_w"] = jnp.concatenate([dwx, dwb], axis=1)
    dw_dt = _mm(h0bt, ddtr, "nn", "d_w_dt", out_dtype=BF16)
    stack = lambda names: [_stack_shards(g[n], BY_COLS[n]) for n in names]
    last_args = (dproj, w_main, ddtr, w_dt, dh0a, x, r1(W["ln_emb_g"]))
    if late is None:
        g["w_in"] = _w_in_blocks(_mm(h0bt, dproj, "nn", "d_w_in", out_dtype=BF16), dw_dt)
        grad_x, dlng, dlnb = _d_h0_ln_bwd(*last_args)
    else:
        dw_main, *recv_a = _mm(h0bt, dproj, "nn", "d_w_in", out_dtype=BF16,
                               exchange=(stack(LATE), ("blocks",) * len(LATE)))
        last = ("ssm_conv_w", "conf_conv_w")
        blocks = [_w_in_blocks(dw_main, dw_dt)] + stack(last)
        grad_x, dlng, dlnb, *recv_b = _d_h0_ln_bwd(*last_args, exchange=(blocks, ("blocks",) * 3))
        g["recv"] = dict(zip(LATE + ("w_in",) + last, recv_a + recv_b))
    g["rows"] = [dlng, dlnb, dbx, dbb2, ddtb, dalog, ddsk, dng, dba, dbb, dccb, dclg, dclb, dbout, dln1g, dln1b,
                 dln2g, dln2b]
    return loss, grad_x, g


W_IN_SEGMENTS = ((0, 2048, 2048), (2048, 5120, 512), (2560, None, N_HEADS), (2576, 0, 2048), (4624, 4096, 1024))


def _w_in_to_main(shards):
    def pieces(p0, width):
        out, p = [], p0
        while p < p0 + width:
            d = p // COLS_PER_DEV
            hi = min(p0 + width, (d + 1) * COLS_PER_DEV)
            out.append(shards[d][:, p - d * COLS_PER_DEV:hi - d * COLS_PER_DEV])
            p = hi
        return out
    main = [s for s in sorted(W_IN_SEGMENTS, key=lambda s: -1 if s[1] is None else s[1]) if s[1] is not None]
    w_main = jnp.concatenate([q for p0, _, width in main for q in pieces(p0, width)], axis=1)
    w_dt = jnp.concatenate(pieces(2560, N_HEADS), axis=1)
    return w_main, jnp.pad(w_dt, ((0, 0), (0, LANES - N_HEADS)))


def _w_in_blocks(dw_main, dw_dt):
    blocks = []
    for d in range(N_DEV):
        lo_d, hi_d = d * COLS_PER_DEV, (d + 1) * COLS_PER_DEV
        parts = []
        for p0, m0, width in W_IN_SEGMENTS:
            lo, hi = max(lo_d, p0), min(hi_d, p0 + width)
            if lo < hi:
                parts.append(dw_dt[:, lo - p0:hi - p0] if m0 is None else dw_main[:, m0 + lo - p0:m0 + hi - p0])
        blocks.append(jnp.concatenate(parts, axis=1))
    return jnp.stack(blocks)


WEIGHTS = ['ln_emb_g', 'ln_emb_b', 'w_in', 'ssm_conv_w', 'ssm_conv_b', 'dt_bias', 'a_log', 'd_skip', 'ssm_norm_g',
           'b_glu', 'conf_conv_w', 'conf_conv_b', 'conf_ln_g', 'conf_ln_b', 'w_out', 'b_out', 'ln1_g', 'ln1_b',
           'w_ple_gate', 'w_ple_proj', 'ln2_g', 'ln2_b']
SHARDED = (("w_in", True), ("w_out", False), ("w_ple_gate", False), ("w_ple_proj", True), ("ssm_conv_w", True),
           ("conf_conv_w", True))
BY_COLS = dict(SHARDED)


def _stack_shards(a, by_cols):
    if by_cols:
        return a.reshape(a.shape[0], N_DEV, a.shape[1] // N_DEV).transpose(1, 0, 2)
    return a.reshape(N_DEV, a.shape[0] // N_DEV, a.shape[1])


def _unstack_shards(a, by_cols):
    if by_cols:
        return a.transpose(1, 0, 2).reshape(a.shape[1], N_DEV * a.shape[2])
    return a.reshape(N_DEV * a.shape[1], a.shape[2])


def kernel(x, p, ln_emb_g, ln_emb_b, w_in, ssm_conv_w, ssm_conv_b, dt_bias, a_log, d_skip, ssm_norm_g, b_glu, conf_conv_w, conf_conv_b, conf_ln_g, conf_ln_b, w_out, b_out, ln1_g, ln1_b, w_ple_gate, w_ple_proj, ln2_g, ln2_b, loss_target, m_ln_emb_g, m_ln_emb_b, m_w_in, m_ssm_conv_w, m_ssm_conv_b, m_dt_bias, m_a_log, m_d_skip, m_ssm_norm_g, m_b_glu, m_conf_conv_w, m_conf_conv_b, m_conf_ln_g, m_conf_ln_b, m_w_out, m_b_out, m_ln1_g, m_ln1_b, m_w_ple_gate, m_w_ple_proj, m_ln2_g, m_ln2_b, v_ln_emb_g, v_ln_emb_b, v_w_in, v_ssm_conv_w, v_ssm_conv_b, v_dt_bias, v_a_log, v_d_skip, v_ssm_norm_g, v_b_glu, v_conf_conv_w, v_conf_conv_b, v_conf_ln_g, v_conf_ln_b, v_w_out, v_b_out, v_ln1_g, v_ln1_b, v_w_ple_gate, v_w_ple_proj, v_ln2_g, v_ln2_b):
    loc = dict(locals())
    w = {n: loc[n] for n in WEIGHTS}
    m = {n: loc["m_" + n] for n in WEIGHTS}
    v = {n: loc["v_" + n] for n in WEIGHTS}
    sharded = [n for n, _ in SHARDED]

    shards = {n: w[n][0].astype(BF16) if n.startswith("w_") else w[n][0] for n in sharded}
    W = {n: w[n].reshape(-1) for n in SMALL}
    loss, grad_x, g = _local_step(x[0], p[0, 0], loss_target[0], W, shards=shards)
    (recv_small,) = _exchange([_pack_small(g["rows"], loss)], ("all",), "small_exchange")

    grads, delta, new_m, new_v = {}, {}, {}, {}
    for n in sharded:
        res = _sum8_adamw(g["recv"][n], w[n][0], m[n][0], v[n][0], "adamw_" + n)
        grads[n], delta[n], new_m[n], new_v[n] = (r[None] for r in res)
    two_d = lambda d: {n: d[n].reshape(1, -1) for n in SMALL}
    *small, loss = _small_update(recv_small, two_d(w), two_d(m), two_d(v))
    for dst, res in zip((grads, delta, new_m, new_v), small):
        for n in SMALL:
            dst[n] = res[n].reshape(w[n].shape)
    return (loss[0, 0], grad_x[None], *[grads[n] for n in WEIGHTS], *[delta[n] for n in WEIGHTS],
            *[new_m[n] for n in WEIGHTS], *[new_v[n] for n in WEIGHTS])
```

```python
import functools

import numpy as np
import jax
import jax.numpy as jnp
from jax import lax
from jax.experimental import pallas as pl
from jax.experimental.pallas import tpu as pltpu

F32, BF16 = jnp.float32, jnp.bfloat16
S = jax.ShapeDtypeStruct

N_DEV = 8
D = 1024
D_PLE = 256
D_SSM = 1024
D_CONF = 1024
N_HEADS = 16
HEAD = 64
N_STATE = 128
CHUNK = 128
K_SSM = 4
K_CONF = 31
D_IN = 5648
COLS_PER_DEV = D_IN // N_DEV
LN_EPS = 1e-5
RMS_EPS = 1e-5
ALPHA = 2.0 ** 0.25
LANES = 128
TB = 256
RG = 32
ROW_UNROLL = 4
HALO_SSM = 8
HALO_CONF = 32
VMEM_LIMIT = 56 * 1024 * 1024

ADAM_LR, ADAM_B1, ADAM_B2, ADAM_EPS, ADAM_WD, ADAM_STEP = 0.001, 0.9, 0.999, 1e-08, 0.01, 10

C_GLUA, C_GLUB, C_XS, C_Z, C_CG = 0, 1, 2, 3, 4
C_BC = 10
N_MAIN = 5632


def _params(sem, vmem=VMEM_LIMIT):
    return pltpu.CompilerParams(dimension_semantics=sem, vmem_limit_bytes=vmem)


def _row(tb, n, col=0):
    return pl.BlockSpec((tb, n), lambda i: (i, col))


def _colt(n, tb, row=0):
    return pl.BlockSpec((n, tb), lambda i: (row, i))


def _full(shape):
    return pl.BlockSpec(shape, lambda i: (0,) * len(shape))


_ANY = pl.BlockSpec(memory_space=pl.ANY)


def _prev(tb, halo, n, col=0):
    r = tb // halo
    return pl.BlockSpec((halo, n), lambda i: (jnp.maximum(i * r - 1, 0), col))


def _next(tb, halo, n, nt, col=0):
    r = tb // halo
    return pl.BlockSpec((halo, n), lambda i: (jnp.minimum((i + 1) * r, nt * r - 1), col))


def _row_loop(tb, fn):
    def it(r, c):
        fn(pl.ds(pl.multiple_of(r * RG, RG), RG))
        return c
    lax.fori_loop(0, tb // RG, it, 0, unroll=ROW_UNROLL)


def _col_loop(n, fn):
    def it(j, c):
        fn(pl.ds(pl.multiple_of(j * LANES, LANES), LANES))
        return c
    lax.fori_loop(0, n // LANES, it, 0)


def _sigmoid(x):
    return 1.0 / (1.0 + jnp.exp(-x))


def _dsilu(x, s):
    return s * (1.0 + x * (1.0 - s))


def _ln_stats(v):
    mu = jnp.mean(v, axis=-1, keepdims=True)
    c = v - mu
    r = lax.rsqrt(jnp.mean(c * c, axis=-1, keepdims=True) + LN_EPS)
    return c * r, r


def _ln_bwd(dy, xhat, r, g):
    dxh = dy * g
    dv = r * (dxh - jnp.mean(dxh, axis=-1, keepdims=True) - xhat * jnp.mean(dxh * xhat, axis=-1, keepdims=True))
    return dv, jnp.sum(dy * xhat, axis=0, keepdims=True), jnp.sum(dy, axis=0, keepdims=True)


def _dot(a, b, dims=((1,), (0,))):
    return lax.dot_general(a.astype(BF16), b.astype(BF16), (dims, ((), ())), preferred_element_type=F32)


_NT = ((1,), (1,))
_TN = ((0,), (0,))


def _split3(x):
    hi = x.astype(BF16)
    r = x - hi.astype(F32)
    mid = r.astype(BF16)
    return hi, mid, (r - mid.astype(F32)).astype(BF16)


def _dot_sel_b(a, b, dims=((1,), (0,))):
    hi, mid, lo = _split3(a)
    return (_dot(lo, b, dims) + _dot(mid, b, dims)) + _dot(hi, b, dims)


def _dot_sel_a(a, b, dims=((1,), (0,))):
    hi, mid, lo = _split3(b)
    return (_dot(a, lo, dims) + _dot(a, mid, dims)) + _dot(a, hi, dims)


def _mm(a, b, mode, name, out_dtype=F32, add=None, tm=1024, tn=None, tk=1024, exchange=None):
    if mode == "nn":
        (M, K), N = a.shape, b.shape[1]
    elif mode == "tn":
        (K, M), N = a.shape, b.shape[1]
    else:
        (M, K), N = a.shape, b.shape[0]
    if tn is None:
        tn = next(t for t in (1024, 1408, 512, 256, LANES) if N % t == 0)
    tm, tn, tk = min(tm, M), min(tn, N), min(tk, K)
    assert M % tm == 0 and N % tn == 0 and K % tk == 0, (name, M, N, K)
    grid = (M // tm, N // tn, K // tk)
    nk = grid[2]
    dims = {"nn": ((1,), (0,)), "tn": _TN, "nt": _NT}[mode]
    n_in = 2 + (add is not None)
    xbufs, kinds = exchange if exchange is not None else ((), ())
    nx = len(xbufs)

    def body(*refs):
        a_ref, b_ref = refs[:2]
        o_ref = refs[n_in + nx]
        acc = refs[n_in + 2 * nx + 1]
        i, j, k = pl.program_id(0), pl.program_id(1), pl.program_id(2)
        if nx:
            start, finish = _exchange_plan(refs[n_in:n_in + nx], refs[n_in + nx + 1:n_in + 2 * nx + 1], kinds,
                                           *refs[n_in + 2 * nx + 2:])
            pl.when((i == 0) & (j == 0) & (k == 0))(start)

        d = _dot(a_ref[...], b_ref[...], dims)

        def write_out(r):
            if add is not None:
                r = r + refs[2][...]
            o_ref[...] = r.astype(out_dtype)

        if nk == 1:
            write_out(d)
        else:
            @pl.when(k == 0)
            def _():
                acc[...] = d

            @pl.when((k > 0) & (k < nk - 1))
            def _():
                acc[...] += d

            @pl.when(k == nk - 1)
            def _():
                write_out(acc[...] + d)

        if nx:
            pl.when((i == grid[0] - 1) & (j == grid[1] - 1) & (k == nk - 1))(finish)

    a_spec = pl.BlockSpec((tk, tm), lambda i, j, k: (k, i)) if mode == "tn" else pl.BlockSpec((tm, tk), lambda i, j, k: (i, k))
    b_spec = pl.BlockSpec((tn, tk), lambda i, j, k: (j, k)) if mode == "nt" else pl.BlockSpec((tk, tn), lambda i, j, k: (k, j))
    o_spec = pl.BlockSpec((tm, tn), lambda i, j, k: (i, j))
    ins, specs = [a, b], [a_spec, b_spec]
    if add is not None:
        ins.append(add)
        specs.append(o_spec)
    acc_spec = pltpu.VMEM((tm, tn) if nk > 1 else (8, LANES), F32)
    if not nx:
        return pl.pallas_call(
            body, name=name, grid=grid, in_specs=specs, out_specs=o_spec,
            out_shape=S((M, N), out_dtype), scratch_shapes=[acc_spec],
            compiler_params=_params(("parallel", "parallel", "arbitrary")))(*ins)
    return pl.pallas_call(
        body, name=name, grid=grid, in_specs=specs + [_ANY] * nx, out_specs=[o_spec] + [_ANY] * nx,
        out_shape=[S((M, N), out_dtype)] + _exchange_shapes(xbufs, kinds),
        scratch_shapes=[acc_spec] + _exchange_sems(nx),
        compiler_params=_params(("arbitrary", "arbitrary", "arbitrary")))(*ins, *xbufs)


def _ln_emb_fwd(x, g, b, exchange=None):
    T = x.shape[0]

    nt = T // TB
    xbufs, kinds = exchange if exchange is not None else ((), ())
    nx = len(xbufs)

    def body(*refs):
        x_ref, g_ref, b_ref = refs[:3]
        h_ref, hb_ref, hbt_ref = refs[3 + nx:6 + nx]
        i = pl.program_id(0)
        if nx:
            start, finish = _exchange_plan(refs[3:3 + nx], refs[6 + nx:6 + 2 * nx], kinds, *refs[6 + 2 * nx:])
            pl.when(i == 0)(start)

        def rows(rs):
            xh, _ = _ln_stats(x_ref[rs, :])
            h = xh * g_ref[...] + b_ref[...]
            h_ref[rs, :] = h
            hb_ref[rs, :] = h.astype(BF16)
        _row_loop(TB, rows)
        hbt_ref[...] = hb_ref[...].T
        if nx:
            pl.when(i == nt - 1)(finish)

    return pl.pallas_call(
        body, name="ln_emb_fwd", grid=(nt,),
        in_specs=[_row(TB, D), _full((1, D)), _full((1, D))] + [_ANY] * nx,
        out_specs=[_row(TB, D), _row(TB, D), _colt(D, TB)] + [_ANY] * nx,
        out_shape=[S((T, D), F32), S((T, D), BF16), S((D, T), BF16)] + _exchange_shapes(xbufs, kinds),
        scratch_shapes=_exchange_sems(nx) if nx else [],
        compiler_params=_params(("arbitrary",)))(x, g, b, *xbufs)


def _out_proj_post1(mix, w_out, h0, b_out, g, b, tm=512, tk=1024):
    T, K = mix.shape
    tm = min(tm, T)
    nk = K // tk
    assert T % tm == 0 and K % tk == 0 and nk >= 2

    def body(mix_ref, w_ref, h0_ref, bo_ref, g_ref, b_ref, out_ref, h_ref, hb_ref, hbt_ref, acc):
        k = pl.program_id(1)
        d = _dot(mix_ref[...], w_ref[...])

        @pl.when(k == 0)
        def _():
            acc[...] = d

        @pl.when((k > 0) & (k < nk - 1))
        def _():
            acc[...] += d

        @pl.when(k == nk - 1)
        def _():
            out_ref[...] = acc[...] + d

            def rows(rs):
                xh, _ = _ln_stats(ALPHA * h0_ref[rs, :] + out_ref[rs, :] + bo_ref[...])
                h = xh * g_ref[...] + b_ref[...]
                h_ref[rs, :] = h
                hb_ref[rs, :] = h.astype(BF16)
            _row_loop(tm, rows)
            hbt_ref[...] = hb_ref[...].T

    rowt = lambda n: pl.BlockSpec((tm, n), lambda i, k: (i, 0))
    const = pl.BlockSpec((1, D), lambda i, k: (0, 0))
    return pl.pallas_call(
        body, name="out_proj_post1", grid=(T // tm, nk),
        in_specs=[pl.BlockSpec((tm, tk), lambda i, k: (i, k)), pl.BlockSpec((tk, D), lambda i, k: (k, 0)), rowt(D),
                  const, const, const],
        out_specs=[rowt(D), rowt(D), rowt(D), pl.BlockSpec((D, tm), lambda i, k: (0, i))],
        out_shape=[S((T, D), F32), S((T, D), F32), S((T, D), BF16), S((D, T), BF16)],
        scratch_shapes=[pltpu.VMEM((tm, D), F32)],
        compiler_params=_params(("parallel", "arbitrary")))(mix, w_out, h0, b_out, g, b)


def _ple_post2(h1b, w_gate, pb, w_proj, h1, tgt, g, b, tm=512):
    T = h1.shape[0]
    tm = min(tm, T)
    assert T % tm == 0

    def body(h1b_ref, wg_ref, pb_ref, wp_ref, h1_ref, tgt_ref, g_ref, b_ref,
             dh1_ref, dgp_ref, dple_ref, loss_ref, dg_ref, db_ref, gp_ref, ple_ref):
        @pl.when(pl.program_id(0) == 0)
        def _():
            loss_ref[...] = jnp.zeros_like(loss_ref)
            dg_ref[...] = jnp.zeros_like(dg_ref)
            db_ref[...] = jnp.zeros_like(db_ref)

        gp_ref[...] = _dot(h1b_ref[...], wg_ref[...])
        ple_ref[...] = _dot(pb_ref[...], wp_ref[...])

        def rows(rs):
            gate = _sigmoid(gp_ref[rs, :])
            ple = ple_ref[rs, :]
            xh, r = _ln_stats(ALPHA * h1_ref[rs, :] + gate * ple)
            err = xh * g_ref[...] + b_ref[...] - tgt_ref[rs, :]
            loss_ref[...] += 0.5 * jnp.sum(jnp.mean(err * err, axis=-1, keepdims=True), axis=0, keepdims=True)
            dv, dg, db = _ln_bwd(err * (1.0 / D), xh, r, g_ref[...])
            dg_ref[...] += dg
            db_ref[...] += db
            dh1_ref[rs, :] = ALPHA * dv
            dgp_ref[rs, :] = (dv * ple * gate * (1.0 - gate)).astype(BF16)
            dple_ref[rs, :] = (dv * gate).astype(BF16)
        _row_loop(tm, rows)

    return pl.pallas_call(
        body, name="ple_post2", grid=(T // tm,),
        in_specs=[_row(tm, D), _full((D, D)), _row(tm, D_PLE), _full((D_PLE, D)), _row(tm, D), _row(tm, D),
                  _full((1, D)), _full((1, D))],
        out_specs=[_row(tm, D)] * 3 + [_full((8, LANES)), _full((1, D)), _full((1, D))],
        out_shape=[S((T, D), F32), S((T, D), BF16), S((T, D), BF16), S((8, LANES), F32), S((1, D), F32), S((1, D), F32)],
        scratch_shapes=[pltpu.VMEM((tm, D), F32), pltpu.VMEM((tm, D), F32)],
        compiler_params=_params(("arbitrary",)))(h1b, w_gate, pb, w_proj, h1, tgt, g, b)


def _d_h1_post1_bwd(dgp, w_gate, dh1a, h0, out, b_out, g, tm=512):
    T = h0.shape[0]
    tm = min(tm, T)
    assert T % tm == 0

    def body(dgp_ref, wg_ref, da_ref, h0_ref, out_ref, bo_ref, g_ref, dout_ref, dh0_ref, dg_ref, db_ref, dbo_ref, dh1):
        @pl.when(pl.program_id(0) == 0)
        def _():
            dg_ref[...] = jnp.zeros_like(dg_ref)
            db_ref[...] = jnp.zeros_like(db_ref)
            dbo_ref[...] = jnp.zeros_like(dbo_ref)

        dh1[...] = da_ref[...] + _dot(dgp_ref[...], wg_ref[...], _NT)

        def rows(rs):
            xh, r = _ln_stats(ALPHA * h0_ref[rs, :] + out_ref[rs, :] + bo_ref[...])
            dv, dg, db = _ln_bwd(dh1[rs, :], xh, r, g_ref[...])
            dg_ref[...] += dg
            db_ref[...] += db
            dbo_ref[...] += jnp.sum(dv, axis=0, keepdims=True)
            dout_ref[rs, :] = dv.astype(BF16)
            dh0_ref[rs, :] = ALPHA * dv
        _row_loop(tm, rows)

    return pl.pallas_call(
        body, name="d_h1_post1_bwd", grid=(T // tm,),
        in_specs=[_row(tm, D), _full((D, D))] + [_row(tm, D)] * 3 + [_full((1, D))] * 2,
        out_specs=[_row(tm, D)] * 2 + [_full((1, D))] * 3,
        out_shape=[S((T, D), BF16), S((T, D), F32)] + [S((1, D), F32)] * 3,
        scratch_shapes=[pltpu.VMEM((tm, D), F32)],
        compiler_params=_params(("arbitrary",)))(dgp, w_gate, dh1a, h0, out, b_out, g)


def _d_h0_ln_bwd(dproj, w_main, ddtr, w_dt, dh0a, x, g, exchange=None, tm=1024, tk=1408):
    T, K = dproj.shape
    tm = min(tm, T)
    assert T % tm == 0 and K % tk == 0
    ni, nk = T // tm, K // tk
    xbufs, kinds = exchange if exchange is not None else ((), ())
    nx = len(xbufs)

    def body(*refs):
        dp_ref, w_ref, dt_ref, wdt_ref, da_ref, x_ref, g_ref = refs[:7]
        dx_ref, dg_ref, db_ref = refs[7 + nx:10 + nx]
        acc = refs[10 + 2 * nx]
        i, k = pl.program_id(0), pl.program_id(1)
        if nx:
            start, finish = _exchange_plan(refs[7:7 + nx], refs[10 + nx:10 + 2 * nx], kinds, *refs[11 + 2 * nx:])
            pl.when((i == 0) & (k == 0))(start)

        @pl.when((i == 0) & (k == 0))
        def _():
            dg_ref[...] = jnp.zeros_like(dg_ref)
            db_ref[...] = jnp.zeros_like(db_ref)

        d = _dot(dp_ref[...], w_ref[...], _NT)

        @pl.when(k == 0)
        def _():
            acc[...] = da_ref[...] + _dot(dt_ref[...], wdt_ref[...], _NT) + d

        @pl.when(k > 0)
        def _():
            acc[...] += d

        @pl.when(k == nk - 1)
        def _():
            def rows(rs):
                xh, r = _ln_stats(x_ref[rs, :])
                dv, dg, db = _ln_bwd(acc[rs, :], xh, r, g_ref[...])
                dg_ref[...] += dg
                db_ref[...] += db
                dx_ref[rs, :] = dv
            _row_loop(tm, rows)

        if nx:
            pl.when((i == ni - 1) & (k == nk - 1))(finish)

    rowt = lambda n: pl.BlockSpec((tm, n), lambda i, k: (i, 0))
    const = lambda shape: pl.BlockSpec(shape, lambda i, k: (0, 0))
    return pl.pallas_call(
        body, name="d_h0_ln_bwd", grid=(ni, nk),
        in_specs=[pl.BlockSpec((tm, tk), lambda i, k: (i, k)), pl.BlockSpec((D, tk), lambda i, k: (0, k)),
                  rowt(LANES), const((D, LANES)), rowt(D), rowt(D), const((1, D))] + [_ANY] * nx,
        out_specs=[rowt(D), const((1, D)), const((1, D))] + [_ANY] * nx,
        out_shape=[S((T, D), F32), S((1, D), F32), S((1, D), F32)] + _exchange_shapes(xbufs, kinds),
        scratch_shapes=[pltpu.VMEM((tm, D), F32)] + (_exchange_sems(nx) if nx else []),
        compiler_params=_params(("arbitrary", "arbitrary")))(dproj, w_main, ddtr, w_dt, dh0a, x, g, *xbufs)


def _softplus(x):
    return jnp.maximum(x, 0.0) + jnp.log1p(jnp.exp(-jnp.abs(x)))


def _ssd_pre_fwd(proj, dt_raw, wx, wb, bx, bb, dt_bias):
    T = proj.shape[0]
    H = HALO_SSM

    def body(xs_ref, xsp_ref, bc_ref, bcp_ref, dtr_ref, wx_ref, wb_ref, bx_ref, bb_ref, dtb_ref,
             xso_ref, bco_ref, dto_ref, extx, extb):
        first = pl.program_id(0) == 0

        def conv(t_ref, p_ref, w_ref, b_ref, o_ref, ext, n):
            def blk(cols):
                ext[0:H, cols] = jnp.where(first, 0.0, p_ref[:, cols])
                ext[H:, cols] = t_ref[:, cols]
                for r0 in range(0, TB, 64):
                    acc = jnp.broadcast_to(b_ref[:, cols], (64, LANES))
                    for k in range(K_SSM):
                        acc = acc + w_ref[k:k + 1, cols] * ext[pl.ds(r0 + H - (K_SSM - 1) + k, 64), cols]
                    o_ref[pl.ds(r0, 64), cols] = acc * _sigmoid(acc)
            _col_loop(n, blk)

        conv(xs_ref, xsp_ref, wx_ref, bx_ref, xso_ref, extx, D_SSM)
        conv(bc_ref, bcp_ref, wb_ref, bb_ref, bco_ref, extb, 512)
        dto_ref[...] = _softplus(dtr_ref[...] + dtb_ref[...])

    return pl.pallas_call(
        body, name="ssd_pre_fwd", grid=(T // TB,),
        in_specs=[_row(TB, 1024, C_XS), _prev(TB, H, 1024, C_XS), _row(TB, 512, C_BC), _prev(TB, H, 512, C_BC),
                  _row(TB, LANES), _full((K_SSM, 1024)), _full((K_SSM, 512)), _full((1, 1024)), _full((1, 512)),
                  _full((1, LANES))],
        out_specs=[_row(TB, 1024), _row(TB, 512), _row(TB, LANES)],
        out_shape=[S((T, 1024), F32), S((T, 512), F32), S((T, LANES), F32)],
        scratch_shapes=[pltpu.VMEM((H + TB, 1024), F32), pltpu.VMEM((H + TB, 512), F32)],
        compiler_params=_params(("parallel",)))(proj, proj, proj, proj, dt_raw, wx, wb, bx, bb, dt_bias)


def _ssd_conv_bwd(dproj, proj, d_c, w, b, n, col, name):
    T = proj.shape[0]
    nt = T // TB
    H = HALO_SSM
    R = TB + H

    def body(dproj_ref, t_ref, p_ref, n_ref, d_ref, dn_ref, w_ref, b_ref, o_ref, dw_ref, dbias_ref, ext, dp):
        i = pl.program_id(0)
        first, last = i == 0, i == nt - 1

        @pl.when(first)
        def _():
            dw_ref[...] = jnp.zeros_like(dw_ref)
            dbias_ref[...] = jnp.zeros_like(dbias_ref)

        def blk(cols):
            ext[0:H, cols] = jnp.where(first, 0.0, p_ref[:, cols])
            ext[H:H + TB, cols] = t_ref[:, cols]
            ext[H + TB:, cols] = n_ref[:, cols]
            pre = jnp.broadcast_to(b_ref[:, cols], (R, LANES))
            for k in range(K_SSM):
                pre = pre + w_ref[k:k + 1, cols] * ext[pl.ds(H - (K_SSM - 1) + k, R), cols]
            s = _sigmoid(pre)
            ds = _dsilu(pre, s)
            dp[0:TB, cols] = d_ref[:, cols] * ds[0:TB]
            dp[TB:, cols] = jnp.where(last, 0.0, dn_ref[:, cols] * ds[TB:])
            dpt = dp[0:TB, cols]
            dbias_ref[:, cols] += jnp.sum(dpt, axis=0, keepdims=True)
            acc = jnp.zeros((TB, LANES), F32)
            for k in range(K_SSM):
                dw_ref[k:k + 1, cols] += jnp.sum(dpt * ext[pl.ds(H - (K_SSM - 1) + k, TB), cols], axis=0, keepdims=True)
                acc = acc + w_ref[k:k + 1, cols] * dp[pl.ds(K_SSM - 1 - k, TB), cols]
            o_ref[:, cols] = acc.astype(BF16)
        _col_loop(n, blk)

    return pl.pallas_call(
        body, name=name, grid=(nt,),
        in_specs=[_ANY, _row(TB, n, col), _prev(TB, H, n, col), _next(TB, H, n, nt, col),
                  _row(TB, n), _next(TB, H, n, nt), _full((K_SSM, n)), _full((1, n))],
        out_specs=[_row(TB, n, col), _full((K_SSM, n)), _full((1, n))],
        out_shape=[S(dproj.shape, BF16), S((K_SSM, n), F32), S((1, n), F32)],
        input_output_aliases={0: 0},
        scratch_shapes=[pltpu.VMEM((H + TB + H, n), F32), pltpu.VMEM((R, n), F32)],
        compiler_params=_params(("arbitrary",)))(dproj, proj, proj, proj, d_c, d_c, w, b)


def _ssd_consts():
    ex = np.zeros((LANES, D_SSM), np.float32)
    for h in range(N_HEADS):
        ex[h, h * HEAD:(h + 1) * HEAD] = 1.0
    tri = np.tril(np.ones((CHUNK, CHUNK), np.float32))
    return jnp.asarray(ex), jnp.asarray(ex.T.copy()), jnp.asarray(tri), jnp.asarray(tri.T.copy())


def _ssd_common(xs, dt, alog_ref, ex_ref, tri_ref):
    lane = lax.broadcasted_iota(jnp.int32, (1, LANES), 1)
    a = jnp.where(lane < N_HEADS, -jnp.exp(alog_ref[...]), 0.0)
    A = _dot_sel_a(tri_ref[...], dt * a)
    ex = ex_ref[...]
    Aex = _dot_sel_b(A, ex)
    dtex = _dot_sel_b(dt, ex)
    expA = jnp.exp(Aex)
    dec = jnp.exp(Aex[CHUNK - 1:CHUNK, :] - Aex)
    cd = _dot_sel_a(ex, jnp.broadcast_to(jnp.exp(A.T[:, CHUNK - 1:CHUNK]), (LANES, LANES)), _TN)
    return a, A, dtex, expA, dec, cd


def _decay_mask():
    sub = lax.broadcasted_iota(jnp.int32, (CHUNK, CHUNK), 0)
    lane = lax.broadcasted_iota(jnp.int32, (CHUNK, CHUNK), 1)
    return sub, lane, sub >= lane


def _ssd_fwd(xs_c, bc_c, dt, proj, alog, dskip_row, norm_g):
    T = xs_c.shape[0]
    nc = T // CHUNK
    ex, _, tri, _ = _ssd_consts()

    def body(xs_ref, bc_ref, dt_ref, z_ref, alog_ref, dsk_ref, ng_ref, ex_ref, tri_ref,
             ys_ref, ypre_ref, hprev_ref, yst_ref, Hs, ybuf):
        @pl.when(pl.program_id(0) == 0)
        def _():
            Hs[...] = jnp.zeros_like(Hs)

        hprev_ref[0] = Hs[...]
        xs, dt = xs_ref[...], dt_ref[...]
        a, A, dtex, expA, dec, cd = _ssd_common(xs, dt, alog_ref, ex_ref, tri_ref)
        AT = A.T
        xdt = xs * dtex
        xdec = xdt * dec
        _, _, causal = _decay_mask()
        for g in range(2):
            gs = slice(g * 512, (g + 1) * 512)
            B = bc_ref[:, g * N_STATE:(g + 1) * N_STATE]
            C = bc_ref[:, 256 + g * N_STATE:256 + (g + 1) * N_STATE]
            cb = _dot(C, B, _NT)
            Hg = Hs[gs, :]
            yoff = _dot(C, Hg, _NT) * expA[:, gs]
            for j in range(8):
                h = g * 8 + j
                hs = slice(h * HEAD, (h + 1) * HEAD)
                L = jnp.exp(jnp.where(causal, A[:, h:h + 1] - AT[h:h + 1, :], -1e30))
                ybuf[:, hs] = _dot(cb * L, xdt[:, hs]) + yoff[:, j * HEAD:(j + 1) * HEAD]
            Hs[gs, :] = cd[gs, :] * Hg + _dot(xdec[:, gs], B, _TN)
        ypre = ybuf[...] + dsk_ref[...] * xs
        ypre_ref[...] = ypre
        z = z_ref[...]
        yz = ypre * (z * _sigmoid(z))
        for g in range(2):
            gs = slice(g * 512, (g + 1) * 512)
            v = yz[:, gs]
            r = lax.rsqrt(jnp.mean(v * v, axis=-1, keepdims=True) + RMS_EPS)
            ys_ref[:, gs] = (v * r * ng_ref[:, gs]).astype(BF16)
        yst_ref[...] = ys_ref[...].T

    return pl.pallas_call(
        body, name="ssd_fwd", grid=(nc,),
        in_specs=[_row(CHUNK, 1024), _row(CHUNK, 512), _row(CHUNK, LANES), _row(CHUNK, 1024, C_Z),
                  _full((1, LANES)), _full((1, 1024)), _full((1, 1024)), _full((LANES, 1024)), _full((CHUNK, CHUNK))],
        out_specs=[_row(CHUNK, 1024), _row(CHUNK, 1024), pl.BlockSpec((1, 1024, N_STATE), lambda c: (c, 0, 0)),
                   _colt(1024, CHUNK)],
        out_shape=[S((T, 2048), BF16), S((T, 1024), F32), S((nc, 1024, N_STATE), F32), S((2048, T), BF16)],
        scratch_shapes=[pltpu.VMEM((1024, N_STATE), F32), pltpu.VMEM((CHUNK, 1024), F32)],
        compiler_params=_params(("arbitrary",)))(xs_c, bc_c, dt, proj, alog, dskip_row, norm_g, ex, tri)


def _ssd_bwd(dproj, xs_c, bc_c, dt, dt_raw, dt_bias, proj, ypre, hprev, dmix, alog, dskip_row, norm_g, exchange=None):
    T = xs_c.shape[0]
    nc = T // CHUNK
    ex, ext, tri, triu = _ssd_consts()
    rev = lambda n, col=0: pl.BlockSpec((CHUNK, n), lambda c: (nc - 1 - c, col))
    xbufs, kinds = exchange if exchange is not None else ((), ())
    nx = len(xbufs)
    N_IN, N_OUT = 17, 8

    def body(*refs):
        (dproj_ref, xs_ref, bc_ref, dt_ref, dtr_ref, dtb_ref, z_ref, ypre_ref, hprev_ref, dys_ref, alog_ref, dsk_ref,
         ng_ref, ex_ref, ext_ref, tri_ref, triu_ref) = refs[:N_IN]
        (dxs_ref, dbc_ref, ddt_ref, dz_ref, dng_ref, ddsk_ref, dalog_ref,
         ddtb_ref) = refs[N_IN + nx:N_IN + nx + N_OUT]
        dHs, dxbuf, dskacc = refs[N_IN + N_OUT + 2 * nx:N_IN + N_OUT + 2 * nx + 3]
        c = pl.program_id(0)
        if nx:
            start, finish = _exchange_plan(refs[N_IN:N_IN + nx], refs[N_IN + nx + N_OUT:N_IN + N_OUT + 2 * nx], kinds,
                                           *refs[N_IN + N_OUT + 2 * nx + 3:])
            pl.when(c == 0)(start)

        @pl.when(c == 0)
        def _():
            dHs[...] = jnp.zeros_like(dHs)
            dng_ref[...] = jnp.zeros_like(dng_ref)
            dalog_ref[...] = jnp.zeros_like(dalog_ref)
            ddtb_ref[...] = jnp.zeros_like(ddtb_ref)
            dskacc[...] = jnp.zeros_like(dskacc)

        xs, dt, z, ypre, dys = xs_ref[...], dt_ref[...], z_ref[...], ypre_ref[...], dys_ref[...]
        sg = _sigmoid(z)
        sz = z * sg
        yz = ypre * sz
        dyz_parts = []
        for g in range(2):
            gs = slice(g * 512, (g + 1) * 512)
            v = yz[:, gs]
            r = lax.rsqrt(jnp.mean(v * v, axis=-1, keepdims=True) + RMS_EPS)
            vn = v * r
            dng_ref[:, gs] += jnp.sum(dys[:, gs] * vn, axis=0, keepdims=True)
            dvn = dys[:, gs] * ng_ref[:, gs]
            dyz_parts.append(r * (dvn - vn * jnp.mean(dvn * vn, axis=-1, keepdims=True)))
        dyz = jnp.concatenate(dyz_parts, axis=1)
        dy = dyz * sz
        dz_ref[...] = (dyz * ypre * _dsilu(z, sg)).astype(BF16)
        dskacc[...] += jnp.sum(dy * xs, axis=0, keepdims=True)

        a, A, dtex, expA, dec, cd = _ssd_common(xs, dt, alog_ref, ex_ref, tri_ref)
        AT = A.T
        xdt = xs * dtex
        xdec = xdt * dec
        dye = dy * expA
        H = hprev_ref[0]
        dHn = dHs[...]
        sub, lane, causal = _decay_mask()
        dAc = jnp.zeros((CHUNK, LANES), F32)
        Rm = jnp.zeros((CHUNK, LANES), F32)
        yoff_parts, q_parts = [], []
        for g in range(2):
            gs = slice(g * 512, (g + 1) * 512)
            B = bc_ref[:, g * N_STATE:(g + 1) * N_STATE]
            C = bc_ref[:, 256 + g * N_STATE:256 + (g + 1) * N_STATE]
            cb = _dot(C, B, _NT)
            Hg, dHg = H[gs, :], dHn[gs, :]
            Q = _dot(B, dHg, _NT)
            yoff_parts.append(_dot(C, Hg, _NT) * expA[:, gs])
            q_parts.append(Q)
            dcb = jnp.zeros((CHUNK, CHUNK), F32)
            for j in range(8):
                h = g * 8 + j
                hs = slice(h * HEAD, (h + 1) * HEAD)
                L = jnp.exp(jnp.where(causal, A[:, h:h + 1] - AT[h:h + 1, :], -1e30))
                M = cb * L
                G = _dot(dy[:, hs], xdt[:, hs], _NT)
                dxbuf[:, hs] = _dot(M, dy[:, hs], _TN)
                dcb = dcb + G * L
                E = G * M
                dAc = jnp.where(lane == h, jnp.sum(E, axis=1, keepdims=True), dAc)
                Rm = jnp.where(sub == h, jnp.sum(E, axis=0, keepdims=True), Rm)
            dbc_ref[:, g * N_STATE:(g + 1) * N_STATE] = _dot(dcb, C, _TN) + _dot(xdec[:, gs], dHg)
            dbc_ref[:, 256 + g * N_STATE:256 + (g + 1) * N_STATE] = _dot(dcb, B) + _dot(dye[:, gs], Hg)
            dHs[gs, :] = cd[gs, :] * dHg + _dot(dye[:, gs], C, _TN)
        yoff = jnp.concatenate(yoff_parts, axis=1)
        Qd = jnp.concatenate(q_parts, axis=1) * dec
        dxdt = dxbuf[...] + Qd
        extm = ext_ref[...]
        red_s = _dot_sel_b(xdt * Qd, extm)
        dA = dAc - Rm.T + _dot_sel_b(dy * yoff, extm) - red_s
        hd = jnp.sum(_dot_sel_b(H * dHn, extm, _TN), axis=0, keepdims=True)
        last_add = jnp.sum(red_s, axis=0, keepdims=True) + jnp.exp(A[CHUNK - 1:CHUNK, :]) * hd
        dA = dA + jnp.where(sub == CHUNK - 1, last_add, 0.0)
        dadt = _dot_sel_a(triu_ref[...], dA)
        ddtr = (dadt * a + _dot_sel_b(dxdt * xs, extm)) * _sigmoid(dtr_ref[...] + dtb_ref[...])
        ddt_ref[...] = ddtr.astype(BF16)
        ddtb_ref[...] += jnp.sum(ddtr, axis=0, keepdims=True)
        dalog_ref[...] += jnp.sum(dadt * dt, axis=0, keepdims=True) * a
        dxs_ref[...] = dxdt * dtex + dsk_ref[...] * dy

        @pl.when(c == nc - 1)
        def _():
            ddsk_ref[...] = _dot_sel_b(jnp.broadcast_to(dskacc[...], (8, 1024)), extm)[0:1, :]

        if nx:
            pl.when(c == nc - 1)(finish)

    return pl.pallas_call(
        body, name="ssd_bwd", grid=(nc,),
        in_specs=[_ANY, rev(1024), rev(512), rev(LANES), rev(LANES), _full((1, LANES)), rev(1024, C_Z), rev(1024),
                  pl.BlockSpec((1, 1024, N_STATE), lambda c: (nc - 1 - c, 0, 0)), rev(1024, 0),
                  _full((1, LANES)), _full((1, 1024)), _full((1, 1024)),
                  _full((LANES, 1024)), _full((1024, LANES)), _full((CHUNK, CHUNK)), _full((CHUNK, CHUNK))] + [_ANY] * nx,
        out_specs=[rev(1024), rev(512), rev(LANES), rev(1024, C_Z), _full((1, 1024)), _full((1, LANES)),
                   _full((1, LANES)), _full((1, LANES))] + [_ANY] * nx,
        out_shape=[S((T, 1024), F32), S((T, 512), F32), S((T, LANES), BF16), S(dproj.shape, BF16),
                   S((1, 1024), F32), S((1, LANES), F32), S((1, LANES), F32), S((1, LANES), F32)]
        + _exchange_shapes(xbufs, kinds),
        input_output_aliases={0: 3},
        scratch_shapes=[pltpu.VMEM((1024, N_STATE), F32), pltpu.VMEM((CHUNK, 1024), F32), pltpu.VMEM((1, 1024), F32)]
        + (_exchange_sems(nx) if nx else []),
        compiler_params=_params(("arbitrary",)))(
            dproj, xs_c, bc_c, dt, dt_raw, dt_bias, proj, ypre, hprev, dmix, alog, dskip_row, norm_g, ex, ext, tri, triu,
            *xbufs)


def _shifted_copies(ext, ext8):
    n = ext8.shape[1]
    for r in range(8):
        ext8[r] = ext[pl.ds(r, n), :]


def _shifted(ext8, off, rows):
    return ext8[off % 8, pl.ds(off - off % 8, rows), :]


def _conf_fwd(mix, mixt, proj, w, cb, lg, lb, ba, bb):
    T = proj.shape[0]
    H = HALO_CONF

    def body(mix_ref, mixt_ref, ga_ref, gap_ref, gb_ref, gbp_ref, cg_ref, w_ref, cb_ref, lg_ref, lb_ref, ba_ref,
             bb_ref, u1_ref, yc_ref, yct_ref, ext, ext8):
        first = pl.program_id(0) == 0
        ext[H + TB:, :] = jnp.zeros((8, LANES), F32)

        def blk(cols):
            up = (gap_ref[:, cols] + ba_ref[:, cols]) * _sigmoid(gbp_ref[:, cols] + bb_ref[:, cols])
            ext[0:H, :] = jnp.where(first, 0.0, up)
            ext[H:H + TB, :] = (ga_ref[:, cols] + ba_ref[:, cols]) * _sigmoid(gb_ref[:, cols] + bb_ref[:, cols])
            _shifted_copies(ext, ext8)
            for r0 in range(0, TB, 64):
                acc = jnp.broadcast_to(cb_ref[:, cols], (64, LANES))
                for k in range(K_CONF):
                    acc = acc + w_ref[k:k + 1, cols] * _shifted(ext8, r0 + H - (K_CONF - 1) + k, 64)
                u1_ref[pl.ds(r0, 64), cols] = acc
        _col_loop(D_CONF, blk)

        def rows(rs):
            xh, _ = _ln_stats(u1_ref[rs, :])
            u2 = xh * lg_ref[...] + lb_ref[...]
            cg = cg_ref[rs, :]
            yc_ref[rs, :] = (u2 * _sigmoid(u2) * cg * _sigmoid(cg)).astype(BF16)
        _row_loop(TB, rows)
        yct_ref[...] = yc_ref[...].T

    return pl.pallas_call(
        body, name="conf_fwd", grid=(T // TB,),
        in_specs=[_ANY, _ANY, _row(TB, 1024, C_GLUA), _prev(TB, H, 1024, C_GLUA), _row(TB, 1024, C_GLUB),
                  _prev(TB, H, 1024, C_GLUB), _row(TB, 1024, C_CG), _full((K_CONF, 1024))] + [_full((1, 1024))] * 5,
        out_specs=[_row(TB, 1024), _row(TB, 1024, 1), _colt(1024, TB, 1)],
        out_shape=[S((T, 1024), F32), S((T, 2048), BF16), S((2048, T), BF16)],
        input_output_aliases={0: 1, 1: 2},
        scratch_shapes=[pltpu.VMEM((H + TB + 8, LANES), F32), pltpu.VMEM((8, H + TB, LANES), F32)],
        compiler_params=_params(("parallel",)))(mix, mixt, proj, proj, proj, proj, proj, w, cb, lg, lb, ba, bb)


def _conf_bwd1(dmix, u1, proj, lg, lb):
    T = u1.shape[0]

    def body(dy_ref, u1_ref, cg_ref, lg_ref, lb_ref, du1_ref, dcg_ref, dg_ref, db_ref):
        @pl.when(pl.program_id(0) == 0)
        def _():
            dg_ref[...] = jnp.zeros_like(dg_ref)
            db_ref[...] = jnp.zeros_like(db_ref)

        def rows(rs):
            xh, r = _ln_stats(u1_ref[rs, :])
            u2 = xh * lg_ref[...] + lb_ref[...]
            s2 = _sigmoid(u2)
            cg = cg_ref[rs, :]
            sc = _sigmoid(cg)
            dy = dy_ref[rs, :]
            dcg_ref[rs, :] = (dy * u2 * s2 * _dsilu(cg, sc)).astype(BF16)
            dv, dg, db = _ln_bwd(dy * cg * sc * _dsilu(u2, s2), xh, r, lg_ref[...])
            dg_ref[...] += dg
            db_ref[...] += db
            du1_ref[rs, :] = dv
        _row_loop(TB, rows)

    return pl.pallas_call(
        body, name="conf_bwd1", grid=(T // TB,),
        in_specs=[_row(TB, 1024, 1), _row(TB, 1024), _row(TB, 1024, C_CG), _full((1, 1024)), _full((1, 1024))],
        out_specs=[_row(TB, 1024), _row(TB, 1024, C_CG), _full((1, 1024)), _full((1, 1024))],
        out_shape=[S((T, 1024), F32), S((T, N_MAIN), BF16), S((1, 1024), F32), S((1, 1024), F32)],
        compiler_params=_params(("arbitrary",)))(dmix, u1, proj, lg, lb)


def _conf_bwd2(dproj, proj, du1, w, ba, bb):
    T = du1.shape[0]
    nt = T // TB
    H = HALO_CONF

    def body(dproj_ref, ga_ref, gap_ref, gb_ref, gbp_ref, du_ref, dun_ref, w_ref, ba_ref, bb_ref,
             dg_ref, dw_ref, dcb_ref, dba_ref, dbb_ref, ext, dext, ext8, dext8, dwacc):
        i = pl.program_id(0)
        first, last = i == 0, i == nt - 1

        @pl.when(first)
        def _():
            for r in (dcb_ref, dba_ref, dbb_ref, dwacc):
                r[...] = jnp.zeros_like(r)

        ext[H + TB:, :] = jnp.zeros((8, LANES), F32)
        dext[H + TB:, :] = jnp.zeros((8, LANES), F32)

        def blk(cols):
            cols_b = pl.ds(pl.multiple_of(cols.start + D_CONF, LANES), LANES)
            up = (gap_ref[:, cols] + ba_ref[:, cols]) * _sigmoid(gbp_ref[:, cols] + bb_ref[:, cols])
            ext[0:H, :] = jnp.where(first, 0.0, up)
            a = ga_ref[:, cols] + ba_ref[:, cols]
            sb = _sigmoid(gb_ref[:, cols] + bb_ref[:, cols])
            ext[H:H + TB, :] = a * sb
            du = du_ref[:, cols]
            dext[0:TB, :] = du
            dext[TB:TB + H, :] = jnp.where(last, 0.0, dun_ref[:, cols])
            _shifted_copies(ext, ext8)
            _shifted_copies(dext, dext8)
            dcb_ref[:, cols] += jnp.sum(du, axis=0, keepdims=True)
            for r0 in range(0, TB, 64):
                dur = du_ref[pl.ds(r0, 64), cols]
                acc = jnp.zeros((64, LANES), F32)
                for k in range(K_CONF):
                    prod = dur * _shifted(ext8, r0 + H - (K_CONF - 1) + k, 64)
                    dwacc[k * 8:(k + 1) * 8, cols] += prod.reshape(8, 8, LANES).sum(axis=0)
                    acc = acc + w_ref[k:k + 1, cols] * _shifted(dext8, r0 + K_CONF - 1 - k, 64)
                ar, sr = a[r0:r0 + 64], sb[r0:r0 + 64]
                da = acc * sr
                dbv = acc * ar * sr * (1.0 - sr)
                dg_ref[pl.ds(r0, 64), cols] = da.astype(BF16)
                dg_ref[pl.ds(r0, 64), cols_b] = dbv.astype(BF16)
                dba_ref[:, cols] += jnp.sum(da, axis=0, keepdims=True)
                dbb_ref[:, cols] += jnp.sum(dbv, axis=0, keepdims=True)
        _col_loop(D_CONF, blk)

        @pl.when(last)
        def _():
            dw_ref[...] = jnp.sum(dwacc[...].reshape(K_CONF, 8, D_CONF), axis=1)

    return pl.pallas_call(
        body, name="conf_bwd2", grid=(nt,),
        in_specs=[_ANY, _row(TB, 1024, C_GLUA), _prev(TB, H, 1024, C_GLUA), _row(TB, 1024, C_GLUB),
                  _prev(TB, H, 1024, C_GLUB), _row(TB, 1024), _next(TB, H, 1024, nt), _full((K_CONF, 1024)),
                  _full((1, 1024)), _full((1, 1024))],
        out_specs=[_row(TB, 2048), _full((K_CONF, 1024)), _full((1, 1024)), _full((1, 1024)), _full((1, 1024))],
        out_shape=[S(dproj.shape, BF16), S((K_CONF, 1024), F32)] + [S((1, 1024), F32)] * 3,
        input_output_aliases={0: 0},
        scratch_shapes=[pltpu.VMEM((H + TB + 8, LANES), F32), pltpu.VMEM((TB + H + 8, LANES), F32),
                        pltpu.VMEM((8, H + TB, LANES), F32), pltpu.VMEM((8, TB + H, LANES), F32),
                        pltpu.VMEM((K_CONF * 8, D_CONF), F32)],
        compiler_params=_params(("arbitrary",)))(dproj, proj, proj, proj, proj, du1, du1, w, ba, bb)


def _mesh_pos():
    x, y, c = lax.axis_index("x"), lax.axis_index("y"), lax.axis_index("c")
    return x, y, c, 4 * x + 2 * y + c


def _peer(x, y, c, k):
    return (x ^ ((k >> 2) & 1), y ^ ((k >> 1) & 1), c ^ (k & 1))


def _exchange_copies(ins, outs, kinds, send, recv, loc):
    nb = len(ins)
    x, y, c, me = _mesh_pos()
    src = lambda b, d: ins[b].at[d] if kinds[b] == "blocks" else ins[b]
    copies = [pltpu.make_async_copy(src(b, me), outs[b].at[me], loc.at[b]) for b in range(nb)]
    for k in range(1, N_DEV):
        px, py, pc = _peer(x, y, c, k)
        for b in range(nb):
            s = (k - 1) * nb + b
            copies.append(pltpu.make_async_remote_copy(
                src_ref=src(b, 4 * px + 2 * py + pc), dst_ref=outs[b].at[me], send_sem=send.at[s], recv_sem=recv.at[s],
                device_id=(px, py, pc), device_id_type=pl.DeviceIdType.MESH))
    return copies


def _exchange_shapes(bufs, kinds):
    return [S(b.shape if kd == "blocks" else (N_DEV,) + b.shape, b.dtype) for b, kd in zip(bufs, kinds)]


def _exchange_sems(nb):
    n = (N_DEV - 1) * nb
    return [pltpu.SemaphoreType.DMA((n,)), pltpu.SemaphoreType.DMA((n,)), pltpu.SemaphoreType.DMA((nb,))]


def _two_level_gather(ins, outs, send, recv, loc):
    nb = len(ins)
    x, y, c, me = _mesh_pos()
    here, sibling = (x, y, c), (x, y, 1 - c)
    chips = [(1 - x, y), (x, 1 - y), (1 - x, 1 - y)]

    def copy(slot, b, block, to, src=None):
        d = 4 * block[0] + 2 * block[1] + block[2]
        return pltpu.make_async_remote_copy(
            src_ref=outs[b].at[d] if src is None else src, dst_ref=outs[b].at[d],
            send_sem=send.at[slot * nb + b], recv_sem=recv.at[slot * nb + b],
            device_id=to, device_id_type=pl.DeviceIdType.MESH)

    mine = [pltpu.make_async_copy(ins[b], outs[b].at[me], loc.at[b]) for b in range(nb)]
    first = [copy(0, b, here, sibling, src=ins[b]) for b in range(nb)]
    first += [copy(1 + j, b, here, (*chip, c), src=ins[b]) for j, chip in enumerate(chips) for b in range(nb)]

    def start():
        for cp in mine + first:
            cp.start()

    def finish():
        passed = []
        for j, chip in enumerate(chips):
            for b in range(nb):
                copy(1 + j, b, (*chip, c), here).wait_recv()
            onward = [copy(4 + j, b, (*chip, c), sibling) for b in range(nb)]
            for cp in onward:
                cp.start()
            passed += onward
        for b in range(nb):
            copy(0, b, sibling, here).wait_recv()
        for j, chip in enumerate(chips):
            for b in range(nb):
                copy(4 + j, b, (*chip, 1 - c), here).wait_recv()
        for cp in first + passed:
            cp.wait_send()
        for cp in mine:
            cp.wait()

    return start, finish


def _exchange_plan(ins, outs, kinds, send, recv, loc):
    if all(kd == "gather" for kd in kinds):
        return _two_level_gather(ins, outs, send, recv, loc)
    copies = _exchange_copies(ins, outs, kinds, send, recv, loc)

    def start():
        for cp in copies:
            cp.start()

    def finish():
        for cp in copies:
            cp.wait()

    return start, finish


def _exchange(bufs, kinds, name):
    nb = len(bufs)

    def body(*refs):
        start, finish = _exchange_plan(refs[:nb], refs[nb:2 * nb], kinds, *refs[2 * nb:])
        start()
        finish()

    return pl.pallas_call(
        body, name=name, in_specs=[_ANY] * nb, out_specs=[_ANY] * nb,
        out_shape=_exchange_shapes(bufs, kinds), scratch_shapes=_exchange_sems(nb))(*bufs)


def _sum_parts(p_ref):
    acc = p_ref[0].astype(F32)
    for d in range(1, N_DEV):
        acc = acc + p_ref[d].astype(F32)
    return acc


def _adamw_math(g, w, m, v):
    m = ADAM_B1 * m + (1.0 - ADAM_B1) * g
    v = ADAM_B2 * v + (1.0 - ADAM_B2) * (g * g)
    m_hat = m / (1.0 - ADAM_B1 ** ADAM_STEP)
    v_hat = v / (1.0 - ADAM_B2 ** ADAM_STEP)
    return -ADAM_LR * (m_hat / (jnp.sqrt(v_hat) + ADAM_EPS) + ADAM_WD * w), m, v


def _sum8_adamw(parts, w, m, v, name):
    _, R, C = parts.shape
    tb = 256 if R % 256 == 0 else R

    def body(p_ref, w_ref, m_ref, v_ref, g_ref, d_ref, mo_ref, vo_ref):
        g = _sum_parts(p_ref)
        g_ref[...] = g
        d_ref[...], mo_ref[...], vo_ref[...] = _adamw_math(g, w_ref[...], m_ref[...], v_ref[...])

    return pl.pallas_call(
        body, name=name, grid=(R // tb,),
        in_specs=[pl.BlockSpec((N_DEV, tb, C), lambda i: (0, i, 0))] + [_row(tb, C)] * 3, out_specs=[_row(tb, C)] * 4,
        out_shape=[S((R, C), F32)] * 4, compiler_params=_params(("parallel",)))(parts, w, m, v)


SMALL_LAYOUT = (
    ("ln_emb_g", 0, 1024), ("ln_emb_b", 0, 1024), ("ssm_conv_b", 0, 1024), ("ssm_conv_b", 1024, 512),
    ("dt_bias", 0, N_HEADS), ("a_log", 0, N_HEADS), ("d_skip", 0, N_HEADS), ("ssm_norm_g", 0, 1024),
    ("b_glu", 0, 1024), ("b_glu", 1024, 1024), ("conf_conv_b", 0, 1024), ("conf_ln_g", 0, 1024),
    ("conf_ln_b", 0, 1024), ("b_out", 0, 1024), ("ln1_g", 0, 1024), ("ln1_b", 0, 1024), ("ln2_g", 0, 1024),
    ("ln2_b", 0, 1024))
SMALL_ROWS = 24
SMALL = tuple(dict.fromkeys(n for n, _, _ in SMALL_LAYOUT))


LOSS_ROW = len(SMALL_LAYOUT)


def _pack_small(rows, loss):
    def body(*refs):
        o_ref = refs[-1]
        o_ref[...] = jnp.zeros_like(o_ref)
        for r, ref in enumerate(refs[:-2]):
            o_ref[r:r + 1, 0:ref.shape[1]] = ref[...]
        o_ref[LOSS_ROW:LOSS_ROW + 1, 0:LANES] = refs[-2][0:1, :]

    return pl.pallas_call(body, name="pack_small", out_shape=S((SMALL_ROWS, 1024), F32))(*rows, loss)


def _small_update(parts, w, m, v):
    def body(*refs):
        p_ref = refs[0]
        ins = {n: refs[1 + 3 * i:4 + 3 * i] for i, n in enumerate(SMALL)}
        o0 = 1 + 3 * len(SMALL)
        outs = {n: refs[o0 + 4 * i:o0 + 4 * i + 4] for i, n in enumerate(SMALL)}
        gsum = refs[-1]
        gsum[...] = _sum_parts(p_ref)
        refs[-2][...] = gsum[LOSS_ROW:LOSS_ROW + 1, 0:LANES]
        for r, (n, off, wd) in enumerate(SMALL_LAYOUT):
            cs = slice(off, off + wd)
            g = gsum[r:r + 1, 0:wd]
            w_ref, m_ref, v_ref = ins[n]
            g_ref, d_ref, mo_ref, vo_ref = outs[n]
            g_ref[:, cs] = g
            d_ref[:, cs], mo_ref[:, cs], vo_ref[:, cs] = _adamw_math(g, w_ref[:, cs], m_ref[:, cs], v_ref[:, cs])

    args = [parts] + [a for n in SMALL for a in (w[n], m[n], v[n])]
    res = pl.pallas_call(
        body, name="small_update",
        out_shape=[S(w[n].shape, F32) for n in SMALL for _ in range(4)] + [S((1, LANES), F32)],
        scratch_shapes=[pltpu.VMEM((SMALL_ROWS, 1024), F32)])(*args)
    return tuple({n: res[4 * i + j] for i, n in enumerate(SMALL)} for j in range(4)) + (res[-1],)


EARLY = ("w_in", "ssm_conv_w", "conf_conv_w")
LATE = ("w_out", "w_ple_gate", "w_ple_proj")


def _local_step(x, p, tgt, W, shards=None):
    r1 = lambda v: v.reshape(1, -1).astype(F32)
    pad_l = lambda v: jnp.pad(r1(v), ((0, 0), (0, LANES - v.size)))
    late = None if shards is None else [shards[n] for n in LATE]
    if shards is None:
        h0, h0b, h0bt = _ln_emb_fwd(x, r1(W["ln_emb_g"]), r1(W["ln_emb_b"]))
    else:
        h0, h0b, h0bt, *gathered = _ln_emb_fwd(x, r1(W["ln_emb_g"]), r1(W["ln_emb_b"]),
                                               exchange=([shards[n] for n in EARLY], ("gather",) * len(EARLY)))
        W = dict(W, **{n: a if n == "w_in" else _unstack_shards(a, BY_COLS[n]) for n, a in zip(EARLY, gathered)})
    w_main, w_dt = _w_in_to_main(W["w_in"])
    scw, scb = W["ssm_conv_w"], r1(W["ssm_conv_b"])
    wx, wb, bx, bb = scw[:, :1024], scw[:, 1024:], scb[:, :1024], scb[:, 1024:]
    dt_bias, alog = pad_l(W["dt_bias"]), pad_l(W["a_log"])
    dskip_row = jnp.repeat(W["d_skip"].reshape(-1), HEAD).reshape(1, -1)
    norm_g = r1(W["ssm_norm_g"])
    bglu = r1(W["b_glu"])
    ba, bbg = bglu[:, :1024], bglu[:, 1024:]
    ccw, ccb, clg, clb = W["conf_conv_w"], r1(W["conf_conv_b"]), r1(W["conf_ln_g"]), r1(W["conf_ln_b"])

    if late is None:
        proj = _mm(h0b, w_main, "nn", "in_proj")
    else:
        proj, *gathered = _mm(h0b, w_main, "nn", "in_proj", exchange=(late, ("gather",) * len(LATE)))
        W = dict(W, **{n: _unstack_shards(a, BY_COLS[n]) for n, a in zip(LATE, gathered)})
    dt_raw = _mm(h0b, w_dt, "nn", "in_proj_dt")
    xs_c, bc_c, dt = _ssd_pre_fwd(proj, dt_raw, wx, wb, bx, bb, dt_bias)
    mix, ypre, hprev, mixt = _ssd_fwd(xs_c, bc_c, dt, proj, alog, dskip_row, norm_g)
    u1, mix, mixt = _conf_fwd(mix, mixt, proj, ccw, ccb, clg, clb, ba, bbg)
    out, h1, h1b, h1bt = _out_proj_post1(mix, W["w_out"], h0, r1(W["b_out"]), r1(W["ln1_g"]), r1(W["ln1_b"]))
    pb = p.astype(BF16)
    dh1a, dgp, dple, loss, dln2g, dln2b = _ple_post2(h1b, W["w_ple_gate"], pb, W["w_ple_proj"], h1, tgt,
                                                      r1(W["ln2_g"]), r1(W["ln2_b"]))

    g = {}
    g["w_ple_proj"] = _mm(pb.T, dple, "nn", "d_ple_proj", out_dtype=BF16)
    g["w_ple_gate"] = _mm(h1bt, dgp, "nn", "d_ple_gate", out_dtype=BF16)
    dout, dh0a, dln1g, dln1b, dbout = _d_h1_post1_bwd(dgp, W["w_ple_gate"], dh1a, h0, out, r1(W["b_out"]),
                                                      r1(W["ln1_g"]))
    g["w_out"] = _mm(mixt, dout, "nn", "d_w_out", out_dtype=BF16)
    dmix = _mm(dout, W["w_out"], "nt", "d_mix")
    du1, dproj, dclg, dclb = _conf_bwd1(dmix, u1, proj, clg, clb)
    dproj, g["conf_conv_w"], dccb, dba, dbb = _conf_bwd2(dproj, proj, du1, ccw, ba, bbg)
    stack = lambda names: [_stack_shards(g[n], BY_COLS[n]) for n in names]
    dxs_c, dbc_c, ddtr, dproj, dng, ddsk, dalog, ddtb, *recv_a = _ssd_bwd(
        dproj, xs_c, bc_c, dt, dt_raw, dt_bias, proj, ypre, hprev, dmix, alog, dskip_row, norm_g,
        exchange=None if late is None else (stack(LATE), ("blocks",) * len(LATE)))
    dproj, dwx, dbx = _ssd_conv_bwd(dproj, proj, dxs_c, wx, bx, 1024, C_XS, "ssd_conv_bwd_x")
    dproj, dwb, dbb2 = _ssd_conv_bwd(dproj, proj, dbc_c, wb, bb, 512, C_BC, "ssd_conv_bwd_bc")
    g["ssm_conv_w"] = jnp.concatenate([dwx, dwb], axis=1)
    g["w_in"] = _w_in_blocks(_mm(h0bt, dproj, "nn", "d_w_in", out_dtype=BF16),
                             _mm(h0bt, ddtr, "nn", "d_w_dt", out_dtype=BF16))
    last_args = (dproj, w_main, ddtr, w_dt, dh0a, x, r1(W["ln_emb_g"]))
    if late is None:
        grad_x, dlng, dlnb = _d_h0_ln_bwd(*last_args)
    else:
        last = ("ssm_conv_w", "conf_conv_w")
        grad_x, dlng, dlnb, *recv_b = _d_h0_ln_bwd(*last_args, exchange=([g["w_in"]] + stack(last), ("blocks",) * 3))
        g["recv"] = dict(zip(LATE + ("w_in",) + last, recv_a + recv_b))
    g["rows"] = [dlng, dlnb, dbx, dbb2, ddtb, dalog, ddsk, dng, dba, dbb, dccb, dclg, dclb, dbout, dln1g, dln1b,
                 dln2g, dln2b]
    return loss, grad_x, g


W_IN_SEGMENTS = ((0, 2048, 2048), (2048, 5120, 512), (2560, None, N_HEADS), (2576, 0, 2048), (4624, 4096, 1024))


def _w_in_to_main(shards):
    def pieces(p0, width):
        out, p = [], p0
        while p < p0 + width:
            d = p // COLS_PER_DEV
            hi = min(p0 + width, (d + 1) * COLS_PER_DEV)
            out.append(shards[d][:, p - d * COLS_PER_DEV:hi - d * COLS_PER_DEV])
            p = hi
        return out
    main = [s for s in sorted(W_IN_SEGMENTS, key=lambda s: -1 if s[1] is None else s[1]) if s[1] is not None]
    w_main = jnp.concatenate([q for p0, _, width in main for q in pieces(p0, width)], axis=1)
    w_dt = jnp.concatenate(pieces(2560, N_HEADS), axis=1)
    return w_main, jnp.pad(w_dt, ((0, 0), (0, LANES - N_HEADS)))


def _w_in_blocks(dw_main, dw_dt):
    blocks = []
    for d in range(N_DEV):
        lo_d, hi_d = d * COLS_PER_DEV, (d + 1) * COLS_PER_DEV
        parts = []
        for p0, m0, width in W_IN_SEGMENTS:
            lo, hi = max(lo_d, p0), min(hi_d, p0 + width)
            if lo < hi:
                parts.append(dw_dt[:, lo - p0:hi - p0] if m0 is None else dw_main[:, m0 + lo - p0:m0 + hi - p0])
        blocks.append(jnp.concatenate(parts, axis=1))
    return jnp.stack(blocks)


WEIGHTS = ['ln_emb_g', 'ln_emb_b', 'w_in', 'ssm_conv_w', 'ssm_conv_b', 'dt_bias', 'a_log', 'd_skip', 'ssm_norm_g',
           'b_glu', 'conf_conv_w', 'conf_conv_b', 'conf_ln_g', 'conf_ln_b', 'w_out', 'b_out', 'ln1_g', 'ln1_b',
           'w_ple_gate', 'w_ple_proj', 'ln2_g', 'ln2_b']
SHARDED = (("w_in", True), ("w_out", False), ("w_ple_gate", False), ("w_ple_proj", True), ("ssm_conv_w", True),
           ("conf_conv_w", True))
BY_COLS = dict(SHARDED)


def _stack_shards(a, by_cols):
    if by_cols:
        return a.reshape(a.shape[0], N_DEV, a.shape[1] // N_DEV).transpose(1, 0, 2)
    return a.reshape(N_DEV, a.shape[0] // N_DEV, a.shape[1])


def _unstack_shards(a, by_cols):
    if by_cols:
        return a.transpose(1, 0, 2).reshape(a.shape[1], N_DEV * a.shape[2])
    return a.reshape(N_DEV * a.shape[1], a.shape[2])


def kernel(x, p, ln_emb_g, ln_emb_b, w_in, ssm_conv_w, ssm_conv_b, dt_bias, a_log, d_skip, ssm_norm_g, b_glu, conf_conv_w, conf_conv_b, conf_ln_g, conf_ln_b, w_out, b_out, ln1_g, ln1_b, w_ple_gate, w_ple_proj, ln2_g, ln2_b, loss_target, m_ln_emb_g, m_ln_emb_b, m_w_in, m_ssm_conv_w, m_ssm_conv_b, m_dt_bias, m_a_log, m_d_skip, m_ssm_norm_g, m_b_glu, m_conf_conv_w, m_conf_conv_b, m_conf_ln_g, m_conf_ln_b, m_w_out, m_b_out, m_ln1_g, m_ln1_b, m_w_ple_gate, m_w_ple_proj, m_ln2_g, m_ln2_b, v_ln_emb_g, v_ln_emb_b, v_w_in, v_ssm_conv_w, v_ssm_conv_b, v_dt_bias, v_a_log, v_d_skip, v_ssm_norm_g, v_b_glu, v_conf_conv_w, v_conf_conv_b, v_conf_ln_g, v_conf_ln_b, v_w_out, v_b_out, v_ln1_g, v_ln1_b, v_w_ple_gate, v_w_ple_proj, v_ln2_g, v_ln2_b):
    loc = dict(locals())
    w = {n: loc[n] for n in WEIGHTS}
    m = {n: loc["m_" + n] for n in WEIGHTS}
    v = {n: loc["v_" + n] for n in WEIGHTS}
    sharded = [n for n, _ in SHARDED]

    shards = {n: w[n][0].astype(BF16) if n.startswith("w_") else w[n][0] for n in sharded}
    W = {n: w[n].reshape(-1) for n in SMALL}
    loss, grad_x, g = _local_step(x[0], p[0, 0], loss_target[0], W, shards=shards)
    (recv_small,) = _exchange([_pack_small(g["rows"], loss)], ("all",), "small_exchange")

    grads, delta, new_m, new_v = {}, {}, {}, {}
    for n in sharded:
        res = _sum8_adamw(g["recv"][n], w[n][0], m[n][0], v[n][0], "adamw_" + n)
        grads[n], delta[n], new_m[n], new_v[n] = (r[None] for r in res)
    two_d = lambda d: {n: d[n].reshape(1, -1) for n in SMALL}
    *small, loss = _small_update(recv_small, two_d(w), two_d(m), two_d(v))
    for dst, res in zip((grads, delta, new_m, new_v), small):
        for n in SMALL:
            dst[n] = res[n].reshape(w[n].shape)
    return (loss[0, 0], grad_x[None], *[grads[n] for n in WEIGHTS], *[delta[n] for n in WEIGHTS],
            *[new_m[n] for n in WEIGHTS], *[new_v[n] for n in WEIGHTS])
```

```python
import functools

import numpy as np
import jax
import jax.numpy as jnp
from jax import lax
from jax.experimental import pallas as pl
from jax.experimental.pallas import tpu as pltpu

F32, BF16 = jnp.float32, jnp.bfloat16
S = jax.ShapeDtypeStruct

N_DEV = 8
D = 1024
D_PLE = 256
D_SSM = 1024
D_CONF = 1024
N_HEADS = 16
HEAD = 64
N_STATE = 128
CHUNK = 128
K_SSM = 4
K_CONF = 31
D_IN = 5648
COLS_PER_DEV = D_IN // N_DEV
LN_EPS = 1e-5
RMS_EPS = 1e-5
ALPHA = 2.0 ** 0.25
LANES = 128
TB = 256
RG = 32
ROW_UNROLL = 4
HALO_SSM = 8
HALO_CONF = 32
VMEM_LIMIT = 56 * 1024 * 1024

ADAM_LR, ADAM_B1, ADAM_B2, ADAM_EPS, ADAM_WD, ADAM_STEP = 0.001, 0.9, 0.999, 1e-08, 0.01, 10

C_GLUA, C_GLUB, C_XS, C_Z, C_CG = 0, 1, 2, 3, 4
C_BC = 10
N_MAIN = 5632


def _params(sem, vmem=VMEM_LIMIT):
    return pltpu.CompilerParams(dimension_semantics=sem, vmem_limit_bytes=vmem)


def _row(tb, n, col=0):
    return pl.BlockSpec((tb, n), lambda i: (i, col))


def _colt(n, tb, row=0):
    return pl.BlockSpec((n, tb), lambda i: (row, i))


def _full(shape):
    return pl.BlockSpec(shape, lambda i: (0,) * len(shape))


_ANY = pl.BlockSpec(memory_space=pl.ANY)


def _prev(tb, halo, n, col=0):
    r = tb // halo
    return pl.BlockSpec((halo, n), lambda i: (jnp.maximum(i * r - 1, 0), col))


def _next(tb, halo, n, nt, col=0):
    r = tb // halo
    return pl.BlockSpec((halo, n), lambda i: (jnp.minimum((i + 1) * r, nt * r - 1), col))


def _row_loop(tb, fn):
    def it(r, c):
        fn(pl.ds(pl.multiple_of(r * RG, RG), RG))
        return c
    lax.fori_loop(0, tb // RG, it, 0, unroll=ROW_UNROLL)


def _col_loop(n, fn):
    def it(j, c):
        fn(pl.ds(pl.multiple_of(j * LANES, LANES), LANES))
        return c
    lax.fori_loop(0, n // LANES, it, 0)


def _sigmoid(x):
    return 1.0 / (1.0 + jnp.exp(-x))


def _dsilu(x, s):
    return s * (1.0 + x * (1.0 - s))


def _ln_stats(v):
    mu = jnp.mean(v, axis=-1, keepdims=True)
    c = v - mu
    r = lax.rsqrt(jnp.mean(c * c, axis=-1, keepdims=True) + LN_EPS)
    return c * r, r


def _ln_bwd(dy, xhat, r, g):
    dxh = dy * g
    dv = r * (dxh - jnp.mean(dxh, axis=-1, keepdims=True) - xhat * jnp.mean(dxh * xhat, axis=-1, keepdims=True))
    return dv, jnp.sum(dy * xhat, axis=0, keepdims=True), jnp.sum(dy, axis=0, keepdims=True)


def _dot(a, b, dims=((1,), (0,))):
    return lax.dot_general(a.astype(BF16), b.astype(BF16), (dims, ((), ())), preferred_element_type=F32)


_NT = ((1,), (1,))
_TN = ((0,), (0,))


def _split3(x):
    hi = x.astype(BF16)
    r = x - hi.astype(F32)
    mid = r.astype(BF16)
    return hi, mid, (r - mid.astype(F32)).astype(BF16)


def _dot_sel_b(a, b, dims=((1,), (0,))):
    hi, mid, lo = _split3(a)
    return (_dot(lo, b, dims) + _dot(mid, b, dims)) + _dot(hi, b, dims)


def _dot_sel_a(a, b, dims=((1,), (0,))):
    hi, mid, lo = _split3(b)
    return (_dot(a, lo, dims) + _dot(a, mid, dims)) + _dot(a, hi, dims)


def _mm(a, b, mode, name, out_dtype=F32, add=None, tm=1024, tn=None, tk=1024, exchange=None):
    if mode == "nn":
        (M, K), N = a.shape, b.shape[1]
    elif mode == "tn":
        (K, M), N = a.shape, b.shape[1]
    else:
        (M, K), N = a.shape, b.shape[0]
    if tn is None:
        tn = next(t for t in (1024, 1408, 512, 256, LANES) if N % t == 0)
    tm, tn, tk = min(tm, M), min(tn, N), min(tk, K)
    assert M % tm == 0 and N % tn == 0 and K % tk == 0, (name, M, N, K)
    grid = (M // tm, N // tn, K // tk)
    nk = grid[2]
    dims = {"nn": ((1,), (0,)), "tn": _TN, "nt": _NT}[mode]
    n_in = 2 + (add is not None)
    xbufs, kinds = exchange if exchange is not None else ((), ())
    nx = len(xbufs)

    def body(*refs):
        a_ref, b_ref = refs[:2]
        o_ref = refs[n_in + nx]
        acc = refs[n_in + 2 * nx + 1]
        i, j, k = pl.program_id(0), pl.program_id(1), pl.program_id(2)
        if nx:
            start, finish = _exchange_plan(refs[n_in:n_in + nx], refs[n_in + nx + 1:n_in + 2 * nx + 1], kinds,
                                           *refs[n_in + 2 * nx + 2:])
            pl.when((i == 0) & (j == 0) & (k == 0))(start)

        d = _dot(a_ref[...], b_ref[...], dims)

        def write_out(r):
            if add is not None:
                r = r + refs[2][...]
            o_ref[...] = r.astype(out_dtype)

        if nk == 1:
            write_out(d)
        else:
            @pl.when(k == 0)
            def _():
                acc[...] = d

            @pl.when((k > 0) & (k < nk - 1))
            def _():
                acc[...] += d

            @pl.when(k == nk - 1)
            def _():
                write_out(acc[...] + d)

        if nx:
            pl.when((i == grid[0] - 1) & (j == grid[1] - 1) & (k == nk - 1))(finish)

    a_spec = pl.BlockSpec((tk, tm), lambda i, j, k: (k, i)) if mode == "tn" else pl.BlockSpec((tm, tk), lambda i, j, k: (i, k))
    b_spec = pl.BlockSpec((tn, tk), lambda i, j, k: (j, k)) if mode == "nt" else pl.BlockSpec((tk, tn), lambda i, j, k: (k, j))
    o_spec = pl.BlockSpec((tm, tn), lambda i, j, k: (i, j))
    ins, specs = [a, b], [a_spec, b_spec]
    if add is not None:
        ins.append(add)
        specs.append(o_spec)
    acc_spec = pltpu.VMEM((tm, tn) if nk > 1 else (8, LANES), F32)
    if not nx:
        return pl.pallas_call(
            body, name=name, grid=grid, in_specs=specs, out_specs=o_spec,
            out_shape=S((M, N), out_dtype), scratch_shapes=[acc_spec],
            compiler_params=_params(("parallel", "parallel", "arbitrary")))(*ins)
    return pl.pallas_call(
        body, name=name, grid=grid, in_specs=specs + [_ANY] * nx, out_specs=[o_spec] + [_ANY] * nx,
        out_shape=[S((M, N), out_dtype)] + _exchange_shapes(xbufs, kinds),
        scratch_shapes=[acc_spec] + _exchange_sems(nx),
        compiler_params=_params(("arbitrary", "arbitrary", "arbitrary")))(*ins, *xbufs)


def _ln_emb_fwd(x, g, b, exchange=None):
    T = x.shape[0]

    nt = T // TB
    xbufs, kinds = exchange if exchange is not None else ((), ())
    nx = len(xbufs)

    def body(*refs):
        x_ref, g_ref, b_ref = refs[:3]
        h_ref, hb_ref, hbt_ref = refs[3 + nx:6 + nx]
        i = pl.program_id(0)
        if nx:
            start, finish = _exchange_plan(refs[3:3 + nx], refs[6 + nx:6 + 2 * nx], kinds, *refs[6 + 2 * nx:])
            pl.when(i == 0)(start)

        def rows(rs):
            xh, _ = _ln_stats(x_ref[rs, :])
            h = xh * g_ref[...] + b_ref[...]
            h_ref[rs, :] = h
            hb_ref[rs, :] = h.astype(BF16)
        _row_loop(TB, rows)
        hbt_ref[...] = hb_ref[...].T
        if nx:
            pl.when(i == nt - 1)(finish)

    return pl.pallas_call(
        body, name="ln_emb_fwd", grid=(nt,),
        in_specs=[_row(TB, D), _full((1, D)), _full((1, D))] + [_ANY] * nx,
        out_specs=[_row(TB, D), _row(TB, D), _colt(D, TB)] + [_ANY] * nx,
        out_shape=[S((T, D), F32), S((T, D), BF16), S((D, T), BF16)] + _exchange_shapes(xbufs, kinds),
        scratch_shapes=_exchange_sems(nx) if nx else [],
        compiler_params=_params(("arbitrary",)))(x, g, b, *xbufs)


def _out_proj_post1(mix, w_out, h0, b_out, g, b, tm=512, tk=1024):
    T, K = mix.shape
    tm = min(tm, T)
    nk = K // tk
    assert T % tm == 0 and K % tk == 0 and nk >= 2

    def body(mix_ref, w_ref, h0_ref, bo_ref, g_ref, b_ref, out_ref, h_ref, hb_ref, hbt_ref, acc):
        k = pl.program_id(1)
        d = _dot(mix_ref[...], w_ref[...])

        @pl.when(k == 0)
        def _():
            acc[...] = d

        @pl.when((k > 0) & (k < nk - 1))
        def _():
            acc[...] += d

        @pl.when(k == nk - 1)
        def _():
            out_ref[...] = acc[...] + d

            def rows(rs):
                xh, _ = _ln_stats(ALPHA * h0_ref[rs, :] + out_ref[rs, :] + bo_ref[...])
                h = xh * g_ref[...] + b_ref[...]
                h_ref[rs, :] = h
                hb_ref[rs, :] = h.astype(BF16)
            _row_loop(tm, rows)
            hbt_ref[...] = hb_ref[...].T

    rowt = lambda n: pl.BlockSpec((tm, n), lambda i, k: (i, 0))
    const = pl.BlockSpec((1, D), lambda i, k: (0, 0))
    return pl.pallas_call(
        body, name="out_proj_post1", grid=(T // tm, nk),
        in_specs=[pl.BlockSpec((tm, tk), lambda i, k: (i, k)), pl.BlockSpec((tk, D), lambda i, k: (k, 0)), rowt(D),
                  const, const, const],
        out_specs=[rowt(D), rowt(D), rowt(D), pl.BlockSpec((D, tm), lambda i, k: (0, i))],
        out_shape=[S((T, D), F32), S((T, D), F32), S((T, D), BF16), S((D, T), BF16)],
        scratch_shapes=[pltpu.VMEM((tm, D), F32)],
        compiler_params=_params(("parallel", "arbitrary")))(mix, w_out, h0, b_out, g, b)


def _ple_post2(h1b, w_gate, pb, w_proj, h1, tgt, g, b, tm=512):
    T = h1.shape[0]
    tm = min(tm, T)
    assert T % tm == 0

    def body(h1b_ref, wg_ref, pb_ref, wp_ref, h1_ref, tgt_ref, g_ref, b_ref,
             dh1_ref, dgp_ref, dple_ref, loss_ref, dg_ref, db_ref, gp_ref, ple_ref):
        @pl.when(pl.program_id(0) == 0)
        def _():
            loss_ref[...] = jnp.zeros_like(loss_ref)
            dg_ref[...] = jnp.zeros_like(dg_ref)
            db_ref[...] = jnp.zeros_like(db_ref)

        gp_ref[...] = _dot(h1b_ref[...], wg_ref[...])
        ple_ref[...] = _dot(pb_ref[...], wp_ref[...])

        def rows(rs):
            gate = _sigmoid(gp_ref[rs, :])
            ple = ple_ref[rs, :]
            xh, r = _ln_stats(ALPHA * h1_ref[rs, :] + gate * ple)
            err = xh * g_ref[...] + b_ref[...] - tgt_ref[rs, :]
            loss_ref[...] += 0.5 * jnp.sum(jnp.mean(err * err, axis=-1, keepdims=True), axis=0, keepdims=True)
            dv, dg, db = _ln_bwd(err * (1.0 / D), xh, r, g_ref[...])
            dg_ref[...] += dg
            db_ref[...] += db
            dh1_ref[rs, :] = ALPHA * dv
            dgp_ref[rs, :] = (dv * ple * gate * (1.0 - gate)).astype(BF16)
            dple_ref[rs, :] = (dv * gate).astype(BF16)
        _row_loop(tm, rows)

    return pl.pallas_call(
        body, name="ple_post2", grid=(T // tm,),
        in_specs=[_row(tm, D), _full((D, D)), _row(tm, D_PLE), _full((D_PLE, D)), _row(tm, D), _row(tm, D),
                  _full((1, D)), _full((1, D))],
        out_specs=[_row(tm, D)] * 3 + [_full((8, LANES)), _full((1, D)), _full((1, D))],
        out_shape=[S((T, D), F32), S((T, D), BF16), S((T, D), BF16), S((8, LANES), F32), S((1, D), F32), S((1, D), F32)],
        scratch_shapes=[pltpu.VMEM((tm, D), F32), pltpu.VMEM((tm, D), F32)],
        compiler_params=_params(("arbitrary",)))(h1b, w_gate, pb, w_proj, h1, tgt, g, b)


def _d_h1_post1_bwd(dgp, w_gate, dh1a, h0, out, b_out, g, tm=512):
    T = h0.shape[0]
    tm = min(tm, T)
    assert T % tm == 0

    def body(dgp_ref, wg_ref, da_ref, h0_ref, out_ref, bo_ref, g_ref, dout_ref, dh0_ref, dg_ref, db_ref, dbo_ref, dh1):
        @pl.when(pl.program_id(0) == 0)
        def _():
            dg_ref[...] = jnp.zeros_like(dg_ref)
            db_ref[...] = jnp.zeros_like(db_ref)
            dbo_ref[...] = jnp.zeros_like(dbo_ref)

        dh1[...] = da_ref[...] + _dot(dgp_ref[...], wg_ref[...], _NT)

        def rows(rs):
            xh, r = _ln_stats(ALPHA * h0_ref[rs, :] + out_ref[rs, :] + bo_ref[...])
            dv, dg, db = _ln_bwd(dh1[rs, :], xh, r, g_ref[...])
            dg_ref[...] += dg
            db_ref[...] += db
            dbo_ref[...] += jnp.sum(dv, axis=0, keepdims=True)
            dout_ref[rs, :] = dv.astype(BF16)
            dh0_ref[rs, :] = ALPHA * dv
        _row_loop(tm, rows)

    return pl.pallas_call(
        body, name="d_h1_post1_bwd", grid=(T // tm,),
        in_specs=[_row(tm, D), _full((D, D))] + [_row(tm, D)] * 3 + [_full((1, D))] * 2,
        out_specs=[_row(tm, D)] * 2 + [_full((1, D))] * 3,
        out_shape=[S((T, D), BF16), S((T, D), F32)] + [S((1, D), F32)] * 3,
        scratch_shapes=[pltpu.VMEM((tm, D), F32)],
        compiler_params=_params(("arbitrary",)))(dgp, w_gate, dh1a, h0, out, b_out, g)


def _d_h0_ln_bwd(dproj, w_main, ddtr, w_dt, dh0a, x, g, exchange=None, tm=1024, tk=1408):
    T, K = dproj.shape
    tm = min(tm, T)
    assert T % tm == 0 and K % tk == 0
    ni, nk = T // tm, K // tk
    xbufs, kinds = exchange if exchange is not None else ((), ())
    nx = len(xbufs)

    def body(*refs):
        dp_ref, w_ref, dt_ref, wdt_ref, da_ref, x_ref, g_ref = refs[:7]
        dx_ref, dg_ref, db_ref = refs[7 + nx:10 + nx]
        acc = refs[10 + 2 * nx]
        i, k = pl.program_id(0), pl.program_id(1)
        if nx:
            start, finish = _exchange_plan(refs[7:7 + nx], refs[10 + nx:10 + 2 * nx], kinds, *refs[11 + 2 * nx:])
            pl.when((i == 0) & (k == 0))(start)

        @pl.when((i == 0) & (k == 0))
        def _():
            dg_ref[...] = jnp.zeros_like(dg_ref)
            db_ref[...] = jnp.zeros_like(db_ref)

        d = _dot(dp_ref[...], w_ref[...], _NT)

        @pl.when(k == 0)
        def _():
            acc[...] = da_ref[...] + _dot(dt_ref[...], wdt_ref[...], _NT) + d

        @pl.when(k > 0)
        def _():
            acc[...] += d

        @pl.when(k == nk - 1)
        def _():
            def rows(rs):
                xh, r = _ln_stats(x_ref[rs, :])
                dv, dg, db = _ln_bwd(acc[rs, :], xh, r, g_ref[...])
                dg_ref[...] += dg
                db_ref[...] += db
                dx_ref[rs, :] = dv
            _row_loop(tm, rows)

        if nx:
            pl.when((i == ni - 1) & (k == nk - 1))(finish)

    rowt = lambda n: pl.BlockSpec((tm, n), lambda i, k: (i, 0))
    const = lambda shape: pl.BlockSpec(shape, lambda i, k: (0, 0))
    return pl.pallas_call(
        body, name="d_h0_ln_bwd", grid=(ni, nk),
        in_specs=[pl.BlockSpec((tm, tk), lambda i, k: (i, k)), pl.BlockSpec((D, tk), lambda i, k: (0, k)),
                  rowt(LANES), const((D, LANES)), rowt(D), rowt(D), const((1, D))] + [_ANY] * nx,
        out_specs=[rowt(D), const((1, D)), const((1, D))] + [_ANY] * nx,
        out_shape=[S((T, D), F32), S((1, D), F32), S((1, D), F32)] + _exchange_shapes(xbufs, kinds),
        scratch_shapes=[pltpu.VMEM((tm, D), F32)] + (_exchange_sems(nx) if nx else []),
        compiler_params=_params(("arbitrary", "arbitrary")))(dproj, w_main, ddtr, w_dt, dh0a, x, g, *xbufs)


def _softplus(x):
    return jnp.maximum(x, 0.0) + jnp.log1p(jnp.exp(-jnp.abs(x)))


def _ssd_pre_fwd(proj, dt_raw, wx, wb, bx, bb, dt_bias):
    T = proj.shape[0]
    H = HALO_SSM

    def body(xs_ref, xsp_ref, bc_ref, bcp_ref, dtr_ref, wx_ref, wb_ref, bx_ref, bb_ref, dtb_ref,
             xso_ref, bco_ref, dto_ref, extx, extb):
        first = pl.program_id(0) == 0

        def conv(t_ref, p_ref, w_ref, b_ref, o_ref, ext, n):
            def blk(cols):
                ext[0:H, cols] = jnp.where(first, 0.0, p_ref[:, cols])
                ext[H:, cols] = t_ref[:, cols]
                for r0 in range(0, TB, 64):
                    acc = jnp.broadcast_to(b_ref[:, cols], (64, LANES))
                    for k in range(K_SSM):
                        acc = acc + w_ref[k:k + 1, cols] * ext[pl.ds(r0 + H - (K_SSM - 1) + k, 64), cols]
                    o_ref[pl.ds(r0, 64), cols] = acc * _sigmoid(acc)
            _col_loop(n, blk)

        conv(xs_ref, xsp_ref, wx_ref, bx_ref, xso_ref, extx, D_SSM)
        conv(bc_ref, bcp_ref, wb_ref, bb_ref, bco_ref, extb, 512)
        dto_ref[...] = _softplus(dtr_ref[...] + dtb_ref[...])

    return pl.pallas_call(
        body, name="ssd_pre_fwd", grid=(T // TB,),
        in_specs=[_row(TB, 1024, C_XS), _prev(TB, H, 1024, C_XS), _row(TB, 512, C_BC), _prev(TB, H, 512, C_BC),
                  _row(TB, LANES), _full((K_SSM, 1024)), _full((K_SSM, 512)), _full((1, 1024)), _full((1, 512)),
                  _full((1, LANES))],
        out_specs=[_row(TB, 1024), _row(TB, 512), _row(TB, LANES)],
        out_shape=[S((T, 1024), F32), S((T, 512), F32), S((T, LANES), F32)],
        scratch_shapes=[pltpu.VMEM((H + TB, 1024), F32), pltpu.VMEM((H + TB, 512), F32)],
        compiler_params=_params(("parallel",)))(proj, proj, proj, proj, dt_raw, wx, wb, bx, bb, dt_bias)


def _ssd_conv_bwd(dproj, proj, d_c, w, b, n, col, name):
    T = proj.shape[0]
    nt = T // TB
    H = HALO_SSM
    R = TB + H

    def body(dproj_ref, t_ref, p_ref, n_ref, d_ref, dn_ref, w_ref, b_ref, o_ref, dw_ref, dbias_ref, ext, dp):
        i = pl.program_id(0)
        first, last = i == 0, i == nt - 1

        @pl.when(first)
        def _():
            dw_ref[...] = jnp.zeros_like(dw_ref)
            dbias_ref[...] = jnp.zeros_like(dbias_ref)

        def blk(cols):
            ext[0:H, cols] = jnp.where(first, 0.0, p_ref[:, cols])
            ext[H:H + TB, cols] = t_ref[:, cols]
            ext[H + TB:, cols] = n_ref[:, cols]
            pre = jnp.broadcast_to(b_ref[:, cols], (R, LANES))
            for k in range(K_SSM):
                pre = pre + w_ref[k:k + 1, cols] * ext[pl.ds(H - (K_SSM - 1) + k, R), cols]
            s = _sigmoid(pre)
            ds = _dsilu(pre, s)
            dp[0:TB, cols] = d_ref[:, cols] * ds[0:TB]
            dp[TB:, cols] = jnp.where(last, 0.0, dn_ref[:, cols] * ds[TB:])
            dpt = dp[0:TB, cols]
            dbias_ref[:, cols] += jnp.sum(dpt, axis=0, keepdims=True)
            acc = jnp.zeros((TB, LANES), F32)
            for k in range(K_SSM):
                dw_ref[k:k + 1, cols] += jnp.sum(dpt * ext[pl.ds(H - (K_SSM - 1) + k, TB), cols], axis=0, keepdims=True)
                acc = acc + w_ref[k:k + 1, cols] * dp[pl.ds(K_SSM - 1 - k, TB), cols]
            o_ref[:, cols] = acc.astype(BF16)
        _col_loop(n, blk)

    return pl.pallas_call(
        body, name=name, grid=(nt,),
        in_specs=[_ANY, _row(TB, n, col), _prev(TB, H, n, col), _next(TB, H, n, nt, col),
                  _row(TB, n), _next(TB, H, n, nt), _full((K_SSM, n)), _full((1, n))],
        out_specs=[_row(TB, n, col), _full((K_SSM, n)), _full((1, n))],
        out_shape=[S(dproj.shape, BF16), S((K_SSM, n), F32), S((1, n), F32)],
        input_output_aliases={0: 0},
        scratch_shapes=[pltpu.VMEM((H + TB + H, n), F32), pltpu.VMEM((R, n), F32)],
        compiler_params=_params(("arbitrary",)))(dproj, proj, proj, proj, d_c, d_c, w, b)


def _ssd_consts():
    ex = np.zeros((LANES, D_SSM), np.float32)
    for h in range(N_HEADS):
        ex[h, h * HEAD:(h + 1) * HEAD] = 1.0
    tri = np.tril(np.ones((CHUNK, CHUNK), np.float32))
    return jnp.asarray(ex), jnp.asarray(ex.T.copy()), jnp.asarray(tri), jnp.asarray(tri.T.copy())


def _ssd_common(xs, dt, alog_ref, ex_ref, tri_ref):
    lane = lax.broadcasted_iota(jnp.int32, (1, LANES), 1)
    a = jnp.where(lane < N_HEADS, -jnp.exp(alog_ref[...]), 0.0)
    A = _dot_sel_a(tri_ref[...], dt * a)
    ex = ex_ref[...]
    Aex = _dot_sel_b(A, ex)
    dtex = _dot_sel_b(dt, ex)
    expA = jnp.exp(Aex)
    dec = jnp.exp(Aex[CHUNK - 1:CHUNK, :] - Aex)
    cd = _dot_sel_a(ex, jnp.broadcast_to(jnp.exp(A.T[:, CHUNK - 1:CHUNK]), (LANES, LANES)), _TN)
    return a, A, dtex, expA, dec, cd


def _decay_mask():
    sub = lax.broadcasted_iota(jnp.int32, (CHUNK, CHUNK), 0)
    lane = lax.broadcasted_iota(jnp.int32, (CHUNK, CHUNK), 1)
    return sub, lane, sub >= lane


def _ssd_fwd(xs_c, bc_c, dt, proj, alog, dskip_row, norm_g):
    T = xs_c.shape[0]
    nc = T // CHUNK
    ex, _, tri, _ = _ssd_consts()

    def body(xs_ref, bc_ref, dt_ref, z_ref, alog_ref, dsk_ref, ng_ref, ex_ref, tri_ref,
             ys_ref, ypre_ref, hprev_ref, yst_ref, Hs, ybuf):
        @pl.when(pl.program_id(0) == 0)
        def _():
            Hs[...] = jnp.zeros_like(Hs)

        hprev_ref[0] = Hs[...]
        xs, dt = xs_ref[...], dt_ref[...]
        a, A, dtex, expA, dec, cd = _ssd_common(xs, dt, alog_ref, ex_ref, tri_ref)
        AT = A.T
        xdt = xs * dtex
        xdec = xdt * dec
        _, _, causal = _decay_mask()
        for g in range(2):
            gs = slice(g * 512, (g + 1) * 512)
            B = bc_ref[:, g * N_STATE:(g + 1) * N_STATE]
            C = bc_ref[:, 256 + g * N_STATE:256 + (g + 1) * N_STATE]
            cb = _dot(C, B, _NT)
            Hg = Hs[gs, :]
            yoff = _dot(C, Hg, _NT) * expA[:, gs]
            for j in range(8):
                h = g * 8 + j
                hs = slice(h * HEAD, (h + 1) * HEAD)
                L = jnp.exp(jnp.where(causal, A[:, h:h + 1] - AT[h:h + 1, :], -1e30))
                ybuf[:, hs] = _dot(cb * L, xdt[:, hs]) + yoff[:, j * HEAD:(j + 1) * HEAD]
            Hs[gs, :] = cd[gs, :] * Hg + _dot(xdec[:, gs], B, _TN)
        ypre = ybuf[...] + dsk_ref[...] * xs
        ypre_ref[...] = ypre
        z = z_ref[...]
        yz = ypre * (z * _sigmoid(z))
        for g in range(2):
            gs = slice(g * 512, (g + 1) * 512)
            v = yz[:, gs]
            r = lax.rsqrt(jnp.mean(v * v, axis=-1, keepdims=True) + RMS_EPS)
            ys_ref[:, gs] = (v * r * ng_ref[:, gs]).astype(BF16)
        yst_ref[...] = ys_ref[...].T

    return pl.pallas_call(
        body, name="ssd_fwd", grid=(nc,),
        in_specs=[_row(CHUNK, 1024), _row(CHUNK, 512), _row(CHUNK, LANES), _row(CHUNK, 1024, C_Z),
                  _full((1, LANES)), _full((1, 1024)), _full((1, 1024)), _full((LANES, 1024)), _full((CHUNK, CHUNK))],
        out_specs=[_row(CHUNK, 1024), _row(CHUNK, 1024), pl.BlockSpec((1, 1024, N_STATE), lambda c: (c, 0, 0)),
                   _colt(1024, CHUNK)],
        out_shape=[S((T, 2048), BF16), S((T, 1024), F32), S((nc, 1024, N_STATE), F32), S((2048, T), BF16)],
        scratch_shapes=[pltpu.VMEM((1024, N_STATE), F32), pltpu.VMEM((CHUNK, 1024), F32)],
        compiler_params=_params(("arbitrary",)))(xs_c, bc_c, dt, proj, alog, dskip_row, norm_g, ex, tri)


def _ssd_bwd(dproj, xs_c, bc_c, dt, dt_raw, dt_bias, proj, ypre, hprev, dmix, alog, dskip_row, norm_g, exchange=None):
    T = xs_c.shape[0]
    nc = T // CHUNK
    ex, ext, tri, triu = _ssd_consts()
    rev = lambda n, col=0: pl.BlockSpec((CHUNK, n), lambda c: (nc - 1 - c, col))
    xbufs, kinds = exchange if exchange is not None else ((), ())
    nx = len(xbufs)
    N_IN, N_OUT = 17, 8

    def body(*refs):
        (dproj_ref, xs_ref, bc_ref, dt_ref, dtr_ref, dtb_ref, z_ref, ypre_ref, hprev_ref, dys_ref, alog_ref, dsk_ref,
         ng_ref, ex_ref, ext_ref, tri_ref, triu_ref) = refs[:N_IN]
        (dxs_ref, dbc_ref, ddt_ref, dz_ref, dng_ref, ddsk_ref, dalog_ref,
         ddtb_ref) = refs[N_IN + nx:N_IN + nx + N_OUT]
        dHs, dxbuf, dskacc = refs[N_IN + N_OUT + 2 * nx:N_IN + N_OUT + 2 * nx + 3]
        c = pl.program_id(0)
        if nx:
            start, finish = _exchange_plan(refs[N_IN:N_IN + nx], refs[N_IN + nx + N_OUT:N_IN + N_OUT + 2 * nx], kinds,
                                           *refs[N_IN + N_OUT + 2 * nx + 3:])
            pl.when(c == 0)(start)

        @pl.when(c == 0)
        def _():
            dHs[...] = jnp.zeros_like(dHs)
            dng_ref[...] = jnp.zeros_like(dng_ref)
            dalog_ref[...] = jnp.zeros_like(dalog_ref)
            ddtb_ref[...] = jnp.zeros_like(ddtb_ref)
            dskacc[...] = jnp.zeros_like(dskacc)

        xs, dt, z, ypre, dys = xs_ref[...], dt_ref[...], z_ref[...], ypre_ref[...], dys_ref[...]
        sg = _sigmoid(z)
        sz = z * sg
        yz = ypre * sz
        dyz_parts = []
        for g in range(2):
            gs = slice(g * 512, (g + 1) * 512)
            v = yz[:, gs]
            r = lax.rsqrt(jnp.mean(v * v, axis=-1, keepdims=True) + RMS_EPS)
            vn = v * r
            dng_ref[:, gs] += jnp.sum(dys[:, gs] * vn, axis=0, keepdims=True)
            dvn = dys[:, gs] * ng_ref[:, gs]
            dyz_parts.append(r * (dvn - vn * jnp.mean(dvn * vn, axis=-1, keepdims=True)))
        dyz = jnp.concatenate(dyz_parts, axis=1)
        dy = dyz * sz
        dz_ref[...] = (dyz * ypre * _dsilu(z, sg)).astype(BF16)
        dskacc[...] += jnp.sum(dy * xs, axis=0, keepdims=True)

        a, A, dtex, expA, dec, cd = _ssd_common(xs, dt, alog_ref, ex_ref, tri_ref)
        AT = A.T
        xdt = xs * dtex
        xdec = xdt * dec
        dye = dy * expA
        H = hprev_ref[0]
        dHn = dHs[...]
        sub, lane, causal = _decay_mask()
        dAc = jnp.zeros((CHUNK, LANES), F32)
        Rm = jnp.zeros((CHUNK, LANES), F32)
        yoff_parts, q_parts = [], []
        for g in range(2):
            gs = slice(g * 512, (g + 1) * 512)
            B = bc_ref[:, g * N_STATE:(g + 1) * N_STATE]
            C = bc_ref[:, 256 + g * N_STATE:256 + (g + 1) * N_STATE]
            cb = _dot(C, B, _NT)
            Hg, dHg = H[gs, :], dHn[gs, :]
            Q = _dot(B, dHg, _NT)
            yoff_parts.append(_dot(C, Hg, _NT) * expA[:, gs])
            q_parts.append(Q)
            dcb = jnp.zeros((CHUNK, CHUNK), F32)
            for j in range(8):
                h = g * 8 + j
                hs = slice(h * HEAD, (h + 1) * HEAD)
                L = jnp.exp(jnp.where(causal, A[:, h:h + 1] - AT[h:h + 1, :], -1e30))
                M = cb * L
                G = _dot(dy[:, hs], xdt[:, hs], _NT)
                dxbuf[:, hs] = _dot(M, dy[:, hs], _TN)
                dcb = dcb + G * L
                E = G * M
                dAc = jnp.where(lane == h, jnp.sum(E, axis=1, keepdims=True), dAc)
                Rm = jnp.where(sub == h, jnp.sum(E, axis=0, keepdims=True), Rm)
            dbc_ref[:, g * N_STATE:(g + 1) * N_STATE] = _dot(dcb, C, _TN) + _dot(xdec[:, gs], dHg)
            dbc_ref[:, 256 + g * N_STATE:256 + (g + 1) * N_STATE] = _dot(dcb, B) + _dot(dye[:, gs], Hg)
            dHs[gs, :] = cd[gs, :] * dHg + _dot(dye[:, gs], C, _TN)
        yoff = jnp.concatenate(yoff_parts, axis=1)
        Qd = jnp.concatenate(q_parts, axis=1) * dec
        dxdt = dxbuf[...] + Qd
        extm = ext_ref[...]
        red_s = _dot_sel_b(xdt * Qd, extm)
        dA = dAc - Rm.T + _dot_sel_b(dy * yoff, extm) - red_s
        hd = jnp.sum(_dot_sel_b(H * dHn, extm, _TN), axis=0, keepdims=True)
        last_add = jnp.sum(red_s, axis=0, keepdims=True) + jnp.exp(A[CHUNK - 1:CHUNK, :]) * hd
        dA = dA + jnp.where(sub == CHUNK - 1, last_add, 0.0)
        dadt = _dot_sel_a(triu_ref[...], dA)
        ddtr = (dadt * a + _dot_sel_b(dxdt * xs, extm)) * _sigmoid(dtr_ref[...] + dtb_ref[...])
        ddt_ref[...] = ddtr.astype(BF16)
        ddtb_ref[...] += jnp.sum(ddtr, axis=0, keepdims=True)
        dalog_ref[...] += jnp.sum(dadt * dt, axis=0, keepdims=True) * a
        dxs_ref[...] = dxdt * dtex + dsk_ref[...] * dy

        @pl.when(c == nc - 1)
        def _():
            ddsk_ref[...] = _dot_sel_b(jnp.broadcast_to(dskacc[...], (8, 1024)), extm)[0:1, :]

        if nx:
            pl.when(c == nc - 1)(finish)

    return pl.pallas_call(
        body, name="ssd_bwd", grid=(nc,),
        in_specs=[_ANY, rev(1024), rev(512), rev(LANES), rev(LANES), _full((1, LANES)), rev(1024, C_Z), rev(1024),
                  pl.BlockSpec((1, 1024, N_STATE), lambda c: (nc - 1 - c, 0, 0)), rev(1024, 0),
                  _full((1, LANES)), _full((1, 1024)), _full((1, 1024)),
                  _full((LANES, 1024)), _full((1024, LANES)), _full((CHUNK, CHUNK)), _full((CHUNK, CHUNK))] + [_ANY] * nx,
        out_specs=[rev(1024), rev(512), rev(LANES), rev(1024, C_Z), _full((1, 1024)), _full((1, LANES)),
                   _full((1, LANES)), _full((1, LANES))] + [_ANY] * nx,
        out_shape=[S((T, 1024), F32), S((T, 512), F32), S((T, LANES), BF16), S(dproj.shape, BF16),
                   S((1, 1024), F32), S((1, LANES), F32), S((1, LANES), F32), S((1, LANES), F32)]
        + _exchange_shapes(xbufs, kinds),
        input_output_aliases={0: 3},
        scratch_shapes=[pltpu.VMEM((1024, N_STATE), F32), pltpu.VMEM((CHUNK, 1024), F32), pltpu.VMEM((1, 1024), F32)]
        + (_exchange_sems(nx) if nx else []),
        compiler_params=_params(("arbitrary",)))(
            dproj, xs_c, bc_c, dt, dt_raw, dt_bias, proj, ypre, hprev, dmix, alog, dskip_row, norm_g, ex, ext, tri, triu,
            *xbufs)


def _shifted_copies(ext, ext8):
    n = ext8.shape[1]
    for r in range(8):
        ext8[r] = ext[pl.ds(r, n), :]


def _shifted(ext8, off, rows):
    return ext8[off % 8, pl.ds(off - off % 8, rows), :]


def _conf_fwd(mix, mixt, proj, w, cb, lg, lb, ba, bb):
    T = proj.shape[0]
    H = HALO_CONF

    def body(mix_ref, mixt_ref, ga_ref, gap_ref, gb_ref, gbp_ref, cg_ref, w_ref, cb_ref, lg_ref, lb_ref, ba_ref,
             bb_ref, u1_ref, yc_ref, yct_ref, ext, ext8):
        first = pl.program_id(0) == 0
        ext[H + TB:, :] = jnp.zeros((8, LANES), F32)

        def blk(cols):
            up = (gap_ref[:, cols] + ba_ref[:, cols]) * _sigmoid(gbp_ref[:, cols] + bb_ref[:, cols])
            ext[0:H, :] = jnp.where(first, 0.0, up)
            ext[H:H + TB, :] = (ga_ref[:, cols] + ba_ref[:, cols]) * _sigmoid(gb_ref[:, cols] + bb_ref[:, cols])
            _shifted_copies(ext, ext8)
            for r0 in range(0, TB, 64):
                acc = jnp.broadcast_to(cb_ref[:, cols], (64, LANES))
                for k in range(K_CONF):
                    acc = acc + w_ref[k:k + 1, cols] * _shifted(ext8, r0 + H - (K_CONF - 1) + k, 64)
                u1_ref[pl.ds(r0, 64), cols] = acc
        _col_loop(D_CONF, blk)

        def rows(rs):
            xh, _ = _ln_stats(u1_ref[rs, :])
            u2 = xh * lg_ref[...] + lb_ref[...]
            cg = cg_ref[rs, :]
            yc_ref[rs, :] = (u2 * _sigmoid(u2) * cg * _sigmoid(cg)).astype(BF16)
        _row_loop(TB, rows)
        yct_ref[...] = yc_ref[...].T

    return pl.pallas_call(
        body, name="conf_fwd", grid=(T // TB,),
        in_specs=[_ANY, _ANY, _row(TB, 1024, C_GLUA), _prev(TB, H, 1024, C_GLUA), _row(TB, 1024, C_GLUB),
                  _prev(TB, H, 1024, C_GLUB), _row(TB, 1024, C_CG), _full((K_CONF, 1024))] + [_full((1, 1024))] * 5,
        out_specs=[_row(TB, 1024), _row(TB, 1024, 1), _colt(1024, TB, 1)],
        out_shape=[S((T, 1024), F32), S((T, 2048), BF16), S((2048, T), BF16)],
        input_output_aliases={0: 1, 1: 2},
        scratch_shapes=[pltpu.VMEM((H + TB + 8, LANES), F32), pltpu.VMEM((8, H + TB, LANES), F32)],
        compiler_params=_params(("parallel",)))(mix, mixt, proj, proj, proj, proj, proj, w, cb, lg, lb, ba, bb)


def _conf_bwd1(dmix, u1, proj, lg, lb):
    T = u1.shape[0]

    def body(dy_ref, u1_ref, cg_ref, lg_ref, lb_ref, du1_ref, dcg_ref, dg_ref, db_ref):
        @pl.when(pl.program_id(0) == 0)
        def _():
            dg_ref[...] = jnp.zeros_like(dg_ref)
            db_ref[...] = jnp.zeros_like(db_ref)

        def rows(rs):
            xh, r = _ln_stats(u1_ref[rs, :])
            u2 = xh * lg_ref[...] + lb_ref[...]
            s2 = _sigmoid(u2)
            cg = cg_ref[rs, :]
            sc = _sigmoid(cg)
            dy = dy_ref[rs, :]
            dcg_ref[rs, :] = (dy * u2 * s2 * _dsilu(cg, sc)).astype(BF16)
            dv, dg, db = _ln_bwd(dy * cg * sc * _dsilu(u2, s2), xh, r, lg_ref[...])
            dg_ref[...] += dg
            db_ref[...] += db
            du1_ref[rs, :] = dv
        _row_loop(TB, rows)

    return pl.pallas_call(
        body, name="conf_bwd1", grid=(T // TB,),
        in_specs=[_row(TB, 1024, 1), _row(TB, 1024), _row(TB, 1024, C_CG), _full((1, 1024)), _full((1, 1024))],
        out_specs=[_row(TB, 1024), _row(TB, 1024, C_CG), _full((1, 1024)), _full((1, 1024))],
        out_shape=[S((T, 1024), F32), S((T, N_MAIN), BF16), S((1, 1024), F32), S((1, 1024), F32)],
        compiler_params=_params(("arbitrary",)))(dmix, u1, proj, lg, lb)


def _conf_bwd2(dproj, proj, du1, w, ba, bb):
    T = du1.shape[0]
    nt = T // TB
    H = HALO_CONF

    def body(dproj_ref, ga_ref, gap_ref, gb_ref, gbp_ref, du_ref, dun_ref, w_ref, ba_ref, bb_ref,
             dg_ref, dw_ref, dcb_ref, dba_ref, dbb_ref, ext, dext, ext8, dext8, dwacc):
        i = pl.program_id(0)
        first, last = i == 0, i == nt - 1

        @pl.when(first)
        def _():
            for r in (dcb_ref, dba_ref, dbb_ref, dwacc):
                r[...] = jnp.zeros_like(r)

        ext[H + TB:, :] = jnp.zeros((8, LANES), F32)
        dext[H + TB:, :] = jnp.zeros((8, LANES), F32)

        def blk(cols):
            cols_b = pl.ds(pl.multiple_of(cols.start + D_CONF, LANES), LANES)
            up = (gap_ref[:, cols] + ba_ref[:, cols]) * _sigmoid(gbp_ref[:, cols] + bb_ref[:, cols])
            ext[0:H, :] = jnp.where(first, 0.0, up)
            a = ga_ref[:, cols] + ba_ref[:, cols]
            sb = _sigmoid(gb_ref[:, cols] + bb_ref[:, cols])
            ext[H:H + TB, :] = a * sb
            du = du_ref[:, cols]
            dext[0:TB, :] = du
            dext[TB:TB + H, :] = jnp.where(last, 0.0, dun_ref[:, cols])
            _shifted_copies(ext, ext8)
            _shifted_copies(dext, dext8)
            dcb_ref[:, cols] += jnp.sum(du, axis=0, keepdims=True)
            for r0 in range(0, TB, 64):
                dur = du_ref[pl.ds(r0, 64), cols]
                acc = jnp.zeros((64, LANES), F32)
                for k in range(K_CONF):
                    prod = dur * _shifted(ext8, r0 + H - (K_CONF - 1) + k, 64)
                    dwacc[k * 8:(k + 1) * 8, cols] += prod.reshape(8, 8, LANES).sum(axis=0)
                    acc = acc + w_ref[k:k + 1, cols] * _shifted(dext8, r0 + K_CONF - 1 - k, 64)
                ar, sr = a[r0:r0 + 64], sb[r0:r0 + 64]
                da = acc * sr
                dbv = acc * ar * sr * (1.0 - sr)
                dg_ref[pl.ds(r0, 64), cols] = da.astype(BF16)
                dg_ref[pl.ds(r0, 64), cols_b] = dbv.astype(BF16)
                dba_ref[:, cols] += jnp.sum(da, axis=0, keepdims=True)
                dbb_ref[:, cols] += jnp.sum(dbv, axis=0, keepdims=True)
        _col_loop(D_CONF, blk)

        @pl.when(last)
        def _():
            dw_ref[...] = jnp.sum(dwacc[...].reshape(K_CONF, 8, D_CONF), axis=1)

    return pl.pallas_call(
        body, name="conf_bwd2", grid=(nt,),
        in_specs=[_ANY, _row(TB, 1024, C_GLUA), _prev(TB, H, 1024, C_GLUA), _row(TB, 1024, C_GLUB),
                  _prev(TB, H, 1024, C_GLUB), _row(TB, 1024), _next(TB, H, 1024, nt), _full((K_CONF, 1024)),
                  _full((1, 1024)), _full((1, 1024))],
        out_specs=[_row(TB, 2048), _full((K_CONF, 1024)), _full((1, 1024)), _full((1, 1024)), _full((1, 1024))],
        out_shape=[S(dproj.shape, BF16), S((K_CONF, 1024), F32)] + [S((1, 1024), F32)] * 3,
        input_output_aliases={0: 0},
        scratch_shapes=[pltpu.VMEM((H + TB + 8, LANES), F32), pltpu.VMEM((TB + H + 8, LANES), F32),
                        pltpu.VMEM((8, H + TB, LANES), F32), pltpu.VMEM((8, TB + H, LANES), F32),
                        pltpu.VMEM((K_CONF * 8, D_CONF), F32)],
        compiler_params=_params(("arbitrary",)))(dproj, proj, proj, proj, proj, du1, du1, w, ba, bb)


def _mesh_pos():
    x, y, c = lax.axis_index("x"), lax.axis_index("y"), lax.axis_index("c")
    return x, y, c, 4 * x + 2 * y + c


def _peer(x, y, c, k):
    return (x ^ ((k >> 2) & 1), y ^ ((k >> 1) & 1), c ^ (k & 1))


def _exchange_copies(ins, outs, kinds, send, recv, loc):
    nb = len(ins)
    x, y, c, me = _mesh_pos()
    src = lambda b, d: ins[b].at[d] if kinds[b] == "blocks" else ins[b]
    copies = [pltpu.make_async_copy(src(b, me), outs[b].at[me], loc.at[b]) for b in range(nb)]
    for k in range(1, N_DEV):
        px, py, pc = _peer(x, y, c, k)
        for b in range(nb):
            s = (k - 1) * nb + b
            copies.append(pltpu.make_async_remote_copy(
                src_ref=src(b, 4 * px + 2 * py + pc), dst_ref=outs[b].at[me], send_sem=send.at[s], recv_sem=recv.at[s],
                device_id=(px, py, pc), device_id_type=pl.DeviceIdType.MESH))
    return copies


def _exchange_shapes(bufs, kinds):
    return [S(b.shape if kd == "blocks" else (N_DEV,) + b.shape, b.dtype) for b, kd in zip(bufs, kinds)]


def _exchange_sems(nb):
    n = (N_DEV - 1) * nb
    return [pltpu.SemaphoreType.DMA((n,)), pltpu.SemaphoreType.DMA((n,)), pltpu.SemaphoreType.DMA((nb,))]


def _two_level_gather(ins, outs, send, recv, loc):
    nb = len(ins)
    x, y, c, me = _mesh_pos()
    here, sibling = (x, y, c), (x, y, 1 - c)
    chips = [(1 - x, y), (x, 1 - y), (1 - x, 1 - y)]

    def copy(slot, b, block, to, src=None):
        d = 4 * block[0] + 2 * block[1] + block[2]
        return pltpu.make_async_remote_copy(
            src_ref=outs[b].at[d] if src is None else src, dst_ref=outs[b].at[d],
            send_sem=send.at[slot * nb + b], recv_sem=recv.at[slot * nb + b],
            device_id=to, device_id_type=pl.DeviceIdType.MESH)

    mine = [pltpu.make_async_copy(ins[b], outs[b].at[me], loc.at[b]) for b in range(nb)]
    first = [copy(0, b, here, sibling, src=ins[b]) for b in range(nb)]
    first += [copy(1 + j, b, here, (*chip, c), src=ins[b]) for j, chip in enumerate(chips) for b in range(nb)]

    def start():
        for cp in mine + first:
            cp.start()

    def finish():
        passed = []
        for j, chip in enumerate(chips):
            for b in range(nb):
                copy(1 + j, b, (*chip, c), here).wait_recv()
            onward = [copy(4 + j, b, (*chip, c), sibling) for b in range(nb)]
            for cp in onward:
                cp.start()
            passed += onward
        for b in range(nb):
            copy(0, b, sibling, here).wait_recv()
        for j, chip in enumerate(chips):
            for b in range(nb):
                copy(4 + j, b, (*chip, 1 - c), here).wait_recv()
        for cp in first + passed:
            cp.wait_send()
        for cp in mine:
            cp.wait()

    return start, finish


def _exchange_plan(ins, outs, kinds, send, recv, loc):
    if all(kd == "gather" for kd in kinds):
        return _two_level_gather(ins, outs, send, recv, loc)
    copies = _exchange_copies(ins, outs, kinds, send, recv, loc)

    def start():
        for cp in copies:
            cp.start()

    def finish():
        for cp in copies:
            cp.wait()

    return start, finish


def _exchange(bufs, kinds, name):
    nb = len(bufs)

    def body(*refs):
        start, finish = _exchange_plan(refs[:nb], refs[nb:2 * nb], kinds, *refs[2 * nb:])
        start()
        finish()

    return pl.pallas_call(
        body, name=name, in_specs=[_ANY] * nb, out_specs=[_ANY] * nb,
        out_shape=_exchange_shapes(bufs, kinds), scratch_shapes=_exchange_sems(nb))(*bufs)


def _sum_parts(p_ref):
    acc = p_ref[0].astype(F32)
    for d in range(1, N_DEV):
        acc = acc + p_ref[d].astype(F32)
    return acc


def _adamw_math(g, w, m, v):
    m = ADAM_B1 * m + (1.0 - ADAM_B1) * g
    v = ADAM_B2 * v + (1.0 - ADAM_B2) * (g * g)
    m_hat = m / (1.0 - ADAM_B1 ** ADAM_STEP)
    v_hat = v / (1.0 - ADAM_B2 ** ADAM_STEP)
    return -ADAM_LR * (m_hat / (jnp.sqrt(v_hat) + ADAM_EPS) + ADAM_WD * w), m, v


HEAD_ROWS = 256


def _sum8_adamw(parts, w, m, v, name, head=None):
    _, R, C = w.shape
    tb = HEAD_ROWS if R % HEAD_ROWS == 0 else R
    assert head is None or (tb == HEAD_ROWS and head.shape[1] == HEAD_ROWS and parts.shape[1] == R - HEAD_ROWS)
    skip = 0 if head is None else 1

    def body(*refs):
        p_ref, w_ref, m_ref, v_ref, g_ref, d_ref, mo_ref, vo_ref = refs[skip:]
        g = _sum_parts(p_ref)
        if head is not None:
            g = jnp.where(pl.program_id(0) == 0, _sum_parts(refs[0]), g)
        g_ref[0] = g
        d_ref[0], mo_ref[0], vo_ref[0] = _adamw_math(g, w_ref[0], m_ref[0], v_ref[0])

    first = [] if head is None else [pl.BlockSpec((N_DEV, tb, C), lambda i: (0, 0, 0))]
    own = pl.BlockSpec((1, tb, C), lambda i: (0, i, 0))
    return pl.pallas_call(
        body, name=name, grid=(R // tb,),
        in_specs=first + [pl.BlockSpec((N_DEV, tb, C), lambda i: (0, jnp.maximum(i - skip, 0), 0))] + [own] * 3,
        out_specs=[own] * 4, out_shape=[S((1, R, C), F32)] * 4,
        compiler_params=_params(("parallel",)))(*([] if head is None else [head]), parts, w, m, v)


SMALL_LAYOUT = (
    ("ln_emb_g", 0, 1024), ("ln_emb_b", 0, 1024), ("ssm_conv_b", 0, 1024), ("ssm_conv_b", 1024, 512),
    ("dt_bias", 0, N_HEADS), ("a_log", 0, N_HEADS), ("d_skip", 0, N_HEADS), ("ssm_norm_g", 0, 1024),
    ("b_glu", 0, 1024), ("b_glu", 1024, 1024), ("conf_conv_b", 0, 1024), ("conf_ln_g", 0, 1024),
    ("conf_ln_b", 0, 1024), ("b_out", 0, 1024), ("ln1_g", 0, 1024), ("ln1_b", 0, 1024), ("ln2_g", 0, 1024),
    ("ln2_b", 0, 1024))
SMALL_ROWS = 24
SMALL = tuple(dict.fromkeys(n for n, _, _ in SMALL_LAYOUT))


LOSS_ROW = len(SMALL_LAYOUT)


def _pack_small(rows, loss):
    def body(*refs):
        o_ref = refs[-1]
        o_ref[...] = jnp.zeros_like(o_ref)
        for r, ref in enumerate(refs[:-2]):
            o_ref[r:r + 1, 0:ref.shape[1]] = ref[...]
        o_ref[LOSS_ROW:LOSS_ROW + 1, 0:LANES] = refs[-2][0:1, :]

    return pl.pallas_call(body, name="pack_small", out_shape=S((SMALL_ROWS, 1024), F32))(*rows, loss)


def _small_update(parts, w, m, v):
    def body(*refs):
        p_ref = refs[0]
        ins = {n: refs[1 + 3 * i:4 + 3 * i] for i, n in enumerate(SMALL)}
        o0 = 1 + 3 * len(SMALL)
        outs = {n: refs[o0 + 4 * i:o0 + 4 * i + 4] for i, n in enumerate(SMALL)}
        gsum = refs[-1]
        gsum[...] = _sum_parts(p_ref)
        refs[-2][...] = gsum[LOSS_ROW:LOSS_ROW + 1, 0:LANES]
        for r, (n, off, wd) in enumerate(SMALL_LAYOUT):
            cs = slice(off, off + wd)
            g = gsum[r:r + 1, 0:wd]
            w_ref, m_ref, v_ref = ins[n]
            g_ref, d_ref, mo_ref, vo_ref = outs[n]
            g_ref[:, cs] = g
            d_ref[:, cs], mo_ref[:, cs], vo_ref[:, cs] = _adamw_math(g, w_ref[:, cs], m_ref[:, cs], v_ref[:, cs])

    args = [parts] + [a for n in SMALL for a in (w[n], m[n], v[n])]
    res = pl.pallas_call(
        body, name="small_update",
        out_shape=[S(w[n].shape, F32) for n in SMALL for _ in range(4)] + [S((1, LANES), F32)],
        scratch_shapes=[pltpu.VMEM((SMALL_ROWS, 1024), F32)])(*args)
    return tuple({n: res[4 * i + j] for i, n in enumerate(SMALL)} for j in range(4)) + (res[-1],)


EARLY = ("w_in", "ssm_conv_w", "conf_conv_w")
LATE = ("w_out", "w_ple_gate", "w_ple_proj")


def _local_step(x, p, tgt, W, shards=None):
    r1 = lambda v: v.reshape(1, -1).astype(F32)
    pad_l = lambda v: jnp.pad(r1(v), ((0, 0), (0, LANES - v.size)))
    late = None if shards is None else [shards[n] for n in LATE]
    if shards is None:
        h0, h0b, h0bt = _ln_emb_fwd(x, r1(W["ln_emb_g"]), r1(W["ln_emb_b"]))
    else:
        h0, h0b, h0bt, *gathered = _ln_emb_fwd(x, r1(W["ln_emb_g"]), r1(W["ln_emb_b"]),
                                               exchange=([shards[n] for n in EARLY], ("gather",) * len(EARLY)))
        W = dict(W, **{n: a if n == "w_in" else _unstack_shards(a, BY_COLS[n]) for n, a in zip(EARLY, gathered)})
    w_main, w_dt = _w_in_to_main(W["w_in"])
    scw, scb = W["ssm_conv_w"], r1(W["ssm_conv_b"])
    wx, wb, bx, bb = scw[:, :1024], scw[:, 1024:], scb[:, :1024], scb[:, 1024:]
    dt_bias, alog = pad_l(W["dt_bias"]), pad_l(W["a_log"])
    dskip_row = jnp.repeat(W["d_skip"].reshape(-1), HEAD).reshape(1, -1)
    norm_g = r1(W["ssm_norm_g"])
    bglu = r1(W["b_glu"])
    ba, bbg = bglu[:, :1024], bglu[:, 1024:]
    ccw, ccb, clg, clb = W["conf_conv_w"], r1(W["conf_conv_b"]), r1(W["conf_ln_g"]), r1(W["conf_ln_b"])

    if late is None:
        proj = _mm(h0b, w_main, "nn", "in_proj")
    else:
        proj, *gathered = _mm(h0b, w_main, "nn", "in_proj", exchange=(late, ("gather",) * len(LATE)))
        W = dict(W, **{n: _unstack_shards(a, BY_COLS[n]) for n, a in zip(LATE, gathered)})
    dt_raw = _mm(h0b, w_dt, "nn", "in_proj_dt")
    xs_c, bc_c, dt = _ssd_pre_fwd(proj, dt_raw, wx, wb, bx, bb, dt_bias)
    mix, ypre, hprev, mixt = _ssd_fwd(xs_c, bc_c, dt, proj, alog, dskip_row, norm_g)
    u1, mix, mixt = _conf_fwd(mix, mixt, proj, ccw, ccb, clg, clb, ba, bbg)
    out, h1, h1b, h1bt = _out_proj_post1(mix, W["w_out"], h0, r1(W["b_out"]), r1(W["ln1_g"]), r1(W["ln1_b"]))
    pb = p.astype(BF16)
    dh1a, dgp, dple, loss, dln2g, dln2b = _ple_post2(h1b, W["w_ple_gate"], pb, W["w_ple_proj"], h1, tgt,
                                                      r1(W["ln2_g"]), r1(W["ln2_b"]))

    g = {}
    g["w_ple_proj"] = _mm(pb.T, dple, "nn", "d_ple_proj", out_dtype=BF16)
    g["w_ple_gate"] = _mm(h1bt, dgp, "nn", "d_ple_gate", out_dtype=BF16)
    dout, dh0a, dln1g, dln1b, dbout = _d_h1_post1_bwd(dgp, W["w_ple_gate"], dh1a, h0, out, r1(W["b_out"]),
                                                      r1(W["ln1_g"]))
    g["w_out"] = _mm(mixt, dout, "nn", "d_w_out", out_dtype=BF16)
    dmix = _mm(dout, W["w_out"], "nt", "d_mix")
    du1, dproj, dclg, dclb = _conf_bwd1(dmix, u1, proj, clg, clb)
    dproj, g["conf_conv_w"], dccb, dba, dbb = _conf_bwd2(dproj, proj, du1, ccw, ba, bbg)
    stack = lambda names: [_stack_shards(g[n], BY_COLS[n]) for n in names]
    dxs_c, dbc_c, ddtr, dproj, dng, ddsk, dalog, ddtb, *recv_a = _ssd_bwd(
        dproj, xs_c, bc_c, dt, dt_raw, dt_bias, proj, ypre, hprev, dmix, alog, dskip_row, norm_g,
        exchange=None if late is None else (stack(LATE), ("blocks",) * len(LATE)))
    dproj, dwx, dbx = _ssd_conv_bwd(dproj, proj, dxs_c, wx, bx, 1024, C_XS, "ssd_conv_bwd_x")
    dproj, dwb, dbb2 = _ssd_conv_bwd(dproj, proj, dbc_c, wb, bb, 512, C_BC, "ssd_conv_bwd_bc")
    g["ssm_conv_w"] = jnp.concatenate([dwx, dwb], axis=1)
    dw_dt = _mm(h0bt, ddtr, "nn", "d_w_dt", out_dtype=BF16)
    last_args = (dproj, w_main, ddtr, w_dt, dh0a, x, r1(W["ln_emb_g"]))
    if late is None:
        g["w_in"] = _w_in_blocks(_mm(h0bt, dproj, "nn", "d_w_in", out_dtype=BF16), dw_dt)
        grad_x, dlng, dlnb = _d_h0_ln_bwd(*last_args)
    else:
        hr = HEAD_ROWS
        head = _w_in_blocks(_mm(h0bt[:hr], dproj, "nn", "d_w_in_head", out_dtype=BF16), dw_dt[:hr])
        dw_rest, recv_head = _mm(h0bt[hr:], dproj, "nn", "d_w_in", out_dtype=BF16, exchange=([head], ("blocks",)))
        last = ("ssm_conv_w", "conf_conv_w")
        grad_x, dlng, dlnb, *recv_b = _d_h0_ln_bwd(
            *last_args, exchange=([_w_in_blocks(dw_rest, dw_dt[hr:])] + stack(last), ("blocks",) * 3))
        g["recv"] = dict(zip(LATE + ("w_in",) + last, recv_a + recv_b), w_in_head=recv_head)
    g["rows"] = [dlng, dlnb, dbx, dbb2, ddtb, dalog, ddsk, dng, dba, dbb, dccb, dclg, dclb, dbout, dln1g, dln1b,
                 dln2g, dln2b]
    return loss, grad_x, g


W_IN_SEGMENTS = ((0, 2048, 2048), (2048, 5120, 512), (2560, None, N_HEADS), (2576, 0, 2048), (4624, 4096, 1024))


def _w_in_to_main(shards):
    def pieces(p0, width):
        out, p = [], p0
        while p < p0 + width:
            d = p // COLS_PER_DEV
            hi = min(p0 + width, (d + 1) * COLS_PER_DEV)
            out.append(shards[d][:, p - d * COLS_PER_DEV:hi - d * COLS_PER_DEV])
            p = hi
        return out
    main = [s for s in sorted(W_IN_SEGMENTS, key=lambda s: -1 if s[1] is None else s[1]) if s[1] is not None]
    w_main = jnp.concatenate([q for p0, _, width in main for q in pieces(p0, width)], axis=1)
    w_dt = jnp.concatenate(pieces(2560, N_HEADS), axis=1)
    return w_main, jnp.pad(w_dt, ((0, 0), (0, LANES - N_HEADS)))


def _w_in_blocks(dw_main, dw_dt):
    blocks = []
    for d in range(N_DEV):
        lo_d, hi_d = d * COLS_PER_DEV, (d + 1) * COLS_PER_DEV
        parts = []
        for p0, m0, width in W_IN_SEGMENTS:
            lo, hi = max(lo_d, p0), min(hi_d, p0 + width)
            if lo < hi:
                parts.append(dw_dt[:, lo - p0:hi - p0] if m0 is None else dw_main[:, m0 + lo - p0:m0 + hi - p0])
        blocks.append(jnp.concatenate(parts, axis=1))
    return jnp.stack(blocks)


WEIGHTS = ['ln_emb_g', 'ln_emb_b', 'w_in', 'ssm_conv_w', 'ssm_conv_b', 'dt_bias', 'a_log', 'd_skip', 'ssm_norm_g',
           'b_glu', 'conf_conv_w', 'conf_conv_b', 'conf_ln_g', 'conf_ln_b', 'w_out', 'b_out', 'ln1_g', 'ln1_b',
           'w_ple_gate', 'w_ple_proj', 'ln2_g', 'ln2_b']
SHARDED = (("w_in", True), ("w_out", False), ("w_ple_gate", False), ("w_ple_proj", True), ("ssm_conv_w", True),
           ("conf_conv_w", True))
BY_COLS = dict(SHARDED)


def _stack_shards(a, by_cols):
    if by_cols:
        return a.reshape(a.shape[0], N_DEV, a.shape[1] // N_DEV).transpose(1, 0, 2)
    return a.reshape(N_DEV, a.shape[0] // N_DEV, a.shape[1])


def _unstack_shards(a, by_cols):
    if by_cols:
        return a.transpose(1, 0, 2).reshape(a.shape[1], N_DEV * a.shape[2])
    return a.reshape(N_DEV * a.shape[1], a.shape[2])


def kernel(x, p, ln_emb_g, ln_emb_b, w_in, ssm_conv_w, ssm_conv_b, dt_bias, a_log, d_skip, ssm_norm_g, b_glu, conf_conv_w, conf_conv_b, conf_ln_g, conf_ln_b, w_out, b_out, ln1_g, ln1_b, w_ple_gate, w_ple_proj, ln2_g, ln2_b, loss_target, m_ln_emb_g, m_ln_emb_b, m_w_in, m_ssm_conv_w, m_ssm_conv_b, m_dt_bias, m_a_log, m_d_skip, m_ssm_norm_g, m_b_glu, m_conf_conv_w, m_conf_conv_b, m_conf_ln_g, m_conf_ln_b, m_w_out, m_b_out, m_ln1_g, m_ln1_b, m_w_ple_gate, m_w_ple_proj, m_ln2_g, m_ln2_b, v_ln_emb_g, v_ln_emb_b, v_w_in, v_ssm_conv_w, v_ssm_conv_b, v_dt_bias, v_a_log, v_d_skip, v_ssm_norm_g, v_b_glu, v_conf_conv_w, v_conf_conv_b, v_conf_ln_g, v_conf_ln_b, v_w_out, v_b_out, v_ln1_g, v_ln1_b, v_w_ple_gate, v_w_ple_proj, v_ln2_g, v_ln2_b):
    loc = dict(locals())
    w = {n: loc[n] for n in WEIGHTS}
    m = {n: loc["m_" + n] for n in WEIGHTS}
    v = {n: loc["v_" + n] for n in WEIGHTS}
    sharded = [n for n, _ in SHARDED]

    shards = {n: w[n][0].astype(BF16) if n.startswith("w_") else w[n][0] for n in sharded}
    W = {n: w[n].reshape(-1) for n in SMALL}
    loss, grad_x, g = _local_step(x[0], p[0, 0], loss_target[0], W, shards=shards)
    (recv_small,) = _exchange([_pack_small(g["rows"], loss)], ("all",), "small_exchange")

    grads, delta, new_m, new_v = {}, {}, {}, {}
    for n in sharded:
        grads[n], delta[n], new_m[n], new_v[n] = _sum8_adamw(
            g["recv"][n], w[n], m[n], v[n], "adamw_" + n, head=g["recv"]["w_in_head"] if n == "w_in" else None)
    two_d = lambda d: {n: d[n].reshape(1, -1) for n in SMALL}
    *small, loss = _small_update(recv_small, two_d(w), two_d(m), two_d(v))
    for dst, res in zip((grads, delta, new_m, new_v), small):
        for n in SMALL:
            dst[n] = res[n].reshape(w[n].shape)
    return (loss[0, 0], grad_x[None], *[grads[n] for n in WEIGHTS], *[delta[n] for n in WEIGHTS],
            *[new_m[n] for n in WEIGHTS], *[new_v[n] for n in WEIGHTS])
```

```python
import functools

import numpy as np
import jax
import jax.numpy as jnp
from jax import lax
from jax.experimental import pallas as pl
from jax.experimental.pallas import tpu as pltpu

F32, BF16 = jnp.float32, jnp.bfloat16
S = jax.ShapeDtypeStruct

N_DEV = 8
D = 1024
D_PLE = 256
D_SSM = 1024
D_CONF = 1024
N_HEADS = 16
HEAD = 64
N_STATE = 128
CHUNK = 128
K_SSM = 4
K_CONF = 31
D_IN = 5648
COLS_PER_DEV = D_IN // N_DEV
LN_EPS = 1e-5
RMS_EPS = 1e-5
ALPHA = 2.0 ** 0.25
LANES = 128
TB = 512
RG = 32
ROW_UNROLL = 4
HALO_SSM = 8
HALO_CONF = 32
VMEM_LIMIT = 56 * 1024 * 1024

ADAM_LR, ADAM_B1, ADAM_B2, ADAM_EPS, ADAM_WD, ADAM_STEP = 0.001, 0.9, 0.999, 1e-08, 0.01, 10

C_GLUA, C_GLUB, C_XS, C_Z, C_CG = 0, 1, 2, 3, 4
C_BC = 10
N_MAIN = 5632


def _params(sem, vmem=VMEM_LIMIT):
    return pltpu.CompilerParams(dimension_semantics=sem, vmem_limit_bytes=vmem)


def _row(tb, n, col=0):
    return pl.BlockSpec((tb, n), lambda i: (i, col))


def _colt(n, tb, row=0):
    return pl.BlockSpec((n, tb), lambda i: (row, i))


def _full(shape):
    return pl.BlockSpec(shape, lambda i: (0,) * len(shape))


_ANY = pl.BlockSpec(memory_space=pl.ANY)


def _prev(tb, halo, n, col=0):
    r = tb // halo
    return pl.BlockSpec((halo, n), lambda i: (jnp.maximum(i * r - 1, 0), col))


def _next(tb, halo, n, nt, col=0):
    r = tb // halo
    return pl.BlockSpec((halo, n), lambda i: (jnp.minimum((i + 1) * r, nt * r - 1), col))


def _row_loop(tb, fn):
    def it(r, c):
        fn(pl.ds(pl.multiple_of(r * RG, RG), RG))
        return c
    lax.fori_loop(0, tb // RG, it, 0, unroll=ROW_UNROLL)


def _col_loop(n, fn):
    def it(j, c):
        fn(pl.ds(pl.multiple_of(j * LANES, LANES), LANES))
        return c
    lax.fori_loop(0, n // LANES, it, 0)


def _sigmoid(x):
    return 1.0 / (1.0 + jnp.exp(-x))


def _dsilu(x, s):
    return s * (1.0 + x * (1.0 - s))


def _ln_stats(v):
    mu = jnp.mean(v, axis=-1, keepdims=True)
    c = v - mu
    r = lax.rsqrt(jnp.mean(c * c, axis=-1, keepdims=True) + LN_EPS)
    return c * r, r


def _ln_bwd(dy, xhat, r, g):
    dxh = dy * g
    dv = r * (dxh - jnp.mean(dxh, axis=-1, keepdims=True) - xhat * jnp.mean(dxh * xhat, axis=-1, keepdims=True))
    return dv, jnp.sum(dy * xhat, axis=0, keepdims=True), jnp.sum(dy, axis=0, keepdims=True)


def _dot(a, b, dims=((1,), (0,))):
    return lax.dot_general(a.astype(BF16), b.astype(BF16), (dims, ((), ())), preferred_element_type=F32)


_NT = ((1,), (1,))
_TN = ((0,), (0,))


def _split3(x):
    hi = x.astype(BF16)
    r = x - hi.astype(F32)
    mid = r.astype(BF16)
    return hi, mid, (r - mid.astype(F32)).astype(BF16)


def _dot_sel_b(a, b, dims=((1,), (0,))):
    hi, mid, lo = _split3(a)
    return (_dot(lo, b, dims) + _dot(mid, b, dims)) + _dot(hi, b, dims)


def _dot_sel_a(a, b, dims=((1,), (0,))):
    hi, mid, lo = _split3(b)
    return (_dot(a, lo, dims) + _dot(a, mid, dims)) + _dot(a, hi, dims)


def _mm(a, b, mode, name, out_dtype=F32, add=None, tm=1024, tn=None, tk=1024, exchange=None):
    if mode == "nn":
        (M, K), N = a.shape, b.shape[1]
    elif mode == "tn":
        (K, M), N = a.shape, b.shape[1]
    else:
        (M, K), N = a.shape, b.shape[0]
    if tn is None:
        tn = next(t for t in (1024, 1408, 512, 256, LANES) if N % t == 0)
    tm, tn, tk = min(tm, M), min(tn, N), min(tk, K)
    assert M % tm == 0 and N % tn == 0 and K % tk == 0, (name, M, N, K)
    grid = (M // tm, N // tn, K // tk)
    nk = grid[2]
    dims = {"nn": ((1,), (0,)), "tn": _TN, "nt": _NT}[mode]
    n_in = 2 + (add is not None)
    xbufs, kinds = exchange if exchange is not None else ((), ())
    nx = len(xbufs)

    def body(*refs):
        a_ref, b_ref = refs[:2]
        o_ref = refs[n_in + nx]
        acc = refs[n_in + 2 * nx + 1]
        i, j, k = pl.program_id(0), pl.program_id(1), pl.program_id(2)
        if nx:
            start, finish = _exchange_plan(refs[n_in:n_in + nx], refs[n_in + nx + 1:n_in + 2 * nx + 1], kinds,
                                           *refs[n_in + 2 * nx + 2:])
            pl.when((i == 0) & (j == 0) & (k == 0))(start)

        d = _dot(a_ref[...], b_ref[...], dims)

        def write_out(r):
            if add is not None:
                r = r + refs[2][...]
            o_ref[...] = r.astype(out_dtype)

        if nk == 1:
            write_out(d)
        else:
            @pl.when(k == 0)
            def _():
                acc[...] = d

            @pl.when((k > 0) & (k < nk - 1))
            def _():
                acc[...] += d

            @pl.when(k == nk - 1)
            def _():
                write_out(acc[...] + d)

        if nx:
            pl.when((i == grid[0] - 1) & (j == grid[1] - 1) & (k == nk - 1))(finish)

    a_spec = pl.BlockSpec((tk, tm), lambda i, j, k: (k, i)) if mode == "tn" else pl.BlockSpec((tm, tk), lambda i, j, k: (i, k))
    b_spec = pl.BlockSpec((tn, tk), lambda i, j, k: (j, k)) if mode == "nt" else pl.BlockSpec((tk, tn), lambda i, j, k: (k, j))
    o_spec = pl.BlockSpec((tm, tn), lambda i, j, k: (i, j))
    ins, specs = [a, b], [a_spec, b_spec]
    if add is not None:
        ins.append(add)
        specs.append(o_spec)
    acc_spec = pltpu.VMEM((tm, tn) if nk > 1 else (8, LANES), F32)
    if not nx:
        return pl.pallas_call(
            body, name=name, grid=grid, in_specs=specs, out_specs=o_spec,
            out_shape=S((M, N), out_dtype), scratch_shapes=[acc_spec],
            compiler_params=_params(("parallel", "parallel", "arbitrary")))(*ins)
    return pl.pallas_call(
        body, name=name, grid=grid, in_specs=specs + [_ANY] * nx, out_specs=[o_spec] + [_ANY] * nx,
        out_shape=[S((M, N), out_dtype)] + _exchange_shapes(xbufs, kinds),
        scratch_shapes=[acc_spec] + _exchange_sems(nx),
        compiler_params=_params(("arbitrary", "arbitrary", "arbitrary")))(*ins, *xbufs)


def _ln_emb_fwd(x, g, b, exchange=None):
    T = x.shape[0]

    nt = T // TB
    xbufs, kinds = exchange if exchange is not None else ((), ())
    nx = len(xbufs)

    def body(*refs):
        x_ref, g_ref, b_ref = refs[:3]
        h_ref, hb_ref, hbt_ref = refs[3 + nx:6 + nx]
        i = pl.program_id(0)
        if nx:
            start, finish = _exchange_plan(refs[3:3 + nx], refs[6 + nx:6 + 2 * nx], kinds, *refs[6 + 2 * nx:])
            pl.when(i == 0)(start)

        def rows(rs):
            xh, _ = _ln_stats(x_ref[rs, :])
            h = xh * g_ref[...] + b_ref[...]
            h_ref[rs, :] = h
            hb_ref[rs, :] = h.astype(BF16)
        _row_loop(TB, rows)
        hbt_ref[...] = hb_ref[...].T
        if nx:
            pl.when(i == nt - 1)(finish)

    return pl.pallas_call(
        body, name="ln_emb_fwd", grid=(nt,),
        in_specs=[_row(TB, D), _full((1, D)), _full((1, D))] + [_ANY] * nx,
        out_specs=[_row(TB, D), _row(TB, D), _colt(D, TB)] + [_ANY] * nx,
        out_shape=[S((T, D), F32), S((T, D), BF16), S((D, T), BF16)] + _exchange_shapes(xbufs, kinds),
        scratch_shapes=_exchange_sems(nx) if nx else [],
        compiler_params=_params(("arbitrary",)))(x, g, b, *xbufs)


def _out_proj_post1(mix, w_out, h0, b_out, g, b, tm=512, tk=1024):
    T, K = mix.shape
    tm = min(tm, T)
    nk = K // tk
    assert T % tm == 0 and K % tk == 0 and nk >= 2

    def body(mix_ref, w_ref, h0_ref, bo_ref, g_ref, b_ref, out_ref, h_ref, hb_ref, hbt_ref, acc):
        k = pl.program_id(1)
        d = _dot(mix_ref[...], w_ref[...])

        @pl.when(k == 0)
        def _():
            acc[...] = d

        @pl.when((k > 0) & (k < nk - 1))
        def _():
            acc[...] += d

        @pl.when(k == nk - 1)
        def _():
            out_ref[...] = acc[...] + d

            def rows(rs):
                xh, _ = _ln_stats(ALPHA * h0_ref[rs, :] + out_ref[rs, :] + bo_ref[...])
                h = xh * g_ref[...] + b_ref[...]
                h_ref[rs, :] = h
                hb_ref[rs, :] = h.astype(BF16)
            _row_loop(tm, rows)
            hbt_ref[...] = hb_ref[...].T

    rowt = lambda n: pl.BlockSpec((tm, n), lambda i, k: (i, 0))
    const = pl.BlockSpec((1, D), lambda i, k: (0, 0))
    return pl.pallas_call(
        body, name="out_proj_post1", grid=(T // tm, nk),
        in_specs=[pl.BlockSpec((tm, tk), lambda i, k: (i, k)), pl.BlockSpec((tk, D), lambda i, k: (k, 0)), rowt(D),
                  const, const, const],
        out_specs=[rowt(D), rowt(D), rowt(D), pl.BlockSpec((D, tm), lambda i, k: (0, i))],
        out_shape=[S((T, D), F32), S((T, D), F32), S((T, D), BF16), S((D, T), BF16)],
        scratch_shapes=[pltpu.VMEM((tm, D), F32)],
        compiler_params=_params(("parallel", "arbitrary")))(mix, w_out, h0, b_out, g, b)


def _ple_post2(h1b, w_gate, pb, w_proj, h1, tgt, g, b, tm=512):
    T = h1.shape[0]
    tm = min(tm, T)
    assert T % tm == 0

    def body(h1b_ref, wg_ref, pb_ref, wp_ref, h1_ref, tgt_ref, g_ref, b_ref,
             dh1_ref, dgp_ref, dple_ref, loss_ref, dg_ref, db_ref, gp_ref, ple_ref):
        @pl.when(pl.program_id(0) == 0)
        def _():
            loss_ref[...] = jnp.zeros_like(loss_ref)
            dg_ref[...] = jnp.zeros_like(dg_ref)
            db_ref[...] = jnp.zeros_like(db_ref)

        gp_ref[...] = _dot(h1b_ref[...], wg_ref[...])
        ple_ref[...] = _dot(pb_ref[...], wp_ref[...])

        def rows(rs):
            gate = _sigmoid(gp_ref[rs, :])
            ple = ple_ref[rs, :]
            xh, r = _ln_stats(ALPHA * h1_ref[rs, :] + gate * ple)
            err = xh * g_ref[...] + b_ref[...] - tgt_ref[rs, :]
            loss_ref[...] += 0.5 * jnp.sum(jnp.mean(err * err, axis=-1, keepdims=True), axis=0, keepdims=True)
            dv, dg, db = _ln_bwd(err * (1.0 / D), xh, r, g_ref[...])
            dg_ref[...] += dg
            db_ref[...] += db
            dh1_ref[rs, :] = ALPHA * dv
            dgp_ref[rs, :] = (dv * ple * gate * (1.0 - gate)).astype(BF16)
            dple_ref[rs, :] = (dv * gate).astype(BF16)
        _row_loop(tm, rows)

    return pl.pallas_call(
        body, name="ple_post2", grid=(T // tm,),
        in_specs=[_row(tm, D), _full((D, D)), _row(tm, D_PLE), _full((D_PLE, D)), _row(tm, D), _row(tm, D),
                  _full((1, D)), _full((1, D))],
        out_specs=[_row(tm, D)] * 3 + [_full((8, LANES)), _full((1, D)), _full((1, D))],
        out_shape=[S((T, D), F32), S((T, D), BF16), S((T, D), BF16), S((8, LANES), F32), S((1, D), F32), S((1, D), F32)],
        scratch_shapes=[pltpu.VMEM((tm, D), F32), pltpu.VMEM((tm, D), F32)],
        compiler_params=_params(("arbitrary",)))(h1b, w_gate, pb, w_proj, h1, tgt, g, b)


def _d_h1_post1_bwd(dgp, w_gate, dh1a, h0, out, b_out, g, tm=512):
    T = h0.shape[0]
    tm = min(tm, T)
    assert T % tm == 0

    def body(dgp_ref, wg_ref, da_ref, h0_ref, out_ref, bo_ref, g_ref, dout_ref, dh0_ref, dg_ref, db_ref, dbo_ref, dh1):
        @pl.when(pl.program_id(0) == 0)
        def _():
            dg_ref[...] = jnp.zeros_like(dg_ref)
            db_ref[...] = jnp.zeros_like(db_ref)
            dbo_ref[...] = jnp.zeros_like(dbo_ref)

        dh1[...] = da_ref[...] + _dot(dgp_ref[...], wg_ref[...], _NT)

        def rows(rs):
            xh, r = _ln_stats(ALPHA * h0_ref[rs, :] + out_ref[rs, :] + bo_ref[...])
            dv, dg, db = _ln_bwd(dh1[rs, :], xh, r, g_ref[...])
            dg_ref[...] += dg
            db_ref[...] += db
            dbo_ref[...] += jnp.sum(dv, axis=0, keepdims=True)
            dout_ref[rs, :] = dv.astype(BF16)
            dh0_ref[rs, :] = ALPHA * dv
        _row_loop(tm, rows)

    return pl.pallas_call(
        body, name="d_h1_post1_bwd", grid=(T // tm,),
        in_specs=[_row(tm, D), _full((D, D))] + [_row(tm, D)] * 3 + [_full((1, D))] * 2,
        out_specs=[_row(tm, D)] * 2 + [_full((1, D))] * 3,
        out_shape=[S((T, D), BF16), S((T, D), F32)] + [S((1, D), F32)] * 3,
        scratch_shapes=[pltpu.VMEM((tm, D), F32)],
        compiler_params=_params(("arbitrary",)))(dgp, w_gate, dh1a, h0, out, b_out, g)


def _d_h0_ln_bwd(dproj, w_main, ddtr, w_dt, dh0a, x, g, exchange=None, tm=1024, tk=1408):
    T, K = dproj.shape
    tm = min(tm, T)
    assert T % tm == 0 and K % tk == 0
    ni, nk = T // tm, K // tk
    xbufs, kinds = exchange if exchange is not None else ((), ())
    nx = len(xbufs)

    def body(*refs):
        dp_ref, w_ref, dt_ref, wdt_ref, da_ref, x_ref, g_ref = refs[:7]
        dx_ref, dg_ref, db_ref = refs[7 + nx:10 + nx]
        acc = refs[10 + 2 * nx]
        i, k = pl.program_id(0), pl.program_id(1)
        if nx:
            start, finish = _exchange_plan(refs[7:7 + nx], refs[10 + nx:10 + 2 * nx], kinds, *refs[11 + 2 * nx:])
            pl.when((i == 0) & (k == 0))(start)

        @pl.when((i == 0) & (k == 0))
        def _():
            dg_ref[...] = jnp.zeros_like(dg_ref)
            db_ref[...] = jnp.zeros_like(db_ref)

        d = _dot(dp_ref[...], w_ref[...], _NT)

        @pl.when(k == 0)
        def _():
            acc[...] = da_ref[...] + _dot(dt_ref[...], wdt_ref[...], _NT) + d

        @pl.when(k > 0)
        def _():
            acc[...] += d

        @pl.when(k == nk - 1)
        def _():
            def rows(rs):
                xh, r = _ln_stats(x_ref[rs, :])
                dv, dg, db = _ln_bwd(acc[rs, :], xh, r, g_ref[...])
                dg_ref[...] += dg
                db_ref[...] += db
                dx_ref[rs, :] = dv
            _row_loop(tm, rows)

        if nx:
            pl.when((i == ni - 1) & (k == nk - 1))(finish)

    rowt = lambda n: pl.BlockSpec((tm, n), lambda i, k: (i, 0))
    const = lambda shape: pl.BlockSpec(shape, lambda i, k: (0, 0))
    return pl.pallas_call(
        body, name="d_h0_ln_bwd", grid=(ni, nk),
        in_specs=[pl.BlockSpec((tm, tk), lambda i, k: (i, k)), pl.BlockSpec((D, tk), lambda i, k: (0, k)),
                  rowt(LANES), const((D, LANES)), rowt(D), rowt(D), const((1, D))] + [_ANY] * nx,
        out_specs=[rowt(D), const((1, D)), const((1, D))] + [_ANY] * nx,
        out_shape=[S((T, D), F32), S((1, D), F32), S((1, D), F32)] + _exchange_shapes(xbufs, kinds),
        scratch_shapes=[pltpu.VMEM((tm, D), F32)] + (_exchange_sems(nx) if nx else []),
        compiler_params=_params(("arbitrary", "arbitrary")))(dproj, w_main, ddtr, w_dt, dh0a, x, g, *xbufs)


def _softplus(x):
    return jnp.maximum(x, 0.0) + jnp.log1p(jnp.exp(-jnp.abs(x)))


def _ssd_pre_fwd(proj, dt_raw, wx, wb, bx, bb, dt_bias):
    T = proj.shape[0]
    H = HALO_SSM

    def body(xs_ref, xsp_ref, bc_ref, bcp_ref, dtr_ref, wx_ref, wb_ref, bx_ref, bb_ref, dtb_ref,
             xso_ref, bco_ref, dto_ref, extx, extb):
        first = pl.program_id(0) == 0

        def conv(t_ref, p_ref, w_ref, b_ref, o_ref, ext, n):
            def blk(cols):
                ext[0:H, cols] = jnp.where(first, 0.0, p_ref[:, cols])
                ext[H:, cols] = t_ref[:, cols]
                for r0 in range(0, TB, 64):
                    acc = jnp.broadcast_to(b_ref[:, cols], (64, LANES))
                    for k in range(K_SSM):
                        acc = acc + w_ref[k:k + 1, cols] * ext[pl.ds(r0 + H - (K_SSM - 1) + k, 64), cols]
                    o_ref[pl.ds(r0, 64), cols] = acc * _sigmoid(acc)
            _col_loop(n, blk)

        conv(xs_ref, xsp_ref, wx_ref, bx_ref, xso_ref, extx, D_SSM)
        conv(bc_ref, bcp_ref, wb_ref, bb_ref, bco_ref, extb, 512)
        dto_ref[...] = _softplus(dtr_ref[...] + dtb_ref[...])

    return pl.pallas_call(
        body, name="ssd_pre_fwd", grid=(T // TB,),
        in_specs=[_row(TB, 1024, C_XS), _prev(TB, H, 1024, C_XS), _row(TB, 512, C_BC), _prev(TB, H, 512, C_BC),
                  _row(TB, LANES), _full((K_SSM, 1024)), _full((K_SSM, 512)), _full((1, 1024)), _full((1, 512)),
                  _full((1, LANES))],
        out_specs=[_row(TB, 1024), _row(TB, 512), _row(TB, LANES)],
        out_shape=[S((T, 1024), F32), S((T, 512), F32), S((T, LANES), F32)],
        scratch_shapes=[pltpu.VMEM((H + TB, 1024), F32), pltpu.VMEM((H + TB, 512), F32)],
        compiler_params=_params(("parallel",)))(proj, proj, proj, proj, dt_raw, wx, wb, bx, bb, dt_bias)


def _ssd_conv_bwd(dproj, proj, d_c, w, b, n, col, name):
    T = proj.shape[0]
    nt = T // TB
    H = HALO_SSM
    R = TB + H

    def body(dproj_ref, t_ref, p_ref, n_ref, d_ref, dn_ref, w_ref, b_ref, o_ref, dw_ref, dbias_ref, ext, dp):
        i = pl.program_id(0)
        first, last = i == 0, i == nt - 1

        @pl.when(first)
        def _():
            dw_ref[...] = jnp.zeros_like(dw_ref)
            dbias_ref[...] = jnp.zeros_like(dbias_ref)

        def blk(cols):
            ext[0:H, cols] = jnp.where(first, 0.0, p_ref[:, cols])
            ext[H:H + TB, cols] = t_ref[:, cols]
            ext[H + TB:, cols] = n_ref[:, cols]
            pre = jnp.broadcast_to(b_ref[:, cols], (R, LANES))
            for k in range(K_SSM):
                pre = pre + w_ref[k:k + 1, cols] * ext[pl.ds(H - (K_SSM - 1) + k, R), cols]
            s = _sigmoid(pre)
            ds = _dsilu(pre, s)
            dp[0:TB, cols] = d_ref[:, cols] * ds[0:TB]
            dp[TB:, cols] = jnp.where(last, 0.0, dn_ref[:, cols] * ds[TB:])
            dpt = dp[0:TB, cols]
            dbias_ref[:, cols] += jnp.sum(dpt, axis=0, keepdims=True)
            acc = jnp.zeros((TB, LANES), F32)
            for k in range(K_SSM):
                dw_ref[k:k + 1, cols] += jnp.sum(dpt * ext[pl.ds(H - (K_SSM - 1) + k, TB), cols], axis=0, keepdims=True)
                acc = acc + w_ref[k:k + 1, cols] * dp[pl.ds(K_SSM - 1 - k, TB), cols]
            o_ref[:, cols] = acc.astype(BF16)
        _col_loop(n, blk)

    return pl.pallas_call(
        body, name=name, grid=(nt,),
        in_specs=[_ANY, _row(TB, n, col), _prev(TB, H, n, col), _next(TB, H, n, nt, col),
                  _row(TB, n), _next(TB, H, n, nt), _full((K_SSM, n)), _full((1, n))],
        out_specs=[_row(TB, n, col), _full((K_SSM, n)), _full((1, n))],
        out_shape=[S(dproj.shape, BF16), S((K_SSM, n), F32), S((1, n), F32)],
        input_output_aliases={0: 0},
        scratch_shapes=[pltpu.VMEM((H + TB + H, n), F32), pltpu.VMEM((R, n), F32)],
        compiler_params=_params(("arbitrary",)))(dproj, proj, proj, proj, d_c, d_c, w, b)


def _ssd_consts():
    ex = np.zeros((LANES, D_SSM), np.float32)
    for h in range(N_HEADS):
        ex[h, h * HEAD:(h + 1) * HEAD] = 1.0
    tri = np.tril(np.ones((CHUNK, CHUNK), np.float32))
    return jnp.asarray(ex), jnp.asarray(ex.T.copy()), jnp.asarray(tri), jnp.asarray(tri.T.copy())


def _ssd_common(xs, dt, alog_ref, ex_ref, tri_ref):
    lane = lax.broadcasted_iota(jnp.int32, (1, LANES), 1)
    a = jnp.where(lane < N_HEADS, -jnp.exp(alog_ref[...]), 0.0)
    A = _dot_sel_a(tri_ref[...], dt * a)
    ex = ex_ref[...]
    Aex = _dot_sel_b(A, ex)
    dtex = _dot_sel_b(dt, ex)
    expA = jnp.exp(Aex)
    dec = jnp.exp(Aex[CHUNK - 1:CHUNK, :] - Aex)
    cd = _dot_sel_a(ex, jnp.broadcast_to(jnp.exp(A.T[:, CHUNK - 1:CHUNK]), (LANES, LANES)), _TN)
    return a, A, dtex, expA, dec, cd


def _decay_mask():
    sub = lax.broadcasted_iota(jnp.int32, (CHUNK, CHUNK), 0)
    lane = lax.broadcasted_iota(jnp.int32, (CHUNK, CHUNK), 1)
    return sub, lane, sub >= lane


def _ssd_fwd(xs_c, bc_c, dt, proj, alog, dskip_row, norm_g):
    T = xs_c.shape[0]
    nc = T // CHUNK
    ex, _, tri, _ = _ssd_consts()

    def body(xs_ref, bc_ref, dt_ref, z_ref, alog_ref, dsk_ref, ng_ref, ex_ref, tri_ref,
             ys_ref, ypre_ref, hprev_ref, yst_ref, Hs, ybuf):
        @pl.when(pl.program_id(0) == 0)
        def _():
            Hs[...] = jnp.zeros_like(Hs)

        hprev_ref[0] = Hs[...]
        xs, dt = xs_ref[...], dt_ref[...]
        a, A, dtex, expA, dec, cd = _ssd_common(xs, dt, alog_ref, ex_ref, tri_ref)
        AT = A.T
        xdt = xs * dtex
        xdec = xdt * dec
        _, _, causal = _decay_mask()
        for g in range(2):
            gs = slice(g * 512, (g + 1) * 512)
            B = bc_ref[:, g * N_STATE:(g + 1) * N_STATE]
            C = bc_ref[:, 256 + g * N_STATE:256 + (g + 1) * N_STATE]
            cb = _dot(C, B, _NT)
            Hg = Hs[gs, :]
            yoff = _dot(C, Hg, _NT) * expA[:, gs]
            for j in range(8):
                h = g * 8 + j
                hs = slice(h * HEAD, (h + 1) * HEAD)
                L = jnp.exp(jnp.where(causal, A[:, h:h + 1] - AT[h:h + 1, :], -1e30))
                ybuf[:, hs] = _dot(cb * L, xdt[:, hs]) + yoff[:, j * HEAD:(j + 1) * HEAD]
            Hs[gs, :] = cd[gs, :] * Hg + _dot(xdec[:, gs], B, _TN)
        ypre = ybuf[...] + dsk_ref[...] * xs
        ypre_ref[...] = ypre
        z = z_ref[...]
        yz = ypre * (z * _sigmoid(z))
        for g in range(2):
            gs = slice(g * 512, (g + 1) * 512)
            v = yz[:, gs]
            r = lax.rsqrt(jnp.mean(v * v, axis=-1, keepdims=True) + RMS_EPS)
            ys_ref[:, gs] = (v * r * ng_ref[:, gs]).astype(BF16)
        yst_ref[...] = ys_ref[...].T

    return pl.pallas_call(
        body, name="ssd_fwd", grid=(nc,),
        in_specs=[_row(CHUNK, 1024), _row(CHUNK, 512), _row(CHUNK, LANES), _row(CHUNK, 1024, C_Z),
                  _full((1, LANES)), _full((1, 1024)), _full((1, 1024)), _full((LANES, 1024)), _full((CHUNK, CHUNK))],
        out_specs=[_row(CHUNK, 1024), _row(CHUNK, 1024), pl.BlockSpec((1, 1024, N_STATE), lambda c: (c, 0, 0)),
                   _colt(1024, CHUNK)],
        out_shape=[S((T, 2048), BF16), S((T, 1024), F32), S((nc, 1024, N_STATE), F32), S((2048, T), BF16)],
        scratch_shapes=[pltpu.VMEM((1024, N_STATE), F32), pltpu.VMEM((CHUNK, 1024), F32)],
        compiler_params=_params(("arbitrary",)))(xs_c, bc_c, dt, proj, alog, dskip_row, norm_g, ex, tri)


def _ssd_bwd(dproj, xs_c, bc_c, dt, dt_raw, dt_bias, proj, ypre, hprev, dmix, alog, dskip_row, norm_g, exchange=None):
    T = xs_c.shape[0]
    nc = T // CHUNK
    ex, ext, tri, triu = _ssd_consts()
    rev = lambda n, col=0: pl.BlockSpec((CHUNK, n), lambda c: (nc - 1 - c, col))
    xbufs, kinds = exchange if exchange is not None else ((), ())
    nx = len(xbufs)
    N_IN, N_OUT = 17, 8

    def body(*refs):
        (dproj_ref, xs_ref, bc_ref, dt_ref, dtr_ref, dtb_ref, z_ref, ypre_ref, hprev_ref, dys_ref, alog_ref, dsk_ref,
         ng_ref, ex_ref, ext_ref, tri_ref, triu_ref) = refs[:N_IN]
        (dxs_ref, dbc_ref, ddt_ref, dz_ref, dng_ref, ddsk_ref, dalog_ref,
         ddtb_ref) = refs[N_IN + nx:N_IN + nx + N_OUT]
        dHs, dxbuf, dskacc = refs[N_IN + N_OUT + 2 * nx:N_IN + N_OUT + 2 * nx + 3]
        c = pl.program_id(0)
        if nx:
            start, finish = _exchange_plan(refs[N_IN:N_IN + nx], refs[N_IN + nx + N_OUT:N_IN + N_OUT + 2 * nx], kinds,
                                           *refs[N_IN + N_OUT + 2 * nx + 3:])
            pl.when(c == 0)(start)

        @pl.when(c == 0)
        def _():
            dHs[...] = jnp.zeros_like(dHs)
            dng_ref[...] = jnp.zeros_like(dng_ref)
            dalog_ref[...] = jnp.zeros_like(dalog_ref)
            ddtb_ref[...] = jnp.zeros_like(ddtb_ref)
            dskacc[...] = jnp.zeros_like(dskacc)

        xs, dt, z, ypre, dys = xs_ref[...], dt_ref[...], z_ref[...], ypre_ref[...], dys_ref[...]
        sg = _sigmoid(z)
        sz = z * sg
        yz = ypre * sz
        dyz_parts = []
        for g in range(2):
            gs = slice(g * 512, (g + 1) * 512)
            v = yz[:, gs]
            r = lax.rsqrt(jnp.mean(v * v, axis=-1, keepdims=True) + RMS_EPS)
            vn = v * r
            dng_ref[:, gs] += jnp.sum(dys[:, gs] * vn, axis=0, keepdims=True)
            dvn = dys[:, gs] * ng_ref[:, gs]
            dyz_parts.append(r * (dvn - vn * jnp.mean(dvn * vn, axis=-1, keepdims=True)))
        dyz = jnp.concatenate(dyz_parts, axis=1)
        dy = dyz * sz
        dz_ref[...] = (dyz * ypre * _dsilu(z, sg)).astype(BF16)
        dskacc[...] += jnp.sum(dy * xs, axis=0, keepdims=True)

        a, A, dtex, expA, dec, cd = _ssd_common(xs, dt, alog_ref, ex_ref, tri_ref)
        AT = A.T
        xdt = xs * dtex
        xdec = xdt * dec
        dye = dy * expA
        H = hprev_ref[0]
        dHn = dHs[...]
        sub, lane, causal = _decay_mask()
        dAc = jnp.zeros((CHUNK, LANES), F32)
        Rm = jnp.zeros((CHUNK, LANES), F32)
        yoff_parts, q_parts = [], []
        for g in range(2):
            gs = slice(g * 512, (g + 1) * 512)
            B = bc_ref[:, g * N_STATE:(g + 1) * N_STATE]
            C = bc_ref[:, 256 + g * N_STATE:256 + (g + 1) * N_STATE]
            cb = _dot(C, B, _NT)
            Hg, dHg = H[gs, :], dHn[gs, :]
            Q = _dot(B, dHg, _NT)
            yoff_parts.append(_dot(C, Hg, _NT) * expA[:, gs])
            q_parts.append(Q)
            dcb = jnp.zeros((CHUNK, CHUNK), F32)
            for j in range(8):
                h = g * 8 + j
                hs = slice(h * HEAD, (h + 1) * HEAD)
                L = jnp.exp(jnp.where(causal, A[:, h:h + 1] - AT[h:h + 1, :], -1e30))
                M = cb * L
                G = _dot(dy[:, hs], xdt[:, hs], _NT)
                dxbuf[:, hs] = _dot(M, dy[:, hs], _TN)
                dcb = dcb + G * L
                E = G * M
                dAc = jnp.where(lane == h, jnp.sum(E, axis=1, keepdims=True), dAc)
                Rm = jnp.where(sub == h, jnp.sum(E, axis=0, keepdims=True), Rm)
            dbc_ref[:, g * N_STATE:(g + 1) * N_STATE] = _dot(dcb, C, _TN) + _dot(xdec[:, gs], dHg)
            dbc_ref[:, 256 + g * N_STATE:256 + (g + 1) * N_STATE] = _dot(dcb, B) + _dot(dye[:, gs], Hg)
            dHs[gs, :] = cd[gs, :] * dHg + _dot(dye[:, gs], C, _TN)
        yoff = jnp.concatenate(yoff_parts, axis=1)
        Qd = jnp.concatenate(q_parts, axis=1) * dec
        dxdt = dxbuf[...] + Qd
        extm = ext_ref[...]
        red_s = _dot_sel_b(xdt * Qd, extm)
        dA = dAc - Rm.T + _dot_sel_b(dy * yoff, extm) - red_s
        hd = jnp.sum(_dot_sel_b(H * dHn, extm, _TN), axis=0, keepdims=True)
        last_add = jnp.sum(red_s, axis=0, keepdims=True) + jnp.exp(A[CHUNK - 1:CHUNK, :]) * hd
        dA = dA + jnp.where(sub == CHUNK - 1, last_add, 0.0)
        dadt = _dot_sel_a(triu_ref[...], dA)
        ddtr = (dadt * a + _dot_sel_b(dxdt * xs, extm)) * _sigmoid(dtr_ref[...] + dtb_ref[...])
        ddt_ref[...] = ddtr.astype(BF16)
        ddtb_ref[...] += jnp.sum(ddtr, axis=0, keepdims=True)
        dalog_ref[...] += jnp.sum(dadt * dt, axis=0, keepdims=True) * a
        dxs_ref[...] = dxdt * dtex + dsk_ref[...] * dy

        @pl.when(c == nc - 1)
        def _():
            ddsk_ref[...] = _dot_sel_b(jnp.broadcast_to(dskacc[...], (8, 1024)), extm)[0:1, :]

        if nx:
            pl.when(c == nc - 1)(finish)

    return pl.pallas_call(
        body, name="ssd_bwd", grid=(nc,),
        in_specs=[_ANY, rev(1024), rev(512), rev(LANES), rev(LANES), _full((1, LANES)), rev(1024, C_Z), rev(1024),
                  pl.BlockSpec((1, 1024, N_STATE), lambda c: (nc - 1 - c, 0, 0)), rev(1024, 0),
                  _full((1, LANES)), _full((1, 1024)), _full((1, 1024)),
                  _full((LANES, 1024)), _full((1024, LANES)), _full((CHUNK, CHUNK)), _full((CHUNK, CHUNK))] + [_ANY] * nx,
        out_specs=[rev(1024), rev(512), rev(LANES), rev(1024, C_Z), _full((1, 1024)), _full((1, LANES)),
                   _full((1, LANES)), _full((1, LANES))] + [_ANY] * nx,
        out_shape=[S((T, 1024), F32), S((T, 512), F32), S((T, LANES), BF16), S(dproj.shape, BF16),
                   S((1, 1024), F32), S((1, LANES), F32), S((1, LANES), F32), S((1, LANES), F32)]
        + _exchange_shapes(xbufs, kinds),
        input_output_aliases={0: 3},
        scratch_shapes=[pltpu.VMEM((1024, N_STATE), F32), pltpu.VMEM((CHUNK, 1024), F32), pltpu.VMEM((1, 1024), F32)]
        + (_exchange_sems(nx) if nx else []),
        compiler_params=_params(("arbitrary",)))(
            dproj, xs_c, bc_c, dt, dt_raw, dt_bias, proj, ypre, hprev, dmix, alog, dskip_row, norm_g, ex, ext, tri, triu,
            *xbufs)


def _shifted_copies(ext, ext8):
    n = ext8.shape[1]
    for r in range(8):
        ext8[r] = ext[pl.ds(r, n), :]


def _shifted(ext8, off, rows):
    return ext8[off % 8, pl.ds(off - off % 8, rows), :]


def _conf_fwd(mix, mixt, proj, w, cb, lg, lb, ba, bb):
    T = proj.shape[0]
    H = HALO_CONF

    def body(mix_ref, mixt_ref, ga_ref, gap_ref, gb_ref, gbp_ref, cg_ref, w_ref, cb_ref, lg_ref, lb_ref, ba_ref,
             bb_ref, u1_ref, yc_ref, yct_ref, ext, ext8):
        first = pl.program_id(0) == 0
        ext[H + TB:, :] = jnp.zeros((8, LANES), F32)

        def blk(cols):
            up = (gap_ref[:, cols] + ba_ref[:, cols]) * _sigmoid(gbp_ref[:, cols] + bb_ref[:, cols])
            ext[0:H, :] = jnp.where(first, 0.0, up)
            ext[H:H + TB, :] = (ga_ref[:, cols] + ba_ref[:, cols]) * _sigmoid(gb_ref[:, cols] + bb_ref[:, cols])
            _shifted_copies(ext, ext8)
            for r0 in range(0, TB, 64):
                acc = jnp.broadcast_to(cb_ref[:, cols], (64, LANES))
                for k in range(K_CONF):
                    acc = acc + w_ref[k:k + 1, cols] * _shifted(ext8, r0 + H - (K_CONF - 1) + k, 64)
                u1_ref[pl.ds(r0, 64), cols] = acc
        _col_loop(D_CONF, blk)

        def rows(rs):
            xh, _ = _ln_stats(u1_ref[rs, :])
            u2 = xh * lg_ref[...] + lb_ref[...]
            cg = cg_ref[rs, :]
            yc_ref[rs, :] = (u2 * _sigmoid(u2) * cg * _sigmoid(cg)).astype(BF16)
        _row_loop(TB, rows)
        yct_ref[...] = yc_ref[...].T

    return pl.pallas_call(
        body, name="conf_fwd", grid=(T // TB,),
        in_specs=[_ANY, _ANY, _row(TB, 1024, C_GLUA), _prev(TB, H, 1024, C_GLUA), _row(TB, 1024, C_GLUB),
                  _prev(TB, H, 1024, C_GLUB), _row(TB, 1024, C_CG), _full((K_CONF, 1024))] + [_full((1, 1024))] * 5,
        out_specs=[_row(TB, 1024), _row(TB, 1024, 1), _colt(1024, TB, 1)],
        out_shape=[S((T, 1024), F32), S((T, 2048), BF16), S((2048, T), BF16)],
        input_output_aliases={0: 1, 1: 2},
        scratch_shapes=[pltpu.VMEM((H + TB + 8, LANES), F32), pltpu.VMEM((8, H + TB, LANES), F32)],
        compiler_params=_params(("parallel",)))(mix, mixt, proj, proj, proj, proj, proj, w, cb, lg, lb, ba, bb)


def _conf_bwd1(dmix, u1, proj, lg, lb):
    T = u1.shape[0]

    def body(dy_ref, u1_ref, cg_ref, lg_ref, lb_ref, du1_ref, dcg_ref, dg_ref, db_ref):
        @pl.when(pl.program_id(0) == 0)
        def _():
            dg_ref[...] = jnp.zeros_like(dg_ref)
            db_ref[...] = jnp.zeros_like(db_ref)

        def rows(rs):
            xh, r = _ln_stats(u1_ref[rs, :])
            u2 = xh * lg_ref[...] + lb_ref[...]
            s2 = _sigmoid(u2)
            cg = cg_ref[rs, :]
            sc = _sigmoid(cg)
            dy = dy_ref[rs, :]
            dcg_ref[rs, :] = (dy * u2 * s2 * _dsilu(cg, sc)).astype(BF16)
            dv, dg, db = _ln_bwd(dy * cg * sc * _dsilu(u2, s2), xh, r, lg_ref[...])
            dg_ref[...] += dg
            db_ref[...] += db
            du1_ref[rs, :] = dv
        _row_loop(TB, rows)

    return pl.pallas_call(
        body, name="conf_bwd1", grid=(T // TB,),
        in_specs=[_row(TB, 1024, 1), _row(TB, 1024), _row(TB, 1024, C_CG), _full((1, 1024)), _full((1, 1024))],
        out_specs=[_row(TB, 1024), _row(TB, 1024, C_CG), _full((1, 1024)), _full((1, 1024))],
        out_shape=[S((T, 1024), F32), S((T, N_MAIN), BF16), S((1, 1024), F32), S((1, 1024), F32)],
        compiler_params=_params(("arbitrary",)))(dmix, u1, proj, lg, lb)


def _conf_bwd2(dproj, proj, du1, w, ba, bb):
    T = du1.shape[0]
    nt = T // TB
    H = HALO_CONF

    def body(dproj_ref, ga_ref, gap_ref, gb_ref, gbp_ref, du_ref, dun_ref, w_ref, ba_ref, bb_ref,
             dg_ref, dw_ref, dcb_ref, dba_ref, dbb_ref, ext, dext, ext8, dext8, dwacc):
        i = pl.program_id(0)
        first, last = i == 0, i == nt - 1

        @pl.when(first)
        def _():
            for r in (dcb_ref, dba_ref, dbb_ref, dwacc):
                r[...] = jnp.zeros_like(r)

        ext[H + TB:, :] = jnp.zeros((8, LANES), F32)
        dext[H + TB:, :] = jnp.zeros((8, LANES), F32)

        def blk(cols):
            cols_b = pl.ds(pl.multiple_of(cols.start + D_CONF, LANES), LANES)
            up = (gap_ref[:, cols] + ba_ref[:, cols]) * _sigmoid(gbp_ref[:, cols] + bb_ref[:, cols])
            ext[0:H, :] = jnp.where(first, 0.0, up)
            a = ga_ref[:, cols] + ba_ref[:, cols]
            sb = _sigmoid(gb_ref[:, cols] + bb_ref[:, cols])
            ext[H:H + TB, :] = a * sb
            du = du_ref[:, cols]
            dext[0:TB, :] = du
            dext[TB:TB + H, :] = jnp.where(last, 0.0, dun_ref[:, cols])
            _shifted_copies(ext, ext8)
            _shifted_copies(dext, dext8)
            dcb_ref[:, cols] += jnp.sum(du, axis=0, keepdims=True)
            for r0 in range(0, TB, 64):
                dur = du_ref[pl.ds(r0, 64), cols]
                acc = jnp.zeros((64, LANES), F32)
                for k in range(K_CONF):
                    prod = dur * _shifted(ext8, r0 + H - (K_CONF - 1) + k, 64)
                    dwacc[k * 8:(k + 1) * 8, cols] += prod.reshape(8, 8, LANES).sum(axis=0)
                    acc = acc + w_ref[k:k + 1, cols] * _shifted(dext8, r0 + K_CONF - 1 - k, 64)
                ar, sr = a[r0:r0 + 64], sb[r0:r0 + 64]
                da = acc * sr
                dbv = acc * ar * sr * (1.0 - sr)
                dg_ref[pl.ds(r0, 64), cols] = da.astype(BF16)
                dg_ref[pl.ds(r0, 64), cols_b] = dbv.astype(BF16)
                dba_ref[:, cols] += jnp.sum(da, axis=0, keepdims=True)
                dbb_ref[:, cols] += jnp.sum(dbv, axis=0, keepdims=True)
        _col_loop(D_CONF, blk)

        @pl.when(last)
        def _():
            dw_ref[...] = jnp.sum(dwacc[...].reshape(K_CONF, 8, D_CONF), axis=1)

    return pl.pallas_call(
        body, name="conf_bwd2", grid=(nt,),
        in_specs=[_ANY, _row(TB, 1024, C_GLUA), _prev(TB, H, 1024, C_GLUA), _row(TB, 1024, C_GLUB),
                  _prev(TB, H, 1024, C_GLUB), _row(TB, 1024), _next(TB, H, 1024, nt), _full((K_CONF, 1024)),
                  _full((1, 1024)), _full((1, 1024))],
        out_specs=[_row(TB, 2048), _full((K_CONF, 1024)), _full((1, 1024)), _full((1, 1024)), _full((1, 1024))],
        out_shape=[S(dproj.shape, BF16), S((K_CONF, 1024), F32)] + [S((1, 1024), F32)] * 3,
        input_output_aliases={0: 0},
        scratch_shapes=[pltpu.VMEM((H + TB + 8, LANES), F32), pltpu.VMEM((TB + H + 8, LANES), F32),
                        pltpu.VMEM((8, H + TB, LANES), F32), pltpu.VMEM((8, TB + H, LANES), F32),
                        pltpu.VMEM((K_CONF * 8, D_CONF), F32)],
        compiler_params=_params(("arbitrary",)))(dproj, proj, proj, proj, proj, du1, du1, w, ba, bb)


def _mesh_pos():
    x, y, c = lax.axis_index("x"), lax.axis_index("y"), lax.axis_index("c")
    return x, y, c, 4 * x + 2 * y + c


def _peer(x, y, c, k):
    return (x ^ ((k >> 2) & 1), y ^ ((k >> 1) & 1), c ^ (k & 1))


def _exchange_copies(ins, outs, kinds, send, recv, loc):
    nb = len(ins)
    x, y, c, me = _mesh_pos()
    src = lambda b, d: ins[b].at[d] if kinds[b] == "blocks" else ins[b]
    copies = [pltpu.make_async_copy(src(b, me), outs[b].at[me], loc.at[b]) for b in range(nb)]
    for k in range(1, N_DEV):
        px, py, pc = _peer(x, y, c, k)
        for b in range(nb):
            s = (k - 1) * nb + b
            copies.append(pltpu.make_async_remote_copy(
                src_ref=src(b, 4 * px + 2 * py + pc), dst_ref=outs[b].at[me], send_sem=send.at[s], recv_sem=recv.at[s],
                device_id=(px, py, pc), device_id_type=pl.DeviceIdType.MESH))
    return copies


def _exchange_shapes(bufs, kinds):
    return [S(b.shape if kd == "blocks" else (N_DEV,) + b.shape, b.dtype) for b, kd in zip(bufs, kinds)]


def _exchange_sems(nb):
    n = (N_DEV - 1) * nb
    return [pltpu.SemaphoreType.DMA((n,)), pltpu.SemaphoreType.DMA((n,)), pltpu.SemaphoreType.DMA((nb,))]


def _two_level_gather(ins, outs, send, recv, loc):
    nb = len(ins)
    x, y, c, me = _mesh_pos()
    here, sibling = (x, y, c), (x, y, 1 - c)
    chips = [(1 - x, y), (x, 1 - y), (1 - x, 1 - y)]

    def copy(slot, b, block, to, src=None):
        d = 4 * block[0] + 2 * block[1] + block[2]
        return pltpu.make_async_remote_copy(
            src_ref=outs[b].at[d] if src is None else src, dst_ref=outs[b].at[d],
            send_sem=send.at[slot * nb + b], recv_sem=recv.at[slot * nb + b],
            device_id=to, device_id_type=pl.DeviceIdType.MESH)

    mine = [pltpu.make_async_copy(ins[b], outs[b].at[me], loc.at[b]) for b in range(nb)]
    first = [copy(0, b, here, sibling, src=ins[b]) for b in range(nb)]
    first += [copy(1 + j, b, here, (*chip, c), src=ins[b]) for j, chip in enumerate(chips) for b in range(nb)]

    def start():
        for cp in mine + first:
            cp.start()

    def finish():
        passed = []
        for j, chip in enumerate(chips):
            for b in range(nb):
                copy(1 + j, b, (*chip, c), here).wait_recv()
            onward = [copy(4 + j, b, (*chip, c), sibling) for b in range(nb)]
            for cp in onward:
                cp.start()
            passed += onward
        for b in range(nb):
            copy(0, b, sibling, here).wait_recv()
        for j, chip in enumerate(chips):
            for b in range(nb):
                copy(4 + j, b, (*chip, 1 - c), here).wait_recv()
        for cp in first + passed:
            cp.wait_send()
        for cp in mine:
            cp.wait()

    return start, finish


def _exchange_plan(ins, outs, kinds, send, recv, loc):
    if all(kd == "gather" for kd in kinds):
        return _two_level_gather(ins, outs, send, recv, loc)
    copies = _exchange_copies(ins, outs, kinds, send, recv, loc)

    def start():
        for cp in copies:
            cp.start()

    def finish():
        for cp in copies:
            cp.wait()

    return start, finish


def _exchange(bufs, kinds, name):
    nb = len(bufs)

    def body(*refs):
        start, finish = _exchange_plan(refs[:nb], refs[nb:2 * nb], kinds, *refs[2 * nb:])
        start()
        finish()

    return pl.pallas_call(
        body, name=name, in_specs=[_ANY] * nb, out_specs=[_ANY] * nb,
        out_shape=_exchange_shapes(bufs, kinds), scratch_shapes=_exchange_sems(nb))(*bufs)


def _sum_parts(p_ref):
    acc = p_ref[0].astype(F32)
    for d in range(1, N_DEV):
        acc = acc + p_ref[d].astype(F32)
    return acc


def _adamw_math(g, w, m, v):
    m = ADAM_B1 * m + (1.0 - ADAM_B1) * g
    v = ADAM_B2 * v + (1.0 - ADAM_B2) * (g * g)
    m_hat = m / (1.0 - ADAM_B1 ** ADAM_STEP)
    v_hat = v / (1.0 - ADAM_B2 ** ADAM_STEP)
    return -ADAM_LR * (m_hat / (jnp.sqrt(v_hat) + ADAM_EPS) + ADAM_WD * w), m, v


HEAD_ROWS = 256


def _sum8_adamw(parts, w, m, v, name, head=None):
    _, R, C = w.shape
    tb = HEAD_ROWS if R % HEAD_ROWS == 0 else R
    assert head is None or (tb == HEAD_ROWS and head.shape[1] == HEAD_ROWS and parts.shape[1] == R - HEAD_ROWS)
    skip = 0 if head is None else 1

    def body(*refs):
        p_ref, w_ref, m_ref, v_ref, g_ref, d_ref, mo_ref, vo_ref = refs[skip:]
        g = _sum_parts(p_ref)
        if head is not None:
            g = jnp.where(pl.program_id(0) == 0, _sum_parts(refs[0]), g)
        g_ref[0] = g
        d_ref[0], mo_ref[0], vo_ref[0] = _adamw_math(g, w_ref[0], m_ref[0], v_ref[0])

    first = [] if head is None else [pl.BlockSpec((N_DEV, tb, C), lambda i: (0, 0, 0))]
    own = pl.BlockSpec((1, tb, C), lambda i: (0, i, 0))
    return pl.pallas_call(
        body, name=name, grid=(R // tb,),
        in_specs=first + [pl.BlockSpec((N_DEV, tb, C), lambda i: (0, jnp.maximum(i - skip, 0), 0))] + [own] * 3,
        out_specs=[own] * 4, out_shape=[S((1, R, C), F32)] * 4,
        compiler_params=_params(("parallel",)))(*([] if head is None else [head]), parts, w, m, v)


SMALL_LAYOUT = (
    ("ln_emb_g", 0, 1024), ("ln_emb_b", 0, 1024), ("ssm_conv_b", 0, 1024), ("ssm_conv_b", 1024, 512),
    ("dt_bias", 0, N_HEADS), ("a_log", 0, N_HEADS), ("d_skip", 0, N_HEADS), ("ssm_norm_g", 0, 1024),
    ("b_glu", 0, 1024), ("b_glu", 1024, 1024), ("conf_conv_b", 0, 1024), ("conf_ln_g", 0, 1024),
    ("conf_ln_b", 0, 1024), ("b_out", 0, 1024), ("ln1_g", 0, 1024), ("ln1_b", 0, 1024), ("ln2_g", 0, 1024),
    ("ln2_b", 0, 1024))
SMALL_ROWS = 24
SMALL = tuple(dict.fromkeys(n for n, _, _ in SMALL_LAYOUT))


LOSS_ROW = len(SMALL_LAYOUT)


def _pack_small(rows, loss):
    def body(*refs):
        o_ref = refs[-1]
        o_ref[...] = jnp.zeros_like(o_ref)
        for r, ref in enumerate(refs[:-2]):
            o_ref[r:r + 1, 0:ref.shape[1]] = ref[...]
        o_ref[LOSS_ROW:LOSS_ROW + 1, 0:LANES] = refs[-2][0:1, :]

    return pl.pallas_call(body, name="pack_small", out_shape=S((SMALL_ROWS, 1024), F32))(*rows, loss)


def _small_update(parts, w, m, v):
    def body(*refs):
        p_ref = refs[0]
        ins = {n: refs[1 + 3 * i:4 + 3 * i] for i, n in enumerate(SMALL)}
        o0 = 1 + 3 * len(SMALL)
        outs = {n: refs[o0 + 4 * i:o0 + 4 * i + 4] for i, n in enumerate(SMALL)}
        gsum = refs[-1]
        gsum[...] = _sum_parts(p_ref)
        refs[-2][...] = gsum[LOSS_ROW:LOSS_ROW + 1, 0:LANES]
        for r, (n, off, wd) in enumerate(SMALL_LAYOUT):
            cs = slice(off, off + wd)
            g = gsum[r:r + 1, 0:wd]
            w_ref, m_ref, v_ref = ins[n]
            g_ref, d_ref, mo_ref, vo_ref = outs[n]
            g_ref[:, cs] = g
            d_ref[:, cs], mo_ref[:, cs], vo_ref[:, cs] = _adamw_math(g, w_ref[:, cs], m_ref[:, cs], v_ref[:, cs])

    args = [parts] + [a for n in SMALL for a in (w[n], m[n], v[n])]
    res = pl.pallas_call(
        body, name="small_update",
        out_shape=[S(w[n].shape, F32) for n in SMALL for _ in range(4)] + [S((1, LANES), F32)],
        scratch_shapes=[pltpu.VMEM((SMALL_ROWS, 1024), F32)])(*args)
    return tuple({n: res[4 * i + j] for i, n in enumerate(SMALL)} for j in range(4)) + (res[-1],)


EARLY = ("w_in", "ssm_conv_w", "conf_conv_w")
LATE = ("w_out", "w_ple_gate", "w_ple_proj")


def _local_step(x, p, tgt, W, shards=None):
    r1 = lambda v: v.reshape(1, -1).astype(F32)
    pad_l = lambda v: jnp.pad(r1(v), ((0, 0), (0, LANES - v.size)))
    late = None if shards is None else [shards[n] for n in LATE]
    if shards is None:
        h0, h0b, h0bt = _ln_emb_fwd(x, r1(W["ln_emb_g"]), r1(W["ln_emb_b"]))
    else:
        h0, h0b, h0bt, *gathered = _ln_emb_fwd(x, r1(W["ln_emb_g"]), r1(W["ln_emb_b"]),
                                               exchange=([shards[n] for n in EARLY], ("gather",) * len(EARLY)))
        W = dict(W, **{n: a if n == "w_in" else _unstack_shards(a, BY_COLS[n]) for n, a in zip(EARLY, gathered)})
    w_main, w_dt = _w_in_to_main(W["w_in"])
    scw, scb = W["ssm_conv_w"], r1(W["ssm_conv_b"])
    wx, wb, bx, bb = scw[:, :1024], scw[:, 1024:], scb[:, :1024], scb[:, 1024:]
    dt_bias, alog = pad_l(W["dt_bias"]), pad_l(W["a_log"])
    dskip_row = jnp.repeat(W["d_skip"].reshape(-1), HEAD).reshape(1, -1)
    norm_g = r1(W["ssm_norm_g"])
    bglu = r1(W["b_glu"])
    ba, bbg = bglu[:, :1024], bglu[:, 1024:]
    ccw, ccb, clg, clb = W["conf_conv_w"], r1(W["conf_conv_b"]), r1(W["conf_ln_g"]), r1(W["conf_ln_b"])

    if late is None:
        proj = _mm(h0b, w_main, "nn", "in_proj")
    else:
        proj, *gathered = _mm(h0b, w_main, "nn", "in_proj", exchange=(late, ("gather",) * len(LATE)))
        W = dict(W, **{n: _unstack_shards(a, BY_COLS[n]) for n, a in zip(LATE, gathered)})
    dt_raw = _mm(h0b, w_dt, "nn", "in_proj_dt")
    xs_c, bc_c, dt = _ssd_pre_fwd(proj, dt_raw, wx, wb, bx, bb, dt_bias)
    mix, ypre, hprev, mixt = _ssd_fwd(xs_c, bc_c, dt, proj, alog, dskip_row, norm_g)
    u1, mix, mixt = _conf_fwd(mix, mixt, proj, ccw, ccb, clg, clb, ba, bbg)
    out, h1, h1b, h1bt = _out_proj_post1(mix, W["w_out"], h0, r1(W["b_out"]), r1(W["ln1_g"]), r1(W["ln1_b"]))
    pb = p.astype(BF16)
    dh1a, dgp, dple, loss, dln2g, dln2b = _ple_post2(h1b, W["w_ple_gate"], pb, W["w_ple_proj"], h1, tgt,
                                                      r1(W["ln2_g"]), r1(W["ln2_b"]))

    g = {}
    g["w_ple_proj"] = _mm(pb.T, dple, "nn", "d_ple_proj", out_dtype=BF16)
    g["w_ple_gate"] = _mm(h1bt, dgp, "nn", "d_ple_gate", out_dtype=BF16)
    dout, dh0a, dln1g, dln1b, dbout = _d_h1_post1_bwd(dgp, W["w_ple_gate"], dh1a, h0, out, r1(W["b_out"]),
                                                      r1(W["ln1_g"]))
    g["w_out"] = _mm(mixt, dout, "nn", "d_w_out", out_dtype=BF16)
    dmix = _mm(dout, W["w_out"], "nt", "d_mix")
    du1, dproj, dclg, dclb = _conf_bwd1(dmix, u1, proj, clg, clb)
    dproj, g["conf_conv_w"], dccb, dba, dbb = _conf_bwd2(dproj, proj, du1, ccw, ba, bbg)
    stack = lambda names: [_stack_shards(g[n], BY_COLS[n]) for n in names]
    dxs_c, dbc_c, ddtr, dproj, dng, ddsk, dalog, ddtb, *recv_a = _ssd_bwd(
        dproj, xs_c, bc_c, dt, dt_raw, dt_bias, proj, ypre, hprev, dmix, alog, dskip_row, norm_g,
        exchange=None if late is None else (stack(LATE), ("blocks",) * len(LATE)))
    dproj, dwx, dbx = _ssd_conv_bwd(dproj, proj, dxs_c, wx, bx, 1024, C_XS, "ssd_conv_bwd_x")
    dproj, dwb, dbb2 = _ssd_conv_bwd(dproj, proj, dbc_c, wb, bb, 512, C_BC, "ssd_conv_bwd_bc")
    g["ssm_conv_w"] = jnp.concatenate([dwx, dwb], axis=1)
    dw_dt = _mm(h0bt, ddtr, "nn", "d_w_dt", out_dtype=BF16)
    last_args = (dproj, w_main, ddtr, w_dt, dh0a, x, r1(W["ln_emb_g"]))
    if late is None:
        g["w_in"] = _w_in_blocks(_mm(h0bt, dproj, "nn", "d_w_in", out_dtype=BF16), dw_dt)
        grad_x, dlng, dlnb = _d_h0_ln_bwd(*last_args)
    else:
        hr = HEAD_ROWS
        head = _w_in_blocks(_mm(h0bt[:hr], dproj, "nn", "d_w_in_head", out_dtype=BF16, tk=x.shape[0]), dw_dt[:hr])
        dw_rest, recv_head = _mm(h0bt[hr:], dproj, "nn", "d_w_in", out_dtype=BF16, exchange=([head], ("blocks",)))
        last = ("ssm_conv_w", "conf_conv_w")
        grad_x, dlng, dlnb, *recv_b = _d_h0_ln_bwd(
            *last_args, exchange=([_w_in_blocks(dw_rest, dw_dt[hr:])] + stack(last), ("blocks",) * 3))
        g["recv"] = dict(zip(LATE + ("w_in",) + last, recv_a + recv_b), w_in_head=recv_head)
    g["rows"] = [dlng, dlnb, dbx, dbb2, ddtb, dalog, ddsk, dng, dba, dbb, dccb, dclg, dclb, dbout, dln1g, dln1b,
                 dln2g, dln2b]
    return loss, grad_x, g


W_IN_SEGMENTS = ((0, 2048, 2048), (2048, 5120, 512), (2560, None, N_HEADS), (2576, 0, 2048), (4624, 4096, 1024))


def _w_in_to_main(shards):
    def pieces(p0, width):
        out, p = [], p0
        while p < p0 + width:
            d = p // COLS_PER_DEV
            hi = min(p0 + width, (d + 1) * COLS_PER_DEV)
            out.append(shards[d][:, p - d * COLS_PER_DEV:hi - d * COLS_PER_DEV])
            p = hi
        return out
    main = [s for s in sorted(W_IN_SEGMENTS, key=lambda s: -1 if s[1] is None else s[1]) if s[1] is not None]
    w_main = jnp.concatenate([q for p0, _, width in main for q in pieces(p0, width)], axis=1)
    w_dt = jnp.concatenate(pieces(2560, N_HEADS), axis=1)
    return w_main, jnp.pad(w_dt, ((0, 0), (0, LANES - N_HEADS)))


def _w_in_blocks(dw_main, dw_dt):
    blocks = []
    for d in range(N_DEV):
        lo_d, hi_d = d * COLS_PER_DEV, (d + 1) * COLS_PER_DEV
        parts = []
        for p0, m0, width in W_IN_SEGMENTS:
            lo, hi = max(lo_d, p0), min(hi_d, p0 + width)
            if lo < hi:
                parts.append(dw_dt[:, lo - p0:hi - p0] if m0 is None else dw_main[:, m0 + lo - p0:m0 + hi - p0])
        blocks.append(jnp.concatenate(parts, axis=1))
    return jnp.stack(blocks)


WEIGHTS = ['ln_emb_g', 'ln_emb_b', 'w_in', 'ssm_conv_w', 'ssm_conv_b', 'dt_bias', 'a_log', 'd_skip', 'ssm_norm_g',
           'b_glu', 'conf_conv_w', 'conf_conv_b', 'conf_ln_g', 'conf_ln_b', 'w_out', 'b_out', 'ln1_g', 'ln1_b',
           'w_ple_gate', 'w_ple_proj', 'ln2_g', 'ln2_b']
SHARDED = (("w_in", True), ("w_out", False), ("w_ple_gate", False), ("w_ple_proj", True), ("ssm_conv_w", True),
           ("conf_conv_w", True))
BY_COLS = dict(SHARDED)


def _stack_shards(a, by_cols):
    if by_cols:
        return a.reshape(a.shape[0], N_DEV, a.shape[1] // N_DEV).transpose(1, 0, 2)
    return a.reshape(N_DEV, a.shape[0] // N_DEV, a.shape[1])


def _unstack_shards(a, by_cols):
    if by_cols:
        return a.transpose(1, 0, 2).reshape(a.shape[1], N_DEV * a.shape[2])
    return a.reshape(N_DEV * a.shape[1], a.shape[2])


def kernel(x, p, ln_emb_g, ln_emb_b, w_in, ssm_conv_w, ssm_conv_b, dt_bias, a_log, d_skip, ssm_norm_g, b_glu, conf_conv_w, conf_conv_b, conf_ln_g, conf_ln_b, w_out, b_out, ln1_g, ln1_b, w_ple_gate, w_ple_proj, ln2_g, ln2_b, loss_target, m_ln_emb_g, m_ln_emb_b, m_w_in, m_ssm_conv_w, m_ssm_conv_b, m_dt_bias, m_a_log, m_d_skip, m_ssm_norm_g, m_b_glu, m_conf_conv_w, m_conf_conv_b, m_conf_ln_g, m_conf_ln_b, m_w_out, m_b_out, m_ln1_g, m_ln1_b, m_w_ple_gate, m_w_ple_proj, m_ln2_g, m_ln2_b, v_ln_emb_g, v_ln_emb_b, v_w_in, v_ssm_conv_w, v_ssm_conv_b, v_dt_bias, v_a_log, v_d_skip, v_ssm_norm_g, v_b_glu, v_conf_conv_w, v_conf_conv_b, v_conf_ln_g, v_conf_ln_b, v_w_out, v_b_out, v_ln1_g, v_ln1_b, v_w_ple_gate, v_w_ple_proj, v_ln2_g, v_ln2_b):
    loc = dict(locals())
    w = {n: loc[n] for n in WEIGHTS}
    m = {n: loc["m_" + n] for n in WEIGHTS}
    v = {n: loc["v_" + n] for n in WEIGHTS}
    sharded = [n for n, _ in SHARDED]

    shards = {n: w[n][0].astype(BF16) if n.startswith("w_") else w[n][0] for n in sharded}
    W = {n: w[n].reshape(-1) for n in SMALL}
    loss, grad_x, g = _local_step(x[0], p[0, 0], loss_target[0], W, shards=shards)
    (recv_small,) = _exchange([_pack_small(g["rows"], loss)], ("all",), "small_exchange")

    grads, delta, new_m, new_v = {}, {}, {}, {}
    for n in sharded:
        grads[n], delta[n], new_m[n], new_v[n] = _sum8_adamw(
            g["recv"][n], w[n], m[n], v[n], "adamw_" + n, head=g["recv"]["w_in_head"] if n == "w_in" else None)
    two_d = lambda d: {n: d[n].reshape(1, -1) for n in SMALL}
    *small, loss = _small_update(recv_small, two_d(w), two_d(m), two_d(v))
    for dst, res in zip((grads, delta, new_m, new_v), small):
        for n in SMALL:
            dst[n] = res[n].reshape(w[n].shape)
    return (loss[0, 0], grad_x[None], *[grads[n] for n in WEIGHTS], *[delta[n] for n in WEIGHTS],
            *[new_m[n] for n in WEIGHTS], *[new_v[n] for n in WEIGHTS])
```

```python
import functools

import numpy as np
import jax
import jax.numpy as jnp
from jax import lax
from jax.experimental import pallas as pl
from jax.experimental.pallas import tpu as pltpu

F32, BF16 = jnp.float32, jnp.bfloat16
S = jax.ShapeDtypeStruct

N_DEV = 8
D = 1024
D_PLE = 256
D_SSM = 1024
D_CONF = 1024
N_HEADS = 16
HEAD = 64
N_STATE = 128
CHUNK = 128
K_SSM = 4
K_CONF = 31
D_IN = 5648
COLS_PER_DEV = D_IN // N_DEV
LN_EPS = 1e-5
RMS_EPS = 1e-5
ALPHA = 2.0 ** 0.25
LANES = 128
TB = 512
TB_SSD_CONV_BWD = 256
RG = 32
ROW_UNROLL = 4
HALO_SSM = 8
HALO_CONF = 32
VMEM_LIMIT = 56 * 1024 * 1024

ADAM_LR, ADAM_B1, ADAM_B2, ADAM_EPS, ADAM_WD, ADAM_STEP = 0.001, 0.9, 0.999, 1e-08, 0.01, 10

C_GLUA, C_GLUB, C_XS, C_Z, C_CG = 0, 1, 2, 3, 4
C_BC = 10
N_MAIN = 5632


def _params(sem, vmem=VMEM_LIMIT):
    return pltpu.CompilerParams(dimension_semantics=sem, vmem_limit_bytes=vmem)


def _row(tb, n, col=0):
    return pl.BlockSpec((tb, n), lambda i: (i, col))


def _colt(n, tb, row=0):
    return pl.BlockSpec((n, tb), lambda i: (row, i))


def _full(shape):
    return pl.BlockSpec(shape, lambda i: (0,) * len(shape))


_ANY = pl.BlockSpec(memory_space=pl.ANY)


def _prev(tb, halo, n, col=0):
    r = tb // halo
    return pl.BlockSpec((halo, n), lambda i: (jnp.maximum(i * r - 1, 0), col))


def _next(tb, halo, n, nt, col=0):
    r = tb // halo
    return pl.BlockSpec((halo, n), lambda i: (jnp.minimum((i + 1) * r, nt * r - 1), col))


def _row_loop(tb, fn):
    def it(r, c):
        fn(pl.ds(pl.multiple_of(r * RG, RG), RG))
        return c
    lax.fori_loop(0, tb // RG, it, 0, unroll=ROW_UNROLL)


def _col_loop(n, fn):
    def it(j, c):
        fn(pl.ds(pl.multiple_of(j * LANES, LANES), LANES))
        return c
    lax.fori_loop(0, n // LANES, it, 0)


def _sigmoid(x):
    return 1.0 / (1.0 + jnp.exp(-x))


def _dsilu(x, s):
    return s * (1.0 + x * (1.0 - s))


def _ln_stats(v):
    mu = jnp.mean(v, axis=-1, keepdims=True)
    c = v - mu
    r = lax.rsqrt(jnp.mean(c * c, axis=-1, keepdims=True) + LN_EPS)
    return c * r, r


def _ln_bwd(dy, xhat, r, g):
    dxh = dy * g
    dv = r * (dxh - jnp.mean(dxh, axis=-1, keepdims=True) - xhat * jnp.mean(dxh * xhat, axis=-1, keepdims=True))
    return dv, jnp.sum(dy * xhat, axis=0, keepdims=True), jnp.sum(dy, axis=0, keepdims=True)


def _dot(a, b, dims=((1,), (0,))):
    return lax.dot_general(a.astype(BF16), b.astype(BF16), (dims, ((), ())), preferred_element_type=F32)


_NT = ((1,), (1,))
_TN = ((0,), (0,))


def _split3(x):
    hi = x.astype(BF16)
    r = x - hi.astype(F32)
    mid = r.astype(BF16)
    return hi, mid, (r - mid.astype(F32)).astype(BF16)


def _dot_sel_b(a, b, dims=((1,), (0,))):
    hi, mid, lo = _split3(a)
    return (_dot(lo, b, dims) + _dot(mid, b, dims)) + _dot(hi, b, dims)


def _dot_sel_a(a, b, dims=((1,), (0,))):
    hi, mid, lo = _split3(b)
    return (_dot(a, lo, dims) + _dot(a, mid, dims)) + _dot(a, hi, dims)


def _mm(a, b, mode, name, out_dtype=F32, add=None, tm=1024, tn=None, tk=1024, exchange=None):
    if mode == "nn":
        (M, K), N = a.shape, b.shape[1]
    elif mode == "tn":
        (K, M), N = a.shape, b.shape[1]
    else:
        (M, K), N = a.shape, b.shape[0]
    if tn is None:
        tn = next(t for t in (1024, 1408, 512, 256, LANES) if N % t == 0)
    tm, tn, tk = min(tm, M), min(tn, N), min(tk, K)
    assert M % tm == 0 and N % tn == 0 and K % tk == 0, (name, M, N, K)
    grid = (M // tm, N // tn, K // tk)
    nk = grid[2]
    dims = {"nn": ((1,), (0,)), "tn": _TN, "nt": _NT}[mode]
    n_in = 2 + (add is not None)
    xbufs, kinds = exchange if exchange is not None else ((), ())
    nx = len(xbufs)

    def body(*refs):
        a_ref, b_ref = refs[:2]
        o_ref = refs[n_in + nx]
        acc = refs[n_in + 2 * nx + 1]
        i, j, k = pl.program_id(0), pl.program_id(1), pl.program_id(2)
        if nx:
            start, finish = _exchange_plan(refs[n_in:n_in + nx], refs[n_in + nx + 1:n_in + 2 * nx + 1], kinds,
                                           *refs[n_in + 2 * nx + 2:])
            pl.when((i == 0) & (j == 0) & (k == 0))(start)

        d = _dot(a_ref[...], b_ref[...], dims)

        def write_out(r):
            if add is not None:
                r = r + refs[2][...]
            o_ref[...] = r.astype(out_dtype)

        if nk == 1:
            write_out(d)
        else:
            @pl.when(k == 0)
            def _():
                acc[...] = d

            @pl.when((k > 0) & (k < nk - 1))
            def _():
                acc[...] += d

            @pl.when(k == nk - 1)
            def _():
                write_out(acc[...] + d)

        if nx:
            pl.when((i == grid[0] - 1) & (j == grid[1] - 1) & (k == nk - 1))(finish)

    a_spec = pl.BlockSpec((tk, tm), lambda i, j, k: (k, i)) if mode == "tn" else pl.BlockSpec((tm, tk), lambda i, j, k: (i, k))
    b_spec = pl.BlockSpec((tn, tk), lambda i, j, k: (j, k)) if mode == "nt" else pl.BlockSpec((tk, tn), lambda i, j, k: (k, j))
    o_spec = pl.BlockSpec((tm, tn), lambda i, j, k: (i, j))
    ins, specs = [a, b], [a_spec, b_spec]
    if add is not None:
        ins.append(add)
        specs.append(o_spec)
    acc_spec = pltpu.VMEM((tm, tn) if nk > 1 else (8, LANES), F32)
    if not nx:
        return pl.pallas_call(
            body, name=name, grid=grid, in_specs=specs, out_specs=o_spec,
            out_shape=S((M, N), out_dtype), scratch_shapes=[acc_spec],
            compiler_params=_params(("parallel", "parallel", "arbitrary")))(*ins)
    return pl.pallas_call(
        body, name=name, grid=grid, in_specs=specs + [_ANY] * nx, out_specs=[o_spec] + [_ANY] * nx,
        out_shape=[S((M, N), out_dtype)] + _exchange_shapes(xbufs, kinds),
        scratch_shapes=[acc_spec] + _exchange_sems(nx),
        compiler_params=_params(("arbitrary", "arbitrary", "arbitrary")))(*ins, *xbufs)


def _ln_emb_fwd(x, g, b, exchange=None):
    T = x.shape[0]

    nt = T // TB
    xbufs, kinds = exchange if exchange is not None else ((), ())
    nx = len(xbufs)

    def body(*refs):
        x_ref, g_ref, b_ref = refs[:3]
        h_ref, hb_ref, hbt_ref = refs[3 + nx:6 + nx]
        i = pl.program_id(0)
        if nx:
            start, finish = _exchange_plan(refs[3:3 + nx], refs[6 + nx:6 + 2 * nx], kinds, *refs[6 + 2 * nx:])
            pl.when(i == 0)(start)

        def rows(rs):
            xh, _ = _ln_stats(x_ref[rs, :])
            h = xh * g_ref[...] + b_ref[...]
            h_ref[rs, :] = h
            hb_ref[rs, :] = h.astype(BF16)
        _row_loop(TB, rows)
        hbt_ref[...] = hb_ref[...].T
        if nx:
            pl.when(i == nt - 1)(finish)

    return pl.pallas_call(
        body, name="ln_emb_fwd", grid=(nt,),
        in_specs=[_row(TB, D), _full((1, D)), _full((1, D))] + [_ANY] * nx,
        out_specs=[_row(TB, D), _row(TB, D), _colt(D, TB)] + [_ANY] * nx,
        out_shape=[S((T, D), F32), S((T, D), BF16), S((D, T), BF16)] + _exchange_shapes(xbufs, kinds),
        scratch_shapes=_exchange_sems(nx) if nx else [],
        compiler_params=_params(("arbitrary",)))(x, g, b, *xbufs)


def _out_proj_post1(mix, w_out, h0, b_out, g, b, tm=512, tk=1024):
    T, K = mix.shape
    tm = min(tm, T)
    nk = K // tk
    assert T % tm == 0 and K % tk == 0 and nk >= 2

    def body(mix_ref, w_ref, h0_ref, bo_ref, g_ref, b_ref, out_ref, h_ref, hb_ref, hbt_ref, acc):
        k = pl.program_id(1)
        d = _dot(mix_ref[...], w_ref[...])

        @pl.when(k == 0)
        def _():
            acc[...] = d

        @pl.when((k > 0) & (k < nk - 1))
        def _():
            acc[...] += d

        @pl.when(k == nk - 1)
        def _():
            out_ref[...] = acc[...] + d

            def rows(rs):
                xh, _ = _ln_stats(ALPHA * h0_ref[rs, :] + out_ref[rs, :] + bo_ref[...])
                h = xh * g_ref[...] + b_ref[...]
                h_ref[rs, :] = h
                hb_ref[rs, :] = h.astype(BF16)
            _row_loop(tm, rows)
            hbt_ref[...] = hb_ref[...].T

    rowt = lambda n: pl.BlockSpec((tm, n), lambda i, k: (i, 0))
    const = pl.BlockSpec((1, D), lambda i, k: (0, 0))
    return pl.pallas_call(
        body, name="out_proj_post1", grid=(T // tm, nk),
        in_specs=[pl.BlockSpec((tm, tk), lambda i, k: (i, k)), pl.BlockSpec((tk, D), lambda i, k: (k, 0)), rowt(D),
                  const, const, const],
        out_specs=[rowt(D), rowt(D), rowt(D), pl.BlockSpec((D, tm), lambda i, k: (0, i))],
        out_shape=[S((T, D), F32), S((T, D), F32), S((T, D), BF16), S((D, T), BF16)],
        scratch_shapes=[pltpu.VMEM((tm, D), F32)],
        compiler_params=_params(("parallel", "arbitrary")))(mix, w_out, h0, b_out, g, b)


def _ple_post2(h1b, w_gate, pb, w_proj, h1, tgt, g, b, tm=512):
    T = h1.shape[0]
    tm = min(tm, T)
    assert T % tm == 0

    def body(h1b_ref, wg_ref, pb_ref, wp_ref, h1_ref, tgt_ref, g_ref, b_ref,
             dh1_ref, dgp_ref, dple_ref, loss_ref, dg_ref, db_ref, gp_ref, ple_ref):
        @pl.when(pl.program_id(0) == 0)
        def _():
            loss_ref[...] = jnp.zeros_like(loss_ref)
            dg_ref[...] = jnp.zeros_like(dg_ref)
            db_ref[...] = jnp.zeros_like(db_ref)

        gp_ref[...] = _dot(h1b_ref[...], wg_ref[...])
        ple_ref[...] = _dot(pb_ref[...], wp_ref[...])

        def rows(rs):
            gate = _sigmoid(gp_ref[rs, :])
            ple = ple_ref[rs, :]
            xh, r = _ln_stats(ALPHA * h1_ref[rs, :] + gate * ple)
            err = xh * g_ref[...] + b_ref[...] - tgt_ref[rs, :]
            loss_ref[...] += 0.5 * jnp.sum(jnp.mean(err * err, axis=-1, keepdims=True), axis=0, keepdims=True)
            dv, dg, db = _ln_bwd(err * (1.0 / D), xh, r, g_ref[...])
            dg_ref[...] += dg
            db_ref[...] += db
            dh1_ref[rs, :] = ALPHA * dv
            dgp_ref[rs, :] = (dv * ple * gate * (1.0 - gate)).astype(BF16)
            dple_ref[rs, :] = (dv * gate).astype(BF16)
        _row_loop(tm, rows)

    return pl.pallas_call(
        body, name="ple_post2", grid=(T // tm,),
        in_specs=[_row(tm, D), _full((D, D)), _row(tm, D_PLE), _full((D_PLE, D)), _row(tm, D), _row(tm, D),
                  _full((1, D)), _full((1, D))],
        out_specs=[_row(tm, D)] * 3 + [_full((8, LANES)), _full((1, D)), _full((1, D))],
        out_shape=[S((T, D), F32), S((T, D), BF16), S((T, D), BF16), S((8, LANES), F32), S((1, D), F32), S((1, D), F32)],
        scratch_shapes=[pltpu.VMEM((tm, D), F32), pltpu.VMEM((tm, D), F32)],
        compiler_params=_params(("arbitrary",)))(h1b, w_gate, pb, w_proj, h1, tgt, g, b)


def _d_h1_post1_bwd(dgp, w_gate, dh1a, h0, out, b_out, g, tm=512):
    T = h0.shape[0]
    tm = min(tm, T)
    assert T % tm == 0

    def body(dgp_ref, wg_ref, da_ref, h0_ref, out_ref, bo_ref, g_ref, dout_ref, dh0_ref, dg_ref, db_ref, dbo_ref, dh1):
        @pl.when(pl.program_id(0) == 0)
        def _():
            dg_ref[...] = jnp.zeros_like(dg_ref)
            db_ref[...] = jnp.zeros_like(db_ref)
            dbo_ref[...] = jnp.zeros_like(dbo_ref)

        dh1[...] = da_ref[...] + _dot(dgp_ref[...], wg_ref[...], _NT)

        def rows(rs):
            xh, r = _ln_stats(ALPHA * h0_ref[rs, :] + out_ref[rs, :] + bo_ref[...])
            dv, dg, db = _ln_bwd(dh1[rs, :], xh, r, g_ref[...])
            dg_ref[...] += dg
            db_ref[...] += db
            dbo_ref[...] += jnp.sum(dv, axis=0, keepdims=True)
            dout_ref[rs, :] = dv.astype(BF16)
            dh0_ref[rs, :] = ALPHA * dv
        _row_loop(tm, rows)

    return pl.pallas_call(
        body, name="d_h1_post1_bwd", grid=(T // tm,),
        in_specs=[_row(tm, D), _full((D, D))] + [_row(tm, D)] * 3 + [_full((1, D))] * 2,
        out_specs=[_row(tm, D)] * 2 + [_full((1, D))] * 3,
        out_shape=[S((T, D), BF16), S((T, D), F32)] + [S((1, D), F32)] * 3,
        scratch_shapes=[pltpu.VMEM((tm, D), F32)],
        compiler_params=_params(("arbitrary",)))(dgp, w_gate, dh1a, h0, out, b_out, g)


def _d_h0_ln_bwd(dproj, w_main, ddtr, w_dt, dh0a, x, g, exchange=None, tm=1024, tk=1408):
    T, K = dproj.shape
    tm = min(tm, T)
    assert T % tm == 0 and K % tk == 0
    ni, nk = T // tm, K // tk
    xbufs, kinds = exchange if exchange is not None else ((), ())
    nx = len(xbufs)

    def body(*refs):
        dp_ref, w_ref, dt_ref, wdt_ref, da_ref, x_ref, g_ref = refs[:7]
        dx_ref, dg_ref, db_ref = refs[7 + nx:10 + nx]
        acc = refs[10 + 2 * nx]
        i, k = pl.program_id(0), pl.program_id(1)
        if nx:
            start, finish = _exchange_plan(refs[7:7 + nx], refs[10 + nx:10 + 2 * nx], kinds, *refs[11 + 2 * nx:])
            pl.when((i == 0) & (k == 0))(start)

        @pl.when((i == 0) & (k == 0))
        def _():
            dg_ref[...] = jnp.zeros_like(dg_ref)
            db_ref[...] = jnp.zeros_like(db_ref)

        d = _dot(dp_ref[...], w_ref[...], _NT)

        @pl.when(k == 0)
        def _():
            acc[...] = da_ref[...] + _dot(dt_ref[...], wdt_ref[...], _NT) + d

        @pl.when(k > 0)
        def _():
            acc[...] += d

        @pl.when(k == nk - 1)
        def _():
            def rows(rs):
                xh, r = _ln_stats(x_ref[rs, :])
                dv, dg, db = _ln_bwd(acc[rs, :], xh, r, g_ref[...])
                dg_ref[...] += dg
                db_ref[...] += db
                dx_ref[rs, :] = dv
            _row_loop(tm, rows)

        if nx:
            pl.when((i == ni - 1) & (k == nk - 1))(finish)

    rowt = lambda n: pl.BlockSpec((tm, n), lambda i, k: (i, 0))
    const = lambda shape: pl.BlockSpec(shape, lambda i, k: (0, 0))
    return pl.pallas_call(
        body, name="d_h0_ln_bwd", grid=(ni, nk),
        in_specs=[pl.BlockSpec((tm, tk), lambda i, k: (i, k)), pl.BlockSpec((D, tk), lambda i, k: (0, k)),
                  rowt(LANES), const((D, LANES)), rowt(D), rowt(D), const((1, D))] + [_ANY] * nx,
        out_specs=[rowt(D), const((1, D)), const((1, D))] + [_ANY] * nx,
        out_shape=[S((T, D), F32), S((1, D), F32), S((1, D), F32)] + _exchange_shapes(xbufs, kinds),
        scratch_shapes=[pltpu.VMEM((tm, D), F32)] + (_exchange_sems(nx) if nx else []),
        compiler_params=_params(("arbitrary", "arbitrary")))(dproj, w_main, ddtr, w_dt, dh0a, x, g, *xbufs)


def _softplus(x):
    return jnp.maximum(x, 0.0) + jnp.log1p(jnp.exp(-jnp.abs(x)))


def _ssd_pre_fwd(proj, dt_raw, wx, wb, bx, bb, dt_bias):
    T = proj.shape[0]
    H = HALO_SSM

    def body(xs_ref, xsp_ref, bc_ref, bcp_ref, dtr_ref, wx_ref, wb_ref, bx_ref, bb_ref, dtb_ref,
             xso_ref, bco_ref, dto_ref, extx, extb):
        first = pl.program_id(0) == 0

        def conv(t_ref, p_ref, w_ref, b_ref, o_ref, ext, n):
            def blk(cols):
                ext[0:H, cols] = jnp.where(first, 0.0, p_ref[:, cols])
                ext[H:, cols] = t_ref[:, cols]
                for r0 in range(0, TB, 64):
                    acc = jnp.broadcast_to(b_ref[:, cols], (64, LANES))
                    for k in range(K_SSM):
                        acc = acc + w_ref[k:k + 1, cols] * ext[pl.ds(r0 + H - (K_SSM - 1) + k, 64), cols]
                    o_ref[pl.ds(r0, 64), cols] = acc * _sigmoid(acc)
            _col_loop(n, blk)

        conv(xs_ref, xsp_ref, wx_ref, bx_ref, xso_ref, extx, D_SSM)
        conv(bc_ref, bcp_ref, wb_ref, bb_ref, bco_ref, extb, 512)
        dto_ref[...] = _softplus(dtr_ref[...] + dtb_ref[...])

    return pl.pallas_call(
        body, name="ssd_pre_fwd", grid=(T // TB,),
        in_specs=[_row(TB, 1024, C_XS), _prev(TB, H, 1024, C_XS), _row(TB, 512, C_BC), _prev(TB, H, 512, C_BC),
                  _row(TB, LANES), _full((K_SSM, 1024)), _full((K_SSM, 512)), _full((1, 1024)), _full((1, 512)),
                  _full((1, LANES))],
        out_specs=[_row(TB, 1024), _row(TB, 512), _row(TB, LANES)],
        out_shape=[S((T, 1024), F32), S((T, 512), F32), S((T, LANES), F32)],
        scratch_shapes=[pltpu.VMEM((H + TB, 1024), F32), pltpu.VMEM((H + TB, 512), F32)],
        compiler_params=_params(("parallel",)))(proj, proj, proj, proj, dt_raw, wx, wb, bx, bb, dt_bias)


def _ssd_conv_bwd(dproj, proj, d_c, w, b, n, col, name):
    TB = TB_SSD_CONV_BWD
    T = proj.shape[0]
    nt = T // TB
    H = HALO_SSM
    R = TB + H

    def body(dproj_ref, t_ref, p_ref, n_ref, d_ref, dn_ref, w_ref, b_ref, o_ref, dw_ref, dbias_ref, ext, dp):
        i = pl.program_id(0)
        first, last = i == 0, i == nt - 1

        @pl.when(first)
        def _():
            dw_ref[...] = jnp.zeros_like(dw_ref)
            dbias_ref[...] = jnp.zeros_like(dbias_ref)

        def blk(cols):
            ext[0:H, cols] = jnp.where(first, 0.0, p_ref[:, cols])
            ext[H:H + TB, cols] = t_ref[:, cols]
            ext[H + TB:, cols] = n_ref[:, cols]
            pre = jnp.broadcast_to(b_ref[:, cols], (R, LANES))
            for k in range(K_SSM):
                pre = pre + w_ref[k:k + 1, cols] * ext[pl.ds(H - (K_SSM - 1) + k, R), cols]
            s = _sigmoid(pre)
            ds = _dsilu(pre, s)
            dp[0:TB, cols] = d_ref[:, cols] * ds[0:TB]
            dp[TB:, cols] = jnp.where(last, 0.0, dn_ref[:, cols] * ds[TB:])
            dpt = dp[0:TB, cols]
            dbias_ref[:, cols] += jnp.sum(dpt, axis=0, keepdims=True)
            acc = jnp.zeros((TB, LANES), F32)
            for k in range(K_SSM):
                dw_ref[k:k + 1, cols] += jnp.sum(dpt * ext[pl.ds(H - (K_SSM - 1) + k, TB), cols], axis=0, keepdims=True)
                acc = acc + w_ref[k:k + 1, cols] * dp[pl.ds(K_SSM - 1 - k, TB), cols]
            o_ref[:, cols] = acc.astype(BF16)
        _col_loop(n, blk)

    return pl.pallas_call(
        body, name=name, grid=(nt,),
        in_specs=[_ANY, _row(TB, n, col), _prev(TB, H, n, col), _next(TB, H, n, nt, col),
                  _row(TB, n), _next(TB, H, n, nt), _full((K_SSM, n)), _full((1, n))],
        out_specs=[_row(TB, n, col), _full((K_SSM, n)), _full((1, n))],
        out_shape=[S(dproj.shape, BF16), S((K_SSM, n), F32), S((1, n), F32)],
        input_output_aliases={0: 0},
        scratch_shapes=[pltpu.VMEM((H + TB + H, n), F32), pltpu.VMEM((R, n), F32)],
        compiler_params=_params(("arbitrary",)))(dproj, proj, proj, proj, d_c, d_c, w, b)


def _ssd_consts():
    ex = np.zeros((LANES, D_SSM), np.float32)
    for h in range(N_HEADS):
        ex[h, h * HEAD:(h + 1) * HEAD] = 1.0
    tri = np.tril(np.ones((CHUNK, CHUNK), np.float32))
    return jnp.asarray(ex), jnp.asarray(ex.T.copy()), jnp.asarray(tri), jnp.asarray(tri.T.copy())


def _ssd_common(xs, dt, alog_ref, ex_ref, tri_ref):
    lane = lax.broadcasted_iota(jnp.int32, (1, LANES), 1)
    a = jnp.where(lane < N_HEADS, -jnp.exp(alog_ref[...]), 0.0)
    A = _dot_sel_a(tri_ref[...], dt * a)
    ex = ex_ref[...]
    Aex = _dot_sel_b(A, ex)
    dtex = _dot_sel_b(dt, ex)
    expA = jnp.exp(Aex)
    dec = jnp.exp(Aex[CHUNK - 1:CHUNK, :] - Aex)
    cd = _dot_sel_a(ex, jnp.broadcast_to(jnp.exp(A.T[:, CHUNK - 1:CHUNK]), (LANES, LANES)), _TN)
    return a, A, dtex, expA, dec, cd


def _decay_mask():
    sub = lax.broadcasted_iota(jnp.int32, (CHUNK, CHUNK), 0)
    lane = lax.broadcasted_iota(jnp.int32, (CHUNK, CHUNK), 1)
    return sub, lane, sub >= lane


def _ssd_fwd(xs_c, bc_c, dt, proj, alog, dskip_row, norm_g):
    T = xs_c.shape[0]
    nc = T // CHUNK
    ex, _, tri, _ = _ssd_consts()

    def body(xs_ref, bc_ref, dt_ref, z_ref, alog_ref, dsk_ref, ng_ref, ex_ref, tri_ref,
             ys_ref, ypre_ref, hprev_ref, yst_ref, Hs, ybuf):
        @pl.when(pl.program_id(0) == 0)
        def _():
            Hs[...] = jnp.zeros_like(Hs)

        hprev_ref[0] = Hs[...]
        xs, dt = xs_ref[...], dt_ref[...]
        a, A, dtex, expA, dec, cd = _ssd_common(xs, dt, alog_ref, ex_ref, tri_ref)
        AT = A.T
        xdt = xs * dtex
        xdec = xdt * dec
        _, _, causal = _decay_mask()
        for g in range(2):
            gs = slice(g * 512, (g + 1) * 512)
            B = bc_ref[:, g * N_STATE:(g + 1) * N_STATE]
            C = bc_ref[:, 256 + g * N_STATE:256 + (g + 1) * N_STATE]
            cb = _dot(C, B, _NT)
            Hg = Hs[gs, :]
            yoff = _dot(C, Hg, _NT) * expA[:, gs]
            for j in range(8):
                h = g * 8 + j
                hs = slice(h * HEAD, (h + 1) * HEAD)
                L = jnp.exp(jnp.where(causal, A[:, h:h + 1] - AT[h:h + 1, :], -1e30))
                ybuf[:, hs] = _dot(cb * L, xdt[:, hs]) + yoff[:, j * HEAD:(j + 1) * HEAD]
            Hs[gs, :] = cd[gs, :] * Hg + _dot(xdec[:, gs], B, _TN)
        ypre = ybuf[...] + dsk_ref[...] * xs
        ypre_ref[...] = ypre
        z = z_ref[...]
        yz = ypre * (z * _sigmoid(z))
        for g in range(2):
            gs = slice(g * 512, (g + 1) * 512)
            v = yz[:, gs]
            r = lax.rsqrt(jnp.mean(v * v, axis=-1, keepdims=True) + RMS_EPS)
            ys_ref[:, gs] = (v * r * ng_ref[:, gs]).astype(BF16)
        yst_ref[...] = ys_ref[...].T

    return pl.pallas_call(
        body, name="ssd_fwd", grid=(nc,),
        in_specs=[_row(CHUNK, 1024), _row(CHUNK, 512), _row(CHUNK, LANES), _row(CHUNK, 1024, C_Z),
                  _full((1, LANES)), _full((1, 1024)), _full((1, 1024)), _full((LANES, 1024)), _full((CHUNK, CHUNK))],
        out_specs=[_row(CHUNK, 1024), _row(CHUNK, 1024), pl.BlockSpec((1, 1024, N_STATE), lambda c: (c, 0, 0)),
                   _colt(1024, CHUNK)],
        out_shape=[S((T, 2048), BF16), S((T, 1024), F32), S((nc, 1024, N_STATE), F32), S((2048, T), BF16)],
        scratch_shapes=[pltpu.VMEM((1024, N_STATE), F32), pltpu.VMEM((CHUNK, 1024), F32)],
        compiler_params=_params(("arbitrary",)))(xs_c, bc_c, dt, proj, alog, dskip_row, norm_g, ex, tri)


def _ssd_bwd(dproj, xs_c, bc_c, dt, dt_raw, dt_bias, proj, ypre, hprev, dmix, alog, dskip_row, norm_g, exchange=None):
    T = xs_c.shape[0]
    nc = T // CHUNK
    ex, ext, tri, triu = _ssd_consts()
    rev = lambda n, col=0: pl.BlockSpec((CHUNK, n), lambda c: (nc - 1 - c, col))
    xbufs, kinds = exchange if exchange is not None else ((), ())
    nx = len(xbufs)
    N_IN, N_OUT = 17, 8

    def body(*refs):
        (dproj_ref, xs_ref, bc_ref, dt_ref, dtr_ref, dtb_ref, z_ref, ypre_ref, hprev_ref, dys_ref, alog_ref, dsk_ref,
         ng_ref, ex_ref, ext_ref, tri_ref, triu_ref) = refs[:N_IN]
        (dxs_ref, dbc_ref, ddt_ref, dz_ref, dng_ref, ddsk_ref, dalog_ref,
         ddtb_ref) = refs[N_IN + nx:N_IN + nx + N_OUT]
        dHs, dxbuf, dskacc = refs[N_IN + N_OUT + 2 * nx:N_IN + N_OUT + 2 * nx + 3]
        c = pl.program_id(0)
        if nx:
            start, finish = _exchange_plan(refs[N_IN:N_IN + nx], refs[N_IN + nx + N_OUT:N_IN + N_OUT + 2 * nx], kinds,
                                           *refs[N_IN + N_OUT + 2 * nx + 3:])
            pl.when(c == 0)(start)

        @pl.when(c == 0)
        def _():
            dHs[...] = jnp.zeros_like(dHs)
            dng_ref[...] = jnp.zeros_like(dng_ref)
            dalog_ref[...] = jnp.zeros_like(dalog_ref)
            ddtb_ref[...] = jnp.zeros_like(ddtb_ref)
            dskacc[...] = jnp.zeros_like(dskacc)

        xs, dt, z, ypre, dys = xs_ref[...], dt_ref[...], z_ref[...], ypre_ref[...], dys_ref[...]
        sg = _sigmoid(z)
        sz = z * sg
        yz = ypre * sz
        dyz_parts = []
        for g in range(2):
            gs = slice(g * 512, (g + 1) * 512)
            v = yz[:, gs]
            r = lax.rsqrt(jnp.mean(v * v, axis=-1, keepdims=True) + RMS_EPS)
            vn = v * r
            dng_ref[:, gs] += jnp.sum(dys[:, gs] * vn, axis=0, keepdims=True)
            dvn = dys[:, gs] * ng_ref[:, gs]
            dyz_parts.append(r * (dvn - vn * jnp.mean(dvn * vn, axis=-1, keepdims=True)))
        dyz = jnp.concatenate(dyz_parts, axis=1)
        dy = dyz * sz
        dz_ref[...] = (dyz * ypre * _dsilu(z, sg)).astype(BF16)
        dskacc[...] += jnp.sum(dy * xs, axis=0, keepdims=True)

        a, A, dtex, expA, dec, cd = _ssd_common(xs, dt, alog_ref, ex_ref, tri_ref)
        AT = A.T
        xdt = xs * dtex
        xdec = xdt * dec
        dye = dy * expA
        H = hprev_ref[0]
        dHn = dHs[...]
        sub, lane, causal = _decay_mask()
        dAc = jnp.zeros((CHUNK, LANES), F32)
        Rm = jnp.zeros((CHUNK, LANES), F32)
        yoff_parts, q_parts = [], []
        for g in range(2):
            gs = slice(g * 512, (g + 1) * 512)
            B = bc_ref[:, g * N_STATE:(g + 1) * N_STATE]
            C = bc_ref[:, 256 + g * N_STATE:256 + (g + 1) * N_STATE]
            cb = _dot(C, B, _NT)
            Hg, dHg = H[gs, :], dHn[gs, :]
            Q = _dot(B, dHg, _NT)
            yoff_parts.append(_dot(C, Hg, _NT) * expA[:, gs])
            q_parts.append(Q)
            dcb = jnp.zeros((CHUNK, CHUNK), F32)
            for j in range(8):
                h = g * 8 + j
                hs = slice(h * HEAD, (h + 1) * HEAD)
                L = jnp.exp(jnp.where(causal, A[:, h:h + 1] - AT[h:h + 1, :], -1e30))
                M = cb * L
                G = _dot(dy[:, hs], xdt[:, hs], _NT)
                dxbuf[:, hs] = _dot(M, dy[:, hs], _TN)
                dcb = dcb + G * L
                E = G * M
                dAc = jnp.where(lane == h, jnp.sum(E, axis=1, keepdims=True), dAc)
                Rm = jnp.where(sub == h, jnp.sum(E, axis=0, keepdims=True), Rm)
            dbc_ref[:, g * N_STATE:(g + 1) * N_STATE] = _dot(dcb, C, _TN) + _dot(xdec[:, gs], dHg)
            dbc_ref[:, 256 + g * N_STATE:256 + (g + 1) * N_STATE] = _dot(dcb, B) + _dot(dye[:, gs], Hg)
            dHs[gs, :] = cd[gs, :] * dHg + _dot(dye[:, gs], C, _TN)
        yoff = jnp.concatenate(yoff_parts, axis=1)
        Qd = jnp.concatenate(q_parts, axis=1) * dec
        dxdt = dxbuf[...] + Qd
        extm = ext_ref[...]
        red_s = _dot_sel_b(xdt * Qd, extm)
        dA = dAc - Rm.T + _dot_sel_b(dy * yoff, extm) - red_s
        hd = jnp.sum(_dot_sel_b(H * dHn, extm, _TN), axis=0, keepdims=True)
        last_add = jnp.sum(red_s, axis=0, keepdims=True) + jnp.exp(A[CHUNK - 1:CHUNK, :]) * hd
        dA = dA + jnp.where(sub == CHUNK - 1, last_add, 0.0)
        dadt = _dot_sel_a(triu_ref[...], dA)
        ddtr = (dadt * a + _dot_sel_b(dxdt * xs, extm)) * _sigmoid(dtr_ref[...] + dtb_ref[...])
        ddt_ref[...] = ddtr.astype(BF16)
        ddtb_ref[...] += jnp.sum(ddtr, axis=0, keepdims=True)
        dalog_ref[...] += jnp.sum(dadt * dt, axis=0, keepdims=True) * a
        dxs_ref[...] = dxdt * dtex + dsk_ref[...] * dy

        @pl.when(c == nc - 1)
        def _():
            ddsk_ref[...] = _dot_sel_b(jnp.broadcast_to(dskacc[...], (8, 1024)), extm)[0:1, :]

        if nx:
            pl.when(c == nc - 1)(finish)

    return pl.pallas_call(
        body, name="ssd_bwd", grid=(nc,),
        in_specs=[_ANY, rev(1024), rev(512), rev(LANES), rev(LANES), _full((1, LANES)), rev(1024, C_Z), rev(1024),
                  pl.BlockSpec((1, 1024, N_STATE), lambda c: (nc - 1 - c, 0, 0)), rev(1024, 0),
                  _full((1, LANES)), _full((1, 1024)), _full((1, 1024)),
                  _full((LANES, 1024)), _full((1024, LANES)), _full((CHUNK, CHUNK)), _full((CHUNK, CHUNK))] + [_ANY] * nx,
        out_specs=[rev(1024), rev(512), rev(LANES), rev(1024, C_Z), _full((1, 1024)), _full((1, LANES)),
                   _full((1, LANES)), _full((1, LANES))] + [_ANY] * nx,
        out_shape=[S((T, 1024), F32), S((T, 512), F32), S((T, LANES), BF16), S(dproj.shape, BF16),
                   S((1, 1024), F32), S((1, LANES), F32), S((1, LANES), F32), S((1, LANES), F32)]
        + _exchange_shapes(xbufs, kinds),
        input_output_aliases={0: 3},
        scratch_shapes=[pltpu.VMEM((1024, N_STATE), F32), pltpu.VMEM((CHUNK, 1024), F32), pltpu.VMEM((1, 1024), F32)]
        + (_exchange_sems(nx) if nx else []),
        compiler_params=_params(("arbitrary",)))(
            dproj, xs_c, bc_c, dt, dt_raw, dt_bias, proj, ypre, hprev, dmix, alog, dskip_row, norm_g, ex, ext, tri, triu,
            *xbufs)


def _shifted_copies(ext, ext8):
    n = ext8.shape[1]
    for r in range(8):
        ext8[r] = ext[pl.ds(r, n), :]


def _shifted(ext8, off, rows):
    return ext8[off % 8, pl.ds(off - off % 8, rows), :]


def _conf_fwd(mix, mixt, proj, w, cb, lg, lb, ba, bb):
    T = proj.shape[0]
    H = HALO_CONF

    def body(mix_ref, mixt_ref, ga_ref, gap_ref, gb_ref, gbp_ref, cg_ref, w_ref, cb_ref, lg_ref, lb_ref, ba_ref,
             bb_ref, u1_ref, yc_ref, yct_ref, ext, ext8):
        first = pl.program_id(0) == 0
        ext[H + TB:, :] = jnp.zeros((8, LANES), F32)

        def blk(cols):
            up = (gap_ref[:, cols] + ba_ref[:, cols]) * _sigmoid(gbp_ref[:, cols] + bb_ref[:, cols])
            ext[0:H, :] = jnp.where(first, 0.0, up)
            ext[H:H + TB, :] = (ga_ref[:, cols] + ba_ref[:, cols]) * _sigmoid(gb_ref[:, cols] + bb_ref[:, cols])
            _shifted_copies(ext, ext8)
            for r0 in range(0, TB, 64):
                acc = jnp.broadcast_to(cb_ref[:, cols], (64, LANES))
                for k in range(K_CONF):
                    acc = acc + w_ref[k:k + 1, cols] * _shifted(ext8, r0 + H - (K_CONF - 1) + k, 64)
                u1_ref[pl.ds(r0, 64), cols] = acc
        _col_loop(D_CONF, blk)

        def rows(rs):
            xh, _ = _ln_stats(u1_ref[rs, :])
            u2 = xh * lg_ref[...] + lb_ref[...]
            cg = cg_ref[rs, :]
            yc_ref[rs, :] = (u2 * _sigmoid(u2) * cg * _sigmoid(cg)).astype(BF16)
        _row_loop(TB, rows)
        yct_ref[...] = yc_ref[...].T

    return pl.pallas_call(
        body, name="conf_fwd", grid=(T // TB,),
        in_specs=[_ANY, _ANY, _row(TB, 1024, C_GLUA), _prev(TB, H, 1024, C_GLUA), _row(TB, 1024, C_GLUB),
                  _prev(TB, H, 1024, C_GLUB), _row(TB, 1024, C_CG), _full((K_CONF, 1024))] + [_full((1, 1024))] * 5,
        out_specs=[_row(TB, 1024), _row(TB, 1024, 1), _colt(1024, TB, 1)],
        out_shape=[S((T, 1024), F32), S((T, 2048), BF16), S((2048, T), BF16)],
        input_output_aliases={0: 1, 1: 2},
        scratch_shapes=[pltpu.VMEM((H + TB + 8, LANES), F32), pltpu.VMEM((8, H + TB, LANES), F32)],
        compiler_params=_params(("parallel",)))(mix, mixt, proj, proj, proj, proj, proj, w, cb, lg, lb, ba, bb)


def _conf_bwd1(dmix, u1, proj, lg, lb):
    T = u1.shape[0]

    def body(dy_ref, u1_ref, cg_ref, lg_ref, lb_ref, du1_ref, dcg_ref, dg_ref, db_ref):
        @pl.when(pl.program_id(0) == 0)
        def _():
            dg_ref[...] = jnp.zeros_like(dg_ref)
            db_ref[...] = jnp.zeros_like(db_ref)

        def rows(rs):
            xh, r = _ln_stats(u1_ref[rs, :])
            u2 = xh * lg_ref[...] + lb_ref[...]
            s2 = _sigmoid(u2)
            cg = cg_ref[rs, :]
            sc = _sigmoid(cg)
            dy = dy_ref[rs, :]
            dcg_ref[rs, :] = (dy * u2 * s2 * _dsilu(cg, sc)).astype(BF16)
            dv, dg, db = _ln_bwd(dy * cg * sc * _dsilu(u2, s2), xh, r, lg_ref[...])
            dg_ref[...] += dg
            db_ref[...] += db
            du1_ref[rs, :] = dv
        _row_loop(TB, rows)

    return pl.pallas_call(
        body, name="conf_bwd1", grid=(T // TB,),
        in_specs=[_row(TB, 1024, 1), _row(TB, 1024), _row(TB, 1024, C_CG), _full((1, 1024)), _full((1, 1024))],
        out_specs=[_row(TB, 1024), _row(TB, 1024, C_CG), _full((1, 1024)), _full((1, 1024))],
        out_shape=[S((T, 1024), F32), S((T, N_MAIN), BF16), S((1, 1024), F32), S((1, 1024), F32)],
        compiler_params=_params(("arbitrary",)))(dmix, u1, proj, lg, lb)


def _conf_bwd2(dproj, proj, du1, w, ba, bb):
    T = du1.shape[0]
    nt = T // TB
    H = HALO_CONF

    def body(dproj_ref, ga_ref, gap_ref, gb_ref, gbp_ref, du_ref, dun_ref, w_ref, ba_ref, bb_ref,
             dg_ref, dw_ref, dcb_ref, dba_ref, dbb_ref, ext, dext, ext8, dext8, dwacc):
        i = pl.program_id(0)
        first, last = i == 0, i == nt - 1

        @pl.when(first)
        def _():
            for r in (dcb_ref, dba_ref, dbb_ref, dwacc):
                r[...] = jnp.zeros_like(r)

        ext[H + TB:, :] = jnp.zeros((8, LANES), F32)
        dext[H + TB:, :] = jnp.zeros((8, LANES), F32)

        def blk(cols):
            cols_b = pl.ds(pl.multiple_of(cols.start + D_CONF, LANES), LANES)
            up = (gap_ref[:, cols] + ba_ref[:, cols]) * _sigmoid(gbp_ref[:, cols] + bb_ref[:, cols])
            ext[0:H, :] = jnp.where(first, 0.0, up)
            a = ga_ref[:, cols] + ba_ref[:, cols]
            sb = _sigmoid(gb_ref[:, cols] + bb_ref[:, cols])
            ext[H:H + TB, :] = a * sb
            du = du_ref[:, cols]
            dext[0:TB, :] = du
            dext[TB:TB + H, :] = jnp.where(last, 0.0, dun_ref[:, cols])
            _shifted_copies(ext, ext8)
            _shifted_copies(dext, dext8)
            dcb_ref[:, cols] += jnp.sum(du, axis=0, keepdims=True)
            for r0 in range(0, TB, 64):
                dur = du_ref[pl.ds(r0, 64), cols]
                acc = jnp.zeros((64, LANES), F32)
                for k in range(K_CONF):
                    prod = dur * _shifted(ext8, r0 + H - (K_CONF - 1) + k, 64)
                    dwacc[k * 8:(k + 1) * 8, cols] += prod.reshape(8, 8, LANES).sum(axis=0)
                    acc = acc + w_ref[k:k + 1, cols] * _shifted(dext8, r0 + K_CONF - 1 - k, 64)
                ar, sr = a[r0:r0 + 64], sb[r0:r0 + 64]
                da = acc * sr
                dbv = acc * ar * sr * (1.0 - sr)
                dg_ref[pl.ds(r0, 64), cols] = da.astype(BF16)
                dg_ref[pl.ds(r0, 64), cols_b] = dbv.astype(BF16)
                dba_ref[:, cols] += jnp.sum(da, axis=0, keepdims=True)
                dbb_ref[:, cols] += jnp.sum(dbv, axis=0, keepdims=True)
        _col_loop(D_CONF, blk)

        @pl.when(last)
        def _():
            dw_ref[...] = jnp.sum(dwacc[...].reshape(K_CONF, 8, D_CONF), axis=1)

    return pl.pallas_call(
        body, name="conf_bwd2", grid=(nt,),
        in_specs=[_ANY, _row(TB, 1024, C_GLUA), _prev(TB, H, 1024, C_GLUA), _row(TB, 1024, C_GLUB),
                  _prev(TB, H, 1024, C_GLUB), _row(TB, 1024), _next(TB, H, 1024, nt), _full((K_CONF, 1024)),
                  _full((1, 1024)), _full((1, 1024))],
        out_specs=[_row(TB, 2048), _full((K_CONF, 1024)), _full((1, 1024)), _full((1, 1024)), _full((1, 1024))],
        out_shape=[S(dproj.shape, BF16), S((K_CONF, 1024), F32)] + [S((1, 1024), F32)] * 3,
        input_output_aliases={0: 0},
        scratch_shapes=[pltpu.VMEM((H + TB + 8, LANES), F32), pltpu.VMEM((TB + H + 8, LANES), F32),
                        pltpu.VMEM((8, H + TB, LANES), F32), pltpu.VMEM((8, TB + H, LANES), F32),
                        pltpu.VMEM((K_CONF * 8, D_CONF), F32)],
        compiler_params=_params(("arbitrary",)))(dproj, proj, proj, proj, proj, du1, du1, w, ba, bb)


def _mesh_pos():
    x, y, c = lax.axis_index("x"), lax.axis_index("y"), lax.axis_index("c")
    return x, y, c, 4 * x + 2 * y + c


def _peer(x, y, c, k):
    return (x ^ ((k >> 2) & 1), y ^ ((k >> 1) & 1), c ^ (k & 1))


def _exchange_copies(ins, outs, kinds, send, recv, loc):
    nb = len(ins)
    x, y, c, me = _mesh_pos()
    src = lambda b, d: ins[b].at[d] if kinds[b] == "blocks" else ins[b]
    copies = [pltpu.make_async_copy(src(b, me), outs[b].at[me], loc.at[b]) for b in range(nb)]
    for k in range(1, N_DEV):
        px, py, pc = _peer(x, y, c, k)
        for b in range(nb):
            s = (k - 1) * nb + b
            copies.append(pltpu.make_async_remote_copy(
                src_ref=src(b, 4 * px + 2 * py + pc), dst_ref=outs[b].at[me], send_sem=send.at[s], recv_sem=recv.at[s],
                device_id=(px, py, pc), device_id_type=pl.DeviceIdType.MESH))
    return copies


def _exchange_shapes(bufs, kinds):
    return [S(b.shape if kd == "blocks" else (N_DEV,) + b.shape, b.dtype) for b, kd in zip(bufs, kinds)]


def _exchange_sems(nb):
    n = (N_DEV - 1) * nb
    return [pltpu.SemaphoreType.DMA((n,)), pltpu.SemaphoreType.DMA((n,)), pltpu.SemaphoreType.DMA((nb,))]


def _two_level_gather(ins, outs, send, recv, loc):
    nb = len(ins)
    x, y, c, me = _mesh_pos()
    here, sibling = (x, y, c), (x, y, 1 - c)
    chips = [(1 - x, y), (x, 1 - y), (1 - x, 1 - y)]

    def copy(slot, b, block, to, src=None):
        d = 4 * block[0] + 2 * block[1] + block[2]
        return pltpu.make_async_remote_copy(
            src_ref=outs[b].at[d] if src is None else src, dst_ref=outs[b].at[d],
            send_sem=send.at[slot * nb + b], recv_sem=recv.at[slot * nb + b],
            device_id=to, device_id_type=pl.DeviceIdType.MESH)

    mine = [pltpu.make_async_copy(ins[b], outs[b].at[me], loc.at[b]) for b in range(nb)]
    first = [copy(0, b, here, sibling, src=ins[b]) for b in range(nb)]
    first += [copy(1 + j, b, here, (*chip, c), src=ins[b]) for j, chip in enumerate(chips) for b in range(nb)]

    def start():
        for cp in mine + first:
            cp.start()

    def finish():
        passed = []
        for j, chip in enumerate(chips):
            for b in range(nb):
                copy(1 + j, b, (*chip, c), here).wait_recv()
            onward = [copy(4 + j, b, (*chip, c), sibling) for b in range(nb)]
            for cp in onward:
                cp.start()
            passed += onward
        for b in range(nb):
            copy(0, b, sibling, here).wait_recv()
        for j, chip in enumerate(chips):
            for b in range(nb):
                copy(4 + j, b, (*chip, 1 - c), here).wait_recv()
        for cp in first + passed:
            cp.wait_send()
        for cp in mine:
            cp.wait()

    return start, finish


def _exchange_plan(ins, outs, kinds, send, recv, loc):
    if all(kd == "gather" for kd in kinds):
        return _two_level_gather(ins, outs, send, recv, loc)
    copies = _exchange_copies(ins, outs, kinds, send, recv, loc)

    def start():
        for cp in copies:
            cp.start()

    def finish():
        for cp in copies:
            cp.wait()

    return start, finish


def _exchange(bufs, kinds, name):
    nb = len(bufs)

    def body(*refs):
        start, finish = _exchange_plan(refs[:nb], refs[nb:2 * nb], kinds, *refs[2 * nb:])
        start()
        finish()

    return pl.pallas_call(
        body, name=name, in_specs=[_ANY] * nb, out_specs=[_ANY] * nb,
        out_shape=_exchange_shapes(bufs, kinds), scratch_shapes=_exchange_sems(nb))(*bufs)


def _sum_parts(p_ref):
    acc = p_ref[0].astype(F32)
    for d in range(1, N_DEV):
        acc = acc + p_ref[d].astype(F32)
    return acc


def _adamw_math(g, w, m, v):
    m = ADAM_B1 * m + (1.0 - ADAM_B1) * g
    v = ADAM_B2 * v + (1.0 - ADAM_B2) * (g * g)
    m_hat = m / (1.0 - ADAM_B1 ** ADAM_STEP)
    v_hat = v / (1.0 - ADAM_B2 ** ADAM_STEP)
    return -ADAM_LR * (m_hat / (jnp.sqrt(v_hat) + ADAM_EPS) + ADAM_WD * w), m, v


HEAD_ROWS = 256


def _sum8_adamw(parts, w, m, v, name, head=None):
    _, R, C = w.shape
    tb = HEAD_ROWS if R % HEAD_ROWS == 0 else R
    assert head is None or (tb == HEAD_ROWS and head.shape[1] == HEAD_ROWS and parts.shape[1] == R - HEAD_ROWS)
    skip = 0 if head is None else 1

    def body(*refs):
        p_ref, w_ref, m_ref, v_ref, g_ref, d_ref, mo_ref, vo_ref = refs[skip:]
        g = _sum_parts(p_ref)
        if head is not None:
            g = jnp.where(pl.program_id(0) == 0, _sum_parts(refs[0]), g)
        g_ref[0] = g
        d_ref[0], mo_ref[0], vo_ref[0] = _adamw_math(g, w_ref[0], m_ref[0], v_ref[0])

    first = [] if head is None else [pl.BlockSpec((N_DEV, tb, C), lambda i: (0, 0, 0))]
    own = pl.BlockSpec((1, tb, C), lambda i: (0, i, 0))
    return pl.pallas_call(
        body, name=name, grid=(R // tb,),
        in_specs=first + [pl.BlockSpec((N_DEV, tb, C), lambda i: (0, jnp.maximum(i - skip, 0), 0))] + [own] * 3,
        out_specs=[own] * 4, out_shape=[S((1, R, C), F32)] * 4,
        compiler_params=_params(("parallel",)))(*([] if head is None else [head]), parts, w, m, v)


SMALL_LAYOUT = (
    ("ln_emb_g", 0, 1024), ("ln_emb_b", 0, 1024), ("ssm_conv_b", 0, 1024), ("ssm_conv_b", 1024, 512),
    ("dt_bias", 0, N_HEADS), ("a_log", 0, N_HEADS), ("d_skip", 0, N_HEADS), ("ssm_norm_g", 0, 1024),
    ("b_glu", 0, 1024), ("b_glu", 1024, 1024), ("conf_conv_b", 0, 1024), ("conf_ln_g", 0, 1024),
    ("conf_ln_b", 0, 1024), ("b_out", 0, 1024), ("ln1_g", 0, 1024), ("ln1_b", 0, 1024), ("ln2_g", 0, 1024),
    ("ln2_b", 0, 1024))
SMALL_ROWS = 24
SMALL = tuple(dict.fromkeys(n for n, _, _ in SMALL_LAYOUT))


LOSS_ROW = len(SMALL_LAYOUT)


def _pack_small(rows, loss):
    def body(*refs):
        o_ref = refs[-1]
        o_ref[...] = jnp.zeros_like(o_ref)
        for r, ref in enumerate(refs[:-2]):
            o_ref[r:r + 1, 0:ref.shape[1]] = ref[...]
        o_ref[LOSS_ROW:LOSS_ROW + 1, 0:LANES] = refs[-2][0:1, :]

    return pl.pallas_call(body, name="pack_small", out_shape=S((SMALL_ROWS, 1024), F32))(*rows, loss)


def _small_update(parts, w, m, v):
    def body(*refs):
        p_ref = refs[0]
        ins = {n: refs[1 + 3 * i:4 + 3 * i] for i, n in enumerate(SMALL)}
        o0 = 1 + 3 * len(SMALL)
        outs = {n: refs[o0 + 4 * i:o0 + 4 * i + 4] for i, n in enumerate(SMALL)}
        gsum = refs[-1]
        gsum[...] = _sum_parts(p_ref)
        refs[-2][...] = gsum[LOSS_ROW:LOSS_ROW + 1, 0:LANES]
        for r, (n, off, wd) in enumerate(SMALL_LAYOUT):
            cs = slice(off, off + wd)
            g = gsum[r:r + 1, 0:wd]
            w_ref, m_ref, v_ref = ins[n]
            g_ref, d_ref, mo_ref, vo_ref = outs[n]
            g_ref[:, cs] = g
            d_ref[:, cs], mo_ref[:, cs], vo_ref[:, cs] = _adamw_math(g, w_ref[:, cs], m_ref[:, cs], v_ref[:, cs])

    args = [parts] + [a for n in SMALL for a in (w[n], m[n], v[n])]
    res = pl.pallas_call(
        body, name="small_update",
        out_shape=[S(w[n].shape, F32) for n in SMALL for _ in range(4)] + [S((1, LANES), F32)],
        scratch_shapes=[pltpu.VMEM((SMALL_ROWS, 1024), F32)])(*args)
    return tuple({n: res[4 * i + j] for i, n in enumerate(SMALL)} for j in range(4)) + (res[-1],)


EARLY = ("w_in", "ssm_conv_w", "conf_conv_w")
LATE = ("w_out", "w_ple_gate", "w_ple_proj")


def _local_step(x, p, tgt, W, shards=None):
    r1 = lambda v: v.reshape(1, -1).astype(F32)
    pad_l = lambda v: jnp.pad(r1(v), ((0, 0), (0, LANES - v.size)))
    late = None if shards is None else [shards[n] for n in LATE]
    if shards is None:
        h0, h0b, h0bt = _ln_emb_fwd(x, r1(W["ln_emb_g"]), r1(W["ln_emb_b"]))
    else:
        h0, h0b, h0bt, *gathered = _ln_emb_fwd(x, r1(W["ln_emb_g"]), r1(W["ln_emb_b"]),
                                               exchange=([shards[n] for n in EARLY], ("gather",) * len(EARLY)))
        W = dict(W, **{n: a if n == "w_in" else _unstack_shards(a, BY_COLS[n]) for n, a in zip(EARLY, gathered)})
    w_main, w_dt = _w_in_to_main(W["w_in"])
    scw, scb = W["ssm_conv_w"], r1(W["ssm_conv_b"])
    wx, wb, bx, bb = scw[:, :1024], scw[:, 1024:], scb[:, :1024], scb[:, 1024:]
    dt_bias, alog = pad_l(W["dt_bias"]), pad_l(W["a_log"])
    dskip_row = jnp.repeat(W["d_skip"].reshape(-1), HEAD).reshape(1, -1)
    norm_g = r1(W["ssm_norm_g"])
    bglu = r1(W["b_glu"])
    ba, bbg = bglu[:, :1024], bglu[:, 1024:]
    ccw, ccb, clg, clb = W["conf_conv_w"], r1(W["conf_conv_b"]), r1(W["conf_ln_g"]), r1(W["conf_ln_b"])

    if late is None:
        proj = _mm(h0b, w_main, "nn", "in_proj")
    else:
        proj, *gathered = _mm(h0b, w_main, "nn", "in_proj", exchange=(late, ("gather",) * len(LATE)))
        W = dict(W, **{n: _unstack_shards(a, BY_COLS[n]) for n, a in zip(LATE, gathered)})
    dt_raw = _mm(h0b, w_dt, "nn", "in_proj_dt")
    xs_c, bc_c, dt = _ssd_pre_fwd(proj, dt_raw, wx, wb, bx, bb, dt_bias)
    mix, ypre, hprev, mixt = _ssd_fwd(xs_c, bc_c, dt, proj, alog, dskip_row, norm_g)
    u1, mix, mixt = _conf_fwd(mix, mixt, proj, ccw, ccb, clg, clb, ba, bbg)
    out, h1, h1b, h1bt = _out_proj_post1(mix, W["w_out"], h0, r1(W["b_out"]), r1(W["ln1_g"]), r1(W["ln1_b"]))
    pb = p.astype(BF16)
    dh1a, dgp, dple, loss, dln2g, dln2b = _ple_post2(h1b, W["w_ple_gate"], pb, W["w_ple_proj"], h1, tgt,
                                                      r1(W["ln2_g"]), r1(W["ln2_b"]))

    g = {}
    g["w_ple_proj"] = _mm(pb.T, dple, "nn", "d_ple_proj", out_dtype=BF16)
    g["w_ple_gate"] = _mm(h1bt, dgp, "nn", "d_ple_gate", out_dtype=BF16)
    dout, dh0a, dln1g, dln1b, dbout = _d_h1_post1_bwd(dgp, W["w_ple_gate"], dh1a, h0, out, r1(W["b_out"]),
                                                      r1(W["ln1_g"]))
    g["w_out"] = _mm(mixt, dout, "nn", "d_w_out", out_dtype=BF16)
    dmix = _mm(dout, W["w_out"], "nt", "d_mix")
    du1, dproj, dclg, dclb = _conf_bwd1(dmix, u1, proj, clg, clb)
    dproj, g["conf_conv_w"], dccb, dba, dbb = _conf_bwd2(dproj, proj, du1, ccw, ba, bbg)
    stack = lambda names: [_stack_shards(g[n], BY_COLS[n]) for n in names]
    dxs_c, dbc_c, ddtr, dproj, dng, ddsk, dalog, ddtb, *recv_a = _ssd_bwd(
        dproj, xs_c, bc_c, dt, dt_raw, dt_bias, proj, ypre, hprev, dmix, alog, dskip_row, norm_g,
        exchange=None if late is None else (stack(LATE), ("blocks",) * len(LATE)))
    dproj, dwx, dbx = _ssd_conv_bwd(dproj, proj, dxs_c, wx, bx, 1024, C_XS, "ssd_conv_bwd_x")
    dproj, dwb, dbb2 = _ssd_conv_bwd(dproj, proj, dbc_c, wb, bb, 512, C_BC, "ssd_conv_bwd_bc")
    g["ssm_conv_w"] = jnp.concatenate([dwx, dwb], axis=1)
    dw_dt = _mm(h0bt, ddtr, "nn", "d_w_dt", out_dtype=BF16)
    last_args = (dproj, w_main, ddtr, w_dt, dh0a, x, r1(W["ln_emb_g"]))
    if late is None:
        g["w_in"] = _w_in_blocks(_mm(h0bt, dproj, "nn", "d_w_in", out_dtype=BF16), dw_dt)
        grad_x, dlng, dlnb = _d_h0_ln_bwd(*last_args)
    else:
        hr = HEAD_ROWS
        head = _w_in_blocks(_mm(h0bt[:hr], dproj, "nn", "d_w_in_head", out_dtype=BF16, tk=x.shape[0]), dw_dt[:hr])
        dw_rest, recv_head = _mm(h0bt[hr:], dproj, "nn", "d_w_in", out_dtype=BF16, exchange=([head], ("blocks",)))
        last = ("ssm_conv_w", "conf_conv_w")
        grad_x, dlng, dlnb, *recv_b = _d_h0_ln_bwd(
            *last_args, exchange=([_w_in_blocks(dw_rest, dw_dt[hr:])] + stack(last), ("blocks",) * 3))
        g["recv"] = dict(zip(LATE + ("w_in",) + last, recv_a + recv_b), w_in_head=recv_head)
    g["rows"] = [dlng, dlnb, dbx, dbb2, ddtb, dalog, ddsk, dng, dba, dbb, dccb, dclg, dclb, dbout, dln1g, dln1b,
                 dln2g, dln2b]
    return loss, grad_x, g


W_IN_SEGMENTS = ((0, 2048, 2048), (2048, 5120, 512), (2560, None, N_HEADS), (2576, 0, 2048), (4624, 4096, 1024))


def _w_in_to_main(shards):
    def pieces(p0, width):
        out, p = [], p0
        while p < p0 + width:
            d = p // COLS_PER_DEV
            hi = min(p0 + width, (d + 1) * COLS_PER_DEV)
            out.append(shards[d][:, p - d * COLS_PER_DEV:hi - d * COLS_PER_DEV])
            p = hi
        return out
    main = [s for s in sorted(W_IN_SEGMENTS, key=lambda s: -1 if s[1] is None else s[1]) if s[1] is not None]
    w_main = jnp.concatenate([q for p0, _, width in main for q in pieces(p0, width)], axis=1)
    w_dt = jnp.concatenate(pieces(2560, N_HEADS), axis=1)
    return w_main, jnp.pad(w_dt, ((0, 0), (0, LANES - N_HEADS)))


def _w_in_blocks(dw_main, dw_dt):
    blocks = []
    for d in range(N_DEV):
        lo_d, hi_d = d * COLS_PER_DEV, (d + 1) * COLS_PER_DEV
        parts = []
        for p0, m0, width in W_IN_SEGMENTS:
            lo, hi = max(lo_d, p0), min(hi_d, p0 + width)
            if lo < hi:
                parts.append(dw_dt[:, lo - p0:hi - p0] if m0 is None else dw_main[:, m0 + lo - p0:m0 + hi - p0])
        blocks.append(jnp.concatenate(parts, axis=1))
    return jnp.stack(blocks)


WEIGHTS = ['ln_emb_g', 'ln_emb_b', 'w_in', 'ssm_conv_w', 'ssm_conv_b', 'dt_bias', 'a_log', 'd_skip', 'ssm_norm_g',
           'b_glu', 'conf_conv_w', 'conf_conv_b', 'conf_ln_g', 'conf_ln_b', 'w_out', 'b_out', 'ln1_g', 'ln1_b',
           'w_ple_gate', 'w_ple_proj', 'ln2_g', 'ln2_b']
SHARDED = (("w_in", True), ("w_out", False), ("w_ple_gate", False), ("w_ple_proj", True), ("ssm_conv_w", True),
           ("conf_conv_w", True))
BY_COLS = dict(SHARDED)


def _stack_shards(a, by_cols):
    if by_cols:
        return a.reshape(a.shape[0], N_DEV, a.shape[1] // N_DEV).transpose(1, 0, 2)
    return a.reshape(N_DEV, a.shape[0] // N_DEV, a.shape[1])


def _unstack_shards(a, by_cols):
    if by_cols:
        return a.transpose(1, 0, 2).reshape(a.shape[1], N_DEV * a.shape[2])
    return a.reshape(N_DEV * a.shape[1], a.shape[2])


def kernel(x, p, ln_emb_g, ln_emb_b, w_in, ssm_conv_w, ssm_conv_b, dt_bias, a_log, d_skip, ssm_norm_g, b_glu, conf_conv_w, conf_conv_b, conf_ln_g, conf_ln_b, w_out, b_out, ln1_g, ln1_b, w_ple_gate, w_ple_proj, ln2_g, ln2_b, loss_target, m_ln_emb_g, m_ln_emb_b, m_w_in, m_ssm_conv_w, m_ssm_conv_b, m_dt_bias, m_a_log, m_d_skip, m_ssm_norm_g, m_b_glu, m_conf_conv_w, m_conf_conv_b, m_conf_ln_g, m_conf_ln_b, m_w_out, m_b_out, m_ln1_g, m_ln1_b, m_w_ple_gate, m_w_ple_proj, m_ln2_g, m_ln2_b, v_ln_emb_g, v_ln_emb_b, v_w_in, v_ssm_conv_w, v_ssm_conv_b, v_dt_bias, v_a_log, v_d_skip, v_ssm_norm_g, v_b_glu, v_conf_conv_w, v_conf_conv_b, v_conf_ln_g, v_conf_ln_b, v_w_out, v_b_out, v_ln1_g, v_ln1_b, v_w_ple_gate, v_w_ple_proj, v_ln2_g, v_ln2_b):
    loc = dict(locals())
    w = {n: loc[n] for n in WEIGHTS}
    m = {n: loc["m_" + n] for n in WEIGHTS}
    v = {n: loc["v_" + n] for n in WEIGHTS}
    sharded = [n for n, _ in SHARDED]

    shards = {n: w[n][0].astype(BF16) if n.startswith("w_") else w[n][0] for n in sharded}
    W = {n: w[n].reshape(-1) for n in SMALL}
    loss, grad_x, g = _local_step(x[0], p[0, 0], loss_target[0], W, shards=shards)
    (recv_small,) = _exchange([_pack_small(g["rows"], loss)], ("all",), "small_exchange")

    grads, delta, new_m, new_v = {}, {}, {}, {}
    for n in sharded:
        grads[n], delta[n], new_m[n], new_v[n] = _sum8_adamw(
            g["recv"][n], w[n], m[n], v[n], "adamw_" + n, head=g["recv"]["w_in_head"] if n == "w_in" else None)
    two_d = lambda d: {n: d[n].reshape(1, -1) for n in SMALL}
    *small, loss = _small_update(recv_small, two_d(w), two_d(m), two_d(v))
    for dst, res in zip((grads, delta, new_m, new_v), small):
        for n in SMALL:
            dst[n] = res[n].reshape(w[n].shape)
    return (loss[0, 0], grad_x[None], *[grads[n] for n in WEIGHTS], *[delta[n] for n in WEIGHTS],
            *[new_m[n] for n in WEIGHTS], *[new_v[n] for n in WEIGHTS])
```

```python
import numpy as np
import jax
import jax.numpy as jnp
from jax import lax
from jax.experimental import pallas as pl
from jax.experimental.pallas import tpu as pltpu

F32, BF16 = jnp.float32, jnp.bfloat16
S = jax.ShapeDtypeStruct

N_DEV = 8
D = 1024
D_PLE = 256
D_SSM = 1024
D_CONF = 1024
N_HEADS = 16
HEAD = 64
N_STATE = 128
CHUNK = 128
K_SSM = 4
K_CONF = 31
D_IN = 5648
COLS_PER_DEV = D_IN // N_DEV
LN_EPS = 1e-5
RMS_EPS = 1e-5
ALPHA = 2.0 ** 0.25
LANES = 128
TB = 512
TB_SSD_CONV_BWD = 256
TM_IN_PROJ = 2048
RG = 32
ROW_UNROLL = 4
HALO_SSM = 8
HALO_CONF = 32
VMEM_LIMIT = 56 * 1024 * 1024

ADAM_LR, ADAM_B1, ADAM_B2, ADAM_EPS, ADAM_WD, ADAM_STEP = 0.001, 0.9, 0.999, 1e-08, 0.01, 10

C_GLUA, C_GLUB, C_XS, C_Z, C_CG = 0, 1, 2, 3, 4
C_BC = 10
N_MAIN = 5632


def _params(sem, vmem=VMEM_LIMIT):
    return pltpu.CompilerParams(dimension_semantics=sem, vmem_limit_bytes=vmem)


def _row(tb, n, col=0):
    return pl.BlockSpec((tb, n), lambda i: (i, col))


def _colt(n, tb, row=0):
    return pl.BlockSpec((n, tb), lambda i: (row, i))


def _full(shape):
    return pl.BlockSpec(shape, lambda i: (0,) * len(shape))


_ANY = pl.BlockSpec(memory_space=pl.ANY)


def _prev(tb, halo, n, col=0):
    r = tb // halo
    return pl.BlockSpec((halo, n), lambda i: (jnp.maximum(i * r - 1, 0), col))


def _next(tb, halo, n, nt, col=0):
    r = tb // halo
    return pl.BlockSpec((halo, n), lambda i: (jnp.minimum((i + 1) * r, nt * r - 1), col))


def _row_loop(tb, fn):
    def it(r, c):
        fn(pl.ds(pl.multiple_of(r * RG, RG), RG))
        return c
    lax.fori_loop(0, tb // RG, it, 0, unroll=ROW_UNROLL)


def _col_loop(n, fn):
    def it(j, c):
        fn(pl.ds(pl.multiple_of(j * LANES, LANES), LANES))
        return c
    lax.fori_loop(0, n // LANES, it, 0)


def _sigmoid(x):
    return 1.0 / (1.0 + jnp.exp(-x))


def _dsilu(x, s):
    return s * (1.0 + x * (1.0 - s))


def _ln_stats(v):
    mu = jnp.mean(v, axis=-1, keepdims=True)
    c = v - mu
    r = lax.rsqrt(jnp.mean(c * c, axis=-1, keepdims=True) + LN_EPS)
    return c * r, r


def _ln_bwd(dy, xhat, r, g):
    dxh = dy * g
    dv = r * (dxh - jnp.mean(dxh, axis=-1, keepdims=True) - xhat * jnp.mean(dxh * xhat, axis=-1, keepdims=True))
    return dv, jnp.sum(dy * xhat, axis=0, keepdims=True), jnp.sum(dy, axis=0, keepdims=True)


def _dot(a, b, dims=((1,), (0,))):
    return lax.dot_general(a.astype(BF16), b.astype(BF16), (dims, ((), ())), preferred_element_type=F32)


_NT = ((1,), (1,))
_TN = ((0,), (0,))


def _split3(x):
    hi = x.astype(BF16)
    r = x - hi.astype(F32)
    mid = r.astype(BF16)
    return hi, mid, (r - mid.astype(F32)).astype(BF16)


def _dot_sel_b(a, b, dims=((1,), (0,))):
    hi, mid, lo = _split3(a)
    return (_dot(lo, b, dims) + _dot(mid, b, dims)) + _dot(hi, b, dims)


def _dot_sel_a(a, b, dims=((1,), (0,))):
    hi, mid, lo = _split3(b)
    return (_dot(a, lo, dims) + _dot(a, mid, dims)) + _dot(a, hi, dims)


def _mm(a, b, mode, name, out_dtype=F32, add=None, tm=1024, tn=None, tk=1024, exchange=None, a_rows=None):
    assert mode in ("nn", "nt")
    (M, K), N = a.shape, b.shape[1 if mode == "nn" else 0]
    row0 = 0
    if a_rows is not None:
        row0, M = a_rows
        tm = M
        assert row0 % M == 0
    if tn is None:
        tn = next(t for t in (1024, 1408, 512, 256, LANES) if N % t == 0)
    tm, tn, tk = min(tm, M), min(tn, N), min(tk, K)
    assert M % tm == 0 and N % tn == 0 and K % tk == 0, (name, M, N, K)
    grid = (M // tm, N // tn, K // tk)
    nk = grid[2]
    dims = ((1,), (0,)) if mode == "nn" else _NT
    n_in = 2 + (add is not None)
    xbufs, kinds = exchange if exchange is not None else ((), ())
    nx = len(xbufs)

    def body(*refs):
        a_ref, b_ref = refs[:2]
        o_ref = refs[n_in + nx]
        acc = refs[n_in + 2 * nx + 1]
        i, j, k = pl.program_id(0), pl.program_id(1), pl.program_id(2)
        if nx:
            start, finish = _exchange_plan(refs[n_in:n_in + nx], refs[n_in + nx + 1:n_in + 2 * nx + 1], kinds,
                                           *refs[n_in + 2 * nx + 2:])
            pl.when((i == 0) & (j == 0) & (k == 0))(start)

        d = _dot(a_ref[...], b_ref[...], dims)

        def write_out(r):
            if add is not None:
                r = r + refs[2][...]
            o_ref[...] = r.astype(out_dtype)

        if nk == 1:
            write_out(d)
        else:
            @pl.when(k == 0)
            def _():
                acc[...] = d

            @pl.when((k > 0) & (k < nk - 1))
            def _():
                acc[...] += d

            @pl.when(k == nk - 1)
            def _():
                write_out(acc[...] + d)

        if nx:
            pl.when((i == grid[0] - 1) & (j == grid[1] - 1) & (k == nk - 1))(finish)

    a_spec = pl.BlockSpec((tm, tk), lambda i, j, k: (i + row0 // tm, k))
    b_spec = pl.BlockSpec((tn, tk), lambda i, j, k: (j, k)) if mode == "nt" else pl.BlockSpec((tk, tn), lambda i, j, k: (k, j))
    o_spec = pl.BlockSpec((tm, tn), lambda i, j, k: (i, j))
    ins, specs = [a, b], [a_spec, b_spec]
    if add is not None:
        ins.append(add)
        specs.append(o_spec)
    acc_spec = pltpu.VMEM((tm, tn) if nk > 1 else (8, LANES), F32)
    if not nx:
        return pl.pallas_call(
            body, name=name, grid=grid, in_specs=specs, out_specs=o_spec,
            out_shape=S((M, N), out_dtype), scratch_shapes=[acc_spec],
            compiler_params=_params(("parallel", "parallel", "arbitrary")))(*ins)
    return pl.pallas_call(
        body, name=name, grid=grid, in_specs=specs + [_ANY] * nx, out_specs=[o_spec] + [_ANY] * nx,
        out_shape=[S((M, N), out_dtype)] + _exchange_shapes(xbufs, kinds),
        scratch_shapes=[acc_spec] + _exchange_sems(nx),
        compiler_params=_params(("arbitrary", "arbitrary", "arbitrary")))(*ins, *xbufs)


def _ln_emb_fwd(x, g, b, exchange=None):
    T = x.shape[0]

    nt = T // TB
    xbufs, kinds = exchange if exchange is not None else ((), ())
    nx = len(xbufs)

    def body(*refs):
        x_ref, g_ref, b_ref = refs[:3]
        h_ref, hb_ref, hbt_ref = refs[3 + nx:6 + nx]
        i = pl.program_id(0)
        if nx:
            start, finish = _exchange_plan(refs[3:3 + nx], refs[6 + nx:6 + 2 * nx], kinds, *refs[6 + 2 * nx:])
            pl.when(i == 0)(start)

        def rows(rs):
            xh, _ = _ln_stats(x_ref[rs, :])
            h = xh * g_ref[...] + b_ref[...]
            h_ref[rs, :] = h
            hb_ref[rs, :] = h.astype(BF16)
        _row_loop(TB, rows)
        hbt_ref[...] = hb_ref[...].T
        if nx:
            pl.when(i == nt - 1)(finish)

    return pl.pallas_call(
        body, name="ln_emb_fwd", grid=(nt,),
        in_specs=[_row(TB, D), _full((1, D)), _full((1, D))] + [_ANY] * nx,
        out_specs=[_row(TB, D), _row(TB, D), _colt(D, TB)] + [_ANY] * nx,
        out_shape=[S((T, D), F32), S((T, D), BF16), S((D, T), BF16)] + _exchange_shapes(xbufs, kinds),
        scratch_shapes=_exchange_sems(nx) if nx else [],
        compiler_params=_params(("arbitrary",)))(x, g, b, *xbufs)


def _out_proj_post1(mix, w_out, h0, b_out, g, b, tm=512, tk=1024):
    T, K = mix.shape
    tm = min(tm, T)
    nk = K // tk
    assert T % tm == 0 and K % tk == 0 and nk >= 2

    def body(mix_ref, w_ref, h0_ref, bo_ref, g_ref, b_ref, out_ref, h_ref, hb_ref, hbt_ref, acc):
        k = pl.program_id(1)
        d = _dot(mix_ref[...], w_ref[...])

        @pl.when(k == 0)
        def _():
            acc[...] = d

        @pl.when((k > 0) & (k < nk - 1))
        def _():
            acc[...] += d

        @pl.when(k == nk - 1)
        def _():
            out_ref[...] = acc[...] + d

            def rows(rs):
                xh, _ = _ln_stats(ALPHA * h0_ref[rs, :] + out_ref[rs, :] + bo_ref[...])
                h = xh * g_ref[...] + b_ref[...]
                h_ref[rs, :] = h
                hb_ref[rs, :] = h.astype(BF16)
            _row_loop(tm, rows)
            hbt_ref[...] = hb_ref[...].T

    rowt = lambda n: pl.BlockSpec((tm, n), lambda i, k: (i, 0))
    const = pl.BlockSpec((1, D), lambda i, k: (0, 0))
    return pl.pallas_call(
        body, name="out_proj_post1", grid=(T // tm, nk),
        in_specs=[pl.BlockSpec((tm, tk), lambda i, k: (i, k)), pl.BlockSpec((tk, D), lambda i, k: (k, 0)), rowt(D),
                  const, const, const],
        out_specs=[rowt(D), rowt(D), rowt(D), pl.BlockSpec((D, tm), lambda i, k: (0, i))],
        out_shape=[S((T, D), F32), S((T, D), F32), S((T, D), BF16), S((D, T), BF16)],
        scratch_shapes=[pltpu.VMEM((tm, D), F32)],
        compiler_params=_params(("parallel", "arbitrary")))(mix, w_out, h0, b_out, g, b)


def _ple_post2(h1b, w_gate, pb, w_proj, h1, tgt, g, b, tm=512):
    T = h1.shape[0]
    tm = min(tm, T)
    assert T % tm == 0

    def body(h1b_ref, wg_ref, pb_ref, wp_ref, h1_ref, tgt_ref, g_ref, b_ref,
             dh1_ref, dgp_ref, dple_ref, loss_ref, dg_ref, db_ref, gp_ref, ple_ref):
        @pl.when(pl.program_id(0) == 0)
        def _():
            loss_ref[...] = jnp.zeros_like(loss_ref)
            dg_ref[...] = jnp.zeros_like(dg_ref)
            db_ref[...] = jnp.zeros_like(db_ref)

        gp_ref[...] = _dot(h1b_ref[...], wg_ref[...])
        ple_ref[...] = _dot(pb_ref[...], wp_ref[...])

        def rows(rs):
            gate = _sigmoid(gp_ref[rs, :])
            ple = ple_ref[rs, :]
            xh, r = _ln_stats(ALPHA * h1_ref[rs, :] + gate * ple)
            err = xh * g_ref[...] + b_ref[...] - tgt_ref[rs, :]
            loss_ref[...] += 0.5 * jnp.sum(jnp.mean(err * err, axis=-1, keepdims=True), axis=0, keepdims=True)
            dv, dg, db = _ln_bwd(err * (1.0 / D), xh, r, g_ref[...])
            dg_ref[...] += dg
            db_ref[...] += db
            dh1_ref[rs, :] = ALPHA * dv
            dgp_ref[rs, :] = (dv * ple * gate * (1.0 - gate)).astype(BF16)
            dple_ref[rs, :] = (dv * gate).astype(BF16)
        _row_loop(tm, rows)

    return pl.pallas_call(
        body, name="ple_post2", grid=(T // tm,),
        in_specs=[_row(tm, D), _full((D, D)), _row(tm, D_PLE), _full((D_PLE, D)), _row(tm, D), _row(tm, D),
                  _full((1, D)), _full((1, D))],
        out_specs=[_row(tm, D)] * 3 + [_full((8, LANES)), _full((1, D)), _full((1, D))],
        out_shape=[S((T, D), F32), S((T, D), BF16), S((T, D), BF16), S((8, LANES), F32), S((1, D), F32), S((1, D), F32)],
        scratch_shapes=[pltpu.VMEM((tm, D), F32), pltpu.VMEM((tm, D), F32)],
        compiler_params=_params(("arbitrary",)))(h1b, w_gate, pb, w_proj, h1, tgt, g, b)


def _d_h1_post1_bwd(dgp, w_gate, dh1a, h0, out, b_out, g, tm=512):
    T = h0.shape[0]
    tm = min(tm, T)
    assert T % tm == 0

    def body(dgp_ref, wg_ref, da_ref, h0_ref, out_ref, bo_ref, g_ref, dout_ref, dh0_ref, dg_ref, db_ref, dbo_ref, dh1):
        @pl.when(pl.program_id(0) == 0)
        def _():
            dg_ref[...] = jnp.zeros_like(dg_ref)
            db_ref[...] = jnp.zeros_like(db_ref)
            dbo_ref[...] = jnp.zeros_like(dbo_ref)

        dh1[...] = da_ref[...] + _dot(dgp_ref[...], wg_ref[...], _NT)

        def rows(rs):
            xh, r = _ln_stats(ALPHA * h0_ref[rs, :] + out_ref[rs, :] + bo_ref[...])
            dv, dg, db = _ln_bwd(dh1[rs, :], xh, r, g_ref[...])
            dg_ref[...] += dg
            db_ref[...] += db
            dbo_ref[...] += jnp.sum(dv, axis=0, keepdims=True)
            dout_ref[rs, :] = dv.astype(BF16)
            dh0_ref[rs, :] = ALPHA * dv
        _row_loop(tm, rows)

    return pl.pallas_call(
        body, name="d_h1_post1_bwd", grid=(T // tm,),
        in_specs=[_row(tm, D), _full((D, D))] + [_row(tm, D)] * 3 + [_full((1, D))] * 2,
        out_specs=[_row(tm, D)] * 2 + [_full((1, D))] * 3,
        out_shape=[S((T, D), BF16), S((T, D), F32)] + [S((1, D), F32)] * 3,
        scratch_shapes=[pltpu.VMEM((tm, D), F32)],
        compiler_params=_params(("arbitrary",)))(dgp, w_gate, dh1a, h0, out, b_out, g)


def _d_h0_ln_bwd(dproj, w_main, ddtr, w_dt, dh0a, x, g, exchange=None, tm=1024, tk=1408):
    T, K = dproj.shape
    tm = min(tm, T)
    assert T % tm == 0 and K % tk == 0
    ni, nk = T // tm, K // tk
    xbufs, kinds = exchange if exchange is not None else ((), ())
    nx = len(xbufs)

    def body(*refs):
        dp_ref, w_ref, dt_ref, wdt_ref, da_ref, x_ref, g_ref = refs[:7]
        dx_ref, dg_ref, db_ref = refs[7 + nx:10 + nx]
        acc = refs[10 + 2 * nx]
        i, k = pl.program_id(0), pl.program_id(1)
        if nx:
            start, finish = _exchange_plan(refs[7:7 + nx], refs[10 + nx:10 + 2 * nx], kinds, *refs[11 + 2 * nx:])
            pl.when((i == 0) & (k == 0))(start)

        @pl.when((i == 0) & (k == 0))
        def _():
            dg_ref[...] = jnp.zeros_like(dg_ref)
            db_ref[...] = jnp.zeros_like(db_ref)

        d = _dot(dp_ref[...], w_ref[...], _NT)

        @pl.when(k == 0)
        def _():
            acc[...] = da_ref[...] + _dot(dt_ref[...], wdt_ref[...], _NT) + d

        @pl.when(k > 0)
        def _():
            acc[...] += d

        @pl.when(k == nk - 1)
        def _():
            def rows(rs):
                xh, r = _ln_stats(x_ref[rs, :])
                dv, dg, db = _ln_bwd(acc[rs, :], xh, r, g_ref[...])
                dg_ref[...] += dg
                db_ref[...] += db
                dx_ref[rs, :] = dv
            _row_loop(tm, rows)

        if nx:
            pl.when((i == ni - 1) & (k == nk - 1))(finish)

    rowt = lambda n: pl.BlockSpec((tm, n), lambda i, k: (i, 0))
    const = lambda shape: pl.BlockSpec(shape, lambda i, k: (0, 0))
    return pl.pallas_call(
        body, name="d_h0_ln_bwd", grid=(ni, nk),
        in_specs=[pl.BlockSpec((tm, tk), lambda i, k: (i, k)), pl.BlockSpec((D, tk), lambda i, k: (0, k)),
                  rowt(LANES), const((D, LANES)), rowt(D), rowt(D), const((1, D))] + [_ANY] * nx,
        out_specs=[rowt(D), const((1, D)), const((1, D))] + [_ANY] * nx,
        out_shape=[S((T, D), F32), S((1, D), F32), S((1, D), F32)] + _exchange_shapes(xbufs, kinds),
        scratch_shapes=[pltpu.VMEM((tm, D), F32)] + (_exchange_sems(nx) if nx else []),
        compiler_params=_params(("arbitrary", "arbitrary")))(dproj, w_main, ddtr, w_dt, dh0a, x, g, *xbufs)


def _softplus(x):
    return jnp.maximum(x, 0.0) + jnp.log1p(jnp.exp(-jnp.abs(x)))


def _ssd_pre_fwd(proj, dt_raw, wx, wb, bx, bb, dt_bias):
    T = proj.shape[0]
    H = HALO_SSM

    def body(xs_ref, xsp_ref, bc_ref, bcp_ref, dtr_ref, wx_ref, wb_ref, bx_ref, bb_ref, dtb_ref,
             xso_ref, bco_ref, dto_ref, extx, extb):
        first = pl.program_id(0) == 0

        def conv(t_ref, p_ref, w_ref, b_ref, o_ref, ext, n):
            def blk(cols):
                ext[0:H, cols] = jnp.where(first, 0.0, p_ref[:, cols])
                ext[H:, cols] = t_ref[:, cols]
                for r0 in range(0, TB, 64):
                    acc = jnp.broadcast_to(b_ref[:, cols], (64, LANES))
                    for k in range(K_SSM):
                        acc = acc + w_ref[k:k + 1, cols] * ext[pl.ds(r0 + H - (K_SSM - 1) + k, 64), cols]
                    o_ref[pl.ds(r0, 64), cols] = acc * _sigmoid(acc)
            _col_loop(n, blk)

        conv(xs_ref, xsp_ref, wx_ref, bx_ref, xso_ref, extx, D_SSM)
        conv(bc_ref, bcp_ref, wb_ref, bb_ref, bco_ref, extb, 512)
        dto_ref[...] = _softplus(dtr_ref[...] + dtb_ref[...])

    return pl.pallas_call(
        body, name="ssd_pre_fwd", grid=(T // TB,),
        in_specs=[_row(TB, 1024, C_XS), _prev(TB, H, 1024, C_XS), _row(TB, 512, C_BC), _prev(TB, H, 512, C_BC),
                  _row(TB, LANES), _full((K_SSM, 1024)), _full((K_SSM, 512)), _full((1, 1024)), _full((1, 512)),
                  _full((1, LANES))],
        out_specs=[_row(TB, 1024), _row(TB, 512), _row(TB, LANES)],
        out_shape=[S((T, 1024), F32), S((T, 512), F32), S((T, LANES), F32)],
        scratch_shapes=[pltpu.VMEM((H + TB, 1024), F32), pltpu.VMEM((H + TB, 512), F32)],
        compiler_params=_params(("parallel",)))(proj, proj, proj, proj, dt_raw, wx, wb, bx, bb, dt_bias)


def _ssd_conv_bwd(dproj, proj, d_c, w, b, n, col, name):
    TB = TB_SSD_CONV_BWD
    T = proj.shape[0]
    nt = T // TB
    H = HALO_SSM
    R = TB + H

    def body(dproj_ref, t_ref, p_ref, n_ref, d_ref, dn_ref, w_ref, b_ref, o_ref, dw_ref, dbias_ref, ext, dp):
        i = pl.program_id(0)
        first, last = i == 0, i == nt - 1

        @pl.when(first)
        def _():
            dw_ref[...] = jnp.zeros_like(dw_ref)
            dbias_ref[...] = jnp.zeros_like(dbias_ref)

        def blk(cols):
            ext[0:H, cols] = jnp.where(first, 0.0, p_ref[:, cols])
            ext[H:H + TB, cols] = t_ref[:, cols]
            ext[H + TB:, cols] = n_ref[:, cols]
            pre = jnp.broadcast_to(b_ref[:, cols], (R, LANES))
            for k in range(K_SSM):
                pre = pre + w_ref[k:k + 1, cols] * ext[pl.ds(H - (K_SSM - 1) + k, R), cols]
            s = _sigmoid(pre)
            ds = _dsilu(pre, s)
            dp[0:TB, cols] = d_ref[:, cols] * ds[0:TB]
            dp[TB:, cols] = jnp.where(last, 0.0, dn_ref[:, cols] * ds[TB:])
            dpt = dp[0:TB, cols]
            dbias_ref[:, cols] += jnp.sum(dpt, axis=0, keepdims=True)
            acc = jnp.zeros((TB, LANES), F32)
            for k in range(K_SSM):
                dw_ref[k:k + 1, cols] += jnp.sum(dpt * ext[pl.ds(H - (K_SSM - 1) + k, TB), cols], axis=0, keepdims=True)
                acc = acc + w_ref[k:k + 1, cols] * dp[pl.ds(K_SSM - 1 - k, TB), cols]
            o_ref[:, cols] = acc.astype(BF16)
        _col_loop(n, blk)

    return pl.pallas_call(
        body, name=name, grid=(nt,),
        in_specs=[_ANY, _row(TB, n, col), _prev(TB, H, n, col), _next(TB, H, n, nt, col),
                  _row(TB, n), _next(TB, H, n, nt), _full((K_SSM, n)), _full((1, n))],
        out_specs=[_row(TB, n, col), _full((K_SSM, n)), _full((1, n))],
        out_shape=[S(dproj.shape, BF16), S((K_SSM, n), F32), S((1, n), F32)],
        input_output_aliases={0: 0},
        scratch_shapes=[pltpu.VMEM((H + TB + H, n), F32), pltpu.VMEM((R, n), F32)],
        compiler_params=_params(("arbitrary",)))(dproj, proj, proj, proj, d_c, d_c, w, b)


def _ssd_consts():
    ex = np.zeros((LANES, D_SSM), np.float32)
    for h in range(N_HEADS):
        ex[h, h * HEAD:(h + 1) * HEAD] = 1.0
    tri = np.tril(np.ones((CHUNK, CHUNK), np.float32))
    return jnp.asarray(ex), jnp.asarray(ex.T.copy()), jnp.asarray(tri), jnp.asarray(tri.T.copy())


def _ssd_common(xs, dt, alog_ref, ex_ref, tri_ref):
    lane = lax.broadcasted_iota(jnp.int32, (1, LANES), 1)
    a = jnp.where(lane < N_HEADS, -jnp.exp(alog_ref[...]), 0.0)
    A = _dot_sel_a(tri_ref[...], dt * a)
    ex = ex_ref[...]
    Aex = _dot_sel_b(A, ex)
    dtex = _dot_sel_b(dt, ex)
    expA = jnp.exp(Aex)
    dec = jnp.exp(Aex[CHUNK - 1:CHUNK, :] - Aex)
    cd = _dot_sel_a(ex, jnp.broadcast_to(jnp.exp(A.T[:, CHUNK - 1:CHUNK]), (LANES, LANES)), _TN)
    return a, A, dtex, expA, dec, cd


def _decay_mask():
    sub = lax.broadcasted_iota(jnp.int32, (CHUNK, CHUNK), 0)
    lane = lax.broadcasted_iota(jnp.int32, (CHUNK, CHUNK), 1)
    return sub, lane, sub >= lane


def _ssd_fwd(xs_c, bc_c, dt, proj, alog, dskip_row, norm_g):
    T = xs_c.shape[0]
    nc = T // CHUNK
    ex, _, tri, _ = _ssd_consts()

    def body(xs_ref, bc_ref, dt_ref, z_ref, alog_ref, dsk_ref, ng_ref, ex_ref, tri_ref,
             ys_ref, ypre_ref, hprev_ref, yst_ref, Hs, ybuf):
        @pl.when(pl.program_id(0) == 0)
        def _():
            Hs[...] = jnp.zeros_like(Hs)

        hprev_ref[0] = Hs[...]
        xs, dt = xs_ref[...], dt_ref[...]
        a, A, dtex, expA, dec, cd = _ssd_common(xs, dt, alog_ref, ex_ref, tri_ref)
        AT = A.T
        xdt = xs * dtex
        xdec = xdt * dec
        _, _, causal = _decay_mask()
        for g in range(2):
            gs = slice(g * 512, (g + 1) * 512)
            B = bc_ref[:, g * N_STATE:(g + 1) * N_STATE]
            C = bc_ref[:, 256 + g * N_STATE:256 + (g + 1) * N_STATE]
            cb = _dot(C, B, _NT)
            Hg = Hs[gs, :]
            yoff = _dot(C, Hg, _NT) * expA[:, gs]
            for j in range(8):
                h = g * 8 + j
                hs = slice(h * HEAD, (h + 1) * HEAD)
                L = jnp.exp(jnp.where(causal, A[:, h:h + 1] - AT[h:h + 1, :], -1e30))
                ybuf[:, hs] = _dot(cb * L, xdt[:, hs]) + yoff[:, j * HEAD:(j + 1) * HEAD]
            Hs[gs, :] = cd[gs, :] * Hg + _dot(xdec[:, gs], B, _TN)
        ypre = ybuf[...] + dsk_ref[...] * xs
        ypre_ref[...] = ypre
        z = z_ref[...]
        yz = ypre * (z * _sigmoid(z))
        for g in range(2):
            gs = slice(g * 512, (g + 1) * 512)
            v = yz[:, gs]
            r = lax.rsqrt(jnp.mean(v * v, axis=-1, keepdims=True) + RMS_EPS)
            ys_ref[:, gs] = (v * r * ng_ref[:, gs]).astype(BF16)
        yst_ref[...] = ys_ref[...].T

    return pl.pallas_call(
        body, name="ssd_fwd", grid=(nc,),
        in_specs=[_row(CHUNK, 1024), _row(CHUNK, 512), _row(CHUNK, LANES), _row(CHUNK, 1024, C_Z),
                  _full((1, LANES)), _full((1, 1024)), _full((1, 1024)), _full((LANES, 1024)), _full((CHUNK, CHUNK))],
        out_specs=[_row(CHUNK, 1024), _row(CHUNK, 1024), pl.BlockSpec((1, 1024, N_STATE), lambda c: (c, 0, 0)),
                   _colt(1024, CHUNK)],
        out_shape=[S((T, 2048), BF16), S((T, 1024), F32), S((nc, 1024, N_STATE), F32), S((2048, T), BF16)],
        scratch_shapes=[pltpu.VMEM((1024, N_STATE), F32), pltpu.VMEM((CHUNK, 1024), F32)],
        compiler_params=_params(("arbitrary",)))(xs_c, bc_c, dt, proj, alog, dskip_row, norm_g, ex, tri)


def _ssd_bwd(dproj, xs_c, bc_c, dt, dt_raw, dt_bias, proj, ypre, hprev, dmix, alog, dskip_row, norm_g, exchange=None):
    T = xs_c.shape[0]
    nc = T // CHUNK
    ex, ext, tri, triu = _ssd_consts()
    rev = lambda n, col=0: pl.BlockSpec((CHUNK, n), lambda c: (nc - 1 - c, col))
    xbufs, kinds = exchange if exchange is not None else ((), ())
    nx = len(xbufs)
    N_IN, N_OUT = 17, 8

    def body(*refs):
        (dproj_ref, xs_ref, bc_ref, dt_ref, dtr_ref, dtb_ref, z_ref, ypre_ref, hprev_ref, dys_ref, alog_ref, dsk_ref,
         ng_ref, ex_ref, ext_ref, tri_ref, triu_ref) = refs[:N_IN]
        (dxs_ref, dbc_ref, ddt_ref, dz_ref, dng_ref, ddsk_ref, dalog_ref,
         ddtb_ref) = refs[N_IN + nx:N_IN + nx + N_OUT]
        dHs, dxbuf, dskacc = refs[N_IN + N_OUT + 2 * nx:N_IN + N_OUT + 2 * nx + 3]
        c = pl.program_id(0)
        if nx:
            start, finish = _exchange_plan(refs[N_IN:N_IN + nx], refs[N_IN + nx + N_OUT:N_IN + N_OUT + 2 * nx], kinds,
                                           *refs[N_IN + N_OUT + 2 * nx + 3:])
            pl.when(c == 0)(start)

        @pl.when(c == 0)
        def _():
            dHs[...] = jnp.zeros_like(dHs)
            dng_ref[...] = jnp.zeros_like(dng_ref)
            dalog_ref[...] = jnp.zeros_like(dalog_ref)
            ddtb_ref[...] = jnp.zeros_like(ddtb_ref)
            dskacc[...] = jnp.zeros_like(dskacc)

        xs, dt, z, ypre, dys = xs_ref[...], dt_ref[...], z_ref[...], ypre_ref[...], dys_ref[...]
        sg = _sigmoid(z)
        sz = z * sg
        yz = ypre * sz
        dyz_parts = []
        for g in range(2):
            gs = slice(g * 512, (g + 1) * 512)
            v = yz[:, gs]
            r = lax.rsqrt(jnp.mean(v * v, axis=-1, keepdims=True) + RMS_EPS)
            vn = v * r
            dng_ref[:, gs] += jnp.sum(dys[:, gs] * vn, axis=0, keepdims=True)
            dvn = dys[:, gs] * ng_ref[:, gs]
            dyz_parts.append(r * (dvn - vn * jnp.mean(dvn * vn, axis=-1, keepdims=True)))
        dyz = jnp.concatenate(dyz_parts, axis=1)
        dy = dyz * sz
        dz_ref[...] = (dyz * ypre * _dsilu(z, sg)).astype(BF16)
        dskacc[...] += jnp.sum(dy * xs, axis=0, keepdims=True)

        a, A, dtex, expA, dec, cd = _ssd_common(xs, dt, alog_ref, ex_ref, tri_ref)
        AT = A.T
        xdt = xs * dtex
        xdec = xdt * dec
        dye = dy * expA
        H = hprev_ref[0]
        dHn = dHs[...]
        sub, lane, causal = _decay_mask()
        dAc = jnp.zeros((CHUNK, LANES), F32)
        Rm = jnp.zeros((CHUNK, LANES), F32)
        yoff_parts, q_parts = [], []
        for g in range(2):
            gs = slice(g * 512, (g + 1) * 512)
            B = bc_ref[:, g * N_STATE:(g + 1) * N_STATE]
            C = bc_ref[:, 256 + g * N_STATE:256 + (g + 1) * N_STATE]
            cb = _dot(C, B, _NT)
            Hg, dHg = H[gs, :], dHn[gs, :]
            Q = _dot(B, dHg, _NT)
            yoff_parts.append(_dot(C, Hg, _NT) * expA[:, gs])
            q_parts.append(Q)
            dcb = jnp.zeros((CHUNK, CHUNK), F32)
            for j in range(8):
                h = g * 8 + j
                hs = slice(h * HEAD, (h + 1) * HEAD)
                L = jnp.exp(jnp.where(causal, A[:, h:h + 1] - AT[h:h + 1, :], -1e30))
                M = cb * L
                G = _dot(dy[:, hs], xdt[:, hs], _NT)
                dxbuf[:, hs] = _dot(M, dy[:, hs], _TN)
                dcb = dcb + G * L
                E = G * M
                dAc = jnp.where(lane == h, jnp.sum(E, axis=1, keepdims=True), dAc)
                Rm = jnp.where(sub == h, jnp.sum(E, axis=0, keepdims=True), Rm)
            dbc_ref[:, g * N_STATE:(g + 1) * N_STATE] = _dot(dcb, C, _TN) + _dot(xdec[:, gs], dHg)
            dbc_ref[:, 256 + g * N_STATE:256 + (g + 1) * N_STATE] = _dot(dcb, B) + _dot(dye[:, gs], Hg)
            dHs[gs, :] = cd[gs, :] * dHg + _dot(dye[:, gs], C, _TN)
        yoff = jnp.concatenate(yoff_parts, axis=1)
        Qd = jnp.concatenate(q_parts, axis=1) * dec
        dxdt = dxbuf[...] + Qd
        extm = ext_ref[...]
        red_s = _dot_sel_b(xdt * Qd, extm)
        dA = dAc - Rm.T + _dot_sel_b(dy * yoff, extm) - red_s
        hd = jnp.sum(_dot_sel_b(H * dHn, extm, _TN), axis=0, keepdims=True)
        last_add = jnp.sum(red_s, axis=0, keepdims=True) + jnp.exp(A[CHUNK - 1:CHUNK, :]) * hd
        dA = dA + jnp.where(sub == CHUNK - 1, last_add, 0.0)
        dadt = _dot_sel_a(triu_ref[...], dA)
        ddtr = (dadt * a + _dot_sel_b(dxdt * xs, extm)) * _sigmoid(dtr_ref[...] + dtb_ref[...])
        ddt_ref[...] = ddtr.astype(BF16)
        ddtb_ref[...] += jnp.sum(ddtr, axis=0, keepdims=True)
        dalog_ref[...] += jnp.sum(dadt * dt, axis=0, keepdims=True) * a
        dxs_ref[...] = dxdt * dtex + dsk_ref[...] * dy

        @pl.when(c == nc - 1)
        def _():
            ddsk_ref[...] = _dot_sel_b(jnp.broadcast_to(dskacc[...], (8, 1024)), extm)[0:1, :]

        if nx:
            pl.when(c == nc - 1)(finish)

    return pl.pallas_call(
        body, name="ssd_bwd", grid=(nc,),
        in_specs=[_ANY, rev(1024), rev(512), rev(LANES), rev(LANES), _full((1, LANES)), rev(1024, C_Z), rev(1024),
                  pl.BlockSpec((1, 1024, N_STATE), lambda c: (nc - 1 - c, 0, 0)), rev(1024, 0),
                  _full((1, LANES)), _full((1, 1024)), _full((1, 1024)),
                  _full((LANES, 1024)), _full((1024, LANES)), _full((CHUNK, CHUNK)), _full((CHUNK, CHUNK))] + [_ANY] * nx,
        out_specs=[rev(1024), rev(512), rev(LANES), rev(1024, C_Z), _full((1, 1024)), _full((1, LANES)),
                   _full((1, LANES)), _full((1, LANES))] + [_ANY] * nx,
        out_shape=[S((T, 1024), F32), S((T, 512), F32), S((T, LANES), BF16), S(dproj.shape, BF16),
                   S((1, 1024), F32), S((1, LANES), F32), S((1, LANES), F32), S((1, LANES), F32)]
        + _exchange_shapes(xbufs, kinds),
        input_output_aliases={0: 3},
        scratch_shapes=[pltpu.VMEM((1024, N_STATE), F32), pltpu.VMEM((CHUNK, 1024), F32), pltpu.VMEM((1, 1024), F32)]
        + (_exchange_sems(nx) if nx else []),
        compiler_params=_params(("arbitrary",)))(
            dproj, xs_c, bc_c, dt, dt_raw, dt_bias, proj, ypre, hprev, dmix, alog, dskip_row, norm_g, ex, ext, tri, triu,
            *xbufs)


def _shifted_copies(ext, ext8):
    n = ext8.shape[1]
    for r in range(8):
        ext8[r] = ext[pl.ds(r, n), :]


def _shifted(ext8, off, rows):
    return ext8[off % 8, pl.ds(off - off % 8, rows), :]


def _conf_fwd(mix, mixt, proj, w, cb, lg, lb, ba, bb):
    T = proj.shape[0]
    H = HALO_CONF

    def body(mix_ref, mixt_ref, ga_ref, gap_ref, gb_ref, gbp_ref, cg_ref, w_ref, cb_ref, lg_ref, lb_ref, ba_ref,
             bb_ref, u1_ref, yc_ref, yct_ref, ext, ext8):
        first = pl.program_id(0) == 0
        ext[H + TB:, :] = jnp.zeros((8, LANES), F32)

        def blk(cols):
            up = (gap_ref[:, cols] + ba_ref[:, cols]) * _sigmoid(gbp_ref[:, cols] + bb_ref[:, cols])
            ext[0:H, :] = jnp.where(first, 0.0, up)
            ext[H:H + TB, :] = (ga_ref[:, cols] + ba_ref[:, cols]) * _sigmoid(gb_ref[:, cols] + bb_ref[:, cols])
            _shifted_copies(ext, ext8)
            for r0 in range(0, TB, 64):
                acc = jnp.broadcast_to(cb_ref[:, cols], (64, LANES))
                for k in range(K_CONF):
                    acc = acc + w_ref[k:k + 1, cols] * _shifted(ext8, r0 + H - (K_CONF - 1) + k, 64)
                u1_ref[pl.ds(r0, 64), cols] = acc
        _col_loop(D_CONF, blk)

        def rows(rs):
            xh, _ = _ln_stats(u1_ref[rs, :])
            u2 = xh * lg_ref[...] + lb_ref[...]
            cg = cg_ref[rs, :]
            yc_ref[rs, :] = (u2 * _sigmoid(u2) * cg * _sigmoid(cg)).astype(BF16)
        _row_loop(TB, rows)
        yct_ref[...] = yc_ref[...].T

    return pl.pallas_call(
        body, name="conf_fwd", grid=(T // TB,),
        in_specs=[_ANY, _ANY, _row(TB, 1024, C_GLUA), _prev(TB, H, 1024, C_GLUA), _row(TB, 1024, C_GLUB),
                  _prev(TB, H, 1024, C_GLUB), _row(TB, 1024, C_CG), _full((K_CONF, 1024))] + [_full((1, 1024))] * 5,
        out_specs=[_row(TB, 1024), _row(TB, 1024, 1), _colt(1024, TB, 1)],
        out_shape=[S((T, 1024), F32), S((T, 2048), BF16), S((2048, T), BF16)],
        input_output_aliases={0: 1, 1: 2},
        scratch_shapes=[pltpu.VMEM((H + TB + 8, LANES), F32), pltpu.VMEM((8, H + TB, LANES), F32)],
        compiler_params=_params(("parallel",)))(mix, mixt, proj, proj, proj, proj, proj, w, cb, lg, lb, ba, bb)


def _conf_bwd1(dmix, u1, proj, lg, lb):
    T = u1.shape[0]

    def body(dy_ref, u1_ref, cg_ref, lg_ref, lb_ref, du1_ref, dcg_ref, dg_ref, db_ref):
        @pl.when(pl.program_id(0) == 0)
        def _():
            dg_ref[...] = jnp.zeros_like(dg_ref)
            db_ref[...] = jnp.zeros_like(db_ref)

        def rows(rs):
            xh, r = _ln_stats(u1_ref[rs, :])
            u2 = xh * lg_ref[...] + lb_ref[...]
            s2 = _sigmoid(u2)
            cg = cg_ref[rs, :]
            sc = _sigmoid(cg)
            dy = dy_ref[rs, :]
            dcg_ref[rs, :] = (dy * u2 * s2 * _dsilu(cg, sc)).astype(BF16)
            dv, dg, db = _ln_bwd(dy * cg * sc * _dsilu(u2, s2), xh, r, lg_ref[...])
            dg_ref[...] += dg
            db_ref[...] += db
            du1_ref[rs, :] = dv
        _row_loop(TB, rows)

    return pl.pallas_call(
        body, name="conf_bwd1", grid=(T // TB,),
        in_specs=[_row(TB, 1024, 1), _row(TB, 1024), _row(TB, 1024, C_CG), _full((1, 1024)), _full((1, 1024))],
        out_specs=[_row(TB, 1024), _row(TB, 1024, C_CG), _full((1, 1024)), _full((1, 1024))],
        out_shape=[S((T, 1024), F32), S((T, N_MAIN), BF16), S((1, 1024), F32), S((1, 1024), F32)],
        compiler_params=_params(("arbitrary",)))(dmix, u1, proj, lg, lb)


def _conf_bwd2(dproj, proj, du1, w, ba, bb):
    T = du1.shape[0]
    nt = T // TB
    H = HALO_CONF

    def body(dproj_ref, ga_ref, gap_ref, gb_ref, gbp_ref, du_ref, dun_ref, w_ref, ba_ref, bb_ref,
             dg_ref, dw_ref, dcb_ref, dba_ref, dbb_ref, ext, dext, ext8, dext8, dwacc):
        i = pl.program_id(0)
        first, last = i == 0, i == nt - 1

        @pl.when(first)
        def _():
            for r in (dcb_ref, dba_ref, dbb_ref, dwacc):
                r[...] = jnp.zeros_like(r)

        ext[H + TB:, :] = jnp.zeros((8, LANES), F32)
        dext[H + TB:, :] = jnp.zeros((8, LANES), F32)

        def blk(cols):
            cols_b = pl.ds(pl.multiple_of(cols.start + D_CONF, LANES), LANES)
            up = (gap_ref[:, cols] + ba_ref[:, cols]) * _sigmoid(gbp_ref[:, cols] + bb_ref[:, cols])
            ext[0:H, :] = jnp.where(first, 0.0, up)
            a = ga_ref[:, cols] + ba_ref[:, cols]
            sb = _sigmoid(gb_ref[:, cols] + bb_ref[:, cols])
            ext[H:H + TB, :] = a * sb
            du = du_ref[:, cols]
            dext[0:TB, :] = du
            dext[TB:TB + H, :] = jnp.where(last, 0.0, dun_ref[:, cols])
            _shifted_copies(ext, ext8)
            _shifted_copies(dext, dext8)
            dcb_ref[:, cols] += jnp.sum(du, axis=0, keepdims=True)
            for r0 in range(0, TB, 64):
                dur = du_ref[pl.ds(r0, 64), cols]
                acc = jnp.zeros((64, LANES), F32)
                for k in range(K_CONF):
                    prod = dur * _shifted(ext8, r0 + H - (K_CONF - 1) + k, 64)
                    dwacc[k * 8:(k + 1) * 8, cols] += prod.reshape(8, 8, LANES).sum(axis=0)
                    acc = acc + w_ref[k:k + 1, cols] * _shifted(dext8, r0 + K_CONF - 1 - k, 64)
                ar, sr = a[r0:r0 + 64], sb[r0:r0 + 64]
                da = acc * sr
                dbv = acc * ar * sr * (1.0 - sr)
                dg_ref[pl.ds(r0, 64), cols] = da.astype(BF16)
                dg_ref[pl.ds(r0, 64), cols_b] = dbv.astype(BF16)
                dba_ref[:, cols] += jnp.sum(da, axis=0, keepdims=True)
                dbb_ref[:, cols] += jnp.sum(dbv, axis=0, keepdims=True)
        _col_loop(D_CONF, blk)

        @pl.when(last)
        def _():
            dw_ref[...] = jnp.sum(dwacc[...].reshape(K_CONF, 8, D_CONF), axis=1)

    return pl.pallas_call(
        body, name="conf_bwd2", grid=(nt,),
        in_specs=[_ANY, _row(TB, 1024, C_GLUA), _prev(TB, H, 1024, C_GLUA), _row(TB, 1024, C_GLUB),
                  _prev(TB, H, 1024, C_GLUB), _row(TB, 1024), _next(TB, H, 1024, nt), _full((K_CONF, 1024)),
                  _full((1, 1024)), _full((1, 1024))],
        out_specs=[_row(TB, 2048), _full((K_CONF, 1024)), _full((1, 1024)), _full((1, 1024)), _full((1, 1024))],
        out_shape=[S(dproj.shape, BF16), S((K_CONF, 1024), F32)] + [S((1, 1024), F32)] * 3,
        input_output_aliases={0: 0},
        scratch_shapes=[pltpu.VMEM((H + TB + 8, LANES), F32), pltpu.VMEM((TB + H + 8, LANES), F32),
                        pltpu.VMEM((8, H + TB, LANES), F32), pltpu.VMEM((8, TB + H, LANES), F32),
                        pltpu.VMEM((K_CONF * 8, D_CONF), F32)],
        compiler_params=_params(("arbitrary",)))(dproj, proj, proj, proj, proj, du1, du1, w, ba, bb)


def _mesh_pos():
    x, y, c = lax.axis_index("x"), lax.axis_index("y"), lax.axis_index("c")
    return x, y, c, 4 * x + 2 * y + c


def _peer(x, y, c, k):
    return (x ^ ((k >> 2) & 1), y ^ ((k >> 1) & 1), c ^ (k & 1))


def _exchange_copies(ins, outs, kinds, send, recv, loc):
    nb = len(ins)
    x, y, c, me = _mesh_pos()
    src = lambda b, d: ins[b].at[d] if kinds[b] == "blocks" else ins[b]
    copies = [pltpu.make_async_copy(src(b, me), outs[b].at[me], loc.at[b]) for b in range(nb)]
    for k in range(1, N_DEV):
        px, py, pc = _peer(x, y, c, k)
        for b in range(nb):
            s = (k - 1) * nb + b
            copies.append(pltpu.make_async_remote_copy(
                src_ref=src(b, 4 * px + 2 * py + pc), dst_ref=outs[b].at[me], send_sem=send.at[s], recv_sem=recv.at[s],
                device_id=(px, py, pc), device_id_type=pl.DeviceIdType.MESH))
    return copies


def _exchange_shapes(bufs, kinds):
    return [S(b.shape if kd == "blocks" else (N_DEV,) + b.shape, b.dtype) for b, kd in zip(bufs, kinds)]


def _exchange_sems(nb):
    n = (N_DEV - 1) * nb
    return [pltpu.SemaphoreType.DMA((n,)), pltpu.SemaphoreType.DMA((n,)), pltpu.SemaphoreType.DMA((nb,))]


def _two_level_gather(ins, outs, send, recv, loc):
    nb = len(ins)
    x, y, c, me = _mesh_pos()
    here, sibling = (x, y, c), (x, y, 1 - c)
    chips = [(1 - x, y), (x, 1 - y), (1 - x, 1 - y)]

    def copy(slot, b, block, to, src=None):
        d = 4 * block[0] + 2 * block[1] + block[2]
        return pltpu.make_async_remote_copy(
            src_ref=outs[b].at[d] if src is None else src, dst_ref=outs[b].at[d],
            send_sem=send.at[slot * nb + b], recv_sem=recv.at[slot * nb + b],
            device_id=to, device_id_type=pl.DeviceIdType.MESH)

    mine = [pltpu.make_async_copy(ins[b], outs[b].at[me], loc.at[b]) for b in range(nb)]
    first = [copy(0, b, here, sibling, src=ins[b]) for b in range(nb)]
    first += [copy(1 + j, b, here, (*chip, c), src=ins[b]) for j, chip in enumerate(chips) for b in range(nb)]

    def start():
        for cp in mine + first:
            cp.start()

    def finish():
        passed = []
        for j, chip in enumerate(chips):
            for b in range(nb):
                copy(1 + j, b, (*chip, c), here).wait_recv()
            onward = [copy(4 + j, b, (*chip, c), sibling) for b in range(nb)]
            for cp in onward:
                cp.start()
            passed += onward
        for b in range(nb):
            copy(0, b, sibling, here).wait_recv()
        for j, chip in enumerate(chips):
            for b in range(nb):
                copy(4 + j, b, (*chip, 1 - c), here).wait_recv()
        for cp in first + passed:
            cp.wait_send()
        for cp in mine:
            cp.wait()

    return start, finish


def _exchange_plan(ins, outs, kinds, send, recv, loc):
    if all(kd == "gather" for kd in kinds):
        return _two_level_gather(ins, outs, send, recv, loc)
    copies = _exchange_copies(ins, outs, kinds, send, recv, loc)

    def start():
        for cp in copies:
            cp.start()

    def finish():
        for cp in copies:
            cp.wait()

    return start, finish


def _exchange(bufs, kinds, name):
    nb = len(bufs)

    def body(*refs):
        start, finish = _exchange_plan(refs[:nb], refs[nb:2 * nb], kinds, *refs[2 * nb:])
        start()
        finish()

    return pl.pallas_call(
        body, name=name, in_specs=[_ANY] * nb, out_specs=[_ANY] * nb,
        out_shape=_exchange_shapes(bufs, kinds), scratch_shapes=_exchange_sems(nb))(*bufs)


def _sum_parts(p_ref):
    acc = p_ref[0].astype(F32)
    for d in range(1, N_DEV):
        acc = acc + p_ref[d].astype(F32)
    return acc


def _adamw_math(g, w, m, v):
    m = ADAM_B1 * m + (1.0 - ADAM_B1) * g
    v = ADAM_B2 * v + (1.0 - ADAM_B2) * (g * g)
    m_hat = m / (1.0 - ADAM_B1 ** ADAM_STEP)
    v_hat = v / (1.0 - ADAM_B2 ** ADAM_STEP)
    return -ADAM_LR * (m_hat / (jnp.sqrt(v_hat) + ADAM_EPS) + ADAM_WD * w), m, v


HEAD_ROWS = 256


def _sum8_adamw(parts, w, m, v, name, head=None):
    _, R, C = w.shape
    tb = HEAD_ROWS if R % HEAD_ROWS == 0 else R
    nb = R // tb
    assert head is None or (tb == HEAD_ROWS and head.shape[1] == HEAD_ROWS and parts.shape[1] == R - HEAD_ROWS)
    skip = 0 if head is None else 1

    def body(*refs):
        p_ref, w_ref, m_ref, v_ref, g_ref, d_ref, mo_ref, vo_ref = refs[skip:]
        g = _sum_parts(p_ref)
        if head is not None:
            g = jnp.where(pl.program_id(0) == nb - 1, _sum_parts(refs[0]), g)
        g_ref[0] = g
        d_ref[0], mo_ref[0], vo_ref[0] = _adamw_math(g, w_ref[0], m_ref[0], v_ref[0])

    first = [] if head is None else [pl.BlockSpec((N_DEV, tb, C), lambda i: (0, 0, 0))]
    own = pl.BlockSpec((1, tb, C), lambda i: (0, i, 0))
    last_part = parts.shape[1] // tb - 1
    return pl.pallas_call(
        body, name=name, grid=(nb,),
        in_specs=first + [pl.BlockSpec((N_DEV, tb, C), lambda i: (0, jnp.minimum(i, last_part), 0))] + [own] * 3,
        out_specs=[own] * 4, out_shape=[S((1, R, C), F32)] * 4,
        compiler_params=_params(("parallel",)))(*([] if head is None else [head]), parts, w, m, v)


SMALL_LAYOUT = (
    ("ln_emb_g", 0, 1024), ("ln_emb_b", 0, 1024), ("ssm_conv_b", 0, 1024), ("ssm_conv_b", 1024, 512),
    ("dt_bias", 0, N_HEADS), ("a_log", 0, N_HEADS), ("d_skip", 0, N_HEADS), ("ssm_norm_g", 0, 1024),
    ("b_glu", 0, 1024), ("b_glu", 1024, 1024), ("conf_conv_b", 0, 1024), ("conf_ln_g", 0, 1024),
    ("conf_ln_b", 0, 1024), ("b_out", 0, 1024), ("ln1_g", 0, 1024), ("ln1_b", 0, 1024), ("ln2_g", 0, 1024),
    ("ln2_b", 0, 1024))
SMALL_ROWS = 24
SMALL = tuple(dict.fromkeys(n for n, _, _ in SMALL_LAYOUT))


LOSS_ROW = len(SMALL_LAYOUT)


def _pack_small(rows, loss):
    def body(*refs):
        o_ref = refs[-1]
        o_ref[...] = jnp.zeros_like(o_ref)
        for r, ref in enumerate(refs[:-2]):
            o_ref[r:r + 1, 0:ref.shape[1]] = ref[...]
        o_ref[LOSS_ROW:LOSS_ROW + 1, 0:LANES] = refs[-2][0:1, :]

    return pl.pallas_call(body, name="pack_small", out_shape=S((SMALL_ROWS, 1024), F32))(*rows, loss)


def _small_update(parts, w, m, v):
    def body(*refs):
        p_ref = refs[0]
        ins = {n: refs[1 + 3 * i:4 + 3 * i] for i, n in enumerate(SMALL)}
        o0 = 1 + 3 * len(SMALL)
        outs = {n: refs[o0 + 4 * i:o0 + 4 * i + 4] for i, n in enumerate(SMALL)}
        gsum = refs[-1]
        gsum[...] = _sum_parts(p_ref)
        refs[-2][...] = gsum[LOSS_ROW:LOSS_ROW + 1, 0:LANES]
        for r, (n, off, wd) in enumerate(SMALL_LAYOUT):
            cs = slice(off, off + wd)
            g = gsum[r:r + 1, 0:wd]
            w_ref, m_ref, v_ref = ins[n]
            g_ref, d_ref, mo_ref, vo_ref = outs[n]
            g_ref[:, cs] = g
            d_ref[:, cs], mo_ref[:, cs], vo_ref[:, cs] = _adamw_math(g, w_ref[:, cs], m_ref[:, cs], v_ref[:, cs])

    args = [parts] + [a for n in SMALL for a in (w[n], m[n], v[n])]
    res = pl.pallas_call(
        body, name="small_update",
        out_shape=[S(w[n].shape, F32) for n in SMALL for _ in range(4)] + [S((1, LANES), F32)],
        scratch_shapes=[pltpu.VMEM((SMALL_ROWS, 1024), F32)])(*args)
    return tuple({n: res[4 * i + j] for i, n in enumerate(SMALL)} for j in range(4)) + (res[-1],)


EARLY = ("w_in", "ssm_conv_w", "conf_conv_w")
LATE = ("w_out", "w_ple_gate", "w_ple_proj")


def _local_step(x, p, tgt, W, shards=None):
    r1 = lambda v: v.reshape(1, -1).astype(F32)
    pad_l = lambda v: jnp.pad(r1(v), ((0, 0), (0, LANES - v.size)))
    late = None if shards is None else [shards[n] for n in LATE]
    if shards is None:
        h0, h0b, h0bt = _ln_emb_fwd(x, r1(W["ln_emb_g"]), r1(W["ln_emb_b"]))
    else:
        h0, h0b, h0bt, *gathered = _ln_emb_fwd(x, r1(W["ln_emb_g"]), r1(W["ln_emb_b"]),
                                               exchange=([shards[n] for n in EARLY], ("gather",) * len(EARLY)))
        W = dict(W, **{n: a if n == "w_in" else _unstack_shards(a, BY_COLS[n]) for n, a in zip(EARLY, gathered)})
    w_main, w_dt = _w_in_to_main(W["w_in"])
    scw, scb = W["ssm_conv_w"], r1(W["ssm_conv_b"])
    wx, wb, bx, bb = scw[:, :1024], scw[:, 1024:], scb[:, :1024], scb[:, 1024:]
    dt_bias, alog = pad_l(W["dt_bias"]), pad_l(W["a_log"])
    dskip_row = jnp.repeat(W["d_skip"].reshape(-1), HEAD).reshape(1, -1)
    norm_g = r1(W["ssm_norm_g"])
    bglu = r1(W["b_glu"])
    ba, bbg = bglu[:, :1024], bglu[:, 1024:]
    ccw, ccb, clg, clb = W["conf_conv_w"], r1(W["conf_conv_b"]), r1(W["conf_ln_g"]), r1(W["conf_ln_b"])

    if late is None:
        proj = _mm(h0b, w_main, "nn", "in_proj", tm=TM_IN_PROJ)
    else:
        proj, *gathered = _mm(h0b, w_main, "nn", "in_proj", tm=TM_IN_PROJ, exchange=(late, ("gather",) * len(LATE)))
        W = dict(W, **{n: _unstack_shards(a, BY_COLS[n]) for n, a in zip(LATE, gathered)})
    dt_raw = _mm(h0b, w_dt, "nn", "in_proj_dt")
    xs_c, bc_c, dt = _ssd_pre_fwd(proj, dt_raw, wx, wb, bx, bb, dt_bias)
    mix, ypre, hprev, mixt = _ssd_fwd(xs_c, bc_c, dt, proj, alog, dskip_row, norm_g)
    u1, mix, mixt = _conf_fwd(mix, mixt, proj, ccw, ccb, clg, clb, ba, bbg)
    out, h1, h1b, h1bt = _out_proj_post1(mix, W["w_out"], h0, r1(W["b_out"]), r1(W["ln1_g"]), r1(W["ln1_b"]))
    pb = p.astype(BF16)
    dh1a, dgp, dple, loss, dln2g, dln2b = _ple_post2(h1b, W["w_ple_gate"], pb, W["w_ple_proj"], h1, tgt,
                                                      r1(W["ln2_g"]), r1(W["ln2_b"]))

    g = {}
    g["w_ple_proj"] = _mm(pb.T, dple, "nn", "d_ple_proj", out_dtype=BF16)
    g["w_ple_gate"] = _mm(h1bt, dgp, "nn", "d_ple_gate", out_dtype=BF16)
    dout, dh0a, dln1g, dln1b, dbout = _d_h1_post1_bwd(dgp, W["w_ple_gate"], dh1a, h0, out, r1(W["b_out"]),
                                                      r1(W["ln1_g"]))
    g["w_out"] = _mm(mixt, dout, "nn", "d_w_out", out_dtype=BF16)
    dmix = _mm(dout, W["w_out"], "nt", "d_mix")
    du1, dproj, dclg, dclb = _conf_bwd1(dmix, u1, proj, clg, clb)
    dproj, g["conf_conv_w"], dccb, dba, dbb = _conf_bwd2(dproj, proj, du1, ccw, ba, bbg)
    stack = lambda names: [_stack_shards(g[n], BY_COLS[n]) for n in names]
    dxs_c, dbc_c, ddtr, dproj, dng, ddsk, dalog, ddtb, *recv_a = _ssd_bwd(
        dproj, xs_c, bc_c, dt, dt_raw, dt_bias, proj, ypre, hprev, dmix, alog, dskip_row, norm_g,
        exchange=None if late is None else (stack(LATE), ("blocks",) * len(LATE)))
    dproj, dwx, dbx = _ssd_conv_bwd(dproj, proj, dxs_c, wx, bx, 1024, C_XS, "ssd_conv_bwd_x")
    dproj, dwb, dbb2 = _ssd_conv_bwd(dproj, proj, dbc_c, wb, bb, 512, C_BC, "ssd_conv_bwd_bc")
    g["ssm_conv_w"] = jnp.concatenate([dwx, dwb], axis=1)
    dw_dt = _mm(h0bt, ddtr, "nn", "d_w_dt", out_dtype=BF16)
    last_args = (dproj, w_main, ddtr, w_dt, dh0a, x, r1(W["ln_emb_g"]))
    if late is None:
        g["w_in"] = _w_in_blocks(_mm(h0bt, dproj, "nn", "d_w_in", out_dtype=BF16), dw_dt)
        grad_x, dlng, dlnb = _d_h0_ln_bwd(*last_args)
    else:
        r0 = D - HEAD_ROWS
        head = _w_in_blocks(_mm(h0bt, dproj, "nn", "d_w_in_head", out_dtype=BF16, tk=x.shape[0],
                                a_rows=(r0, HEAD_ROWS)), dw_dt[r0:])
        dw_rest, recv_head = _mm(h0bt, dproj, "nn", "d_w_in", out_dtype=BF16, a_rows=(0, r0),
                                 exchange=([head], ("blocks",)))
        last = ("ssm_conv_w", "conf_conv_w")
        grad_x, dlng, dlnb, *recv_b = _d_h0_ln_bwd(
            *last_args, exchange=([_w_in_blocks(dw_rest, dw_dt[:r0])] + stack(last), ("blocks",) * 3))
        g["recv"] = dict(zip(LATE + ("w_in",) + last, recv_a + recv_b), w_in_head=recv_head)
    g["rows"] = [dlng, dlnb, dbx, dbb2, ddtb, dalog, ddsk, dng, dba, dbb, dccb, dclg, dclb, dbout, dln1g, dln1b,
                 dln2g, dln2b]
    return loss, grad_x, g


W_IN_SEGMENTS = ((0, 2048, 2048), (2048, 5120, 512), (2560, None, N_HEADS), (2576, 0, 2048), (4624, 4096, 1024))


def _w_in_to_main(shards):
    def pieces(p0, width):
        out, p = [], p0
        while p < p0 + width:
            d = p // COLS_PER_DEV
            hi = min(p0 + width, (d + 1) * COLS_PER_DEV)
            out.append(shards[d][:, p - d * COLS_PER_DEV:hi - d * COLS_PER_DEV])
            p = hi
        return out
    main = [s for s in sorted(W_IN_SEGMENTS, key=lambda s: -1 if s[1] is None else s[1]) if s[1] is not None]
    w_main = jnp.concatenate([q for p0, _, width in main for q in pieces(p0, width)], axis=1)
    w_dt = jnp.concatenate(pieces(2560, N_HEADS), axis=1)
    return w_main, jnp.pad(w_dt, ((0, 0), (0, LANES - N_HEADS)))


def _w_in_blocks(dw_main, dw_dt):
    blocks = []
    for d in range(N_DEV):
        lo_d, hi_d = d * COLS_PER_DEV, (d + 1) * COLS_PER_DEV
        parts = []
        for p0, m0, width in W_IN_SEGMENTS:
            lo, hi = max(lo_d, p0), min(hi_d, p0 + width)
            if lo < hi:
                parts.append(dw_dt[:, lo - p0:hi - p0] if m0 is None else dw_main[:, m0 + lo - p0:m0 + hi - p0])
        blocks.append(jnp.concatenate(parts, axis=1))
    return jnp.stack(blocks)


WEIGHTS = ['ln_emb_g', 'ln_emb_b', 'w_in', 'ssm_conv_w', 'ssm_conv_b', 'dt_bias', 'a_log', 'd_skip', 'ssm_norm_g',
           'b_glu', 'conf_conv_w', 'conf_conv_b', 'conf_ln_g', 'conf_ln_b', 'w_out', 'b_out', 'ln1_g', 'ln1_b',
           'w_ple_gate', 'w_ple_proj', 'ln2_g', 'ln2_b']
SHARDED = (("w_in", True), ("w_out", False), ("w_ple_gate", False), ("w_ple_proj", True), ("ssm_conv_w", True),
           ("conf_conv_w", True))
BY_COLS = dict(SHARDED)


def _stack_shards(a, by_cols):
    if by_cols:
        return a.reshape(a.shape[0], N_DEV, a.shape[1] // N_DEV).transpose(1, 0, 2)
    return a.reshape(N_DEV, a.shape[0] // N_DEV, a.shape[1])


def _unstack_shards(a, by_cols):
    if by_cols:
        return a.transpose(1, 0, 2).reshape(a.shape[1], N_DEV * a.shape[2])
    return a.reshape(N_DEV * a.shape[1], a.shape[2])


def kernel(x, p, ln_emb_g, ln_emb_b, w_in, ssm_conv_w, ssm_conv_b, dt_bias, a_log, d_skip, ssm_norm_g, b_glu, conf_conv_w, conf_conv_b, conf_ln_g, conf_ln_b, w_out, b_out, ln1_g, ln1_b, w_ple_gate, w_ple_proj, ln2_g, ln2_b, loss_target, m_ln_emb_g, m_ln_emb_b, m_w_in, m_ssm_conv_w, m_ssm_conv_b, m_dt_bias, m_a_log, m_d_skip, m_ssm_norm_g, m_b_glu, m_conf_conv_w, m_conf_conv_b, m_conf_ln_g, m_conf_ln_b, m_w_out, m_b_out, m_ln1_g, m_ln1_b, m_w_ple_gate, m_w_ple_proj, m_ln2_g, m_ln2_b, v_ln_emb_g, v_ln_emb_b, v_w_in, v_ssm_conv_w, v_ssm_conv_b, v_dt_bias, v_a_log, v_d_skip, v_ssm_norm_g, v_b_glu, v_conf_conv_w, v_conf_conv_b, v_conf_ln_g, v_conf_ln_b, v_w_out, v_b_out, v_ln1_g, v_ln1_b, v_w_ple_gate, v_w_ple_proj, v_ln2_g, v_ln2_b):
    loc = dict(locals())
    w = {n: loc[n] for n in WEIGHTS}
    m = {n: loc["m_" + n] for n in WEIGHTS}
    v = {n: loc["v_" + n] for n in WEIGHTS}
    sharded = [n for n, _ in SHARDED]

    shards = {n: w[n][0].astype(BF16) if n.startswith("w_") else w[n][0] for n in sharded}
    W = {n: w[n].reshape(-1) for n in SMALL}
    loss, grad_x, g = _local_step(x[0], p[0, 0], loss_target[0], W, shards=shards)
    (recv_small,) = _exchange([_pack_small(g["rows"], loss)], ("all",), "small_exchange")

    grads, delta, new_m, new_v = {}, {}, {}, {}
    for n in sharded:
        grads[n], delta[n], new_m[n], new_v[n] = _sum8_adamw(
            g["recv"][n], w[n], m[n], v[n], "adamw_" + n, head=g["recv"]["w_in_head"] if n == "w_in" else None)
    two_d = lambda d: {n: d[n].reshape(1, -1) for n in SMALL}
    *small, loss = _small_update(recv_small, two_d(w), two_d(m), two_d(v))
    for dst, res in zip((grads, delta, new_m, new_v), small):
        for n in SMALL:
            dst[n] = res[n].reshape(w[n].shape)
    return (loss[0, 0], grad_x[None], *[grads[n] for n in WEIGHTS], *[delta[n] for n in WEIGHTS],
            *[new_m[n] for n in WEIGHTS], *[new_v[n] for n in WEIGHTS])
```

```python
import numpy as np
import jax
import jax.numpy as jnp
from jax import lax
from jax.experimental import pallas as pl
from jax.experimental.pallas import tpu as pltpu

F32, BF16 = jnp.float32, jnp.bfloat16
S = jax.ShapeDtypeStruct

N_DEV = 8
D = 1024
D_PLE = 256
D_SSM = 1024
D_CONF = 1024
N_HEADS = 16
HEAD = 64
N_STATE = 128
CHUNK = 128
K_SSM = 4
K_CONF = 31
D_IN = 5648
COLS_PER_DEV = D_IN // N_DEV
LN_EPS = 1e-5
RMS_EPS = 1e-5
ALPHA = 2.0 ** 0.25
LANES = 128
TB = 512
TB_SSD_CONV_BWD = 512
TM_IN_PROJ = 2048
RG = 32
ROW_UNROLL = 4
HALO_SSM = 8
HALO_CONF = 32
VMEM_LIMIT = 56 * 1024 * 1024

ADAM_LR, ADAM_B1, ADAM_B2, ADAM_EPS, ADAM_WD, ADAM_STEP = 0.001, 0.9, 0.999, 1e-08, 0.01, 10

C_GLUA, C_GLUB, C_XS, C_Z, C_CG = 0, 1, 2, 3, 4
C_BC = 10
N_MAIN = 5632


def _params(sem, vmem=VMEM_LIMIT):
    return pltpu.CompilerParams(dimension_semantics=sem, vmem_limit_bytes=vmem)


def _row(tb, n, col=0):
    return pl.BlockSpec((tb, n), lambda i: (i, col))


def _colt(n, tb, row=0):
    return pl.BlockSpec((n, tb), lambda i: (row, i))


def _full(shape):
    return pl.BlockSpec(shape, lambda i: (0,) * len(shape))


_ANY = pl.BlockSpec(memory_space=pl.ANY)


def _prev(tb, halo, n, col=0):
    r = tb // halo
    return pl.BlockSpec((halo, n), lambda i: (jnp.maximum(i * r - 1, 0), col))


def _next(tb, halo, n, nt, col=0):
    r = tb // halo
    return pl.BlockSpec((halo, n), lambda i: (jnp.minimum((i + 1) * r, nt * r - 1), col))


def _row_loop(tb, fn):
    def it(r, c):
        fn(pl.ds(pl.multiple_of(r * RG, RG), RG))
        return c
    lax.fori_loop(0, tb // RG, it, 0, unroll=ROW_UNROLL)


def _col_loop(n, fn):
    def it(j, c):
        fn(pl.ds(pl.multiple_of(j * LANES, LANES), LANES))
        return c
    lax.fori_loop(0, n // LANES, it, 0)


def _sigmoid(x):
    return 1.0 / (1.0 + jnp.exp(-x))


def _dsilu(x, s):
    return s * (1.0 + x * (1.0 - s))


def _ln_stats(v):
    mu = jnp.mean(v, axis=-1, keepdims=True)
    c = v - mu
    r = lax.rsqrt(jnp.mean(c * c, axis=-1, keepdims=True) + LN_EPS)
    return c * r, r


def _ln_bwd(dy, xhat, r, g):
    dxh = dy * g
    dv = r * (dxh - jnp.mean(dxh, axis=-1, keepdims=True) - xhat * jnp.mean(dxh * xhat, axis=-1, keepdims=True))
    return dv, jnp.sum(dy * xhat, axis=0, keepdims=True), jnp.sum(dy, axis=0, keepdims=True)


def _dot(a, b, dims=((1,), (0,))):
    return lax.dot_general(a.astype(BF16), b.astype(BF16), (dims, ((), ())), preferred_element_type=F32)


_NT = ((1,), (1,))
_TN = ((0,), (0,))


def _split3(x):
    hi = x.astype(BF16)
    r = x - hi.astype(F32)
    mid = r.astype(BF16)
    return hi, mid, (r - mid.astype(F32)).astype(BF16)


def _dot_sel_b(a, b, dims=((1,), (0,))):
    hi, mid, lo = _split3(a)
    return (_dot(lo, b, dims) + _dot(mid, b, dims)) + _dot(hi, b, dims)


def _dot_sel_a(a, b, dims=((1,), (0,))):
    hi, mid, lo = _split3(b)
    return (_dot(a, lo, dims) + _dot(a, mid, dims)) + _dot(a, hi, dims)


def _mm(a, b, mode, name, out_dtype=F32, add=None, tm=1024, tn=None, tk=1024, exchange=None, a_rows=None):
    assert mode in ("nn", "nt")
    (M, K), N = a.shape, b.shape[1 if mode == "nn" else 0]
    row0 = 0
    if a_rows is not None:
        row0, M = a_rows
        tm = M
        assert row0 % M == 0
    if tn is None:
        tn = next(t for t in (1024, 1408, 512, 256, LANES) if N % t == 0)
    tm, tn, tk = min(tm, M), min(tn, N), min(tk, K)
    assert M % tm == 0 and N % tn == 0 and K % tk == 0, (name, M, N, K)
    grid = (M // tm, N // tn, K // tk)
    nk = grid[2]
    dims = ((1,), (0,)) if mode == "nn" else _NT
    n_in = 2 + (add is not None)
    xbufs, kinds = exchange if exchange is not None else ((), ())
    nx = len(xbufs)

    def body(*refs):
        a_ref, b_ref = refs[:2]
        o_ref = refs[n_in + nx]
        acc = refs[n_in + 2 * nx + 1]
        i, j, k = pl.program_id(0), pl.program_id(1), pl.program_id(2)
        if nx:
            start, finish = _exchange_plan(refs[n_in:n_in + nx], refs[n_in + nx + 1:n_in + 2 * nx + 1], kinds,
                                           *refs[n_in + 2 * nx + 2:])
            pl.when((i == 0) & (j == 0) & (k == 0))(start)

        d = _dot(a_ref[...], b_ref[...], dims)

        def write_out(r):
            if add is not None:
                r = r + refs[2][...]
            o_ref[...] = r.astype(out_dtype)

        if nk == 1:
            write_out(d)
        else:
            @pl.when(k == 0)
            def _():
                acc[...] = d

            @pl.when((k > 0) & (k < nk - 1))
            def _():
                acc[...] += d

            @pl.when(k == nk - 1)
            def _():
                write_out(acc[...] + d)

        if nx:
            pl.when((i == grid[0] - 1) & (j == grid[1] - 1) & (k == nk - 1))(finish)

    a_spec = pl.BlockSpec((tm, tk), lambda i, j, k: (i + row0 // tm, k))
    b_spec = pl.BlockSpec((tn, tk), lambda i, j, k: (j, k)) if mode == "nt" else pl.BlockSpec((tk, tn), lambda i, j, k: (k, j))
    o_spec = pl.BlockSpec((tm, tn), lambda i, j, k: (i, j))
    ins, specs = [a, b], [a_spec, b_spec]
    if add is not None:
        ins.append(add)
        specs.append(o_spec)
    acc_spec = pltpu.VMEM((tm, tn) if nk > 1 else (8, LANES), F32)
    if not nx:
        return pl.pallas_call(
            body, name=name, grid=grid, in_specs=specs, out_specs=o_spec,
            out_shape=S((M, N), out_dtype), scratch_shapes=[acc_spec],
            compiler_params=_params(("parallel", "parallel", "arbitrary")))(*ins)
    return pl.pallas_call(
        body, name=name, grid=grid, in_specs=specs + [_ANY] * nx, out_specs=[o_spec] + [_ANY] * nx,
        out_shape=[S((M, N), out_dtype)] + _exchange_shapes(xbufs, kinds),
        scratch_shapes=[acc_spec] + _exchange_sems(nx),
        compiler_params=_params(("arbitrary", "arbitrary", "arbitrary")))(*ins, *xbufs)


def _ln_emb_fwd(x, g, b, exchange=None):
    T = x.shape[0]

    nt = T // TB
    xbufs, kinds = exchange if exchange is not None else ((), ())
    nx = len(xbufs)

    def body(*refs):
        x_ref, g_ref, b_ref = refs[:3]
        h_ref, hb_ref, hbt_ref = refs[3 + nx:6 + nx]
        i = pl.program_id(0)
        if nx:
            start, finish = _exchange_plan(refs[3:3 + nx], refs[6 + nx:6 + 2 * nx], kinds, *refs[6 + 2 * nx:])
            pl.when(i == 0)(start)

        def rows(rs):
            xh, _ = _ln_stats(x_ref[rs, :])
            h = xh * g_ref[...] + b_ref[...]
            h_ref[rs, :] = h
            hb_ref[rs, :] = h.astype(BF16)
        _row_loop(TB, rows)
        hbt_ref[...] = hb_ref[...].T
        if nx:
            pl.when(i == nt - 1)(finish)

    return pl.pallas_call(
        body, name="ln_emb_fwd", grid=(nt,),
        in_specs=[_row(TB, D), _full((1, D)), _full((1, D))] + [_ANY] * nx,
        out_specs=[_row(TB, D), _row(TB, D), _colt(D, TB)] + [_ANY] * nx,
        out_shape=[S((T, D), F32), S((T, D), BF16), S((D, T), BF16)] + _exchange_shapes(xbufs, kinds),
        scratch_shapes=_exchange_sems(nx) if nx else [],
        compiler_params=_params(("arbitrary",)))(x, g, b, *xbufs)


def _out_proj_post1(mix, w_out, h0, b_out, g, b, tm=512, tk=1024):
    T, K = mix.shape
    tm = min(tm, T)
    nk = K // tk
    assert T % tm == 0 and K % tk == 0 and nk >= 2

    def body(mix_ref, w_ref, h0_ref, bo_ref, g_ref, b_ref, out_ref, h_ref, hb_ref, hbt_ref, acc):
        k = pl.program_id(1)
        d = _dot(mix_ref[...], w_ref[...])

        @pl.when(k == 0)
        def _():
            acc[...] = d

        @pl.when((k > 0) & (k < nk - 1))
        def _():
            acc[...] += d

        @pl.when(k == nk - 1)
        def _():
            out_ref[...] = acc[...] + d

            def rows(rs):
                xh, _ = _ln_stats(ALPHA * h0_ref[rs, :] + out_ref[rs, :] + bo_ref[...])
                h = xh * g_ref[...] + b_ref[...]
                h_ref[rs, :] = h
                hb_ref[rs, :] = h.astype(BF16)
            _row_loop(tm, rows)
            hbt_ref[...] = hb_ref[...].T

    rowt = lambda n: pl.BlockSpec((tm, n), lambda i, k: (i, 0))
    const = pl.BlockSpec((1, D), lambda i, k: (0, 0))
    return pl.pallas_call(
        body, name="out_proj_post1", grid=(T // tm, nk),
        in_specs=[pl.BlockSpec((tm, tk), lambda i, k: (i, k)), pl.BlockSpec((tk, D), lambda i, k: (k, 0)), rowt(D),
                  const, const, const],
        out_specs=[rowt(D), rowt(D), rowt(D), pl.BlockSpec((D, tm), lambda i, k: (0, i))],
        out_shape=[S((T, D), F32), S((T, D), F32), S((T, D), BF16), S((D, T), BF16)],
        scratch_shapes=[pltpu.VMEM((tm, D), F32)],
        compiler_params=_params(("parallel", "arbitrary")))(mix, w_out, h0, b_out, g, b)


def _ple_post2(h1b, w_gate, pb, w_proj, h1, tgt, g, b, tm=512):
    T = h1.shape[0]
    tm = min(tm, T)
    assert T % tm == 0

    def body(h1b_ref, wg_ref, pb_ref, wp_ref, h1_ref, tgt_ref, g_ref, b_ref,
             dh1_ref, dgp_ref, dple_ref, loss_ref, dg_ref, db_ref, gp_ref, ple_ref):
        @pl.when(pl.program_id(0) == 0)
        def _():
            loss_ref[...] = jnp.zeros_like(loss_ref)
            dg_ref[...] = jnp.zeros_like(dg_ref)
            db_ref[...] = jnp.zeros_like(db_ref)

        gp_ref[...] = _dot(h1b_ref[...], wg_ref[...])
        ple_ref[...] = _dot(pb_ref[...], wp_ref[...])

        def rows(rs):
            gate = _sigmoid(gp_ref[rs, :])
            ple = ple_ref[rs, :]
            xh, r = _ln_stats(ALPHA * h1_ref[rs, :] + gate * ple)
            err = xh * g_ref[...] + b_ref[...] - tgt_ref[rs, :]
            loss_ref[...] += 0.5 * jnp.sum(jnp.mean(err * err, axis=-1, keepdims=True), axis=0, keepdims=True)
            dv, dg, db = _ln_bwd(err * (1.0 / D), xh, r, g_ref[...])
            dg_ref[...] += dg
            db_ref[...] += db
            dh1_ref[rs, :] = ALPHA * dv
            dgp_ref[rs, :] = (dv * ple * gate * (1.0 - gate)).astype(BF16)
            dple_ref[rs, :] = (dv * gate).astype(BF16)
        _row_loop(tm, rows)

    return pl.pallas_call(
        body, name="ple_post2", grid=(T // tm,),
        in_specs=[_row(tm, D), _full((D, D)), _row(tm, D_PLE), _full((D_PLE, D)), _row(tm, D), _row(tm, D),
                  _full((1, D)), _full((1, D))],
        out_specs=[_row(tm, D)] * 3 + [_full((8, LANES)), _full((1, D)), _full((1, D))],
        out_shape=[S((T, D), F32), S((T, D), BF16), S((T, D), BF16), S((8, LANES), F32), S((1, D), F32), S((1, D), F32)],
        scratch_shapes=[pltpu.VMEM((tm, D), F32), pltpu.VMEM((tm, D), F32)],
        compiler_params=_params(("arbitrary",)))(h1b, w_gate, pb, w_proj, h1, tgt, g, b)


def _d_h1_post1_bwd(dgp, w_gate, dh1a, h0, out, b_out, g, tm=512):
    T = h0.shape[0]
    tm = min(tm, T)
    assert T % tm == 0

    def body(dgp_ref, wg_ref, da_ref, h0_ref, out_ref, bo_ref, g_ref, dout_ref, dh0_ref, dg_ref, db_ref, dbo_ref, dh1):
        @pl.when(pl.program_id(0) == 0)
        def _():
            dg_ref[...] = jnp.zeros_like(dg_ref)
            db_ref[...] = jnp.zeros_like(db_ref)
            dbo_ref[...] = jnp.zeros_like(dbo_ref)

        dh1[...] = da_ref[...] + _dot(dgp_ref[...], wg_ref[...], _NT)

        def rows(rs):
            xh, r = _ln_stats(ALPHA * h0_ref[rs, :] + out_ref[rs, :] + bo_ref[...])
            dv, dg, db = _ln_bwd(dh1[rs, :], xh, r, g_ref[...])
            dg_ref[...] += dg
            db_ref[...] += db
            dbo_ref[...] += jnp.sum(dv, axis=0, keepdims=True)
            dout_ref[rs, :] = dv.astype(BF16)
            dh0_ref[rs, :] = ALPHA * dv
        _row_loop(tm, rows)

    return pl.pallas_call(
        body, name="d_h1_post1_bwd", grid=(T // tm,),
        in_specs=[_row(tm, D), _full((D, D))] + [_row(tm, D)] * 3 + [_full((1, D))] * 2,
        out_specs=[_row(tm, D)] * 2 + [_full((1, D))] * 3,
        out_shape=[S((T, D), BF16), S((T, D), F32)] + [S((1, D), F32)] * 3,
        scratch_shapes=[pltpu.VMEM((tm, D), F32)],
        compiler_params=_params(("arbitrary",)))(dgp, w_gate, dh1a, h0, out, b_out, g)


def _d_h0_ln_bwd(dproj, w_main, ddtr, w_dt, dh0a, x, g, exchange=None, tm=1024, tk=1408):
    T, K = dproj.shape
    tm = min(tm, T)
    assert T % tm == 0 and K % tk == 0
    ni, nk = T // tm, K // tk
    xbufs, kinds = exchange if exchange is not None else ((), ())
    nx = len(xbufs)

    def body(*refs):
        dp_ref, w_ref, dt_ref, wdt_ref, da_ref, x_ref, g_ref = refs[:7]
        dx_ref, dg_ref, db_ref = refs[7 + nx:10 + nx]
        acc = refs[10 + 2 * nx]
        i, k = pl.program_id(0), pl.program_id(1)
        if nx:
            start, finish = _exchange_plan(refs[7:7 + nx], refs[10 + nx:10 + 2 * nx], kinds, *refs[11 + 2 * nx:])
            pl.when((i == 0) & (k == 0))(start)

        @pl.when((i == 0) & (k == 0))
        def _():
            dg_ref[...] = jnp.zeros_like(dg_ref)
            db_ref[...] = jnp.zeros_like(db_ref)

        d = _dot(dp_ref[...], w_ref[...], _NT)

        @pl.when(k == 0)
        def _():
            acc[...] = da_ref[...] + _dot(dt_ref[...], wdt_ref[...], _NT) + d

        @pl.when(k > 0)
        def _():
            acc[...] += d

        @pl.when(k == nk - 1)
        def _():
            def rows(rs):
                xh, r = _ln_stats(x_ref[rs, :])
                dv, dg, db = _ln_bwd(acc[rs, :], xh, r, g_ref[...])
                dg_ref[...] += dg
                db_ref[...] += db
                dx_ref[rs, :] = dv
            _row_loop(tm, rows)

        if nx:
            pl.when((i == ni - 1) & (k == nk - 1))(finish)

    rowt = lambda n: pl.BlockSpec((tm, n), lambda i, k: (i, 0))
    const = lambda shape: pl.BlockSpec(shape, lambda i, k: (0, 0))
    return pl.pallas_call(
        body, name="d_h0_ln_bwd", grid=(ni, nk),
        in_specs=[pl.BlockSpec((tm, tk), lambda i, k: (i, k)), pl.BlockSpec((D, tk), lambda i, k: (0, k)),
                  rowt(LANES), const((D, LANES)), rowt(D), rowt(D), const((1, D))] + [_ANY] * nx,
        out_specs=[rowt(D), const((1, D)), const((1, D))] + [_ANY] * nx,
        out_shape=[S((T, D), F32), S((1, D), F32), S((1, D), F32)] + _exchange_shapes(xbufs, kinds),
        scratch_shapes=[pltpu.VMEM((tm, D), F32)] + (_exchange_sems(nx) if nx else []),
        compiler_params=_params(("arbitrary", "arbitrary")))(dproj, w_main, ddtr, w_dt, dh0a, x, g, *xbufs)


def _softplus(x):
    return jnp.maximum(x, 0.0) + jnp.log1p(jnp.exp(-jnp.abs(x)))


def _ssd_pre_fwd(proj, dt_raw, wx, wb, bx, bb, dt_bias):
    T = proj.shape[0]
    H = HALO_SSM

    def body(xs_ref, xsp_ref, bc_ref, bcp_ref, dtr_ref, wx_ref, wb_ref, bx_ref, bb_ref, dtb_ref,
             xso_ref, bco_ref, dto_ref, extx, extb):
        first = pl.program_id(0) == 0

        def conv(t_ref, p_ref, w_ref, b_ref, o_ref, ext, n):
            def blk(cols):
                ext[0:H, cols] = jnp.where(first, 0.0, p_ref[:, cols])
                ext[H:, cols] = t_ref[:, cols]
                for r0 in range(0, TB, 64):
                    acc = jnp.broadcast_to(b_ref[:, cols], (64, LANES))
                    for k in range(K_SSM):
                        acc = acc + w_ref[k:k + 1, cols] * ext[pl.ds(r0 + H - (K_SSM - 1) + k, 64), cols]
                    o_ref[pl.ds(r0, 64), cols] = acc * _sigmoid(acc)
            _col_loop(n, blk)

        conv(xs_ref, xsp_ref, wx_ref, bx_ref, xso_ref, extx, D_SSM)
        conv(bc_ref, bcp_ref, wb_ref, bb_ref, bco_ref, extb, 512)
        dto_ref[...] = _softplus(dtr_ref[...] + dtb_ref[...])

    return pl.pallas_call(
        body, name="ssd_pre_fwd", grid=(T // TB,),
        in_specs=[_row(TB, 1024, C_XS), _prev(TB, H, 1024, C_XS), _row(TB, 512, C_BC), _prev(TB, H, 512, C_BC),
                  _row(TB, LANES), _full((K_SSM, 1024)), _full((K_SSM, 512)), _full((1, 1024)), _full((1, 512)),
                  _full((1, LANES))],
        out_specs=[_row(TB, 1024), _row(TB, 512), _row(TB, LANES)],
        out_shape=[S((T, 1024), F32), S((T, 512), F32), S((T, LANES), F32)],
        scratch_shapes=[pltpu.VMEM((H + TB, 1024), F32), pltpu.VMEM((H + TB, 512), F32)],
        compiler_params=_params(("parallel",)))(proj, proj, proj, proj, dt_raw, wx, wb, bx, bb, dt_bias)


def _ssd_conv_bwd(dproj, proj, d_c, w, b, n, col, name):
    TB = TB_SSD_CONV_BWD
    T = proj.shape[0]
    nt = T // TB
    H = HALO_SSM
    R = TB + H

    def body(dproj_ref, t_ref, p_ref, n_ref, d_ref, dn_ref, w_ref, b_ref, o_ref, dw_ref, dbias_ref, ext, dp):
        i = pl.program_id(0)
        first, last = i == 0, i == nt - 1

        @pl.when(first)
        def _():
            dw_ref[...] = jnp.zeros_like(dw_ref)
            dbias_ref[...] = jnp.zeros_like(dbias_ref)

        def blk(cols):
            ext[0:H, cols] = jnp.where(first, 0.0, p_ref[:, cols])
            ext[H:H + TB, cols] = t_ref[:, cols]
            ext[H + TB:, cols] = n_ref[:, cols]
            def taps_and_dsilu(r0, rows):
                taps = [ext[pl.ds(r0 + H - (K_SSM - 1) + k, rows), cols] for k in range(K_SSM)]
                pre = jnp.broadcast_to(b_ref[:, cols], (rows, LANES))
                for k in range(K_SSM):
                    pre = pre + w_ref[k:k + 1, cols] * taps[k]
                return taps, _dsilu(pre, _sigmoid(pre))

            for r0 in range(0, TB, 64):
                taps, ds = taps_and_dsilu(r0, 64)
                dpt = d_ref[pl.ds(r0, 64), cols] * ds
                dp[pl.ds(r0, 64), cols] = dpt
                dbias_ref[:, cols] += jnp.sum(dpt, axis=0, keepdims=True)
                for k in range(K_SSM):
                    dw_ref[k:k + 1, cols] += jnp.sum(dpt * taps[k], axis=0, keepdims=True)
            dp[TB:, cols] = jnp.where(last, 0.0, dn_ref[:, cols] * taps_and_dsilu(TB, H)[1])
            for r0 in range(0, TB, 64):
                acc = jnp.zeros((64, LANES), F32)
                for k in range(K_SSM):
                    acc = acc + w_ref[k:k + 1, cols] * dp[pl.ds(r0 + K_SSM - 1 - k, 64), cols]
                o_ref[pl.ds(r0, 64), cols] = acc.astype(BF16)
        _col_loop(n, blk)

    return pl.pallas_call(
        body, name=name, grid=(nt,),
        in_specs=[_ANY, _row(TB, n, col), _prev(TB, H, n, col), _next(TB, H, n, nt, col),
                  _row(TB, n), _next(TB, H, n, nt), _full((K_SSM, n)), _full((1, n))],
        out_specs=[_row(TB, n, col), _full((K_SSM, n)), _full((1, n))],
        out_shape=[S(dproj.shape, BF16), S((K_SSM, n), F32), S((1, n), F32)],
        input_output_aliases={0: 0},
        scratch_shapes=[pltpu.VMEM((H + TB + H, n), F32), pltpu.VMEM((R, n), F32)],
        compiler_params=_params(("arbitrary",)))(dproj, proj, proj, proj, d_c, d_c, w, b)


def _ssd_consts():
    ex = np.zeros((LANES, D_SSM), np.float32)
    for h in range(N_HEADS):
        ex[h, h * HEAD:(h + 1) * HEAD] = 1.0
    tri = np.tril(np.ones((CHUNK, CHUNK), np.float32))
    return jnp.asarray(ex), jnp.asarray(ex.T.copy()), jnp.asarray(tri), jnp.asarray(tri.T.copy())


def _ssd_common(xs, dt, alog_ref, ex_ref, tri_ref):
    lane = lax.broadcasted_iota(jnp.int32, (1, LANES), 1)
    a = jnp.where(lane < N_HEADS, -jnp.exp(alog_ref[...]), 0.0)
    A = _dot_sel_a(tri_ref[...], dt * a)
    ex = ex_ref[...]
    Aex = _dot_sel_b(A, ex)
    dtex = _dot_sel_b(dt, ex)
    expA = jnp.exp(Aex)
    dec = jnp.exp(Aex[CHUNK - 1:CHUNK, :] - Aex)
    cd = _dot_sel_a(ex, jnp.broadcast_to(jnp.exp(A.T[:, CHUNK - 1:CHUNK]), (LANES, LANES)), _TN)
    return a, A, dtex, expA, dec, cd


def _decay_mask():
    sub = lax.broadcasted_iota(jnp.int32, (CHUNK, CHUNK), 0)
    lane = lax.broadcasted_iota(jnp.int32, (CHUNK, CHUNK), 1)
    return sub, lane, sub >= lane


def _ssd_fwd(xs_c, bc_c, dt, proj, alog, dskip_row, norm_g):
    T = xs_c.shape[0]
    nc = T // CHUNK
    ex, _, tri, _ = _ssd_consts()

    def body(xs_ref, bc_ref, dt_ref, z_ref, alog_ref, dsk_ref, ng_ref, ex_ref, tri_ref,
             ys_ref, ypre_ref, hprev_ref, yst_ref, Hs, ybuf):
        @pl.when(pl.program_id(0) == 0)
        def _():
            Hs[...] = jnp.zeros_like(Hs)

        hprev_ref[0] = Hs[...]
        xs, dt = xs_ref[...], dt_ref[...]
        a, A, dtex, expA, dec, cd = _ssd_common(xs, dt, alog_ref, ex_ref, tri_ref)
        AT = A.T
        xdt = xs * dtex
        xdec = xdt * dec
        _, _, causal = _decay_mask()
        for g in range(2):
            gs = slice(g * 512, (g + 1) * 512)
            B = bc_ref[:, g * N_STATE:(g + 1) * N_STATE]
            C = bc_ref[:, 256 + g * N_STATE:256 + (g + 1) * N_STATE]
            cb = _dot(C, B, _NT)
            Hg = Hs[gs, :]
            yoff = _dot(C, Hg, _NT) * expA[:, gs]
            for j in range(8):
                h = g * 8 + j
                hs = slice(h * HEAD, (h + 1) * HEAD)
                L = jnp.exp(jnp.where(causal, A[:, h:h + 1] - AT[h:h + 1, :], -1e30))
                ybuf[:, hs] = _dot(cb * L, xdt[:, hs]) + yoff[:, j * HEAD:(j + 1) * HEAD]
            Hs[gs, :] = cd[gs, :] * Hg + _dot(xdec[:, gs], B, _TN)
        ypre = ybuf[...] + dsk_ref[...] * xs
        ypre_ref[...] = ypre
        z = z_ref[...]
        yz = ypre * (z * _sigmoid(z))
        for g in range(2):
            gs = slice(g * 512, (g + 1) * 512)
            v = yz[:, gs]
            r = lax.rsqrt(jnp.mean(v * v, axis=-1, keepdims=True) + RMS_EPS)
            ys_ref[:, gs] = (v * r * ng_ref[:, gs]).astype(BF16)
        yst_ref[...] = ys_ref[...].T

    return pl.pallas_call(
        body, name="ssd_fwd", grid=(nc,),
        in_specs=[_row(CHUNK, 1024), _row(CHUNK, 512), _row(CHUNK, LANES), _row(CHUNK, 1024, C_Z),
                  _full((1, LANES)), _full((1, 1024)), _full((1, 1024)), _full((LANES, 1024)), _full((CHUNK, CHUNK))],
        out_specs=[_row(CHUNK, 1024), _row(CHUNK, 1024), pl.BlockSpec((1, 1024, N_STATE), lambda c: (c, 0, 0)),
                   _colt(1024, CHUNK)],
        out_shape=[S((T, 2048), BF16), S((T, 1024), F32), S((nc, 1024, N_STATE), F32), S((2048, T), BF16)],
        scratch_shapes=[pltpu.VMEM((1024, N_STATE), F32), pltpu.VMEM((CHUNK, 1024), F32)],
        compiler_params=_params(("arbitrary",)))(xs_c, bc_c, dt, proj, alog, dskip_row, norm_g, ex, tri)


def _ssd_bwd(dproj, xs_c, bc_c, dt, dt_raw, dt_bias, proj, ypre, hprev, dmix, alog, dskip_row, norm_g, exchange=None):
    T = xs_c.shape[0]
    nc = T // CHUNK
    ex, ext, tri, triu = _ssd_consts()
    rev = lambda n, col=0: pl.BlockSpec((CHUNK, n), lambda c: (nc - 1 - c, col))
    xbufs, kinds = exchange if exchange is not None else ((), ())
    nx = len(xbufs)
    N_IN, N_OUT = 17, 8

    def body(*refs):
        (dproj_ref, xs_ref, bc_ref, dt_ref, dtr_ref, dtb_ref, z_ref, ypre_ref, hprev_ref, dys_ref, alog_ref, dsk_ref,
         ng_ref, ex_ref, ext_ref, tri_ref, triu_ref) = refs[:N_IN]
        (dxs_ref, dbc_ref, ddt_ref, dz_ref, dng_ref, ddsk_ref, dalog_ref,
         ddtb_ref) = refs[N_IN + nx:N_IN + nx + N_OUT]
        dHs, dxbuf, dskacc = refs[N_IN + N_OUT + 2 * nx:N_IN + N_OUT + 2 * nx + 3]
        c = pl.program_id(0)
        if nx:
            start, finish = _exchange_plan(refs[N_IN:N_IN + nx], refs[N_IN + nx + N_OUT:N_IN + N_OUT + 2 * nx], kinds,
                                           *refs[N_IN + N_OUT + 2 * nx + 3:])
            pl.when(c == 0)(start)

        @pl.when(c == 0)
        def _():
            dHs[...] = jnp.zeros_like(dHs)
            dng_ref[...] = jnp.zeros_like(dng_ref)
            dalog_ref[...] = jnp.zeros_like(dalog_ref)
            ddtb_ref[...] = jnp.zeros_like(ddtb_ref)
            dskacc[...] = jnp.zeros_like(dskacc)

        xs, dt, z, ypre, dys = xs_ref[...], dt_ref[...], z_ref[...], ypre_ref[...], dys_ref[...]
        sg = _sigmoid(z)
        sz = z * sg
        yz = ypre * sz
        dyz_parts = []
        for g in range(2):
            gs = slice(g * 512, (g + 1) * 512)
            v = yz[:, gs]
            r = lax.rsqrt(jnp.mean(v * v, axis=-1, keepdims=True) + RMS_EPS)
            vn = v * r
            dng_ref[:, gs] += jnp.sum(dys[:, gs] * vn, axis=0, keepdims=True)
            dvn = dys[:, gs] * ng_ref[:, gs]
            dyz_parts.append(r * (dvn - vn * jnp.mean(dvn * vn, axis=-1, keepdims=True)))
        dyz = jnp.concatenate(dyz_parts, axis=1)
        dy = dyz * sz
        dz_ref[...] = (dyz * ypre * _dsilu(z, sg)).astype(BF16)
        dskacc[...] += jnp.sum(dy * xs, axis=0, keepdims=True)

        a, A, dtex, expA, dec, cd = _ssd_common(xs, dt, alog_ref, ex_ref, tri_ref)
        AT = A.T
        xdt = xs * dtex
        xdec = xdt * dec
        dye = dy * expA
        H = hprev_ref[0]
        dHn = dHs[...]
        sub, lane, causal = _decay_mask()
        dAc = jnp.zeros((CHUNK, LANES), F32)
        Rm = jnp.zeros((CHUNK, LANES), F32)
        yoff_parts, q_parts = [], []
        for g in range(2):
            gs = slice(g * 512, (g + 1) * 512)
            B = bc_ref[:, g * N_STATE:(g + 1) * N_STATE]
            C = bc_ref[:, 256 + g * N_STATE:256 + (g + 1) * N_STATE]
            cb = _dot(C, B, _NT)
            Hg, dHg = H[gs, :], dHn[gs, :]
            Q = _dot(B, dHg, _NT)
            yoff_parts.append(_dot(C, Hg, _NT) * expA[:, gs])
            q_parts.append(Q)
            dcb = jnp.zeros((CHUNK, CHUNK), F32)
            for j in range(8):
                h = g * 8 + j
                hs = slice(h * HEAD, (h + 1) * HEAD)
                L = jnp.exp(jnp.where(causal, A[:, h:h + 1] - AT[h:h + 1, :], -1e30))
                M = cb * L
                G = _dot(dy[:, hs], xdt[:, hs], _NT)
                dxbuf[:, hs] = _dot(M, dy[:, hs], _TN)
                dcb = dcb + G * L
                E = G * M
                dAc = jnp.where(lane == h, jnp.sum(E, axis=1, keepdims=True), dAc)
                Rm = jnp.where(sub == h, jnp.sum(E, axis=0, keepdims=True), Rm)
            dbc_ref[:, g * N_STATE:(g + 1) * N_STATE] = _dot(dcb, C, _TN) + _dot(xdec[:, gs], dHg)
            dbc_ref[:, 256 + g * N_STATE:256 + (g + 1) * N_STATE] = _dot(dcb, B) + _dot(dye[:, gs], Hg)
            dHs[gs, :] = cd[gs, :] * dHg + _dot(dye[:, gs], C, _TN)
        yoff = jnp.concatenate(yoff_parts, axis=1)
        Qd = jnp.concatenate(q_parts, axis=1) * dec
        dxdt = dxbuf[...] + Qd
        extm = ext_ref[...]
        red_s = _dot_sel_b(xdt * Qd, extm)
        dA = dAc - Rm.T + _dot_sel_b(dy * yoff, extm) - red_s
        hd = jnp.sum(_dot_sel_b(H * dHn, extm, _TN), axis=0, keepdims=True)
        last_add = jnp.sum(red_s, axis=0, keepdims=True) + jnp.exp(A[CHUNK - 1:CHUNK, :]) * hd
        dA = dA + jnp.where(sub == CHUNK - 1, last_add, 0.0)
        dadt = _dot_sel_a(triu_ref[...], dA)
        ddtr = (dadt * a + _dot_sel_b(dxdt * xs, extm)) * _sigmoid(dtr_ref[...] + dtb_ref[...])
        ddt_ref[...] = ddtr.astype(BF16)
        ddtb_ref[...] += jnp.sum(ddtr, axis=0, keepdims=True)
        dalog_ref[...] += jnp.sum(dadt * dt, axis=0, keepdims=True) * a
        dxs_ref[...] = dxdt * dtex + dsk_ref[...] * dy

        @pl.when(c == nc - 1)
        def _():
            ddsk_ref[...] = _dot_sel_b(jnp.broadcast_to(dskacc[...], (8, 1024)), extm)[0:1, :]

        if nx:
            pl.when(c == nc - 1)(finish)

    return pl.pallas_call(
        body, name="ssd_bwd", grid=(nc,),
        in_specs=[_ANY, rev(1024), rev(512), rev(LANES), rev(LANES), _full((1, LANES)), rev(1024, C_Z), rev(1024),
                  pl.BlockSpec((1, 1024, N_STATE), lambda c: (nc - 1 - c, 0, 0)), rev(1024, 0),
                  _full((1, LANES)), _full((1, 1024)), _full((1, 1024)),
                  _full((LANES, 1024)), _full((1024, LANES)), _full((CHUNK, CHUNK)), _full((CHUNK, CHUNK))] + [_ANY] * nx,
        out_specs=[rev(1024), rev(512), rev(LANES), rev(1024, C_Z), _full((1, 1024)), _full((1, LANES)),
                   _full((1, LANES)), _full((1, LANES))] + [_ANY] * nx,
        out_shape=[S((T, 1024), F32), S((T, 512), F32), S((T, LANES), BF16), S(dproj.shape, BF16),
                   S((1, 1024), F32), S((1, LANES), F32), S((1, LANES), F32), S((1, LANES), F32)]
        + _exchange_shapes(xbufs, kinds),
        input_output_aliases={0: 3},
        scratch_shapes=[pltpu.VMEM((1024, N_STATE), F32), pltpu.VMEM((CHUNK, 1024), F32), pltpu.VMEM((1, 1024), F32)]
        + (_exchange_sems(nx) if nx else []),
        compiler_params=_params(("arbitrary",)))(
            dproj, xs_c, bc_c, dt, dt_raw, dt_bias, proj, ypre, hprev, dmix, alog, dskip_row, norm_g, ex, ext, tri, triu,
            *xbufs)


def _shifted_copies(ext, ext8):
    n = ext8.shape[1]
    for r in range(8):
        ext8[r] = ext[pl.ds(r, n), :]


def _shifted(ext8, off, rows):
    return ext8[off % 8, pl.ds(off - off % 8, rows), :]


def _conf_fwd(mix, mixt, proj, w, cb, lg, lb, ba, bb):
    T = proj.shape[0]
    H = HALO_CONF

    def body(mix_ref, mixt_ref, ga_ref, gap_ref, gb_ref, gbp_ref, cg_ref, w_ref, cb_ref, lg_ref, lb_ref, ba_ref,
             bb_ref, u1_ref, yc_ref, yct_ref, ext, ext8):
        first = pl.program_id(0) == 0
        ext[H + TB:, :] = jnp.zeros((8, LANES), F32)

        def blk(cols):
            up = (gap_ref[:, cols] + ba_ref[:, cols]) * _sigmoid(gbp_ref[:, cols] + bb_ref[:, cols])
            ext[0:H, :] = jnp.where(first, 0.0, up)
            ext[H:H + TB, :] = (ga_ref[:, cols] + ba_ref[:, cols]) * _sigmoid(gb_ref[:, cols] + bb_ref[:, cols])
            _shifted_copies(ext, ext8)
            for r0 in range(0, TB, 64):
                acc = jnp.broadcast_to(cb_ref[:, cols], (64, LANES))
                for k in range(K_CONF):
                    acc = acc + w_ref[k:k + 1, cols] * _shifted(ext8, r0 + H - (K_CONF - 1) + k, 64)
                u1_ref[pl.ds(r0, 64), cols] = acc
        _col_loop(D_CONF, blk)

        def rows(rs):
            xh, _ = _ln_stats(u1_ref[rs, :])
            u2 = xh * lg_ref[...] + lb_ref[...]
            cg = cg_ref[rs, :]
            yc_ref[rs, :] = (u2 * _sigmoid(u2) * cg * _sigmoid(cg)).astype(BF16)
        _row_loop(TB, rows)
        yct_ref[...] = yc_ref[...].T

    return pl.pallas_call(
        body, name="conf_fwd", grid=(T // TB,),
        in_specs=[_ANY, _ANY, _row(TB, 1024, C_GLUA), _prev(TB, H, 1024, C_GLUA), _row(TB, 1024, C_GLUB),
                  _prev(TB, H, 1024, C_GLUB), _row(TB, 1024, C_CG), _full((K_CONF, 1024))] + [_full((1, 1024))] * 5,
        out_specs=[_row(TB, 1024), _row(TB, 1024, 1), _colt(1024, TB, 1)],
        out_shape=[S((T, 1024), F32), S((T, 2048), BF16), S((2048, T), BF16)],
        input_output_aliases={0: 1, 1: 2},
        scratch_shapes=[pltpu.VMEM((H + TB + 8, LANES), F32), pltpu.VMEM((8, H + TB, LANES), F32)],
        compiler_params=_params(("parallel",)))(mix, mixt, proj, proj, proj, proj, proj, w, cb, lg, lb, ba, bb)


def _conf_bwd1(dmix, u1, proj, lg, lb):
    T = u1.shape[0]

    def body(dy_ref, u1_ref, cg_ref, lg_ref, lb_ref, du1_ref, dcg_ref, dg_ref, db_ref):
        @pl.when(pl.program_id(0) == 0)
        def _():
            dg_ref[...] = jnp.zeros_like(dg_ref)
            db_ref[...] = jnp.zeros_like(db_ref)

        def rows(rs):
            xh, r = _ln_stats(u1_ref[rs, :])
            u2 = xh * lg_ref[...] + lb_ref[...]
            s2 = _sigmoid(u2)
            cg = cg_ref[rs, :]
            sc = _sigmoid(cg)
            dy = dy_ref[rs, :]
            dcg_ref[rs, :] = (dy * u2 * s2 * _dsilu(cg, sc)).astype(BF16)
            dv, dg, db = _ln_bwd(dy * cg * sc * _dsilu(u2, s2), xh, r, lg_ref[...])
            dg_ref[...] += dg
            db_ref[...] += db
            du1_ref[rs, :] = dv
        _row_loop(TB, rows)

    return pl.pallas_call(
        body, name="conf_bwd1", grid=(T // TB,),
        in_specs=[_row(TB, 1024, 1), _row(TB, 1024), _row(TB, 1024, C_CG), _full((1, 1024)), _full((1, 1024))],
        out_specs=[_row(TB, 1024), _row(TB, 1024, C_CG), _full((1, 1024)), _full((1, 1024))],
        out_shape=[S((T, 1024), F32), S((T, N_MAIN), BF16), S((1, 1024), F32), S((1, 1024), F32)],
        compiler_params=_params(("arbitrary",)))(dmix, u1, proj, lg, lb)


def _conf_bwd2(dproj, proj, du1, w, ba, bb):
    T = du1.shape[0]
    nt = T // TB
    H = HALO_CONF

    def body(dproj_ref, ga_ref, gap_ref, gb_ref, gbp_ref, du_ref, dun_ref, w_ref, ba_ref, bb_ref,
             dg_ref, dw_ref, dcb_ref, dba_ref, dbb_ref, ext, dext, ext8, dext8, dwacc):
        i = pl.program_id(0)
        first, last = i == 0, i == nt - 1

        @pl.when(first)
        def _():
            for r in (dcb_ref, dba_ref, dbb_ref, dwacc):
                r[...] = jnp.zeros_like(r)

        ext[H + TB:, :] = jnp.zeros((8, LANES), F32)
        dext[H + TB:, :] = jnp.zeros((8, LANES), F32)

        def blk(cols):
            cols_b = pl.ds(pl.multiple_of(cols.start + D_CONF, LANES), LANES)
            up = (gap_ref[:, cols] + ba_ref[:, cols]) * _sigmoid(gbp_ref[:, cols] + bb_ref[:, cols])
            ext[0:H, :] = jnp.where(first, 0.0, up)
            a = ga_ref[:, cols] + ba_ref[:, cols]
            sb = _sigmoid(gb_ref[:, cols] + bb_ref[:, cols])
            ext[H:H + TB, :] = a * sb
            du = du_ref[:, cols]
            dext[0:TB, :] = du
            dext[TB:TB + H, :] = jnp.where(last, 0.0, dun_ref[:, cols])
            _shifted_copies(ext, ext8)
            _shifted_copies(dext, dext8)
            dcb_ref[:, cols] += jnp.sum(du, axis=0, keepdims=True)
            for r0 in range(0, TB, 64):
                dur = du_ref[pl.ds(r0, 64), cols]
                acc = jnp.zeros((64, LANES), F32)
                for k in range(K_CONF):
                    prod = dur * _shifted(ext8, r0 + H - (K_CONF - 1) + k, 64)
                    dwacc[k * 8:(k + 1) * 8, cols] += prod.reshape(8, 8, LANES).sum(axis=0)
                    acc = acc + w_ref[k:k + 1, cols] * _shifted(dext8, r0 + K_CONF - 1 - k, 64)
                ar, sr = a[r0:r0 + 64], sb[r0:r0 + 64]
                da = acc * sr
                dbv = acc * ar * sr * (1.0 - sr)
                dg_ref[pl.ds(r0, 64), cols] = da.astype(BF16)
                dg_ref[pl.ds(r0, 64), cols_b] = dbv.astype(BF16)
                dba_ref[:, cols] += jnp.sum(da, axis=0, keepdims=True)
                dbb_ref[:, cols] += jnp.sum(dbv, axis=0, keepdims=True)
        _col_loop(D_CONF, blk)

        @pl.when(last)
        def _():
            dw_ref[...] = jnp.sum(dwacc[...].reshape(K_CONF, 8, D_CONF), axis=1)

    return pl.pallas_call(
        body, name="conf_bwd2", grid=(nt,),
        in_specs=[_ANY, _row(TB, 1024, C_GLUA), _prev(TB, H, 1024, C_GLUA), _row(TB, 1024, C_GLUB),
                  _prev(TB, H, 1024, C_GLUB), _row(TB, 1024), _next(TB, H, 1024, nt), _full((K_CONF, 1024)),
                  _full((1, 1024)), _full((1, 1024))],
        out_specs=[_row(TB, 2048), _full((K_CONF, 1024)), _full((1, 1024)), _full((1, 1024)), _full((1, 1024))],
        out_shape=[S(dproj.shape, BF16), S((K_CONF, 1024), F32)] + [S((1, 1024), F32)] * 3,
        input_output_aliases={0: 0},
        scratch_shapes=[pltpu.VMEM((H + TB + 8, LANES), F32), pltpu.VMEM((TB + H + 8, LANES), F32),
                        pltpu.VMEM((8, H + TB, LANES), F32), pltpu.VMEM((8, TB + H, LANES), F32),
                        pltpu.VMEM((K_CONF * 8, D_CONF), F32)],
        compiler_params=_params(("arbitrary",)))(dproj, proj, proj, proj, proj, du1, du1, w, ba, bb)


def _mesh_pos():
    x, y, c = lax.axis_index("x"), lax.axis_index("y"), lax.axis_index("c")
    return x, y, c, 4 * x + 2 * y + c


def _peer(x, y, c, k):
    return (x ^ ((k >> 2) & 1), y ^ ((k >> 1) & 1), c ^ (k & 1))


def _exchange_copies(ins, outs, kinds, send, recv, loc):
    nb = len(ins)
    x, y, c, me = _mesh_pos()
    src = lambda b, d: ins[b].at[d] if kinds[b] == "blocks" else ins[b]
    copies = [pltpu.make_async_copy(src(b, me), outs[b].at[me], loc.at[b]) for b in range(nb)]
    for k in range(1, N_DEV):
        px, py, pc = _peer(x, y, c, k)
        for b in range(nb):
            s = (k - 1) * nb + b
            copies.append(pltpu.make_async_remote_copy(
                src_ref=src(b, 4 * px + 2 * py + pc), dst_ref=outs[b].at[me], send_sem=send.at[s], recv_sem=recv.at[s],
                device_id=(px, py, pc), device_id_type=pl.DeviceIdType.MESH))
    return copies


def _exchange_shapes(bufs, kinds):
    return [S(b.shape if kd == "blocks" else (N_DEV,) + b.shape, b.dtype) for b, kd in zip(bufs, kinds)]


def _exchange_sems(nb):
    n = (N_DEV - 1) * nb
    return [pltpu.SemaphoreType.DMA((n,)), pltpu.SemaphoreType.DMA((n,)), pltpu.SemaphoreType.DMA((nb,))]


def _two_level_gather(ins, outs, send, recv, loc):
    nb = len(ins)
    x, y, c, me = _mesh_pos()
    here, sibling = (x, y, c), (x, y, 1 - c)
    chips = [(1 - x, y), (x, 1 - y), (1 - x, 1 - y)]

    def copy(slot, b, block, to, src=None):
        d = 4 * block[0] + 2 * block[1] + block[2]
        return pltpu.make_async_remote_copy(
            src_ref=outs[b].at[d] if src is None else src, dst_ref=outs[b].at[d],
            send_sem=send.at[slot * nb + b], recv_sem=recv.at[slot * nb + b],
            device_id=to, device_id_type=pl.DeviceIdType.MESH)

    mine = [pltpu.make_async_copy(ins[b], outs[b].at[me], loc.at[b]) for b in range(nb)]
    first = [copy(0, b, here, sibling, src=ins[b]) for b in range(nb)]
    first += [copy(1 + j, b, here, (*chip, c), src=ins[b]) for j, chip in enumerate(chips) for b in range(nb)]

    def start():
        for cp in mine + first:
            cp.start()

    def finish():
        passed = []
        for j, chip in enumerate(chips):
            for b in range(nb):
                copy(1 + j, b, (*chip, c), here).wait_recv()
            onward = [copy(4 + j, b, (*chip, c), sibling) for b in range(nb)]
            for cp in onward:
                cp.start()
            passed += onward
        for b in range(nb):
            copy(0, b, sibling, here).wait_recv()
        for j, chip in enumerate(chips):
            for b in range(nb):
                copy(4 + j, b, (*chip, 1 - c), here).wait_recv()
        for cp in first + passed:
            cp.wait_send()
        for cp in mine:
            cp.wait()

    return start, finish


def _exchange_plan(ins, outs, kinds, send, recv, loc):
    if all(kd == "gather" for kd in kinds):
        return _two_level_gather(ins, outs, send, recv, loc)
    copies = _exchange_copies(ins, outs, kinds, send, recv, loc)

    def start():
        for cp in copies:
            cp.start()

    def finish():
        for cp in copies:
            cp.wait()

    return start, finish


def _exchange(bufs, kinds, name):
    nb = len(bufs)

    def body(*refs):
        start, finish = _exchange_plan(refs[:nb], refs[nb:2 * nb], kinds, *refs[2 * nb:])
        start()
        finish()

    return pl.pallas_call(
        body, name=name, in_specs=[_ANY] * nb, out_specs=[_ANY] * nb,
        out_shape=_exchange_shapes(bufs, kinds), scratch_shapes=_exchange_sems(nb))(*bufs)


def _sum_parts(p_ref):
    acc = p_ref[0].astype(F32)
    for d in range(1, N_DEV):
        acc = acc + p_ref[d].astype(F32)
    return acc


def _adamw_math(g, w, m, v):
    m = ADAM_B1 * m + (1.0 - ADAM_B1) * g
    v = ADAM_B2 * v + (1.0 - ADAM_B2) * (g * g)
    m_hat = m / (1.0 - ADAM_B1 ** ADAM_STEP)
    v_hat = v / (1.0 - ADAM_B2 ** ADAM_STEP)
    return -ADAM_LR * (m_hat / (jnp.sqrt(v_hat) + ADAM_EPS) + ADAM_WD * w), m, v


HEAD_ROWS = 256


def _sum8_adamw(parts, w, m, v, name, head=None):
    _, R, C = w.shape
    tb = HEAD_ROWS if R % HEAD_ROWS == 0 else R
    nb = R // tb
    assert head is None or (tb == HEAD_ROWS and head.shape[1] == HEAD_ROWS and parts.shape[1] == R - HEAD_ROWS)
    skip = 0 if head is None else 1

    def body(*refs):
        p_ref, w_ref, m_ref, v_ref, g_ref, d_ref, mo_ref, vo_ref = refs[skip:]
        g = _sum_parts(p_ref)
        if head is not None:
            g = jnp.where(pl.program_id(0) == nb - 1, _sum_parts(refs[0]), g)
        g_ref[0] = g
        d_ref[0], mo_ref[0], vo_ref[0] = _adamw_math(g, w_ref[0], m_ref[0], v_ref[0])

    first = [] if head is None else [pl.BlockSpec((N_DEV, tb, C), lambda i: (0, 0, 0))]
    own = pl.BlockSpec((1, tb, C), lambda i: (0, i, 0))
    last_part = parts.shape[1] // tb - 1
    return pl.pallas_call(
        body, name=name, grid=(nb,),
        in_specs=first + [pl.BlockSpec((N_DEV, tb, C), lambda i: (0, jnp.minimum(i, last_part), 0))] + [own] * 3,
        out_specs=[own] * 4, out_shape=[S((1, R, C), F32)] * 4,
        compiler_params=_params(("parallel",)))(*([] if head is None else [head]), parts, w, m, v)


SMALL_LAYOUT = (
    ("ln_emb_g", 0, 1024), ("ln_emb_b", 0, 1024), ("ssm_conv_b", 0, 1024), ("ssm_conv_b", 1024, 512),
    ("dt_bias", 0, N_HEADS), ("a_log", 0, N_HEADS), ("d_skip", 0, N_HEADS), ("ssm_norm_g", 0, 1024),
    ("b_glu", 0, 1024), ("b_glu", 1024, 1024), ("conf_conv_b", 0, 1024), ("conf_ln_g", 0, 1024),
    ("conf_ln_b", 0, 1024), ("b_out", 0, 1024), ("ln1_g", 0, 1024), ("ln1_b", 0, 1024), ("ln2_g", 0, 1024),
    ("ln2_b", 0, 1024))
SMALL_ROWS = 24
SMALL = tuple(dict.fromkeys(n for n, _, _ in SMALL_LAYOUT))


LOSS_ROW = len(SMALL_LAYOUT)


def _pack_small(rows, loss):
    def body(*refs):
        o_ref = refs[-1]
        o_ref[...] = jnp.zeros_like(o_ref)
        for r, ref in enumerate(refs[:-2]):
            o_ref[r:r + 1, 0:ref.shape[1]] = ref[...]
        o_ref[LOSS_ROW:LOSS_ROW + 1, 0:LANES] = refs[-2][0:1, :]

    return pl.pallas_call(body, name="pack_small", out_shape=S((SMALL_ROWS, 1024), F32))(*rows, loss)


def _small_update(parts, w, m, v):
    def body(*refs):
        p_ref = refs[0]
        ins = {n: refs[1 + 3 * i:4 + 3 * i] for i, n in enumerate(SMALL)}
        o0 = 1 + 3 * len(SMALL)
        outs = {n: refs[o0 + 4 * i:o0 + 4 * i + 4] for i, n in enumerate(SMALL)}
        gsum = refs[-1]
        gsum[...] = _sum_parts(p_ref)
        refs[-2][...] = gsum[LOSS_ROW:LOSS_ROW + 1, 0:LANES]
        for r, (n, off, wd) in enumerate(SMALL_LAYOUT):
            cs = slice(off, off + wd)
            g = gsum[r:r + 1, 0:wd]
            w_ref, m_ref, v_ref = ins[n]
            g_ref, d_ref, mo_ref, vo_ref = outs[n]
            g_ref[:, cs] = g
            d_ref[:, cs], mo_ref[:, cs], vo_ref[:, cs] = _adamw_math(g, w_ref[:, cs], m_ref[:, cs], v_ref[:, cs])

    args = [parts] + [a for n in SMALL for a in (w[n], m[n], v[n])]
    res = pl.pallas_call(
        body, name="small_update",
        out_shape=[S(w[n].shape, F32) for n in SMALL for _ in range(4)] + [S((1, LANES), F32)],
        scratch_shapes=[pltpu.VMEM((SMALL_ROWS, 1024), F32)])(*args)
    return tuple({n: res[4 * i + j] for i, n in enumerate(SMALL)} for j in range(4)) + (res[-1],)


EARLY = ("w_in", "ssm_conv_w", "conf_conv_w")
LATE = ("w_out", "w_ple_gate", "w_ple_proj")


def _local_step(x, p, tgt, W, shards=None):
    r1 = lambda v: v.reshape(1, -1).astype(F32)
    pad_l = lambda v: jnp.pad(r1(v), ((0, 0), (0, LANES - v.size)))
    late = None if shards is None else [shards[n] for n in LATE]
    if shards is None:
        h0, h0b, h0bt = _ln_emb_fwd(x, r1(W["ln_emb_g"]), r1(W["ln_emb_b"]))
    else:
        h0, h0b, h0bt, *gathered = _ln_emb_fwd(x, r1(W["ln_emb_g"]), r1(W["ln_emb_b"]),
                                               exchange=([shards[n] for n in EARLY], ("gather",) * len(EARLY)))
        W = dict(W, **{n: a if n == "w_in" else _unstack_shards(a, BY_COLS[n]) for n, a in zip(EARLY, gathered)})
    w_main, w_dt = _w_in_to_main(W["w_in"])
    scw, scb = W["ssm_conv_w"], r1(W["ssm_conv_b"])
    wx, wb, bx, bb = scw[:, :1024], scw[:, 1024:], scb[:, :1024], scb[:, 1024:]
    dt_bias, alog = pad_l(W["dt_bias"]), pad_l(W["a_log"])
    dskip_row = jnp.repeat(W["d_skip"].reshape(-1), HEAD).reshape(1, -1)
    norm_g = r1(W["ssm_norm_g"])
    bglu = r1(W["b_glu"])
    ba, bbg = bglu[:, :1024], bglu[:, 1024:]
    ccw, ccb, clg, clb = W["conf_conv_w"], r1(W["conf_conv_b"]), r1(W["conf_ln_g"]), r1(W["conf_ln_b"])

    if late is None:
        proj = _mm(h0b, w_main, "nn", "in_proj", tm=TM_IN_PROJ)
    else:
        proj, *gathered = _mm(h0b, w_main, "nn", "in_proj", tm=TM_IN_PROJ, exchange=(late, ("gather",) * len(LATE)))
        W = dict(W, **{n: _unstack_shards(a, BY_COLS[n]) for n, a in zip(LATE, gathered)})
    dt_raw = _mm(h0b, w_dt, "nn", "in_proj_dt")
    xs_c, bc_c, dt = _ssd_pre_fwd(proj, dt_raw, wx, wb, bx, bb, dt_bias)
    mix, ypre, hprev, mixt = _ssd_fwd(xs_c, bc_c, dt, proj, alog, dskip_row, norm_g)
    u1, mix, mixt = _conf_fwd(mix, mixt, proj, ccw, ccb, clg, clb, ba, bbg)
    out, h1, h1b, h1bt = _out_proj_post1(mix, W["w_out"], h0, r1(W["b_out"]), r1(W["ln1_g"]), r1(W["ln1_b"]))
    pb = p.astype(BF16)
    dh1a, dgp, dple, loss, dln2g, dln2b = _ple_post2(h1b, W["w_ple_gate"], pb, W["w_ple_proj"], h1, tgt,
                                                      r1(W["ln2_g"]), r1(W["ln2_b"]))

    g = {}
    g["w_ple_proj"] = _mm(pb.T, dple, "nn", "d_ple_proj", out_dtype=BF16)
    g["w_ple_gate"] = _mm(h1bt, dgp, "nn", "d_ple_gate", out_dtype=BF16)
    dout, dh0a, dln1g, dln1b, dbout = _d_h1_post1_bwd(dgp, W["w_ple_gate"], dh1a, h0, out, r1(W["b_out"]),
                                                      r1(W["ln1_g"]))
    g["w_out"] = _mm(mixt, dout, "nn", "d_w_out", out_dtype=BF16)
    dmix = _mm(dout, W["w_out"], "nt", "d_mix")
    du1, dproj, dclg, dclb = _conf_bwd1(dmix, u1, proj, clg, clb)
    dproj, g["conf_conv_w"], dccb, dba, dbb = _conf_bwd2(dproj, proj, du1, ccw, ba, bbg)
    stack = lambda names: [_stack_shards(g[n], BY_COLS[n]) for n in names]
    dxs_c, dbc_c, ddtr, dproj, dng, ddsk, dalog, ddtb, *recv_a = _ssd_bwd(
        dproj, xs_c, bc_c, dt, dt_raw, dt_bias, proj, ypre, hprev, dmix, alog, dskip_row, norm_g,
        exchange=None if late is None else (stack(LATE), ("blocks",) * len(LATE)))
    dproj, dwx, dbx = _ssd_conv_bwd(dproj, proj, dxs_c, wx, bx, 1024, C_XS, "ssd_conv_bwd_x")
    dproj, dwb, dbb2 = _ssd_conv_bwd(dproj, proj, dbc_c, wb, bb, 512, C_BC, "ssd_conv_bwd_bc")
    g["ssm_conv_w"] = jnp.concatenate([dwx, dwb], axis=1)
    dw_dt = _mm(h0bt, ddtr, "nn", "d_w_dt", out_dtype=BF16)
    last_args = (dproj, w_main, ddtr, w_dt, dh0a, x, r1(W["ln_emb_g"]))
    if late is None:
        g["w_in"] = _w_in_blocks(_mm(h0bt, dproj, "nn", "d_w_in", out_dtype=BF16), dw_dt)
        grad_x, dlng, dlnb = _d_h0_ln_bwd(*last_args)
    else:
        r0 = D - HEAD_ROWS
        head = _w_in_blocks(_mm(h0bt, dproj, "nn", "d_w_in_head", out_dtype=BF16, tk=x.shape[0],
                                a_rows=(r0, HEAD_ROWS)), dw_dt[r0:])
        dw_rest, recv_head = _mm(h0bt, dproj, "nn", "d_w_in", out_dtype=BF16, a_rows=(0, r0),
                                 exchange=([head], ("blocks",)))
        last = ("ssm_conv_w", "conf_conv_w")
        grad_x, dlng, dlnb, *recv_b = _d_h0_ln_bwd(
            *last_args, exchange=([_w_in_blocks(dw_rest, dw_dt[:r0])] + stack(last), ("blocks",) * 3))
        g["recv"] = dict(zip(LATE + ("w_in",) + last, recv_a + recv_b), w_in_head=recv_head)
    g["rows"] = [dlng, dlnb, dbx, dbb2, ddtb, dalog, ddsk, dng, dba, dbb, dccb, dclg, dclb, dbout, dln1g, dln1b,
                 dln2g, dln2b]
    return loss, grad_x, g


W_IN_SEGMENTS = ((0, 2048, 2048), (2048, 5120, 512), (2560, None, N_HEADS), (2576, 0, 2048), (4624, 4096, 1024))


def _w_in_to_main(shards):
    def pieces(p0, width):
        out, p = [], p0
        while p < p0 + width:
            d = p // COLS_PER_DEV
            hi = min(p0 + width, (d + 1) * COLS_PER_DEV)
            out.append(shards[d][:, p - d * COLS_PER_DEV:hi - d * COLS_PER_DEV])
            p = hi
        return out
    main = [s for s in sorted(W_IN_SEGMENTS, key=lambda s: -1 if s[1] is None else s[1]) if s[1] is not None]
    w_main = jnp.concatenate([q for p0, _, width in main for q in pieces(p0, width)], axis=1)
    w_dt = jnp.concatenate(pieces(2560, N_HEADS), axis=1)
    return w_main, jnp.pad(w_dt, ((0, 0), (0, LANES - N_HEADS)))


def _w_in_blocks(dw_main, dw_dt):
    blocks = []
    for d in range(N_DEV):
        lo_d, hi_d = d * COLS_PER_DEV, (d + 1) * COLS_PER_DEV
        parts = []
        for p0, m0, width in W_IN_SEGMENTS:
            lo, hi = max(lo_d, p0), min(hi_d, p0 + width)
            if lo < hi:
                parts.append(dw_dt[:, lo - p0:hi - p0] if m0 is None else dw_main[:, m0 + lo - p0:m0 + hi - p0])
        blocks.append(jnp.concatenate(parts, axis=1))
    return jnp.stack(blocks)


WEIGHTS = ['ln_emb_g', 'ln_emb_b', 'w_in', 'ssm_conv_w', 'ssm_conv_b', 'dt_bias', 'a_log', 'd_skip', 'ssm_norm_g',
           'b_glu', 'conf_conv_w', 'conf_conv_b', 'conf_ln_g', 'conf_ln_b', 'w_out', 'b_out', 'ln1_g', 'ln1_b',
           'w_ple_gate', 'w_ple_proj', 'ln2_g', 'ln2_b']
SHARDED = (("w_in", True), ("w_out", False), ("w_ple_gate", False), ("w_ple_proj", True), ("ssm_conv_w", True),
           ("conf_conv_w", True))
BY_COLS = dict(SHARDED)


def _stack_shards(a, by_cols):
    if by_cols:
        return a.reshape(a.shape[0], N_DEV, a.shape[1] // N_DEV).transpose(1, 0, 2)
    return a.reshape(N_DEV, a.shape[0] // N_DEV, a.shape[1])


def _unstack_shards(a, by_cols):
    if by_cols:
        return a.transpose(1, 0, 2).reshape(a.shape[1], N_DEV * a.shape[2])
    return a.reshape(N_DEV * a.shape[1], a.shape[2])


def kernel(x, p, ln_emb_g, ln_emb_b, w_in, ssm_conv_w, ssm_conv_b, dt_bias, a_log, d_skip, ssm_norm_g, b_glu, conf_conv_w, conf_conv_b, conf_ln_g, conf_ln_b, w_out, b_out, ln1_g, ln1_b, w_ple_gate, w_ple_proj, ln2_g, ln2_b, loss_target, m_ln_emb_g, m_ln_emb_b, m_w_in, m_ssm_conv_w, m_ssm_conv_b, m_dt_bias, m_a_log, m_d_skip, m_ssm_norm_g, m_b_glu, m_conf_conv_w, m_conf_conv_b, m_conf_ln_g, m_conf_ln_b, m_w_out, m_b_out, m_ln1_g, m_ln1_b, m_w_ple_gate, m_w_ple_proj, m_ln2_g, m_ln2_b, v_ln_emb_g, v_ln_emb_b, v_w_in, v_ssm_conv_w, v_ssm_conv_b, v_dt_bias, v_a_log, v_d_skip, v_ssm_norm_g, v_b_glu, v_conf_conv_w, v_conf_conv_b, v_conf_ln_g, v_conf_ln_b, v_w_out, v_b_out, v_ln1_g, v_ln1_b, v_w_ple_gate, v_w_ple_proj, v_ln2_g, v_ln2_b):
    loc = dict(locals())
    w = {n: loc[n] for n in WEIGHTS}
    m = {n: loc["m_" + n] for n in WEIGHTS}
    v = {n: loc["v_" + n] for n in WEIGHTS}
    sharded = [n for n, _ in SHARDED]

    shards = {n: w[n][0].astype(BF16) if n.startswith("w_") else w[n][0] for n in sharded}
    W = {n: w[n].reshape(-1) for n in SMALL}
    loss, grad_x, g = _local_step(x[0], p[0, 0], loss_target[0], W, shards=shards)
    (recv_small,) = _exchange([_pack_small(g["rows"], loss)], ("all",), "small_exchange")

    grads, delta, new_m, new_v = {}, {}, {}, {}
    for n in sharded:
        grads[n], delta[n], new_m[n], new_v[n] = _sum8_adamw(
            g["recv"][n], w[n], m[n], v[n], "adamw_" + n, head=g["recv"]["w_in_head"] if n == "w_in" else None)
    two_d = lambda d: {n: d[n].reshape(1, -1) for n in SMALL}
    *small, loss = _small_update(recv_small, two_d(w), two_d(m), two_d(v))
    for dst, res in zip((grads, delta, new_m, new_v), small):
        for n in SMALL:
            dst[n] = res[n].reshape(w[n].shape)
    return (loss[0, 0], grad_x[None], *[grads[n] for n in WEIGHTS], *[delta[n] for n in WEIGHTS],
            *[new_m[n] for n in WEIGHTS], *[new_v[n] for n in WEIGHTS])
```

```python
import numpy as np
import jax
import jax.numpy as jnp
from jax import lax
from jax.experimental import pallas as pl
from jax.experimental.pallas import tpu as pltpu

F32, BF16 = jnp.float32, jnp.bfloat16
S = jax.ShapeDtypeStruct

N_DEV = 8
D = 1024
D_PLE = 256
D_SSM = 1024
D_CONF = 1024
N_HEADS = 16
HEAD = 64
N_STATE = 128
CHUNK = 128
K_SSM = 4
K_CONF = 31
D_IN = 5648
COLS_PER_DEV = D_IN // N_DEV
LN_EPS = 1e-5
RMS_EPS = 1e-5
ALPHA = 2.0 ** 0.25
LANES = 128
TB = 512
TB_SSD_CONV_BWD = 512
TM_IN_PROJ = 2048
TM_FUSED = 1024
RG = 32
ROW_UNROLL = 4
HALO_SSM = 8
HALO_CONF = 32
VMEM_LIMIT = 56 * 1024 * 1024

ADAM_LR, ADAM_B1, ADAM_B2, ADAM_EPS, ADAM_WD, ADAM_STEP = 0.001, 0.9, 0.999, 1e-08, 0.01, 10

C_GLUA, C_GLUB, C_XS, C_Z, C_CG = 0, 1, 2, 3, 4
C_BC = 10
N_MAIN = 5632


def _params(sem, vmem=VMEM_LIMIT):
    return pltpu.CompilerParams(dimension_semantics=sem, vmem_limit_bytes=vmem)


def _row(tb, n, col=0):
    return pl.BlockSpec((tb, n), lambda i: (i, col))


def _colt(n, tb, row=0):
    return pl.BlockSpec((n, tb), lambda i: (row, i))


def _full(shape):
    return pl.BlockSpec(shape, lambda i: (0,) * len(shape))


_ANY = pl.BlockSpec(memory_space=pl.ANY)


def _prev(tb, halo, n, col=0):
    r = tb // halo
    return pl.BlockSpec((halo, n), lambda i: (jnp.maximum(i * r - 1, 0), col))


def _next(tb, halo, n, nt, col=0):
    r = tb // halo
    return pl.BlockSpec((halo, n), lambda i: (jnp.minimum((i + 1) * r, nt * r - 1), col))


def _row_loop(tb, fn):
    def it(r, c):
        fn(pl.ds(pl.multiple_of(r * RG, RG), RG))
        return c
    lax.fori_loop(0, tb // RG, it, 0, unroll=ROW_UNROLL)


def _col_loop(n, fn):
    def it(j, c):
        fn(pl.ds(pl.multiple_of(j * LANES, LANES), LANES))
        return c
    lax.fori_loop(0, n // LANES, it, 0)


def _sigmoid(x):
    return 1.0 / (1.0 + jnp.exp(-x))


def _dsilu(x, s):
    return s * (1.0 + x * (1.0 - s))


def _ln_stats(v):
    mu = jnp.mean(v, axis=-1, keepdims=True)
    c = v - mu
    r = lax.rsqrt(jnp.mean(c * c, axis=-1, keepdims=True) + LN_EPS)
    return c * r, r


def _ln_bwd(dy, xhat, r, g):
    dxh = dy * g
    dv = r * (dxh - jnp.mean(dxh, axis=-1, keepdims=True) - xhat * jnp.mean(dxh * xhat, axis=-1, keepdims=True))
    return dv, jnp.sum(dy * xhat, axis=0, keepdims=True), jnp.sum(dy, axis=0, keepdims=True)


def _dot(a, b, dims=((1,), (0,))):
    return lax.dot_general(a.astype(BF16), b.astype(BF16), (dims, ((), ())), preferred_element_type=F32)


_NT = ((1,), (1,))
_TN = ((0,), (0,))


def _split3(x):
    hi = x.astype(BF16)
    r = x - hi.astype(F32)
    mid = r.astype(BF16)
    return hi, mid, (r - mid.astype(F32)).astype(BF16)


def _dot_sel_b(a, b, dims=((1,), (0,))):
    hi, mid, lo = _split3(a)
    return (_dot(lo, b, dims) + _dot(mid, b, dims)) + _dot(hi, b, dims)


def _dot_sel_a(a, b, dims=((1,), (0,))):
    hi, mid, lo = _split3(b)
    return (_dot(a, lo, dims) + _dot(a, mid, dims)) + _dot(a, hi, dims)


def _mm(a, b, mode, name, out_dtype=F32, add=None, tm=1024, tn=None, tk=1024, exchange=None, a_rows=None):
    assert mode in ("nn", "nt")
    (M, K), N = a.shape, b.shape[1 if mode == "nn" else 0]
    row0 = 0
    if a_rows is not None:
        row0, M = a_rows
        tm = M
        assert row0 % M == 0
    if tn is None:
        tn = next(t for t in (1024, 1408, 512, 256, LANES) if N % t == 0)
    tm, tn, tk = min(tm, M), min(tn, N), min(tk, K)
    assert M % tm == 0 and N % tn == 0 and K % tk == 0, (name, M, N, K)
    grid = (M // tm, N // tn, K // tk)
    nk = grid[2]
    dims = ((1,), (0,)) if mode == "nn" else _NT
    n_in = 2 + (add is not None)
    xbufs, kinds = exchange if exchange is not None else ((), ())
    nx = len(xbufs)

    def body(*refs):
        a_ref, b_ref = refs[:2]
        o_ref = refs[n_in + nx]
        acc = refs[n_in + 2 * nx + 1]
        i, j, k = pl.program_id(0), pl.program_id(1), pl.program_id(2)
        if nx:
            start, finish = _exchange_plan(refs[n_in:n_in + nx], refs[n_in + nx + 1:n_in + 2 * nx + 1], kinds,
                                           *refs[n_in + 2 * nx + 2:])
            pl.when((i == 0) & (j == 0) & (k == 0))(start)

        d = _dot(a_ref[...], b_ref[...], dims)

        def write_out(r):
            if add is not None:
                r = r + refs[2][...]
            o_ref[...] = r.astype(out_dtype)

        if nk == 1:
            write_out(d)
        else:
            @pl.when(k == 0)
            def _():
                acc[...] = d

            @pl.when((k > 0) & (k < nk - 1))
            def _():
                acc[...] += d

            @pl.when(k == nk - 1)
            def _():
                write_out(acc[...] + d)

        if nx:
            pl.when((i == grid[0] - 1) & (j == grid[1] - 1) & (k == nk - 1))(finish)

    a_spec = pl.BlockSpec((tm, tk), lambda i, j, k: (i + row0 // tm, k))
    b_spec = pl.BlockSpec((tn, tk), lambda i, j, k: (j, k)) if mode == "nt" else pl.BlockSpec((tk, tn), lambda i, j, k: (k, j))
    o_spec = pl.BlockSpec((tm, tn), lambda i, j, k: (i, j))
    ins, specs = [a, b], [a_spec, b_spec]
    if add is not None:
        ins.append(add)
        specs.append(o_spec)
    acc_spec = pltpu.VMEM((tm, tn) if nk > 1 else (8, LANES), F32)
    if not nx:
        return pl.pallas_call(
            body, name=name, grid=grid, in_specs=specs, out_specs=o_spec,
            out_shape=S((M, N), out_dtype), scratch_shapes=[acc_spec],
            compiler_params=_params(("parallel", "parallel", "arbitrary")))(*ins)
    return pl.pallas_call(
        body, name=name, grid=grid, in_specs=specs + [_ANY] * nx, out_specs=[o_spec] + [_ANY] * nx,
        out_shape=[S((M, N), out_dtype)] + _exchange_shapes(xbufs, kinds),
        scratch_shapes=[acc_spec] + _exchange_sems(nx),
        compiler_params=_params(("arbitrary", "arbitrary", "arbitrary")))(*ins, *xbufs)


def _ln_emb_fwd(x, g, b, exchange=None):
    T = x.shape[0]

    nt = T // TB
    xbufs, kinds = exchange if exchange is not None else ((), ())
    nx = len(xbufs)

    def body(*refs):
        x_ref, g_ref, b_ref = refs[:3]
        h_ref, hb_ref, hbt_ref = refs[3 + nx:6 + nx]
        i = pl.program_id(0)
        if nx:
            start, finish = _exchange_plan(refs[3:3 + nx], refs[6 + nx:6 + 2 * nx], kinds, *refs[6 + 2 * nx:])
            pl.when(i == 0)(start)

        def rows(rs):
            xh, _ = _ln_stats(x_ref[rs, :])
            h = xh * g_ref[...] + b_ref[...]
            h_ref[rs, :] = h
            hb_ref[rs, :] = h.astype(BF16)
        _row_loop(TB, rows)
        hbt_ref[...] = hb_ref[...].T
        if nx:
            pl.when(i == nt - 1)(finish)

    return pl.pallas_call(
        body, name="ln_emb_fwd", grid=(nt,),
        in_specs=[_row(TB, D), _full((1, D)), _full((1, D))] + [_ANY] * nx,
        out_specs=[_row(TB, D), _row(TB, D), _colt(D, TB)] + [_ANY] * nx,
        out_shape=[S((T, D), F32), S((T, D), BF16), S((D, T), BF16)] + _exchange_shapes(xbufs, kinds),
        scratch_shapes=_exchange_sems(nx) if nx else [],
        compiler_params=_params(("arbitrary",)))(x, g, b, *xbufs)


def _out_proj_post1(mix, w_out, h0, b_out, g, b, tm=TM_FUSED, tk=1024):
    T, K = mix.shape
    tm = min(tm, T)
    nk = K // tk
    assert T % tm == 0 and K % tk == 0 and nk >= 2

    def body(mix_ref, w_ref, h0_ref, bo_ref, g_ref, b_ref, out_ref, h_ref, hb_ref, hbt_ref, acc):
        k = pl.program_id(1)
        d = _dot(mix_ref[...], w_ref[...])

        @pl.when(k == 0)
        def _():
            acc[...] = d

        @pl.when((k > 0) & (k < nk - 1))
        def _():
            acc[...] += d

        @pl.when(k == nk - 1)
        def _():
            out_ref[...] = acc[...] + d

            def rows(rs):
                xh, _ = _ln_stats(ALPHA * h0_ref[rs, :] + out_ref[rs, :] + bo_ref[...])
                h = xh * g_ref[...] + b_ref[...]
                h_ref[rs, :] = h
                hb_ref[rs, :] = h.astype(BF16)
            _row_loop(tm, rows)
            hbt_ref[...] = hb_ref[...].T

    rowt = lambda n: pl.BlockSpec((tm, n), lambda i, k: (i, 0))
    const = pl.BlockSpec((1, D), lambda i, k: (0, 0))
    return pl.pallas_call(
        body, name="out_proj_post1", grid=(T // tm, nk),
        in_specs=[pl.BlockSpec((tm, tk), lambda i, k: (i, k)), pl.BlockSpec((tk, D), lambda i, k: (k, 0)), rowt(D),
                  const, const, const],
        out_specs=[rowt(D), rowt(D), rowt(D), pl.BlockSpec((D, tm), lambda i, k: (0, i))],
        out_shape=[S((T, D), F32), S((T, D), F32), S((T, D), BF16), S((D, T), BF16)],
        scratch_shapes=[pltpu.VMEM((tm, D), F32)],
        compiler_params=_params(("parallel", "arbitrary")))(mix, w_out, h0, b_out, g, b)


def _ple_post2(h1b, w_gate, pb, w_proj, h1, tgt, g, b, tm=TM_FUSED):
    T = h1.shape[0]
    tm = min(tm, T)
    assert T % tm == 0

    def body(h1b_ref, wg_ref, pb_ref, wp_ref, h1_ref, tgt_ref, g_ref, b_ref,
             dh1_ref, dgp_ref, dple_ref, loss_ref, dg_ref, db_ref, gp_ref, ple_ref):
        @pl.when(pl.program_id(0) == 0)
        def _():
            loss_ref[...] = jnp.zeros_like(loss_ref)
            dg_ref[...] = jnp.zeros_like(dg_ref)
            db_ref[...] = jnp.zeros_like(db_ref)

        gp_ref[...] = _dot(h1b_ref[...], wg_ref[...])
        ple_ref[...] = _dot(pb_ref[...], wp_ref[...])

        def rows(rs):
            gate = _sigmoid(gp_ref[rs, :])
            ple = ple_ref[rs, :]
            xh, r = _ln_stats(ALPHA * h1_ref[rs, :] + gate * ple)
            err = xh * g_ref[...] + b_ref[...] - tgt_ref[rs, :]
            loss_ref[...] += 0.5 * jnp.sum(jnp.mean(err * err, axis=-1, keepdims=True), axis=0, keepdims=True)
            dv, dg, db = _ln_bwd(err * (1.0 / D), xh, r, g_ref[...])
            dg_ref[...] += dg
            db_ref[...] += db
            dh1_ref[rs, :] = ALPHA * dv
            dgp_ref[rs, :] = (dv * ple * gate * (1.0 - gate)).astype(BF16)
            dple_ref[rs, :] = (dv * gate).astype(BF16)
        _row_loop(tm, rows)

    return pl.pallas_call(
        body, name="ple_post2", grid=(T // tm,),
        in_specs=[_row(tm, D), _full((D, D)), _row(tm, D_PLE), _full((D_PLE, D)), _row(tm, D), _row(tm, D),
                  _full((1, D)), _full((1, D))],
        out_specs=[_row(tm, D)] * 3 + [_full((8, LANES)), _full((1, D)), _full((1, D))],
        out_shape=[S((T, D), F32), S((T, D), BF16), S((T, D), BF16), S((8, LANES), F32), S((1, D), F32), S((1, D), F32)],
        scratch_shapes=[pltpu.VMEM((tm, D), F32), pltpu.VMEM((tm, D), F32)],
        compiler_params=_params(("arbitrary",)))(h1b, w_gate, pb, w_proj, h1, tgt, g, b)


def _d_h1_post1_bwd(dgp, w_gate, dh1a, h0, out, b_out, g, tm=TM_FUSED):
    T = h0.shape[0]
    tm = min(tm, T)
    assert T % tm == 0

    def body(dgp_ref, wg_ref, da_ref, h0_ref, out_ref, bo_ref, g_ref, dout_ref, dh0_ref, dg_ref, db_ref, dbo_ref, dh1):
        @pl.when(pl.program_id(0) == 0)
        def _():
            dg_ref[...] = jnp.zeros_like(dg_ref)
            db_ref[...] = jnp.zeros_like(db_ref)
            dbo_ref[...] = jnp.zeros_like(dbo_ref)

        dh1[...] = da_ref[...] + _dot(dgp_ref[...], wg_ref[...], _NT)

        def rows(rs):
            xh, r = _ln_stats(ALPHA * h0_ref[rs, :] + out_ref[rs, :] + bo_ref[...])
            dv, dg, db = _ln_bwd(dh1[rs, :], xh, r, g_ref[...])
            dg_ref[...] += dg
            db_ref[...] += db
            dbo_ref[...] += jnp.sum(dv, axis=0, keepdims=True)
            dout_ref[rs, :] = dv.astype(BF16)
            dh0_ref[rs, :] = ALPHA * dv
        _row_loop(tm, rows)

    return pl.pallas_call(
        body, name="d_h1_post1_bwd", grid=(T // tm,),
        in_specs=[_row(tm, D), _full((D, D))] + [_row(tm, D)] * 3 + [_full((1, D))] * 2,
        out_specs=[_row(tm, D)] * 2 + [_full((1, D))] * 3,
        out_shape=[S((T, D), BF16), S((T, D), F32)] + [S((1, D), F32)] * 3,
        scratch_shapes=[pltpu.VMEM((tm, D), F32)],
        compiler_params=_params(("arbitrary",)))(dgp, w_gate, dh1a, h0, out, b_out, g)


def _d_h0_ln_bwd(dproj, w_main, ddtr, w_dt, dh0a, x, g, exchange=None, tm=1024, tk=1408):
    T, K = dproj.shape
    tm = min(tm, T)
    assert T % tm == 0 and K % tk == 0
    ni, nk = T // tm, K // tk
    xbufs, kinds = exchange if exchange is not None else ((), ())
    nx = len(xbufs)

    def body(*refs):
        dp_ref, w_ref, dt_ref, wdt_ref, da_ref, x_ref, g_ref = refs[:7]
        dx_ref, dg_ref, db_ref = refs[7 + nx:10 + nx]
        acc = refs[10 + 2 * nx]
        i, k = pl.program_id(0), pl.program_id(1)
        if nx:
            start, finish = _exchange_plan(refs[7:7 + nx], refs[10 + nx:10 + 2 * nx], kinds, *refs[11 + 2 * nx:])
            pl.when((i == 0) & (k == 0))(start)

        @pl.when((i == 0) & (k == 0))
        def _():
            dg_ref[...] = jnp.zeros_like(dg_ref)
            db_ref[...] = jnp.zeros_like(db_ref)

        d = _dot(dp_ref[...], w_ref[...], _NT)

        @pl.when(k == 0)
        def _():
            acc[...] = da_ref[...] + _dot(dt_ref[...], wdt_ref[...], _NT) + d

        @pl.when(k > 0)
        def _():
            acc[...] += d

        @pl.when(k == nk - 1)
        def _():
            def rows(rs):
                xh, r = _ln_stats(x_ref[rs, :])
                dv, dg, db = _ln_bwd(acc[rs, :], xh, r, g_ref[...])
                dg_ref[...] += dg
                db_ref[...] += db
                dx_ref[rs, :] = dv
            _row_loop(tm, rows)

        if nx:
            pl.when((i == ni - 1) & (k == nk - 1))(finish)

    rowt = lambda n: pl.BlockSpec((tm, n), lambda i, k: (i, 0))
    const = lambda shape: pl.BlockSpec(shape, lambda i, k: (0, 0))
    return pl.pallas_call(
        body, name="d_h0_ln_bwd", grid=(ni, nk),
        in_specs=[pl.BlockSpec((tm, tk), lambda i, k: (i, k)), pl.BlockSpec((D, tk), lambda i, k: (0, k)),
                  rowt(LANES), const((D, LANES)), rowt(D), rowt(D), const((1, D))] + [_ANY] * nx,
        out_specs=[rowt(D), const((1, D)), const((1, D))] + [_ANY] * nx,
        out_shape=[S((T, D), F32), S((1, D), F32), S((1, D), F32)] + _exchange_shapes(xbufs, kinds),
        scratch_shapes=[pltpu.VMEM((tm, D), F32)] + (_exchange_sems(nx) if nx else []),
        compiler_params=_params(("arbitrary", "arbitrary")))(dproj, w_main, ddtr, w_dt, dh0a, x, g, *xbufs)


def _softplus(x):
    return jnp.maximum(x, 0.0) + jnp.log1p(jnp.exp(-jnp.abs(x)))


def _ssd_pre_fwd(proj, dt_raw, wx, wb, bx, bb, dt_bias):
    T = proj.shape[0]
    H = HALO_SSM

    def body(xs_ref, xsp_ref, bc_ref, bcp_ref, dtr_ref, wx_ref, wb_ref, bx_ref, bb_ref, dtb_ref,
             xso_ref, bco_ref, dto_ref, extx, extb):
        first = pl.program_id(0) == 0

        def conv(t_ref, p_ref, w_ref, b_ref, o_ref, ext, n):
            def blk(cols):
                ext[0:H, cols] = jnp.where(first, 0.0, p_ref[:, cols])
                ext[H:, cols] = t_ref[:, cols]
                for r0 in range(0, TB, 64):
                    acc = jnp.broadcast_to(b_ref[:, cols], (64, LANES))
                    for k in range(K_SSM):
                        acc = acc + w_ref[k:k + 1, cols] * ext[pl.ds(r0 + H - (K_SSM - 1) + k, 64), cols]
                    o_ref[pl.ds(r0, 64), cols] = acc * _sigmoid(acc)
            _col_loop(n, blk)

        conv(xs_ref, xsp_ref, wx_ref, bx_ref, xso_ref, extx, D_SSM)
        conv(bc_ref, bcp_ref, wb_ref, bb_ref, bco_ref, extb, 512)
        dto_ref[...] = _softplus(dtr_ref[...] + dtb_ref[...])

    return pl.pallas_call(
        body, name="ssd_pre_fwd", grid=(T // TB,),
        in_specs=[_row(TB, 1024, C_XS), _prev(TB, H, 1024, C_XS), _row(TB, 512, C_BC), _prev(TB, H, 512, C_BC),
                  _row(TB, LANES), _full((K_SSM, 1024)), _full((K_SSM, 512)), _full((1, 1024)), _full((1, 512)),
                  _full((1, LANES))],
        out_specs=[_row(TB, 1024), _row(TB, 512), _row(TB, LANES)],
        out_shape=[S((T, 1024), F32), S((T, 512), F32), S((T, LANES), F32)],
        scratch_shapes=[pltpu.VMEM((H + TB, 1024), F32), pltpu.VMEM((H + TB, 512), F32)],
        compiler_params=_params(("parallel",)))(proj, proj, proj, proj, dt_raw, wx, wb, bx, bb, dt_bias)


def _ssd_conv_bwd(dproj, proj, d_c, w, b, n, col, name):
    TB = TB_SSD_CONV_BWD
    T = proj.shape[0]
    nt = T // TB
    H = HALO_SSM
    R = TB + H

    def body(dproj_ref, t_ref, p_ref, n_ref, d_ref, dn_ref, w_ref, b_ref, o_ref, dw_ref, dbias_ref, ext, dp):
        i = pl.program_id(0)
        first, last = i == 0, i == nt - 1

        @pl.when(first)
        def _():
            dw_ref[...] = jnp.zeros_like(dw_ref)
            dbias_ref[...] = jnp.zeros_like(dbias_ref)

        def blk(cols):
            ext[0:H, cols] = jnp.where(first, 0.0, p_ref[:, cols])
            ext[H:H + TB, cols] = t_ref[:, cols]
            ext[H + TB:, cols] = n_ref[:, cols]
            def taps_and_dsilu(r0, rows):
                taps = [ext[pl.ds(r0 + H - (K_SSM - 1) + k, rows), cols] for k in range(K_SSM)]
                pre = jnp.broadcast_to(b_ref[:, cols], (rows, LANES))
                for k in range(K_SSM):
                    pre = pre + w_ref[k:k + 1, cols] * taps[k]
                return taps, _dsilu(pre, _sigmoid(pre))

            for r0 in range(0, TB, 64):
                taps, ds = taps_and_dsilu(r0, 64)
                dpt = d_ref[pl.ds(r0, 64), cols] * ds
                dp[pl.ds(r0, 64), cols] = dpt
                dbias_ref[:, cols] += jnp.sum(dpt, axis=0, keepdims=True)
                for k in range(K_SSM):
                    dw_ref[k:k + 1, cols] += jnp.sum(dpt * taps[k], axis=0, keepdims=True)
            dp[TB:, cols] = jnp.where(last, 0.0, dn_ref[:, cols] * taps_and_dsilu(TB, H)[1])
            for r0 in range(0, TB, 64):
                acc = jnp.zeros((64, LANES), F32)
                for k in range(K_SSM):
                    acc = acc + w_ref[k:k + 1, cols] * dp[pl.ds(r0 + K_SSM - 1 - k, 64), cols]
                o_ref[pl.ds(r0, 64), cols] = acc.astype(BF16)
        _col_loop(n, blk)

    return pl.pallas_call(
        body, name=name, grid=(nt,),
        in_specs=[_ANY, _row(TB, n, col), _prev(TB, H, n, col), _next(TB, H, n, nt, col),
                  _row(TB, n), _next(TB, H, n, nt), _full((K_SSM, n)), _full((1, n))],
        out_specs=[_row(TB, n, col), _full((K_SSM, n)), _full((1, n))],
        out_shape=[S(dproj.shape, BF16), S((K_SSM, n), F32), S((1, n), F32)],
        input_output_aliases={0: 0},
        scratch_shapes=[pltpu.VMEM((H + TB + H, n), F32), pltpu.VMEM((R, n), F32)],
        compiler_params=_params(("arbitrary",)))(dproj, proj, proj, proj, d_c, d_c, w, b)


def _ssd_consts():
    ex = np.zeros((LANES, D_SSM), np.float32)
    for h in range(N_HEADS):
        ex[h, h * HEAD:(h + 1) * HEAD] = 1.0
    tri = np.tril(np.ones((CHUNK, CHUNK), np.float32))
    return jnp.asarray(ex), jnp.asarray(ex.T.copy()), jnp.asarray(tri), jnp.asarray(tri.T.copy())


def _ssd_common(xs, dt, alog_ref, ex_ref, tri_ref):
    lane = lax.broadcasted_iota(jnp.int32, (1, LANES), 1)
    a = jnp.where(lane < N_HEADS, -jnp.exp(alog_ref[...]), 0.0)
    A = _dot_sel_a(tri_ref[...], dt * a)
    ex = ex_ref[...]
    Aex = _dot_sel_b(A, ex)
    dtex = _dot_sel_b(dt, ex)
    expA = jnp.exp(Aex)
    dec = jnp.exp(Aex[CHUNK - 1:CHUNK, :] - Aex)
    cd = _dot_sel_a(ex, jnp.broadcast_to(jnp.exp(A.T[:, CHUNK - 1:CHUNK]), (LANES, LANES)), _TN)
    return a, A, dtex, expA, dec, cd


def _decay_mask():
    sub = lax.broadcasted_iota(jnp.int32, (CHUNK, CHUNK), 0)
    lane = lax.broadcasted_iota(jnp.int32, (CHUNK, CHUNK), 1)
    return sub, lane, sub >= lane


def _ssd_fwd(xs_c, bc_c, dt, proj, alog, dskip_row, norm_g):
    T = xs_c.shape[0]
    nc = T // CHUNK
    ex, _, tri, _ = _ssd_consts()

    def body(xs_ref, bc_ref, dt_ref, z_ref, alog_ref, dsk_ref, ng_ref, ex_ref, tri_ref,
             ys_ref, ypre_ref, hprev_ref, yst_ref, Hs, ybuf):
        @pl.when(pl.program_id(0) == 0)
        def _():
            Hs[...] = jnp.zeros_like(Hs)

        hprev_ref[0] = Hs[...]
        xs, dt = xs_ref[...], dt_ref[...]
        a, A, dtex, expA, dec, cd = _ssd_common(xs, dt, alog_ref, ex_ref, tri_ref)
        AT = A.T
        xdt = xs * dtex
        xdec = xdt * dec
        _, _, causal = _decay_mask()
        for g in range(2):
            gs = slice(g * 512, (g + 1) * 512)
            B = bc_ref[:, g * N_STATE:(g + 1) * N_STATE]
            C = bc_ref[:, 256 + g * N_STATE:256 + (g + 1) * N_STATE]
            cb = _dot(C, B, _NT)
            Hg = Hs[gs, :]
            yoff = _dot(C, Hg, _NT) * expA[:, gs]
            for j in range(8):
                h = g * 8 + j
                hs = slice(h * HEAD, (h + 1) * HEAD)
                L = jnp.exp(jnp.where(causal, A[:, h:h + 1] - AT[h:h + 1, :], -1e30))
                ybuf[:, hs] = _dot(cb * L, xdt[:, hs]) + yoff[:, j * HEAD:(j + 1) * HEAD]
            Hs[gs, :] = cd[gs, :] * Hg + _dot(xdec[:, gs], B, _TN)
        ypre = ybuf[...] + dsk_ref[...] * xs
        ypre_ref[...] = ypre
        z = z_ref[...]
        yz = ypre * (z * _sigmoid(z))
        for g in range(2):
            gs = slice(g * 512, (g + 1) * 512)
            v = yz[:, gs]
            r = lax.rsqrt(jnp.mean(v * v, axis=-1, keepdims=True) + RMS_EPS)
            ys_ref[:, gs] = (v * r * ng_ref[:, gs]).astype(BF16)
        yst_ref[...] = ys_ref[...].T

    return pl.pallas_call(
        body, name="ssd_fwd", grid=(nc,),
        in_specs=[_row(CHUNK, 1024), _row(CHUNK, 512), _row(CHUNK, LANES), _row(CHUNK, 1024, C_Z),
                  _full((1, LANES)), _full((1, 1024)), _full((1, 1024)), _full((LANES, 1024)), _full((CHUNK, CHUNK))],
        out_specs=[_row(CHUNK, 1024), _row(CHUNK, 1024), pl.BlockSpec((1, 1024, N_STATE), lambda c: (c, 0, 0)),
                   _colt(1024, CHUNK)],
        out_shape=[S((T, 2048), BF16), S((T, 1024), F32), S((nc, 1024, N_STATE), F32), S((2048, T), BF16)],
        scratch_shapes=[pltpu.VMEM((1024, N_STATE), F32), pltpu.VMEM((CHUNK, 1024), F32)],
        compiler_params=_params(("arbitrary",)))(xs_c, bc_c, dt, proj, alog, dskip_row, norm_g, ex, tri)


def _ssd_bwd(dproj, xs_c, bc_c, dt, dt_raw, dt_bias, proj, ypre, hprev, dmix, alog, dskip_row, norm_g, exchange=None):
    T = xs_c.shape[0]
    nc = T // CHUNK
    ex, ext, tri, triu = _ssd_consts()
    rev = lambda n, col=0: pl.BlockSpec((CHUNK, n), lambda c: (nc - 1 - c, col))
    xbufs, kinds = exchange if exchange is not None else ((), ())
    nx = len(xbufs)
    N_IN, N_OUT = 17, 8

    def body(*refs):
        (dproj_ref, xs_ref, bc_ref, dt_ref, dtr_ref, dtb_ref, z_ref, ypre_ref, hprev_ref, dys_ref, alog_ref, dsk_ref,
         ng_ref, ex_ref, ext_ref, tri_ref, triu_ref) = refs[:N_IN]
        (dxs_ref, dbc_ref, ddt_ref, dz_ref, dng_ref, ddsk_ref, dalog_ref,
         ddtb_ref) = refs[N_IN + nx:N_IN + nx + N_OUT]
        dHs, dxbuf, dskacc = refs[N_IN + N_OUT + 2 * nx:N_IN + N_OUT + 2 * nx + 3]
        c = pl.program_id(0)
        if nx:
            start, finish = _exchange_plan(refs[N_IN:N_IN + nx], refs[N_IN + nx + N_OUT:N_IN + N_OUT + 2 * nx], kinds,
                                           *refs[N_IN + N_OUT + 2 * nx + 3:])
            pl.when(c == 0)(start)

        @pl.when(c == 0)
        def _():
            dHs[...] = jnp.zeros_like(dHs)
            dng_ref[...] = jnp.zeros_like(dng_ref)
            dalog_ref[...] = jnp.zeros_like(dalog_ref)
            ddtb_ref[...] = jnp.zeros_like(ddtb_ref)
            dskacc[...] = jnp.zeros_like(dskacc)

        xs, dt, z, ypre, dys = xs_ref[...], dt_ref[...], z_ref[...], ypre_ref[...], dys_ref[...]
        sg = _sigmoid(z)
        sz = z * sg
        yz = ypre * sz
        dyz_parts = []
        for g in range(2):
            gs = slice(g * 512, (g + 1) * 512)
            v = yz[:, gs]
            r = lax.rsqrt(jnp.mean(v * v, axis=-1, keepdims=True) + RMS_EPS)
            vn = v * r
            dng_ref[:, gs] += jnp.sum(dys[:, gs] * vn, axis=0, keepdims=True)
            dvn = dys[:, gs] * ng_ref[:, gs]
            dyz_parts.append(r * (dvn - vn * jnp.mean(dvn * vn, axis=-1, keepdims=True)))
        dyz = jnp.concatenate(dyz_parts, axis=1)
        dy = dyz * sz
        dz_ref[...] = (dyz * ypre * _dsilu(z, sg)).astype(BF16)
        dskacc[...] += jnp.sum(dy * xs, axis=0, keepdims=True)

        a, A, dtex, expA, dec, cd = _ssd_common(xs, dt, alog_ref, ex_ref, tri_ref)
        AT = A.T
        xdt = xs * dtex
        xdec = xdt * dec
        dye = dy * expA
        H = hprev_ref[0]
        dHn = dHs[...]
        sub, lane, causal = _decay_mask()
        dAc = jnp.zeros((CHUNK, LANES), F32)
        Rm = jnp.zeros((CHUNK, LANES), F32)
        yoff_parts, q_parts = [], []
        for g in range(2):
            gs = slice(g * 512, (g + 1) * 512)
            B = bc_ref[:, g * N_STATE:(g + 1) * N_STATE]
            C = bc_ref[:, 256 + g * N_STATE:256 + (g + 1) * N_STATE]
            cb = _dot(C, B, _NT)
            Hg, dHg = H[gs, :], dHn[gs, :]
            Q = _dot(B, dHg, _NT)
            yoff_parts.append(_dot(C, Hg, _NT) * expA[:, gs])
            q_parts.append(Q)
            dcb = jnp.zeros((CHUNK, CHUNK), F32)
            for j in range(8):
                h = g * 8 + j
                hs = slice(h * HEAD, (h + 1) * HEAD)
                L = jnp.exp(jnp.where(causal, A[:, h:h + 1] - AT[h:h + 1, :], -1e30))
                M = cb * L
                G = _dot(dy[:, hs], xdt[:, hs], _NT)
                dxbuf[:, hs] = _dot(M, dy[:, hs], _TN)
                dcb = dcb + G * L
                E = G * M
                dAc = jnp.where(lane == h, jnp.sum(E, axis=1, keepdims=True), dAc)
                Rm = jnp.where(sub == h, jnp.sum(E, axis=0, keepdims=True), Rm)
            dbc_ref[:, g * N_STATE:(g + 1) * N_STATE] = _dot(dcb, C, _TN) + _dot(xdec[:, gs], dHg)
            dbc_ref[:, 256 + g * N_STATE:256 + (g + 1) * N_STATE] = _dot(dcb, B) + _dot(dye[:, gs], Hg)
            dHs[gs, :] = cd[gs, :] * dHg + _dot(dye[:, gs], C, _TN)
        yoff = jnp.concatenate(yoff_parts, axis=1)
        Qd = jnp.concatenate(q_parts, axis=1) * dec
        dxdt = dxbuf[...] + Qd
        extm = ext_ref[...]
        red_s = _dot_sel_b(xdt * Qd, extm)
        dA = dAc - Rm.T + _dot_sel_b(dy * yoff, extm) - red_s
        hd = jnp.sum(_dot_sel_b(H * dHn, extm, _TN), axis=0, keepdims=True)
        last_add = jnp.sum(red_s, axis=0, keepdims=True) + jnp.exp(A[CHUNK - 1:CHUNK, :]) * hd
        dA = dA + jnp.where(sub == CHUNK - 1, last_add, 0.0)
        dadt = _dot_sel_a(triu_ref[...], dA)
        ddtr = (dadt * a + _dot_sel_b(dxdt * xs, extm)) * _sigmoid(dtr_ref[...] + dtb_ref[...])
        ddt_ref[...] = ddtr.astype(BF16)
        ddtb_ref[...] += jnp.sum(ddtr, axis=0, keepdims=True)
        dalog_ref[...] += jnp.sum(dadt * dt, axis=0, keepdims=True) * a
        dxs_ref[...] = dxdt * dtex + dsk_ref[...] * dy

        @pl.when(c == nc - 1)
        def _():
            ddsk_ref[...] = _dot_sel_b(jnp.broadcast_to(dskacc[...], (8, 1024)), extm)[0:1, :]

        if nx:
            pl.when(c == nc - 1)(finish)

    return pl.pallas_call(
        body, name="ssd_bwd", grid=(nc,),
        in_specs=[_ANY, rev(1024), rev(512), rev(LANES), rev(LANES), _full((1, LANES)), rev(1024, C_Z), rev(1024),
                  pl.BlockSpec((1, 1024, N_STATE), lambda c: (nc - 1 - c, 0, 0)), rev(1024, 0),
                  _full((1, LANES)), _full((1, 1024)), _full((1, 1024)),
                  _full((LANES, 1024)), _full((1024, LANES)), _full((CHUNK, CHUNK)), _full((CHUNK, CHUNK))] + [_ANY] * nx,
        out_specs=[rev(1024), rev(512), rev(LANES), rev(1024, C_Z), _full((1, 1024)), _full((1, LANES)),
                   _full((1, LANES)), _full((1, LANES))] + [_ANY] * nx,
        out_shape=[S((T, 1024), F32), S((T, 512), F32), S((T, LANES), BF16), S(dproj.shape, BF16),
                   S((1, 1024), F32), S((1, LANES), F32), S((1, LANES), F32), S((1, LANES), F32)]
        + _exchange_shapes(xbufs, kinds),
        input_output_aliases={0: 3},
        scratch_shapes=[pltpu.VMEM((1024, N_STATE), F32), pltpu.VMEM((CHUNK, 1024), F32), pltpu.VMEM((1, 1024), F32)]
        + (_exchange_sems(nx) if nx else []),
        compiler_params=_params(("arbitrary",)))(
            dproj, xs_c, bc_c, dt, dt_raw, dt_bias, proj, ypre, hprev, dmix, alog, dskip_row, norm_g, ex, ext, tri, triu,
            *xbufs)


def _shifted_copies(ext, ext8):
    n = ext8.shape[1]
    for r in range(8):
        ext8[r] = ext[pl.ds(r, n), :]


def _shifted(ext8, off, rows):
    return ext8[off % 8, pl.ds(off - off % 8, rows), :]


def _conf_fwd(mix, mixt, proj, w, cb, lg, lb, ba, bb):
    T = proj.shape[0]
    H = HALO_CONF

    def body(mix_ref, mixt_ref, ga_ref, gap_ref, gb_ref, gbp_ref, cg_ref, w_ref, cb_ref, lg_ref, lb_ref, ba_ref,
             bb_ref, u1_ref, yc_ref, yct_ref, ext, ext8):
        first = pl.program_id(0) == 0
        ext[H + TB:, :] = jnp.zeros((8, LANES), F32)

        def blk(cols):
            up = (gap_ref[:, cols] + ba_ref[:, cols]) * _sigmoid(gbp_ref[:, cols] + bb_ref[:, cols])
            ext[0:H, :] = jnp.where(first, 0.0, up)
            ext[H:H + TB, :] = (ga_ref[:, cols] + ba_ref[:, cols]) * _sigmoid(gb_ref[:, cols] + bb_ref[:, cols])
            _shifted_copies(ext, ext8)
            for r0 in range(0, TB, 64):
                acc = jnp.broadcast_to(cb_ref[:, cols], (64, LANES))
                for k in range(K_CONF):
                    acc = acc + w_ref[k:k + 1, cols] * _shifted(ext8, r0 + H - (K_CONF - 1) + k, 64)
                u1_ref[pl.ds(r0, 64), cols] = acc
        _col_loop(D_CONF, blk)

        def rows(rs):
            xh, _ = _ln_stats(u1_ref[rs, :])
            u2 = xh * lg_ref[...] + lb_ref[...]
            cg = cg_ref[rs, :]
            yc_ref[rs, :] = (u2 * _sigmoid(u2) * cg * _sigmoid(cg)).astype(BF16)
        _row_loop(TB, rows)
        yct_ref[...] = yc_ref[...].T

    return pl.pallas_call(
        body, name="conf_fwd", grid=(T // TB,),
        in_specs=[_ANY, _ANY, _row(TB, 1024, C_GLUA), _prev(TB, H, 1024, C_GLUA), _row(TB, 1024, C_GLUB),
                  _prev(TB, H, 1024, C_GLUB), _row(TB, 1024, C_CG), _full((K_CONF, 1024))] + [_full((1, 1024))] * 5,
        out_specs=[_row(TB, 1024), _row(TB, 1024, 1), _colt(1024, TB, 1)],
        out_shape=[S((T, 1024), F32), S((T, 2048), BF16), S((2048, T), BF16)],
        input_output_aliases={0: 1, 1: 2},
        scratch_shapes=[pltpu.VMEM((H + TB + 8, LANES), F32), pltpu.VMEM((8, H + TB, LANES), F32)],
        compiler_params=_params(("parallel",)))(mix, mixt, proj, proj, proj, proj, proj, w, cb, lg, lb, ba, bb)


def _conf_bwd1(dmix, u1, proj, lg, lb):
    T = u1.shape[0]

    def body(dy_ref, u1_ref, cg_ref, lg_ref, lb_ref, du1_ref, dcg_ref, dg_ref, db_ref):
        @pl.when(pl.program_id(0) == 0)
        def _():
            dg_ref[...] = jnp.zeros_like(dg_ref)
            db_ref[...] = jnp.zeros_like(db_ref)

        def rows(rs):
            xh, r = _ln_stats(u1_ref[rs, :])
            u2 = xh * lg_ref[...] + lb_ref[...]
            s2 = _sigmoid(u2)
            cg = cg_ref[rs, :]
            sc = _sigmoid(cg)
            dy = dy_ref[rs, :]
            dcg_ref[rs, :] = (dy * u2 * s2 * _dsilu(cg, sc)).astype(BF16)
            dv, dg, db = _ln_bwd(dy * cg * sc * _dsilu(u2, s2), xh, r, lg_ref[...])
            dg_ref[...] += dg
            db_ref[...] += db
            du1_ref[rs, :] = dv
        _row_loop(TB, rows)

    return pl.pallas_call(
        body, name="conf_bwd1", grid=(T // TB,),
        in_specs=[_row(TB, 1024, 1), _row(TB, 1024), _row(TB, 1024, C_CG), _full((1, 1024)), _full((1, 1024))],
        out_specs=[_row(TB, 1024), _row(TB, 1024, C_CG), _full((1, 1024)), _full((1, 1024))],
        out_shape=[S((T, 1024), F32), S((T, N_MAIN), BF16), S((1, 1024), F32), S((1, 1024), F32)],
        compiler_params=_params(("arbitrary",)))(dmix, u1, proj, lg, lb)


def _conf_bwd2(dproj, proj, du1, w, ba, bb):
    T = du1.shape[0]
    nt = T // TB
    H = HALO_CONF

    def body(dproj_ref, ga_ref, gap_ref, gb_ref, gbp_ref, du_ref, dun_ref, w_ref, ba_ref, bb_ref,
             dg_ref, dw_ref, dcb_ref, dba_ref, dbb_ref, ext, dext, ext8, dext8, dwacc):
        i = pl.program_id(0)
        first, last = i == 0, i == nt - 1

        @pl.when(first)
        def _():
            for r in (dcb_ref, dba_ref, dbb_ref, dwacc):
                r[...] = jnp.zeros_like(r)

        ext[H + TB:, :] = jnp.zeros((8, LANES), F32)
        dext[H + TB:, :] = jnp.zeros((8, LANES), F32)

        def blk(cols):
            cols_b = pl.ds(pl.multiple_of(cols.start + D_CONF, LANES), LANES)
            up = (gap_ref[:, cols] + ba_ref[:, cols]) * _sigmoid(gbp_ref[:, cols] + bb_ref[:, cols])
            ext[0:H, :] = jnp.where(first, 0.0, up)
            a = ga_ref[:, cols] + ba_ref[:, cols]
            sb = _sigmoid(gb_ref[:, cols] + bb_ref[:, cols])
            ext[H:H + TB, :] = a * sb
            du = du_ref[:, cols]
            dext[0:TB, :] = du
            dext[TB:TB + H, :] = jnp.where(last, 0.0, dun_ref[:, cols])
            _shifted_copies(ext, ext8)
            _shifted_copies(dext, dext8)
            dcb_ref[:, cols] += jnp.sum(du, axis=0, keepdims=True)
            for r0 in range(0, TB, 64):
                dur = du_ref[pl.ds(r0, 64), cols]
                acc = jnp.zeros((64, LANES), F32)
                for k in range(K_CONF):
                    prod = dur * _shifted(ext8, r0 + H - (K_CONF - 1) + k, 64)
                    dwacc[k * 8:(k + 1) * 8, cols] += prod.reshape(8, 8, LANES).sum(axis=0)
                    acc = acc + w_ref[k:k + 1, cols] * _shifted(dext8, r0 + K_CONF - 1 - k, 64)
                ar, sr = a[r0:r0 + 64], sb[r0:r0 + 64]
                da = acc * sr
                dbv = acc * ar * sr * (1.0 - sr)
                dg_ref[pl.ds(r0, 64), cols] = da.astype(BF16)
                dg_ref[pl.ds(r0, 64), cols_b] = dbv.astype(BF16)
                dba_ref[:, cols] += jnp.sum(da, axis=0, keepdims=True)
                dbb_ref[:, cols] += jnp.sum(dbv, axis=0, keepdims=True)
        _col_loop(D_CONF, blk)

        @pl.when(last)
        def _():
            dw_ref[...] = jnp.sum(dwacc[...].reshape(K_CONF, 8, D_CONF), axis=1)

    return pl.pallas_call(
        body, name="conf_bwd2", grid=(nt,),
        in_specs=[_ANY, _row(TB, 1024, C_GLUA), _prev(TB, H, 1024, C_GLUA), _row(TB, 1024, C_GLUB),
                  _prev(TB, H, 1024, C_GLUB), _row(TB, 1024), _next(TB, H, 1024, nt), _full((K_CONF, 1024)),
                  _full((1, 1024)), _full((1, 1024))],
        out_specs=[_row(TB, 2048), _full((K_CONF, 1024)), _full((1, 1024)), _full((1, 1024)), _full((1, 1024))],
        out_shape=[S(dproj.shape, BF16), S((K_CONF, 1024), F32)] + [S((1, 1024), F32)] * 3,
        input_output_aliases={0: 0},
        scratch_shapes=[pltpu.VMEM((H + TB + 8, LANES), F32), pltpu.VMEM((TB + H + 8, LANES), F32),
                        pltpu.VMEM((8, H + TB, LANES), F32), pltpu.VMEM((8, TB + H, LANES), F32),
                        pltpu.VMEM((K_CONF * 8, D_CONF), F32)],
        compiler_params=_params(("arbitrary",)))(dproj, proj, proj, proj, proj, du1, du1, w, ba, bb)


def _mesh_pos():
    x, y, c = lax.axis_index("x"), lax.axis_index("y"), lax.axis_index("c")
    return x, y, c, 4 * x + 2 * y + c


def _peer(x, y, c, k):
    return (x ^ ((k >> 2) & 1), y ^ ((k >> 1) & 1), c ^ (k & 1))


def _exchange_copies(ins, outs, kinds, send, recv, loc):
    nb = len(ins)
    x, y, c, me = _mesh_pos()
    src = lambda b, d: ins[b].at[d] if kinds[b] == "blocks" else ins[b]
    copies = [pltpu.make_async_copy(src(b, me), outs[b].at[me], loc.at[b]) for b in range(nb)]
    for k in range(1, N_DEV):
        px, py, pc = _peer(x, y, c, k)
        for b in range(nb):
            s = (k - 1) * nb + b
            copies.append(pltpu.make_async_remote_copy(
                src_ref=src(b, 4 * px + 2 * py + pc), dst_ref=outs[b].at[me], send_sem=send.at[s], recv_sem=recv.at[s],
                device_id=(px, py, pc), device_id_type=pl.DeviceIdType.MESH))
    return copies


def _exchange_shapes(bufs, kinds):
    return [S(b.shape if kd == "blocks" else (N_DEV,) + b.shape, b.dtype) for b, kd in zip(bufs, kinds)]


def _exchange_sems(nb):
    n = (N_DEV - 1) * nb
    return [pltpu.SemaphoreType.DMA((n,)), pltpu.SemaphoreType.DMA((n,)), pltpu.SemaphoreType.DMA((nb,))]


def _two_level_gather(ins, outs, send, recv, loc):
    nb = len(ins)
    x, y, c, me = _mesh_pos()
    here, sibling = (x, y, c), (x, y, 1 - c)
    chips = [(1 - x, y), (x, 1 - y), (1 - x, 1 - y)]

    def copy(slot, b, block, to, src=None):
        d = 4 * block[0] + 2 * block[1] + block[2]
        return pltpu.make_async_remote_copy(
            src_ref=outs[b].at[d] if src is None else src, dst_ref=outs[b].at[d],
            send_sem=send.at[slot * nb + b], recv_sem=recv.at[slot * nb + b],
            device_id=to, device_id_type=pl.DeviceIdType.MESH)

    mine = [pltpu.make_async_copy(ins[b], outs[b].at[me], loc.at[b]) for b in range(nb)]
    first = [copy(0, b, here, sibling, src=ins[b]) for b in range(nb)]
    first += [copy(1 + j, b, here, (*chip, c), src=ins[b]) for j, chip in enumerate(chips) for b in range(nb)]

    def start():
        for cp in mine + first:
            cp.start()

    def finish():
        passed = []
        for j, chip in enumerate(chips):
            for b in range(nb):
                copy(1 + j, b, (*chip, c), here).wait_recv()
            onward = [copy(4 + j, b, (*chip, c), sibling) for b in range(nb)]
            for cp in onward:
                cp.start()
            passed += onward
        for b in range(nb):
            copy(0, b, sibling, here).wait_recv()
        for j, chip in enumerate(chips):
            for b in range(nb):
                copy(4 + j, b, (*chip, 1 - c), here).wait_recv()
        for cp in first + passed:
            cp.wait_send()
        for cp in mine:
            cp.wait()

    return start, finish


def _exchange_plan(ins, outs, kinds, send, recv, loc):
    if all(kd == "gather" for kd in kinds):
        return _two_level_gather(ins, outs, send, recv, loc)
    copies = _exchange_copies(ins, outs, kinds, send, recv, loc)

    def start():
        for cp in copies:
            cp.start()

    def finish():
        for cp in copies:
            cp.wait()

    return start, finish


def _exchange(bufs, kinds, name):
    nb = len(bufs)

    def body(*refs):
        start, finish = _exchange_plan(refs[:nb], refs[nb:2 * nb], kinds, *refs[2 * nb:])
        start()
        finish()

    return pl.pallas_call(
        body, name=name, in_specs=[_ANY] * nb, out_specs=[_ANY] * nb,
        out_shape=_exchange_shapes(bufs, kinds), scratch_shapes=_exchange_sems(nb))(*bufs)


def _sum_parts(p_ref):
    acc = p_ref[0].astype(F32)
    for d in range(1, N_DEV):
        acc = acc + p_ref[d].astype(F32)
    return acc


def _adamw_math(g, w, m, v):
    m = ADAM_B1 * m + (1.0 - ADAM_B1) * g
    v = ADAM_B2 * v + (1.0 - ADAM_B2) * (g * g)
    m_hat = m / (1.0 - ADAM_B1 ** ADAM_STEP)
    v_hat = v / (1.0 - ADAM_B2 ** ADAM_STEP)
    return -ADAM_LR * (m_hat / (jnp.sqrt(v_hat) + ADAM_EPS) + ADAM_WD * w), m, v


HEAD_ROWS = 256


def _sum8_adamw(parts, w, m, v, name, head=None):
    _, R, C = w.shape
    tb = HEAD_ROWS if R % HEAD_ROWS == 0 else R
    nb = R // tb
    assert head is None or (tb == HEAD_ROWS and head.shape[1] == HEAD_ROWS and parts.shape[1] == R - HEAD_ROWS)
    skip = 0 if head is None else 1

    def body(*refs):
        p_ref, w_ref, m_ref, v_ref, g_ref, d_ref, mo_ref, vo_ref = refs[skip:]
        g = _sum_parts(p_ref)
        if head is not None:
            g = jnp.where(pl.program_id(0) == nb - 1, _sum_parts(refs[0]), g)
        g_ref[0] = g
        d_ref[0], mo_ref[0], vo_ref[0] = _adamw_math(g, w_ref[0], m_ref[0], v_ref[0])

    first = [] if head is None else [pl.BlockSpec((N_DEV, tb, C), lambda i: (0, 0, 0))]
    own = pl.BlockSpec((1, tb, C), lambda i: (0, i, 0))
    last_part = parts.shape[1] // tb - 1
    return pl.pallas_call(
        body, name=name, grid=(nb,),
        in_specs=first + [pl.BlockSpec((N_DEV, tb, C), lambda i: (0, jnp.minimum(i, last_part), 0))] + [own] * 3,
        out_specs=[own] * 4, out_shape=[S((1, R, C), F32)] * 4,
        compiler_params=_params(("parallel",)))(*([] if head is None else [head]), parts, w, m, v)


SMALL_LAYOUT = (
    ("ln_emb_g", 0, 1024), ("ln_emb_b", 0, 1024), ("ssm_conv_b", 0, 1024), ("ssm_conv_b", 1024, 512),
    ("dt_bias", 0, N_HEADS), ("a_log", 0, N_HEADS), ("d_skip", 0, N_HEADS), ("ssm_norm_g", 0, 1024),
    ("b_glu", 0, 1024), ("b_glu", 1024, 1024), ("conf_conv_b", 0, 1024), ("conf_ln_g", 0, 1024),
    ("conf_ln_b", 0, 1024), ("b_out", 0, 1024), ("ln1_g", 0, 1024), ("ln1_b", 0, 1024), ("ln2_g", 0, 1024),
    ("ln2_b", 0, 1024))
SMALL_ROWS = 24
SMALL = tuple(dict.fromkeys(n for n, _, _ in SMALL_LAYOUT))


LOSS_ROW = len(SMALL_LAYOUT)


def _pack_small(rows, loss):
    def body(*refs):
        o_ref = refs[-1]
        o_ref[...] = jnp.zeros_like(o_ref)
        for r, ref in enumerate(refs[:-2]):
            o_ref[r:r + 1, 0:ref.shape[1]] = ref[...]
        o_ref[LOSS_ROW:LOSS_ROW + 1, 0:LANES] = refs[-2][0:1, :]

    return pl.pallas_call(body, name="pack_small", out_shape=S((SMALL_ROWS, 1024), F32))(*rows, loss)


def _small_update(parts, w, m, v):
    def body(*refs):
        p_ref = refs[0]
        ins = {n: refs[1 + 3 * i:4 + 3 * i] for i, n in enumerate(SMALL)}
        o0 = 1 + 3 * len(SMALL)
        outs = {n: refs[o0 + 4 * i:o0 + 4 * i + 4] for i, n in enumerate(SMALL)}
        gsum = refs[-1]
        gsum[...] = _sum_parts(p_ref)
        refs[-2][...] = gsum[LOSS_ROW:LOSS_ROW + 1, 0:LANES]
        for r, (n, off, wd) in enumerate(SMALL_LAYOUT):
            cs = slice(off, off + wd)
            g = gsum[r:r + 1, 0:wd]
            w_ref, m_ref, v_ref = ins[n]
            g_ref, d_ref, mo_ref, vo_ref = outs[n]
            g_ref[:, cs] = g
            d_ref[:, cs], mo_ref[:, cs], vo_ref[:, cs] = _adamw_math(g, w_ref[:, cs], m_ref[:, cs], v_ref[:, cs])

    args = [parts] + [a for n in SMALL for a in (w[n], m[n], v[n])]
    res = pl.pallas_call(
        body, name="small_update",
        out_shape=[S(w[n].shape, F32) for n in SMALL for _ in range(4)] + [S((1, LANES), F32)],
        scratch_shapes=[pltpu.VMEM((SMALL_ROWS, 1024), F32)])(*args)
    return tuple({n: res[4 * i + j] for i, n in enumerate(SMALL)} for j in range(4)) + (res[-1],)


EARLY = ("w_in", "ssm_conv_w", "conf_conv_w")
LATE = ("w_out", "w_ple_gate", "w_ple_proj")


def _local_step(x, p, tgt, W, shards=None):
    r1 = lambda v: v.reshape(1, -1).astype(F32)
    pad_l = lambda v: jnp.pad(r1(v), ((0, 0), (0, LANES - v.size)))
    late = None if shards is None else [shards[n] for n in LATE]
    if shards is None:
        h0, h0b, h0bt = _ln_emb_fwd(x, r1(W["ln_emb_g"]), r1(W["ln_emb_b"]))
    else:
        h0, h0b, h0bt, *gathered = _ln_emb_fwd(x, r1(W["ln_emb_g"]), r1(W["ln_emb_b"]),
                                               exchange=([shards[n] for n in EARLY], ("gather",) * len(EARLY)))
        W = dict(W, **{n: a if n == "w_in" else _unstack_shards(a, BY_COLS[n]) for n, a in zip(EARLY, gathered)})
    w_main, w_dt = _w_in_to_main(W["w_in"])
    scw, scb = W["ssm_conv_w"], r1(W["ssm_conv_b"])
    wx, wb, bx, bb = scw[:, :1024], scw[:, 1024:], scb[:, :1024], scb[:, 1024:]
    dt_bias, alog = pad_l(W["dt_bias"]), pad_l(W["a_log"])
    dskip_row = jnp.repeat(W["d_skip"].reshape(-1), HEAD).reshape(1, -1)
    norm_g = r1(W["ssm_norm_g"])
    bglu = r1(W["b_glu"])
    ba, bbg = bglu[:, :1024], bglu[:, 1024:]
    ccw, ccb, clg, clb = W["conf_conv_w"], r1(W["conf_conv_b"]), r1(W["conf_ln_g"]), r1(W["conf_ln_b"])

    if late is None:
        proj = _mm(h0b, w_main, "nn", "in_proj", tm=TM_IN_PROJ)
    else:
        proj, *gathered = _mm(h0b, w_main, "nn", "in_proj", tm=TM_IN_PROJ, exchange=(late, ("gather",) * len(LATE)))
        W = dict(W, **{n: _unstack_shards(a, BY_COLS[n]) for n, a in zip(LATE, gathered)})
    dt_raw = _mm(h0b, w_dt, "nn", "in_proj_dt")
    xs_c, bc_c, dt = _ssd_pre_fwd(proj, dt_raw, wx, wb, bx, bb, dt_bias)
    mix, ypre, hprev, mixt = _ssd_fwd(xs_c, bc_c, dt, proj, alog, dskip_row, norm_g)
    u1, mix, mixt = _conf_fwd(mix, mixt, proj, ccw, ccb, clg, clb, ba, bbg)
    out, h1, h1b, h1bt = _out_proj_post1(mix, W["w_out"], h0, r1(W["b_out"]), r1(W["ln1_g"]), r1(W["ln1_b"]))
    pb = p.astype(BF16)
    dh1a, dgp, dple, loss, dln2g, dln2b = _ple_post2(h1b, W["w_ple_gate"], pb, W["w_ple_proj"], h1, tgt,
                                                      r1(W["ln2_g"]), r1(W["ln2_b"]))

    g = {}
    g["w_ple_proj"] = _mm(pb.T, dple, "nn", "d_ple_proj", out_dtype=BF16)
    g["w_ple_gate"] = _mm(h1bt, dgp, "nn", "d_ple_gate", out_dtype=BF16)
    dout, dh0a, dln1g, dln1b, dbout = _d_h1_post1_bwd(dgp, W["w_ple_gate"], dh1a, h0, out, r1(W["b_out"]),
                                                      r1(W["ln1_g"]))
    g["w_out"] = _mm(mixt, dout, "nn", "d_w_out", out_dtype=BF16)
    dmix = _mm(dout, W["w_out"], "nt", "d_mix")
    du1, dproj, dclg, dclb = _conf_bwd1(dmix, u1, proj, clg, clb)
    dproj, g["conf_conv_w"], dccb, dba, dbb = _conf_bwd2(dproj, proj, du1, ccw, ba, bbg)
    stack = lambda names: [_stack_shards(g[n], BY_COLS[n]) for n in names]
    dxs_c, dbc_c, ddtr, dproj, dng, ddsk, dalog, ddtb, *recv_a = _ssd_bwd(
        dproj, xs_c, bc_c, dt, dt_raw, dt_bias, proj, ypre, hprev, dmix, alog, dskip_row, norm_g,
        exchange=None if late is None else (stack(LATE), ("blocks",) * len(LATE)))
    dproj, dwx, dbx = _ssd_conv_bwd(dproj, proj, dxs_c, wx, bx, 1024, C_XS, "ssd_conv_bwd_x")
    dproj, dwb, dbb2 = _ssd_conv_bwd(dproj, proj, dbc_c, wb, bb, 512, C_BC, "ssd_conv_bwd_bc")
    g["ssm_conv_w"] = jnp.concatenate([dwx, dwb], axis=1)
    dw_dt = _mm(h0bt, ddtr, "nn", "d_w_dt", out_dtype=BF16)
    last_args = (dproj, w_main, ddtr, w_dt, dh0a, x, r1(W["ln_emb_g"]))
    if late is None:
        g["w_in"] = _w_in_blocks(_mm(h0bt, dproj, "nn", "d_w_in", out_dtype=BF16), dw_dt)
        grad_x, dlng, dlnb = _d_h0_ln_bwd(*last_args)
    else:
        r0 = D - HEAD_ROWS
        head = _w_in_blocks(_mm(h0bt, dproj, "nn", "d_w_in_head", out_dtype=BF16, tk=x.shape[0],
                                a_rows=(r0, HEAD_ROWS)), dw_dt[r0:])
        dw_rest, recv_head = _mm(h0bt, dproj, "nn", "d_w_in", out_dtype=BF16, a_rows=(0, r0),
                                 exchange=([head], ("blocks",)))
        last = ("ssm_conv_w", "conf_conv_w")
        grad_x, dlng, dlnb, *recv_b = _d_h0_ln_bwd(
            *last_args, exchange=([_w_in_blocks(dw_rest, dw_dt[:r0])] + stack(last), ("blocks",) * 3))
        g["recv"] = dict(zip(LATE + ("w_in",) + last, recv_a + recv_b), w_in_head=recv_head)
    g["rows"] = [dlng, dlnb, dbx, dbb2, ddtb, dalog, ddsk, dng, dba, dbb, dccb, dclg, dclb, dbout, dln1g, dln1b,
                 dln2g, dln2b]
    return loss, grad_x, g


W_IN_SEGMENTS = ((0, 2048, 2048), (2048, 5120, 512), (2560, None, N_HEADS), (2576, 0, 2048), (4624, 4096, 1024))


def _w_in_to_main(shards):
    def pieces(p0, width):
        out, p = [], p0
        while p < p0 + width:
            d = p // COLS_PER_DEV
            hi = min(p0 + width, (d + 1) * COLS_PER_DEV)
            out.append(shards[d][:, p - d * COLS_PER_DEV:hi - d * COLS_PER_DEV])
            p = hi
        return out
    main = [s for s in sorted(W_IN_SEGMENTS, key=lambda s: -1 if s[1] is None else s[1]) if s[1] is not None]
    w_main = jnp.concatenate([q for p0, _, width in main for q in pieces(p0, width)], axis=1)
    w_dt = jnp.concatenate(pieces(2560, N_HEADS), axis=1)
    return w_main, jnp.pad(w_dt, ((0, 0), (0, LANES - N_HEADS)))


def _w_in_blocks(dw_main, dw_dt):
    blocks = []
    for d in range(N_DEV):
        lo_d, hi_d = d * COLS_PER_DEV, (d + 1) * COLS_PER_DEV
        parts = []
        for p0, m0, width in W_IN_SEGMENTS:
            lo, hi = max(lo_d, p0), min(hi_d, p0 + width)
            if lo < hi:
                parts.append(dw_dt[:, lo - p0:hi - p0] if m0 is None else dw_main[:, m0 + lo - p0:m0 + hi - p0])
        blocks.append(jnp.concatenate(parts, axis=1))
    return jnp.stack(blocks)


WEIGHTS = ['ln_emb_g', 'ln_emb_b', 'w_in', 'ssm_conv_w', 'ssm_conv_b', 'dt_bias', 'a_log', 'd_skip', 'ssm_norm_g',
           'b_glu', 'conf_conv_w', 'conf_conv_b', 'conf_ln_g', 'conf_ln_b', 'w_out', 'b_out', 'ln1_g', 'ln1_b',
           'w_ple_gate', 'w_ple_proj', 'ln2_g', 'ln2_b']
SHARDED = (("w_in", True), ("w_out", False), ("w_ple_gate", False), ("w_ple_proj", True), ("ssm_conv_w", True),
           ("conf_conv_w", True))
BY_COLS = dict(SHARDED)


def _stack_shards(a, by_cols):
    if by_cols:
        return a.reshape(a.shape[0], N_DEV, a.shape[1] // N_DEV).transpose(1, 0, 2)
    return a.reshape(N_DEV, a.shape[0] // N_DEV, a.shape[1])


def _unstack_shards(a, by_cols):
    if by_cols:
        return a.transpose(1, 0, 2).reshape(a.shape[1], N_DEV * a.shape[2])
    return a.reshape(N_DEV * a.shape[1], a.shape[2])


def kernel(x, p, ln_emb_g, ln_emb_b, w_in, ssm_conv_w, ssm_conv_b, dt_bias, a_log, d_skip, ssm_norm_g, b_glu, conf_conv_w, conf_conv_b, conf_ln_g, conf_ln_b, w_out, b_out, ln1_g, ln1_b, w_ple_gate, w_ple_proj, ln2_g, ln2_b, loss_target, m_ln_emb_g, m_ln_emb_b, m_w_in, m_ssm_conv_w, m_ssm_conv_b, m_dt_bias, m_a_log, m_d_skip, m_ssm_norm_g, m_b_glu, m_conf_conv_w, m_conf_conv_b, m_conf_ln_g, m_conf_ln_b, m_w_out, m_b_out, m_ln1_g, m_ln1_b, m_w_ple_gate, m_w_ple_proj, m_ln2_g, m_ln2_b, v_ln_emb_g, v_ln_emb_b, v_w_in, v_ssm_conv_w, v_ssm_conv_b, v_dt_bias, v_a_log, v_d_skip, v_ssm_norm_g, v_b_glu, v_conf_conv_w, v_conf_conv_b, v_conf_ln_g, v_conf_ln_b, v_w_out, v_b_out, v_ln1_g, v_ln1_b, v_w_ple_gate, v_w_ple_proj, v_ln2_g, v_ln2_b):
    loc = dict(locals())
    w = {n: loc[n] for n in WEIGHTS}
    m = {n: loc["m_" + n] for n in WEIGHTS}
    v = {n: loc["v_" + n] for n in WEIGHTS}
    sharded = [n for n, _ in SHARDED]

    shards = {n: w[n][0].astype(BF16) if n.startswith("w_") else w[n][0] for n in sharded}
    W = {n: w[n].reshape(-1) for n in SMALL}
    loss, grad_x, g = _local_step(x[0], p[0, 0], loss_target[0], W, shards=shards)
    (recv_small,) = _exchange([_pack_small(g["rows"], loss)], ("all",), "small_exchange")

    grads, delta, new_m, new_v = {}, {}, {}, {}
    for n in sharded:
        grads[n], delta[n], new_m[n], new_v[n] = _sum8_adamw(
            g["recv"][n], w[n], m[n], v[n], "adamw_" + n, head=g["recv"]["w_in_head"] if n == "w_in" else None)
    two_d = lambda d: {n: d[n].reshape(1, -1) for n in SMALL}
    *small, loss = _small_update(recv_small, two_d(w), two_d(m), two_d(v))
    for dst, res in zip((grads, delta, new_m, new_v), small):
        for n in SMALL:
            dst[n] = res[n].reshape(w[n].shape)
    return (loss[0, 0], grad_x[None], *[grads[n] for n in WEIGHTS], *[delta[n] for n in WEIGHTS],
            *[new_m[n] for n in WEIGHTS], *[new_v[n] for n in WEIGHTS])
```

```python
import numpy as np
import jax
import jax.numpy as jnp
from jax import lax
from jax.experimental import pallas as pl
from jax.experimental.pallas import tpu as pltpu

F32, BF16 = jnp.float32, jnp.bfloat16
S = jax.ShapeDtypeStruct

N_DEV = 8
D = 1024
D_PLE = 256
D_SSM = 1024
D_CONF = 1024
N_HEADS = 16
HEAD = 64
N_STATE = 128
CHUNK = 128
K_SSM = 4
K_CONF = 31
D_IN = 5648
COLS_PER_DEV = D_IN // N_DEV
LN_EPS = 1e-5
RMS_EPS = 1e-5
ALPHA = 2.0 ** 0.25
LANES = 128
TB = 512
TB_SSD_CONV_BWD = 512
TM_IN_PROJ = 2048
TM_FUSED = 1024
RG = 32
ROW_UNROLL = 4
HALO_SSM = 8
HALO_CONF = 32
VMEM_LIMIT = 56 * 1024 * 1024

ADAM_LR, ADAM_B1, ADAM_B2, ADAM_EPS, ADAM_WD, ADAM_STEP = 0.001, 0.9, 0.999, 1e-08, 0.01, 10

C_GLUA, C_GLUB, C_XS, C_Z, C_CG = 0, 1, 2, 3, 4
C_BC = 10
N_MAIN = 5632


def _params(sem, vmem=VMEM_LIMIT):
    return pltpu.CompilerParams(dimension_semantics=sem, vmem_limit_bytes=vmem)


def _row(tb, n, col=0):
    return pl.BlockSpec((tb, n), lambda i: (i, col))


def _colt(n, tb, row=0):
    return pl.BlockSpec((n, tb), lambda i: (row, i))


def _full(shape):
    return pl.BlockSpec(shape, lambda i: (0,) * len(shape))


_ANY = pl.BlockSpec(memory_space=pl.ANY)


def _prev(tb, halo, n, col=0):
    r = tb // halo
    return pl.BlockSpec((halo, n), lambda i: (jnp.maximum(i * r - 1, 0), col))


def _next(tb, halo, n, nt, col=0):
    r = tb // halo
    return pl.BlockSpec((halo, n), lambda i: (jnp.minimum((i + 1) * r, nt * r - 1), col))


def _row_loop(tb, fn):
    def it(r, c):
        fn(pl.ds(pl.multiple_of(r * RG, RG), RG))
        return c
    lax.fori_loop(0, tb // RG, it, 0, unroll=ROW_UNROLL)


def _col_loop(n, fn):
    def it(j, c):
        fn(pl.ds(pl.multiple_of(j * LANES, LANES), LANES))
        return c
    lax.fori_loop(0, n // LANES, it, 0)


def _sigmoid(x):
    return 1.0 / (1.0 + jnp.exp(-x))


def _dsilu(x, s):
    return s * (1.0 + x * (1.0 - s))


def _ln_stats(v):
    mu = jnp.mean(v, axis=-1, keepdims=True)
    c = v - mu
    r = lax.rsqrt(jnp.mean(c * c, axis=-1, keepdims=True) + LN_EPS)
    return c * r, r


def _ln_bwd(dy, xhat, r, g):
    dxh = dy * g
    dv = r * (dxh - jnp.mean(dxh, axis=-1, keepdims=True) - xhat * jnp.mean(dxh * xhat, axis=-1, keepdims=True))
    return dv, jnp.sum(dy * xhat, axis=0, keepdims=True), jnp.sum(dy, axis=0, keepdims=True)


def _dot(a, b, dims=((1,), (0,))):
    return lax.dot_general(a.astype(BF16), b.astype(BF16), (dims, ((), ())), preferred_element_type=F32)


_NT = ((1,), (1,))
_TN = ((0,), (0,))


def _split3(x):
    hi = x.astype(BF16)
    r = x - hi.astype(F32)
    mid = r.astype(BF16)
    return hi, mid, (r - mid.astype(F32)).astype(BF16)


def _dot_sel_b(a, b, dims=((1,), (0,))):
    hi, mid, lo = _split3(a)
    return (_dot(lo, b, dims) + _dot(mid, b, dims)) + _dot(hi, b, dims)


def _dot_sel_a(a, b, dims=((1,), (0,))):
    hi, mid, lo = _split3(b)
    return (_dot(a, lo, dims) + _dot(a, mid, dims)) + _dot(a, hi, dims)


def _mm(a, b, mode, name, out_dtype=F32, add=None, tm=1024, tn=None, tk=1024, exchange=None, a_rows=None):
    assert mode in ("nn", "nt")
    (M, K), N = a.shape, b.shape[1 if mode == "nn" else 0]
    row0 = 0
    if a_rows is not None:
        row0, M = a_rows
        tm = M
        assert row0 % M == 0
    if tn is None:
        tn = next(t for t in (1024, 1408, 512, 256, LANES) if N % t == 0)
    tm, tn, tk = min(tm, M), min(tn, N), min(tk, K)
    assert M % tm == 0 and N % tn == 0 and K % tk == 0, (name, M, N, K)
    grid = (M // tm, N // tn, K // tk)
    nk = grid[2]
    dims = ((1,), (0,)) if mode == "nn" else _NT
    n_in = 2 + (add is not None)
    xbufs, kinds = exchange if exchange is not None else ((), ())
    nx = len(xbufs)

    def body(*refs):
        a_ref, b_ref = refs[:2]
        o_ref = refs[n_in + nx]
        acc = refs[n_in + 2 * nx + 1]
        i, j, k = pl.program_id(0), pl.program_id(1), pl.program_id(2)
        if nx:
            start, finish = _exchange_plan(refs[n_in:n_in + nx], refs[n_in + nx + 1:n_in + 2 * nx + 1], kinds,
                                           *refs[n_in + 2 * nx + 2:])
            pl.when((i == 0) & (j == 0) & (k == 0))(start)

        d = _dot(a_ref[...], b_ref[...], dims)

        def write_out(r):
            if add is not None:
                r = r + refs[2][...]
            o_ref[...] = r.astype(out_dtype)

        if nk == 1:
            write_out(d)
        else:
            @pl.when(k == 0)
            def _():
                acc[...] = d

            @pl.when((k > 0) & (k < nk - 1))
            def _():
                acc[...] += d

            @pl.when(k == nk - 1)
            def _():
                write_out(acc[...] + d)

        if nx:
            pl.when((i == grid[0] - 1) & (j == grid[1] - 1) & (k == nk - 1))(finish)

    a_spec = pl.BlockSpec((tm, tk), lambda i, j, k: (i + row0 // tm, k))
    b_spec = pl.BlockSpec((tn, tk), lambda i, j, k: (j, k)) if mode == "nt" else pl.BlockSpec((tk, tn), lambda i, j, k: (k, j))
    o_spec = pl.BlockSpec((tm, tn), lambda i, j, k: (i, j))
    ins, specs = [a, b], [a_spec, b_spec]
    if add is not None:
        ins.append(add)
        specs.append(o_spec)
    acc_spec = pltpu.VMEM((tm, tn) if nk > 1 else (8, LANES), F32)
    if not nx:
        return pl.pallas_call(
            body, name=name, grid=grid, in_specs=specs, out_specs=o_spec,
            out_shape=S((M, N), out_dtype), scratch_shapes=[acc_spec],
            compiler_params=_params(("parallel", "parallel", "arbitrary")))(*ins)
    return pl.pallas_call(
        body, name=name, grid=grid, in_specs=specs + [_ANY] * nx, out_specs=[o_spec] + [_ANY] * nx,
        out_shape=[S((M, N), out_dtype)] + _exchange_shapes(xbufs, kinds),
        scratch_shapes=[acc_spec] + _exchange_sems(nx),
        compiler_params=_params(("arbitrary", "arbitrary", "arbitrary")))(*ins, *xbufs)


def _ln_emb_fwd(x, g, b, exchange=None):
    T = x.shape[0]

    nt = T // TB
    xbufs, kinds = exchange if exchange is not None else ((), ())
    nx = len(xbufs)

    def body(*refs):
        x_ref, g_ref, b_ref = refs[:3]
        h_ref, hb_ref, hbt_ref = refs[3 + nx:6 + nx]
        i = pl.program_id(0)
        if nx:
            start, finish = _exchange_plan(refs[3:3 + nx], refs[6 + nx:6 + 2 * nx], kinds, *refs[6 + 2 * nx:])
            pl.when(i == 0)(start)

        def rows(rs):
            xh, _ = _ln_stats(x_ref[rs, :])
            h = xh * g_ref[...] + b_ref[...]
            h_ref[rs, :] = h
            hb_ref[rs, :] = h.astype(BF16)
        _row_loop(TB, rows)
        hbt_ref[...] = hb_ref[...].T
        if nx:
            pl.when(i == nt - 1)(finish)

    return pl.pallas_call(
        body, name="ln_emb_fwd", grid=(nt,),
        in_specs=[_row(TB, D), _full((1, D)), _full((1, D))] + [_ANY] * nx,
        out_specs=[_row(TB, D), _row(TB, D), _colt(D, TB)] + [_ANY] * nx,
        out_shape=[S((T, D), F32), S((T, D), BF16), S((D, T), BF16)] + _exchange_shapes(xbufs, kinds),
        scratch_shapes=_exchange_sems(nx) if nx else [],
        compiler_params=_params(("arbitrary",)))(x, g, b, *xbufs)


def _out_proj_post1(mix, w_out, h0, b_out, g, b, tm=TM_FUSED, tk=1024):
    T, K = mix.shape
    tm = min(tm, T)
    nk = K // tk
    assert T % tm == 0 and K % tk == 0 and nk >= 2

    def body(mix_ref, w_ref, h0_ref, bo_ref, g_ref, b_ref, out_ref, h_ref, hb_ref, hbt_ref, acc):
        k = pl.program_id(1)
        d = _dot(mix_ref[...], w_ref[...])

        @pl.when(k == 0)
        def _():
            acc[...] = d

        @pl.when((k > 0) & (k < nk - 1))
        def _():
            acc[...] += d

        @pl.when(k == nk - 1)
        def _():
            out_ref[...] = acc[...] + d

            def rows(rs):
                xh, _ = _ln_stats(ALPHA * h0_ref[rs, :] + out_ref[rs, :] + bo_ref[...])
                h = xh * g_ref[...] + b_ref[...]
                h_ref[rs, :] = h
                hb_ref[rs, :] = h.astype(BF16)
            _row_loop(tm, rows)
            hbt_ref[...] = hb_ref[...].T

    rowt = lambda n: pl.BlockSpec((tm, n), lambda i, k: (i, 0))
    const = pl.BlockSpec((1, D), lambda i, k: (0, 0))
    return pl.pallas_call(
        body, name="out_proj_post1", grid=(T // tm, nk),
        in_specs=[pl.BlockSpec((tm, tk), lambda i, k: (i, k)), pl.BlockSpec((tk, D), lambda i, k: (k, 0)), rowt(D),
                  const, const, const],
        out_specs=[rowt(D), rowt(D), rowt(D), pl.BlockSpec((D, tm), lambda i, k: (0, i))],
        out_shape=[S((T, D), F32), S((T, D), F32), S((T, D), BF16), S((D, T), BF16)],
        scratch_shapes=[pltpu.VMEM((tm, D), F32)],
        compiler_params=_params(("parallel", "arbitrary")))(mix, w_out, h0, b_out, g, b)


def _ple_post2(h1b, w_gate, pb, w_proj, h1, tgt, g, b, tm=TM_FUSED // 2):
    T = h1.shape[0]
    tm = min(tm, T)
    assert T % tm == 0

    def body(h1b_ref, wg_ref, pb_ref, wp_ref, h1_ref, tgt_ref, g_ref, b_ref,
             dh1_ref, dgp_ref, dple_ref, loss_ref, dg_ref, db_ref, gp_ref, ple_ref):
        @pl.when(pl.program_id(0) == 0)
        def _():
            loss_ref[...] = jnp.zeros_like(loss_ref)
            dg_ref[...] = jnp.zeros_like(dg_ref)
            db_ref[...] = jnp.zeros_like(db_ref)

        gp_ref[...] = _dot(h1b_ref[...], wg_ref[...])
        ple_ref[...] = _dot(pb_ref[...], wp_ref[...])

        def rows(rs):
            gate = _sigmoid(gp_ref[rs, :])
            ple = ple_ref[rs, :]
            xh, r = _ln_stats(ALPHA * h1_ref[rs, :] + gate * ple)
            err = xh * g_ref[...] + b_ref[...] - tgt_ref[rs, :]
            loss_ref[...] += 0.5 * jnp.sum(jnp.mean(err * err, axis=-1, keepdims=True), axis=0, keepdims=True)
            dv, dg, db = _ln_bwd(err * (1.0 / D), xh, r, g_ref[...])
            dg_ref[...] += dg
            db_ref[...] += db
            dh1_ref[rs, :] = ALPHA * dv
            dgp_ref[rs, :] = (dv * ple * gate * (1.0 - gate)).astype(BF16)
            dple_ref[rs, :] = (dv * gate).astype(BF16)
        _row_loop(tm, rows)

    return pl.pallas_call(
        body, name="ple_post2", grid=(T // tm,),
        in_specs=[_row(tm, D), _full((D, D)), _row(tm, D_PLE), _full((D_PLE, D)), _row(tm, D), _row(tm, D),
                  _full((1, D)), _full((1, D))],
        out_specs=[_row(tm, D)] * 3 + [_full((8, LANES)), _full((1, D)), _full((1, D))],
        out_shape=[S((T, D), F32), S((T, D), BF16), S((T, D), BF16), S((8, LANES), F32), S((1, D), F32), S((1, D), F32)],
        scratch_shapes=[pltpu.VMEM((tm, D), F32), pltpu.VMEM((tm, D), F32)],
        compiler_params=_params(("arbitrary",)))(h1b, w_gate, pb, w_proj, h1, tgt, g, b)


def _d_h1_post1_bwd(dgp, w_gate, dh1a, h0, out, b_out, g, tm=TM_FUSED // 2):
    T = h0.shape[0]
    tm = min(tm, T)
    assert T % tm == 0

    def body(dgp_ref, wg_ref, da_ref, h0_ref, out_ref, bo_ref, g_ref, dout_ref, dh0_ref, dg_ref, db_ref, dbo_ref, dh1):
        @pl.when(pl.program_id(0) == 0)
        def _():
            dg_ref[...] = jnp.zeros_like(dg_ref)
            db_ref[...] = jnp.zeros_like(db_ref)
            dbo_ref[...] = jnp.zeros_like(dbo_ref)

        dh1[...] = da_ref[...] + _dot(dgp_ref[...], wg_ref[...], _NT)

        def rows(rs):
            xh, r = _ln_stats(ALPHA * h0_ref[rs, :] + out_ref[rs, :] + bo_ref[...])
            dv, dg, db = _ln_bwd(dh1[rs, :], xh, r, g_ref[...])
            dg_ref[...] += dg
            db_ref[...] += db
            dbo_ref[...] += jnp.sum(dv, axis=0, keepdims=True)
            dout_ref[rs, :] = dv.astype(BF16)
            dh0_ref[rs, :] = ALPHA * dv
        _row_loop(tm, rows)

    return pl.pallas_call(
        body, name="d_h1_post1_bwd", grid=(T // tm,),
        in_specs=[_row(tm, D), _full((D, D))] + [_row(tm, D)] * 3 + [_full((1, D))] * 2,
        out_specs=[_row(tm, D)] * 2 + [_full((1, D))] * 3,
        out_shape=[S((T, D), BF16), S((T, D), F32)] + [S((1, D), F32)] * 3,
        scratch_shapes=[pltpu.VMEM((tm, D), F32)],
        compiler_params=_params(("arbitrary",)))(dgp, w_gate, dh1a, h0, out, b_out, g)


def _d_h0_ln_bwd(dproj, w_main, ddtr, w_dt, dh0a, x, g, exchange=None, tm=1024, tk=1408):
    T, K = dproj.shape
    tm = min(tm, T)
    assert T % tm == 0 and K % tk == 0
    ni, nk = T // tm, K // tk
    xbufs, kinds = exchange if exchange is not None else ((), ())
    nx = len(xbufs)

    def body(*refs):
        dp_ref, w_ref, dt_ref, wdt_ref, da_ref, x_ref, g_ref = refs[:7]
        dx_ref, dg_ref, db_ref = refs[7 + nx:10 + nx]
        acc = refs[10 + 2 * nx]
        i, k = pl.program_id(0), pl.program_id(1)
        if nx:
            start, finish = _exchange_plan(refs[7:7 + nx], refs[10 + nx:10 + 2 * nx], kinds, *refs[11 + 2 * nx:])
            pl.when((i == 0) & (k == 0))(start)

        @pl.when((i == 0) & (k == 0))
        def _():
            dg_ref[...] = jnp.zeros_like(dg_ref)
            db_ref[...] = jnp.zeros_like(db_ref)

        d = _dot(dp_ref[...], w_ref[...], _NT)

        @pl.when(k == 0)
        def _():
            acc[...] = da_ref[...] + _dot(dt_ref[...], wdt_ref[...], _NT) + d

        @pl.when(k > 0)
        def _():
            acc[...] += d

        @pl.when(k == nk - 1)
        def _():
            def rows(rs):
                xh, r = _ln_stats(x_ref[rs, :])
                dv, dg, db = _ln_bwd(acc[rs, :], xh, r, g_ref[...])
                dg_ref[...] += dg
                db_ref[...] += db
                dx_ref[rs, :] = dv
            _row_loop(tm, rows)

        if nx:
            pl.when((i == ni - 1) & (k == nk - 1))(finish)

    rowt = lambda n: pl.BlockSpec((tm, n), lambda i, k: (i, 0))
    const = lambda shape: pl.BlockSpec(shape, lambda i, k: (0, 0))
    return pl.pallas_call(
        body, name="d_h0_ln_bwd", grid=(ni, nk),
        in_specs=[pl.BlockSpec((tm, tk), lambda i, k: (i, k)), pl.BlockSpec((D, tk), lambda i, k: (0, k)),
                  rowt(LANES), const((D, LANES)), rowt(D), rowt(D), const((1, D))] + [_ANY] * nx,
        out_specs=[rowt(D), const((1, D)), const((1, D))] + [_ANY] * nx,
        out_shape=[S((T, D), F32), S((1, D), F32), S((1, D), F32)] + _exchange_shapes(xbufs, kinds),
        scratch_shapes=[pltpu.VMEM((tm, D), F32)] + (_exchange_sems(nx) if nx else []),
        compiler_params=_params(("arbitrary", "arbitrary")))(dproj, w_main, ddtr, w_dt, dh0a, x, g, *xbufs)


def _softplus(x):
    return jnp.maximum(x, 0.0) + jnp.log1p(jnp.exp(-jnp.abs(x)))


def _ssd_pre_fwd(proj, h0b, w_dt, wx, wb, bx, bb, dt_bias):
    T = proj.shape[0]
    H = HALO_SSM

    def body(xs_ref, xsp_ref, bc_ref, bcp_ref, h0b_ref, wdt_ref, wx_ref, wb_ref, bx_ref, bb_ref, dtb_ref,
             xso_ref, bco_ref, dto_ref, dtr_ref, extx, extb):
        first = pl.program_id(0) == 0

        def conv(t_ref, p_ref, w_ref, b_ref, o_ref, ext, n):
            def blk(cols):
                ext[0:H, cols] = jnp.where(first, 0.0, p_ref[:, cols])
                ext[H:, cols] = t_ref[:, cols]
                for r0 in range(0, TB, 64):
                    acc = jnp.broadcast_to(b_ref[:, cols], (64, LANES))
                    for k in range(K_SSM):
                        acc = acc + w_ref[k:k + 1, cols] * ext[pl.ds(r0 + H - (K_SSM - 1) + k, 64), cols]
                    o_ref[pl.ds(r0, 64), cols] = acc * _sigmoid(acc)
            _col_loop(n, blk)

        conv(xs_ref, xsp_ref, wx_ref, bx_ref, xso_ref, extx, D_SSM)
        conv(bc_ref, bcp_ref, wb_ref, bb_ref, bco_ref, extb, 512)
        dtr_ref[...] = _dot(h0b_ref[...], wdt_ref[...])
        dto_ref[...] = _softplus(dtr_ref[...] + dtb_ref[...])

    return pl.pallas_call(
        body, name="ssd_pre_fwd", grid=(T // TB,),
        in_specs=[_row(TB, 1024, C_XS), _prev(TB, H, 1024, C_XS), _row(TB, 512, C_BC), _prev(TB, H, 512, C_BC),
                  _row(TB, D), _full((D, LANES)), _full((K_SSM, 1024)), _full((K_SSM, 512)), _full((1, 1024)),
                  _full((1, 512)), _full((1, LANES))],
        out_specs=[_row(TB, 1024), _row(TB, 512), _row(TB, LANES), _row(TB, LANES)],
        out_shape=[S((T, 1024), F32), S((T, 512), F32), S((T, LANES), F32), S((T, LANES), F32)],
        scratch_shapes=[pltpu.VMEM((H + TB, 1024), F32), pltpu.VMEM((H + TB, 512), F32)],
        compiler_params=_params(("parallel",)))(proj, proj, proj, proj, h0b, w_dt, wx, wb, bx, bb, dt_bias)


def _ssd_conv_bwd(dproj, proj, d_c, w, b, n, col, name):
    TB = TB_SSD_CONV_BWD
    T = proj.shape[0]
    nt = T // TB
    H = HALO_SSM
    R = TB + H

    def body(dproj_ref, t_ref, p_ref, n_ref, d_ref, dn_ref, w_ref, b_ref, o_ref, dw_ref, dbias_ref, ext, dp):
        i = pl.program_id(0)
        first, last = i == 0, i == nt - 1

        @pl.when(first)
        def _():
            dw_ref[...] = jnp.zeros_like(dw_ref)
            dbias_ref[...] = jnp.zeros_like(dbias_ref)

        def blk(cols):
            ext[0:H, cols] = jnp.where(first, 0.0, p_ref[:, cols])
            ext[H:H + TB, cols] = t_ref[:, cols]
            ext[H + TB:, cols] = n_ref[:, cols]
            def taps_and_dsilu(r0, rows):
                taps = [ext[pl.ds(r0 + H - (K_SSM - 1) + k, rows), cols] for k in range(K_SSM)]
                pre = jnp.broadcast_to(b_ref[:, cols], (rows, LANES))
                for k in range(K_SSM):
                    pre = pre + w_ref[k:k + 1, cols] * taps[k]
                return taps, _dsilu(pre, _sigmoid(pre))

            for r0 in range(0, TB, 64):
                taps, ds = taps_and_dsilu(r0, 64)
                dpt = d_ref[pl.ds(r0, 64), cols] * ds
                dp[pl.ds(r0, 64), cols] = dpt
                dbias_ref[:, cols] += jnp.sum(dpt, axis=0, keepdims=True)
                for k in range(K_SSM):
                    dw_ref[k:k + 1, cols] += jnp.sum(dpt * taps[k], axis=0, keepdims=True)
            dp[TB:, cols] = jnp.where(last, 0.0, dn_ref[:, cols] * taps_and_dsilu(TB, H)[1])
            for r0 in range(0, TB, 64):
                acc = jnp.zeros((64, LANES), F32)
                for k in range(K_SSM):
                    acc = acc + w_ref[k:k + 1, cols] * dp[pl.ds(r0 + K_SSM - 1 - k, 64), cols]
                o_ref[pl.ds(r0, 64), cols] = acc.astype(BF16)
        _col_loop(n, blk)

    return pl.pallas_call(
        body, name=name, grid=(nt,),
        in_specs=[_ANY, _row(TB, n, col), _prev(TB, H, n, col), _next(TB, H, n, nt, col),
                  _row(TB, n), _next(TB, H, n, nt), _full((K_SSM, n)), _full((1, n))],
        out_specs=[_row(TB, n, col), _full((K_SSM, n)), _full((1, n))],
        out_shape=[S(dproj.shape, BF16), S((K_SSM, n), F32), S((1, n), F32)],
        input_output_aliases={0: 0},
        scratch_shapes=[pltpu.VMEM((H + TB + H, n), F32), pltpu.VMEM((R, n), F32)],
        compiler_params=_params(("arbitrary",)))(dproj, proj, proj, proj, d_c, d_c, w, b)


def _ssd_consts():
    ex = np.zeros((LANES, D_SSM), np.float32)
    for h in range(N_HEADS):
        ex[h, h * HEAD:(h + 1) * HEAD] = 1.0
    tri = np.tril(np.ones((CHUNK, CHUNK), np.float32))
    return jnp.asarray(ex), jnp.asarray(ex.T.copy()), jnp.asarray(tri), jnp.asarray(tri.T.copy())


def _ssd_common(xs, dt, alog_ref, ex_ref, tri_ref):
    lane = lax.broadcasted_iota(jnp.int32, (1, LANES), 1)
    a = jnp.where(lane < N_HEADS, -jnp.exp(alog_ref[...]), 0.0)
    A = _dot_sel_a(tri_ref[...], dt * a)
    ex = ex_ref[...]
    Aex = _dot_sel_b(A, ex)
    dtex = _dot_sel_b(dt, ex)
    expA = jnp.exp(Aex)
    dec = jnp.exp(Aex[CHUNK - 1:CHUNK, :] - Aex)
    cd = _dot_sel_a(ex, jnp.broadcast_to(jnp.exp(A.T[:, CHUNK - 1:CHUNK]), (LANES, LANES)), _TN)
    return a, A, dtex, expA, dec, cd


def _decay_mask():
    sub = lax.broadcasted_iota(jnp.int32, (CHUNK, CHUNK), 0)
    lane = lax.broadcasted_iota(jnp.int32, (CHUNK, CHUNK), 1)
    return sub, lane, sub >= lane


def _ssd_fwd(xs_c, bc_c, dt, proj, alog, dskip_row, norm_g):
    T = xs_c.shape[0]
    nc = T // CHUNK
    ex, _, tri, _ = _ssd_consts()

    def body(xs_ref, bc_ref, dt_ref, z_ref, alog_ref, dsk_ref, ng_ref, ex_ref, tri_ref,
             ys_ref, ypre_ref, hprev_ref, yst_ref, Hs, ybuf):
        @pl.when(pl.program_id(0) == 0)
        def _():
            Hs[...] = jnp.zeros_like(Hs)

        hprev_ref[0] = Hs[...]
        xs, dt = xs_ref[...], dt_ref[...]
        a, A, dtex, expA, dec, cd = _ssd_common(xs, dt, alog_ref, ex_ref, tri_ref)
        AT = A.T
        xdt = xs * dtex
        xdec = xdt * dec
        _, _, causal = _decay_mask()
        for g in range(2):
            gs = slice(g * 512, (g + 1) * 512)
            B = bc_ref[:, g * N_STATE:(g + 1) * N_STATE]
            C = bc_ref[:, 256 + g * N_STATE:256 + (g + 1) * N_STATE]
            cb = _dot(C, B, _NT)
            Hg = Hs[gs, :]
            yoff = _dot(C, Hg, _NT) * expA[:, gs]
            for j in range(8):
                h = g * 8 + j
                hs = slice(h * HEAD, (h + 1) * HEAD)
                L = jnp.exp(jnp.where(causal, A[:, h:h + 1] - AT[h:h + 1, :], -1e30))
                ybuf[:, hs] = _dot(cb * L, xdt[:, hs]) + yoff[:, j * HEAD:(j + 1) * HEAD]
            Hs[gs, :] = cd[gs, :] * Hg + _dot(xdec[:, gs], B, _TN)
        ypre = ybuf[...] + dsk_ref[...] * xs
        ypre_ref[...] = ypre
        z = z_ref[...]
        yz = ypre * (z * _sigmoid(z))
        for g in range(2):
            gs = slice(g * 512, (g + 1) * 512)
            v = yz[:, gs]
            r = lax.rsqrt(jnp.mean(v * v, axis=-1, keepdims=True) + RMS_EPS)
            ys_ref[:, gs] = (v * r * ng_ref[:, gs]).astype(BF16)
        yst_ref[...] = ys_ref[...].T

    return pl.pallas_call(
        body, name="ssd_fwd", grid=(nc,),
        in_specs=[_row(CHUNK, 1024), _row(CHUNK, 512), _row(CHUNK, LANES), _row(CHUNK, 1024, C_Z),
                  _full((1, LANES)), _full((1, 1024)), _full((1, 1024)), _full((LANES, 1024)), _full((CHUNK, CHUNK))],
        out_specs=[_row(CHUNK, 1024), _row(CHUNK, 1024), pl.BlockSpec((1, 1024, N_STATE), lambda c: (c, 0, 0)),
                   _colt(1024, CHUNK)],
        out_shape=[S((T, 2048), BF16), S((T, 1024), F32), S((nc, 1024, N_STATE), F32), S((2048, T), BF16)],
        scratch_shapes=[pltpu.VMEM((1024, N_STATE), F32), pltpu.VMEM((CHUNK, 1024), F32)],
        compiler_params=_params(("arbitrary",)))(xs_c, bc_c, dt, proj, alog, dskip_row, norm_g, ex, tri)


def _ssd_bwd(dproj, xs_c, bc_c, dt, dt_raw, dt_bias, proj, ypre, hprev, dmix, alog, dskip_row, norm_g, exchange=None):
    T = xs_c.shape[0]
    nc = T // CHUNK
    ex, ext, tri, triu = _ssd_consts()
    rev = lambda n, col=0: pl.BlockSpec((CHUNK, n), lambda c: (nc - 1 - c, col))
    xbufs, kinds = exchange if exchange is not None else ((), ())
    nx = len(xbufs)
    N_IN, N_OUT = 17, 8

    def body(*refs):
        (dproj_ref, xs_ref, bc_ref, dt_ref, dtr_ref, dtb_ref, z_ref, ypre_ref, hprev_ref, dys_ref, alog_ref, dsk_ref,
         ng_ref, ex_ref, ext_ref, tri_ref, triu_ref) = refs[:N_IN]
        (dxs_ref, dbc_ref, ddt_ref, dz_ref, dng_ref, ddsk_ref, dalog_ref,
         ddtb_ref) = refs[N_IN + nx:N_IN + nx + N_OUT]
        dHs, dxbuf, dskacc = refs[N_IN + N_OUT + 2 * nx:N_IN + N_OUT + 2 * nx + 3]
        c = pl.program_id(0)
        if nx:
            start, finish = _exchange_plan(refs[N_IN:N_IN + nx], refs[N_IN + nx + N_OUT:N_IN + N_OUT + 2 * nx], kinds,
                                           *refs[N_IN + N_OUT + 2 * nx + 3:])
            pl.when(c == 0)(start)

        @pl.when(c == 0)
        def _():
            dHs[...] = jnp.zeros_like(dHs)
            dng_ref[...] = jnp.zeros_like(dng_ref)
            dalog_ref[...] = jnp.zeros_like(dalog_ref)
            ddtb_ref[...] = jnp.zeros_like(ddtb_ref)
            dskacc[...] = jnp.zeros_like(dskacc)

        xs, dt, z, ypre, dys = xs_ref[...], dt_ref[...], z_ref[...], ypre_ref[...], dys_ref[...]
        sg = _sigmoid(z)
        sz = z * sg
        yz = ypre * sz
        dyz_parts = []
        for g in range(2):
            gs = slice(g * 512, (g + 1) * 512)
            v = yz[:, gs]
            r = lax.rsqrt(jnp.mean(v * v, axis=-1, keepdims=True) + RMS_EPS)
            vn = v * r
            dng_ref[:, gs] += jnp.sum(dys[:, gs] * vn, axis=0, keepdims=True)
            dvn = dys[:, gs] * ng_ref[:, gs]
            dyz_parts.append(r * (dvn - vn * jnp.mean(dvn * vn, axis=-1, keepdims=True)))
        dyz = jnp.concatenate(dyz_parts, axis=1)
        dy = dyz * sz
        dz_ref[...] = (dyz * ypre * _dsilu(z, sg)).astype(BF16)
        dskacc[...] += jnp.sum(dy * xs, axis=0, keepdims=True)

        a, A, dtex, expA, dec, cd = _ssd_common(xs, dt, alog_ref, ex_ref, tri_ref)
        AT = A.T
        xdt = xs * dtex
        xdec = xdt * dec
        dye = dy * expA
        H = hprev_ref[0]
        dHn = dHs[...]
        sub, lane, causal = _decay_mask()
        dAc = jnp.zeros((CHUNK, LANES), F32)
        Rm = jnp.zeros((CHUNK, LANES), F32)
        yoff_parts, q_parts = [], []
        for g in range(2):
            gs = slice(g * 512, (g + 1) * 512)
            B = bc_ref[:, g * N_STATE:(g + 1) * N_STATE]
            C = bc_ref[:, 256 + g * N_STATE:256 + (g + 1) * N_STATE]
            cb = _dot(C, B, _NT)
            Hg, dHg = H[gs, :], dHn[gs, :]
            Q = _dot(B, dHg, _NT)
            yoff_parts.append(_dot(C, Hg, _NT) * expA[:, gs])
            q_parts.append(Q)
            dcb = jnp.zeros((CHUNK, CHUNK), F32)
            for j in range(8):
                h = g * 8 + j
                hs = slice(h * HEAD, (h + 1) * HEAD)
                L = jnp.exp(jnp.where(causal, A[:, h:h + 1] - AT[h:h + 1, :], -1e30))
                M = cb * L
                G = _dot(dy[:, hs], xdt[:, hs], _NT)
                dxbuf[:, hs] = _dot(M, dy[:, hs], _TN)
                dcb = dcb + G * L
                E = G * M
                dAc = jnp.where(lane == h, jnp.sum(E, axis=1, keepdims=True), dAc)
                Rm = jnp.where(sub == h, jnp.sum(E, axis=0, keepdims=True), Rm)
            dbc_ref[:, g * N_STATE:(g + 1) * N_STATE] = _dot(dcb, C, _TN) + _dot(xdec[:, gs], dHg)
            dbc_ref[:, 256 + g * N_STATE:256 + (g + 1) * N_STATE] = _dot(dcb, B) + _dot(dye[:, gs], Hg)
            dHs[gs, :] = cd[gs, :] * dHg + _dot(dye[:, gs], C, _TN)
        yoff = jnp.concatenate(yoff_parts, axis=1)
        Qd = jnp.concatenate(q_parts, axis=1) * dec
        dxdt = dxbuf[...] + Qd
        extm = ext_ref[...]
        red_s = _dot_sel_b(xdt * Qd, extm)
        dA = dAc - Rm.T + _dot_sel_b(dy * yoff, extm) - red_s
        hd = jnp.sum(_dot_sel_b(H * dHn, extm, _TN), axis=0, keepdims=True)
        last_add = jnp.sum(red_s, axis=0, keepdims=True) + jnp.exp(A[CHUNK - 1:CHUNK, :]) * hd
        dA = dA + jnp.where(sub == CHUNK - 1, last_add, 0.0)
        dadt = _dot_sel_a(triu_ref[...], dA)
        ddtr = (dadt * a + _dot_sel_b(dxdt * xs, extm)) * _sigmoid(dtr_ref[...] + dtb_ref[...])
        ddt_ref[...] = ddtr.astype(BF16)
        ddtb_ref[...] += jnp.sum(ddtr, axis=0, keepdims=True)
        dalog_ref[...] += jnp.sum(dadt * dt, axis=0, keepdims=True) * a
        dxs_ref[...] = dxdt * dtex + dsk_ref[...] * dy

        @pl.when(c == nc - 1)
        def _():
            ddsk_ref[...] = _dot_sel_b(jnp.broadcast_to(dskacc[...], (8, 1024)), extm)[0:1, :]

        if nx:
            pl.when(c == nc - 1)(finish)

    return pl.pallas_call(
        body, name="ssd_bwd", grid=(nc,),
        in_specs=[_ANY, rev(1024), rev(512), rev(LANES), rev(LANES), _full((1, LANES)), rev(1024, C_Z), rev(1024),
                  pl.BlockSpec((1, 1024, N_STATE), lambda c: (nc - 1 - c, 0, 0)), rev(1024, 0),
                  _full((1, LANES)), _full((1, 1024)), _full((1, 1024)),
                  _full((LANES, 1024)), _full((1024, LANES)), _full((CHUNK, CHUNK)), _full((CHUNK, CHUNK))] + [_ANY] * nx,
        out_specs=[rev(1024), rev(512), rev(LANES), rev(1024, C_Z), _full((1, 1024)), _full((1, LANES)),
                   _full((1, LANES)), _full((1, LANES))] + [_ANY] * nx,
        out_shape=[S((T, 1024), F32), S((T, 512), F32), S((T, LANES), BF16), S(dproj.shape, BF16),
                   S((1, 1024), F32), S((1, LANES), F32), S((1, LANES), F32), S((1, LANES), F32)]
        + _exchange_shapes(xbufs, kinds),
        input_output_aliases={0: 3},
        scratch_shapes=[pltpu.VMEM((1024, N_STATE), F32), pltpu.VMEM((CHUNK, 1024), F32), pltpu.VMEM((1, 1024), F32)]
        + (_exchange_sems(nx) if nx else []),
        compiler_params=_params(("arbitrary",)))(
            dproj, xs_c, bc_c, dt, dt_raw, dt_bias, proj, ypre, hprev, dmix, alog, dskip_row, norm_g, ex, ext, tri, triu,
            *xbufs)


def _shifted_copies(ext, ext8):
    n = ext8.shape[1]
    for r in range(8):
        ext8[r] = ext[pl.ds(r, n), :]


def _shifted(ext8, off, rows):
    return ext8[off % 8, pl.ds(off - off % 8, rows), :]


def _conf_fwd(mix, mixt, proj, w, cb, lg, lb, ba, bb):
    T = proj.shape[0]
    H = HALO_CONF

    def body(mix_ref, mixt_ref, ga_ref, gap_ref, gb_ref, gbp_ref, cg_ref, w_ref, cb_ref, lg_ref, lb_ref, ba_ref,
             bb_ref, u1_ref, yc_ref, yct_ref, ext, ext8):
        first = pl.program_id(0) == 0
        ext[H + TB:, :] = jnp.zeros((8, LANES), F32)

        def blk(cols):
            up = (gap_ref[:, cols] + ba_ref[:, cols]) * _sigmoid(gbp_ref[:, cols] + bb_ref[:, cols])
            ext[0:H, :] = jnp.where(first, 0.0, up)
            ext[H:H + TB, :] = (ga_ref[:, cols] + ba_ref[:, cols]) * _sigmoid(gb_ref[:, cols] + bb_ref[:, cols])
            _shifted_copies(ext, ext8)
            for r0 in range(0, TB, 64):
                acc = jnp.broadcast_to(cb_ref[:, cols], (64, LANES))
                for k in range(K_CONF):
                    acc = acc + w_ref[k:k + 1, cols] * _shifted(ext8, r0 + H - (K_CONF - 1) + k, 64)
                u1_ref[pl.ds(r0, 64), cols] = acc
        _col_loop(D_CONF, blk)

        def rows(rs):
            xh, _ = _ln_stats(u1_ref[rs, :])
            u2 = xh * lg_ref[...] + lb_ref[...]
            cg = cg_ref[rs, :]
            yc_ref[rs, :] = (u2 * _sigmoid(u2) * cg * _sigmoid(cg)).astype(BF16)
        _row_loop(TB, rows)
        yct_ref[...] = yc_ref[...].T

    return pl.pallas_call(
        body, name="conf_fwd", grid=(T // TB,),
        in_specs=[_ANY, _ANY, _row(TB, 1024, C_GLUA), _prev(TB, H, 1024, C_GLUA), _row(TB, 1024, C_GLUB),
                  _prev(TB, H, 1024, C_GLUB), _row(TB, 1024, C_CG), _full((K_CONF, 1024))] + [_full((1, 1024))] * 5,
        out_specs=[_row(TB, 1024), _row(TB, 1024, 1), _colt(1024, TB, 1)],
        out_shape=[S((T, 1024), F32), S((T, 2048), BF16), S((2048, T), BF16)],
        input_output_aliases={0: 1, 1: 2},
        scratch_shapes=[pltpu.VMEM((H + TB + 8, LANES), F32), pltpu.VMEM((8, H + TB, LANES), F32)],
        compiler_params=_params(("parallel",)))(mix, mixt, proj, proj, proj, proj, proj, w, cb, lg, lb, ba, bb)


def _conf_bwd1(dmix, u1, proj, lg, lb):
    T = u1.shape[0]

    def body(dy_ref, u1_ref, cg_ref, lg_ref, lb_ref, du1_ref, dcg_ref, dg_ref, db_ref):
        @pl.when(pl.program_id(0) == 0)
        def _():
            dg_ref[...] = jnp.zeros_like(dg_ref)
            db_ref[...] = jnp.zeros_like(db_ref)

        def rows(rs):
            xh, r = _ln_stats(u1_ref[rs, :])
            u2 = xh * lg_ref[...] + lb_ref[...]
            s2 = _sigmoid(u2)
            cg = cg_ref[rs, :]
            sc = _sigmoid(cg)
            dy = dy_ref[rs, :]
            dcg_ref[rs, :] = (dy * u2 * s2 * _dsilu(cg, sc)).astype(BF16)
            dv, dg, db = _ln_bwd(dy * cg * sc * _dsilu(u2, s2), xh, r, lg_ref[...])
            dg_ref[...] += dg
            db_ref[...] += db
            du1_ref[rs, :] = dv
        _row_loop(TB, rows)

    return pl.pallas_call(
        body, name="conf_bwd1", grid=(T // TB,),
        in_specs=[_row(TB, 1024, 1), _row(TB, 1024), _row(TB, 1024, C_CG), _full((1, 1024)), _full((1, 1024))],
        out_specs=[_row(TB, 1024), _row(TB, 1024, C_CG), _full((1, 1024)), _full((1, 1024))],
        out_shape=[S((T, 1024), F32), S((T, N_MAIN), BF16), S((1, 1024), F32), S((1, 1024), F32)],
        compiler_params=_params(("arbitrary",)))(dmix, u1, proj, lg, lb)


def _conf_bwd2(dproj, proj, du1, w, ba, bb):
    T = du1.shape[0]
    nt = T // TB
    H = HALO_CONF

    def body(dproj_ref, ga_ref, gap_ref, gb_ref, gbp_ref, du_ref, dun_ref, w_ref, ba_ref, bb_ref,
             dg_ref, dw_ref, dcb_ref, dba_ref, dbb_ref, ext, dext, ext8, dext8, dwacc):
        i = pl.program_id(0)
        first, last = i == 0, i == nt - 1

        @pl.when(first)
        def _():
            for r in (dcb_ref, dba_ref, dbb_ref, dwacc):
                r[...] = jnp.zeros_like(r)

        ext[H + TB:, :] = jnp.zeros((8, LANES), F32)
        dext[H + TB:, :] = jnp.zeros((8, LANES), F32)

        def blk(cols):
            cols_b = pl.ds(pl.multiple_of(cols.start + D_CONF, LANES), LANES)
            up = (gap_ref[:, cols] + ba_ref[:, cols]) * _sigmoid(gbp_ref[:, cols] + bb_ref[:, cols])
            ext[0:H, :] = jnp.where(first, 0.0, up)
            a = ga_ref[:, cols] + ba_ref[:, cols]
            sb = _sigmoid(gb_ref[:, cols] + bb_ref[:, cols])
            ext[H:H + TB, :] = a * sb
            du = du_ref[:, cols]
            dext[0:TB, :] = du
            dext[TB:TB + H, :] = jnp.where(last, 0.0, dun_ref[:, cols])
            _shifted_copies(ext, ext8)
            _shifted_copies(dext, dext8)
            dcb_ref[:, cols] += jnp.sum(du, axis=0, keepdims=True)
            for r0 in range(0, TB, 64):
                dur = du_ref[pl.ds(r0, 64), cols]
                acc = jnp.zeros((64, LANES), F32)
                for k in range(K_CONF):
                    prod = dur * _shifted(ext8, r0 + H - (K_CONF - 1) + k, 64)
                    dwacc[k * 8:(k + 1) * 8, cols] += prod.reshape(8, 8, LANES).sum(axis=0)
                    acc = acc + w_ref[k:k + 1, cols] * _shifted(dext8, r0 + K_CONF - 1 - k, 64)
                ar, sr = a[r0:r0 + 64], sb[r0:r0 + 64]
                da = acc * sr
                dbv = acc * ar * sr * (1.0 - sr)
                dg_ref[pl.ds(r0, 64), cols] = da.astype(BF16)
                dg_ref[pl.ds(r0, 64), cols_b] = dbv.astype(BF16)
                dba_ref[:, cols] += jnp.sum(da, axis=0, keepdims=True)
                dbb_ref[:, cols] += jnp.sum(dbv, axis=0, keepdims=True)
        _col_loop(D_CONF, blk)

        @pl.when(last)
        def _():
            dw_ref[...] = jnp.sum(dwacc[...].reshape(K_CONF, 8, D_CONF), axis=1)

    return pl.pallas_call(
        body, name="conf_bwd2", grid=(nt,),
        in_specs=[_ANY, _row(TB, 1024, C_GLUA), _prev(TB, H, 1024, C_GLUA), _row(TB, 1024, C_GLUB),
                  _prev(TB, H, 1024, C_GLUB), _row(TB, 1024), _next(TB, H, 1024, nt), _full((K_CONF, 1024)),
                  _full((1, 1024)), _full((1, 1024))],
        out_specs=[_row(TB, 2048), _full((K_CONF, 1024)), _full((1, 1024)), _full((1, 1024)), _full((1, 1024))],
        out_shape=[S(dproj.shape, BF16), S((K_CONF, 1024), F32)] + [S((1, 1024), F32)] * 3,
        input_output_aliases={0: 0},
        scratch_shapes=[pltpu.VMEM((H + TB + 8, LANES), F32), pltpu.VMEM((TB + H + 8, LANES), F32),
                        pltpu.VMEM((8, H + TB, LANES), F32), pltpu.VMEM((8, TB + H, LANES), F32),
                        pltpu.VMEM((K_CONF * 8, D_CONF), F32)],
        compiler_params=_params(("arbitrary",)))(dproj, proj, proj, proj, proj, du1, du1, w, ba, bb)


def _mesh_pos():
    x, y, c = lax.axis_index("x"), lax.axis_index("y"), lax.axis_index("c")
    return x, y, c, 4 * x + 2 * y + c


def _peer(x, y, c, k):
    return (x ^ ((k >> 2) & 1), y ^ ((k >> 1) & 1), c ^ (k & 1))


def _exchange_copies(ins, outs, kinds, send, recv, loc):
    nb = len(ins)
    x, y, c, me = _mesh_pos()
    src = lambda b, d: ins[b].at[d] if kinds[b] == "blocks" else ins[b]
    copies = [pltpu.make_async_copy(src(b, me), outs[b].at[me], loc.at[b]) for b in range(nb)]
    for k in range(1, N_DEV):
        px, py, pc = _peer(x, y, c, k)
        for b in range(nb):
            s = (k - 1) * nb + b
            copies.append(pltpu.make_async_remote_copy(
                src_ref=src(b, 4 * px + 2 * py + pc), dst_ref=outs[b].at[me], send_sem=send.at[s], recv_sem=recv.at[s],
                device_id=(px, py, pc), device_id_type=pl.DeviceIdType.MESH))
    return copies


def _exchange_shapes(bufs, kinds):
    return [S(b.shape if kd == "blocks" else (N_DEV,) + b.shape, b.dtype) for b, kd in zip(bufs, kinds)]


def _exchange_sems(nb):
    n = (N_DEV - 1) * nb
    return [pltpu.SemaphoreType.DMA((n,)), pltpu.SemaphoreType.DMA((n,)), pltpu.SemaphoreType.DMA((nb,))]


def _two_level_gather(ins, outs, send, recv, loc):
    nb = len(ins)
    x, y, c, me = _mesh_pos()
    here, sibling = (x, y, c), (x, y, 1 - c)
    chips = [(1 - x, y), (x, 1 - y), (1 - x, 1 - y)]

    def copy(slot, b, block, to, src=None):
        d = 4 * block[0] + 2 * block[1] + block[2]
        return pltpu.make_async_remote_copy(
            src_ref=outs[b].at[d] if src is None else src, dst_ref=outs[b].at[d],
            send_sem=send.at[slot * nb + b], recv_sem=recv.at[slot * nb + b],
            device_id=to, device_id_type=pl.DeviceIdType.MESH)

    mine = [pltpu.make_async_copy(ins[b], outs[b].at[me], loc.at[b]) for b in range(nb)]
    first = [copy(0, b, here, sibling, src=ins[b]) for b in range(nb)]
    first += [copy(1 + j, b, here, (*chip, c), src=ins[b]) for j, chip in enumerate(chips) for b in range(nb)]

    def start():
        for cp in mine + first:
            cp.start()

    def finish():
        passed = []
        for j, chip in enumerate(chips):
            for b in range(nb):
                copy(1 + j, b, (*chip, c), here).wait_recv()
            onward = [copy(4 + j, b, (*chip, c), sibling) for b in range(nb)]
            for cp in onward:
                cp.start()
            passed += onward
        for b in range(nb):
            copy(0, b, sibling, here).wait_recv()
        for j, chip in enumerate(chips):
            for b in range(nb):
                copy(4 + j, b, (*chip, 1 - c), here).wait_recv()
        for cp in first + passed:
            cp.wait_send()
        for cp in mine:
            cp.wait()

    return start, finish


def _exchange_plan(ins, outs, kinds, send, recv, loc):
    if all(kd == "gather" for kd in kinds):
        return _two_level_gather(ins, outs, send, recv, loc)
    copies = _exchange_copies(ins, outs, kinds, send, recv, loc)

    def start():
        for cp in copies:
            cp.start()

    def finish():
        for cp in copies:
            cp.wait()

    return start, finish


def _exchange(bufs, kinds, name):
    nb = len(bufs)

    def body(*refs):
        start, finish = _exchange_plan(refs[:nb], refs[nb:2 * nb], kinds, *refs[2 * nb:])
        start()
        finish()

    return pl.pallas_call(
        body, name=name, in_specs=[_ANY] * nb, out_specs=[_ANY] * nb,
        out_shape=_exchange_shapes(bufs, kinds), scratch_shapes=_exchange_sems(nb))(*bufs)


def _sum_parts(p_ref):
    acc = p_ref[0].astype(F32)
    for d in range(1, N_DEV):
        acc = acc + p_ref[d].astype(F32)
    return acc


def _adamw_math(g, w, m, v):
    m = ADAM_B1 * m + (1.0 - ADAM_B1) * g
    v = ADAM_B2 * v + (1.0 - ADAM_B2) * (g * g)
    m_hat = m / (1.0 - ADAM_B1 ** ADAM_STEP)
    v_hat = v / (1.0 - ADAM_B2 ** ADAM_STEP)
    return -ADAM_LR * (m_hat / (jnp.sqrt(v_hat) + ADAM_EPS) + ADAM_WD * w), m, v


HEAD_ROWS = 256


def _sum8_adamw(parts, w, m, v, name, head=None):
    _, R, C = w.shape
    tb = HEAD_ROWS if R % HEAD_ROWS == 0 else R
    nb = R // tb
    assert head is None or (tb == HEAD_ROWS and head.shape[1] == HEAD_ROWS and parts.shape[1] == R - HEAD_ROWS)
    skip = 0 if head is None else 1

    def body(*refs):
        p_ref, w_ref, m_ref, v_ref, g_ref, d_ref, mo_ref, vo_ref = refs[skip:]
        g = _sum_parts(p_ref)
        if head is not None:
            g = jnp.where(pl.program_id(0) == nb - 1, _sum_parts(refs[0]), g)
        g_ref[0] = g
        d_ref[0], mo_ref[0], vo_ref[0] = _adamw_math(g, w_ref[0], m_ref[0], v_ref[0])

    first = [] if head is None else [pl.BlockSpec((N_DEV, tb, C), lambda i: (0, 0, 0))]
    own = pl.BlockSpec((1, tb, C), lambda i: (0, i, 0))
    last_part = parts.shape[1] // tb - 1
    return pl.pallas_call(
        body, name=name, grid=(nb,),
        in_specs=first + [pl.BlockSpec((N_DEV, tb, C), lambda i: (0, jnp.minimum(i, last_part), 0))] + [own] * 3,
        out_specs=[own] * 4, out_shape=[S((1, R, C), F32)] * 4,
        compiler_params=_params(("parallel",)))(*([] if head is None else [head]), parts, w, m, v)


SMALL_LAYOUT = (
    ("ln_emb_g", 0, 1024), ("ln_emb_b", 0, 1024), ("ssm_conv_b", 0, 1024), ("ssm_conv_b", 1024, 512),
    ("dt_bias", 0, N_HEADS), ("a_log", 0, N_HEADS), ("d_skip", 0, N_HEADS), ("ssm_norm_g", 0, 1024),
    ("b_glu", 0, 1024), ("b_glu", 1024, 1024), ("conf_conv_b", 0, 1024), ("conf_ln_g", 0, 1024),
    ("conf_ln_b", 0, 1024), ("b_out", 0, 1024), ("ln1_g", 0, 1024), ("ln1_b", 0, 1024), ("ln2_g", 0, 1024),
    ("ln2_b", 0, 1024))
SMALL_ROWS = 24
SMALL = tuple(dict.fromkeys(n for n, _, _ in SMALL_LAYOUT))


LOSS_ROW = len(SMALL_LAYOUT)


def _pack_small(rows, loss):
    def body(*refs):
        o_ref = refs[-1]
        o_ref[...] = jnp.zeros_like(o_ref)
        for r, ref in enumerate(refs[:-2]):
            o_ref[r:r + 1, 0:ref.shape[1]] = ref[...]
        o_ref[LOSS_ROW:LOSS_ROW + 1, 0:LANES] = refs[-2][0:1, :]

    return pl.pallas_call(body, name="pack_small", out_shape=S((SMALL_ROWS, 1024), F32))(*rows, loss)


def _small_update(parts, w, m, v):
    def body(*refs):
        p_ref = refs[0]
        ins = {n: refs[1 + 3 * i:4 + 3 * i] for i, n in enumerate(SMALL)}
        o0 = 1 + 3 * len(SMALL)
        outs = {n: refs[o0 + 4 * i:o0 + 4 * i + 4] for i, n in enumerate(SMALL)}
        gsum = refs[-1]
        gsum[...] = _sum_parts(p_ref)
        refs[-2][...] = gsum[LOSS_ROW:LOSS_ROW + 1, 0:LANES]
        for r, (n, off, wd) in enumerate(SMALL_LAYOUT):
            cs = slice(off, off + wd)
            g = gsum[r:r + 1, 0:wd]
            w_ref, m_ref, v_ref = ins[n]
            g_ref, d_ref, mo_ref, vo_ref = outs[n]
            g_ref[:, cs] = g
            d_ref[:, cs], mo_ref[:, cs], vo_ref[:, cs] = _adamw_math(g, w_ref[:, cs], m_ref[:, cs], v_ref[:, cs])

    args = [parts] + [a for n in SMALL for a in (w[n], m[n], v[n])]
    res = pl.pallas_call(
        body, name="small_update",
        out_shape=[S(w[n].shape, F32) for n in SMALL for _ in range(4)] + [S((1, LANES), F32)],
        scratch_shapes=[pltpu.VMEM((SMALL_ROWS, 1024), F32)])(*args)
    return tuple({n: res[4 * i + j] for i, n in enumerate(SMALL)} for j in range(4)) + (res[-1],)


EARLY = ("w_in", "ssm_conv_w", "conf_conv_w")
LATE = ("w_out", "w_ple_gate", "w_ple_proj")


def _local_step(x, p, tgt, W, shards=None):
    r1 = lambda v: v.reshape(1, -1).astype(F32)
    pad_l = lambda v: jnp.pad(r1(v), ((0, 0), (0, LANES - v.size)))
    late = None if shards is None else [shards[n] for n in LATE]
    if shards is None:
        h0, h0b, h0bt = _ln_emb_fwd(x, r1(W["ln_emb_g"]), r1(W["ln_emb_b"]))
    else:
        h0, h0b, h0bt, *gathered = _ln_emb_fwd(x, r1(W["ln_emb_g"]), r1(W["ln_emb_b"]),
                                               exchange=([shards[n] for n in EARLY], ("gather",) * len(EARLY)))
        W = dict(W, **{n: a if n == "w_in" else _unstack_shards(a, BY_COLS[n]) for n, a in zip(EARLY, gathered)})
    w_main, w_dt = _w_in_to_main(W["w_in"])
    scw, scb = W["ssm_conv_w"], r1(W["ssm_conv_b"])
    wx, wb, bx, bb = scw[:, :1024], scw[:, 1024:], scb[:, :1024], scb[:, 1024:]
    dt_bias, alog = pad_l(W["dt_bias"]), pad_l(W["a_log"])
    dskip_row = jnp.repeat(W["d_skip"].reshape(-1), HEAD).reshape(1, -1)
    norm_g = r1(W["ssm_norm_g"])
    bglu = r1(W["b_glu"])
    ba, bbg = bglu[:, :1024], bglu[:, 1024:]
    ccw, ccb, clg, clb = W["conf_conv_w"], r1(W["conf_conv_b"]), r1(W["conf_ln_g"]), r1(W["conf_ln_b"])

    if late is None:
        proj = _mm(h0b, w_main, "nn", "in_proj", tm=TM_IN_PROJ)
    else:
        proj, *gathered = _mm(h0b, w_main, "nn", "in_proj", tm=TM_IN_PROJ, exchange=(late, ("gather",) * len(LATE)))
        W = dict(W, **{n: _unstack_shards(a, BY_COLS[n]) for n, a in zip(LATE, gathered)})
    xs_c, bc_c, dt, dt_raw = _ssd_pre_fwd(proj, h0b, w_dt, wx, wb, bx, bb, dt_bias)
    mix, ypre, hprev, mixt = _ssd_fwd(xs_c, bc_c, dt, proj, alog, dskip_row, norm_g)
    u1, mix, mixt = _conf_fwd(mix, mixt, proj, ccw, ccb, clg, clb, ba, bbg)
    out, h1, h1b, h1bt = _out_proj_post1(mix, W["w_out"], h0, r1(W["b_out"]), r1(W["ln1_g"]), r1(W["ln1_b"]))
    pb = p.astype(BF16)
    dh1a, dgp, dple, loss, dln2g, dln2b = _ple_post2(h1b, W["w_ple_gate"], pb, W["w_ple_proj"], h1, tgt,
                                                      r1(W["ln2_g"]), r1(W["ln2_b"]))

    g = {}
    g["w_ple_proj"] = _mm(pb.T, dple, "nn", "d_ple_proj", out_dtype=BF16)
    g["w_ple_gate"] = _mm(h1bt, dgp, "nn", "d_ple_gate", out_dtype=BF16)
    dout, dh0a, dln1g, dln1b, dbout = _d_h1_post1_bwd(dgp, W["w_ple_gate"], dh1a, h0, out, r1(W["b_out"]),
                                                      r1(W["ln1_g"]))
    g["w_out"] = _mm(mixt, dout, "nn", "d_w_out", out_dtype=BF16)
    dmix = _mm(dout, W["w_out"], "nt", "d_mix")
    du1, dproj, dclg, dclb = _conf_bwd1(dmix, u1, proj, clg, clb)
    dproj, g["conf_conv_w"], dccb, dba, dbb = _conf_bwd2(dproj, proj, du1, ccw, ba, bbg)
    stack = lambda names: [_stack_shards(g[n], BY_COLS[n]) for n in names]
    dxs_c, dbc_c, ddtr, dproj, dng, ddsk, dalog, ddtb, *recv_a = _ssd_bwd(
        dproj, xs_c, bc_c, dt, dt_raw, dt_bias, proj, ypre, hprev, dmix, alog, dskip_row, norm_g,
        exchange=None if late is None else (stack(LATE), ("blocks",) * len(LATE)))
    dproj, dwx, dbx = _ssd_conv_bwd(dproj, proj, dxs_c, wx, bx, 1024, C_XS, "ssd_conv_bwd_x")
    dproj, dwb, dbb2 = _ssd_conv_bwd(dproj, proj, dbc_c, wb, bb, 512, C_BC, "ssd_conv_bwd_bc")
    g["ssm_conv_w"] = jnp.concatenate([dwx, dwb], axis=1)
    dw_dt = _mm(h0bt, ddtr, "nn", "d_w_dt", out_dtype=BF16)
    last_args = (dproj, w_main, ddtr, w_dt, dh0a, x, r1(W["ln_emb_g"]))
    if late is None:
        g["w_in"] = _w_in_blocks(_mm(h0bt, dproj, "nn", "d_w_in", out_dtype=BF16), dw_dt)
        grad_x, dlng, dlnb = _d_h0_ln_bwd(*last_args)
    else:
        r0 = D - HEAD_ROWS
        head = _w_in_blocks(_mm(h0bt, dproj, "nn", "d_w_in_head", out_dtype=BF16, tk=x.shape[0],
                                a_rows=(r0, HEAD_ROWS)), dw_dt[r0:])
        dw_rest, recv_head = _mm(h0bt, dproj, "nn", "d_w_in", out_dtype=BF16, a_rows=(0, r0),
                                 exchange=([head], ("blocks",)))
        last = ("ssm_conv_w", "conf_conv_w")
        grad_x, dlng, dlnb, *recv_b = _d_h0_ln_bwd(
            *last_args, exchange=([_w_in_blocks(dw_rest, dw_dt[:r0])] + stack(last), ("blocks",) * 3))
        g["recv"] = dict(zip(LATE + ("w_in",) + last, recv_a + recv_b), w_in_head=recv_head)
    g["rows"] = [dlng, dlnb, dbx, dbb2, ddtb, dalog, ddsk, dng, dba, dbb, dccb, dclg, dclb, dbout, dln1g, dln1b,
                 dln2g, dln2b]
    return loss, grad_x, g


W_IN_SEGMENTS = ((0, 2048, 2048), (2048, 5120, 512), (2560, None, N_HEADS), (2576, 0, 2048), (4624, 4096, 1024))


def _w_in_to_main(shards):
    def pieces(p0, width):
        out, p = [], p0
        while p < p0 + width:
            d = p // COLS_PER_DEV
            hi = min(p0 + width, (d + 1) * COLS_PER_DEV)
            out.append(shards[d][:, p - d * COLS_PER_DEV:hi - d * COLS_PER_DEV])
            p = hi
        return out
    main = [s for s in sorted(W_IN_SEGMENTS, key=lambda s: -1 if s[1] is None else s[1]) if s[1] is not None]
    w_main = jnp.concatenate([q for p0, _, width in main for q in pieces(p0, width)], axis=1)
    w_dt = jnp.concatenate(pieces(2560, N_HEADS), axis=1)
    return w_main, jnp.pad(w_dt, ((0, 0), (0, LANES - N_HEADS)))


def _w_in_blocks(dw_main, dw_dt):
    blocks = []
    for d in range(N_DEV):
        lo_d, hi_d = d * COLS_PER_DEV, (d + 1) * COLS_PER_DEV
        parts = []
        for p0, m0, width in W_IN_SEGMENTS:
            lo, hi = max(lo_d, p0), min(hi_d, p0 + width)
            if lo < hi:
                parts.append(dw_dt[:, lo - p0:hi - p0] if m0 is None else dw_main[:, m0 + lo - p0:m0 + hi - p0])
        blocks.append(jnp.concatenate(parts, axis=1))
    return jnp.stack(blocks)


WEIGHTS = ['ln_emb_g', 'ln_emb_b', 'w_in', 'ssm_conv_w', 'ssm_conv_b', 'dt_bias', 'a_log', 'd_skip', 'ssm_norm_g',
           'b_glu', 'conf_conv_w', 'conf_conv_b', 'conf_ln_g', 'conf_ln_b', 'w_out', 'b_out', 'ln1_g', 'ln1_b',
           'w_ple_gate', 'w_ple_proj', 'ln2_g', 'ln2_b']
SHARDED = (("w_in", True), ("w_out", False), ("w_ple_gate", False), ("w_ple_proj", True), ("ssm_conv_w", True),
           ("conf_conv_w", True))
BY_COLS = dict(SHARDED)


def _stack_shards(a, by_cols):
    if by_cols:
        return a.reshape(a.shape[0], N_DEV, a.shape[1] // N_DEV).transpose(1, 0, 2)
    return a.reshape(N_DEV, a.shape[0] // N_DEV, a.shape[1])


def _unstack_shards(a, by_cols):
    if by_cols:
        return a.transpose(1, 0, 2).reshape(a.shape[1], N_DEV * a.shape[2])
    return a.reshape(N_DEV * a.shape[1], a.shape[2])


def kernel(x, p, ln_emb_g, ln_emb_b, w_in, ssm_conv_w, ssm_conv_b, dt_bias, a_log, d_skip, ssm_norm_g, b_glu, conf_conv_w, conf_conv_b, conf_ln_g, conf_ln_b, w_out, b_out, ln1_g, ln1_b, w_ple_gate, w_ple_proj, ln2_g, ln2_b, loss_target, m_ln_emb_g, m_ln_emb_b, m_w_in, m_ssm_conv_w, m_ssm_conv_b, m_dt_bias, m_a_log, m_d_skip, m_ssm_norm_g, m_b_glu, m_conf_conv_w, m_conf_conv_b, m_conf_ln_g, m_conf_ln_b, m_w_out, m_b_out, m_ln1_g, m_ln1_b, m_w_ple_gate, m_w_ple_proj, m_ln2_g, m_ln2_b, v_ln_emb_g, v_ln_emb_b, v_w_in, v_ssm_conv_w, v_ssm_conv_b, v_dt_bias, v_a_log, v_d_skip, v_ssm_norm_g, v_b_glu, v_conf_conv_w, v_conf_conv_b, v_conf_ln_g, v_conf_ln_b, v_w_out, v_b_out, v_ln1_g, v_ln1_b, v_w_ple_gate, v_w_ple_proj, v_ln2_g, v_ln2_b):
    loc = dict(locals())
    w = {n: loc[n] for n in WEIGHTS}
    m = {n: loc["m_" + n] for n in WEIGHTS}
    v = {n: loc["v_" + n] for n in WEIGHTS}
    sharded = [n for n, _ in SHARDED]

    shards = {n: w[n][0].astype(BF16) if n.startswith("w_") else w[n][0] for n in sharded}
    W = {n: w[n].reshape(-1) for n in SMALL}
    loss, grad_x, g = _local_step(x[0], p[0, 0], loss_target[0], W, shards=shards)
    (recv_small,) = _exchange([_pack_small(g["rows"], loss)], ("all",), "small_exchange")

    grads, delta, new_m, new_v = {}, {}, {}, {}
    for n in sharded:
        grads[n], delta[n], new_m[n], new_v[n] = _sum8_adamw(
            g["recv"][n], w[n], m[n], v[n], "adamw_" + n, head=g["recv"]["w_in_head"] if n == "w_in" else None)
    two_d = lambda d: {n: d[n].reshape(1, -1) for n in SMALL}
    *small, loss = _small_update(recv_small, two_d(w), two_d(m), two_d(v))
    for dst, res in zip((grads, delta, new_m, new_v), small):
        for n in SMALL:
            dst[n] = res[n].reshape(w[n].shape)
    return (loss[0, 0], grad_x[None], *[grads[n] for n in WEIGHTS], *[delta[n] for n in WEIGHTS],
            *[new_m[n] for n in WEIGHTS], *[new_v[n] for n in WEIGHTS])
```

```python
import numpy as np
import jax
import jax.numpy as jnp
from jax import lax
from jax.experimental import pallas as pl
from jax.experimental.pallas import tpu as pltpu

F32, BF16 = jnp.float32, jnp.bfloat16
S = jax.ShapeDtypeStruct

N_DEV = 8
D = 1024
D_PLE = 256
D_SSM = 1024
D_CONF = 1024
N_HEADS = 16
HEAD = 64
N_STATE = 128
CHUNK = 128
K_SSM = 4
K_CONF = 31
D_IN = 5648
COLS_PER_DEV = D_IN // N_DEV
LN_EPS = 1e-5
RMS_EPS = 1e-5
ALPHA = 2.0 ** 0.25
LANES = 128
TB = 512
TB_SSD_CONV_BWD = 512
TM_IN_PROJ = 2048
TM_FUSED = 1024
RG = 32
ROW_UNROLL = 4
HALO_SSM = 8
HALO_CONF = 32
VMEM_LIMIT = 56 * 1024 * 1024

ADAM_LR, ADAM_B1, ADAM_B2, ADAM_EPS, ADAM_WD, ADAM_STEP = 0.001, 0.9, 0.999, 1e-08, 0.01, 10

C_GLUA, C_GLUB, C_XS, C_Z, C_CG = 0, 1, 2, 3, 4
C_BC = 10
N_MAIN = 5632


def _params(sem, vmem=VMEM_LIMIT):
    return pltpu.CompilerParams(dimension_semantics=sem, vmem_limit_bytes=vmem)


def _row(tb, n, col=0):
    return pl.BlockSpec((tb, n), lambda i: (i, col))


def _colt(n, tb, row=0):
    return pl.BlockSpec((n, tb), lambda i: (row, i))


def _full(shape):
    return pl.BlockSpec(shape, lambda i: (0,) * len(shape))


_ANY = pl.BlockSpec(memory_space=pl.ANY)


def _prev(tb, halo, n, col=0):
    r = tb // halo
    return pl.BlockSpec((halo, n), lambda i: (jnp.maximum(i * r - 1, 0), col))


def _next(tb, halo, n, nt, col=0):
    r = tb // halo
    return pl.BlockSpec((halo, n), lambda i: (jnp.minimum((i + 1) * r, nt * r - 1), col))


def _row_loop(tb, fn):
    def it(r, c):
        fn(pl.ds(pl.multiple_of(r * RG, RG), RG))
        return c
    lax.fori_loop(0, tb // RG, it, 0, unroll=ROW_UNROLL)


def _col_loop(n, fn):
    def it(j, c):
        fn(pl.ds(pl.multiple_of(j * LANES, LANES), LANES))
        return c
    lax.fori_loop(0, n // LANES, it, 0)


def _sigmoid(x):
    return 1.0 / (1.0 + jnp.exp(-x))


def _dsilu(x, s):
    return s * (1.0 + x * (1.0 - s))


def _ln_stats(v):
    mu = jnp.mean(v, axis=-1, keepdims=True)
    c = v - mu
    r = lax.rsqrt(jnp.mean(c * c, axis=-1, keepdims=True) + LN_EPS)
    return c * r, r


def _ln_bwd(dy, xhat, r, g):
    dxh = dy * g
    dv = r * (dxh - jnp.mean(dxh, axis=-1, keepdims=True) - xhat * jnp.mean(dxh * xhat, axis=-1, keepdims=True))
    return dv, jnp.sum(dy * xhat, axis=0, keepdims=True), jnp.sum(dy, axis=0, keepdims=True)


def _dot(a, b, dims=((1,), (0,))):
    return lax.dot_general(a.astype(BF16), b.astype(BF16), (dims, ((), ())), preferred_element_type=F32)


_NT = ((1,), (1,))
_TN = ((0,), (0,))


def _split3(x):
    hi = x.astype(BF16)
    r = x - hi.astype(F32)
    mid = r.astype(BF16)
    return hi, mid, (r - mid.astype(F32)).astype(BF16)


def _dot_sel_b(a, b, dims=((1,), (0,))):
    hi, mid, lo = _split3(a)
    return (_dot(lo, b, dims) + _dot(mid, b, dims)) + _dot(hi, b, dims)


def _dot_sel_a(a, b, dims=((1,), (0,))):
    hi, mid, lo = _split3(b)
    return (_dot(a, lo, dims) + _dot(a, mid, dims)) + _dot(a, hi, dims)


def _mm(a, b, mode, name, out_dtype=F32, add=None, tm=1024, tn=None, tk=1024, exchange=None, a_rows=None):
    assert mode in ("nn", "nt")
    (M, K), N = a.shape, b.shape[1 if mode == "nn" else 0]
    row0 = 0
    if a_rows is not None:
        row0, M = a_rows
        tm = M
        assert row0 % M == 0
    if tn is None:
        tn = next(t for t in (1024, 1408, 512, 256, LANES) if N % t == 0)
    tm, tn, tk = min(tm, M), min(tn, N), min(tk, K)
    assert M % tm == 0 and N % tn == 0 and K % tk == 0, (name, M, N, K)
    grid = (M // tm, N // tn, K // tk)
    nk = grid[2]
    dims = ((1,), (0,)) if mode == "nn" else _NT
    n_in = 2 + (add is not None)
    xbufs, kinds = exchange if exchange is not None else ((), ())
    nx = len(xbufs)

    def body(*refs):
        a_ref, b_ref = refs[:2]
        o_ref = refs[n_in + nx]
        acc = refs[n_in + 2 * nx + 1]
        i, j, k = pl.program_id(0), pl.program_id(1), pl.program_id(2)
        if nx:
            start, finish = _exchange_plan(refs[n_in:n_in + nx], refs[n_in + nx + 1:n_in + 2 * nx + 1], kinds,
                                           *refs[n_in + 2 * nx + 2:])
            pl.when((i == 0) & (j == 0) & (k == 0))(start)

        d = _dot(a_ref[...], b_ref[...], dims)

        def write_out(r):
            if add is not None:
                r = r + refs[2][...]
            o_ref[...] = r.astype(out_dtype)

        if nk == 1:
            write_out(d)
        else:
            @pl.when(k == 0)
            def _():
                acc[...] = d

            @pl.when((k > 0) & (k < nk - 1))
            def _():
                acc[...] += d

            @pl.when(k == nk - 1)
            def _():
                write_out(acc[...] + d)

        if nx:
            pl.when((i == grid[0] - 1) & (j == grid[1] - 1) & (k == nk - 1))(finish)

    a_spec = pl.BlockSpec((tm, tk), lambda i, j, k: (i + row0 // tm, k))
    b_spec = pl.BlockSpec((tn, tk), lambda i, j, k: (j, k)) if mode == "nt" else pl.BlockSpec((tk, tn), lambda i, j, k: (k, j))
    o_spec = pl.BlockSpec((tm, tn), lambda i, j, k: (i, j))
    ins, specs = [a, b], [a_spec, b_spec]
    if add is not None:
        ins.append(add)
        specs.append(o_spec)
    acc_spec = pltpu.VMEM((tm, tn) if nk > 1 else (8, LANES), F32)
    if not nx:
        return pl.pallas_call(
            body, name=name, grid=grid, in_specs=specs, out_specs=o_spec,
            out_shape=S((M, N), out_dtype), scratch_shapes=[acc_spec],
            compiler_params=_params(("parallel", "parallel", "arbitrary")))(*ins)
    return pl.pallas_call(
        body, name=name, grid=grid, in_specs=specs + [_ANY] * nx, out_specs=[o_spec] + [_ANY] * nx,
        out_shape=[S((M, N), out_dtype)] + _exchange_shapes(xbufs, kinds),
        scratch_shapes=[acc_spec] + _exchange_sems(nx),
        compiler_params=_params(("arbitrary", "arbitrary", "arbitrary")))(*ins, *xbufs)


def _ln_emb_fwd(x, g, b, exchange=None):
    T = x.shape[0]

    nt = T // TB
    xbufs, kinds = exchange if exchange is not None else ((), ())
    nx = len(xbufs)

    def body(*refs):
        x_ref, g_ref, b_ref = refs[:3]
        h_ref, hb_ref, hbt_ref = refs[3 + nx:6 + nx]
        i = pl.program_id(0)
        if nx:
            start, finish = _exchange_plan(refs[3:3 + nx], refs[6 + nx:6 + 2 * nx], kinds, *refs[6 + 2 * nx:])
            pl.when(i == 0)(start)

        def rows(rs):
            xh, _ = _ln_stats(x_ref[rs, :])
            h = xh * g_ref[...] + b_ref[...]
            h_ref[rs, :] = h
            hb_ref[rs, :] = h.astype(BF16)
        _row_loop(TB, rows)
        hbt_ref[...] = hb_ref[...].T
        if nx:
            pl.when(i == nt - 1)(finish)

    return pl.pallas_call(
        body, name="ln_emb_fwd", grid=(nt,),
        in_specs=[_row(TB, D), _full((1, D)), _full((1, D))] + [_ANY] * nx,
        out_specs=[_row(TB, D), _row(TB, D), _colt(D, TB)] + [_ANY] * nx,
        out_shape=[S((T, D), F32), S((T, D), BF16), S((D, T), BF16)] + _exchange_shapes(xbufs, kinds),
        scratch_shapes=_exchange_sems(nx) if nx else [],
        compiler_params=_params(("arbitrary",)))(x, g, b, *xbufs)


def _out_proj_post1(mix, w_out, h0, b_out, g, b, tm=TM_FUSED, tk=1024):
    T, K = mix.shape
    tm = min(tm, T)
    nk = K // tk
    assert T % tm == 0 and K % tk == 0 and nk >= 2

    def body(mix_ref, w_ref, h0_ref, bo_ref, g_ref, b_ref, out_ref, h_ref, hb_ref, hbt_ref, acc):
        k = pl.program_id(1)
        d = _dot(mix_ref[...], w_ref[...])

        @pl.when(k == 0)
        def _():
            acc[...] = d

        @pl.when((k > 0) & (k < nk - 1))
        def _():
            acc[...] += d

        @pl.when(k == nk - 1)
        def _():
            out_ref[...] = acc[...] + d

            def rows(rs):
                xh, _ = _ln_stats(ALPHA * h0_ref[rs, :] + out_ref[rs, :] + bo_ref[...])
                h = xh * g_ref[...] + b_ref[...]
                h_ref[rs, :] = h
                hb_ref[rs, :] = h.astype(BF16)
            _row_loop(tm, rows)
            hbt_ref[...] = hb_ref[...].T

    rowt = lambda n: pl.BlockSpec((tm, n), lambda i, k: (i, 0))
    const = pl.BlockSpec((1, D), lambda i, k: (0, 0))
    return pl.pallas_call(
        body, name="out_proj_post1", grid=(T // tm, nk),
        in_specs=[pl.BlockSpec((tm, tk), lambda i, k: (i, k)), pl.BlockSpec((tk, D), lambda i, k: (k, 0)), rowt(D),
                  const, const, const],
        out_specs=[rowt(D), rowt(D), rowt(D), pl.BlockSpec((D, tm), lambda i, k: (0, i))],
        out_shape=[S((T, D), F32), S((T, D), F32), S((T, D), BF16), S((D, T), BF16)],
        scratch_shapes=[pltpu.VMEM((tm, D), F32)],
        compiler_params=_params(("parallel", "arbitrary")))(mix, w_out, h0, b_out, g, b)


def _ple_post2(h1b, w_gate, pb, w_proj, h1, tgt, g, b, tm=TM_FUSED // 2):
    T = h1.shape[0]
    tm = min(tm, T)
    assert T % tm == 0

    def body(h1b_ref, wg_ref, pb_ref, wp_ref, h1_ref, tgt_ref, g_ref, b_ref,
             dh1_ref, dgp_ref, dple_ref, loss_ref, dg_ref, db_ref, gp_ref, ple_ref):
        @pl.when(pl.program_id(0) == 0)
        def _():
            loss_ref[...] = jnp.zeros_like(loss_ref)
            dg_ref[...] = jnp.zeros_like(dg_ref)
            db_ref[...] = jnp.zeros_like(db_ref)

        gp_ref[...] = _dot(h1b_ref[...], wg_ref[...])
        ple_ref[...] = _dot(pb_ref[...], wp_ref[...])

        def rows(rs):
            gate = _sigmoid(gp_ref[rs, :])
            ple = ple_ref[rs, :]
            xh, r = _ln_stats(ALPHA * h1_ref[rs, :] + gate * ple)
            err = xh * g_ref[...] + b_ref[...] - tgt_ref[rs, :]
            loss_ref[...] += 0.5 * jnp.sum(jnp.mean(err * err, axis=-1, keepdims=True), axis=0, keepdims=True)
            dv, dg, db = _ln_bwd(err * (1.0 / D), xh, r, g_ref[...])
            dg_ref[...] += dg
            db_ref[...] += db
            dh1_ref[rs, :] = ALPHA * dv
            dgp_ref[rs, :] = (dv * ple * gate * (1.0 - gate)).astype(BF16)
            dple_ref[rs, :] = (dv * gate).astype(BF16)
        _row_loop(tm, rows)

    return pl.pallas_call(
        body, name="ple_post2", grid=(T // tm,),
        in_specs=[_row(tm, D), _full((D, D)), _row(tm, D_PLE), _full((D_PLE, D)), _row(tm, D), _row(tm, D),
                  _full((1, D)), _full((1, D))],
        out_specs=[_row(tm, D)] * 3 + [_full((8, LANES)), _full((1, D)), _full((1, D))],
        out_shape=[S((T, D), F32), S((T, D), BF16), S((T, D), BF16), S((8, LANES), F32), S((1, D), F32), S((1, D), F32)],
        scratch_shapes=[pltpu.VMEM((tm, D), F32), pltpu.VMEM((tm, D), F32)],
        compiler_params=_params(("arbitrary",)))(h1b, w_gate, pb, w_proj, h1, tgt, g, b)


def _d_h1_post1_bwd(dgp, w_gate, dh1a, h0, out, b_out, g, tm=TM_FUSED // 2):
    T = h0.shape[0]
    tm = min(tm, T)
    assert T % tm == 0

    def body(dgp_ref, wg_ref, da_ref, h0_ref, out_ref, bo_ref, g_ref, dout_ref, dh0_ref, dg_ref, db_ref, dbo_ref, dh1):
        @pl.when(pl.program_id(0) == 0)
        def _():
            dg_ref[...] = jnp.zeros_like(dg_ref)
            db_ref[...] = jnp.zeros_like(db_ref)
            dbo_ref[...] = jnp.zeros_like(dbo_ref)

        dh1[...] = da_ref[...] + _dot(dgp_ref[...], wg_ref[...], _NT)

        def rows(rs):
            xh, r = _ln_stats(ALPHA * h0_ref[rs, :] + out_ref[rs, :] + bo_ref[...])
            dv, dg, db = _ln_bwd(dh1[rs, :], xh, r, g_ref[...])
            dg_ref[...] += dg
            db_ref[...] += db
            dbo_ref[...] += jnp.sum(dv, axis=0, keepdims=True)
            dout_ref[rs, :] = dv.astype(BF16)
            dh0_ref[rs, :] = ALPHA * dv
        _row_loop(tm, rows)

    return pl.pallas_call(
        body, name="d_h1_post1_bwd", grid=(T // tm,),
        in_specs=[_row(tm, D), _full((D, D))] + [_row(tm, D)] * 3 + [_full((1, D))] * 2,
        out_specs=[_row(tm, D)] * 2 + [_full((1, D))] * 3,
        out_shape=[S((T, D), BF16), S((T, D), F32)] + [S((1, D), F32)] * 3,
        scratch_shapes=[pltpu.VMEM((tm, D), F32)],
        compiler_params=_params(("arbitrary",)))(dgp, w_gate, dh1a, h0, out, b_out, g)


def _d_h0_ln_bwd(dproj, w_main, ddtr, w_dt, dh0a, x, g, exchange=None, tm=1024, tk=1408):
    T, K = dproj.shape
    tm = min(tm, T)
    assert T % tm == 0 and K % tk == 0
    ni, nk = T // tm, K // tk
    xbufs, kinds = exchange if exchange is not None else ((), ())
    nx = len(xbufs)

    def body(*refs):
        dp_ref, w_ref, dt_ref, wdt_ref, da_ref, x_ref, g_ref = refs[:7]
        dx_ref, dg_ref, db_ref = refs[7 + nx:10 + nx]
        acc = refs[10 + 2 * nx]
        i, k = pl.program_id(0), pl.program_id(1)
        if nx:
            start, finish = _exchange_plan(refs[7:7 + nx], refs[10 + nx:10 + 2 * nx], kinds, *refs[11 + 2 * nx:])
            pl.when((i == 0) & (k == 0))(start)

        @pl.when((i == 0) & (k == 0))
        def _():
            dg_ref[...] = jnp.zeros_like(dg_ref)
            db_ref[...] = jnp.zeros_like(db_ref)

        d = _dot(dp_ref[...], w_ref[...], _NT)

        @pl.when(k == 0)
        def _():
            acc[...] = da_ref[...] + _dot(dt_ref[...], wdt_ref[...], _NT) + d

        @pl.when(k > 0)
        def _():
            acc[...] += d

        @pl.when(k == nk - 1)
        def _():
            def rows(rs):
                xh, r = _ln_stats(x_ref[rs, :])
                dv, dg, db = _ln_bwd(acc[rs, :], xh, r, g_ref[...])
                dg_ref[...] += dg
                db_ref[...] += db
                dx_ref[rs, :] = dv
            _row_loop(tm, rows)

        if nx:
            pl.when((i == ni - 1) & (k == nk - 1))(finish)

    rowt = lambda n: pl.BlockSpec((tm, n), lambda i, k: (i, 0))
    const = lambda shape: pl.BlockSpec(shape, lambda i, k: (0, 0))
    return pl.pallas_call(
        body, name="d_h0_ln_bwd", grid=(ni, nk),
        in_specs=[pl.BlockSpec((tm, tk), lambda i, k: (i, k)), pl.BlockSpec((D, tk), lambda i, k: (0, k)),
                  rowt(LANES), const((D, LANES)), rowt(D), rowt(D), const((1, D))] + [_ANY] * nx,
        out_specs=[rowt(D), const((1, D)), const((1, D))] + [_ANY] * nx,
        out_shape=[S((T, D), F32), S((1, D), F32), S((1, D), F32)] + _exchange_shapes(xbufs, kinds),
        scratch_shapes=[pltpu.VMEM((tm, D), F32)] + (_exchange_sems(nx) if nx else []),
        compiler_params=_params(("arbitrary", "arbitrary")))(dproj, w_main, ddtr, w_dt, dh0a, x, g, *xbufs)


def _softplus(x):
    return jnp.maximum(x, 0.0) + jnp.log1p(jnp.exp(-jnp.abs(x)))


def _ssd_pre_fwd(proj, h0b, w_dt, wx, wb, bx, bb, dt_bias):
    T = proj.shape[0]
    H = HALO_SSM

    def body(xs_ref, xsp_ref, bc_ref, bcp_ref, h0b_ref, wdt_ref, wx_ref, wb_ref, bx_ref, bb_ref, dtb_ref,
             xso_ref, bco_ref, dto_ref, dtr_ref, extx, extb):
        first = pl.program_id(0) == 0

        def conv(t_ref, p_ref, w_ref, b_ref, o_ref, ext, n):
            def blk(cols):
                ext[0:H, cols] = jnp.where(first, 0.0, p_ref[:, cols])
                ext[H:, cols] = t_ref[:, cols]
                for r0 in range(0, TB, 64):
                    acc = jnp.broadcast_to(b_ref[:, cols], (64, LANES))
                    for k in range(K_SSM):
                        acc = acc + w_ref[k:k + 1, cols] * ext[pl.ds(r0 + H - (K_SSM - 1) + k, 64), cols]
                    o_ref[pl.ds(r0, 64), cols] = acc * _sigmoid(acc)
            _col_loop(n, blk)

        conv(xs_ref, xsp_ref, wx_ref, bx_ref, xso_ref, extx, D_SSM)
        conv(bc_ref, bcp_ref, wb_ref, bb_ref, bco_ref, extb, 512)
        dtr_ref[...] = _dot(h0b_ref[...], wdt_ref[...])
        dto_ref[...] = _softplus(dtr_ref[...] + dtb_ref[...])

    return pl.pallas_call(
        body, name="ssd_pre_fwd", grid=(T // TB,),
        in_specs=[_row(TB, 1024, C_XS), _prev(TB, H, 1024, C_XS), _row(TB, 512, C_BC), _prev(TB, H, 512, C_BC),
                  _row(TB, D), _full((D, LANES)), _full((K_SSM, 1024)), _full((K_SSM, 512)), _full((1, 1024)),
                  _full((1, 512)), _full((1, LANES))],
        out_specs=[_row(TB, 1024), _row(TB, 512), _row(TB, LANES), _row(TB, LANES)],
        out_shape=[S((T, 1024), F32), S((T, 512), F32), S((T, LANES), F32), S((T, LANES), F32)],
        scratch_shapes=[pltpu.VMEM((H + TB, 1024), F32), pltpu.VMEM((H + TB, 512), F32)],
        compiler_params=_params(("parallel",)))(proj, proj, proj, proj, h0b, w_dt, wx, wb, bx, bb, dt_bias)


def _ssd_conv_bwd(dproj, proj, d_c, w, b, n, col, name):
    TB = TB_SSD_CONV_BWD
    T = proj.shape[0]
    nt = T // TB
    H = HALO_SSM
    R = TB + H

    def body(dproj_ref, t_ref, p_ref, n_ref, d_ref, dn_ref, w_ref, b_ref, o_ref, dw_ref, dbias_ref, ext, dp):
        i = pl.program_id(0)
        first, last = i == 0, i == nt - 1

        @pl.when(first)
        def _():
            dw_ref[...] = jnp.zeros_like(dw_ref)
            dbias_ref[...] = jnp.zeros_like(dbias_ref)

        def blk(cols):
            ext[0:H, cols] = jnp.where(first, 0.0, p_ref[:, cols])
            ext[H:H + TB, cols] = t_ref[:, cols]
            ext[H + TB:, cols] = n_ref[:, cols]
            def taps_and_dsilu(r0, rows):
                taps = [ext[pl.ds(r0 + H - (K_SSM - 1) + k, rows), cols] for k in range(K_SSM)]
                pre = jnp.broadcast_to(b_ref[:, cols], (rows, LANES))
                for k in range(K_SSM):
                    pre = pre + w_ref[k:k + 1, cols] * taps[k]
                return taps, _dsilu(pre, _sigmoid(pre))

            for r0 in range(0, TB, 64):
                taps, ds = taps_and_dsilu(r0, 64)
                dpt = d_ref[pl.ds(r0, 64), cols] * ds
                dp[pl.ds(r0, 64), cols] = dpt
                dbias_ref[:, cols] += jnp.sum(dpt, axis=0, keepdims=True)
                for k in range(K_SSM):
                    dw_ref[k:k + 1, cols] += jnp.sum(dpt * taps[k], axis=0, keepdims=True)
            dp[TB:, cols] = jnp.where(last, 0.0, dn_ref[:, cols] * taps_and_dsilu(TB, H)[1])
            for r0 in range(0, TB, 64):
                acc = jnp.zeros((64, LANES), F32)
                for k in range(K_SSM):
                    acc = acc + w_ref[k:k + 1, cols] * dp[pl.ds(r0 + K_SSM - 1 - k, 64), cols]
                o_ref[pl.ds(r0, 64), cols] = acc.astype(BF16)
        _col_loop(n, blk)

    return pl.pallas_call(
        body, name=name, grid=(nt,),
        in_specs=[_ANY, _row(TB, n, col), _prev(TB, H, n, col), _next(TB, H, n, nt, col),
                  _row(TB, n), _next(TB, H, n, nt), _full((K_SSM, n)), _full((1, n))],
        out_specs=[_row(TB, n, col), _full((K_SSM, n)), _full((1, n))],
        out_shape=[S(dproj.shape, BF16), S((K_SSM, n), F32), S((1, n), F32)],
        input_output_aliases={0: 0},
        scratch_shapes=[pltpu.VMEM((H + TB + H, n), F32), pltpu.VMEM((R, n), F32)],
        compiler_params=_params(("arbitrary",)))(dproj, proj, proj, proj, d_c, d_c, w, b)


def _ssd_consts():
    ex = np.zeros((LANES, D_SSM), np.float32)
    for h in range(N_HEADS):
        ex[h, h * HEAD:(h + 1) * HEAD] = 1.0
    tri = np.tril(np.ones((CHUNK, CHUNK), np.float32))
    return jnp.asarray(ex), jnp.asarray(ex.T.copy()), jnp.asarray(tri), jnp.asarray(tri.T.copy())


def _ssd_common(xs, dt, alog_ref, ex_ref, tri_ref):
    lane = lax.broadcasted_iota(jnp.int32, (1, LANES), 1)
    a = jnp.where(lane < N_HEADS, -jnp.exp(alog_ref[...]), 0.0)
    A = _dot_sel_a(tri_ref[...], dt * a)
    ex = ex_ref[...]
    Aex = _dot_sel_b(A, ex)
    dtex = _dot_sel_b(dt, ex)
    expA = jnp.exp(Aex)
    dec = jnp.exp(Aex[CHUNK - 1:CHUNK, :] - Aex)
    cd = _dot_sel_a(ex, jnp.broadcast_to(jnp.exp(A.T[:, CHUNK - 1:CHUNK]), (LANES, LANES)), _TN)
    return a, A, dtex, expA, dec, cd


def _decay_mask():
    sub = lax.broadcasted_iota(jnp.int32, (CHUNK, CHUNK), 0)
    lane = lax.broadcasted_iota(jnp.int32, (CHUNK, CHUNK), 1)
    return sub, lane, sub >= lane


def _ssd_fwd(xs_c, bc_c, dt, proj, alog, dskip_row, norm_g):
    T = xs_c.shape[0]
    nc = T // CHUNK
    ex, _, tri, _ = _ssd_consts()

    def body(xs_ref, bc_ref, dt_ref, z_ref, alog_ref, dsk_ref, ng_ref, ex_ref, tri_ref,
             ys_ref, ypre_ref, hprev_ref, yst_ref, Hs, ybuf):
        @pl.when(pl.program_id(0) == 0)
        def _():
            Hs[...] = jnp.zeros_like(Hs)

        hprev_ref[0] = Hs[...]
        xs, dt = xs_ref[...], dt_ref[...]
        a, A, dtex, expA, dec, cd = _ssd_common(xs, dt, alog_ref, ex_ref, tri_ref)
        AT = A.T
        xdt = xs * dtex
        xdec = xdt * dec
        _, _, causal = _decay_mask()
        for g in range(2):
            gs = slice(g * 512, (g + 1) * 512)
            B = bc_ref[:, g * N_STATE:(g + 1) * N_STATE]
            C = bc_ref[:, 256 + g * N_STATE:256 + (g + 1) * N_STATE]
            cb = _dot(C, B, _NT)
            Hg = Hs[gs, :]
            yoff = _dot(C, Hg, _NT) * expA[:, gs]
            for j in range(8):
                h = g * 8 + j
                hs = slice(h * HEAD, (h + 1) * HEAD)
                L = jnp.exp(jnp.where(causal, A[:, h:h + 1] - AT[h:h + 1, :], -1e30))
                ybuf[:, hs] = _dot(cb * L, xdt[:, hs]) + yoff[:, j * HEAD:(j + 1) * HEAD]
            Hs[gs, :] = cd[gs, :] * Hg + _dot(xdec[:, gs], B, _TN)
        ypre = ybuf[...] + dsk_ref[...] * xs
        ypre_ref[...] = ypre
        z = z_ref[...]
        yz = ypre * (z * _sigmoid(z))
        for g in range(2):
            gs = slice(g * 512, (g + 1) * 512)
            v = yz[:, gs]
            r = lax.rsqrt(jnp.mean(v * v, axis=-1, keepdims=True) + RMS_EPS)
            ys_ref[:, gs] = (v * r * ng_ref[:, gs]).astype(BF16)
        yst_ref[...] = ys_ref[...].T

    return pl.pallas_call(
        body, name="ssd_fwd", grid=(nc,),
        in_specs=[_row(CHUNK, 1024), _row(CHUNK, 512), _row(CHUNK, LANES), _row(CHUNK, 1024, C_Z),
                  _full((1, LANES)), _full((1, 1024)), _full((1, 1024)), _full((LANES, 1024)), _full((CHUNK, CHUNK))],
        out_specs=[_row(CHUNK, 1024), _row(CHUNK, 1024), pl.BlockSpec((1, 1024, N_STATE), lambda c: (c, 0, 0)),
                   _colt(1024, CHUNK)],
        out_shape=[S((T, 2048), BF16), S((T, 1024), F32), S((nc, 1024, N_STATE), F32), S((2048, T), BF16)],
        scratch_shapes=[pltpu.VMEM((1024, N_STATE), F32), pltpu.VMEM((CHUNK, 1024), F32)],
        compiler_params=_params(("arbitrary",)))(xs_c, bc_c, dt, proj, alog, dskip_row, norm_g, ex, tri)


def _ssd_bwd(dproj, xs_c, bc_c, dt, dt_raw, dt_bias, proj, ypre, hprev, dmix, alog, dskip_row, norm_g, exchange=None):
    T = xs_c.shape[0]
    nc = T // CHUNK
    ex, ext, tri, triu = _ssd_consts()
    rev = lambda n, col=0: pl.BlockSpec((CHUNK, n), lambda c: (nc - 1 - c, col))
    xbufs, kinds = exchange if exchange is not None else ((), ())
    nx = len(xbufs)
    N_IN, N_OUT = 17, 8

    def body(*refs):
        (dproj_ref, xs_ref, bc_ref, dt_ref, dtr_ref, dtb_ref, z_ref, ypre_ref, hprev_ref, dys_ref, alog_ref, dsk_ref,
         ng_ref, ex_ref, ext_ref, tri_ref, triu_ref) = refs[:N_IN]
        (dxs_ref, dbc_ref, ddt_ref, dz_ref, dng_ref, ddsk_ref, dalog_ref,
         ddtb_ref) = refs[N_IN + nx:N_IN + nx + N_OUT]
        dHs, dxbuf, dskacc = refs[N_IN + N_OUT + 2 * nx:N_IN + N_OUT + 2 * nx + 3]
        c = pl.program_id(0)
        if nx:
            start, finish = _exchange_plan(refs[N_IN:N_IN + nx], refs[N_IN + nx + N_OUT:N_IN + N_OUT + 2 * nx], kinds,
                                           *refs[N_IN + N_OUT + 2 * nx + 3:])
            pl.when(c == 0)(start)

        @pl.when(c == 0)
        def _():
            dHs[...] = jnp.zeros_like(dHs)
            dng_ref[...] = jnp.zeros_like(dng_ref)
            dalog_ref[...] = jnp.zeros_like(dalog_ref)
            ddtb_ref[...] = jnp.zeros_like(ddtb_ref)
            dskacc[...] = jnp.zeros_like(dskacc)

        xs, dt, z, ypre, dys = xs_ref[...], dt_ref[...], z_ref[...], ypre_ref[...], dys_ref[...]
        sg = _sigmoid(z)
        sz = z * sg
        yz = ypre * sz
        dyz_parts = []
        for g in range(2):
            gs = slice(g * 512, (g + 1) * 512)
            v = yz[:, gs]
            r = lax.rsqrt(jnp.mean(v * v, axis=-1, keepdims=True) + RMS_EPS)
            vn = v * r
            dng_ref[:, gs] += jnp.sum(dys[:, gs] * vn, axis=0, keepdims=True)
            dvn = dys[:, gs] * ng_ref[:, gs]
            dyz_parts.append(r * (dvn - vn * jnp.mean(dvn * vn, axis=-1, keepdims=True)))
        dyz = jnp.concatenate(dyz_parts, axis=1)
        dy = dyz * sz
        dz_ref[...] = (dyz * ypre * _dsilu(z, sg)).astype(BF16)
        dskacc[...] += jnp.sum(dy * xs, axis=0, keepdims=True)

        a, A, dtex, expA, dec, cd = _ssd_common(xs, dt, alog_ref, ex_ref, tri_ref)
        AT = A.T
        xdt = xs * dtex
        xdec = xdt * dec
        dye = dy * expA
        H = hprev_ref[0]
        dHn = dHs[...]
        sub, lane, causal = _decay_mask()
        dAc = jnp.zeros((CHUNK, LANES), F32)
        Rm = jnp.zeros((CHUNK, LANES), F32)
        yoff_parts, q_parts = [], []
        for g in range(2):
            gs = slice(g * 512, (g + 1) * 512)
            B = bc_ref[:, g * N_STATE:(g + 1) * N_STATE]
            C = bc_ref[:, 256 + g * N_STATE:256 + (g + 1) * N_STATE]
            cb = _dot(C, B, _NT)
            Hg, dHg = H[gs, :], dHn[gs, :]
            Q = _dot(B, dHg, _NT)
            yoff_parts.append(_dot(C, Hg, _NT) * expA[:, gs])
            q_parts.append(Q)
            dcb = jnp.zeros((CHUNK, CHUNK), F32)
            for j in range(8):
                h = g * 8 + j
                hs = slice(h * HEAD, (h + 1) * HEAD)
                L = jnp.exp(jnp.where(causal, A[:, h:h + 1] - AT[h:h + 1, :], -1e30))
                M = cb * L
                G = _dot(dy[:, hs], xdt[:, hs], _NT)
                dxbuf[:, hs] = _dot(M, dy[:, hs], _TN)
                dcb = dcb + G * L
                E = G * M
                dAc = jnp.where(lane == h, jnp.sum(E, axis=1, keepdims=True), dAc)
                Rm = jnp.where(sub == h, jnp.sum(E, axis=0, keepdims=True), Rm)
            dbc_ref[:, g * N_STATE:(g + 1) * N_STATE] = _dot(dcb, C, _TN) + _dot(xdec[:, gs], dHg)
            dbc_ref[:, 256 + g * N_STATE:256 + (g + 1) * N_STATE] = _dot(dcb, B) + _dot(dye[:, gs], Hg)
            dHs[gs, :] = cd[gs, :] * dHg + _dot(dye[:, gs], C, _TN)
        yoff = jnp.concatenate(yoff_parts, axis=1)
        Qd = jnp.concatenate(q_parts, axis=1) * dec
        dxdt = dxbuf[...] + Qd
        extm = ext_ref[...]
        red_s = _dot_sel_b(xdt * Qd, extm)
        dA = dAc - Rm.T + _dot_sel_b(dy * yoff, extm) - red_s
        hd = jnp.sum(_dot_sel_b(H * dHn, extm, _TN), axis=0, keepdims=True)
        last_add = jnp.sum(red_s, axis=0, keepdims=True) + jnp.exp(A[CHUNK - 1:CHUNK, :]) * hd
        dA = dA + jnp.where(sub == CHUNK - 1, last_add, 0.0)
        dadt = _dot_sel_a(triu_ref[...], dA)
        ddtr = (dadt * a + _dot_sel_b(dxdt * xs, extm)) * _sigmoid(dtr_ref[...] + dtb_ref[...])
        ddt_ref[...] = ddtr.astype(BF16)
        ddtb_ref[...] += jnp.sum(ddtr, axis=0, keepdims=True)
        dalog_ref[...] += jnp.sum(dadt * dt, axis=0, keepdims=True) * a
        dxs_ref[...] = dxdt * dtex + dsk_ref[...] * dy

        @pl.when(c == nc - 1)
        def _():
            ddsk_ref[...] = _dot_sel_b(jnp.broadcast_to(dskacc[...], (8, 1024)), extm)[0:1, :]

        if nx:
            pl.when(c == nc - 1)(finish)

    return pl.pallas_call(
        body, name="ssd_bwd", grid=(nc,),
        in_specs=[_ANY, rev(1024), rev(512), rev(LANES), rev(LANES), _full((1, LANES)), rev(1024, C_Z), rev(1024),
                  pl.BlockSpec((1, 1024, N_STATE), lambda c: (nc - 1 - c, 0, 0)), rev(1024, 0),
                  _full((1, LANES)), _full((1, 1024)), _full((1, 1024)),
                  _full((LANES, 1024)), _full((1024, LANES)), _full((CHUNK, CHUNK)), _full((CHUNK, CHUNK))] + [_ANY] * nx,
        out_specs=[rev(1024), rev(512), rev(LANES), rev(1024, C_Z), _full((1, 1024)), _full((1, LANES)),
                   _full((1, LANES)), _full((1, LANES))] + [_ANY] * nx,
        out_shape=[S((T, 1024), F32), S((T, 512), F32), S((T, LANES), BF16), S(dproj.shape, BF16),
                   S((1, 1024), F32), S((1, LANES), F32), S((1, LANES), F32), S((1, LANES), F32)]
        + _exchange_shapes(xbufs, kinds),
        input_output_aliases={0: 3},
        scratch_shapes=[pltpu.VMEM((1024, N_STATE), F32), pltpu.VMEM((CHUNK, 1024), F32), pltpu.VMEM((1, 1024), F32)]
        + (_exchange_sems(nx) if nx else []),
        compiler_params=_params(("arbitrary",)))(
            dproj, xs_c, bc_c, dt, dt_raw, dt_bias, proj, ypre, hprev, dmix, alog, dskip_row, norm_g, ex, ext, tri, triu,
            *xbufs)


def _shifted_copies(ext, ext8):
    n = ext8.shape[1]
    for r in range(8):
        ext8[r] = ext[pl.ds(r, n), :]


def _shifted(ext8, off, rows):
    return ext8[off % 8, pl.ds(off - off % 8, rows), :]


def _conf_fwd(mix, mixt, proj, w, cb, lg, lb, ba, bb):
    T = proj.shape[0]
    H = HALO_CONF

    def body(mix_ref, mixt_ref, ga_ref, gap_ref, gb_ref, gbp_ref, cg_ref, w_ref, cb_ref, lg_ref, lb_ref, ba_ref,
             bb_ref, u1_ref, yc_ref, yct_ref, ext, ext8):
        first = pl.program_id(0) == 0
        ext[H + TB:, :] = jnp.zeros((8, LANES), F32)

        def blk(cols):
            up = (gap_ref[:, cols] + ba_ref[:, cols]) * _sigmoid(gbp_ref[:, cols] + bb_ref[:, cols])
            ext[0:H, :] = jnp.where(first, 0.0, up)
            ext[H:H + TB, :] = (ga_ref[:, cols] + ba_ref[:, cols]) * _sigmoid(gb_ref[:, cols] + bb_ref[:, cols])
            _shifted_copies(ext, ext8)
            for r0 in range(0, TB, 64):
                acc = jnp.broadcast_to(cb_ref[:, cols], (64, LANES))
                for k in range(K_CONF):
                    acc = acc + w_ref[k:k + 1, cols] * _shifted(ext8, r0 + H - (K_CONF - 1) + k, 64)
                u1_ref[pl.ds(r0, 64), cols] = acc
        _col_loop(D_CONF, blk)

        def rows(rs):
            xh, _ = _ln_stats(u1_ref[rs, :])
            u2 = xh * lg_ref[...] + lb_ref[...]
            cg = cg_ref[rs, :]
            yc_ref[rs, :] = (u2 * _sigmoid(u2) * cg * _sigmoid(cg)).astype(BF16)
        _row_loop(TB, rows)
        yct_ref[...] = yc_ref[...].T

    return pl.pallas_call(
        body, name="conf_fwd", grid=(T // TB,),
        in_specs=[_ANY, _ANY, _row(TB, 1024, C_GLUA), _prev(TB, H, 1024, C_GLUA), _row(TB, 1024, C_GLUB),
                  _prev(TB, H, 1024, C_GLUB), _row(TB, 1024, C_CG), _full((K_CONF, 1024))] + [_full((1, 1024))] * 5,
        out_specs=[_row(TB, 1024), _row(TB, 1024, 1), _colt(1024, TB, 1)],
        out_shape=[S((T, 1024), F32), S((T, 2048), BF16), S((2048, T), BF16)],
        input_output_aliases={0: 1, 1: 2},
        scratch_shapes=[pltpu.VMEM((H + TB + 8, LANES), F32), pltpu.VMEM((8, H + TB, LANES), F32)],
        compiler_params=_params(("parallel",)))(mix, mixt, proj, proj, proj, proj, proj, w, cb, lg, lb, ba, bb)


def _d_mix_conf_bwd1(dout, w_out, u1, proj, lg, lb):
    T = u1.shape[0]

    def body(dout_ref, w_ref, u1_ref, cg_ref, lg_ref, lb_ref, dys_ref, du1_ref, dcg_ref, dg_ref, db_ref, dy_ref):
        @pl.when(pl.program_id(0) == 0)
        def _():
            dg_ref[...] = jnp.zeros_like(dg_ref)
            db_ref[...] = jnp.zeros_like(db_ref)

        dys_ref[...] = _dot(dout_ref[...], w_ref[0:D_SSM, :], _NT)
        dy_ref[...] = _dot(dout_ref[...], w_ref[D_SSM:, :], _NT)

        def rows(rs):
            xh, r = _ln_stats(u1_ref[rs, :])
            u2 = xh * lg_ref[...] + lb_ref[...]
            s2 = _sigmoid(u2)
            cg = cg_ref[rs, :]
            sc = _sigmoid(cg)
            dy = dy_ref[rs, :]
            dcg_ref[rs, :] = (dy * u2 * s2 * _dsilu(cg, sc)).astype(BF16)
            dv, dg, db = _ln_bwd(dy * cg * sc * _dsilu(u2, s2), xh, r, lg_ref[...])
            dg_ref[...] += dg
            db_ref[...] += db
            du1_ref[rs, :] = dv
        _row_loop(TB, rows)

    return pl.pallas_call(
        body, name="d_mix_conf_bwd1", grid=(T // TB,),
        in_specs=[_row(TB, D), _full((2 * D, D)), _row(TB, 1024), _row(TB, 1024, C_CG), _full((1, 1024)),
                  _full((1, 1024))],
        out_specs=[_row(TB, 1024), _row(TB, 1024), _row(TB, 1024, C_CG), _full((1, 1024)), _full((1, 1024))],
        out_shape=[S((T, 1024), F32), S((T, 1024), F32), S((T, N_MAIN), BF16), S((1, 1024), F32), S((1, 1024), F32)],
        scratch_shapes=[pltpu.VMEM((TB, D_CONF), F32)],
        compiler_params=_params(("arbitrary",)))(dout, w_out, u1, proj, lg, lb)


def _conf_bwd2(dproj, proj, du1, w, ba, bb):
    T = du1.shape[0]
    nt = T // TB
    H = HALO_CONF

    def body(dproj_ref, ga_ref, gap_ref, gb_ref, gbp_ref, du_ref, dun_ref, w_ref, ba_ref, bb_ref,
             dg_ref, dw_ref, dcb_ref, dba_ref, dbb_ref, ext, dext, ext8, dext8, dwacc):
        i = pl.program_id(0)
        first, last = i == 0, i == nt - 1

        @pl.when(first)
        def _():
            for r in (dcb_ref, dba_ref, dbb_ref, dwacc):
                r[...] = jnp.zeros_like(r)

        ext[H + TB:, :] = jnp.zeros((8, LANES), F32)
        dext[H + TB:, :] = jnp.zeros((8, LANES), F32)

        def blk(cols):
            cols_b = pl.ds(pl.multiple_of(cols.start + D_CONF, LANES), LANES)
            up = (gap_ref[:, cols] + ba_ref[:, cols]) * _sigmoid(gbp_ref[:, cols] + bb_ref[:, cols])
            ext[0:H, :] = jnp.where(first, 0.0, up)
            a = ga_ref[:, cols] + ba_ref[:, cols]
            sb = _sigmoid(gb_ref[:, cols] + bb_ref[:, cols])
            ext[H:H + TB, :] = a * sb
            du = du_ref[:, cols]
            dext[0:TB, :] = du
            dext[TB:TB + H, :] = jnp.where(last, 0.0, dun_ref[:, cols])
            _shifted_copies(ext, ext8)
            _shifted_copies(dext, dext8)
            dcb_ref[:, cols] += jnp.sum(du, axis=0, keepdims=True)
            for r0 in range(0, TB, 64):
                dur = du_ref[pl.ds(r0, 64), cols]
                acc = jnp.zeros((64, LANES), F32)
                for k in range(K_CONF):
                    prod = dur * _shifted(ext8, r0 + H - (K_CONF - 1) + k, 64)
                    dwacc[k * 8:(k + 1) * 8, cols] += prod.reshape(8, 8, LANES).sum(axis=0)
                    acc = acc + w_ref[k:k + 1, cols] * _shifted(dext8, r0 + K_CONF - 1 - k, 64)
                ar, sr = a[r0:r0 + 64], sb[r0:r0 + 64]
                da = acc * sr
                dbv = acc * ar * sr * (1.0 - sr)
                dg_ref[pl.ds(r0, 64), cols] = da.astype(BF16)
                dg_ref[pl.ds(r0, 64), cols_b] = dbv.astype(BF16)
                dba_ref[:, cols] += jnp.sum(da, axis=0, keepdims=True)
                dbb_ref[:, cols] += jnp.sum(dbv, axis=0, keepdims=True)
        _col_loop(D_CONF, blk)

        @pl.when(last)
        def _():
            dw_ref[...] = jnp.sum(dwacc[...].reshape(K_CONF, 8, D_CONF), axis=1)

    return pl.pallas_call(
        body, name="conf_bwd2", grid=(nt,),
        in_specs=[_ANY, _row(TB, 1024, C_GLUA), _prev(TB, H, 1024, C_GLUA), _row(TB, 1024, C_GLUB),
                  _prev(TB, H, 1024, C_GLUB), _row(TB, 1024), _next(TB, H, 1024, nt), _full((K_CONF, 1024)),
                  _full((1, 1024)), _full((1, 1024))],
        out_specs=[_row(TB, 2048), _full((K_CONF, 1024)), _full((1, 1024)), _full((1, 1024)), _full((1, 1024))],
        out_shape=[S(dproj.shape, BF16), S((K_CONF, 1024), F32)] + [S((1, 1024), F32)] * 3,
        input_output_aliases={0: 0},
        scratch_shapes=[pltpu.VMEM((H + TB + 8, LANES), F32), pltpu.VMEM((TB + H + 8, LANES), F32),
                        pltpu.VMEM((8, H + TB, LANES), F32), pltpu.VMEM((8, TB + H, LANES), F32),
                        pltpu.VMEM((K_CONF * 8, D_CONF), F32)],
        compiler_params=_params(("arbitrary",)))(dproj, proj, proj, proj, proj, du1, du1, w, ba, bb)


def _mesh_pos():
    x, y, c = lax.axis_index("x"), lax.axis_index("y"), lax.axis_index("c")
    return x, y, c, 4 * x + 2 * y + c


def _peer(x, y, c, k):
    return (x ^ ((k >> 2) & 1), y ^ ((k >> 1) & 1), c ^ (k & 1))


def _exchange_copies(ins, outs, kinds, send, recv, loc):
    nb = len(ins)
    x, y, c, me = _mesh_pos()
    src = lambda b, d: ins[b].at[d] if kinds[b] == "blocks" else ins[b]
    copies = [pltpu.make_async_copy(src(b, me), outs[b].at[me], loc.at[b]) for b in range(nb)]
    for k in range(1, N_DEV):
        px, py, pc = _peer(x, y, c, k)
        for b in range(nb):
            s = (k - 1) * nb + b
            copies.append(pltpu.make_async_remote_copy(
                src_ref=src(b, 4 * px + 2 * py + pc), dst_ref=outs[b].at[me], send_sem=send.at[s], recv_sem=recv.at[s],
                device_id=(px, py, pc), device_id_type=pl.DeviceIdType.MESH))
    return copies


def _exchange_shapes(bufs, kinds):
    return [S(b.shape if kd == "blocks" else (N_DEV,) + b.shape, b.dtype) for b, kd in zip(bufs, kinds)]


def _exchange_sems(nb):
    n = (N_DEV - 1) * nb
    return [pltpu.SemaphoreType.DMA((n,)), pltpu.SemaphoreType.DMA((n,)), pltpu.SemaphoreType.DMA((nb,))]


def _two_level_gather(ins, outs, send, recv, loc):
    nb = len(ins)
    x, y, c, me = _mesh_pos()
    here, sibling = (x, y, c), (x, y, 1 - c)
    chips = [(1 - x, y), (x, 1 - y), (1 - x, 1 - y)]

    def copy(slot, b, block, to, src=None):
        d = 4 * block[0] + 2 * block[1] + block[2]
        return pltpu.make_async_remote_copy(
            src_ref=outs[b].at[d] if src is None else src, dst_ref=outs[b].at[d],
            send_sem=send.at[slot * nb + b], recv_sem=recv.at[slot * nb + b],
            device_id=to, device_id_type=pl.DeviceIdType.MESH)

    mine = [pltpu.make_async_copy(ins[b], outs[b].at[me], loc.at[b]) for b in range(nb)]
    first = [copy(0, b, here, sibling, src=ins[b]) for b in range(nb)]
    first += [copy(1 + j, b, here, (*chip, c), src=ins[b]) for j, chip in enumerate(chips) for b in range(nb)]

    def start():
        for cp in mine + first:
            cp.start()

    def finish():
        passed = []
        for j, chip in enumerate(chips):
            for b in range(nb):
                copy(1 + j, b, (*chip, c), here).wait_recv()
            onward = [copy(4 + j, b, (*chip, c), sibling) for b in range(nb)]
            for cp in onward:
                cp.start()
            passed += onward
        for b in range(nb):
            copy(0, b, sibling, here).wait_recv()
        for j, chip in enumerate(chips):
            for b in range(nb):
                copy(4 + j, b, (*chip, 1 - c), here).wait_recv()
        for cp in first + passed:
            cp.wait_send()
        for cp in mine:
            cp.wait()

    return start, finish


def _exchange_plan(ins, outs, kinds, send, recv, loc):
    if all(kd == "gather" for kd in kinds):
        return _two_level_gather(ins, outs, send, recv, loc)
    copies = _exchange_copies(ins, outs, kinds, send, recv, loc)

    def start():
        for cp in copies:
            cp.start()

    def finish():
        for cp in copies:
            cp.wait()

    return start, finish


def _exchange(bufs, kinds, name):
    nb = len(bufs)

    def body(*refs):
        start, finish = _exchange_plan(refs[:nb], refs[nb:2 * nb], kinds, *refs[2 * nb:])
        start()
        finish()

    return pl.pallas_call(
        body, name=name, in_specs=[_ANY] * nb, out_specs=[_ANY] * nb,
        out_shape=_exchange_shapes(bufs, kinds), scratch_shapes=_exchange_sems(nb))(*bufs)


def _sum_parts(p_ref):
    acc = p_ref[0].astype(F32)
    for d in range(1, N_DEV):
        acc = acc + p_ref[d].astype(F32)
    return acc


def _adamw_math(g, w, m, v):
    m = ADAM_B1 * m + (1.0 - ADAM_B1) * g
    v = ADAM_B2 * v + (1.0 - ADAM_B2) * (g * g)
    m_hat = m / (1.0 - ADAM_B1 ** ADAM_STEP)
    v_hat = v / (1.0 - ADAM_B2 ** ADAM_STEP)
    return -ADAM_LR * (m_hat / (jnp.sqrt(v_hat) + ADAM_EPS) + ADAM_WD * w), m, v


HEAD_ROWS = 256


def _sum8_adamw(parts, w, m, v, name, head=None):
    _, R, C = w.shape
    tb = HEAD_ROWS if R % HEAD_ROWS == 0 else R
    nb = R // tb
    assert head is None or (tb == HEAD_ROWS and head.shape[1] == HEAD_ROWS and parts.shape[1] == R - HEAD_ROWS)
    skip = 0 if head is None else 1

    def body(*refs):
        p_ref, w_ref, m_ref, v_ref, g_ref, d_ref, mo_ref, vo_ref = refs[skip:]
        g = _sum_parts(p_ref)
        if head is not None:
            g = jnp.where(pl.program_id(0) == nb - 1, _sum_parts(refs[0]), g)
        g_ref[0] = g
        d_ref[0], mo_ref[0], vo_ref[0] = _adamw_math(g, w_ref[0], m_ref[0], v_ref[0])

    first = [] if head is None else [pl.BlockSpec((N_DEV, tb, C), lambda i: (0, 0, 0))]
    own = pl.BlockSpec((1, tb, C), lambda i: (0, i, 0))
    last_part = parts.shape[1] // tb - 1
    return pl.pallas_call(
        body, name=name, grid=(nb,),
        in_specs=first + [pl.BlockSpec((N_DEV, tb, C), lambda i: (0, jnp.minimum(i, last_part), 0))] + [own] * 3,
        out_specs=[own] * 4, out_shape=[S((1, R, C), F32)] * 4,
        compiler_params=_params(("parallel",)))(*([] if head is None else [head]), parts, w, m, v)


SMALL_LAYOUT = (
    ("ln_emb_g", 0, 1024), ("ln_emb_b", 0, 1024), ("ssm_conv_b", 0, 1024), ("ssm_conv_b", 1024, 512),
    ("dt_bias", 0, N_HEADS), ("a_log", 0, N_HEADS), ("d_skip", 0, N_HEADS), ("ssm_norm_g", 0, 1024),
    ("b_glu", 0, 1024), ("b_glu", 1024, 1024), ("conf_conv_b", 0, 1024), ("conf_ln_g", 0, 1024),
    ("conf_ln_b", 0, 1024), ("b_out", 0, 1024), ("ln1_g", 0, 1024), ("ln1_b", 0, 1024), ("ln2_g", 0, 1024),
    ("ln2_b", 0, 1024))
SMALL_ROWS = 24
SMALL = tuple(dict.fromkeys(n for n, _, _ in SMALL_LAYOUT))


LOSS_ROW = len(SMALL_LAYOUT)


def _pack_small(rows, loss):
    def body(*refs):
        o_ref = refs[-1]
        o_ref[...] = jnp.zeros_like(o_ref)
        for r, ref in enumerate(refs[:-2]):
            o_ref[r:r + 1, 0:ref.shape[1]] = ref[...]
        o_ref[LOSS_ROW:LOSS_ROW + 1, 0:LANES] = refs[-2][0:1, :]

    return pl.pallas_call(body, name="pack_small", out_shape=S((SMALL_ROWS, 1024), F32))(*rows, loss)


def _small_update(parts, w, m, v):
    def body(*refs):
        p_ref = refs[0]
        ins = {n: refs[1 + 3 * i:4 + 3 * i] for i, n in enumerate(SMALL)}
        o0 = 1 + 3 * len(SMALL)
        outs = {n: refs[o0 + 4 * i:o0 + 4 * i + 4] for i, n in enumerate(SMALL)}
        gsum = refs[-1]
        gsum[...] = _sum_parts(p_ref)
        refs[-2][...] = gsum[LOSS_ROW:LOSS_ROW + 1, 0:LANES]
        for r, (n, off, wd) in enumerate(SMALL_LAYOUT):
            cs = slice(off, off + wd)
            g = gsum[r:r + 1, 0:wd]
            w_ref, m_ref, v_ref = ins[n]
            g_ref, d_ref, mo_ref, vo_ref = outs[n]
            g_ref[:, cs] = g
            d_ref[:, cs], mo_ref[:, cs], vo_ref[:, cs] = _adamw_math(g, w_ref[:, cs], m_ref[:, cs], v_ref[:, cs])

    args = [parts] + [a for n in SMALL for a in (w[n], m[n], v[n])]
    res = pl.pallas_call(
        body, name="small_update",
        out_shape=[S(w[n].shape, F32) for n in SMALL for _ in range(4)] + [S((1, LANES), F32)],
        scratch_shapes=[pltpu.VMEM((SMALL_ROWS, 1024), F32)])(*args)
    return tuple({n: res[4 * i + j] for i, n in enumerate(SMALL)} for j in range(4)) + (res[-1],)


EARLY = ("w_in", "ssm_conv_w", "conf_conv_w")
LATE = ("w_out", "w_ple_gate", "w_ple_proj")


def _local_step(x, p, tgt, W, shards=None):
    r1 = lambda v: v.reshape(1, -1).astype(F32)
    pad_l = lambda v: jnp.pad(r1(v), ((0, 0), (0, LANES - v.size)))
    late = None if shards is None else [shards[n] for n in LATE]
    if shards is None:
        h0, h0b, h0bt = _ln_emb_fwd(x, r1(W["ln_emb_g"]), r1(W["ln_emb_b"]))
    else:
        h0, h0b, h0bt, *gathered = _ln_emb_fwd(x, r1(W["ln_emb_g"]), r1(W["ln_emb_b"]),
                                               exchange=([shards[n] for n in EARLY], ("gather",) * len(EARLY)))
        W = dict(W, **{n: a if n == "w_in" else _unstack_shards(a, BY_COLS[n]) for n, a in zip(EARLY, gathered)})
    w_main, w_dt = _w_in_to_main(W["w_in"])
    scw, scb = W["ssm_conv_w"], r1(W["ssm_conv_b"])
    wx, wb, bx, bb = scw[:, :1024], scw[:, 1024:], scb[:, :1024], scb[:, 1024:]
    dt_bias, alog = pad_l(W["dt_bias"]), pad_l(W["a_log"])
    dskip_row = jnp.repeat(W["d_skip"].reshape(-1), HEAD).reshape(1, -1)
    norm_g = r1(W["ssm_norm_g"])
    bglu = r1(W["b_glu"])
    ba, bbg = bglu[:, :1024], bglu[:, 1024:]
    ccw, ccb, clg, clb = W["conf_conv_w"], r1(W["conf_conv_b"]), r1(W["conf_ln_g"]), r1(W["conf_ln_b"])

    if late is None:
        proj = _mm(h0b, w_main, "nn", "in_proj", tm=TM_IN_PROJ)
    else:
        proj, *gathered = _mm(h0b, w_main, "nn", "in_proj", tm=TM_IN_PROJ, exchange=(late, ("gather",) * len(LATE)))
        W = dict(W, **{n: _unstack_shards(a, BY_COLS[n]) for n, a in zip(LATE, gathered)})
    xs_c, bc_c, dt, dt_raw = _ssd_pre_fwd(proj, h0b, w_dt, wx, wb, bx, bb, dt_bias)
    mix, ypre, hprev, mixt = _ssd_fwd(xs_c, bc_c, dt, proj, alog, dskip_row, norm_g)
    u1, mix, mixt = _conf_fwd(mix, mixt, proj, ccw, ccb, clg, clb, ba, bbg)
    out, h1, h1b, h1bt = _out_proj_post1(mix, W["w_out"], h0, r1(W["b_out"]), r1(W["ln1_g"]), r1(W["ln1_b"]))
    pb = p.astype(BF16)
    dh1a, dgp, dple, loss, dln2g, dln2b = _ple_post2(h1b, W["w_ple_gate"], pb, W["w_ple_proj"], h1, tgt,
                                                      r1(W["ln2_g"]), r1(W["ln2_b"]))

    g = {}
    g["w_ple_proj"] = _mm(pb.T, dple, "nn", "d_ple_proj", out_dtype=BF16)
    g["w_ple_gate"] = _mm(h1bt, dgp, "nn", "d_ple_gate", out_dtype=BF16)
    dout, dh0a, dln1g, dln1b, dbout = _d_h1_post1_bwd(dgp, W["w_ple_gate"], dh1a, h0, out, r1(W["b_out"]),
                                                      r1(W["ln1_g"]))
    g["w_out"] = _mm(mixt, dout, "nn", "d_w_out", out_dtype=BF16)
    dmix, du1, dproj, dclg, dclb = _d_mix_conf_bwd1(dout, W["w_out"], u1, proj, clg, clb)
    dproj, g["conf_conv_w"], dccb, dba, dbb = _conf_bwd2(dproj, proj, du1, ccw, ba, bbg)
    stack = lambda names: [_stack_shards(g[n], BY_COLS[n]) for n in names]
    dxs_c, dbc_c, ddtr, dproj, dng, ddsk, dalog, ddtb, *recv_a = _ssd_bwd(
        dproj, xs_c, bc_c, dt, dt_raw, dt_bias, proj, ypre, hprev, dmix, alog, dskip_row, norm_g,
        exchange=None if late is None else (stack(LATE), ("blocks",) * len(LATE)))
    dproj, dwx, dbx = _ssd_conv_bwd(dproj, proj, dxs_c, wx, bx, 1024, C_XS, "ssd_conv_bwd_x")
    dproj, dwb, dbb2 = _ssd_conv_bwd(dproj, proj, dbc_c, wb, bb, 512, C_BC, "ssd_conv_bwd_bc")
    g["ssm_conv_w"] = jnp.concatenate([dwx, dwb], axis=1)
    dw_dt = _mm(h0bt, ddtr, "nn", "d_w_dt", out_dtype=BF16)
    last_args = (dproj, w_main, ddtr, w_dt, dh0a, x, r1(W["ln_emb_g"]))
    if late is None:
        g["w_in"] = _w_in_blocks(_mm(h0bt, dproj, "nn", "d_w_in", out_dtype=BF16), dw_dt)
        grad_x, dlng, dlnb = _d_h0_ln_bwd(*last_args)
    else:
        r0 = D - HEAD_ROWS
        head = _w_in_blocks(_mm(h0bt, dproj, "nn", "d_w_in_head", out_dtype=BF16, tk=x.shape[0],
                                a_rows=(r0, HEAD_ROWS)), dw_dt[r0:])
        dw_rest, recv_head = _mm(h0bt, dproj, "nn", "d_w_in", out_dtype=BF16, a_rows=(0, r0),
                                 exchange=([head], ("blocks",)))
        last = ("ssm_conv_w", "conf_conv_w")
        grad_x, dlng, dlnb, *recv_b = _d_h0_ln_bwd(
            *last_args, exchange=([_w_in_blocks(dw_rest, dw_dt[:r0])] + stack(last), ("blocks",) * 3))
        g["recv"] = dict(zip(LATE + ("w_in",) + last, recv_a + recv_b), w_in_head=recv_head)
    g["rows"] = [dlng, dlnb, dbx, dbb2, ddtb, dalog, ddsk, dng, dba, dbb, dccb, dclg, dclb, dbout, dln1g, dln1b,
                 dln2g, dln2b]
    return loss, grad_x, g


W_IN_SEGMENTS = ((0, 2048, 2048), (2048, 5120, 512), (2560, None, N_HEADS), (2576, 0, 2048), (4624, 4096, 1024))


def _w_in_to_main(shards):
    def pieces(p0, width):
        out, p = [], p0
        while p < p0 + width:
            d = p // COLS_PER_DEV
            hi = min(p0 + width, (d + 1) * COLS_PER_DEV)
            out.append(shards[d][:, p - d * COLS_PER_DEV:hi - d * COLS_PER_DEV])
            p = hi
        return out
    main = [s for s in sorted(W_IN_SEGMENTS, key=lambda s: -1 if s[1] is None else s[1]) if s[1] is not None]
    w_main = jnp.concatenate([q for p0, _, width in main for q in pieces(p0, width)], axis=1)
    w_dt = jnp.concatenate(pieces(2560, N_HEADS), axis=1)
    return w_main, jnp.pad(w_dt, ((0, 0), (0, LANES - N_HEADS)))


def _w_in_blocks(dw_main, dw_dt):
    blocks = []
    for d in range(N_DEV):
        lo_d, hi_d = d * COLS_PER_DEV, (d + 1) * COLS_PER_DEV
        parts = []
        for p0, m0, width in W_IN_SEGMENTS:
            lo, hi = max(lo_d, p0), min(hi_d, p0 + width)
            if lo < hi:
                parts.append(dw_dt[:, lo - p0:hi - p0] if m0 is None else dw_main[:, m0 + lo - p0:m0 + hi - p0])
        blocks.append(jnp.concatenate(parts, axis=1))
    return jnp.stack(blocks)


WEIGHTS = ['ln_emb_g', 'ln_emb_b', 'w_in', 'ssm_conv_w', 'ssm_conv_b', 'dt_bias', 'a_log', 'd_skip', 'ssm_norm_g',
           'b_glu', 'conf_conv_w', 'conf_conv_b', 'conf_ln_g', 'conf_ln_b', 'w_out', 'b_out', 'ln1_g', 'ln1_b',
           'w_ple_gate', 'w_ple_proj', 'ln2_g', 'ln2_b']
SHARDED = (("w_in", True), ("w_out", False), ("w_ple_gate", False), ("w_ple_proj", True), ("ssm_conv_w", True),
           ("conf_conv_w", True))
BY_COLS = dict(SHARDED)


def _stack_shards(a, by_cols):
    if by_cols:
        return a.reshape(a.shape[0], N_DEV, a.shape[1] // N_DEV).transpose(1, 0, 2)
    return a.reshape(N_DEV, a.shape[0] // N_DEV, a.shape[1])


def _unstack_shards(a, by_cols):
    if by_cols:
        return a.transpose(1, 0, 2).reshape(a.shape[1], N_DEV * a.shape[2])
    return a.reshape(N_DEV * a.shape[1], a.shape[2])


def kernel(x, p, ln_emb_g, ln_emb_b, w_in, ssm_conv_w, ssm_conv_b, dt_bias, a_log, d_skip, ssm_norm_g, b_glu, conf_conv_w, conf_conv_b, conf_ln_g, conf_ln_b, w_out, b_out, ln1_g, ln1_b, w_ple_gate, w_ple_proj, ln2_g, ln2_b, loss_target, m_ln_emb_g, m_ln_emb_b, m_w_in, m_ssm_conv_w, m_ssm_conv_b, m_dt_bias, m_a_log, m_d_skip, m_ssm_norm_g, m_b_glu, m_conf_conv_w, m_conf_conv_b, m_conf_ln_g, m_conf_ln_b, m_w_out, m_b_out, m_ln1_g, m_ln1_b, m_w_ple_gate, m_w_ple_proj, m_ln2_g, m_ln2_b, v_ln_emb_g, v_ln_emb_b, v_w_in, v_ssm_conv_w, v_ssm_conv_b, v_dt_bias, v_a_log, v_d_skip, v_ssm_norm_g, v_b_glu, v_conf_conv_w, v_conf_conv_b, v_conf_ln_g, v_conf_ln_b, v_w_out, v_b_out, v_ln1_g, v_ln1_b, v_w_ple_gate, v_w_ple_proj, v_ln2_g, v_ln2_b):
    loc = dict(locals())
    w = {n: loc[n] for n in WEIGHTS}
    m = {n: loc["m_" + n] for n in WEIGHTS}
    v = {n: loc["v_" + n] for n in WEIGHTS}
    sharded = [n for n, _ in SHARDED]

    shards = {n: w[n][0].astype(BF16) if n.startswith("w_") else w[n][0] for n in sharded}
    W = {n: w[n].reshape(-1) for n in SMALL}
    loss, grad_x, g = _local_step(x[0], p[0, 0], loss_target[0], W, shards=shards)
    (recv_small,) = _exchange([_pack_small(g["rows"], loss)], ("all",), "small_exchange")

    grads, delta, new_m, new_v = {}, {}, {}, {}
    for n in sharded:
        grads[n], delta[n], new_m[n], new_v[n] = _sum8_adamw(
            g["recv"][n], w[n], m[n], v[n], "adamw_" + n, head=g["recv"]["w_in_head"] if n == "w_in" else None)
    two_d = lambda d: {n: d[n].reshape(1, -1) for n in SMALL}
    *small, loss = _small_update(recv_small, two_d(w), two_d(m), two_d(v))
    for dst, res in zip((grads, delta, new_m, new_v), small):
        for n in SMALL:
            dst[n] = res[n].reshape(w[n].shape)
    return (loss[0, 0], grad_x[None], *[grads[n] for n in WEIGHTS], *[delta[n] for n in WEIGHTS],
            *[new_m[n] for n in WEIGHTS], *[new_v[n] for n in WEIGHTS])
```

```python
import numpy as np
import jax
import jax.numpy as jnp
from jax import lax
from jax.experimental import pallas as pl
from jax.experimental.pallas import tpu as pltpu

F32, BF16 = jnp.float32, jnp.bfloat16
S = jax.ShapeDtypeStruct

N_DEV = 8
D = 1024
D_PLE = 256
D_SSM = 1024
D_CONF = 1024
N_HEADS = 16
HEAD = 64
N_STATE = 128
CHUNK = 128
K_SSM = 4
K_CONF = 31
D_IN = 5648
COLS_PER_DEV = D_IN // N_DEV
LN_EPS = 1e-5
RMS_EPS = 1e-5
ALPHA = 2.0 ** 0.25
LANES = 128
TB = 512
TB_SSD_CONV_BWD = 512
TM_IN_PROJ = 2048
TM_FUSED = 1024
RG = 32
ROW_UNROLL = 8
HALO_SSM = 8
HALO_CONF = 32
VMEM_LIMIT = 56 * 1024 * 1024

ADAM_LR, ADAM_B1, ADAM_B2, ADAM_EPS, ADAM_WD, ADAM_STEP = 0.001, 0.9, 0.999, 1e-08, 0.01, 10

C_GLUA, C_GLUB, C_XS, C_Z, C_CG = 0, 1, 2, 3, 4
C_BC = 10
N_MAIN = 5632


def _params(sem, vmem=VMEM_LIMIT):
    return pltpu.CompilerParams(dimension_semantics=sem, vmem_limit_bytes=vmem)


def _row(tb, n, col=0):
    return pl.BlockSpec((tb, n), lambda i: (i, col))


def _colt(n, tb, row=0):
    return pl.BlockSpec((n, tb), lambda i: (row, i))


def _full(shape):
    return pl.BlockSpec(shape, lambda i: (0,) * len(shape))


_ANY = pl.BlockSpec(memory_space=pl.ANY)


def _prev(tb, halo, n, col=0):
    r = tb // halo
    return pl.BlockSpec((halo, n), lambda i: (jnp.maximum(i * r - 1, 0), col))


def _next(tb, halo, n, nt, col=0):
    r = tb // halo
    return pl.BlockSpec((halo, n), lambda i: (jnp.minimum((i + 1) * r, nt * r - 1), col))


def _row_loop(tb, fn):
    def it(r, c):
        fn(pl.ds(pl.multiple_of(r * RG, RG), RG))
        return c
    lax.fori_loop(0, tb // RG, it, 0, unroll=ROW_UNROLL)


def _col_loop(n, fn):
    def it(j, c):
        fn(pl.ds(pl.multiple_of(j * LANES, LANES), LANES))
        return c
    lax.fori_loop(0, n // LANES, it, 0)


def _sigmoid(x):
    return 1.0 / (1.0 + jnp.exp(-x))


def _dsilu(x, s):
    return s * (1.0 + x * (1.0 - s))


def _ln_stats(v):
    mu = jnp.mean(v, axis=-1, keepdims=True)
    c = v - mu
    r = lax.rsqrt(jnp.mean(c * c, axis=-1, keepdims=True) + LN_EPS)
    return c * r, r


def _ln_bwd(dy, xhat, r, g):
    dxh = dy * g
    dv = r * (dxh - jnp.mean(dxh, axis=-1, keepdims=True) - xhat * jnp.mean(dxh * xhat, axis=-1, keepdims=True))
    return dv, jnp.sum(dy * xhat, axis=0, keepdims=True), jnp.sum(dy, axis=0, keepdims=True)


def _dot(a, b, dims=((1,), (0,))):
    return lax.dot_general(a.astype(BF16), b.astype(BF16), (dims, ((), ())), preferred_element_type=F32)


_NT = ((1,), (1,))
_TN = ((0,), (0,))


def _split3(x):
    hi = x.astype(BF16)
    r = x - hi.astype(F32)
    mid = r.astype(BF16)
    return hi, mid, (r - mid.astype(F32)).astype(BF16)


def _dot_sel_b(a, b, dims=((1,), (0,))):
    hi, mid, lo = _split3(a)
    return (_dot(lo, b, dims) + _dot(mid, b, dims)) + _dot(hi, b, dims)


def _dot_sel_a(a, b, dims=((1,), (0,))):
    hi, mid, lo = _split3(b)
    return (_dot(a, lo, dims) + _dot(a, mid, dims)) + _dot(a, hi, dims)


def _mm(a, b, mode, name, out_dtype=F32, add=None, tm=1024, tn=None, tk=1024, exchange=None, a_rows=None):
    assert mode in ("nn", "nt")
    (M, K), N = a.shape, b.shape[1 if mode == "nn" else 0]
    row0 = 0
    if a_rows is not None:
        row0, M = a_rows
        tm = M
        assert row0 % M == 0
    if tn is None:
        tn = next(t for t in (1024, 1408, 512, 256, LANES) if N % t == 0)
    tm, tn, tk = min(tm, M), min(tn, N), min(tk, K)
    assert M % tm == 0 and N % tn == 0 and K % tk == 0, (name, M, N, K)
    grid = (M // tm, N // tn, K // tk)
    nk = grid[2]
    dims = ((1,), (0,)) if mode == "nn" else _NT
    n_in = 2 + (add is not None)
    xbufs, kinds = exchange if exchange is not None else ((), ())
    nx = len(xbufs)

    def body(*refs):
        a_ref, b_ref = refs[:2]
        o_ref = refs[n_in + nx]
        acc = refs[n_in + 2 * nx + 1]
        i, j, k = pl.program_id(0), pl.program_id(1), pl.program_id(2)
        if nx:
            start, finish = _exchange_plan(refs[n_in:n_in + nx], refs[n_in + nx + 1:n_in + 2 * nx + 1], kinds,
                                           *refs[n_in + 2 * nx + 2:])
            pl.when((i == 0) & (j == 0) & (k == 0))(start)

        d = _dot(a_ref[...], b_ref[...], dims)

        def write_out(r):
            if add is not None:
                r = r + refs[2][...]
            o_ref[...] = r.astype(out_dtype)

        if nk == 1:
            write_out(d)
        else:
            @pl.when(k == 0)
            def _():
                acc[...] = d

            @pl.when((k > 0) & (k < nk - 1))
            def _():
                acc[...] += d

            @pl.when(k == nk - 1)
            def _():
                write_out(acc[...] + d)

        if nx:
            pl.when((i == grid[0] - 1) & (j == grid[1] - 1) & (k == nk - 1))(finish)

    a_spec = pl.BlockSpec((tm, tk), lambda i, j, k: (i + row0 // tm, k))
    b_spec = pl.BlockSpec((tn, tk), lambda i, j, k: (j, k)) if mode == "nt" else pl.BlockSpec((tk, tn), lambda i, j, k: (k, j))
    o_spec = pl.BlockSpec((tm, tn), lambda i, j, k: (i, j))
    ins, specs = [a, b], [a_spec, b_spec]
    if add is not None:
        ins.append(add)
        specs.append(o_spec)
    acc_spec = pltpu.VMEM((tm, tn) if nk > 1 else (8, LANES), F32)
    if not nx:
        return pl.pallas_call(
            body, name=name, grid=grid, in_specs=specs, out_specs=o_spec,
            out_shape=S((M, N), out_dtype), scratch_shapes=[acc_spec],
            compiler_params=_params(("parallel", "parallel", "arbitrary")))(*ins)
    return pl.pallas_call(
        body, name=name, grid=grid, in_specs=specs + [_ANY] * nx, out_specs=[o_spec] + [_ANY] * nx,
        out_shape=[S((M, N), out_dtype)] + _exchange_shapes(xbufs, kinds),
        scratch_shapes=[acc_spec] + _exchange_sems(nx),
        compiler_params=_params(("arbitrary", "arbitrary", "arbitrary")))(*ins, *xbufs)


def _ln_emb_fwd(x, g, b, exchange=None):
    T = x.shape[0]

    nt = T // TB
    xbufs, kinds = exchange if exchange is not None else ((), ())
    nx = len(xbufs)

    def body(*refs):
        x_ref, g_ref, b_ref = refs[:3]
        h_ref, hb_ref, hbt_ref = refs[3 + nx:6 + nx]
        i = pl.program_id(0)
        if nx:
            start, finish = _exchange_plan(refs[3:3 + nx], refs[6 + nx:6 + 2 * nx], kinds, *refs[6 + 2 * nx:])
            pl.when(i == 0)(start)

        def rows(rs):
            xh, _ = _ln_stats(x_ref[rs, :])
            h = xh * g_ref[...] + b_ref[...]
            h_ref[rs, :] = h
            hb_ref[rs, :] = h.astype(BF16)
        _row_loop(TB, rows)
        hbt_ref[...] = hb_ref[...].T
        if nx:
            pl.when(i == nt - 1)(finish)

    return pl.pallas_call(
        body, name="ln_emb_fwd", grid=(nt,),
        in_specs=[_row(TB, D), _full((1, D)), _full((1, D))] + [_ANY] * nx,
        out_specs=[_row(TB, D), _row(TB, D), _colt(D, TB)] + [_ANY] * nx,
        out_shape=[S((T, D), F32), S((T, D), BF16), S((D, T), BF16)] + _exchange_shapes(xbufs, kinds),
        scratch_shapes=_exchange_sems(nx) if nx else [],
        compiler_params=_params(("arbitrary",)))(x, g, b, *xbufs)


def _out_proj_post1(mix, w_out, h0, b_out, g, b, tm=TM_FUSED, tk=1024):
    T, K = mix.shape
    tm = min(tm, T)
    nk = K // tk
    assert T % tm == 0 and K % tk == 0 and nk >= 2

    def body(mix_ref, w_ref, h0_ref, bo_ref, g_ref, b_ref, out_ref, h_ref, hb_ref, hbt_ref, acc):
        k = pl.program_id(1)
        d = _dot(mix_ref[...], w_ref[...])

        @pl.when(k == 0)
        def _():
            acc[...] = d

        @pl.when((k > 0) & (k < nk - 1))
        def _():
            acc[...] += d

        @pl.when(k == nk - 1)
        def _():
            out_ref[...] = acc[...] + d

            def rows(rs):
                xh, _ = _ln_stats(ALPHA * h0_ref[rs, :] + out_ref[rs, :] + bo_ref[...])
                h = xh * g_ref[...] + b_ref[...]
                h_ref[rs, :] = h
                hb_ref[rs, :] = h.astype(BF16)
            _row_loop(tm, rows)
            hbt_ref[...] = hb_ref[...].T

    rowt = lambda n: pl.BlockSpec((tm, n), lambda i, k: (i, 0))
    const = pl.BlockSpec((1, D), lambda i, k: (0, 0))
    return pl.pallas_call(
        body, name="out_proj_post1", grid=(T // tm, nk),
        in_specs=[pl.BlockSpec((tm, tk), lambda i, k: (i, k)), pl.BlockSpec((tk, D), lambda i, k: (k, 0)), rowt(D),
                  const, const, const],
        out_specs=[rowt(D), rowt(D), rowt(D), pl.BlockSpec((D, tm), lambda i, k: (0, i))],
        out_shape=[S((T, D), F32), S((T, D), F32), S((T, D), BF16), S((D, T), BF16)],
        scratch_shapes=[pltpu.VMEM((tm, D), F32)],
        compiler_params=_params(("parallel", "arbitrary")))(mix, w_out, h0, b_out, g, b)


def _ple_post2(h1b, w_gate, pb, w_proj, h1, tgt, g, b, tm=TM_FUSED // 2):
    T = h1.shape[0]
    tm = min(tm, T)
    assert T % tm == 0

    def body(h1b_ref, wg_ref, pb_ref, wp_ref, h1_ref, tgt_ref, g_ref, b_ref,
             dh1_ref, dgp_ref, dple_ref, loss_ref, dg_ref, db_ref, gp_ref, ple_ref):
        @pl.when(pl.program_id(0) == 0)
        def _():
            loss_ref[...] = jnp.zeros_like(loss_ref)
            dg_ref[...] = jnp.zeros_like(dg_ref)
            db_ref[...] = jnp.zeros_like(db_ref)

        gp_ref[...] = _dot(h1b_ref[...], wg_ref[...])
        ple_ref[...] = _dot(pb_ref[...], wp_ref[...])

        def rows(rs):
            gate = _sigmoid(gp_ref[rs, :])
            ple = ple_ref[rs, :]
            xh, r = _ln_stats(ALPHA * h1_ref[rs, :] + gate * ple)
            err = xh * g_ref[...] + b_ref[...] - tgt_ref[rs, :]
            loss_ref[...] += 0.5 * jnp.sum(jnp.mean(err * err, axis=-1, keepdims=True), axis=0, keepdims=True)
            dv, dg, db = _ln_bwd(err * (1.0 / D), xh, r, g_ref[...])
            dg_ref[...] += dg
            db_ref[...] += db
            dh1_ref[rs, :] = ALPHA * dv
            dgp_ref[rs, :] = (dv * ple * gate * (1.0 - gate)).astype(BF16)
            dple_ref[rs, :] = (dv * gate).astype(BF16)
        _row_loop(tm, rows)

    return pl.pallas_call(
        body, name="ple_post2", grid=(T // tm,),
        in_specs=[_row(tm, D), _full((D, D)), _row(tm, D_PLE), _full((D_PLE, D)), _row(tm, D), _row(tm, D),
                  _full((1, D)), _full((1, D))],
        out_specs=[_row(tm, D)] * 3 + [_full((8, LANES)), _full((1, D)), _full((1, D))],
        out_shape=[S((T, D), F32), S((T, D), BF16), S((T, D), BF16), S((8, LANES), F32), S((1, D), F32), S((1, D), F32)],
        scratch_shapes=[pltpu.VMEM((tm, D), F32), pltpu.VMEM((tm, D), F32)],
        compiler_params=_params(("arbitrary",)))(h1b, w_gate, pb, w_proj, h1, tgt, g, b)


def _d_h1_post1_bwd(dgp, w_gate, dh1a, h0, out, b_out, g, tm=TM_FUSED // 2):
    T = h0.shape[0]
    tm = min(tm, T)
    assert T % tm == 0

    def body(dgp_ref, wg_ref, da_ref, h0_ref, out_ref, bo_ref, g_ref, dout_ref, dh0_ref, dg_ref, db_ref, dbo_ref, dh1):
        @pl.when(pl.program_id(0) == 0)
        def _():
            dg_ref[...] = jnp.zeros_like(dg_ref)
            db_ref[...] = jnp.zeros_like(db_ref)
            dbo_ref[...] = jnp.zeros_like(dbo_ref)

        dh1[...] = da_ref[...] + _dot(dgp_ref[...], wg_ref[...], _NT)

        def rows(rs):
            xh, r = _ln_stats(ALPHA * h0_ref[rs, :] + out_ref[rs, :] + bo_ref[...])
            dv, dg, db = _ln_bwd(dh1[rs, :], xh, r, g_ref[...])
            dg_ref[...] += dg
            db_ref[...] += db
            dbo_ref[...] += jnp.sum(dv, axis=0, keepdims=True)
            dout_ref[rs, :] = dv.astype(BF16)
            dh0_ref[rs, :] = ALPHA * dv
        _row_loop(tm, rows)

    return pl.pallas_call(
        body, name="d_h1_post1_bwd", grid=(T // tm,),
        in_specs=[_row(tm, D), _full((D, D))] + [_row(tm, D)] * 3 + [_full((1, D))] * 2,
        out_specs=[_row(tm, D)] * 2 + [_full((1, D))] * 3,
        out_shape=[S((T, D), BF16), S((T, D), F32)] + [S((1, D), F32)] * 3,
        scratch_shapes=[pltpu.VMEM((tm, D), F32)],
        compiler_params=_params(("arbitrary",)))(dgp, w_gate, dh1a, h0, out, b_out, g)


def _d_h0_ln_bwd(dproj, w_main, ddtr, w_dt, dh0a, x, g, exchange=None, tm=1024, tk=1408):
    T, K = dproj.shape
    tm = min(tm, T)
    assert T % tm == 0 and K % tk == 0
    ni, nk = T // tm, K // tk
    xbufs, kinds = exchange if exchange is not None else ((), ())
    nx = len(xbufs)

    def body(*refs):
        dp_ref, w_ref, dt_ref, wdt_ref, da_ref, x_ref, g_ref = refs[:7]
        dx_ref, dg_ref, db_ref = refs[7 + nx:10 + nx]
        acc = refs[10 + 2 * nx]
        i, k = pl.program_id(0), pl.program_id(1)
        if nx:
            start, finish = _exchange_plan(refs[7:7 + nx], refs[10 + nx:10 + 2 * nx], kinds, *refs[11 + 2 * nx:])
            pl.when((i == 0) & (k == 0))(start)

        @pl.when((i == 0) & (k == 0))
        def _():
            dg_ref[...] = jnp.zeros_like(dg_ref)
            db_ref[...] = jnp.zeros_like(db_ref)

        d = _dot(dp_ref[...], w_ref[...], _NT)

        @pl.when(k == 0)
        def _():
            acc[...] = da_ref[...] + _dot(dt_ref[...], wdt_ref[...], _NT) + d

        @pl.when(k > 0)
        def _():
            acc[...] += d

        @pl.when(k == nk - 1)
        def _():
            def rows(rs):
                xh, r = _ln_stats(x_ref[rs, :])
                dv, dg, db = _ln_bwd(acc[rs, :], xh, r, g_ref[...])
                dg_ref[...] += dg
                db_ref[...] += db
                dx_ref[rs, :] = dv
            _row_loop(tm, rows)

        if nx:
            pl.when((i == ni - 1) & (k == nk - 1))(finish)

    rowt = lambda n: pl.BlockSpec((tm, n), lambda i, k: (i, 0))
    const = lambda shape: pl.BlockSpec(shape, lambda i, k: (0, 0))
    return pl.pallas_call(
        body, name="d_h0_ln_bwd", grid=(ni, nk),
        in_specs=[pl.BlockSpec((tm, tk), lambda i, k: (i, k)), pl.BlockSpec((D, tk), lambda i, k: (0, k)),
                  rowt(LANES), const((D, LANES)), rowt(D), rowt(D), const((1, D))] + [_ANY] * nx,
        out_specs=[rowt(D), const((1, D)), const((1, D))] + [_ANY] * nx,
        out_shape=[S((T, D), F32), S((1, D), F32), S((1, D), F32)] + _exchange_shapes(xbufs, kinds),
        scratch_shapes=[pltpu.VMEM((tm, D), F32)] + (_exchange_sems(nx) if nx else []),
        compiler_params=_params(("arbitrary", "arbitrary")))(dproj, w_main, ddtr, w_dt, dh0a, x, g, *xbufs)


def _softplus(x):
    return jnp.maximum(x, 0.0) + jnp.log1p(jnp.exp(-jnp.abs(x)))


def _ssd_pre_fwd(proj, h0b, w_dt, wx, wb, bx, bb, dt_bias):
    T = proj.shape[0]
    H = HALO_SSM

    def body(xs_ref, xsp_ref, bc_ref, bcp_ref, h0b_ref, wdt_ref, wx_ref, wb_ref, bx_ref, bb_ref, dtb_ref,
             xso_ref, bco_ref, dto_ref, dtr_ref, extx, extb):
        first = pl.program_id(0) == 0

        def conv(t_ref, p_ref, w_ref, b_ref, o_ref, ext, n):
            def blk(cols):
                ext[0:H, cols] = jnp.where(first, 0.0, p_ref[:, cols])
                ext[H:, cols] = t_ref[:, cols]
                for r0 in range(0, TB, 64):
                    acc = jnp.broadcast_to(b_ref[:, cols], (64, LANES))
                    for k in range(K_SSM):
                        acc = acc + w_ref[k:k + 1, cols] * ext[pl.ds(r0 + H - (K_SSM - 1) + k, 64), cols]
                    o_ref[pl.ds(r0, 64), cols] = acc * _sigmoid(acc)
            _col_loop(n, blk)

        conv(xs_ref, xsp_ref, wx_ref, bx_ref, xso_ref, extx, D_SSM)
        conv(bc_ref, bcp_ref, wb_ref, bb_ref, bco_ref, extb, 512)
        dtr_ref[...] = _dot(h0b_ref[...], wdt_ref[...])
        dto_ref[...] = _softplus(dtr_ref[...] + dtb_ref[...])

    return pl.pallas_call(
        body, name="ssd_pre_fwd", grid=(T // TB,),
        in_specs=[_row(TB, 1024, C_XS), _prev(TB, H, 1024, C_XS), _row(TB, 512, C_BC), _prev(TB, H, 512, C_BC),
                  _row(TB, D), _full((D, LANES)), _full((K_SSM, 1024)), _full((K_SSM, 512)), _full((1, 1024)),
                  _full((1, 512)), _full((1, LANES))],
        out_specs=[_row(TB, 1024), _row(TB, 512), _row(TB, LANES), _row(TB, LANES)],
        out_shape=[S((T, 1024), F32), S((T, 512), F32), S((T, LANES), F32), S((T, LANES), F32)],
        scratch_shapes=[pltpu.VMEM((H + TB, 1024), F32), pltpu.VMEM((H + TB, 512), F32)],
        compiler_params=_params(("parallel",)))(proj, proj, proj, proj, h0b, w_dt, wx, wb, bx, bb, dt_bias)


def _ssd_conv_bwd(dproj, proj, d_c, w, b, n, col, name):
    TB = TB_SSD_CONV_BWD
    T = proj.shape[0]
    nt = T // TB
    H = HALO_SSM
    R = TB + H

    def body(dproj_ref, t_ref, p_ref, n_ref, d_ref, dn_ref, w_ref, b_ref, o_ref, dw_ref, dbias_ref, ext, dp):
        i = pl.program_id(0)
        first, last = i == 0, i == nt - 1

        @pl.when(first)
        def _():
            dw_ref[...] = jnp.zeros_like(dw_ref)
            dbias_ref[...] = jnp.zeros_like(dbias_ref)

        def blk(cols):
            ext[0:H, cols] = jnp.where(first, 0.0, p_ref[:, cols])
            ext[H:H + TB, cols] = t_ref[:, cols]
            ext[H + TB:, cols] = n_ref[:, cols]
            def taps_and_dsilu(r0, rows):
                taps = [ext[pl.ds(r0 + H - (K_SSM - 1) + k, rows), cols] for k in range(K_SSM)]
                pre = jnp.broadcast_to(b_ref[:, cols], (rows, LANES))
                for k in range(K_SSM):
                    pre = pre + w_ref[k:k + 1, cols] * taps[k]
                return taps, _dsilu(pre, _sigmoid(pre))

            for r0 in range(0, TB, 64):
                taps, ds = taps_and_dsilu(r0, 64)
                dpt = d_ref[pl.ds(r0, 64), cols] * ds
                dp[pl.ds(r0, 64), cols] = dpt
                dbias_ref[:, cols] += jnp.sum(dpt, axis=0, keepdims=True)
                for k in range(K_SSM):
                    dw_ref[k:k + 1, cols] += jnp.sum(dpt * taps[k], axis=0, keepdims=True)
            dp[TB:, cols] = jnp.where(last, 0.0, dn_ref[:, cols] * taps_and_dsilu(TB, H)[1])
            for r0 in range(0, TB, 64):
                acc = jnp.zeros((64, LANES), F32)
                for k in range(K_SSM):
                    acc = acc + w_ref[k:k + 1, cols] * dp[pl.ds(r0 + K_SSM - 1 - k, 64), cols]
                o_ref[pl.ds(r0, 64), cols] = acc.astype(BF16)
        _col_loop(n, blk)

    return pl.pallas_call(
        body, name=name, grid=(nt,),
        in_specs=[_ANY, _row(TB, n, col), _prev(TB, H, n, col), _next(TB, H, n, nt, col),
                  _row(TB, n), _next(TB, H, n, nt), _full((K_SSM, n)), _full((1, n))],
        out_specs=[_row(TB, n, col), _full((K_SSM, n)), _full((1, n))],
        out_shape=[S(dproj.shape, BF16), S((K_SSM, n), F32), S((1, n), F32)],
        input_output_aliases={0: 0},
        scratch_shapes=[pltpu.VMEM((H + TB + H, n), F32), pltpu.VMEM((R, n), F32)],
        compiler_params=_params(("arbitrary",)))(dproj, proj, proj, proj, d_c, d_c, w, b)


def _ssd_consts():
    ex = np.zeros((LANES, D_SSM), np.float32)
    for h in range(N_HEADS):
        ex[h, h * HEAD:(h + 1) * HEAD] = 1.0
    tri = np.tril(np.ones((CHUNK, CHUNK), np.float32))
    return jnp.asarray(ex), jnp.asarray(ex.T.copy()), jnp.asarray(tri), jnp.asarray(tri.T.copy())


def _ssd_common(xs, dt, alog_ref, ex_ref, tri_ref):
    lane = lax.broadcasted_iota(jnp.int32, (1, LANES), 1)
    a = jnp.where(lane < N_HEADS, -jnp.exp(alog_ref[...]), 0.0)
    A = _dot_sel_a(tri_ref[...], dt * a)
    ex = ex_ref[...]
    Aex = _dot_sel_b(A, ex)
    dtex = _dot_sel_b(dt, ex)
    expA = jnp.exp(Aex)
    dec = jnp.exp(Aex[CHUNK - 1:CHUNK, :] - Aex)
    cd = _dot_sel_a(ex, jnp.broadcast_to(jnp.exp(A.T[:, CHUNK - 1:CHUNK]), (LANES, LANES)), _TN)
    return a, A, dtex, expA, dec, cd


def _decay_mask():
    sub = lax.broadcasted_iota(jnp.int32, (CHUNK, CHUNK), 0)
    lane = lax.broadcasted_iota(jnp.int32, (CHUNK, CHUNK), 1)
    return sub, lane, sub >= lane


def _ssd_fwd(xs_c, bc_c, dt, proj, alog, dskip_row, norm_g):
    T = xs_c.shape[0]
    nc = T // CHUNK
    ex, _, tri, _ = _ssd_consts()

    def body(xs_ref, bc_ref, dt_ref, z_ref, alog_ref, dsk_ref, ng_ref, ex_ref, tri_ref,
             ys_ref, ypre_ref, hprev_ref, yst_ref, Hs, ybuf):
        @pl.when(pl.program_id(0) == 0)
        def _():
            Hs[...] = jnp.zeros_like(Hs)

        hprev_ref[0] = Hs[...]
        xs, dt = xs_ref[...], dt_ref[...]
        a, A, dtex, expA, dec, cd = _ssd_common(xs, dt, alog_ref, ex_ref, tri_ref)
        AT = A.T
        xdt = xs * dtex
        xdec = xdt * dec
        _, _, causal = _decay_mask()
        for g in range(2):
            gs = slice(g * 512, (g + 1) * 512)
            B = bc_ref[:, g * N_STATE:(g + 1) * N_STATE]
            C = bc_ref[:, 256 + g * N_STATE:256 + (g + 1) * N_STATE]
            cb = _dot(C, B, _NT)
            Hg = Hs[gs, :]
            yoff = _dot(C, Hg, _NT) * expA[:, gs]
            for j in range(8):
                h = g * 8 + j
                hs = slice(h * HEAD, (h + 1) * HEAD)
                L = jnp.exp(jnp.where(causal, A[:, h:h + 1] - AT[h:h + 1, :], -1e30))
                ybuf[:, hs] = _dot(cb * L, xdt[:, hs]) + yoff[:, j * HEAD:(j + 1) * HEAD]
            Hs[gs, :] = cd[gs, :] * Hg + _dot(xdec[:, gs], B, _TN)
        ypre = ybuf[...] + dsk_ref[...] * xs
        ypre_ref[...] = ypre
        z = z_ref[...]
        yz = ypre * (z * _sigmoid(z))
        for g in range(2):
            gs = slice(g * 512, (g + 1) * 512)
            v = yz[:, gs]
            r = lax.rsqrt(jnp.mean(v * v, axis=-1, keepdims=True) + RMS_EPS)
            ys_ref[:, gs] = (v * r * ng_ref[:, gs]).astype(BF16)
        yst_ref[...] = ys_ref[...].T

    return pl.pallas_call(
        body, name="ssd_fwd", grid=(nc,),
        in_specs=[_row(CHUNK, 1024), _row(CHUNK, 512), _row(CHUNK, LANES), _row(CHUNK, 1024, C_Z),
                  _full((1, LANES)), _full((1, 1024)), _full((1, 1024)), _full((LANES, 1024)), _full((CHUNK, CHUNK))],
        out_specs=[_row(CHUNK, 1024), _row(CHUNK, 1024), pl.BlockSpec((1, 1024, N_STATE), lambda c: (c, 0, 0)),
                   _colt(1024, CHUNK)],
        out_shape=[S((T, 2048), BF16), S((T, 1024), F32), S((nc, 1024, N_STATE), F32), S((2048, T), BF16)],
        scratch_shapes=[pltpu.VMEM((1024, N_STATE), F32), pltpu.VMEM((CHUNK, 1024), F32)],
        compiler_params=_params(("arbitrary",)))(xs_c, bc_c, dt, proj, alog, dskip_row, norm_g, ex, tri)


def _ssd_bwd(dproj, xs_c, bc_c, dt, dt_raw, dt_bias, proj, ypre, hprev, dmix, alog, dskip_row, norm_g, exchange=None):
    T = xs_c.shape[0]
    nc = T // CHUNK
    ex, ext, tri, triu = _ssd_consts()
    rev = lambda n, col=0: pl.BlockSpec((CHUNK, n), lambda c: (nc - 1 - c, col))
    xbufs, kinds = exchange if exchange is not None else ((), ())
    nx = len(xbufs)
    N_IN, N_OUT = 17, 8

    def body(*refs):
        (dproj_ref, xs_ref, bc_ref, dt_ref, dtr_ref, dtb_ref, z_ref, ypre_ref, hprev_ref, dys_ref, alog_ref, dsk_ref,
         ng_ref, ex_ref, ext_ref, tri_ref, triu_ref) = refs[:N_IN]
        (dxs_ref, dbc_ref, ddt_ref, dz_ref, dng_ref, ddsk_ref, dalog_ref,
         ddtb_ref) = refs[N_IN + nx:N_IN + nx + N_OUT]
        dHs, dxbuf, dskacc = refs[N_IN + N_OUT + 2 * nx:N_IN + N_OUT + 2 * nx + 3]
        c = pl.program_id(0)
        if nx:
            start, finish = _exchange_plan(refs[N_IN:N_IN + nx], refs[N_IN + nx + N_OUT:N_IN + N_OUT + 2 * nx], kinds,
                                           *refs[N_IN + N_OUT + 2 * nx + 3:])
            pl.when(c == 0)(start)

        @pl.when(c == 0)
        def _():
            dHs[...] = jnp.zeros_like(dHs)
            dng_ref[...] = jnp.zeros_like(dng_ref)
            dalog_ref[...] = jnp.zeros_like(dalog_ref)
            ddtb_ref[...] = jnp.zeros_like(ddtb_ref)
            dskacc[...] = jnp.zeros_like(dskacc)

        xs, dt, z, ypre, dys = xs_ref[...], dt_ref[...], z_ref[...], ypre_ref[...], dys_ref[...]
        sg = _sigmoid(z)
        sz = z * sg
        yz = ypre * sz
        dyz_parts = []
        for g in range(2):
            gs = slice(g * 512, (g + 1) * 512)
            v = yz[:, gs]
            r = lax.rsqrt(jnp.mean(v * v, axis=-1, keepdims=True) + RMS_EPS)
            vn = v * r
            dng_ref[:, gs] += jnp.sum(dys[:, gs] * vn, axis=0, keepdims=True)
            dvn = dys[:, gs] * ng_ref[:, gs]
            dyz_parts.append(r * (dvn - vn * jnp.mean(dvn * vn, axis=-1, keepdims=True)))
        dyz = jnp.concatenate(dyz_parts, axis=1)
        dy = dyz * sz
        dz_ref[...] = (dyz * ypre * _dsilu(z, sg)).astype(BF16)
        dskacc[...] += jnp.sum(dy * xs, axis=0, keepdims=True)

        a, A, dtex, expA, dec, cd = _ssd_common(xs, dt, alog_ref, ex_ref, tri_ref)
        AT = A.T
        xdt = xs * dtex
        xdec = xdt * dec
        dye = dy * expA
        H = hprev_ref[0]
        dHn = dHs[...]
        sub, lane, causal = _decay_mask()
        dAc = jnp.zeros((CHUNK, LANES), F32)
        Rm = jnp.zeros((CHUNK, LANES), F32)
        yoff_parts, q_parts = [], []
        for g in range(2):
            gs = slice(g * 512, (g + 1) * 512)
            B = bc_ref[:, g * N_STATE:(g + 1) * N_STATE]
            C = bc_ref[:, 256 + g * N_STATE:256 + (g + 1) * N_STATE]
            cb = _dot(C, B, _NT)
            Hg, dHg = H[gs, :], dHn[gs, :]
            Q = _dot(B, dHg, _NT)
            yoff_parts.append(_dot(C, Hg, _NT) * expA[:, gs])
            q_parts.append(Q)
            dcb = jnp.zeros((CHUNK, CHUNK), F32)
            for j in range(8):
                h = g * 8 + j
                hs = slice(h * HEAD, (h + 1) * HEAD)
                L = jnp.exp(jnp.where(causal, A[:, h:h + 1] - AT[h:h + 1, :], -1e30))
                M = cb * L
                G = _dot(dy[:, hs], xdt[:, hs], _NT)
                dxbuf[:, hs] = _dot(M, dy[:, hs], _TN)
                dcb = dcb + G * L
                E = G * M
                dAc = jnp.where(lane == h, jnp.sum(E, axis=1, keepdims=True), dAc)
                Rm = jnp.where(sub == h, jnp.sum(E, axis=0, keepdims=True), Rm)
            dbc_ref[:, g * N_STATE:(g + 1) * N_STATE] = _dot(dcb, C, _TN) + _dot(xdec[:, gs], dHg)
            dbc_ref[:, 256 + g * N_STATE:256 + (g + 1) * N_STATE] = _dot(dcb, B) + _dot(dye[:, gs], Hg)
            dHs[gs, :] = cd[gs, :] * dHg + _dot(dye[:, gs], C, _TN)
        yoff = jnp.concatenate(yoff_parts, axis=1)
        Qd = jnp.concatenate(q_parts, axis=1) * dec
        dxdt = dxbuf[...] + Qd
        extm = ext_ref[...]
        red_s = _dot_sel_b(xdt * Qd, extm)
        dA = dAc - Rm.T + _dot_sel_b(dy * yoff, extm) - red_s
        hd = jnp.sum(_dot_sel_b(H * dHn, extm, _TN), axis=0, keepdims=True)
        last_add = jnp.sum(red_s, axis=0, keepdims=True) + jnp.exp(A[CHUNK - 1:CHUNK, :]) * hd
        dA = dA + jnp.where(sub == CHUNK - 1, last_add, 0.0)
        dadt = _dot_sel_a(triu_ref[...], dA)
        ddtr = (dadt * a + _dot_sel_b(dxdt * xs, extm)) * _sigmoid(dtr_ref[...] + dtb_ref[...])
        ddt_ref[...] = ddtr.astype(BF16)
        ddtb_ref[...] += jnp.sum(ddtr, axis=0, keepdims=True)
        dalog_ref[...] += jnp.sum(dadt * dt, axis=0, keepdims=True) * a
        dxs_ref[...] = dxdt * dtex + dsk_ref[...] * dy

        @pl.when(c == nc - 1)
        def _():
            ddsk_ref[...] = _dot_sel_b(jnp.broadcast_to(dskacc[...], (8, 1024)), extm)[0:1, :]

        if nx:
            pl.when(c == nc - 1)(finish)

    return pl.pallas_call(
        body, name="ssd_bwd", grid=(nc,),
        in_specs=[_ANY, rev(1024), rev(512), rev(LANES), rev(LANES), _full((1, LANES)), rev(1024, C_Z), rev(1024),
                  pl.BlockSpec((1, 1024, N_STATE), lambda c: (nc - 1 - c, 0, 0)), rev(1024, 0),
                  _full((1, LANES)), _full((1, 1024)), _full((1, 1024)),
                  _full((LANES, 1024)), _full((1024, LANES)), _full((CHUNK, CHUNK)), _full((CHUNK, CHUNK))] + [_ANY] * nx,
        out_specs=[rev(1024), rev(512), rev(LANES), rev(1024, C_Z), _full((1, 1024)), _full((1, LANES)),
                   _full((1, LANES)), _full((1, LANES))] + [_ANY] * nx,
        out_shape=[S((T, 1024), F32), S((T, 512), F32), S((T, LANES), BF16), S(dproj.shape, BF16),
                   S((1, 1024), F32), S((1, LANES), F32), S((1, LANES), F32), S((1, LANES), F32)]
        + _exchange_shapes(xbufs, kinds),
        input_output_aliases={0: 3},
        scratch_shapes=[pltpu.VMEM((1024, N_STATE), F32), pltpu.VMEM((CHUNK, 1024), F32), pltpu.VMEM((1, 1024), F32)]
        + (_exchange_sems(nx) if nx else []),
        compiler_params=_params(("arbitrary",)))(
            dproj, xs_c, bc_c, dt, dt_raw, dt_bias, proj, ypre, hprev, dmix, alog, dskip_row, norm_g, ex, ext, tri, triu,
            *xbufs)


def _shifted_copies(ext, ext8):
    n = ext8.shape[1]
    for r in range(8):
        ext8[r] = ext[pl.ds(r, n), :]


def _shifted(ext8, off, rows):
    return ext8[off % 8, pl.ds(off - off % 8, rows), :]


def _conf_fwd(mix, mixt, proj, w, cb, lg, lb, ba, bb):
    T = proj.shape[0]
    H = HALO_CONF

    def body(mix_ref, mixt_ref, ga_ref, gap_ref, gb_ref, gbp_ref, cg_ref, w_ref, cb_ref, lg_ref, lb_ref, ba_ref,
             bb_ref, u1_ref, yc_ref, yct_ref, ext, ext8):
        first = pl.program_id(0) == 0
        ext[H + TB:, :] = jnp.zeros((8, LANES), F32)

        def blk(cols):
            up = (gap_ref[:, cols] + ba_ref[:, cols]) * _sigmoid(gbp_ref[:, cols] + bb_ref[:, cols])
            ext[0:H, :] = jnp.where(first, 0.0, up)
            ext[H:H + TB, :] = (ga_ref[:, cols] + ba_ref[:, cols]) * _sigmoid(gb_ref[:, cols] + bb_ref[:, cols])
            _shifted_copies(ext, ext8)
            for r0 in range(0, TB, 64):
                acc = jnp.broadcast_to(cb_ref[:, cols], (64, LANES))
                for k in range(K_CONF):
                    acc = acc + w_ref[k:k + 1, cols] * _shifted(ext8, r0 + H - (K_CONF - 1) + k, 64)
                u1_ref[pl.ds(r0, 64), cols] = acc
        _col_loop(D_CONF, blk)

        def rows(rs):
            xh, _ = _ln_stats(u1_ref[rs, :])
            u2 = xh * lg_ref[...] + lb_ref[...]
            cg = cg_ref[rs, :]
            yc_ref[rs, :] = (u2 * _sigmoid(u2) * cg * _sigmoid(cg)).astype(BF16)
        _row_loop(TB, rows)
        yct_ref[...] = yc_ref[...].T

    return pl.pallas_call(
        body, name="conf_fwd", grid=(T // TB,),
        in_specs=[_ANY, _ANY, _row(TB, 1024, C_GLUA), _prev(TB, H, 1024, C_GLUA), _row(TB, 1024, C_GLUB),
                  _prev(TB, H, 1024, C_GLUB), _row(TB, 1024, C_CG), _full((K_CONF, 1024))] + [_full((1, 1024))] * 5,
        out_specs=[_row(TB, 1024), _row(TB, 1024, 1), _colt(1024, TB, 1)],
        out_shape=[S((T, 1024), F32), S((T, 2048), BF16), S((2048, T), BF16)],
        input_output_aliases={0: 1, 1: 2},
        scratch_shapes=[pltpu.VMEM((H + TB + 8, LANES), F32), pltpu.VMEM((8, H + TB, LANES), F32)],
        compiler_params=_params(("parallel",)))(mix, mixt, proj, proj, proj, proj, proj, w, cb, lg, lb, ba, bb)


def _d_mix_conf_bwd1(dout, w_out, u1, proj, lg, lb):
    T = u1.shape[0]

    def body(dout_ref, w_ref, u1_ref, cg_ref, lg_ref, lb_ref, dys_ref, du1_ref, dcg_ref, dg_ref, db_ref, dy_ref):
        @pl.when(pl.program_id(0) == 0)
        def _():
            dg_ref[...] = jnp.zeros_like(dg_ref)
            db_ref[...] = jnp.zeros_like(db_ref)

        dys_ref[...] = _dot(dout_ref[...], w_ref[0:D_SSM, :], _NT)
        dy_ref[...] = _dot(dout_ref[...], w_ref[D_SSM:, :], _NT)

        def rows(rs):
            xh, r = _ln_stats(u1_ref[rs, :])
            u2 = xh * lg_ref[...] + lb_ref[...]
            s2 = _sigmoid(u2)
            cg = cg_ref[rs, :]
            sc = _sigmoid(cg)
            dy = dy_ref[rs, :]
            dcg_ref[rs, :] = (dy * u2 * s2 * _dsilu(cg, sc)).astype(BF16)
            dv, dg, db = _ln_bwd(dy * cg * sc * _dsilu(u2, s2), xh, r, lg_ref[...])
            dg_ref[...] += dg
            db_ref[...] += db
            du1_ref[rs, :] = dv
        _row_loop(TB, rows)

    return pl.pallas_call(
        body, name="d_mix_conf_bwd1", grid=(T // TB,),
        in_specs=[_row(TB, D), _full((2 * D, D)), _row(TB, 1024), _row(TB, 1024, C_CG), _full((1, 1024)),
                  _full((1, 1024))],
        out_specs=[_row(TB, 1024), _row(TB, 1024), _row(TB, 1024, C_CG), _full((1, 1024)), _full((1, 1024))],
        out_shape=[S((T, 1024), F32), S((T, 1024), F32), S((T, N_MAIN), BF16), S((1, 1024), F32), S((1, 1024), F32)],
        scratch_shapes=[pltpu.VMEM((TB, D_CONF), F32)],
        compiler_params=_params(("arbitrary",)))(dout, w_out, u1, proj, lg, lb)


def _conf_bwd2(dproj, proj, du1, w, ba, bb):
    T = du1.shape[0]
    nt = T // TB
    H = HALO_CONF

    def body(dproj_ref, ga_ref, gap_ref, gb_ref, gbp_ref, du_ref, dun_ref, w_ref, ba_ref, bb_ref,
             dg_ref, dw_ref, dcb_ref, dba_ref, dbb_ref, ext, dext, ext8, dext8, dwacc):
        i = pl.program_id(0)
        first, last = i == 0, i == nt - 1

        @pl.when(first)
        def _():
            for r in (dcb_ref, dba_ref, dbb_ref, dwacc):
                r[...] = jnp.zeros_like(r)

        ext[H + TB:, :] = jnp.zeros((8, LANES), F32)
        dext[H + TB:, :] = jnp.zeros((8, LANES), F32)

        def blk(cols):
            cols_b = pl.ds(pl.multiple_of(cols.start + D_CONF, LANES), LANES)
            up = (gap_ref[:, cols] + ba_ref[:, cols]) * _sigmoid(gbp_ref[:, cols] + bb_ref[:, cols])
            ext[0:H, :] = jnp.where(first, 0.0, up)
            a = ga_ref[:, cols] + ba_ref[:, cols]
            sb = _sigmoid(gb_ref[:, cols] + bb_ref[:, cols])
            ext[H:H + TB, :] = a * sb
            du = du_ref[:, cols]
            dext[0:TB, :] = du
            dext[TB:TB + H, :] = jnp.where(last, 0.0, dun_ref[:, cols])
            _shifted_copies(ext, ext8)
            _shifted_copies(dext, dext8)
            dcb_ref[:, cols] += jnp.sum(du, axis=0, keepdims=True)
            for r0 in range(0, TB, 64):
                dur = du_ref[pl.ds(r0, 64), cols]
                acc = jnp.zeros((64, LANES), F32)
                for k in range(K_CONF):
                    prod = dur * _shifted(ext8, r0 + H - (K_CONF - 1) + k, 64)
                    dwacc[k * 8:(k + 1) * 8, cols] += prod.reshape(8, 8, LANES).sum(axis=0)
                    acc = acc + w_ref[k:k + 1, cols] * _shifted(dext8, r0 + K_CONF - 1 - k, 64)
                ar, sr = a[r0:r0 + 64], sb[r0:r0 + 64]
                da = acc * sr
                dbv = acc * ar * sr * (1.0 - sr)
                dg_ref[pl.ds(r0, 64), cols] = da.astype(BF16)
                dg_ref[pl.ds(r0, 64), cols_b] = dbv.astype(BF16)
                dba_ref[:, cols] += jnp.sum(da, axis=0, keepdims=True)
                dbb_ref[:, cols] += jnp.sum(dbv, axis=0, keepdims=True)
        _col_loop(D_CONF, blk)

        @pl.when(last)
        def _():
            dw_ref[...] = jnp.sum(dwacc[...].reshape(K_CONF, 8, D_CONF), axis=1)

    return pl.pallas_call(
        body, name="conf_bwd2", grid=(nt,),
        in_specs=[_ANY, _row(TB, 1024, C_GLUA), _prev(TB, H, 1024, C_GLUA), _row(TB, 1024, C_GLUB),
                  _prev(TB, H, 1024, C_GLUB), _row(TB, 1024), _next(TB, H, 1024, nt), _full((K_CONF, 1024)),
                  _full((1, 1024)), _full((1, 1024))],
        out_specs=[_row(TB, 2048), _full((K_CONF, 1024)), _full((1, 1024)), _full((1, 1024)), _full((1, 1024))],
        out_shape=[S(dproj.shape, BF16), S((K_CONF, 1024), F32)] + [S((1, 1024), F32)] * 3,
        input_output_aliases={0: 0},
        scratch_shapes=[pltpu.VMEM((H + TB + 8, LANES), F32), pltpu.VMEM((TB + H + 8, LANES), F32),
                        pltpu.VMEM((8, H + TB, LANES), F32), pltpu.VMEM((8, TB + H, LANES), F32),
                        pltpu.VMEM((K_CONF * 8, D_CONF), F32)],
        compiler_params=_params(("arbitrary",)))(dproj, proj, proj, proj, proj, du1, du1, w, ba, bb)


def _mesh_pos():
    x, y, c = lax.axis_index("x"), lax.axis_index("y"), lax.axis_index("c")
    return x, y, c, 4 * x + 2 * y + c


def _peer(x, y, c, k):
    return (x ^ ((k >> 2) & 1), y ^ ((k >> 1) & 1), c ^ (k & 1))


def _exchange_copies(ins, outs, kinds, send, recv, loc):
    nb = len(ins)
    x, y, c, me = _mesh_pos()
    src = lambda b, d: ins[b].at[d] if kinds[b] == "blocks" else ins[b]
    copies = [pltpu.make_async_copy(src(b, me), outs[b].at[me], loc.at[b]) for b in range(nb)]
    for k in range(1, N_DEV):
        px, py, pc = _peer(x, y, c, k)
        for b in range(nb):
            s = (k - 1) * nb + b
            copies.append(pltpu.make_async_remote_copy(
                src_ref=src(b, 4 * px + 2 * py + pc), dst_ref=outs[b].at[me], send_sem=send.at[s], recv_sem=recv.at[s],
                device_id=(px, py, pc), device_id_type=pl.DeviceIdType.MESH))
    return copies


def _exchange_shapes(bufs, kinds):
    return [S(b.shape if kd == "blocks" else (N_DEV,) + b.shape, b.dtype) for b, kd in zip(bufs, kinds)]


def _exchange_sems(nb):
    n = (N_DEV - 1) * nb
    return [pltpu.SemaphoreType.DMA((n,)), pltpu.SemaphoreType.DMA((n,)), pltpu.SemaphoreType.DMA((nb,))]


def _two_level_gather(ins, outs, send, recv, loc):
    nb = len(ins)
    x, y, c, me = _mesh_pos()
    here, sibling = (x, y, c), (x, y, 1 - c)
    chips = [(1 - x, y), (x, 1 - y), (1 - x, 1 - y)]

    def copy(slot, b, block, to, src=None):
        d = 4 * block[0] + 2 * block[1] + block[2]
        return pltpu.make_async_remote_copy(
            src_ref=outs[b].at[d] if src is None else src, dst_ref=outs[b].at[d],
            send_sem=send.at[slot * nb + b], recv_sem=recv.at[slot * nb + b],
            device_id=to, device_id_type=pl.DeviceIdType.MESH)

    mine = [pltpu.make_async_copy(ins[b], outs[b].at[me], loc.at[b]) for b in range(nb)]
    first = [copy(0, b, here, sibling, src=ins[b]) for b in range(nb)]
    first += [copy(1 + j, b, here, (*chip, c), src=ins[b]) for j, chip in enumerate(chips) for b in range(nb)]

    def start():
        for cp in mine + first:
            cp.start()

    def finish():
        passed = []
        for j, chip in enumerate(chips):
            for b in range(nb):
                copy(1 + j, b, (*chip, c), here).wait_recv()
            onward = [copy(4 + j, b, (*chip, c), sibling) for b in range(nb)]
            for cp in onward:
                cp.start()
            passed += onward
        for b in range(nb):
            copy(0, b, sibling, here).wait_recv()
        for j, chip in enumerate(chips):
            for b in range(nb):
                copy(4 + j, b, (*chip, 1 - c), here).wait_recv()
        for cp in first + passed:
            cp.wait_send()
        for cp in mine:
            cp.wait()

    return start, finish


def _exchange_plan(ins, outs, kinds, send, recv, loc):
    if all(kd == "gather" for kd in kinds):
        return _two_level_gather(ins, outs, send, recv, loc)
    copies = _exchange_copies(ins, outs, kinds, send, recv, loc)

    def start():
        for cp in copies:
            cp.start()

    def finish():
        for cp in copies:
            cp.wait()

    return start, finish


def _exchange(bufs, kinds, name):
    nb = len(bufs)

    def body(*refs):
        start, finish = _exchange_plan(refs[:nb], refs[nb:2 * nb], kinds, *refs[2 * nb:])
        start()
        finish()

    return pl.pallas_call(
        body, name=name, in_specs=[_ANY] * nb, out_specs=[_ANY] * nb,
        out_shape=_exchange_shapes(bufs, kinds), scratch_shapes=_exchange_sems(nb))(*bufs)


def _sum_parts(p_ref):
    acc = p_ref[0].astype(F32)
    for d in range(1, N_DEV):
        acc = acc + p_ref[d].astype(F32)
    return acc


def _adamw_math(g, w, m, v):
    m = ADAM_B1 * m + (1.0 - ADAM_B1) * g
    v = ADAM_B2 * v + (1.0 - ADAM_B2) * (g * g)
    m_hat = m / (1.0 - ADAM_B1 ** ADAM_STEP)
    v_hat = v / (1.0 - ADAM_B2 ** ADAM_STEP)
    return -ADAM_LR * (m_hat / (jnp.sqrt(v_hat) + ADAM_EPS) + ADAM_WD * w), m, v


HEAD_ROWS = 256


def _sum8_adamw(parts, w, m, v, name, head=None):
    _, R, C = w.shape
    tb = HEAD_ROWS if R % HEAD_ROWS == 0 else R
    nb = R // tb
    assert head is None or (tb == HEAD_ROWS and head.shape[1] == HEAD_ROWS and parts.shape[1] == R - HEAD_ROWS)
    skip = 0 if head is None else 1

    def body(*refs):
        p_ref, w_ref, m_ref, v_ref, g_ref, d_ref, mo_ref, vo_ref = refs[skip:]
        g = _sum_parts(p_ref)
        if head is not None:
            g = jnp.where(pl.program_id(0) == nb - 1, _sum_parts(refs[0]), g)
        g_ref[0] = g
        d_ref[0], mo_ref[0], vo_ref[0] = _adamw_math(g, w_ref[0], m_ref[0], v_ref[0])

    first = [] if head is None else [pl.BlockSpec((N_DEV, tb, C), lambda i: (0, 0, 0))]
    own = pl.BlockSpec((1, tb, C), lambda i: (0, i, 0))
    last_part = parts.shape[1] // tb - 1
    return pl.pallas_call(
        body, name=name, grid=(nb,),
        in_specs=first + [pl.BlockSpec((N_DEV, tb, C), lambda i: (0, jnp.minimum(i, last_part), 0))] + [own] * 3,
        out_specs=[own] * 4, out_shape=[S((1, R, C), F32)] * 4,
        compiler_params=_params(("parallel",)))(*([] if head is None else [head]), parts, w, m, v)


SMALL_LAYOUT = (
    ("ln_emb_g", 0, 1024), ("ln_emb_b", 0, 1024), ("ssm_conv_b", 0, 1024), ("ssm_conv_b", 1024, 512),
    ("dt_bias", 0, N_HEADS), ("a_log", 0, N_HEADS), ("d_skip", 0, N_HEADS), ("ssm_norm_g", 0, 1024),
    ("b_glu", 0, 1024), ("b_glu", 1024, 1024), ("conf_conv_b", 0, 1024), ("conf_ln_g", 0, 1024),
    ("conf_ln_b", 0, 1024), ("b_out", 0, 1024), ("ln1_g", 0, 1024), ("ln1_b", 0, 1024), ("ln2_g", 0, 1024),
    ("ln2_b", 0, 1024))
SMALL_ROWS = 24
SMALL = tuple(dict.fromkeys(n for n, _, _ in SMALL_LAYOUT))


LOSS_ROW = len(SMALL_LAYOUT)


def _pack_small(rows, loss):
    def body(*refs):
        o_ref = refs[-1]
        o_ref[...] = jnp.zeros_like(o_ref)
        for r, ref in enumerate(refs[:-2]):
            o_ref[r:r + 1, 0:ref.shape[1]] = ref[...]
        o_ref[LOSS_ROW:LOSS_ROW + 1, 0:LANES] = refs[-2][0:1, :]

    return pl.pallas_call(body, name="pack_small", out_shape=S((SMALL_ROWS, 1024), F32))(*rows, loss)


def _small_update(parts, w, m, v):
    def body(*refs):
        p_ref = refs[0]
        ins = {n: refs[1 + 3 * i:4 + 3 * i] for i, n in enumerate(SMALL)}
        o0 = 1 + 3 * len(SMALL)
        outs = {n: refs[o0 + 4 * i:o0 + 4 * i + 4] for i, n in enumerate(SMALL)}
        gsum = refs[-1]
        gsum[...] = _sum_parts(p_ref)
        refs[-2][...] = gsum[LOSS_ROW:LOSS_ROW + 1, 0:LANES]
        for r, (n, off, wd) in enumerate(SMALL_LAYOUT):
            cs = slice(off, off + wd)
            g = gsum[r:r + 1, 0:wd]
            w_ref, m_ref, v_ref = ins[n]
            g_ref, d_ref, mo_ref, vo_ref = outs[n]
            g_ref[:, cs] = g
            d_ref[:, cs], mo_ref[:, cs], vo_ref[:, cs] = _adamw_math(g, w_ref[:, cs], m_ref[:, cs], v_ref[:, cs])

    args = [parts] + [a for n in SMALL for a in (w[n], m[n], v[n])]
    res = pl.pallas_call(
        body, name="small_update",
        out_shape=[S(w[n].shape, F32) for n in SMALL for _ in range(4)] + [S((1, LANES), F32)],
        scratch_shapes=[pltpu.VMEM((SMALL_ROWS, 1024), F32)])(*args)
    return tuple({n: res[4 * i + j] for i, n in enumerate(SMALL)} for j in range(4)) + (res[-1],)


EARLY = ("w_in", "ssm_conv_w", "conf_conv_w")
LATE = ("w_out", "w_ple_gate", "w_ple_proj")


def _local_step(x, p, tgt, W, shards=None):
    r1 = lambda v: v.reshape(1, -1).astype(F32)
    pad_l = lambda v: jnp.pad(r1(v), ((0, 0), (0, LANES - v.size)))
    late = None if shards is None else [shards[n] for n in LATE]
    if shards is None:
        h0, h0b, h0bt = _ln_emb_fwd(x, r1(W["ln_emb_g"]), r1(W["ln_emb_b"]))
    else:
        h0, h0b, h0bt, *gathered = _ln_emb_fwd(x, r1(W["ln_emb_g"]), r1(W["ln_emb_b"]),
                                               exchange=([shards[n] for n in EARLY], ("gather",) * len(EARLY)))
        W = dict(W, **{n: a if n == "w_in" else _unstack_shards(a, BY_COLS[n]) for n, a in zip(EARLY, gathered)})
    w_main, w_dt = _w_in_to_main(W["w_in"])
    scw, scb = W["ssm_conv_w"], r1(W["ssm_conv_b"])
    wx, wb, bx, bb = scw[:, :1024], scw[:, 1024:], scb[:, :1024], scb[:, 1024:]
    dt_bias, alog = pad_l(W["dt_bias"]), pad_l(W["a_log"])
    dskip_row = jnp.repeat(W["d_skip"].reshape(-1), HEAD).reshape(1, -1)
    norm_g = r1(W["ssm_norm_g"])
    bglu = r1(W["b_glu"])
    ba, bbg = bglu[:, :1024], bglu[:, 1024:]
    ccw, ccb, clg, clb = W["conf_conv_w"], r1(W["conf_conv_b"]), r1(W["conf_ln_g"]), r1(W["conf_ln_b"])

    if late is None:
        proj = _mm(h0b, w_main, "nn", "in_proj", tm=TM_IN_PROJ)
    else:
        proj, *gathered = _mm(h0b, w_main, "nn", "in_proj", tm=TM_IN_PROJ, exchange=(late, ("gather",) * len(LATE)))
        W = dict(W, **{n: _unstack_shards(a, BY_COLS[n]) for n, a in zip(LATE, gathered)})
    xs_c, bc_c, dt, dt_raw = _ssd_pre_fwd(proj, h0b, w_dt, wx, wb, bx, bb, dt_bias)
    mix, ypre, hprev, mixt = _ssd_fwd(xs_c, bc_c, dt, proj, alog, dskip_row, norm_g)
    u1, mix, mixt = _conf_fwd(mix, mixt, proj, ccw, ccb, clg, clb, ba, bbg)
    out, h1, h1b, h1bt = _out_proj_post1(mix, W["w_out"], h0, r1(W["b_out"]), r1(W["ln1_g"]), r1(W["ln1_b"]))
    pb = p.astype(BF16)
    dh1a, dgp, dple, loss, dln2g, dln2b = _ple_post2(h1b, W["w_ple_gate"], pb, W["w_ple_proj"], h1, tgt,
                                                      r1(W["ln2_g"]), r1(W["ln2_b"]))

    g = {}
    g["w_ple_proj"] = _mm(pb.T, dple, "nn", "d_ple_proj", out_dtype=BF16)
    g["w_ple_gate"] = _mm(h1bt, dgp, "nn", "d_ple_gate", out_dtype=BF16)
    dout, dh0a, dln1g, dln1b, dbout = _d_h1_post1_bwd(dgp, W["w_ple_gate"], dh1a, h0, out, r1(W["b_out"]),
                                                      r1(W["ln1_g"]))
    g["w_out"] = _mm(mixt, dout, "nn", "d_w_out", out_dtype=BF16)
    dmix, du1, dproj, dclg, dclb = _d_mix_conf_bwd1(dout, W["w_out"], u1, proj, clg, clb)
    dproj, g["conf_conv_w"], dccb, dba, dbb = _conf_bwd2(dproj, proj, du1, ccw, ba, bbg)
    stack = lambda names: [_stack_shards(g[n], BY_COLS[n]) for n in names]
    dxs_c, dbc_c, ddtr, dproj, dng, ddsk, dalog, ddtb, *recv_a = _ssd_bwd(
        dproj, xs_c, bc_c, dt, dt_raw, dt_bias, proj, ypre, hprev, dmix, alog, dskip_row, norm_g,
        exchange=None if late is None else (stack(LATE), ("blocks",) * len(LATE)))
    dproj, dwx, dbx = _ssd_conv_bwd(dproj, proj, dxs_c, wx, bx, 1024, C_XS, "ssd_conv_bwd_x")
    dproj, dwb, dbb2 = _ssd_conv_bwd(dproj, proj, dbc_c, wb, bb, 512, C_BC, "ssd_conv_bwd_bc")
    g["ssm_conv_w"] = jnp.concatenate([dwx, dwb], axis=1)
    dw_dt = _mm(h0bt, ddtr, "nn", "d_w_dt", out_dtype=BF16)
    last_args = (dproj, w_main, ddtr, w_dt, dh0a, x, r1(W["ln_emb_g"]))
    if late is None:
        g["w_in"] = _w_in_blocks(_mm(h0bt, dproj, "nn", "d_w_in", out_dtype=BF16), dw_dt)
        grad_x, dlng, dlnb = _d_h0_ln_bwd(*last_args)
    else:
        r0 = D - HEAD_ROWS
        head = _w_in_blocks(_mm(h0bt, dproj, "nn", "d_w_in_head", out_dtype=BF16, tk=x.shape[0],
                                a_rows=(r0, HEAD_ROWS)), dw_dt[r0:])
        dw_rest, recv_head = _mm(h0bt, dproj, "nn", "d_w_in", out_dtype=BF16, a_rows=(0, r0),
                                 exchange=([head], ("blocks",)))
        last = ("ssm_conv_w", "conf_conv_w")
        grad_x, dlng, dlnb, *recv_b = _d_h0_ln_bwd(
            *last_args, exchange=([_w_in_blocks(dw_rest, dw_dt[:r0])] + stack(last), ("blocks",) * 3))
        g["recv"] = dict(zip(LATE + ("w_in",) + last, recv_a + recv_b), w_in_head=recv_head)
    g["rows"] = [dlng, dlnb, dbx, dbb2, ddtb, dalog, ddsk, dng, dba, dbb, dccb, dclg, dclb, dbout, dln1g, dln1b,
                 dln2g, dln2b]
    return loss, grad_x, g


W_IN_SEGMENTS = ((0, 2048, 2048), (2048, 5120, 512), (2560, None, N_HEADS), (2576, 0, 2048), (4624, 4096, 1024))


def _w_in_to_main(shards):
    def pieces(p0, width):
        out, p = [], p0
        while p < p0 + width:
            d = p // COLS_PER_DEV
            hi = min(p0 + width, (d + 1) * COLS_PER_DEV)
            out.append(shards[d][:, p - d * COLS_PER_DEV:hi - d * COLS_PER_DEV])
            p = hi
        return out
    main = [s for s in sorted(W_IN_SEGMENTS, key=lambda s: -1 if s[1] is None else s[1]) if s[1] is not None]
    w_main = jnp.concatenate([q for p0, _, width in main for q in pieces(p0, width)], axis=1)
    w_dt = jnp.concatenate(pieces(2560, N_HEADS), axis=1)
    return w_main, jnp.pad(w_dt, ((0, 0), (0, LANES - N_HEADS)))


def _w_in_blocks(dw_main, dw_dt):
    blocks = []
    for d in range(N_DEV):
        lo_d, hi_d = d * COLS_PER_DEV, (d + 1) * COLS_PER_DEV
        parts = []
        for p0, m0, width in W_IN_SEGMENTS:
            lo, hi = max(lo_d, p0), min(hi_d, p0 + width)
            if lo < hi:
                parts.append(dw_dt[:, lo - p0:hi - p0] if m0 is None else dw_main[:, m0 + lo - p0:m0 + hi - p0])
        blocks.append(jnp.concatenate(parts, axis=1))
    return jnp.stack(blocks)


WEIGHTS = ['ln_emb_g', 'ln_emb_b', 'w_in', 'ssm_conv_w', 'ssm_conv_b', 'dt_bias', 'a_log', 'd_skip', 'ssm_norm_g',
           'b_glu', 'conf_conv_w', 'conf_conv_b', 'conf_ln_g', 'conf_ln_b', 'w_out', 'b_out', 'ln1_g', 'ln1_b',
           'w_ple_gate', 'w_ple_proj', 'ln2_g', 'ln2_b']
SHARDED = (("w_in", True), ("w_out", False), ("w_ple_gate", False), ("w_ple_proj", True), ("ssm_conv_w", True),
           ("conf_conv_w", True))
BY_COLS = dict(SHARDED)


def _stack_shards(a, by_cols):
    if by_cols:
        return a.reshape(a.shape[0], N_DEV, a.shape[1] // N_DEV).transpose(1, 0, 2)
    return a.reshape(N_DEV, a.shape[0] // N_DEV, a.shape[1])


def _unstack_shards(a, by_cols):
    if by_cols:
        return a.transpose(1, 0, 2).reshape(a.shape[1], N_DEV * a.shape[2])
    return a.reshape(N_DEV * a.shape[1], a.shape[2])


def kernel(x, p, ln_emb_g, ln_emb_b, w_in, ssm_conv_w, ssm_conv_b, dt_bias, a_log, d_skip, ssm_norm_g, b_glu, conf_conv_w, conf_conv_b, conf_ln_g, conf_ln_b, w_out, b_out, ln1_g, ln1_b, w_ple_gate, w_ple_proj, ln2_g, ln2_b, loss_target, m_ln_emb_g, m_ln_emb_b, m_w_in, m_ssm_conv_w, m_ssm_conv_b, m_dt_bias, m_a_log, m_d_skip, m_ssm_norm_g, m_b_glu, m_conf_conv_w, m_conf_conv_b, m_conf_ln_g, m_conf_ln_b, m_w_out, m_b_out, m_ln1_g, m_ln1_b, m_w_ple_gate, m_w_ple_proj, m_ln2_g, m_ln2_b, v_ln_emb_g, v_ln_emb_b, v_w_in, v_ssm_conv_w, v_ssm_conv_b, v_dt_bias, v_a_log, v_d_skip, v_ssm_norm_g, v_b_glu, v_conf_conv_w, v_conf_conv_b, v_conf_ln_g, v_conf_ln_b, v_w_out, v_b_out, v_ln1_g, v_ln1_b, v_w_ple_gate, v_w_ple_proj, v_ln2_g, v_ln2_b):
    loc = dict(locals())
    w = {n: loc[n] for n in WEIGHTS}
    m = {n: loc["m_" + n] for n in WEIGHTS}
    v = {n: loc["v_" + n] for n in WEIGHTS}
    sharded = [n for n, _ in SHARDED]

    shards = {n: w[n][0].astype(BF16) if n.startswith("w_") else w[n][0] for n in sharded}
    W = {n: w[n].reshape(-1) for n in SMALL}
    loss, grad_x, g = _local_step(x[0], p[0, 0], loss_target[0], W, shards=shards)
    (recv_small,) = _exchange([_pack_small(g["rows"], loss)], ("all",), "small_exchange")

    grads, delta, new_m, new_v = {}, {}, {}, {}
    for n in sharded:
        grads[n], delta[n], new_m[n], new_v[n] = _sum8_adamw(
            g["recv"][n], w[n], m[n], v[n], "adamw_" + n, head=g["recv"]["w_in_head"] if n == "w_in" else None)
    two_d = lambda d: {n: d[n].reshape(1, -1) for n in SMALL}
    *small, loss = _small_update(recv_small, two_d(w), two_d(m), two_d(v))
    for dst, res in zip((grads, delta, new_m, new_v), small):
        for n in SMALL:
            dst[n] = res[n].reshape(w[n].shape)
    return (loss[0, 0], grad_x[None], *[grads[n] for n in WEIGHTS], *[delta[n] for n in WEIGHTS],
            *[new_m[n] for n in WEIGHTS], *[new_v[n] for n in WEIGHTS])
```

```python
import numpy as np
import jax
import jax.numpy as jnp
from jax import lax
from jax.experimental import pallas as pl
from jax.experimental.pallas import tpu as pltpu

F32, BF16 = jnp.float32, jnp.bfloat16
S = jax.ShapeDtypeStruct

N_DEV = 8
D = 1024
D_PLE = 256
D_SSM = 1024
D_CONF = 1024
N_HEADS = 16
HEAD = 64
N_STATE = 128
CHUNK = 128
K_SSM = 4
K_CONF = 31
D_IN = 5648
COLS_PER_DEV = D_IN // N_DEV
LN_EPS = 1e-5
RMS_EPS = 1e-5
ALPHA = 2.0 ** 0.25
LANES = 128
TB = 512
TB_SSD_CONV_BWD = 512
TM_IN_PROJ = 2048
TM_FUSED = 1024
RG = 32
ROW_UNROLL = 8
HALO_SSM = 8
HALO_CONF = 32
VMEM_LIMIT = 56 * 1024 * 1024

ADAM_LR, ADAM_B1, ADAM_B2, ADAM_EPS, ADAM_WD, ADAM_STEP = 0.001, 0.9, 0.999, 1e-08, 0.01, 10

C_GLUA, C_GLUB, C_XS, C_Z, C_CG = 0, 1, 2, 3, 4
C_BC = 10
N_MAIN = 5632


def _params(sem, vmem=VMEM_LIMIT):
    return pltpu.CompilerParams(dimension_semantics=sem, vmem_limit_bytes=vmem)


def _row(tb, n, col=0):
    return pl.BlockSpec((tb, n), lambda i: (i, col))


def _colt(n, tb, row=0):
    return pl.BlockSpec((n, tb), lambda i: (row, i))


def _full(shape):
    return pl.BlockSpec(shape, lambda i: (0,) * len(shape))


_ANY = pl.BlockSpec(memory_space=pl.ANY)


def _prev(tb, halo, n, col=0):
    r = tb // halo
    return pl.BlockSpec((halo, n), lambda i: (jnp.maximum(i * r - 1, 0), col))


def _next(tb, halo, n, nt, col=0):
    r = tb // halo
    return pl.BlockSpec((halo, n), lambda i: (jnp.minimum((i + 1) * r, nt * r - 1), col))


def _row_loop(tb, fn):
    def it(r, c):
        fn(pl.ds(pl.multiple_of(r * RG, RG), RG))
        return c
    lax.fori_loop(0, tb // RG, it, 0, unroll=ROW_UNROLL)


def _col_loop(n, fn):
    def it(j, c):
        fn(pl.ds(pl.multiple_of(j * LANES, LANES), LANES))
        return c
    lax.fori_loop(0, n // LANES, it, 0)


def _sigmoid(x):
    return 1.0 / (1.0 + jnp.exp(-x))


def _dsilu(x, s):
    return s * (1.0 + x * (1.0 - s))


def _ln_stats(v):
    mu = jnp.mean(v, axis=-1, keepdims=True)
    c = v - mu
    r = lax.rsqrt(jnp.mean(c * c, axis=-1, keepdims=True) + LN_EPS)
    return c * r, r


def _ln_bwd(dy, xhat, r, g):
    dxh = dy * g
    dv = r * (dxh - jnp.mean(dxh, axis=-1, keepdims=True) - xhat * jnp.mean(dxh * xhat, axis=-1, keepdims=True))
    return dv, jnp.sum(dy * xhat, axis=0, keepdims=True), jnp.sum(dy, axis=0, keepdims=True)


def _dot(a, b, dims=((1,), (0,))):
    return lax.dot_general(a.astype(BF16), b.astype(BF16), (dims, ((), ())), preferred_element_type=F32)


_NT = ((1,), (1,))
_TN = ((0,), (0,))


def _split3(x):
    hi = x.astype(BF16)
    r = x - hi.astype(F32)
    mid = r.astype(BF16)
    return hi, mid, (r - mid.astype(F32)).astype(BF16)


def _dot_sel_b(a, b, dims=((1,), (0,))):
    hi, mid, lo = _split3(a)
    return (_dot(lo, b, dims) + _dot(mid, b, dims)) + _dot(hi, b, dims)


def _dot_sel_a(a, b, dims=((1,), (0,))):
    hi, mid, lo = _split3(b)
    return (_dot(a, lo, dims) + _dot(a, mid, dims)) + _dot(a, hi, dims)


def _mm(a, b, mode, name, out_dtype=F32, add=None, tm=1024, tn=None, tk=1024, exchange=None, a_rows=None):
    assert mode in ("nn", "nt")
    (M, K), N = a.shape, b.shape[1 if mode == "nn" else 0]
    row0 = 0
    if a_rows is not None:
        row0, M = a_rows
        tm = M
        assert row0 % M == 0
    if tn is None:
        tn = next(t for t in (1024, 1408, 512, 256, LANES) if N % t == 0)
    tm, tn, tk = min(tm, M), min(tn, N), min(tk, K)
    assert M % tm == 0 and N % tn == 0 and K % tk == 0, (name, M, N, K)
    grid = (M // tm, N // tn, K // tk)
    nk = grid[2]
    dims = ((1,), (0,)) if mode == "nn" else _NT
    n_in = 2 + (add is not None)
    xbufs, kinds = exchange if exchange is not None else ((), ())
    nx = len(xbufs)

    def body(*refs):
        a_ref, b_ref = refs[:2]
        o_ref = refs[n_in + nx]
        acc = refs[n_in + 2 * nx + 1]
        i, j, k = pl.program_id(0), pl.program_id(1), pl.program_id(2)
        if nx:
            start, finish = _exchange_plan(refs[n_in:n_in + nx], refs[n_in + nx + 1:n_in + 2 * nx + 1], kinds,
                                           *refs[n_in + 2 * nx + 2:])
            pl.when((i == 0) & (j == 0) & (k == 0))(start)

        d = _dot(a_ref[...], b_ref[...], dims)

        def write_out(r):
            if add is not None:
                r = r + refs[2][...]
            o_ref[...] = r.astype(out_dtype)

        if nk == 1:
            write_out(d)
        else:
            @pl.when(k == 0)
            def _():
                acc[...] = d

            @pl.when((k > 0) & (k < nk - 1))
            def _():
                acc[...] += d

            @pl.when(k == nk - 1)
            def _():
                write_out(acc[...] + d)

        if nx:
            pl.when((i == grid[0] - 1) & (j == grid[1] - 1) & (k == nk - 1))(finish)

    a_spec = pl.BlockSpec((tm, tk), lambda i, j, k: (i + row0 // tm, k))
    b_spec = pl.BlockSpec((tn, tk), lambda i, j, k: (j, k)) if mode == "nt" else pl.BlockSpec((tk, tn), lambda i, j, k: (k, j))
    o_spec = pl.BlockSpec((tm, tn), lambda i, j, k: (i, j))
    ins, specs = [a, b], [a_spec, b_spec]
    if add is not None:
        ins.append(add)
        specs.append(o_spec)
    acc_spec = pltpu.VMEM((tm, tn) if nk > 1 else (8, LANES), F32)
    if not nx:
        return pl.pallas_call(
            body, name=name, grid=grid, in_specs=specs, out_specs=o_spec,
            out_shape=S((M, N), out_dtype), scratch_shapes=[acc_spec],
            compiler_params=_params(("parallel", "parallel", "arbitrary")))(*ins)
    return pl.pallas_call(
        body, name=name, grid=grid, in_specs=specs + [_ANY] * nx, out_specs=[o_spec] + [_ANY] * nx,
        out_shape=[S((M, N), out_dtype)] + _exchange_shapes(xbufs, kinds),
        scratch_shapes=[acc_spec] + _exchange_sems(nx),
        compiler_params=_params(("arbitrary", "arbitrary", "arbitrary")))(*ins, *xbufs)


def _ln_emb_fwd(x, g, b, exchange=None):
    T = x.shape[0]

    nt = T // TB
    xbufs, kinds = exchange if exchange is not None else ((), ())
    nx = len(xbufs)

    def body(*refs):
        x_ref, g_ref, b_ref = refs[:3]
        h_ref, hb_ref, hbt_ref = refs[3 + nx:6 + nx]
        i = pl.program_id(0)
        if nx:
            start, finish = _exchange_plan(refs[3:3 + nx], refs[6 + nx:6 + 2 * nx], kinds, *refs[6 + 2 * nx:])
            pl.when(i == 0)(start)

        def rows(rs):
            xh, _ = _ln_stats(x_ref[rs, :])
            h = xh * g_ref[...] + b_ref[...]
            h_ref[rs, :] = h
            hb_ref[rs, :] = h.astype(BF16)
        _row_loop(TB, rows)
        hbt_ref[...] = hb_ref[...].T
        if nx:
            pl.when(i == nt - 1)(finish)

    return pl.pallas_call(
        body, name="ln_emb_fwd", grid=(nt,),
        in_specs=[_row(TB, D), _full((1, D)), _full((1, D))] + [_ANY] * nx,
        out_specs=[_row(TB, D), _row(TB, D), _colt(D, TB)] + [_ANY] * nx,
        out_shape=[S((T, D), F32), S((T, D), BF16), S((D, T), BF16)] + _exchange_shapes(xbufs, kinds),
        scratch_shapes=_exchange_sems(nx) if nx else [],
        compiler_params=_params(("arbitrary",)))(x, g, b, *xbufs)


def _out_proj_post1(mix, w_out, h0, b_out, g, b, tm=TM_FUSED, tk=1024):
    T, K = mix.shape
    tm = min(tm, T)
    nk = K // tk
    assert T % tm == 0 and K % tk == 0 and nk >= 2

    def body(mix_ref, w_ref, h0_ref, bo_ref, g_ref, b_ref, out_ref, h_ref, hb_ref, hbt_ref, acc):
        k = pl.program_id(1)
        d = _dot(mix_ref[...], w_ref[...])

        @pl.when(k == 0)
        def _():
            acc[...] = d

        @pl.when((k > 0) & (k < nk - 1))
        def _():
            acc[...] += d

        @pl.when(k == nk - 1)
        def _():
            out_ref[...] = acc[...] + d

            def rows(rs):
                xh, _ = _ln_stats(ALPHA * h0_ref[rs, :] + out_ref[rs, :] + bo_ref[...])
                h = xh * g_ref[...] + b_ref[...]
                h_ref[rs, :] = h
                hb_ref[rs, :] = h.astype(BF16)
            _row_loop(tm, rows)
            hbt_ref[...] = hb_ref[...].T

    rowt = lambda n: pl.BlockSpec((tm, n), lambda i, k: (i, 0))
    const = pl.BlockSpec((1, D), lambda i, k: (0, 0))
    return pl.pallas_call(
        body, name="out_proj_post1", grid=(T // tm, nk),
        in_specs=[pl.BlockSpec((tm, tk), lambda i, k: (i, k)), pl.BlockSpec((tk, D), lambda i, k: (k, 0)), rowt(D),
                  const, const, const],
        out_specs=[rowt(D), rowt(D), rowt(D), pl.BlockSpec((D, tm), lambda i, k: (0, i))],
        out_shape=[S((T, D), F32), S((T, D), F32), S((T, D), BF16), S((D, T), BF16)],
        scratch_shapes=[pltpu.VMEM((tm, D), F32)],
        compiler_params=_params(("parallel", "arbitrary")))(mix, w_out, h0, b_out, g, b)


def _ple_post2(h1b, w_gate, p, w_proj, h1, tgt, g, b, tm=TM_FUSED // 2):
    T = h1.shape[0]
    tm = min(tm, T)
    assert T % tm == 0

    def body(h1b_ref, wg_ref, p_ref, wp_ref, h1_ref, tgt_ref, g_ref, b_ref,
             dh1_ref, dgp_ref, dple_ref, loss_ref, dg_ref, db_ref, pt_ref, gp_ref, ple_ref):
        @pl.when(pl.program_id(0) == 0)
        def _():
            loss_ref[...] = jnp.zeros_like(loss_ref)
            dg_ref[...] = jnp.zeros_like(dg_ref)
            db_ref[...] = jnp.zeros_like(db_ref)

        pb = p_ref[...].astype(BF16)
        pt_ref[...] = pb.T
        gp_ref[...] = _dot(h1b_ref[...], wg_ref[...])
        ple_ref[...] = _dot(pb, wp_ref[...])

        def rows(rs):
            gate = _sigmoid(gp_ref[rs, :])
            ple = ple_ref[rs, :]
            xh, r = _ln_stats(ALPHA * h1_ref[rs, :] + gate * ple)
            err = xh * g_ref[...] + b_ref[...] - tgt_ref[rs, :]
            loss_ref[...] += 0.5 * jnp.sum(jnp.mean(err * err, axis=-1, keepdims=True), axis=0, keepdims=True)
            dv, dg, db = _ln_bwd(err * (1.0 / D), xh, r, g_ref[...])
            dg_ref[...] += dg
            db_ref[...] += db
            dh1_ref[rs, :] = ALPHA * dv
            dgp_ref[rs, :] = (dv * ple * gate * (1.0 - gate)).astype(BF16)
            dple_ref[rs, :] = (dv * gate).astype(BF16)
        _row_loop(tm, rows)

    return pl.pallas_call(
        body, name="ple_post2", grid=(T // tm,),
        in_specs=[_row(tm, D), _full((D, D)), _row(tm, D_PLE), _full((D_PLE, D)), _row(tm, D), _row(tm, D),
                  _full((1, D)), _full((1, D))],
        out_specs=[_row(tm, D)] * 3 + [_full((8, LANES)), _full((1, D)), _full((1, D)), _colt(D_PLE, tm)],
        out_shape=[S((T, D), F32), S((T, D), BF16), S((T, D), BF16), S((8, LANES), F32), S((1, D), F32), S((1, D), F32),
                   S((D_PLE, T), BF16)],
        scratch_shapes=[pltpu.VMEM((tm, D), F32), pltpu.VMEM((tm, D), F32)],
        compiler_params=_params(("arbitrary",)))(h1b, w_gate, p, w_proj, h1, tgt, g, b)


def _d_h1_post1_bwd(dgp, w_gate, dh1a, h0, out, b_out, g, tm=TM_FUSED // 2):
    T = h0.shape[0]
    tm = min(tm, T)
    assert T % tm == 0

    def body(dgp_ref, wg_ref, da_ref, h0_ref, out_ref, bo_ref, g_ref, dout_ref, dh0_ref, dg_ref, db_ref, dbo_ref, dh1):
        @pl.when(pl.program_id(0) == 0)
        def _():
            dg_ref[...] = jnp.zeros_like(dg_ref)
            db_ref[...] = jnp.zeros_like(db_ref)
            dbo_ref[...] = jnp.zeros_like(dbo_ref)

        dh1[...] = da_ref[...] + _dot(dgp_ref[...], wg_ref[...], _NT)

        def rows(rs):
            xh, r = _ln_stats(ALPHA * h0_ref[rs, :] + out_ref[rs, :] + bo_ref[...])
            dv, dg, db = _ln_bwd(dh1[rs, :], xh, r, g_ref[...])
            dg_ref[...] += dg
            db_ref[...] += db
            dbo_ref[...] += jnp.sum(dv, axis=0, keepdims=True)
            dout_ref[rs, :] = dv.astype(BF16)
            dh0_ref[rs, :] = ALPHA * dv
        _row_loop(tm, rows)

    return pl.pallas_call(
        body, name="d_h1_post1_bwd", grid=(T // tm,),
        in_specs=[_row(tm, D), _full((D, D))] + [_row(tm, D)] * 3 + [_full((1, D))] * 2,
        out_specs=[_row(tm, D)] * 2 + [_full((1, D))] * 3,
        out_shape=[S((T, D), BF16), S((T, D), F32)] + [S((1, D), F32)] * 3,
        scratch_shapes=[pltpu.VMEM((tm, D), F32)],
        compiler_params=_params(("arbitrary",)))(dgp, w_gate, dh1a, h0, out, b_out, g)


def _d_h0_ln_bwd(dproj, w_main, ddtr, w_dt, dh0a, x, g, exchange=None, tm=1024, tk=1408):
    T, K = dproj.shape
    tm = min(tm, T)
    assert T % tm == 0 and K % tk == 0
    ni, nk = T // tm, K // tk
    xbufs, kinds = exchange if exchange is not None else ((), ())
    nx = len(xbufs)

    def body(*refs):
        dp_ref, w_ref, dt_ref, wdt_ref, da_ref, x_ref, g_ref = refs[:7]
        dx_ref, dg_ref, db_ref = refs[7 + nx:10 + nx]
        acc = refs[10 + 2 * nx]
        i, k = pl.program_id(0), pl.program_id(1)
        if nx:
            start, finish = _exchange_plan(refs[7:7 + nx], refs[10 + nx:10 + 2 * nx], kinds, *refs[11 + 2 * nx:])
            pl.when((i == 0) & (k == 0))(start)

        @pl.when((i == 0) & (k == 0))
        def _():
            dg_ref[...] = jnp.zeros_like(dg_ref)
            db_ref[...] = jnp.zeros_like(db_ref)

        d = _dot(dp_ref[...], w_ref[...], _NT)

        @pl.when(k == 0)
        def _():
            acc[...] = da_ref[...] + _dot(dt_ref[...], wdt_ref[...], _NT) + d

        @pl.when(k > 0)
        def _():
            acc[...] += d

        @pl.when(k == nk - 1)
        def _():
            def rows(rs):
                xh, r = _ln_stats(x_ref[rs, :])
                dv, dg, db = _ln_bwd(acc[rs, :], xh, r, g_ref[...])
                dg_ref[...] += dg
                db_ref[...] += db
                dx_ref[rs, :] = dv
            _row_loop(tm, rows)

        if nx:
            pl.when((i == ni - 1) & (k == nk - 1))(finish)

    rowt = lambda n: pl.BlockSpec((tm, n), lambda i, k: (i, 0))
    const = lambda shape: pl.BlockSpec(shape, lambda i, k: (0, 0))
    return pl.pallas_call(
        body, name="d_h0_ln_bwd", grid=(ni, nk),
        in_specs=[pl.BlockSpec((tm, tk), lambda i, k: (i, k)), pl.BlockSpec((D, tk), lambda i, k: (0, k)),
                  rowt(LANES), const((D, LANES)), rowt(D), rowt(D), const((1, D))] + [_ANY] * nx,
        out_specs=[rowt(D), const((1, D)), const((1, D))] + [_ANY] * nx,
        out_shape=[S((T, D), F32), S((1, D), F32), S((1, D), F32)] + _exchange_shapes(xbufs, kinds),
        scratch_shapes=[pltpu.VMEM((tm, D), F32)] + (_exchange_sems(nx) if nx else []),
        compiler_params=_params(("arbitrary", "arbitrary")))(dproj, w_main, ddtr, w_dt, dh0a, x, g, *xbufs)


def _softplus(x):
    return jnp.maximum(x, 0.0) + jnp.log1p(jnp.exp(-jnp.abs(x)))


def _ssd_pre_fwd(proj, h0b, w_dt, wx, wb, bx, bb, dt_bias):
    T = proj.shape[0]
    H = HALO_SSM

    def body(xs_ref, xsp_ref, bc_ref, bcp_ref, h0b_ref, wdt_ref, wx_ref, wb_ref, bx_ref, bb_ref, dtb_ref,
             xso_ref, bco_ref, dto_ref, dtr_ref, extx, extb):
        first = pl.program_id(0) == 0

        def conv(t_ref, p_ref, w_ref, b_ref, o_ref, ext, n):
            def blk(cols):
                ext[0:H, cols] = jnp.where(first, 0.0, p_ref[:, cols])
                ext[H:, cols] = t_ref[:, cols]
                for r0 in range(0, TB, 64):
                    acc = jnp.broadcast_to(b_ref[:, cols], (64, LANES))
                    for k in range(K_SSM):
                        acc = acc + w_ref[k:k + 1, cols] * ext[pl.ds(r0 + H - (K_SSM - 1) + k, 64), cols]
                    o_ref[pl.ds(r0, 64), cols] = acc * _sigmoid(acc)
            _col_loop(n, blk)

        conv(xs_ref, xsp_ref, wx_ref, bx_ref, xso_ref, extx, D_SSM)
        conv(bc_ref, bcp_ref, wb_ref, bb_ref, bco_ref, extb, 512)
        dtr_ref[...] = _dot(h0b_ref[...], wdt_ref[...])
        dto_ref[...] = _softplus(dtr_ref[...] + dtb_ref[...])

    return pl.pallas_call(
        body, name="ssd_pre_fwd", grid=(T // TB,),
        in_specs=[_row(TB, 1024, C_XS), _prev(TB, H, 1024, C_XS), _row(TB, 512, C_BC), _prev(TB, H, 512, C_BC),
                  _row(TB, D), _full((D, LANES)), _full((K_SSM, 1024)), _full((K_SSM, 512)), _full((1, 1024)),
                  _full((1, 512)), _full((1, LANES))],
        out_specs=[_row(TB, 1024), _row(TB, 512), _row(TB, LANES), _row(TB, LANES)],
        out_shape=[S((T, 1024), F32), S((T, 512), F32), S((T, LANES), F32), S((T, LANES), F32)],
        scratch_shapes=[pltpu.VMEM((H + TB, 1024), F32), pltpu.VMEM((H + TB, 512), F32)],
        compiler_params=_params(("parallel",)))(proj, proj, proj, proj, h0b, w_dt, wx, wb, bx, bb, dt_bias)


def _ssd_conv_bwd(dproj, proj, d_c, w, b, n, col, name):
    TB = TB_SSD_CONV_BWD
    T = proj.shape[0]
    nt = T // TB
    H = HALO_SSM
    R = TB + H

    def body(dproj_ref, t_ref, p_ref, n_ref, d_ref, dn_ref, w_ref, b_ref, o_ref, dw_ref, dbias_ref, ext, dp):
        i = pl.program_id(0)
        first, last = i == 0, i == nt - 1

        @pl.when(first)
        def _():
            dw_ref[...] = jnp.zeros_like(dw_ref)
            dbias_ref[...] = jnp.zeros_like(dbias_ref)

        def blk(cols):
            ext[0:H, cols] = jnp.where(first, 0.0, p_ref[:, cols])
            ext[H:H + TB, cols] = t_ref[:, cols]
            ext[H + TB:, cols] = n_ref[:, cols]
            def taps_and_dsilu(r0, rows):
                taps = [ext[pl.ds(r0 + H - (K_SSM - 1) + k, rows), cols] for k in range(K_SSM)]
                pre = jnp.broadcast_to(b_ref[:, cols], (rows, LANES))
                for k in range(K_SSM):
                    pre = pre + w_ref[k:k + 1, cols] * taps[k]
                return taps, _dsilu(pre, _sigmoid(pre))

            for r0 in range(0, TB, 64):
                taps, ds = taps_and_dsilu(r0, 64)
                dpt = d_ref[pl.ds(r0, 64), cols] * ds
                dp[pl.ds(r0, 64), cols] = dpt
                dbias_ref[:, cols] += jnp.sum(dpt, axis=0, keepdims=True)
                for k in range(K_SSM):
                    dw_ref[k:k + 1, cols] += jnp.sum(dpt * taps[k], axis=0, keepdims=True)
            dp[TB:, cols] = jnp.where(last, 0.0, dn_ref[:, cols] * taps_and_dsilu(TB, H)[1])
            for r0 in range(0, TB, 64):
                acc = jnp.zeros((64, LANES), F32)
                for k in range(K_SSM):
                    acc = acc + w_ref[k:k + 1, cols] * dp[pl.ds(r0 + K_SSM - 1 - k, 64), cols]
                o_ref[pl.ds(r0, 64), cols] = acc.astype(BF16)
        _col_loop(n, blk)

    return pl.pallas_call(
        body, name=name, grid=(nt,),
        in_specs=[_ANY, _row(TB, n, col), _prev(TB, H, n, col), _next(TB, H, n, nt, col),
                  _row(TB, n), _next(TB, H, n, nt), _full((K_SSM, n)), _full((1, n))],
        out_specs=[_row(TB, n, col), _full((K_SSM, n)), _full((1, n))],
        out_shape=[S(dproj.shape, BF16), S((K_SSM, n), F32), S((1, n), F32)],
        input_output_aliases={0: 0},
        scratch_shapes=[pltpu.VMEM((H + TB + H, n), F32), pltpu.VMEM((R, n), F32)],
        compiler_params=_params(("arbitrary",)))(dproj, proj, proj, proj, d_c, d_c, w, b)


def _ssd_consts():
    ex = np.zeros((LANES, D_SSM), np.float32)
    for h in range(N_HEADS):
        ex[h, h * HEAD:(h + 1) * HEAD] = 1.0
    tri = np.tril(np.ones((CHUNK, CHUNK), np.float32))
    return jnp.asarray(ex), jnp.asarray(ex.T.copy()), jnp.asarray(tri), jnp.asarray(tri.T.copy())


def _ssd_common(xs, dt, alog_ref, ex_ref, tri_ref):
    lane = lax.broadcasted_iota(jnp.int32, (1, LANES), 1)
    a = jnp.where(lane < N_HEADS, -jnp.exp(alog_ref[...]), 0.0)
    A = _dot_sel_a(tri_ref[...], dt * a)
    ex = ex_ref[...]
    Aex = _dot_sel_b(A, ex)
    dtex = _dot_sel_b(dt, ex)
    expA = jnp.exp(Aex)
    dec = jnp.exp(Aex[CHUNK - 1:CHUNK, :] - Aex)
    cd = _dot_sel_a(ex, jnp.broadcast_to(jnp.exp(A.T[:, CHUNK - 1:CHUNK]), (LANES, LANES)), _TN)
    return a, A, dtex, expA, dec, cd


def _decay_mask():
    sub = lax.broadcasted_iota(jnp.int32, (CHUNK, CHUNK), 0)
    lane = lax.broadcasted_iota(jnp.int32, (CHUNK, CHUNK), 1)
    return sub, lane, sub >= lane


def _ssd_fwd(xs_c, bc_c, dt, proj, alog, dskip_row, norm_g):
    T = xs_c.shape[0]
    nc = T // CHUNK
    ex, _, tri, _ = _ssd_consts()

    def body(xs_ref, bc_ref, dt_ref, z_ref, alog_ref, dsk_ref, ng_ref, ex_ref, tri_ref,
             ys_ref, ypre_ref, hprev_ref, yst_ref, Hs, ybuf):
        @pl.when(pl.program_id(0) == 0)
        def _():
            Hs[...] = jnp.zeros_like(Hs)

        hprev_ref[0] = Hs[...]
        xs, dt = xs_ref[...], dt_ref[...]
        a, A, dtex, expA, dec, cd = _ssd_common(xs, dt, alog_ref, ex_ref, tri_ref)
        AT = A.T
        xdt = xs * dtex
        xdec = xdt * dec
        _, _, causal = _decay_mask()
        for g in range(2):
            gs = slice(g * 512, (g + 1) * 512)
            B = bc_ref[:, g * N_STATE:(g + 1) * N_STATE]
            C = bc_ref[:, 256 + g * N_STATE:256 + (g + 1) * N_STATE]
            cb = _dot(C, B, _NT)
            Hg = Hs[gs, :]
            yoff = _dot(C, Hg, _NT) * expA[:, gs]
            for j in range(8):
                h = g * 8 + j
                hs = slice(h * HEAD, (h + 1) * HEAD)
                L = jnp.exp(jnp.where(causal, A[:, h:h + 1] - AT[h:h + 1, :], -1e30))
                ybuf[:, hs] = _dot(cb * L, xdt[:, hs]) + yoff[:, j * HEAD:(j + 1) * HEAD]
            Hs[gs, :] = cd[gs, :] * Hg + _dot(xdec[:, gs], B, _TN)
        ypre = ybuf[...] + dsk_ref[...] * xs
        ypre_ref[...] = ypre
        z = z_ref[...]
        yz = ypre * (z * _sigmoid(z))
        for g in range(2):
            gs = slice(g * 512, (g + 1) * 512)
            v = yz[:, gs]
            r = lax.rsqrt(jnp.mean(v * v, axis=-1, keepdims=True) + RMS_EPS)
            ys_ref[:, gs] = (v * r * ng_ref[:, gs]).astype(BF16)
        yst_ref[...] = ys_ref[...].T

    return pl.pallas_call(
        body, name="ssd_fwd", grid=(nc,),
        in_specs=[_row(CHUNK, 1024), _row(CHUNK, 512), _row(CHUNK, LANES), _row(CHUNK, 1024, C_Z),
                  _full((1, LANES)), _full((1, 1024)), _full((1, 1024)), _full((LANES, 1024)), _full((CHUNK, CHUNK))],
        out_specs=[_row(CHUNK, 1024), _row(CHUNK, 1024), pl.BlockSpec((1, 1024, N_STATE), lambda c: (c, 0, 0)),
                   _colt(1024, CHUNK)],
        out_shape=[S((T, 2048), BF16), S((T, 1024), F32), S((nc, 1024, N_STATE), F32), S((2048, T), BF16)],
        scratch_shapes=[pltpu.VMEM((1024, N_STATE), F32), pltpu.VMEM((CHUNK, 1024), F32)],
        compiler_params=_params(("arbitrary",)))(xs_c, bc_c, dt, proj, alog, dskip_row, norm_g, ex, tri)


def _ssd_bwd(dproj, xs_c, bc_c, dt, dt_raw, dt_bias, proj, ypre, hprev, dmix, alog, dskip_row, norm_g, exchange=None):
    T = xs_c.shape[0]
    nc = T // CHUNK
    ex, ext, tri, triu = _ssd_consts()
    rev = lambda n, col=0: pl.BlockSpec((CHUNK, n), lambda c: (nc - 1 - c, col))
    xbufs, kinds = exchange if exchange is not None else ((), ())
    nx = len(xbufs)
    N_IN, N_OUT = 17, 8

    def body(*refs):
        (dproj_ref, xs_ref, bc_ref, dt_ref, dtr_ref, dtb_ref, z_ref, ypre_ref, hprev_ref, dys_ref, alog_ref, dsk_ref,
         ng_ref, ex_ref, ext_ref, tri_ref, triu_ref) = refs[:N_IN]
        (dxs_ref, dbc_ref, ddt_ref, dz_ref, dng_ref, ddsk_ref, dalog_ref,
         ddtb_ref) = refs[N_IN + nx:N_IN + nx + N_OUT]
        dHs, dxbuf, dskacc = refs[N_IN + N_OUT + 2 * nx:N_IN + N_OUT + 2 * nx + 3]
        c = pl.program_id(0)
        if nx:
            start, finish = _exchange_plan(refs[N_IN:N_IN + nx], refs[N_IN + nx + N_OUT:N_IN + N_OUT + 2 * nx], kinds,
                                           *refs[N_IN + N_OUT + 2 * nx + 3:])
            pl.when(c == 0)(start)

        @pl.when(c == 0)
        def _():
            dHs[...] = jnp.zeros_like(dHs)
            dng_ref[...] = jnp.zeros_like(dng_ref)
            dalog_ref[...] = jnp.zeros_like(dalog_ref)
            ddtb_ref[...] = jnp.zeros_like(ddtb_ref)
            dskacc[...] = jnp.zeros_like(dskacc)

        xs, dt, z, ypre, dys = xs_ref[...], dt_ref[...], z_ref[...], ypre_ref[...], dys_ref[...]
        sg = _sigmoid(z)
        sz = z * sg
        yz = ypre * sz
        dyz_parts = []
        for g in range(2):
            gs = slice(g * 512, (g + 1) * 512)
            v = yz[:, gs]
            r = lax.rsqrt(jnp.mean(v * v, axis=-1, keepdims=True) + RMS_EPS)
            vn = v * r
            dng_ref[:, gs] += jnp.sum(dys[:, gs] * vn, axis=0, keepdims=True)
            dvn = dys[:, gs] * ng_ref[:, gs]
            dyz_parts.append(r * (dvn - vn * jnp.mean(dvn * vn, axis=-1, keepdims=True)))
        dyz = jnp.concatenate(dyz_parts, axis=1)
        dy = dyz * sz
        dz_ref[...] = (dyz * ypre * _dsilu(z, sg)).astype(BF16)
        dskacc[...] += jnp.sum(dy * xs, axis=0, keepdims=True)

        a, A, dtex, expA, dec, cd = _ssd_common(xs, dt, alog_ref, ex_ref, tri_ref)
        AT = A.T
        xdt = xs * dtex
        xdec = xdt * dec
        dye = dy * expA
        H = hprev_ref[0]
        dHn = dHs[...]
        sub, lane, causal = _decay_mask()
        dAc = jnp.zeros((CHUNK, LANES), F32)
        Rm = jnp.zeros((CHUNK, LANES), F32)
        yoff_parts, q_parts = [], []
        for g in range(2):
            gs = slice(g * 512, (g + 1) * 512)
            B = bc_ref[:, g * N_STATE:(g + 1) * N_STATE]
            C = bc_ref[:, 256 + g * N_STATE:256 + (g + 1) * N_STATE]
            cb = _dot(C, B, _NT)
            Hg, dHg = H[gs, :], dHn[gs, :]
            Q = _dot(B, dHg, _NT)
            yoff_parts.append(_dot(C, Hg, _NT) * expA[:, gs])
            q_parts.append(Q)
            dcb = jnp.zeros((CHUNK, CHUNK), F32)
            for j in range(8):
                h = g * 8 + j
                hs = slice(h * HEAD, (h + 1) * HEAD)
                L = jnp.exp(jnp.where(causal, A[:, h:h + 1] - AT[h:h + 1, :], -1e30))
                M = cb * L
                G = _dot(dy[:, hs], xdt[:, hs], _NT)
                dxbuf[:, hs] = _dot(M, dy[:, hs], _TN)
                dcb = dcb + G * L
                E = G * M
                dAc = jnp.where(lane == h, jnp.sum(E, axis=1, keepdims=True), dAc)
                Rm = jnp.where(sub == h, jnp.sum(E, axis=0, keepdims=True), Rm)
            dbc_ref[:, g * N_STATE:(g + 1) * N_STATE] = _dot(dcb, C, _TN) + _dot(xdec[:, gs], dHg)
            dbc_ref[:, 256 + g * N_STATE:256 + (g + 1) * N_STATE] = _dot(dcb, B) + _dot(dye[:, gs], Hg)
            dHs[gs, :] = cd[gs, :] * dHg + _dot(dye[:, gs], C, _TN)
        yoff = jnp.concatenate(yoff_parts, axis=1)
        Qd = jnp.concatenate(q_parts, axis=1) * dec
        dxdt = dxbuf[...] + Qd
        extm = ext_ref[...]
        red_s = _dot_sel_b(xdt * Qd, extm)
        dA = dAc - Rm.T + _dot_sel_b(dy * yoff, extm) - red_s
        hd = jnp.sum(_dot_sel_b(H * dHn, extm, _TN), axis=0, keepdims=True)
        last_add = jnp.sum(red_s, axis=0, keepdims=True) + jnp.exp(A[CHUNK - 1:CHUNK, :]) * hd
        dA = dA + jnp.where(sub == CHUNK - 1, last_add, 0.0)
        dadt = _dot_sel_a(triu_ref[...], dA)
        ddtr = (dadt * a + _dot_sel_b(dxdt * xs, extm)) * _sigmoid(dtr_ref[...] + dtb_ref[...])
        ddt_ref[...] = ddtr.astype(BF16)
        ddtb_ref[...] += jnp.sum(ddtr, axis=0, keepdims=True)
        dalog_ref[...] += jnp.sum(dadt * dt, axis=0, keepdims=True) * a
        dxs_ref[...] = dxdt * dtex + dsk_ref[...] * dy

        @pl.when(c == nc - 1)
        def _():
            ddsk_ref[...] = _dot_sel_b(jnp.broadcast_to(dskacc[...], (8, 1024)), extm)[0:1, :]

        if nx:
            pl.when(c == nc - 1)(finish)

    return pl.pallas_call(
        body, name="ssd_bwd", grid=(nc,),
        in_specs=[_ANY, rev(1024), rev(512), rev(LANES), rev(LANES), _full((1, LANES)), rev(1024, C_Z), rev(1024),
                  pl.BlockSpec((1, 1024, N_STATE), lambda c: (nc - 1 - c, 0, 0)), rev(1024, 0),
                  _full((1, LANES)), _full((1, 1024)), _full((1, 1024)),
                  _full((LANES, 1024)), _full((1024, LANES)), _full((CHUNK, CHUNK)), _full((CHUNK, CHUNK))] + [_ANY] * nx,
        out_specs=[rev(1024), rev(512), rev(LANES), rev(1024, C_Z), _full((1, 1024)), _full((1, LANES)),
                   _full((1, LANES)), _full((1, LANES))] + [_ANY] * nx,
        out_shape=[S((T, 1024), F32), S((T, 512), F32), S((T, LANES), BF16), S(dproj.shape, BF16),
                   S((1, 1024), F32), S((1, LANES), F32), S((1, LANES), F32), S((1, LANES), F32)]
        + _exchange_shapes(xbufs, kinds),
        input_output_aliases={0: 3},
        scratch_shapes=[pltpu.VMEM((1024, N_STATE), F32), pltpu.VMEM((CHUNK, 1024), F32), pltpu.VMEM((1, 1024), F32)]
        + (_exchange_sems(nx) if nx else []),
        compiler_params=_params(("arbitrary",)))(
            dproj, xs_c, bc_c, dt, dt_raw, dt_bias, proj, ypre, hprev, dmix, alog, dskip_row, norm_g, ex, ext, tri, triu,
            *xbufs)


def _shifted_copies(ext, ext8):
    n = ext8.shape[1]
    for r in range(8):
        ext8[r] = ext[pl.ds(r, n), :]


def _shifted(ext8, off, rows):
    return ext8[off % 8, pl.ds(off - off % 8, rows), :]


def _conf_fwd(mix, mixt, proj, w, cb, lg, lb, ba, bb):
    T = proj.shape[0]
    H = HALO_CONF

    def body(mix_ref, mixt_ref, ga_ref, gap_ref, gb_ref, gbp_ref, cg_ref, w_ref, cb_ref, lg_ref, lb_ref, ba_ref,
             bb_ref, u1_ref, yc_ref, yct_ref, ext, ext8):
        first = pl.program_id(0) == 0
        ext[H + TB:, :] = jnp.zeros((8, LANES), F32)

        def blk(cols):
            up = (gap_ref[:, cols] + ba_ref[:, cols]) * _sigmoid(gbp_ref[:, cols] + bb_ref[:, cols])
            ext[0:H, :] = jnp.where(first, 0.0, up)
            ext[H:H + TB, :] = (ga_ref[:, cols] + ba_ref[:, cols]) * _sigmoid(gb_ref[:, cols] + bb_ref[:, cols])
            _shifted_copies(ext, ext8)
            for r0 in range(0, TB, 64):
                acc = jnp.broadcast_to(cb_ref[:, cols], (64, LANES))
                for k in range(K_CONF):
                    acc = acc + w_ref[k:k + 1, cols] * _shifted(ext8, r0 + H - (K_CONF - 1) + k, 64)
                u1_ref[pl.ds(r0, 64), cols] = acc
        _col_loop(D_CONF, blk)

        def rows(rs):
            xh, _ = _ln_stats(u1_ref[rs, :])
            u2 = xh * lg_ref[...] + lb_ref[...]
            cg = cg_ref[rs, :]
            yc_ref[rs, :] = (u2 * _sigmoid(u2) * cg * _sigmoid(cg)).astype(BF16)
        _row_loop(TB, rows)
        yct_ref[...] = yc_ref[...].T

    return pl.pallas_call(
        body, name="conf_fwd", grid=(T // TB,),
        in_specs=[_ANY, _ANY, _row(TB, 1024, C_GLUA), _prev(TB, H, 1024, C_GLUA), _row(TB, 1024, C_GLUB),
                  _prev(TB, H, 1024, C_GLUB), _row(TB, 1024, C_CG), _full((K_CONF, 1024))] + [_full((1, 1024))] * 5,
        out_specs=[_row(TB, 1024), _row(TB, 1024, 1), _colt(1024, TB, 1)],
        out_shape=[S((T, 1024), F32), S((T, 2048), BF16), S((2048, T), BF16)],
        input_output_aliases={0: 1, 1: 2},
        scratch_shapes=[pltpu.VMEM((H + TB + 8, LANES), F32), pltpu.VMEM((8, H + TB, LANES), F32)],
        compiler_params=_params(("parallel",)))(mix, mixt, proj, proj, proj, proj, proj, w, cb, lg, lb, ba, bb)


def _d_mix_conf_bwd1(dout, w_out, u1, proj, lg, lb):
    T = u1.shape[0]

    def body(dout_ref, w_ref, u1_ref, cg_ref, lg_ref, lb_ref, dys_ref, du1_ref, dcg_ref, dg_ref, db_ref, dy_ref):
        @pl.when(pl.program_id(0) == 0)
        def _():
            dg_ref[...] = jnp.zeros_like(dg_ref)
            db_ref[...] = jnp.zeros_like(db_ref)

        dys_ref[...] = _dot(dout_ref[...], w_ref[0:D_SSM, :], _NT)
        dy_ref[...] = _dot(dout_ref[...], w_ref[D_SSM:, :], _NT)

        def rows(rs):
            xh, r = _ln_stats(u1_ref[rs, :])
            u2 = xh * lg_ref[...] + lb_ref[...]
            s2 = _sigmoid(u2)
            cg = cg_ref[rs, :]
            sc = _sigmoid(cg)
            dy = dy_ref[rs, :]
            dcg_ref[rs, :] = (dy * u2 * s2 * _dsilu(cg, sc)).astype(BF16)
            dv, dg, db = _ln_bwd(dy * cg * sc * _dsilu(u2, s2), xh, r, lg_ref[...])
            dg_ref[...] += dg
            db_ref[...] += db
            du1_ref[rs, :] = dv
        _row_loop(TB, rows)

    return pl.pallas_call(
        body, name="d_mix_conf_bwd1", grid=(T // TB,),
        in_specs=[_row(TB, D), _full((2 * D, D)), _row(TB, 1024), _row(TB, 1024, C_CG), _full((1, 1024)),
                  _full((1, 1024))],
        out_specs=[_row(TB, 1024), _row(TB, 1024), _row(TB, 1024, C_CG), _full((1, 1024)), _full((1, 1024))],
        out_shape=[S((T, 1024), F32), S((T, 1024), F32), S((T, N_MAIN), BF16), S((1, 1024), F32), S((1, 1024), F32)],
        scratch_shapes=[pltpu.VMEM((TB, D_CONF), F32)],
        compiler_params=_params(("arbitrary",)))(dout, w_out, u1, proj, lg, lb)


def _conf_bwd2(dproj, proj, du1, w, ba, bb):
    T = du1.shape[0]
    nt = T // TB
    H = HALO_CONF

    def body(dproj_ref, ga_ref, gap_ref, gb_ref, gbp_ref, du_ref, dun_ref, w_ref, ba_ref, bb_ref,
             dg_ref, dw_ref, dcb_ref, dba_ref, dbb_ref, ext, dext, ext8, dext8, dwacc):
        i = pl.program_id(0)
        first, last = i == 0, i == nt - 1

        @pl.when(first)
        def _():
            for r in (dcb_ref, dba_ref, dbb_ref, dwacc):
                r[...] = jnp.zeros_like(r)

        ext[H + TB:, :] = jnp.zeros((8, LANES), F32)
        dext[H + TB:, :] = jnp.zeros((8, LANES), F32)

        def blk(cols):
            cols_b = pl.ds(pl.multiple_of(cols.start + D_CONF, LANES), LANES)
            up = (gap_ref[:, cols] + ba_ref[:, cols]) * _sigmoid(gbp_ref[:, cols] + bb_ref[:, cols])
            ext[0:H, :] = jnp.where(first, 0.0, up)
            a = ga_ref[:, cols] + ba_ref[:, cols]
            sb = _sigmoid(gb_ref[:, cols] + bb_ref[:, cols])
            ext[H:H + TB, :] = a * sb
            du = du_ref[:, cols]
            dext[0:TB, :] = du
            dext[TB:TB + H, :] = jnp.where(last, 0.0, dun_ref[:, cols])
            _shifted_copies(ext, ext8)
            _shifted_copies(dext, dext8)
            dcb_ref[:, cols] += jnp.sum(du, axis=0, keepdims=True)
            for r0 in range(0, TB, 64):
                dur = du_ref[pl.ds(r0, 64), cols]
                acc = jnp.zeros((64, LANES), F32)
                for k in range(K_CONF):
                    prod = dur * _shifted(ext8, r0 + H - (K_CONF - 1) + k, 64)
                    dwacc[k * 8:(k + 1) * 8, cols] += prod.reshape(8, 8, LANES).sum(axis=0)
                    acc = acc + w_ref[k:k + 1, cols] * _shifted(dext8, r0 + K_CONF - 1 - k, 64)
                ar, sr = a[r0:r0 + 64], sb[r0:r0 + 64]
                da = acc * sr
                dbv = acc * ar * sr * (1.0 - sr)
                dg_ref[pl.ds(r0, 64), cols] = da.astype(BF16)
                dg_ref[pl.ds(r0, 64), cols_b] = dbv.astype(BF16)
                dba_ref[:, cols] += jnp.sum(da, axis=0, keepdims=True)
                dbb_ref[:, cols] += jnp.sum(dbv, axis=0, keepdims=True)
        _col_loop(D_CONF, blk)

        @pl.when(last)
        def _():
            dw_ref[...] = jnp.sum(dwacc[...].reshape(K_CONF, 8, D_CONF), axis=1)

    return pl.pallas_call(
        body, name="conf_bwd2", grid=(nt,),
        in_specs=[_ANY, _row(TB, 1024, C_GLUA), _prev(TB, H, 1024, C_GLUA), _row(TB, 1024, C_GLUB),
                  _prev(TB, H, 1024, C_GLUB), _row(TB, 1024), _next(TB, H, 1024, nt), _full((K_CONF, 1024)),
                  _full((1, 1024)), _full((1, 1024))],
        out_specs=[_row(TB, 2048), _full((K_CONF, 1024)), _full((1, 1024)), _full((1, 1024)), _full((1, 1024))],
        out_shape=[S(dproj.shape, BF16), S((K_CONF, 1024), F32)] + [S((1, 1024), F32)] * 3,
        input_output_aliases={0: 0},
        scratch_shapes=[pltpu.VMEM((H + TB + 8, LANES), F32), pltpu.VMEM((TB + H + 8, LANES), F32),
                        pltpu.VMEM((8, H + TB, LANES), F32), pltpu.VMEM((8, TB + H, LANES), F32),
                        pltpu.VMEM((K_CONF * 8, D_CONF), F32)],
        compiler_params=_params(("arbitrary",)))(dproj, proj, proj, proj, proj, du1, du1, w, ba, bb)


def _mesh_pos():
    x, y, c = lax.axis_index("x"), lax.axis_index("y"), lax.axis_index("c")
    return x, y, c, 4 * x + 2 * y + c


def _peer(x, y, c, k):
    return (x ^ ((k >> 2) & 1), y ^ ((k >> 1) & 1), c ^ (k & 1))


def _exchange_copies(ins, outs, kinds, send, recv, loc):
    nb = len(ins)
    x, y, c, me = _mesh_pos()
    src = lambda b, d: ins[b].at[d] if kinds[b] == "blocks" else ins[b]
    copies = [pltpu.make_async_copy(src(b, me), outs[b].at[me], loc.at[b]) for b in range(nb)]
    for k in range(1, N_DEV):
        px, py, pc = _peer(x, y, c, k)
        for b in range(nb):
            s = (k - 1) * nb + b
            copies.append(pltpu.make_async_remote_copy(
                src_ref=src(b, 4 * px + 2 * py + pc), dst_ref=outs[b].at[me], send_sem=send.at[s], recv_sem=recv.at[s],
                device_id=(px, py, pc), device_id_type=pl.DeviceIdType.MESH))
    return copies


def _exchange_shapes(bufs, kinds):
    return [S(b.shape if kd == "blocks" else (N_DEV,) + b.shape, b.dtype) for b, kd in zip(bufs, kinds)]


def _exchange_sems(nb):
    n = (N_DEV - 1) * nb
    return [pltpu.SemaphoreType.DMA((n,)), pltpu.SemaphoreType.DMA((n,)), pltpu.SemaphoreType.DMA((nb,))]


def _two_level_gather(ins, outs, send, recv, loc):
    nb = len(ins)
    x, y, c, me = _mesh_pos()
    here, sibling = (x, y, c), (x, y, 1 - c)
    chips = [(1 - x, y), (x, 1 - y), (1 - x, 1 - y)]

    def copy(slot, b, block, to, src=None):
        d = 4 * block[0] + 2 * block[1] + block[2]
        return pltpu.make_async_remote_copy(
            src_ref=outs[b].at[d] if src is None else src, dst_ref=outs[b].at[d],
            send_sem=send.at[slot * nb + b], recv_sem=recv.at[slot * nb + b],
            device_id=to, device_id_type=pl.DeviceIdType.MESH)

    mine = [pltpu.make_async_copy(ins[b], outs[b].at[me], loc.at[b]) for b in range(nb)]
    first = [copy(0, b, here, sibling, src=ins[b]) for b in range(nb)]
    first += [copy(1 + j, b, here, (*chip, c), src=ins[b]) for j, chip in enumerate(chips) for b in range(nb)]

    def start():
        for cp in mine + first:
            cp.start()

    def finish():
        passed = []
        for j, chip in enumerate(chips):
            for b in range(nb):
                copy(1 + j, b, (*chip, c), here).wait_recv()
            onward = [copy(4 + j, b, (*chip, c), sibling) for b in range(nb)]
            for cp in onward:
                cp.start()
            passed += onward
        for b in range(nb):
            copy(0, b, sibling, here).wait_recv()
        for j, chip in enumerate(chips):
            for b in range(nb):
                copy(4 + j, b, (*chip, 1 - c), here).wait_recv()
        for cp in first + passed:
            cp.wait_send()
        for cp in mine:
            cp.wait()

    return start, finish


def _exchange_plan(ins, outs, kinds, send, recv, loc):
    if all(kd == "gather" for kd in kinds):
        return _two_level_gather(ins, outs, send, recv, loc)
    copies = _exchange_copies(ins, outs, kinds, send, recv, loc)

    def start():
        for cp in copies:
            cp.start()

    def finish():
        for cp in copies:
            cp.wait()

    return start, finish


def _exchange(bufs, kinds, name):
    nb = len(bufs)

    def body(*refs):
        start, finish = _exchange_plan(refs[:nb], refs[nb:2 * nb], kinds, *refs[2 * nb:])
        start()
        finish()

    return pl.pallas_call(
        body, name=name, in_specs=[_ANY] * nb, out_specs=[_ANY] * nb,
        out_shape=_exchange_shapes(bufs, kinds), scratch_shapes=_exchange_sems(nb))(*bufs)


def _sum_parts(p_ref):
    acc = p_ref[0].astype(F32)
    for d in range(1, N_DEV):
        acc = acc + p_ref[d].astype(F32)
    return acc


def _adamw_math(g, w, m, v):
    m = ADAM_B1 * m + (1.0 - ADAM_B1) * g
    v = ADAM_B2 * v + (1.0 - ADAM_B2) * (g * g)
    m_hat = m / (1.0 - ADAM_B1 ** ADAM_STEP)
    v_hat = v / (1.0 - ADAM_B2 ** ADAM_STEP)
    return -ADAM_LR * (m_hat / (jnp.sqrt(v_hat) + ADAM_EPS) + ADAM_WD * w), m, v


HEAD_ROWS = 256


def _sum8_adamw(parts, w, m, v, name, head=None):
    _, R, C = w.shape
    tb = HEAD_ROWS if R % HEAD_ROWS == 0 else R
    nb = R // tb
    assert head is None or (tb == HEAD_ROWS and head.shape[1] == HEAD_ROWS and parts.shape[1] == R - HEAD_ROWS)
    skip = 0 if head is None else 1

    def body(*refs):
        p_ref, w_ref, m_ref, v_ref, g_ref, d_ref, mo_ref, vo_ref = refs[skip:]
        g = _sum_parts(p_ref)
        if head is not None:
            g = jnp.where(pl.program_id(0) == nb - 1, _sum_parts(refs[0]), g)
        g_ref[0] = g
        d_ref[0], mo_ref[0], vo_ref[0] = _adamw_math(g, w_ref[0], m_ref[0], v_ref[0])

    first = [] if head is None else [pl.BlockSpec((N_DEV, tb, C), lambda i: (0, 0, 0))]
    own = pl.BlockSpec((1, tb, C), lambda i: (0, i, 0))
    last_part = parts.shape[1] // tb - 1
    return pl.pallas_call(
        body, name=name, grid=(nb,),
        in_specs=first + [pl.BlockSpec((N_DEV, tb, C), lambda i: (0, jnp.minimum(i, last_part), 0))] + [own] * 3,
        out_specs=[own] * 4, out_shape=[S((1, R, C), F32)] * 4,
        compiler_params=_params(("parallel",)))(*([] if head is None else [head]), parts, w, m, v)


SMALL_LAYOUT = (
    ("ln_emb_g", 0, 1024), ("ln_emb_b", 0, 1024), ("ssm_conv_b", 0, 1024), ("ssm_conv_b", 1024, 512),
    ("dt_bias", 0, N_HEADS), ("a_log", 0, N_HEADS), ("d_skip", 0, N_HEADS), ("ssm_norm_g", 0, 1024),
    ("b_glu", 0, 1024), ("b_glu", 1024, 1024), ("conf_conv_b", 0, 1024), ("conf_ln_g", 0, 1024),
    ("conf_ln_b", 0, 1024), ("b_out", 0, 1024), ("ln1_g", 0, 1024), ("ln1_b", 0, 1024), ("ln2_g", 0, 1024),
    ("ln2_b", 0, 1024))
SMALL_ROWS = 24
SMALL = tuple(dict.fromkeys(n for n, _, _ in SMALL_LAYOUT))


LOSS_ROW = len(SMALL_LAYOUT)


def _pack_small(rows, loss):
    def body(*refs):
        o_ref = refs[-1]
        o_ref[...] = jnp.zeros_like(o_ref)
        for r, ref in enumerate(refs[:-2]):
            o_ref[r:r + 1, 0:ref.shape[1]] = ref[...]
        o_ref[LOSS_ROW:LOSS_ROW + 1, 0:LANES] = refs[-2][0:1, :]

    return pl.pallas_call(body, name="pack_small", out_shape=S((SMALL_ROWS, 1024), F32))(*rows, loss)


def _small_update(parts, w, m, v):
    def body(*refs):
        p_ref = refs[0]
        ins = {n: refs[1 + 3 * i:4 + 3 * i] for i, n in enumerate(SMALL)}
        o0 = 1 + 3 * len(SMALL)
        outs = {n: refs[o0 + 4 * i:o0 + 4 * i + 4] for i, n in enumerate(SMALL)}
        gsum = refs[-1]
        gsum[...] = _sum_parts(p_ref)
        refs[-2][...] = gsum[LOSS_ROW:LOSS_ROW + 1, 0:LANES]
        for r, (n, off, wd) in enumerate(SMALL_LAYOUT):
            cs = slice(off, off + wd)
            g = gsum[r:r + 1, 0:wd]
            w_ref, m_ref, v_ref = ins[n]
            g_ref, d_ref, mo_ref, vo_ref = outs[n]
            g_ref[:, cs] = g
            d_ref[:, cs], mo_ref[:, cs], vo_ref[:, cs] = _adamw_math(g, w_ref[:, cs], m_ref[:, cs], v_ref[:, cs])

    args = [parts] + [a for n in SMALL for a in (w[n], m[n], v[n])]
    res = pl.pallas_call(
        body, name="small_update",
        out_shape=[S(w[n].shape, F32) for n in SMALL for _ in range(4)] + [S((1, LANES), F32)],
        scratch_shapes=[pltpu.VMEM((SMALL_ROWS, 1024), F32)])(*args)
    return tuple({n: res[4 * i + j] for i, n in enumerate(SMALL)} for j in range(4)) + (res[-1],)


EARLY = ("w_in", "ssm_conv_w", "conf_conv_w")
LATE = ("w_out", "w_ple_gate", "w_ple_proj")


def _local_step(x, p, tgt, W, shards=None):
    r1 = lambda v: v.reshape(1, -1).astype(F32)
    pad_l = lambda v: jnp.pad(r1(v), ((0, 0), (0, LANES - v.size)))
    late = None if shards is None else [shards[n] for n in LATE]
    if shards is None:
        h0, h0b, h0bt = _ln_emb_fwd(x, r1(W["ln_emb_g"]), r1(W["ln_emb_b"]))
    else:
        h0, h0b, h0bt, *gathered = _ln_emb_fwd(x, r1(W["ln_emb_g"]), r1(W["ln_emb_b"]),
                                               exchange=([shards[n] for n in EARLY], ("gather",) * len(EARLY)))
        W = dict(W, **{n: a if n == "w_in" else _unstack_shards(a, BY_COLS[n]) for n, a in zip(EARLY, gathered)})
    w_main, w_dt = _w_in_to_main(W["w_in"])
    scw, scb = W["ssm_conv_w"], r1(W["ssm_conv_b"])
    wx, wb, bx, bb = scw[:, :1024], scw[:, 1024:], scb[:, :1024], scb[:, 1024:]
    dt_bias, alog = pad_l(W["dt_bias"]), pad_l(W["a_log"])
    dskip_row = jnp.repeat(W["d_skip"].reshape(-1), HEAD).reshape(1, -1)
    norm_g = r1(W["ssm_norm_g"])
    bglu = r1(W["b_glu"])
    ba, bbg = bglu[:, :1024], bglu[:, 1024:]
    ccw, ccb, clg, clb = W["conf_conv_w"], r1(W["conf_conv_b"]), r1(W["conf_ln_g"]), r1(W["conf_ln_b"])

    if late is None:
        proj = _mm(h0b, w_main, "nn", "in_proj", tm=TM_IN_PROJ)
    else:
        proj, *gathered = _mm(h0b, w_main, "nn", "in_proj", tm=TM_IN_PROJ, exchange=(late, ("gather",) * len(LATE)))
        W = dict(W, **{n: _unstack_shards(a, BY_COLS[n]) for n, a in zip(LATE, gathered)})
    xs_c, bc_c, dt, dt_raw = _ssd_pre_fwd(proj, h0b, w_dt, wx, wb, bx, bb, dt_bias)
    mix, ypre, hprev, mixt = _ssd_fwd(xs_c, bc_c, dt, proj, alog, dskip_row, norm_g)
    u1, mix, mixt = _conf_fwd(mix, mixt, proj, ccw, ccb, clg, clb, ba, bbg)
    out, h1, h1b, h1bt = _out_proj_post1(mix, W["w_out"], h0, r1(W["b_out"]), r1(W["ln1_g"]), r1(W["ln1_b"]))
    dh1a, dgp, dple, loss, dln2g, dln2b, pbt = _ple_post2(h1b, W["w_ple_gate"], p, W["w_ple_proj"], h1, tgt,
                                                      r1(W["ln2_g"]), r1(W["ln2_b"]))

    g = {}
    g["w_ple_proj"] = _mm(pbt, dple, "nn", "d_ple_proj", out_dtype=BF16)
    g["w_ple_gate"] = _mm(h1bt, dgp, "nn", "d_ple_gate", out_dtype=BF16)
    dout, dh0a, dln1g, dln1b, dbout = _d_h1_post1_bwd(dgp, W["w_ple_gate"], dh1a, h0, out, r1(W["b_out"]),
                                                      r1(W["ln1_g"]))
    g["w_out"] = _mm(mixt, dout, "nn", "d_w_out", out_dtype=BF16)
    dmix, du1, dproj, dclg, dclb = _d_mix_conf_bwd1(dout, W["w_out"], u1, proj, clg, clb)
    dproj, g["conf_conv_w"], dccb, dba, dbb = _conf_bwd2(dproj, proj, du1, ccw, ba, bbg)
    stack = lambda names: [_stack_shards(g[n], BY_COLS[n]) for n in names]
    dxs_c, dbc_c, ddtr, dproj, dng, ddsk, dalog, ddtb, *recv_a = _ssd_bwd(
        dproj, xs_c, bc_c, dt, dt_raw, dt_bias, proj, ypre, hprev, dmix, alog, dskip_row, norm_g,
        exchange=None if late is None else (stack(LATE), ("blocks",) * len(LATE)))
    dproj, dwx, dbx = _ssd_conv_bwd(dproj, proj, dxs_c, wx, bx, 1024, C_XS, "ssd_conv_bwd_x")
    dproj, dwb, dbb2 = _ssd_conv_bwd(dproj, proj, dbc_c, wb, bb, 512, C_BC, "ssd_conv_bwd_bc")
    g["ssm_conv_w"] = jnp.concatenate([dwx, dwb], axis=1)
    dw_dt = _mm(h0bt, ddtr, "nn", "d_w_dt", out_dtype=BF16)
    last_args = (dproj, w_main, ddtr, w_dt, dh0a, x, r1(W["ln_emb_g"]))
    if late is None:
        g["w_in"] = _w_in_blocks(_mm(h0bt, dproj, "nn", "d_w_in", out_dtype=BF16), dw_dt)
        grad_x, dlng, dlnb = _d_h0_ln_bwd(*last_args)
    else:
        r0 = D - HEAD_ROWS
        head = _w_in_blocks(_mm(h0bt, dproj, "nn", "d_w_in_head", out_dtype=BF16, tk=x.shape[0],
                                a_rows=(r0, HEAD_ROWS)), dw_dt[r0:])
        dw_rest, recv_head = _mm(h0bt, dproj, "nn", "d_w_in", out_dtype=BF16, a_rows=(0, r0),
                                 exchange=([head], ("blocks",)))
        last = ("ssm_conv_w", "conf_conv_w")
        grad_x, dlng, dlnb, *recv_b = _d_h0_ln_bwd(
            *last_args, exchange=([_w_in_blocks(dw_rest, dw_dt[:r0])] + stack(last), ("blocks",) * 3))
        g["recv"] = dict(zip(LATE + ("w_in",) + last, recv_a + recv_b), w_in_head=recv_head)
    g["rows"] = [dlng, dlnb, dbx, dbb2, ddtb, dalog, ddsk, dng, dba, dbb, dccb, dclg, dclb, dbout, dln1g, dln1b,
                 dln2g, dln2b]
    return loss, grad_x, g


W_IN_SEGMENTS = ((0, 2048, 2048), (2048, 5120, 512), (2560, None, N_HEADS), (2576, 0, 2048), (4624, 4096, 1024))


def _w_in_to_main(shards):
    def pieces(p0, width):
        out, p = [], p0
        while p < p0 + width:
            d = p // COLS_PER_DEV
            hi = min(p0 + width, (d + 1) * COLS_PER_DEV)
            out.append(shards[d][:, p - d * COLS_PER_DEV:hi - d * COLS_PER_DEV])
            p = hi
        return out
    main = [s for s in sorted(W_IN_SEGMENTS, key=lambda s: -1 if s[1] is None else s[1]) if s[1] is not None]
    w_main = jnp.concatenate([q for p0, _, width in main for q in pieces(p0, width)], axis=1)
    w_dt = jnp.concatenate(pieces(2560, N_HEADS), axis=1)
    return w_main, jnp.pad(w_dt, ((0, 0), (0, LANES - N_HEADS)))


def _w_in_blocks(dw_main, dw_dt):
    blocks = []
    for d in range(N_DEV):
        lo_d, hi_d = d * COLS_PER_DEV, (d + 1) * COLS_PER_DEV
        parts = []
        for p0, m0, width in W_IN_SEGMENTS:
            lo, hi = max(lo_d, p0), min(hi_d, p0 + width)
            if lo < hi:
                parts.append(dw_dt[:, lo - p0:hi - p0] if m0 is None else dw_main[:, m0 + lo - p0:m0 + hi - p0])
        blocks.append(jnp.concatenate(parts, axis=1))
    return jnp.stack(blocks)


WEIGHTS = ['ln_emb_g', 'ln_emb_b', 'w_in', 'ssm_conv_w', 'ssm_conv_b', 'dt_bias', 'a_log', 'd_skip', 'ssm_norm_g',
           'b_glu', 'conf_conv_w', 'conf_conv_b', 'conf_ln_g', 'conf_ln_b', 'w_out', 'b_out', 'ln1_g', 'ln1_b',
           'w_ple_gate', 'w_ple_proj', 'ln2_g', 'ln2_b']
SHARDED = (("w_in", True), ("w_out", False), ("w_ple_gate", False), ("w_ple_proj", True), ("ssm_conv_w", True),
           ("conf_conv_w", True))
BY_COLS = dict(SHARDED)


def _stack_shards(a, by_cols):
    if by_cols:
        return a.reshape(a.shape[0], N_DEV, a.shape[1] // N_DEV).transpose(1, 0, 2)
    return a.reshape(N_DEV, a.shape[0] // N_DEV, a.shape[1])


def _unstack_shards(a, by_cols):
    if by_cols:
        return a.transpose(1, 0, 2).reshape(a.shape[1], N_DEV * a.shape[2])
    return a.reshape(N_DEV * a.shape[1], a.shape[2])


def kernel(x, p, ln_emb_g, ln_emb_b, w_in, ssm_conv_w, ssm_conv_b, dt_bias, a_log, d_skip, ssm_norm_g, b_glu, conf_conv_w, conf_conv_b, conf_ln_g, conf_ln_b, w_out, b_out, ln1_g, ln1_b, w_ple_gate, w_ple_proj, ln2_g, ln2_b, loss_target, m_ln_emb_g, m_ln_emb_b, m_w_in, m_ssm_conv_w, m_ssm_conv_b, m_dt_bias, m_a_log, m_d_skip, m_ssm_norm_g, m_b_glu, m_conf_conv_w, m_conf_conv_b, m_conf_ln_g, m_conf_ln_b, m_w_out, m_b_out, m_ln1_g, m_ln1_b, m_w_ple_gate, m_w_ple_proj, m_ln2_g, m_ln2_b, v_ln_emb_g, v_ln_emb_b, v_w_in, v_ssm_conv_w, v_ssm_conv_b, v_dt_bias, v_a_log, v_d_skip, v_ssm_norm_g, v_b_glu, v_conf_conv_w, v_conf_conv_b, v_conf_ln_g, v_conf_ln_b, v_w_out, v_b_out, v_ln1_g, v_ln1_b, v_w_ple_gate, v_w_ple_proj, v_ln2_g, v_ln2_b):
    loc = dict(locals())
    w = {n: loc[n] for n in WEIGHTS}
    m = {n: loc["m_" + n] for n in WEIGHTS}
    v = {n: loc["v_" + n] for n in WEIGHTS}
    sharded = [n for n, _ in SHARDED]

    shards = {n: w[n][0].astype(BF16) if n.startswith("w_") else w[n][0] for n in sharded}
    W = {n: w[n].reshape(-1) for n in SMALL}
    loss, grad_x, g = _local_step(x[0], p[0, 0], loss_target[0], W, shards=shards)
    (recv_small,) = _exchange([_pack_small(g["rows"], loss)], ("all",), "small_exchange")

    grads, delta, new_m, new_v = {}, {}, {}, {}
    for n in sharded:
        grads[n], delta[n], new_m[n], new_v[n] = _sum8_adamw(
            g["recv"][n], w[n], m[n], v[n], "adamw_" + n, head=g["recv"]["w_in_head"] if n == "w_in" else None)
    two_d = lambda d: {n: d[n].reshape(1, -1) for n in SMALL}
    *small, loss = _small_update(recv_small, two_d(w), two_d(m), two_d(v))
    for dst, res in zip((grads, delta, new_m, new_v), small):
        for n in SMALL:
            dst[n] = res[n].reshape(w[n].shape)
    return (loss[0, 0], grad_x[None], *[grads[n] for n in WEIGHTS], *[delta[n] for n in WEIGHTS],
            *[new_m[n] for n in WEIGHTS], *[new_v[n] for n in WEIGHTS])
```

```python
import numpy as np
import jax
import jax.numpy as jnp
from jax import lax
from jax.experimental import pallas as pl
from jax.experimental.pallas import tpu as pltpu

F32, BF16 = jnp.float32, jnp.bfloat16
S = jax.ShapeDtypeStruct

N_DEV = 8
D = 1024
D_PLE = 256
D_SSM = 1024
D_CONF = 1024
N_HEADS = 16
HEAD = 64
N_STATE = 128
CHUNK = 128
K_SSM = 4
K_CONF = 31
D_IN = 5648
COLS_PER_DEV = D_IN // N_DEV
LN_EPS = 1e-5
RMS_EPS = 1e-5
ALPHA = 2.0 ** 0.25
LANES = 128
TB = 512
TB_SSD_CONV_BWD = 512
TM_IN_PROJ = 2048
TM_FUSED = 1024
TM_SKEWED = 512
RG = 32
ROW_UNROLL = 8
HALO_SSM = 8
HALO_CONF = 32
VMEM_LIMIT = 56 * 1024 * 1024

ADAM_LR, ADAM_B1, ADAM_B2, ADAM_EPS, ADAM_WD, ADAM_STEP = 0.001, 0.9, 0.999, 1e-08, 0.01, 10

C_GLUA, C_GLUB, C_XS, C_Z, C_CG = 0, 1, 2, 3, 4
C_BC = 10
N_MAIN = 5632


def _params(sem, vmem=VMEM_LIMIT):
    return pltpu.CompilerParams(dimension_semantics=sem, vmem_limit_bytes=vmem)


def _row(tb, n, col=0):
    return pl.BlockSpec((tb, n), lambda i: (i, col))


def _colt(n, tb, row=0):
    return pl.BlockSpec((n, tb), lambda i: (row, i))


def _full(shape):
    return pl.BlockSpec(shape, lambda i: (0,) * len(shape))


_ANY = pl.BlockSpec(memory_space=pl.ANY)


def _prev(tb, halo, n, col=0):
    r = tb // halo
    return pl.BlockSpec((halo, n), lambda i: (jnp.maximum(i * r - 1, 0), col))


def _next(tb, halo, n, nt, col=0):
    r = tb // halo
    return pl.BlockSpec((halo, n), lambda i: (jnp.minimum((i + 1) * r, nt * r - 1), col))


def _row_loop(tb, fn):
    def it(r, c):
        fn(pl.ds(pl.multiple_of(r * RG, RG), RG))
        return c
    lax.fori_loop(0, tb // RG, it, 0, unroll=ROW_UNROLL)


def _col_loop(n, fn):
    def it(j, c):
        fn(pl.ds(pl.multiple_of(j * LANES, LANES), LANES))
        return c
    lax.fori_loop(0, n // LANES, it, 0)


def _sigmoid(x):
    return 1.0 / (1.0 + jnp.exp(-x))


def _dsilu(x, s):
    return s * (1.0 + x * (1.0 - s))


def _ln_stats(v):
    mu = jnp.mean(v, axis=-1, keepdims=True)
    c = v - mu
    r = lax.rsqrt(jnp.mean(c * c, axis=-1, keepdims=True) + LN_EPS)
    return c * r, r


def _ln_bwd(dy, xhat, r, g):
    dxh = dy * g
    dv = r * (dxh - jnp.mean(dxh, axis=-1, keepdims=True) - xhat * jnp.mean(dxh * xhat, axis=-1, keepdims=True))
    return dv, jnp.sum(dy * xhat, axis=0, keepdims=True), jnp.sum(dy, axis=0, keepdims=True)


def _dot(a, b, dims=((1,), (0,))):
    return lax.dot_general(a.astype(BF16), b.astype(BF16), (dims, ((), ())), preferred_element_type=F32)


_NT = ((1,), (1,))
_TN = ((0,), (0,))


def _split3(x):
    hi = x.astype(BF16)
    r = x - hi.astype(F32)
    mid = r.astype(BF16)
    return hi, mid, (r - mid.astype(F32)).astype(BF16)


def _dot_sel_b(a, b, dims=((1,), (0,))):
    hi, mid, lo = _split3(a)
    return (_dot(lo, b, dims) + _dot(mid, b, dims)) + _dot(hi, b, dims)


def _dot_sel_a(a, b, dims=((1,), (0,))):
    hi, mid, lo = _split3(b)
    return (_dot(a, lo, dims) + _dot(a, mid, dims)) + _dot(a, hi, dims)


def _mm(a, b, mode, name, out_dtype=F32, add=None, tm=1024, tn=None, tk=1024, exchange=None, a_rows=None):
    assert mode in ("nn", "nt")
    (M, K), N = a.shape, b.shape[1 if mode == "nn" else 0]
    row0 = 0
    if a_rows is not None:
        row0, M = a_rows
        tm = M
        assert row0 % M == 0
    if tn is None:
        tn = next(t for t in (1024, 1408, 512, 256, LANES) if N % t == 0)
    tm, tn, tk = min(tm, M), min(tn, N), min(tk, K)
    assert M % tm == 0 and N % tn == 0 and K % tk == 0, (name, M, N, K)
    grid = (M // tm, N // tn, K // tk)
    nk = grid[2]
    dims = ((1,), (0,)) if mode == "nn" else _NT
    n_in = 2 + (add is not None)
    xbufs, kinds = exchange if exchange is not None else ((), ())
    nx = len(xbufs)

    def body(*refs):
        a_ref, b_ref = refs[:2]
        o_ref = refs[n_in + nx]
        acc = refs[n_in + 2 * nx + 1]
        i, j, k = pl.program_id(0), pl.program_id(1), pl.program_id(2)
        if nx:
            start, finish = _exchange_plan(refs[n_in:n_in + nx], refs[n_in + nx + 1:n_in + 2 * nx + 1], kinds,
                                           *refs[n_in + 2 * nx + 2:])
            pl.when((i == 0) & (j == 0) & (k == 0))(start)

        d = _dot(a_ref[...], b_ref[...], dims)

        def write_out(r):
            if add is not None:
                r = r + refs[2][...]
            o_ref[...] = r.astype(out_dtype)

        if nk == 1:
            write_out(d)
        else:
            @pl.when(k == 0)
            def _():
                acc[...] = d

            @pl.when((k > 0) & (k < nk - 1))
            def _():
                acc[...] += d

            @pl.when(k == nk - 1)
            def _():
                write_out(acc[...] + d)

        if nx:
            pl.when((i == grid[0] - 1) & (j == grid[1] - 1) & (k == nk - 1))(finish)

    a_spec = pl.BlockSpec((tm, tk), lambda i, j, k: (i + row0 // tm, k))
    b_spec = pl.BlockSpec((tn, tk), lambda i, j, k: (j, k)) if mode == "nt" else pl.BlockSpec((tk, tn), lambda i, j, k: (k, j))
    o_spec = pl.BlockSpec((tm, tn), lambda i, j, k: (i, j))
    ins, specs = [a, b], [a_spec, b_spec]
    if add is not None:
        ins.append(add)
        specs.append(o_spec)
    acc_spec = pltpu.VMEM((tm, tn) if nk > 1 else (8, LANES), F32)
    if not nx:
        return pl.pallas_call(
            body, name=name, grid=grid, in_specs=specs, out_specs=o_spec,
            out_shape=S((M, N), out_dtype), scratch_shapes=[acc_spec],
            compiler_params=_params(("parallel", "parallel", "arbitrary")))(*ins)
    return pl.pallas_call(
        body, name=name, grid=grid, in_specs=specs + [_ANY] * nx, out_specs=[o_spec] + [_ANY] * nx,
        out_shape=[S((M, N), out_dtype)] + _exchange_shapes(xbufs, kinds),
        scratch_shapes=[acc_spec] + _exchange_sems(nx),
        compiler_params=_params(("arbitrary", "arbitrary", "arbitrary")))(*ins, *xbufs)


def _ln_emb_fwd(x, g, b, exchange=None):
    T = x.shape[0]

    nt = T // TB
    xbufs, kinds = exchange if exchange is not None else ((), ())
    nx = len(xbufs)

    def body(*refs):
        x_ref, g_ref, b_ref = refs[:3]
        h_ref, hb_ref, hbt_ref = refs[3 + nx:6 + nx]
        i = pl.program_id(0)
        if nx:
            start, finish = _exchange_plan(refs[3:3 + nx], refs[6 + nx:6 + 2 * nx], kinds, *refs[6 + 2 * nx:])
            pl.when(i == 0)(start)

        def rows(rs):
            xh, _ = _ln_stats(x_ref[rs, :])
            h = xh * g_ref[...] + b_ref[...]
            h_ref[rs, :] = h
            hb_ref[rs, :] = h.astype(BF16)
        _row_loop(TB, rows)
        hbt_ref[...] = hb_ref[...].T
        if nx:
            pl.when(i == nt - 1)(finish)

    return pl.pallas_call(
        body, name="ln_emb_fwd", grid=(nt,),
        in_specs=[_row(TB, D), _full((1, D)), _full((1, D))] + [_ANY] * nx,
        out_specs=[_row(TB, D), _row(TB, D), _colt(D, TB)] + [_ANY] * nx,
        out_shape=[S((T, D), F32), S((T, D), BF16), S((D, T), BF16)] + _exchange_shapes(xbufs, kinds),
        scratch_shapes=_exchange_sems(nx) if nx else [],
        compiler_params=_params(("arbitrary",)))(x, g, b, *xbufs)


def _out_proj_post1(mix, w_out, h0, b_out, g, b, tm=TM_SKEWED):
    T, K = mix.shape
    tm = min(tm, T)
    assert T % tm == 0
    n = T // tm

    def body(mix_ref, w_ref, h0_ref, bo_ref, g_ref, b_ref, out_ref, h_ref, hb_ref, hbt_ref, prev):
        @pl.when(pl.program_id(0) == 0)
        def _():
            prev[...] = jnp.zeros_like(prev)

        d = _dot(mix_ref[...], w_ref[...])
        out_ref[...] = d
        for r0 in range(0, tm, RG):
            rs = pl.ds(r0, RG)
            xh, _ = _ln_stats(ALPHA * h0_ref[rs, :] + prev[rs, :] + bo_ref[...])
            h = xh * g_ref[...] + b_ref[...]
            h_ref[rs, :] = h
            hb_ref[rs, :] = h.astype(BF16)
        hbt_ref[...] = hb_ref[...].T
        prev[...] = d

    cur = lambda i: jnp.minimum(i, n - 1)
    old = lambda i: jnp.maximum(i - 1, 0)
    const = pl.BlockSpec((1, D), lambda i: (0, 0))
    return pl.pallas_call(
        body, name="out_proj_post1", grid=(n + 1,),
        in_specs=[pl.BlockSpec((tm, K), lambda i: (cur(i), 0)), _full((K, D)),
                  pl.BlockSpec((tm, D), lambda i: (old(i), 0)), const, const, const],
        out_specs=[pl.BlockSpec((tm, D), lambda i: (cur(i), 0))] + [pl.BlockSpec((tm, D), lambda i: (old(i), 0))] * 2
        + [pl.BlockSpec((D, tm), lambda i: (0, old(i)))],
        out_shape=[S((T, D), F32), S((T, D), F32), S((T, D), BF16), S((D, T), BF16)],
        scratch_shapes=[pltpu.VMEM((tm, D), F32)],
        compiler_params=_params(("arbitrary",)))(mix, w_out, h0, b_out, g, b)


def _ple_post2(h1b, w_gate, p, w_proj, h1, tgt, g, b, tm=TM_FUSED // 2):
    T = h1.shape[0]
    tm = min(tm, T)
    assert T % tm == 0

    def body(h1b_ref, wg_ref, p_ref, wp_ref, h1_ref, tgt_ref, g_ref, b_ref,
             dh1_ref, dgp_ref, dple_ref, loss_ref, dg_ref, db_ref, pt_ref, gp_ref, ple_ref):
        @pl.when(pl.program_id(0) == 0)
        def _():
            loss_ref[...] = jnp.zeros_like(loss_ref)
            dg_ref[...] = jnp.zeros_like(dg_ref)
            db_ref[...] = jnp.zeros_like(db_ref)

        pb = p_ref[...].astype(BF16)
        pt_ref[...] = pb.T
        gp_ref[...] = _dot(h1b_ref[...], wg_ref[...])
        ple_ref[...] = _dot(pb, wp_ref[...])

        def rows(rs):
            gate = _sigmoid(gp_ref[rs, :])
            ple = ple_ref[rs, :]
            xh, r = _ln_stats(ALPHA * h1_ref[rs, :] + gate * ple)
            err = xh * g_ref[...] + b_ref[...] - tgt_ref[rs, :]
            loss_ref[...] += 0.5 * jnp.sum(jnp.mean(err * err, axis=-1, keepdims=True), axis=0, keepdims=True)
            dv, dg, db = _ln_bwd(err * (1.0 / D), xh, r, g_ref[...])
            dg_ref[...] += dg
            db_ref[...] += db
            dh1_ref[rs, :] = ALPHA * dv
            dgp_ref[rs, :] = (dv * ple * gate * (1.0 - gate)).astype(BF16)
            dple_ref[rs, :] = (dv * gate).astype(BF16)
        _row_loop(tm, rows)

    return pl.pallas_call(
        body, name="ple_post2", grid=(T // tm,),
        in_specs=[_row(tm, D), _full((D, D)), _row(tm, D_PLE), _full((D_PLE, D)), _row(tm, D), _row(tm, D),
                  _full((1, D)), _full((1, D))],
        out_specs=[_row(tm, D)] * 3 + [_full((8, LANES)), _full((1, D)), _full((1, D)), _colt(D_PLE, tm)],
        out_shape=[S((T, D), F32), S((T, D), BF16), S((T, D), BF16), S((8, LANES), F32), S((1, D), F32), S((1, D), F32),
                   S((D_PLE, T), BF16)],
        scratch_shapes=[pltpu.VMEM((tm, D), F32), pltpu.VMEM((tm, D), F32)],
        compiler_params=_params(("arbitrary",)))(h1b, w_gate, p, w_proj, h1, tgt, g, b)


def _d_h1_post1_bwd(dgp, w_gate, dh1a, h0, out, b_out, g, tm=TM_FUSED // 2):
    T = h0.shape[0]
    tm = min(tm, T)
    assert T % tm == 0

    def body(dgp_ref, wg_ref, da_ref, h0_ref, out_ref, bo_ref, g_ref, dout_ref, dh0_ref, dg_ref, db_ref, dbo_ref, dh1):
        @pl.when(pl.program_id(0) == 0)
        def _():
            dg_ref[...] = jnp.zeros_like(dg_ref)
            db_ref[...] = jnp.zeros_like(db_ref)
            dbo_ref[...] = jnp.zeros_like(dbo_ref)

        dh1[...] = da_ref[...] + _dot(dgp_ref[...], wg_ref[...], _NT)

        def rows(rs):
            xh, r = _ln_stats(ALPHA * h0_ref[rs, :] + out_ref[rs, :] + bo_ref[...])
            dv, dg, db = _ln_bwd(dh1[rs, :], xh, r, g_ref[...])
            dg_ref[...] += dg
            db_ref[...] += db
            dbo_ref[...] += jnp.sum(dv, axis=0, keepdims=True)
            dout_ref[rs, :] = dv.astype(BF16)
            dh0_ref[rs, :] = ALPHA * dv
        _row_loop(tm, rows)

    return pl.pallas_call(
        body, name="d_h1_post1_bwd", grid=(T // tm,),
        in_specs=[_row(tm, D), _full((D, D))] + [_row(tm, D)] * 3 + [_full((1, D))] * 2,
        out_specs=[_row(tm, D)] * 2 + [_full((1, D))] * 3,
        out_shape=[S((T, D), BF16), S((T, D), F32)] + [S((1, D), F32)] * 3,
        scratch_shapes=[pltpu.VMEM((tm, D), F32)],
        compiler_params=_params(("arbitrary",)))(dgp, w_gate, dh1a, h0, out, b_out, g)


def _d_h0_ln_bwd(dproj, w_main, ddtr, w_dt, dh0a, x, g, exchange=None, tm=1024, tk=1408):
    T, K = dproj.shape
    tm = min(tm, T)
    assert T % tm == 0 and K % tk == 0
    ni, nk = T // tm, K // tk
    xbufs, kinds = exchange if exchange is not None else ((), ())
    nx = len(xbufs)

    def body(*refs):
        dp_ref, w_ref, dt_ref, wdt_ref, da_ref, x_ref, g_ref = refs[:7]
        dx_ref, dg_ref, db_ref = refs[7 + nx:10 + nx]
        acc = refs[10 + 2 * nx]
        i, k = pl.program_id(0), pl.program_id(1)
        if nx:
            start, finish = _exchange_plan(refs[7:7 + nx], refs[10 + nx:10 + 2 * nx], kinds, *refs[11 + 2 * nx:])
            pl.when((i == 0) & (k == 0))(start)

        @pl.when((i == 0) & (k == 0))
        def _():
            dg_ref[...] = jnp.zeros_like(dg_ref)
            db_ref[...] = jnp.zeros_like(db_ref)

        d = _dot(dp_ref[...], w_ref[...], _NT)

        @pl.when(k == 0)
        def _():
            acc[...] = da_ref[...] + _dot(dt_ref[...], wdt_ref[...], _NT) + d

        @pl.when(k > 0)
        def _():
            acc[...] += d

        @pl.when(k == nk - 1)
        def _():
            def rows(rs):
                xh, r = _ln_stats(x_ref[rs, :])
                dv, dg, db = _ln_bwd(acc[rs, :], xh, r, g_ref[...])
                dg_ref[...] += dg
                db_ref[...] += db
                dx_ref[rs, :] = dv
            _row_loop(tm, rows)

        if nx:
            pl.when((i == ni - 1) & (k == nk - 1))(finish)

    rowt = lambda n: pl.BlockSpec((tm, n), lambda i, k: (i, 0))
    const = lambda shape: pl.BlockSpec(shape, lambda i, k: (0, 0))
    return pl.pallas_call(
        body, name="d_h0_ln_bwd", grid=(ni, nk),
        in_specs=[pl.BlockSpec((tm, tk), lambda i, k: (i, k)), pl.BlockSpec((D, tk), lambda i, k: (0, k)),
                  rowt(LANES), const((D, LANES)), rowt(D), rowt(D), const((1, D))] + [_ANY] * nx,
        out_specs=[rowt(D), const((1, D)), const((1, D))] + [_ANY] * nx,
        out_shape=[S((T, D), F32), S((1, D), F32), S((1, D), F32)] + _exchange_shapes(xbufs, kinds),
        scratch_shapes=[pltpu.VMEM((tm, D), F32)] + (_exchange_sems(nx) if nx else []),
        compiler_params=_params(("arbitrary", "arbitrary")))(dproj, w_main, ddtr, w_dt, dh0a, x, g, *xbufs)


def _softplus(x):
    return jnp.maximum(x, 0.0) + jnp.log1p(jnp.exp(-jnp.abs(x)))


def _ssd_pre_fwd(proj, h0b, w_dt, wx, wb, bx, bb, dt_bias):
    T = proj.shape[0]
    H = HALO_SSM

    def body(xs_ref, xsp_ref, bc_ref, bcp_ref, h0b_ref, wdt_ref, wx_ref, wb_ref, bx_ref, bb_ref, dtb_ref,
             xso_ref, bco_ref, dto_ref, dtr_ref, extx, extb):
        first = pl.program_id(0) == 0

        def conv(t_ref, p_ref, w_ref, b_ref, o_ref, ext, n):
            def blk(cols):
                ext[0:H, cols] = jnp.where(first, 0.0, p_ref[:, cols])
                ext[H:, cols] = t_ref[:, cols]
                for r0 in range(0, TB, 64):
                    acc = jnp.broadcast_to(b_ref[:, cols], (64, LANES))
                    for k in range(K_SSM):
                        acc = acc + w_ref[k:k + 1, cols] * ext[pl.ds(r0 + H - (K_SSM - 1) + k, 64), cols]
                    o_ref[pl.ds(r0, 64), cols] = acc * _sigmoid(acc)
            _col_loop(n, blk)

        conv(xs_ref, xsp_ref, wx_ref, bx_ref, xso_ref, extx, D_SSM)
        conv(bc_ref, bcp_ref, wb_ref, bb_ref, bco_ref, extb, 512)
        dtr_ref[...] = _dot(h0b_ref[...], wdt_ref[...])
        dto_ref[...] = _softplus(dtr_ref[...] + dtb_ref[...])

    return pl.pallas_call(
        body, name="ssd_pre_fwd", grid=(T // TB,),
        in_specs=[_row(TB, 1024, C_XS), _prev(TB, H, 1024, C_XS), _row(TB, 512, C_BC), _prev(TB, H, 512, C_BC),
                  _row(TB, D), _full((D, LANES)), _full((K_SSM, 1024)), _full((K_SSM, 512)), _full((1, 1024)),
                  _full((1, 512)), _full((1, LANES))],
        out_specs=[_row(TB, 1024), _row(TB, 512), _row(TB, LANES), _row(TB, LANES)],
        out_shape=[S((T, 1024), F32), S((T, 512), F32), S((T, LANES), F32), S((T, LANES), F32)],
        scratch_shapes=[pltpu.VMEM((H + TB, 1024), F32), pltpu.VMEM((H + TB, 512), F32)],
        compiler_params=_params(("parallel",)))(proj, proj, proj, proj, h0b, w_dt, wx, wb, bx, bb, dt_bias)


def _ssd_conv_bwd(dproj, proj, d_c, w, b, n, col, name):
    TB = TB_SSD_CONV_BWD
    T = proj.shape[0]
    nt = T // TB
    H = HALO_SSM
    R = TB + H

    def body(dproj_ref, t_ref, p_ref, n_ref, d_ref, dn_ref, w_ref, b_ref, o_ref, dw_ref, dbias_ref, ext, dp):
        i = pl.program_id(0)
        first, last = i == 0, i == nt - 1

        @pl.when(first)
        def _():
            dw_ref[...] = jnp.zeros_like(dw_ref)
            dbias_ref[...] = jnp.zeros_like(dbias_ref)

        def blk(cols):
            ext[0:H, cols] = jnp.where(first, 0.0, p_ref[:, cols])
            ext[H:H + TB, cols] = t_ref[:, cols]
            ext[H + TB:, cols] = n_ref[:, cols]
            def taps_and_dsilu(r0, rows):
                taps = [ext[pl.ds(r0 + H - (K_SSM - 1) + k, rows), cols] for k in range(K_SSM)]
                pre = jnp.broadcast_to(b_ref[:, cols], (rows, LANES))
                for k in range(K_SSM):
                    pre = pre + w_ref[k:k + 1, cols] * taps[k]
                return taps, _dsilu(pre, _sigmoid(pre))

            for r0 in range(0, TB, 64):
                taps, ds = taps_and_dsilu(r0, 64)
                dpt = d_ref[pl.ds(r0, 64), cols] * ds
                dp[pl.ds(r0, 64), cols] = dpt
                dbias_ref[:, cols] += jnp.sum(dpt, axis=0, keepdims=True)
                for k in range(K_SSM):
                    dw_ref[k:k + 1, cols] += jnp.sum(dpt * taps[k], axis=0, keepdims=True)
            dp[TB:, cols] = jnp.where(last, 0.0, dn_ref[:, cols] * taps_and_dsilu(TB, H)[1])
            for r0 in range(0, TB, 64):
                acc = jnp.zeros((64, LANES), F32)
                for k in range(K_SSM):
                    acc = acc + w_ref[k:k + 1, cols] * dp[pl.ds(r0 + K_SSM - 1 - k, 64), cols]
                o_ref[pl.ds(r0, 64), cols] = acc.astype(BF16)
        _col_loop(n, blk)

    return pl.pallas_call(
        body, name=name, grid=(nt,),
        in_specs=[_ANY, _row(TB, n, col), _prev(TB, H, n, col), _next(TB, H, n, nt, col),
                  _row(TB, n), _next(TB, H, n, nt), _full((K_SSM, n)), _full((1, n))],
        out_specs=[_row(TB, n, col), _full((K_SSM, n)), _full((1, n))],
        out_shape=[S(dproj.shape, BF16), S((K_SSM, n), F32), S((1, n), F32)],
        input_output_aliases={0: 0},
        scratch_shapes=[pltpu.VMEM((H + TB + H, n), F32), pltpu.VMEM((R, n), F32)],
        compiler_params=_params(("arbitrary",)))(dproj, proj, proj, proj, d_c, d_c, w, b)


def _ssd_consts():
    ex = np.zeros((LANES, D_SSM), np.float32)
    for h in range(N_HEADS):
        ex[h, h * HEAD:(h + 1) * HEAD] = 1.0
    tri = np.tril(np.ones((CHUNK, CHUNK), np.float32))
    return jnp.asarray(ex), jnp.asarray(ex.T.copy()), jnp.asarray(tri), jnp.asarray(tri.T.copy())


def _ssd_common(xs, dt, alog_ref, ex_ref, tri_ref):
    lane = lax.broadcasted_iota(jnp.int32, (1, LANES), 1)
    a = jnp.where(lane < N_HEADS, -jnp.exp(alog_ref[...]), 0.0)
    A = _dot_sel_a(tri_ref[...], dt * a)
    ex = ex_ref[...]
    Aex = _dot_sel_b(A, ex)
    dtex = _dot_sel_b(dt, ex)
    expA = jnp.exp(Aex)
    dec = jnp.exp(Aex[CHUNK - 1:CHUNK, :] - Aex)
    cd = _dot_sel_a(ex, jnp.broadcast_to(jnp.exp(A.T[:, CHUNK - 1:CHUNK]), (LANES, LANES)), _TN)
    return a, A, dtex, expA, dec, cd


def _decay_mask():
    sub = lax.broadcasted_iota(jnp.int32, (CHUNK, CHUNK), 0)
    lane = lax.broadcasted_iota(jnp.int32, (CHUNK, CHUNK), 1)
    return sub, lane, sub >= lane


def _ssd_fwd(xs_c, bc_c, dt, proj, alog, dskip_row, norm_g):
    T = xs_c.shape[0]
    nc = T // CHUNK
    ex, _, tri, _ = _ssd_consts()

    def body(xs_ref, bc_ref, dt_ref, z_ref, alog_ref, dsk_ref, ng_ref, ex_ref, tri_ref,
             ys_ref, ypre_ref, hprev_ref, yst_ref, Hs, ybuf):
        @pl.when(pl.program_id(0) == 0)
        def _():
            Hs[...] = jnp.zeros_like(Hs)

        hprev_ref[0] = Hs[...]
        xs, dt = xs_ref[...], dt_ref[...]
        a, A, dtex, expA, dec, cd = _ssd_common(xs, dt, alog_ref, ex_ref, tri_ref)
        AT = A.T
        xdt = xs * dtex
        xdec = xdt * dec
        _, _, causal = _decay_mask()
        for g in range(2):
            gs = slice(g * 512, (g + 1) * 512)
            B = bc_ref[:, g * N_STATE:(g + 1) * N_STATE]
            C = bc_ref[:, 256 + g * N_STATE:256 + (g + 1) * N_STATE]
            cb = _dot(C, B, _NT)
            Hg = Hs[gs, :]
            yoff = _dot(C, Hg, _NT) * expA[:, gs]
            for j in range(8):
                h = g * 8 + j
                hs = slice(h * HEAD, (h + 1) * HEAD)
                L = jnp.exp(jnp.where(causal, A[:, h:h + 1] - AT[h:h + 1, :], -1e30))
                ybuf[:, hs] = _dot(cb * L, xdt[:, hs]) + yoff[:, j * HEAD:(j + 1) * HEAD]
            Hs[gs, :] = cd[gs, :] * Hg + _dot(xdec[:, gs], B, _TN)
        ypre = ybuf[...] + dsk_ref[...] * xs
        ypre_ref[...] = ypre
        z = z_ref[...]
        yz = ypre * (z * _sigmoid(z))
        for g in range(2):
            gs = slice(g * 512, (g + 1) * 512)
            v = yz[:, gs]
            r = lax.rsqrt(jnp.mean(v * v, axis=-1, keepdims=True) + RMS_EPS)
            ys_ref[:, gs] = (v * r * ng_ref[:, gs]).astype(BF16)
        yst_ref[...] = ys_ref[...].T

    return pl.pallas_call(
        body, name="ssd_fwd", grid=(nc,),
        in_specs=[_row(CHUNK, 1024), _row(CHUNK, 512), _row(CHUNK, LANES), _row(CHUNK, 1024, C_Z),
                  _full((1, LANES)), _full((1, 1024)), _full((1, 1024)), _full((LANES, 1024)), _full((CHUNK, CHUNK))],
        out_specs=[_row(CHUNK, 1024), _row(CHUNK, 1024), pl.BlockSpec((1, 1024, N_STATE), lambda c: (c, 0, 0)),
                   _colt(1024, CHUNK)],
        out_shape=[S((T, 2048), BF16), S((T, 1024), F32), S((nc, 1024, N_STATE), F32), S((2048, T), BF16)],
        scratch_shapes=[pltpu.VMEM((1024, N_STATE), F32), pltpu.VMEM((CHUNK, 1024), F32)],
        compiler_params=_params(("arbitrary",)))(xs_c, bc_c, dt, proj, alog, dskip_row, norm_g, ex, tri)


def _ssd_bwd(dproj, xs_c, bc_c, dt, dt_raw, dt_bias, proj, ypre, hprev, dmix, alog, dskip_row, norm_g, exchange=None):
    T = xs_c.shape[0]
    nc = T // CHUNK
    ex, ext, tri, triu = _ssd_consts()
    rev = lambda n, col=0: pl.BlockSpec((CHUNK, n), lambda c: (nc - 1 - c, col))
    xbufs, kinds = exchange if exchange is not None else ((), ())
    nx = len(xbufs)
    N_IN, N_OUT = 17, 8

    def body(*refs):
        (dproj_ref, xs_ref, bc_ref, dt_ref, dtr_ref, dtb_ref, z_ref, ypre_ref, hprev_ref, dys_ref, alog_ref, dsk_ref,
         ng_ref, ex_ref, ext_ref, tri_ref, triu_ref) = refs[:N_IN]
        (dxs_ref, dbc_ref, ddt_ref, dz_ref, dng_ref, ddsk_ref, dalog_ref,
         ddtb_ref) = refs[N_IN + nx:N_IN + nx + N_OUT]
        dHs, dxbuf, dskacc = refs[N_IN + N_OUT + 2 * nx:N_IN + N_OUT + 2 * nx + 3]
        c = pl.program_id(0)
        if nx:
            start, finish = _exchange_plan(refs[N_IN:N_IN + nx], refs[N_IN + nx + N_OUT:N_IN + N_OUT + 2 * nx], kinds,
                                           *refs[N_IN + N_OUT + 2 * nx + 3:])
            pl.when(c == 0)(start)

        @pl.when(c == 0)
        def _():
            dHs[...] = jnp.zeros_like(dHs)
            dng_ref[...] = jnp.zeros_like(dng_ref)
            dalog_ref[...] = jnp.zeros_like(dalog_ref)
            ddtb_ref[...] = jnp.zeros_like(ddtb_ref)
            dskacc[...] = jnp.zeros_like(dskacc)

        xs, dt, z, ypre, dys = xs_ref[...], dt_ref[...], z_ref[...], ypre_ref[...], dys_ref[...]
        sg = _sigmoid(z)
        sz = z * sg
        yz = ypre * sz
        dyz_parts = []
        for g in range(2):
            gs = slice(g * 512, (g + 1) * 512)
            v = yz[:, gs]
            r = lax.rsqrt(jnp.mean(v * v, axis=-1, keepdims=True) + RMS_EPS)
            vn = v * r
            dng_ref[:, gs] += jnp.sum(dys[:, gs] * vn, axis=0, keepdims=True)
            dvn = dys[:, gs] * ng_ref[:, gs]
            dyz_parts.append(r * (dvn - vn * jnp.mean(dvn * vn, axis=-1, keepdims=True)))
        dyz = jnp.concatenate(dyz_parts, axis=1)
        dy = dyz * sz
        dz_ref[...] = (dyz * ypre * _dsilu(z, sg)).astype(BF16)
        dskacc[...] += jnp.sum(dy * xs, axis=0, keepdims=True)

        a, A, dtex, expA, dec, cd = _ssd_common(xs, dt, alog_ref, ex_ref, tri_ref)
        AT = A.T
        xdt = xs * dtex
        xdec = xdt * dec
        dye = dy * expA
        H = hprev_ref[0]
        dHn = dHs[...]
        sub, lane, causal = _decay_mask()
        dAc = jnp.zeros((CHUNK, LANES), F32)
        Rm = jnp.zeros((CHUNK, LANES), F32)
        yoff_parts, q_parts = [], []
        for g in range(2):
            gs = slice(g * 512, (g + 1) * 512)
            B = bc_ref[:, g * N_STATE:(g + 1) * N_STATE]
            C = bc_ref[:, 256 + g * N_STATE:256 + (g + 1) * N_STATE]
            cb = _dot(C, B, _NT)
            Hg, dHg = H[gs, :], dHn[gs, :]
            Q = _dot(B, dHg, _NT)
            yoff_parts.append(_dot(C, Hg, _NT) * expA[:, gs])
            q_parts.append(Q)
            dcb = jnp.zeros((CHUNK, CHUNK), F32)
            for j in range(8):
                h = g * 8 + j
                hs = slice(h * HEAD, (h + 1) * HEAD)
                L = jnp.exp(jnp.where(causal, A[:, h:h + 1] - AT[h:h + 1, :], -1e30))
                M = cb * L
                G = _dot(dy[:, hs], xdt[:, hs], _NT)
                dxbuf[:, hs] = _dot(M, dy[:, hs], _TN)
                dcb = dcb + G * L
                E = G * M
                dAc = jnp.where(lane == h, jnp.sum(E, axis=1, keepdims=True), dAc)
                Rm = jnp.where(sub == h, jnp.sum(E, axis=0, keepdims=True), Rm)
            dbc_ref[:, g * N_STATE:(g + 1) * N_STATE] = _dot(dcb, C, _TN) + _dot(xdec[:, gs], dHg)
            dbc_ref[:, 256 + g * N_STATE:256 + (g + 1) * N_STATE] = _dot(dcb, B) + _dot(dye[:, gs], Hg)
            dHs[gs, :] = cd[gs, :] * dHg + _dot(dye[:, gs], C, _TN)
        yoff = jnp.concatenate(yoff_parts, axis=1)
        Qd = jnp.concatenate(q_parts, axis=1) * dec
        dxdt = dxbuf[...] + Qd
        extm = ext_ref[...]
        red_s = _dot_sel_b(xdt * Qd, extm)
        dA = dAc - Rm.T + _dot_sel_b(dy * yoff, extm) - red_s
        hd = jnp.sum(_dot_sel_b(H * dHn, extm, _TN), axis=0, keepdims=True)
        last_add = jnp.sum(red_s, axis=0, keepdims=True) + jnp.exp(A[CHUNK - 1:CHUNK, :]) * hd
        dA = dA + jnp.where(sub == CHUNK - 1, last_add, 0.0)
        dadt = _dot_sel_a(triu_ref[...], dA)
        ddtr = (dadt * a + _dot_sel_b(dxdt * xs, extm)) * _sigmoid(dtr_ref[...] + dtb_ref[...])
        ddt_ref[...] = ddtr.astype(BF16)
        ddtb_ref[...] += jnp.sum(ddtr, axis=0, keepdims=True)
        dalog_ref[...] += jnp.sum(dadt * dt, axis=0, keepdims=True) * a
        dxs_ref[...] = dxdt * dtex + dsk_ref[...] * dy

        @pl.when(c == nc - 1)
        def _():
            ddsk_ref[...] = _dot_sel_b(jnp.broadcast_to(dskacc[...], (8, 1024)), extm)[0:1, :]

        if nx:
            pl.when(c == nc - 1)(finish)

    return pl.pallas_call(
        body, name="ssd_bwd", grid=(nc,),
        in_specs=[_ANY, rev(1024), rev(512), rev(LANES), rev(LANES), _full((1, LANES)), rev(1024, C_Z), rev(1024),
                  pl.BlockSpec((1, 1024, N_STATE), lambda c: (nc - 1 - c, 0, 0)), rev(1024, 0),
                  _full((1, LANES)), _full((1, 1024)), _full((1, 1024)),
                  _full((LANES, 1024)), _full((1024, LANES)), _full((CHUNK, CHUNK)), _full((CHUNK, CHUNK))] + [_ANY] * nx,
        out_specs=[rev(1024), rev(512), rev(LANES), rev(1024, C_Z), _full((1, 1024)), _full((1, LANES)),
                   _full((1, LANES)), _full((1, LANES))] + [_ANY] * nx,
        out_shape=[S((T, 1024), F32), S((T, 512), F32), S((T, LANES), BF16), S(dproj.shape, BF16),
                   S((1, 1024), F32), S((1, LANES), F32), S((1, LANES), F32), S((1, LANES), F32)]
        + _exchange_shapes(xbufs, kinds),
        input_output_aliases={0: 3},
        scratch_shapes=[pltpu.VMEM((1024, N_STATE), F32), pltpu.VMEM((CHUNK, 1024), F32), pltpu.VMEM((1, 1024), F32)]
        + (_exchange_sems(nx) if nx else []),
        compiler_params=_params(("arbitrary",)))(
            dproj, xs_c, bc_c, dt, dt_raw, dt_bias, proj, ypre, hprev, dmix, alog, dskip_row, norm_g, ex, ext, tri, triu,
            *xbufs)


def _shifted_copies(ext, ext8):
    n = ext8.shape[1]
    for r in range(8):
        ext8[r] = ext[pl.ds(r, n), :]


def _shifted(ext8, off, rows):
    return ext8[off % 8, pl.ds(off - off % 8, rows), :]


def _conf_fwd(mix, mixt, proj, w, cb, lg, lb, ba, bb):
    T = proj.shape[0]
    H = HALO_CONF

    def body(mix_ref, mixt_ref, ga_ref, gap_ref, gb_ref, gbp_ref, cg_ref, w_ref, cb_ref, lg_ref, lb_ref, ba_ref,
             bb_ref, u1_ref, yc_ref, yct_ref, ext, ext8):
        first = pl.program_id(0) == 0
        ext[H + TB:, :] = jnp.zeros((8, LANES), F32)

        def blk(cols):
            up = (gap_ref[:, cols] + ba_ref[:, cols]) * _sigmoid(gbp_ref[:, cols] + bb_ref[:, cols])
            ext[0:H, :] = jnp.where(first, 0.0, up)
            ext[H:H + TB, :] = (ga_ref[:, cols] + ba_ref[:, cols]) * _sigmoid(gb_ref[:, cols] + bb_ref[:, cols])
            _shifted_copies(ext, ext8)
            for r0 in range(0, TB, 64):
                acc = jnp.broadcast_to(cb_ref[:, cols], (64, LANES))
                for k in range(K_CONF):
                    acc = acc + w_ref[k:k + 1, cols] * _shifted(ext8, r0 + H - (K_CONF - 1) + k, 64)
                u1_ref[pl.ds(r0, 64), cols] = acc
        _col_loop(D_CONF, blk)

        def rows(rs):
            xh, _ = _ln_stats(u1_ref[rs, :])
            u2 = xh * lg_ref[...] + lb_ref[...]
            cg = cg_ref[rs, :]
            yc_ref[rs, :] = (u2 * _sigmoid(u2) * cg * _sigmoid(cg)).astype(BF16)
        _row_loop(TB, rows)
        yct_ref[...] = yc_ref[...].T

    return pl.pallas_call(
        body, name="conf_fwd", grid=(T // TB,),
        in_specs=[_ANY, _ANY, _row(TB, 1024, C_GLUA), _prev(TB, H, 1024, C_GLUA), _row(TB, 1024, C_GLUB),
                  _prev(TB, H, 1024, C_GLUB), _row(TB, 1024, C_CG), _full((K_CONF, 1024))] + [_full((1, 1024))] * 5,
        out_specs=[_row(TB, 1024), _row(TB, 1024, 1), _colt(1024, TB, 1)],
        out_shape=[S((T, 1024), F32), S((T, 2048), BF16), S((2048, T), BF16)],
        input_output_aliases={0: 1, 1: 2},
        scratch_shapes=[pltpu.VMEM((H + TB + 8, LANES), F32), pltpu.VMEM((8, H + TB, LANES), F32)],
        compiler_params=_params(("parallel",)))(mix, mixt, proj, proj, proj, proj, proj, w, cb, lg, lb, ba, bb)


def _d_mix_conf_bwd1(dout, w_out, u1, proj, lg, lb):
    T = u1.shape[0]

    def body(dout_ref, w_ref, u1_ref, cg_ref, lg_ref, lb_ref, dys_ref, du1_ref, dcg_ref, dg_ref, db_ref, dy_ref):
        @pl.when(pl.program_id(0) == 0)
        def _():
            dg_ref[...] = jnp.zeros_like(dg_ref)
            db_ref[...] = jnp.zeros_like(db_ref)

        dys_ref[...] = _dot(dout_ref[...], w_ref[0:D_SSM, :], _NT)
        dy_ref[...] = _dot(dout_ref[...], w_ref[D_SSM:, :], _NT)

        def rows(rs):
            xh, r = _ln_stats(u1_ref[rs, :])
            u2 = xh * lg_ref[...] + lb_ref[...]
            s2 = _sigmoid(u2)
            cg = cg_ref[rs, :]
            sc = _sigmoid(cg)
            dy = dy_ref[rs, :]
            dcg_ref[rs, :] = (dy * u2 * s2 * _dsilu(cg, sc)).astype(BF16)
            dv, dg, db = _ln_bwd(dy * cg * sc * _dsilu(u2, s2), xh, r, lg_ref[...])
            dg_ref[...] += dg
            db_ref[...] += db
            du1_ref[rs, :] = dv
        _row_loop(TB, rows)

    return pl.pallas_call(
        body, name="d_mix_conf_bwd1", grid=(T // TB,),
        in_specs=[_row(TB, D), _full((2 * D, D)), _row(TB, 1024), _row(TB, 1024, C_CG), _full((1, 1024)),
                  _full((1, 1024))],
        out_specs=[_row(TB, 1024), _row(TB, 1024), _row(TB, 1024, C_CG), _full((1, 1024)), _full((1, 1024))],
        out_shape=[S((T, 1024), F32), S((T, 1024), F32), S((T, N_MAIN), BF16), S((1, 1024), F32), S((1, 1024), F32)],
        scratch_shapes=[pltpu.VMEM((TB, D_CONF), F32)],
        compiler_params=_params(("arbitrary",)))(dout, w_out, u1, proj, lg, lb)


def _conf_bwd2(dproj, proj, du1, w, ba, bb):
    T = du1.shape[0]
    nt = T // TB
    H = HALO_CONF

    def body(dproj_ref, ga_ref, gap_ref, gb_ref, gbp_ref, du_ref, dun_ref, w_ref, ba_ref, bb_ref,
             dg_ref, dw_ref, dcb_ref, dba_ref, dbb_ref, ext, dext, ext8, dext8, dwacc):
        i = pl.program_id(0)
        first, last = i == 0, i == nt - 1

        @pl.when(first)
        def _():
            for r in (dcb_ref, dba_ref, dbb_ref, dwacc):
                r[...] = jnp.zeros_like(r)

        ext[H + TB:, :] = jnp.zeros((8, LANES), F32)
        dext[H + TB:, :] = jnp.zeros((8, LANES), F32)

        def blk(cols):
            cols_b = pl.ds(pl.multiple_of(cols.start + D_CONF, LANES), LANES)
            up = (gap_ref[:, cols] + ba_ref[:, cols]) * _sigmoid(gbp_ref[:, cols] + bb_ref[:, cols])
            ext[0:H, :] = jnp.where(first, 0.0, up)
            a = ga_ref[:, cols] + ba_ref[:, cols]
            sb = _sigmoid(gb_ref[:, cols] + bb_ref[:, cols])
            ext[H:H + TB, :] = a * sb
            du = du_ref[:, cols]
            dext[0:TB, :] = du
            dext[TB:TB + H, :] = jnp.where(last, 0.0, dun_ref[:, cols])
            _shifted_copies(ext, ext8)
            _shifted_copies(dext, dext8)
            dcb_ref[:, cols] += jnp.sum(du, axis=0, keepdims=True)
            for r0 in range(0, TB, 64):
                dur = du_ref[pl.ds(r0, 64), cols]
                acc = jnp.zeros((64, LANES), F32)
                for k in range(K_CONF):
                    prod = dur * _shifted(ext8, r0 + H - (K_CONF - 1) + k, 64)
                    dwacc[k * 8:(k + 1) * 8, cols] += prod.reshape(8, 8, LANES).sum(axis=0)
                    acc = acc + w_ref[k:k + 1, cols] * _shifted(dext8, r0 + K_CONF - 1 - k, 64)
                ar, sr = a[r0:r0 + 64], sb[r0:r0 + 64]
                da = acc * sr
                dbv = acc * ar * sr * (1.0 - sr)
                dg_ref[pl.ds(r0, 64), cols] = da.astype(BF16)
                dg_ref[pl.ds(r0, 64), cols_b] = dbv.astype(BF16)
                dba_ref[:, cols] += jnp.sum(da, axis=0, keepdims=True)
                dbb_ref[:, cols] += jnp.sum(dbv, axis=0, keepdims=True)
        _col_loop(D_CONF, blk)

        @pl.when(last)
        def _():
            dw_ref[...] = jnp.sum(dwacc[...].reshape(K_CONF, 8, D_CONF), axis=1)

    return pl.pallas_call(
        body, name="conf_bwd2", grid=(nt,),
        in_specs=[_ANY, _row(TB, 1024, C_GLUA), _prev(TB, H, 1024, C_GLUA), _row(TB, 1024, C_GLUB),
                  _prev(TB, H, 1024, C_GLUB), _row(TB, 1024), _next(TB, H, 1024, nt), _full((K_CONF, 1024)),
                  _full((1, 1024)), _full((1, 1024))],
        out_specs=[_row(TB, 2048), _full((K_CONF, 1024)), _full((1, 1024)), _full((1, 1024)), _full((1, 1024))],
        out_shape=[S(dproj.shape, BF16), S((K_CONF, 1024), F32)] + [S((1, 1024), F32)] * 3,
        input_output_aliases={0: 0},
        scratch_shapes=[pltpu.VMEM((H + TB + 8, LANES), F32), pltpu.VMEM((TB + H + 8, LANES), F32),
                        pltpu.VMEM((8, H + TB, LANES), F32), pltpu.VMEM((8, TB + H, LANES), F32),
                        pltpu.VMEM((K_CONF * 8, D_CONF), F32)],
        compiler_params=_params(("arbitrary",)))(dproj, proj, proj, proj, proj, du1, du1, w, ba, bb)


def _mesh_pos():
    x, y, c = lax.axis_index("x"), lax.axis_index("y"), lax.axis_index("c")
    return x, y, c, 4 * x + 2 * y + c


def _peer(x, y, c, k):
    return (x ^ ((k >> 2) & 1), y ^ ((k >> 1) & 1), c ^ (k & 1))


def _exchange_copies(ins, outs, kinds, send, recv, loc):
    nb = len(ins)
    x, y, c, me = _mesh_pos()
    src = lambda b, d: ins[b].at[d] if kinds[b] == "blocks" else ins[b]
    copies = [pltpu.make_async_copy(src(b, me), outs[b].at[me], loc.at[b]) for b in range(nb)]
    for k in range(1, N_DEV):
        px, py, pc = _peer(x, y, c, k)
        for b in range(nb):
            s = (k - 1) * nb + b
            copies.append(pltpu.make_async_remote_copy(
                src_ref=src(b, 4 * px + 2 * py + pc), dst_ref=outs[b].at[me], send_sem=send.at[s], recv_sem=recv.at[s],
                device_id=(px, py, pc), device_id_type=pl.DeviceIdType.MESH))
    return copies


def _exchange_shapes(bufs, kinds):
    return [S(b.shape if kd == "blocks" else (N_DEV,) + b.shape, b.dtype) for b, kd in zip(bufs, kinds)]


def _exchange_sems(nb):
    n = (N_DEV - 1) * nb
    return [pltpu.SemaphoreType.DMA((n,)), pltpu.SemaphoreType.DMA((n,)), pltpu.SemaphoreType.DMA((nb,))]


def _two_level_gather(ins, outs, send, recv, loc):
    nb = len(ins)
    x, y, c, me = _mesh_pos()
    here, sibling = (x, y, c), (x, y, 1 - c)
    chips = [(1 - x, y), (x, 1 - y), (1 - x, 1 - y)]

    def copy(slot, b, block, to, src=None):
        d = 4 * block[0] + 2 * block[1] + block[2]
        return pltpu.make_async_remote_copy(
            src_ref=outs[b].at[d] if src is None else src, dst_ref=outs[b].at[d],
            send_sem=send.at[slot * nb + b], recv_sem=recv.at[slot * nb + b],
            device_id=to, device_id_type=pl.DeviceIdType.MESH)

    mine = [pltpu.make_async_copy(ins[b], outs[b].at[me], loc.at[b]) for b in range(nb)]
    first = [copy(0, b, here, sibling, src=ins[b]) for b in range(nb)]
    first += [copy(1 + j, b, here, (*chip, c), src=ins[b]) for j, chip in enumerate(chips) for b in range(nb)]

    def start():
        for cp in mine + first:
            cp.start()

    def finish():
        passed = []
        for j, chip in enumerate(chips):
            for b in range(nb):
                copy(1 + j, b, (*chip, c), here).wait_recv()
            onward = [copy(4 + j, b, (*chip, c), sibling) for b in range(nb)]
            for cp in onward:
                cp.start()
            passed += onward
        for b in range(nb):
            copy(0, b, sibling, here).wait_recv()
        for j, chip in enumerate(chips):
            for b in range(nb):
                copy(4 + j, b, (*chip, 1 - c), here).wait_recv()
        for cp in first + passed:
            cp.wait_send()
        for cp in mine:
            cp.wait()

    return start, finish


def _exchange_plan(ins, outs, kinds, send, recv, loc):
    if all(kd == "gather" for kd in kinds):
        return _two_level_gather(ins, outs, send, recv, loc)
    copies = _exchange_copies(ins, outs, kinds, send, recv, loc)

    def start():
        for cp in copies:
            cp.start()

    def finish():
        for cp in copies:
            cp.wait()

    return start, finish


def _exchange(bufs, kinds, name):
    nb = len(bufs)

    def body(*refs):
        start, finish = _exchange_plan(refs[:nb], refs[nb:2 * nb], kinds, *refs[2 * nb:])
        start()
        finish()

    return pl.pallas_call(
        body, name=name, in_specs=[_ANY] * nb, out_specs=[_ANY] * nb,
        out_shape=_exchange_shapes(bufs, kinds), scratch_shapes=_exchange_sems(nb))(*bufs)


def _sum_parts(p_ref):
    acc = p_ref[0].astype(F32)
    for d in range(1, N_DEV):
        acc = acc + p_ref[d].astype(F32)
    return acc


def _adamw_math(g, w, m, v):
    m = ADAM_B1 * m + (1.0 - ADAM_B1) * g
    v = ADAM_B2 * v + (1.0 - ADAM_B2) * (g * g)
    m_hat = m / (1.0 - ADAM_B1 ** ADAM_STEP)
    v_hat = v / (1.0 - ADAM_B2 ** ADAM_STEP)
    return -ADAM_LR * (m_hat / (jnp.sqrt(v_hat) + ADAM_EPS) + ADAM_WD * w), m, v


HEAD_ROWS = 256


def _sum8_adamw(parts, w, m, v, name, head=None):
    _, R, C = w.shape
    tb = HEAD_ROWS if R % HEAD_ROWS == 0 else R
    nb = R // tb
    assert head is None or (tb == HEAD_ROWS and head.shape[1] == HEAD_ROWS and parts.shape[1] == R - HEAD_ROWS)
    skip = 0 if head is None else 1

    def body(*refs):
        p_ref, w_ref, m_ref, v_ref, g_ref, d_ref, mo_ref, vo_ref = refs[skip:]
        g = _sum_parts(p_ref)
        if head is not None:
            g = jnp.where(pl.program_id(0) == nb - 1, _sum_parts(refs[0]), g)
        g_ref[0] = g
        d_ref[0], mo_ref[0], vo_ref[0] = _adamw_math(g, w_ref[0], m_ref[0], v_ref[0])

    first = [] if head is None else [pl.BlockSpec((N_DEV, tb, C), lambda i: (0, 0, 0))]
    own = pl.BlockSpec((1, tb, C), lambda i: (0, i, 0))
    last_part = parts.shape[1] // tb - 1
    return pl.pallas_call(
        body, name=name, grid=(nb,),
        in_specs=first + [pl.BlockSpec((N_DEV, tb, C), lambda i: (0, jnp.minimum(i, last_part), 0))] + [own] * 3,
        out_specs=[own] * 4, out_shape=[S((1, R, C), F32)] * 4,
        compiler_params=_params(("parallel",)))(*([] if head is None else [head]), parts, w, m, v)


SMALL_LAYOUT = (
    ("ln_emb_g", 0, 1024), ("ln_emb_b", 0, 1024), ("ssm_conv_b", 0, 1024), ("ssm_conv_b", 1024, 512),
    ("dt_bias", 0, N_HEADS), ("a_log", 0, N_HEADS), ("d_skip", 0, N_HEADS), ("ssm_norm_g", 0, 1024),
    ("b_glu", 0, 1024), ("b_glu", 1024, 1024), ("conf_conv_b", 0, 1024), ("conf_ln_g", 0, 1024),
    ("conf_ln_b", 0, 1024), ("b_out", 0, 1024), ("ln1_g", 0, 1024), ("ln1_b", 0, 1024), ("ln2_g", 0, 1024),
    ("ln2_b", 0, 1024))
SMALL_ROWS = 24
SMALL = tuple(dict.fromkeys(n for n, _, _ in SMALL_LAYOUT))


LOSS_ROW = len(SMALL_LAYOUT)


def _pack_small(rows, loss):
    def body(*refs):
        o_ref = refs[-1]
        o_ref[...] = jnp.zeros_like(o_ref)
        for r, ref in enumerate(refs[:-2]):
            o_ref[r:r + 1, 0:ref.shape[1]] = ref[...]
        o_ref[LOSS_ROW:LOSS_ROW + 1, 0:LANES] = refs[-2][0:1, :]

    return pl.pallas_call(body, name="pack_small", out_shape=S((SMALL_ROWS, 1024), F32))(*rows, loss)


def _small_update(parts, w, m, v):
    def body(*refs):
        p_ref = refs[0]
        ins = {n: refs[1 + 3 * i:4 + 3 * i] for i, n in enumerate(SMALL)}
        o0 = 1 + 3 * len(SMALL)
        outs = {n: refs[o0 + 4 * i:o0 + 4 * i + 4] for i, n in enumerate(SMALL)}
        gsum = refs[-1]
        gsum[...] = _sum_parts(p_ref)
        refs[-2][...] = gsum[LOSS_ROW:LOSS_ROW + 1, 0:LANES]
        for r, (n, off, wd) in enumerate(SMALL_LAYOUT):
            cs = slice(off, off + wd)
            g = gsum[r:r + 1, 0:wd]
            w_ref, m_ref, v_ref = ins[n]
            g_ref, d_ref, mo_ref, vo_ref = outs[n]
            g_ref[:, cs] = g
            d_ref[:, cs], mo_ref[:, cs], vo_ref[:, cs] = _adamw_math(g, w_ref[:, cs], m_ref[:, cs], v_ref[:, cs])

    args = [parts] + [a for n in SMALL for a in (w[n], m[n], v[n])]
    res = pl.pallas_call(
        body, name="small_update",
        out_shape=[S(w[n].shape, F32) for n in SMALL for _ in range(4)] + [S((1, LANES), F32)],
        scratch_shapes=[pltpu.VMEM((SMALL_ROWS, 1024), F32)])(*args)
    return tuple({n: res[4 * i + j] for i, n in enumerate(SMALL)} for j in range(4)) + (res[-1],)


EARLY = ("w_in", "ssm_conv_w", "conf_conv_w")
LATE = ("w_out", "w_ple_gate", "w_ple_proj")


def _local_step(x, p, tgt, W, shards=None):
    r1 = lambda v: v.reshape(1, -1).astype(F32)
    pad_l = lambda v: jnp.pad(r1(v), ((0, 0), (0, LANES - v.size)))
    late = None if shards is None else [shards[n] for n in LATE]
    if shards is None:
        h0, h0b, h0bt = _ln_emb_fwd(x, r1(W["ln_emb_g"]), r1(W["ln_emb_b"]))
    else:
        h0, h0b, h0bt, *gathered = _ln_emb_fwd(x, r1(W["ln_emb_g"]), r1(W["ln_emb_b"]),
                                               exchange=([shards[n] for n in EARLY], ("gather",) * len(EARLY)))
        W = dict(W, **{n: a if n == "w_in" else _unstack_shards(a, BY_COLS[n]) for n, a in zip(EARLY, gathered)})
    w_main, w_dt = _w_in_to_main(W["w_in"])
    scw, scb = W["ssm_conv_w"], r1(W["ssm_conv_b"])
    wx, wb, bx, bb = scw[:, :1024], scw[:, 1024:], scb[:, :1024], scb[:, 1024:]
    dt_bias, alog = pad_l(W["dt_bias"]), pad_l(W["a_log"])
    dskip_row = jnp.repeat(W["d_skip"].reshape(-1), HEAD).reshape(1, -1)
    norm_g = r1(W["ssm_norm_g"])
    bglu = r1(W["b_glu"])
    ba, bbg = bglu[:, :1024], bglu[:, 1024:]
    ccw, ccb, clg, clb = W["conf_conv_w"], r1(W["conf_conv_b"]), r1(W["conf_ln_g"]), r1(W["conf_ln_b"])

    if late is None:
        proj = _mm(h0b, w_main, "nn", "in_proj", tm=TM_IN_PROJ)
    else:
        proj, *gathered = _mm(h0b, w_main, "nn", "in_proj", tm=TM_IN_PROJ, exchange=(late, ("gather",) * len(LATE)))
        W = dict(W, **{n: _unstack_shards(a, BY_COLS[n]) for n, a in zip(LATE, gathered)})
    xs_c, bc_c, dt, dt_raw = _ssd_pre_fwd(proj, h0b, w_dt, wx, wb, bx, bb, dt_bias)
    mix, ypre, hprev, mixt = _ssd_fwd(xs_c, bc_c, dt, proj, alog, dskip_row, norm_g)
    u1, mix, mixt = _conf_fwd(mix, mixt, proj, ccw, ccb, clg, clb, ba, bbg)
    out, h1, h1b, h1bt = _out_proj_post1(mix, W["w_out"], h0, r1(W["b_out"]), r1(W["ln1_g"]), r1(W["ln1_b"]))
    dh1a, dgp, dple, loss, dln2g, dln2b, pbt = _ple_post2(h1b, W["w_ple_gate"], p, W["w_ple_proj"], h1, tgt,
                                                      r1(W["ln2_g"]), r1(W["ln2_b"]))

    g = {}
    g["w_ple_proj"] = _mm(pbt, dple, "nn", "d_ple_proj", out_dtype=BF16)
    g["w_ple_gate"] = _mm(h1bt, dgp, "nn", "d_ple_gate", out_dtype=BF16)
    dout, dh0a, dln1g, dln1b, dbout = _d_h1_post1_bwd(dgp, W["w_ple_gate"], dh1a, h0, out, r1(W["b_out"]),
                                                      r1(W["ln1_g"]))
    g["w_out"] = _mm(mixt, dout, "nn", "d_w_out", out_dtype=BF16)
    dmix, du1, dproj, dclg, dclb = _d_mix_conf_bwd1(dout, W["w_out"], u1, proj, clg, clb)
    dproj, g["conf_conv_w"], dccb, dba, dbb = _conf_bwd2(dproj, proj, du1, ccw, ba, bbg)
    stack = lambda names: [_stack_shards(g[n], BY_COLS[n]) for n in names]
    dxs_c, dbc_c, ddtr, dproj, dng, ddsk, dalog, ddtb, *recv_a = _ssd_bwd(
        dproj, xs_c, bc_c, dt, dt_raw, dt_bias, proj, ypre, hprev, dmix, alog, dskip_row, norm_g,
        exchange=None if late is None else (stack(LATE), ("blocks",) * len(LATE)))
    dproj, dwx, dbx = _ssd_conv_bwd(dproj, proj, dxs_c, wx, bx, 1024, C_XS, "ssd_conv_bwd_x")
    dproj, dwb, dbb2 = _ssd_conv_bwd(dproj, proj, dbc_c, wb, bb, 512, C_BC, "ssd_conv_bwd_bc")
    g["ssm_conv_w"] = jnp.concatenate([dwx, dwb], axis=1)
    dw_dt = _mm(h0bt, ddtr, "nn", "d_w_dt", out_dtype=BF16)
    last_args = (dproj, w_main, ddtr, w_dt, dh0a, x, r1(W["ln_emb_g"]))
    if late is None:
        g["w_in"] = _w_in_blocks(_mm(h0bt, dproj, "nn", "d_w_in", out_dtype=BF16), dw_dt)
        grad_x, dlng, dlnb = _d_h0_ln_bwd(*last_args)
    else:
        r0 = D - HEAD_ROWS
        head = _w_in_blocks(_mm(h0bt, dproj, "nn", "d_w_in_head", out_dtype=BF16, tk=x.shape[0],
                                a_rows=(r0, HEAD_ROWS)), dw_dt[r0:])
        dw_rest, recv_head = _mm(h0bt, dproj, "nn", "d_w_in", out_dtype=BF16, a_rows=(0, r0),
                                 exchange=([head], ("blocks",)))
        last = ("ssm_conv_w", "conf_conv_w")
        grad_x, dlng, dlnb, *recv_b = _d_h0_ln_bwd(
            *last_args, exchange=([_w_in_blocks(dw_rest, dw_dt[:r0])] + stack(last), ("blocks",) * 3))
        g["recv"] = dict(zip(LATE + ("w_in",) + last, recv_a + recv_b), w_in_head=recv_head)
    g["rows"] = [dlng, dlnb, dbx, dbb2, ddtb, dalog, ddsk, dng, dba, dbb, dccb, dclg, dclb, dbout, dln1g, dln1b,
                 dln2g, dln2b]
    return loss, grad_x, g


W_IN_SEGMENTS = ((0, 2048, 2048), (2048, 5120, 512), (2560, None, N_HEADS), (2576, 0, 2048), (4624, 4096, 1024))


def _w_in_to_main(shards):
    def pieces(p0, width):
        out, p = [], p0
        while p < p0 + width:
            d = p // COLS_PER_DEV
            hi = min(p0 + width, (d + 1) * COLS_PER_DEV)
            out.append(shards[d][:, p - d * COLS_PER_DEV:hi - d * COLS_PER_DEV])
            p = hi
        return out
    main = [s for s in sorted(W_IN_SEGMENTS, key=lambda s: -1 if s[1] is None else s[1]) if s[1] is not None]
    w_main = jnp.concatenate([q for p0, _, width in main for q in pieces(p0, width)], axis=1)
    w_dt = jnp.concatenate(pieces(2560, N_HEADS), axis=1)
    return w_main, jnp.pad(w_dt, ((0, 0), (0, LANES - N_HEADS)))


def _w_in_blocks(dw_main, dw_dt):
    blocks = []
    for d in range(N_DEV):
        lo_d, hi_d = d * COLS_PER_DEV, (d + 1) * COLS_PER_DEV
        parts = []
        for p0, m0, width in W_IN_SEGMENTS:
            lo, hi = max(lo_d, p0), min(hi_d, p0 + width)
            if lo < hi:
                parts.append(dw_dt[:, lo - p0:hi - p0] if m0 is None else dw_main[:, m0 + lo - p0:m0 + hi - p0])
        blocks.append(jnp.concatenate(parts, axis=1))
    return jnp.stack(blocks)


WEIGHTS = ['ln_emb_g', 'ln_emb_b', 'w_in', 'ssm_conv_w', 'ssm_conv_b', 'dt_bias', 'a_log', 'd_skip', 'ssm_norm_g',
           'b_glu', 'conf_conv_w', 'conf_conv_b', 'conf_ln_g', 'conf_ln_b', 'w_out', 'b_out', 'ln1_g', 'ln1_b',
           'w_ple_gate', 'w_ple_proj', 'ln2_g', 'ln2_b']
SHARDED = (("w_in", True), ("w_out", False), ("w_ple_gate", False), ("w_ple_proj", True), ("ssm_conv_w", True),
           ("conf_conv_w", True))
BY_COLS = dict(SHARDED)


def _stack_shards(a, by_cols):
    if by_cols:
        return a.reshape(a.shape[0], N_DEV, a.shape[1] // N_DEV).transpose(1, 0, 2)
    return a.reshape(N_DEV, a.shape[0] // N_DEV, a.shape[1])


def _unstack_shards(a, by_cols):
    if by_cols:
        return a.transpose(1, 0, 2).reshape(a.shape[1], N_DEV * a.shape[2])
    return a.reshape(N_DEV * a.shape[1], a.shape[2])


def kernel(x, p, ln_emb_g, ln_emb_b, w_in, ssm_conv_w, ssm_conv_b, dt_bias, a_log, d_skip, ssm_norm_g, b_glu, conf_conv_w, conf_conv_b, conf_ln_g, conf_ln_b, w_out, b_out, ln1_g, ln1_b, w_ple_gate, w_ple_proj, ln2_g, ln2_b, loss_target, m_ln_emb_g, m_ln_emb_b, m_w_in, m_ssm_conv_w, m_ssm_conv_b, m_dt_bias, m_a_log, m_d_skip, m_ssm_norm_g, m_b_glu, m_conf_conv_w, m_conf_conv_b, m_conf_ln_g, m_conf_ln_b, m_w_out, m_b_out, m_ln1_g, m_ln1_b, m_w_ple_gate, m_w_ple_proj, m_ln2_g, m_ln2_b, v_ln_emb_g, v_ln_emb_b, v_w_in, v_ssm_conv_w, v_ssm_conv_b, v_dt_bias, v_a_log, v_d_skip, v_ssm_norm_g, v_b_glu, v_conf_conv_w, v_conf_conv_b, v_conf_ln_g, v_conf_ln_b, v_w_out, v_b_out, v_ln1_g, v_ln1_b, v_w_ple_gate, v_w_ple_proj, v_ln2_g, v_ln2_b):
    loc = dict(locals())
    w = {n: loc[n] for n in WEIGHTS}
    m = {n: loc["m_" + n] for n in WEIGHTS}
    v = {n: loc["v_" + n] for n in WEIGHTS}
    sharded = [n for n, _ in SHARDED]

    shards = {n: w[n][0].astype(BF16) if n.startswith("w_") else w[n][0] for n in sharded}
    W = {n: w[n].reshape(-1) for n in SMALL}
    loss, grad_x, g = _local_step(x[0], p[0, 0], loss_target[0], W, shards=shards)
    (recv_small,) = _exchange([_pack_small(g["rows"], loss)], ("all",), "small_exchange")

    grads, delta, new_m, new_v = {}, {}, {}, {}
    for n in sharded:
        grads[n], delta[n], new_m[n], new_v[n] = _sum8_adamw(
            g["recv"][n], w[n], m[n], v[n], "adamw_" + n, head=g["recv"]["w_in_head"] if n == "w_in" else None)
    two_d = lambda d: {n: d[n].reshape(1, -1) for n in SMALL}
    *small, loss = _small_update(recv_small, two_d(w), two_d(m), two_d(v))
    for dst, res in zip((grads, delta, new_m, new_v), small):
        for n in SMALL:
            dst[n] = res[n].reshape(w[n].shape)
    return (loss[0, 0], grad_x[None], *[grads[n] for n in WEIGHTS], *[delta[n] for n in WEIGHTS],
            *[new_m[n] for n in WEIGHTS], *[new_v[n] for n in WEIGHTS])
```

```python
import numpy as np
import jax
import jax.numpy as jnp
from jax import lax
from jax.experimental import pallas as pl
from jax.experimental.pallas import tpu as pltpu

F32, BF16 = jnp.float32, jnp.bfloat16
S = jax.ShapeDtypeStruct

N_DEV = 8
D = 1024
D_PLE = 256
D_SSM = 1024
D_CONF = 1024
N_HEADS = 16
HEAD = 64
N_STATE = 128
CHUNK = 128
K_SSM = 4
K_CONF = 31
D_IN = 5648
COLS_PER_DEV = D_IN // N_DEV
LN_EPS = 1e-5
RMS_EPS = 1e-5
ALPHA = 2.0 ** 0.25
LANES = 128
TB = 512
TB_SSD_CONV_BWD = 512
TM_IN_PROJ = 2048
TM_FUSED = 1024
TM_SKEWED = 512
RG = 32
ROW_UNROLL = 8
HALO_SSM = 8
HALO_CONF = 32
VMEM_LIMIT = 56 * 1024 * 1024

ADAM_LR, ADAM_B1, ADAM_B2, ADAM_EPS, ADAM_WD, ADAM_STEP = 0.001, 0.9, 0.999, 1e-08, 0.01, 10

C_GLUA, C_GLUB, C_XS, C_Z, C_CG = 0, 1, 2, 3, 4
C_BC = 10
N_MAIN = 5632


def _params(sem, vmem=VMEM_LIMIT):
    return pltpu.CompilerParams(dimension_semantics=sem, vmem_limit_bytes=vmem)


def _row(tb, n, col=0):
    return pl.BlockSpec((tb, n), lambda i: (i, col))


def _colt(n, tb, row=0):
    return pl.BlockSpec((n, tb), lambda i: (row, i))


def _full(shape):
    return pl.BlockSpec(shape, lambda i: (0,) * len(shape))


_ANY = pl.BlockSpec(memory_space=pl.ANY)


def _prev(tb, halo, n, col=0):
    r = tb // halo
    return pl.BlockSpec((halo, n), lambda i: (jnp.maximum(i * r - 1, 0), col))


def _next(tb, halo, n, nt, col=0):
    r = tb // halo
    return pl.BlockSpec((halo, n), lambda i: (jnp.minimum((i + 1) * r, nt * r - 1), col))


def _row_loop(tb, fn):
    def it(r, c):
        fn(pl.ds(pl.multiple_of(r * RG, RG), RG))
        return c
    lax.fori_loop(0, tb // RG, it, 0, unroll=ROW_UNROLL)


def _col_loop(n, fn):
    def it(j, c):
        fn(pl.ds(pl.multiple_of(j * LANES, LANES), LANES))
        return c
    lax.fori_loop(0, n // LANES, it, 0)


def _sigmoid(x):
    return 1.0 / (1.0 + jnp.exp(-x))


def _dsilu(x, s):
    return s * (1.0 + x * (1.0 - s))


def _ln_stats(v):
    mu = jnp.mean(v, axis=-1, keepdims=True)
    c = v - mu
    r = lax.rsqrt(jnp.mean(c * c, axis=-1, keepdims=True) + LN_EPS)
    return c * r, r


def _ln_bwd(dy, xhat, r, g):
    dxh = dy * g
    dv = r * (dxh - jnp.mean(dxh, axis=-1, keepdims=True) - xhat * jnp.mean(dxh * xhat, axis=-1, keepdims=True))
    return dv, jnp.sum(dy * xhat, axis=0, keepdims=True), jnp.sum(dy, axis=0, keepdims=True)


def _dot(a, b, dims=((1,), (0,))):
    return lax.dot_general(a.astype(BF16), b.astype(BF16), (dims, ((), ())), preferred_element_type=F32)


_NT = ((1,), (1,))
_TN = ((0,), (0,))


def _split3(x):
    hi = x.astype(BF16)
    r = x - hi.astype(F32)
    mid = r.astype(BF16)
    return hi, mid, (r - mid.astype(F32)).astype(BF16)


def _dot_sel_b(a, b, dims=((1,), (0,))):
    hi, mid, lo = _split3(a)
    return (_dot(lo, b, dims) + _dot(mid, b, dims)) + _dot(hi, b, dims)


def _dot_sel_a(a, b, dims=((1,), (0,))):
    hi, mid, lo = _split3(b)
    return (_dot(a, lo, dims) + _dot(a, mid, dims)) + _dot(a, hi, dims)


def _mm(a, b, mode, name, out_dtype=F32, add=None, tm=1024, tn=None, tk=1024, exchange=None, a_rows=None):
    assert mode in ("nn", "nt")
    (M, K), N = a.shape, b.shape[1 if mode == "nn" else 0]
    row0 = 0
    if a_rows is not None:
        row0, M = a_rows
        tm = M
        assert row0 % M == 0
    if tn is None:
        tn = next(t for t in (1024, 1408, 512, 256, LANES) if N % t == 0)
    tm, tn, tk = min(tm, M), min(tn, N), min(tk, K)
    assert M % tm == 0 and N % tn == 0 and K % tk == 0, (name, M, N, K)
    grid = (M // tm, N // tn, K // tk)
    nk = grid[2]
    dims = ((1,), (0,)) if mode == "nn" else _NT
    n_in = 2 + (add is not None)
    xbufs, kinds = exchange if exchange is not None else ((), ())
    nx = len(xbufs)

    def body(*refs):
        a_ref, b_ref = refs[:2]
        o_ref = refs[n_in + nx]
        acc = refs[n_in + 2 * nx + 1]
        i, j, k = pl.program_id(0), pl.program_id(1), pl.program_id(2)
        if nx:
            start, finish = _exchange_plan(refs[n_in:n_in + nx], refs[n_in + nx + 1:n_in + 2 * nx + 1], kinds,
                                           *refs[n_in + 2 * nx + 2:])
            pl.when((i == 0) & (j == 0) & (k == 0))(start)

        d = _dot(a_ref[...], b_ref[...], dims)

        def write_out(r):
            if add is not None:
                r = r + refs[2][...]
            o_ref[...] = r.astype(out_dtype)

        if nk == 1:
            write_out(d)
        else:
            @pl.when(k == 0)
            def _():
                acc[...] = d

            @pl.when((k > 0) & (k < nk - 1))
            def _():
                acc[...] += d

            @pl.when(k == nk - 1)
            def _():
                write_out(acc[...] + d)

        if nx:
            pl.when((i == grid[0] - 1) & (j == grid[1] - 1) & (k == nk - 1))(finish)

    a_spec = pl.BlockSpec((tm, tk), lambda i, j, k: (i + row0 // tm, k))
    b_spec = pl.BlockSpec((tn, tk), lambda i, j, k: (j, k)) if mode == "nt" else pl.BlockSpec((tk, tn), lambda i, j, k: (k, j))
    o_spec = pl.BlockSpec((tm, tn), lambda i, j, k: (i, j))
    ins, specs = [a, b], [a_spec, b_spec]
    if add is not None:
        ins.append(add)
        specs.append(o_spec)
    acc_spec = pltpu.VMEM((tm, tn) if nk > 1 else (8, LANES), F32)
    if not nx:
        return pl.pallas_call(
            body, name=name, grid=grid, in_specs=specs, out_specs=o_spec,
            out_shape=S((M, N), out_dtype), scratch_shapes=[acc_spec],
            compiler_params=_params(("parallel", "parallel", "arbitrary")))(*ins)
    return pl.pallas_call(
        body, name=name, grid=grid, in_specs=specs + [_ANY] * nx, out_specs=[o_spec] + [_ANY] * nx,
        out_shape=[S((M, N), out_dtype)] + _exchange_shapes(xbufs, kinds),
        scratch_shapes=[acc_spec] + _exchange_sems(nx),
        compiler_params=_params(("arbitrary", "arbitrary", "arbitrary")))(*ins, *xbufs)


def _ln_emb_fwd(x, g, b, exchange=None):
    T = x.shape[0]

    nt = T // TB
    xbufs, kinds = exchange if exchange is not None else ((), ())
    nx = len(xbufs)

    def body(*refs):
        x_ref, g_ref, b_ref = refs[:3]
        h_ref, hb_ref, hbt_ref = refs[3 + nx:6 + nx]
        i = pl.program_id(0)
        if nx:
            start, finish = _exchange_plan(refs[3:3 + nx], refs[6 + nx:6 + 2 * nx], kinds, *refs[6 + 2 * nx:])
            pl.when(i == 0)(start)

        def rows(rs):
            xh, _ = _ln_stats(x_ref[rs, :])
            h = xh * g_ref[...] + b_ref[...]
            h_ref[rs, :] = h
            hb_ref[rs, :] = h.astype(BF16)
        _row_loop(TB, rows)
        hbt_ref[...] = hb_ref[...].T
        if nx:
            pl.when(i == nt - 1)(finish)

    return pl.pallas_call(
        body, name="ln_emb_fwd", grid=(nt,),
        in_specs=[_row(TB, D), _full((1, D)), _full((1, D))] + [_ANY] * nx,
        out_specs=[_row(TB, D), _row(TB, D), _colt(D, TB)] + [_ANY] * nx,
        out_shape=[S((T, D), F32), S((T, D), BF16), S((D, T), BF16)] + _exchange_shapes(xbufs, kinds),
        scratch_shapes=_exchange_sems(nx) if nx else [],
        compiler_params=_params(("arbitrary",)))(x, g, b, *xbufs)


def _out_proj_post1(mix, w_out, h0, b_out, g, b, tm=TM_SKEWED):
    T, K = mix.shape
    tm = min(tm, T)
    assert T % tm == 0
    n = T // tm

    def body(mix_ref, w_ref, h0_ref, bo_ref, g_ref, b_ref, out_ref, h_ref, hb_ref, hbt_ref, prev):
        @pl.when(pl.program_id(0) == 0)
        def _():
            prev[...] = jnp.zeros_like(prev)

        d = _dot(mix_ref[...], w_ref[...])
        out_ref[...] = d
        for r0 in range(0, tm, RG):
            rs = pl.ds(r0, RG)
            xh, _ = _ln_stats(ALPHA * h0_ref[rs, :] + prev[rs, :] + bo_ref[...])
            h = xh * g_ref[...] + b_ref[...]
            h_ref[rs, :] = h
            hb_ref[rs, :] = h.astype(BF16)
        hbt_ref[...] = hb_ref[...].T
        prev[...] = d

    cur = lambda i: jnp.minimum(i, n - 1)
    old = lambda i: jnp.maximum(i - 1, 0)
    const = pl.BlockSpec((1, D), lambda i: (0, 0))
    return pl.pallas_call(
        body, name="out_proj_post1", grid=(n + 1,),
        in_specs=[pl.BlockSpec((tm, K), lambda i: (cur(i), 0)), _full((K, D)),
                  pl.BlockSpec((tm, D), lambda i: (old(i), 0)), const, const, const],
        out_specs=[pl.BlockSpec((tm, D), lambda i: (cur(i), 0))] + [pl.BlockSpec((tm, D), lambda i: (old(i), 0))] * 2
        + [pl.BlockSpec((D, tm), lambda i: (0, old(i)))],
        out_shape=[S((T, D), F32), S((T, D), F32), S((T, D), BF16), S((D, T), BF16)],
        scratch_shapes=[pltpu.VMEM((tm, D), F32)],
        compiler_params=_params(("arbitrary",)))(mix, w_out, h0, b_out, g, b)


def _ple_post2(h1b, w_gate, p, w_proj, h1, tgt, g, b, tm=TM_FUSED // 2):
    T = h1.shape[0]
    tm = min(tm, T)
    assert T % tm == 0
    n = T // tm

    def body(h1b_ref, wg_ref, p_ref, wp_ref, h1_ref, tgt_ref, g_ref, b_ref,
             dh1_ref, dgp_ref, dple_ref, loss_ref, dg_ref, db_ref, pt_ref, gp_ref, ple_ref):
        i = pl.program_id(0)

        @pl.when(i == 0)
        def _():
            for r in (loss_ref, dg_ref, db_ref, gp_ref, ple_ref):
                r[...] = jnp.zeros_like(r)

        counts = jnp.where(i > 0, 1.0, 0.0)
        pb = p_ref[...].astype(BF16)
        pt_ref[...] = pb.T
        gp_new = _dot(h1b_ref[...], wg_ref[...])
        ple_new = _dot(pb, wp_ref[...])
        for r0 in range(0, tm, RG):
            rs = pl.ds(r0, RG)
            gate = _sigmoid(gp_ref[rs, :])
            ple = ple_ref[rs, :]
            xh, r = _ln_stats(ALPHA * h1_ref[rs, :] + gate * ple)
            err = xh * g_ref[...] + b_ref[...] - tgt_ref[rs, :]
            loss_ref[...] += counts * 0.5 * jnp.sum(jnp.mean(err * err, axis=-1, keepdims=True), axis=0, keepdims=True)
            dv, dg, db = _ln_bwd(err * (1.0 / D), xh, r, g_ref[...])
            dg_ref[...] += counts * dg
            db_ref[...] += counts * db
            dh1_ref[rs, :] = ALPHA * dv
            dgp_ref[rs, :] = (dv * ple * gate * (1.0 - gate)).astype(BF16)
            dple_ref[rs, :] = (dv * gate).astype(BF16)
        gp_ref[...] = gp_new
        ple_ref[...] = ple_new

    cur = lambda i: jnp.minimum(i, n - 1)
    old = lambda i: jnp.maximum(i - 1, 0)
    at = lambda f, w: pl.BlockSpec((tm, w), lambda i: (f(i), 0))
    return pl.pallas_call(
        body, name="ple_post2", grid=(n + 1,),
        in_specs=[at(cur, D), _full((D, D)), at(cur, D_PLE), _full((D_PLE, D)), at(old, D), at(old, D),
                  _full((1, D)), _full((1, D))],
        out_specs=[at(old, D)] * 3 + [_full((8, LANES)), _full((1, D)), _full((1, D)),
                                      pl.BlockSpec((D_PLE, tm), lambda i: (0, cur(i)))],
        out_shape=[S((T, D), F32), S((T, D), BF16), S((T, D), BF16), S((8, LANES), F32), S((1, D), F32), S((1, D), F32),
                   S((D_PLE, T), BF16)],
        scratch_shapes=[pltpu.VMEM((tm, D), F32), pltpu.VMEM((tm, D), F32)],
        compiler_params=_params(("arbitrary",)))(h1b, w_gate, p, w_proj, h1, tgt, g, b)


def _d_h1_post1_bwd(dgp, w_gate, dh1a, h0, out, b_out, g, tm=TM_FUSED // 2):
    T = h0.shape[0]
    tm = min(tm, T)
    assert T % tm == 0

    def body(dgp_ref, wg_ref, da_ref, h0_ref, out_ref, bo_ref, g_ref, dout_ref, dh0_ref, dg_ref, db_ref, dbo_ref, dh1):
        @pl.when(pl.program_id(0) == 0)
        def _():
            dg_ref[...] = jnp.zeros_like(dg_ref)
            db_ref[...] = jnp.zeros_like(db_ref)
            dbo_ref[...] = jnp.zeros_like(dbo_ref)

        dh1[...] = da_ref[...] + _dot(dgp_ref[...], wg_ref[...], _NT)

        def rows(rs):
            xh, r = _ln_stats(ALPHA * h0_ref[rs, :] + out_ref[rs, :] + bo_ref[...])
            dv, dg, db = _ln_bwd(dh1[rs, :], xh, r, g_ref[...])
            dg_ref[...] += dg
            db_ref[...] += db
            dbo_ref[...] += jnp.sum(dv, axis=0, keepdims=True)
            dout_ref[rs, :] = dv.astype(BF16)
            dh0_ref[rs, :] = ALPHA * dv
        _row_loop(tm, rows)

    return pl.pallas_call(
        body, name="d_h1_post1_bwd", grid=(T // tm,),
        in_specs=[_row(tm, D), _full((D, D))] + [_row(tm, D)] * 3 + [_full((1, D))] * 2,
        out_specs=[_row(tm, D)] * 2 + [_full((1, D))] * 3,
        out_shape=[S((T, D), BF16), S((T, D), F32)] + [S((1, D), F32)] * 3,
        scratch_shapes=[pltpu.VMEM((tm, D), F32)],
        compiler_params=_params(("arbitrary",)))(dgp, w_gate, dh1a, h0, out, b_out, g)


def _d_h0_ln_bwd(dproj, w_main, ddtr, w_dt, dh0a, x, g, exchange=None, tm=1024, tk=1408):
    T, K = dproj.shape
    tm = min(tm, T)
    assert T % tm == 0 and K % tk == 0
    ni, nk = T // tm, K // tk
    xbufs, kinds = exchange if exchange is not None else ((), ())
    nx = len(xbufs)

    def body(*refs):
        dp_ref, w_ref, dt_ref, wdt_ref, da_ref, x_ref, g_ref = refs[:7]
        dx_ref, dg_ref, db_ref = refs[7 + nx:10 + nx]
        acc = refs[10 + 2 * nx]
        i, k = pl.program_id(0), pl.program_id(1)
        if nx:
            start, finish = _exchange_plan(refs[7:7 + nx], refs[10 + nx:10 + 2 * nx], kinds, *refs[11 + 2 * nx:])
            pl.when((i == 0) & (k == 0))(start)

        @pl.when((i == 0) & (k == 0))
        def _():
            dg_ref[...] = jnp.zeros_like(dg_ref)
            db_ref[...] = jnp.zeros_like(db_ref)

        d = _dot(dp_ref[...], w_ref[...], _NT)

        @pl.when(k == 0)
        def _():
            acc[...] = da_ref[...] + _dot(dt_ref[...], wdt_ref[...], _NT) + d

        @pl.when(k > 0)
        def _():
            acc[...] += d

        @pl.when(k == nk - 1)
        def _():
            def rows(rs):
                xh, r = _ln_stats(x_ref[rs, :])
                dv, dg, db = _ln_bwd(acc[rs, :], xh, r, g_ref[...])
                dg_ref[...] += dg
                db_ref[...] += db
                dx_ref[rs, :] = dv
            _row_loop(tm, rows)

        if nx:
            pl.when((i == ni - 1) & (k == nk - 1))(finish)

    rowt = lambda n: pl.BlockSpec((tm, n), lambda i, k: (i, 0))
    const = lambda shape: pl.BlockSpec(shape, lambda i, k: (0, 0))
    return pl.pallas_call(
        body, name="d_h0_ln_bwd", grid=(ni, nk),
        in_specs=[pl.BlockSpec((tm, tk), lambda i, k: (i, k)), pl.BlockSpec((D, tk), lambda i, k: (0, k)),
                  rowt(LANES), const((D, LANES)), rowt(D), rowt(D), const((1, D))] + [_ANY] * nx,
        out_specs=[rowt(D), const((1, D)), const((1, D))] + [_ANY] * nx,
        out_shape=[S((T, D), F32), S((1, D), F32), S((1, D), F32)] + _exchange_shapes(xbufs, kinds),
        scratch_shapes=[pltpu.VMEM((tm, D), F32)] + (_exchange_sems(nx) if nx else []),
        compiler_params=_params(("arbitrary", "arbitrary")))(dproj, w_main, ddtr, w_dt, dh0a, x, g, *xbufs)


def _softplus(x):
    return jnp.maximum(x, 0.0) + jnp.log1p(jnp.exp(-jnp.abs(x)))


def _ssd_pre_fwd(proj, h0b, w_dt, wx, wb, bx, bb, dt_bias):
    T = proj.shape[0]
    H = HALO_SSM

    def body(xs_ref, xsp_ref, bc_ref, bcp_ref, h0b_ref, wdt_ref, wx_ref, wb_ref, bx_ref, bb_ref, dtb_ref,
             xso_ref, bco_ref, dto_ref, dtr_ref, extx, extb):
        first = pl.program_id(0) == 0

        def conv(t_ref, p_ref, w_ref, b_ref, o_ref, ext, n):
            def blk(cols):
                ext[0:H, cols] = jnp.where(first, 0.0, p_ref[:, cols])
                ext[H:, cols] = t_ref[:, cols]
                for r0 in range(0, TB, 64):
                    acc = jnp.broadcast_to(b_ref[:, cols], (64, LANES))
                    for k in range(K_SSM):
                        acc = acc + w_ref[k:k + 1, cols] * ext[pl.ds(r0 + H - (K_SSM - 1) + k, 64), cols]
                    o_ref[pl.ds(r0, 64), cols] = acc * _sigmoid(acc)
            _col_loop(n, blk)

        conv(xs_ref, xsp_ref, wx_ref, bx_ref, xso_ref, extx, D_SSM)
        conv(bc_ref, bcp_ref, wb_ref, bb_ref, bco_ref, extb, 512)
        dtr_ref[...] = _dot(h0b_ref[...], wdt_ref[...])
        dto_ref[...] = _softplus(dtr_ref[...] + dtb_ref[...])

    return pl.pallas_call(
        body, name="ssd_pre_fwd", grid=(T // TB,),
        in_specs=[_row(TB, 1024, C_XS), _prev(TB, H, 1024, C_XS), _row(TB, 512, C_BC), _prev(TB, H, 512, C_BC),
                  _row(TB, D), _full((D, LANES)), _full((K_SSM, 1024)), _full((K_SSM, 512)), _full((1, 1024)),
                  _full((1, 512)), _full((1, LANES))],
        out_specs=[_row(TB, 1024), _row(TB, 512), _row(TB, LANES), _row(TB, LANES)],
        out_shape=[S((T, 1024), F32), S((T, 512), F32), S((T, LANES), F32), S((T, LANES), F32)],
        scratch_shapes=[pltpu.VMEM((H + TB, 1024), F32), pltpu.VMEM((H + TB, 512), F32)],
        compiler_params=_params(("parallel",)))(proj, proj, proj, proj, h0b, w_dt, wx, wb, bx, bb, dt_bias)


def _ssd_conv_bwd(dproj, proj, d_c, w, b, n, col, name):
    TB = TB_SSD_CONV_BWD
    T = proj.shape[0]
    nt = T // TB
    H = HALO_SSM
    R = TB + H

    def body(dproj_ref, t_ref, p_ref, n_ref, d_ref, dn_ref, w_ref, b_ref, o_ref, dw_ref, dbias_ref, ext, dp):
        i = pl.program_id(0)
        first, last = i == 0, i == nt - 1

        @pl.when(first)
        def _():
            dw_ref[...] = jnp.zeros_like(dw_ref)
            dbias_ref[...] = jnp.zeros_like(dbias_ref)

        def blk(cols):
            ext[0:H, cols] = jnp.where(first, 0.0, p_ref[:, cols])
            ext[H:H + TB, cols] = t_ref[:, cols]
            ext[H + TB:, cols] = n_ref[:, cols]
            def taps_and_dsilu(r0, rows):
                taps = [ext[pl.ds(r0 + H - (K_SSM - 1) + k, rows), cols] for k in range(K_SSM)]
                pre = jnp.broadcast_to(b_ref[:, cols], (rows, LANES))
                for k in range(K_SSM):
                    pre = pre + w_ref[k:k + 1, cols] * taps[k]
                return taps, _dsilu(pre, _sigmoid(pre))

            for r0 in range(0, TB, 64):
                taps, ds = taps_and_dsilu(r0, 64)
                dpt = d_ref[pl.ds(r0, 64), cols] * ds
                dp[pl.ds(r0, 64), cols] = dpt
                dbias_ref[:, cols] += jnp.sum(dpt, axis=0, keepdims=True)
                for k in range(K_SSM):
                    dw_ref[k:k + 1, cols] += jnp.sum(dpt * taps[k], axis=0, keepdims=True)
            dp[TB:, cols] = jnp.where(last, 0.0, dn_ref[:, cols] * taps_and_dsilu(TB, H)[1])
            for r0 in range(0, TB, 64):
                acc = jnp.zeros((64, LANES), F32)
                for k in range(K_SSM):
                    acc = acc + w_ref[k:k + 1, cols] * dp[pl.ds(r0 + K_SSM - 1 - k, 64), cols]
                o_ref[pl.ds(r0, 64), cols] = acc.astype(BF16)
        _col_loop(n, blk)

    return pl.pallas_call(
        body, name=name, grid=(nt,),
        in_specs=[_ANY, _row(TB, n, col), _prev(TB, H, n, col), _next(TB, H, n, nt, col),
                  _row(TB, n), _next(TB, H, n, nt), _full((K_SSM, n)), _full((1, n))],
        out_specs=[_row(TB, n, col), _full((K_SSM, n)), _full((1, n))],
        out_shape=[S(dproj.shape, BF16), S((K_SSM, n), F32), S((1, n), F32)],
        input_output_aliases={0: 0},
        scratch_shapes=[pltpu.VMEM((H + TB + H, n), F32), pltpu.VMEM((R, n), F32)],
        compiler_params=_params(("arbitrary",)))(dproj, proj, proj, proj, d_c, d_c, w, b)


def _ssd_consts():
    ex = np.zeros((LANES, D_SSM), np.float32)
    for h in range(N_HEADS):
        ex[h, h * HEAD:(h + 1) * HEAD] = 1.0
    tri = np.tril(np.ones((CHUNK, CHUNK), np.float32))
    return jnp.asarray(ex), jnp.asarray(ex.T.copy()), jnp.asarray(tri), jnp.asarray(tri.T.copy())


def _ssd_common(xs, dt, alog_ref, ex_ref, tri_ref):
    lane = lax.broadcasted_iota(jnp.int32, (1, LANES), 1)
    a = jnp.where(lane < N_HEADS, -jnp.exp(alog_ref[...]), 0.0)
    A = _dot_sel_a(tri_ref[...], dt * a)
    ex = ex_ref[...]
    Aex = _dot_sel_b(A, ex)
    dtex = _dot_sel_b(dt, ex)
    expA = jnp.exp(Aex)
    dec = jnp.exp(Aex[CHUNK - 1:CHUNK, :] - Aex)
    cd = _dot_sel_a(ex, jnp.broadcast_to(jnp.exp(A.T[:, CHUNK - 1:CHUNK]), (LANES, LANES)), _TN)
    return a, A, dtex, expA, dec, cd


def _decay_mask():
    sub = lax.broadcasted_iota(jnp.int32, (CHUNK, CHUNK), 0)
    lane = lax.broadcasted_iota(jnp.int32, (CHUNK, CHUNK), 1)
    return sub, lane, sub >= lane


def _ssd_fwd(xs_c, bc_c, dt, proj, alog, dskip_row, norm_g):
    T = xs_c.shape[0]
    nc = T // CHUNK
    ex, _, tri, _ = _ssd_consts()

    def body(xs_ref, bc_ref, dt_ref, z_ref, alog_ref, dsk_ref, ng_ref, ex_ref, tri_ref,
             ys_ref, ypre_ref, hprev_ref, yst_ref, Hs, ybuf):
        @pl.when(pl.program_id(0) == 0)
        def _():
            Hs[...] = jnp.zeros_like(Hs)

        hprev_ref[0] = Hs[...]
        xs, dt = xs_ref[...], dt_ref[...]
        a, A, dtex, expA, dec, cd = _ssd_common(xs, dt, alog_ref, ex_ref, tri_ref)
        AT = A.T
        xdt = xs * dtex
        xdec = xdt * dec
        _, _, causal = _decay_mask()
        for g in range(2):
            gs = slice(g * 512, (g + 1) * 512)
            B = bc_ref[:, g * N_STATE:(g + 1) * N_STATE]
            C = bc_ref[:, 256 + g * N_STATE:256 + (g + 1) * N_STATE]
            cb = _dot(C, B, _NT)
            Hg = Hs[gs, :]
            yoff = _dot(C, Hg, _NT) * expA[:, gs]
            for j in range(8):
                h = g * 8 + j
                hs = slice(h * HEAD, (h + 1) * HEAD)
                L = jnp.exp(jnp.where(causal, A[:, h:h + 1] - AT[h:h + 1, :], -1e30))
                ybuf[:, hs] = _dot(cb * L, xdt[:, hs]) + yoff[:, j * HEAD:(j + 1) * HEAD]
            Hs[gs, :] = cd[gs, :] * Hg + _dot(xdec[:, gs], B, _TN)
        ypre = ybuf[...] + dsk_ref[...] * xs
        ypre_ref[...] = ypre
        z = z_ref[...]
        yz = ypre * (z * _sigmoid(z))
        for g in range(2):
            gs = slice(g * 512, (g + 1) * 512)
            v = yz[:, gs]
            r = lax.rsqrt(jnp.mean(v * v, axis=-1, keepdims=True) + RMS_EPS)
            ys_ref[:, gs] = (v * r * ng_ref[:, gs]).astype(BF16)
        yst_ref[...] = ys_ref[...].T

    return pl.pallas_call(
        body, name="ssd_fwd", grid=(nc,),
        in_specs=[_row(CHUNK, 1024), _row(CHUNK, 512), _row(CHUNK, LANES), _row(CHUNK, 1024, C_Z),
                  _full((1, LANES)), _full((1, 1024)), _full((1, 1024)), _full((LANES, 1024)), _full((CHUNK, CHUNK))],
        out_specs=[_row(CHUNK, 1024), _row(CHUNK, 1024), pl.BlockSpec((1, 1024, N_STATE), lambda c: (c, 0, 0)),
                   _colt(1024, CHUNK)],
        out_shape=[S((T, 2048), BF16), S((T, 1024), F32), S((nc, 1024, N_STATE), F32), S((2048, T), BF16)],
        scratch_shapes=[pltpu.VMEM((1024, N_STATE), F32), pltpu.VMEM((CHUNK, 1024), F32)],
        compiler_params=_params(("arbitrary",)))(xs_c, bc_c, dt, proj, alog, dskip_row, norm_g, ex, tri)


def _ssd_bwd(dproj, xs_c, bc_c, dt, dt_raw, dt_bias, proj, ypre, hprev, dmix, alog, dskip_row, norm_g, exchange=None):
    T = xs_c.shape[0]
    nc = T // CHUNK
    ex, ext, tri, triu = _ssd_consts()
    rev = lambda n, col=0: pl.BlockSpec((CHUNK, n), lambda c: (nc - 1 - c, col))
    xbufs, kinds = exchange if exchange is not None else ((), ())
    nx = len(xbufs)
    N_IN, N_OUT = 17, 8

    def body(*refs):
        (dproj_ref, xs_ref, bc_ref, dt_ref, dtr_ref, dtb_ref, z_ref, ypre_ref, hprev_ref, dys_ref, alog_ref, dsk_ref,
         ng_ref, ex_ref, ext_ref, tri_ref, triu_ref) = refs[:N_IN]
        (dxs_ref, dbc_ref, ddt_ref, dz_ref, dng_ref, ddsk_ref, dalog_ref,
         ddtb_ref) = refs[N_IN + nx:N_IN + nx + N_OUT]
        dHs, dxbuf, dskacc = refs[N_IN + N_OUT + 2 * nx:N_IN + N_OUT + 2 * nx + 3]
        c = pl.program_id(0)
        if nx:
            start, finish = _exchange_plan(refs[N_IN:N_IN + nx], refs[N_IN + nx + N_OUT:N_IN + N_OUT + 2 * nx], kinds,
                                           *refs[N_IN + N_OUT + 2 * nx + 3:])
            pl.when(c == 0)(start)

        @pl.when(c == 0)
        def _():
            dHs[...] = jnp.zeros_like(dHs)
            dng_ref[...] = jnp.zeros_like(dng_ref)
            dalog_ref[...] = jnp.zeros_like(dalog_ref)
            ddtb_ref[...] = jnp.zeros_like(ddtb_ref)
            dskacc[...] = jnp.zeros_like(dskacc)

        xs, dt, z, ypre, dys = xs_ref[...], dt_ref[...], z_ref[...], ypre_ref[...], dys_ref[...]
        sg = _sigmoid(z)
        sz = z * sg
        yz = ypre * sz
        dyz_parts = []
        for g in range(2):
            gs = slice(g * 512, (g + 1) * 512)
            v = yz[:, gs]
            r = lax.rsqrt(jnp.mean(v * v, axis=-1, keepdims=True) + RMS_EPS)
            vn = v * r
            dng_ref[:, gs] += jnp.sum(dys[:, gs] * vn, axis=0, keepdims=True)
            dvn = dys[:, gs] * ng_ref[:, gs]
            dyz_parts.append(r * (dvn - vn * jnp.mean(dvn * vn, axis=-1, keepdims=True)))
        dyz = jnp.concatenate(dyz_parts, axis=1)
        dy = dyz * sz
        dz_ref[...] = (dyz * ypre * _dsilu(z, sg)).astype(BF16)
        dskacc[...] += jnp.sum(dy * xs, axis=0, keepdims=True)

        a, A, dtex, expA, dec, cd = _ssd_common(xs, dt, alog_ref, ex_ref, tri_ref)
        AT = A.T
        xdt = xs * dtex
        xdec = xdt * dec
        dye = dy * expA
        H = hprev_ref[0]
        dHn = dHs[...]
        sub, lane, causal = _decay_mask()
        dAc = jnp.zeros((CHUNK, LANES), F32)
        Rm = jnp.zeros((CHUNK, LANES), F32)
        yoff_parts, q_parts = [], []
        for g in range(2):
            gs = slice(g * 512, (g + 1) * 512)
            B = bc_ref[:, g * N_STATE:(g + 1) * N_STATE]
            C = bc_ref[:, 256 + g * N_STATE:256 + (g + 1) * N_STATE]
            cb = _dot(C, B, _NT)
            Hg, dHg = H[gs, :], dHn[gs, :]
            Q = _dot(B, dHg, _NT)
            yoff_parts.append(_dot(C, Hg, _NT) * expA[:, gs])
            q_parts.append(Q)
            dcb = jnp.zeros((CHUNK, CHUNK), F32)
            for j in range(8):
                h = g * 8 + j
                hs = slice(h * HEAD, (h + 1) * HEAD)
                L = jnp.exp(jnp.where(causal, A[:, h:h + 1] - AT[h:h + 1, :], -1e30))
                M = cb * L
                G = _dot(dy[:, hs], xdt[:, hs], _NT)
                dxbuf[:, hs] = _dot(M, dy[:, hs], _TN)
                dcb = dcb + G * L
                E = G * M
                dAc = jnp.where(lane == h, jnp.sum(E, axis=1, keepdims=True), dAc)
                Rm = jnp.where(sub == h, jnp.sum(E, axis=0, keepdims=True), Rm)
            dbc_ref[:, g * N_STATE:(g + 1) * N_STATE] = _dot(dcb, C, _TN) + _dot(xdec[:, gs], dHg)
            dbc_ref[:, 256 + g * N_STATE:256 + (g + 1) * N_STATE] = _dot(dcb, B) + _dot(dye[:, gs], Hg)
            dHs[gs, :] = cd[gs, :] * dHg + _dot(dye[:, gs], C, _TN)
        yoff = jnp.concatenate(yoff_parts, axis=1)
        Qd = jnp.concatenate(q_parts, axis=1) * dec
        dxdt = dxbuf[...] + Qd
        extm = ext_ref[...]
        red_s = _dot_sel_b(xdt * Qd, extm)
        dA = dAc - Rm.T + _dot_sel_b(dy * yoff, extm) - red_s
        hd = jnp.sum(_dot_sel_b(H * dHn, extm, _TN), axis=0, keepdims=True)
        last_add = jnp.sum(red_s, axis=0, keepdims=True) + jnp.exp(A[CHUNK - 1:CHUNK, :]) * hd
        dA = dA + jnp.where(sub == CHUNK - 1, last_add, 0.0)
        dadt = _dot_sel_a(triu_ref[...], dA)
        ddtr = (dadt * a + _dot_sel_b(dxdt * xs, extm)) * _sigmoid(dtr_ref[...] + dtb_ref[...])
        ddt_ref[...] = ddtr.astype(BF16)
        ddtb_ref[...] += jnp.sum(ddtr, axis=0, keepdims=True)
        dalog_ref[...] += jnp.sum(dadt * dt, axis=0, keepdims=True) * a
        dxs_ref[...] = dxdt * dtex + dsk_ref[...] * dy

        @pl.when(c == nc - 1)
        def _():
            ddsk_ref[...] = _dot_sel_b(jnp.broadcast_to(dskacc[...], (8, 1024)), extm)[0:1, :]

        if nx:
            pl.when(c == nc - 1)(finish)

    return pl.pallas_call(
        body, name="ssd_bwd", grid=(nc,),
        in_specs=[_ANY, rev(1024), rev(512), rev(LANES), rev(LANES), _full((1, LANES)), rev(1024, C_Z), rev(1024),
                  pl.BlockSpec((1, 1024, N_STATE), lambda c: (nc - 1 - c, 0, 0)), rev(1024, 0),
                  _full((1, LANES)), _full((1, 1024)), _full((1, 1024)),
                  _full((LANES, 1024)), _full((1024, LANES)), _full((CHUNK, CHUNK)), _full((CHUNK, CHUNK))] + [_ANY] * nx,
        out_specs=[rev(1024), rev(512), rev(LANES), rev(1024, C_Z), _full((1, 1024)), _full((1, LANES)),
                   _full((1, LANES)), _full((1, LANES))] + [_ANY] * nx,
        out_shape=[S((T, 1024), F32), S((T, 512), F32), S((T, LANES), BF16), S(dproj.shape, BF16),
                   S((1, 1024), F32), S((1, LANES), F32), S((1, LANES), F32), S((1, LANES), F32)]
        + _exchange_shapes(xbufs, kinds),
        input_output_aliases={0: 3},
        scratch_shapes=[pltpu.VMEM((1024, N_STATE), F32), pltpu.VMEM((CHUNK, 1024), F32), pltpu.VMEM((1, 1024), F32)]
        + (_exchange_sems(nx) if nx else []),
        compiler_params=_params(("arbitrary",)))(
            dproj, xs_c, bc_c, dt, dt_raw, dt_bias, proj, ypre, hprev, dmix, alog, dskip_row, norm_g, ex, ext, tri, triu,
            *xbufs)


def _shifted_copies(ext, ext8):
    n = ext8.shape[1]
    for r in range(8):
        ext8[r] = ext[pl.ds(r, n), :]


def _shifted(ext8, off, rows):
    return ext8[off % 8, pl.ds(off - off % 8, rows), :]


def _conf_fwd(mix, mixt, proj, w, cb, lg, lb, ba, bb):
    T = proj.shape[0]
    H = HALO_CONF

    def body(mix_ref, mixt_ref, ga_ref, gap_ref, gb_ref, gbp_ref, cg_ref, w_ref, cb_ref, lg_ref, lb_ref, ba_ref,
             bb_ref, u1_ref, yc_ref, yct_ref, ext, ext8):
        first = pl.program_id(0) == 0
        ext[H + TB:, :] = jnp.zeros((8, LANES), F32)

        def blk(cols):
            up = (gap_ref[:, cols] + ba_ref[:, cols]) * _sigmoid(gbp_ref[:, cols] + bb_ref[:, cols])
            ext[0:H, :] = jnp.where(first, 0.0, up)
            ext[H:H + TB, :] = (ga_ref[:, cols] + ba_ref[:, cols]) * _sigmoid(gb_ref[:, cols] + bb_ref[:, cols])
            _shifted_copies(ext, ext8)
            for r0 in range(0, TB, 64):
                acc = jnp.broadcast_to(cb_ref[:, cols], (64, LANES))
                for k in range(K_CONF):
                    acc = acc + w_ref[k:k + 1, cols] * _shifted(ext8, r0 + H - (K_CONF - 1) + k, 64)
                u1_ref[pl.ds(r0, 64), cols] = acc
        _col_loop(D_CONF, blk)

        def rows(rs):
            xh, _ = _ln_stats(u1_ref[rs, :])
            u2 = xh * lg_ref[...] + lb_ref[...]
            cg = cg_ref[rs, :]
            yc_ref[rs, :] = (u2 * _sigmoid(u2) * cg * _sigmoid(cg)).astype(BF16)
        _row_loop(TB, rows)
        yct_ref[...] = yc_ref[...].T

    return pl.pallas_call(
        body, name="conf_fwd", grid=(T // TB,),
        in_specs=[_ANY, _ANY, _row(TB, 1024, C_GLUA), _prev(TB, H, 1024, C_GLUA), _row(TB, 1024, C_GLUB),
                  _prev(TB, H, 1024, C_GLUB), _row(TB, 1024, C_CG), _full((K_CONF, 1024))] + [_full((1, 1024))] * 5,
        out_specs=[_row(TB, 1024), _row(TB, 1024, 1), _colt(1024, TB, 1)],
        out_shape=[S((T, 1024), F32), S((T, 2048), BF16), S((2048, T), BF16)],
        input_output_aliases={0: 1, 1: 2},
        scratch_shapes=[pltpu.VMEM((H + TB + 8, LANES), F32), pltpu.VMEM((8, H + TB, LANES), F32)],
        compiler_params=_params(("parallel",)))(mix, mixt, proj, proj, proj, proj, proj, w, cb, lg, lb, ba, bb)


def _d_mix_conf_bwd1(dout, w_out, u1, proj, lg, lb):
    T = u1.shape[0]

    def body(dout_ref, w_ref, u1_ref, cg_ref, lg_ref, lb_ref, dys_ref, du1_ref, dcg_ref, dg_ref, db_ref, dy_ref):
        @pl.when(pl.program_id(0) == 0)
        def _():
            dg_ref[...] = jnp.zeros_like(dg_ref)
            db_ref[...] = jnp.zeros_like(db_ref)

        dys_ref[...] = _dot(dout_ref[...], w_ref[0:D_SSM, :], _NT)
        dy_ref[...] = _dot(dout_ref[...], w_ref[D_SSM:, :], _NT)

        def rows(rs):
            xh, r = _ln_stats(u1_ref[rs, :])
            u2 = xh * lg_ref[...] + lb_ref[...]
            s2 = _sigmoid(u2)
            cg = cg_ref[rs, :]
            sc = _sigmoid(cg)
            dy = dy_ref[rs, :]
            dcg_ref[rs, :] = (dy * u2 * s2 * _dsilu(cg, sc)).astype(BF16)
            dv, dg, db = _ln_bwd(dy * cg * sc * _dsilu(u2, s2), xh, r, lg_ref[...])
            dg_ref[...] += dg
            db_ref[...] += db
            du1_ref[rs, :] = dv
        _row_loop(TB, rows)

    return pl.pallas_call(
        body, name="d_mix_conf_bwd1", grid=(T // TB,),
        in_specs=[_row(TB, D), _full((2 * D, D)), _row(TB, 1024), _row(TB, 1024, C_CG), _full((1, 1024)),
                  _full((1, 1024))],
        out_specs=[_row(TB, 1024), _row(TB, 1024), _row(TB, 1024, C_CG), _full((1, 1024)), _full((1, 1024))],
        out_shape=[S((T, 1024), F32), S((T, 1024), F32), S((T, N_MAIN), BF16), S((1, 1024), F32), S((1, 1024), F32)],
        scratch_shapes=[pltpu.VMEM((TB, D_CONF), F32)],
        compiler_params=_params(("arbitrary",)))(dout, w_out, u1, proj, lg, lb)


def _conf_bwd2(dproj, proj, du1, w, ba, bb):
    T = du1.shape[0]
    nt = T // TB
    H = HALO_CONF

    def body(dproj_ref, ga_ref, gap_ref, gb_ref, gbp_ref, du_ref, dun_ref, w_ref, ba_ref, bb_ref,
             dg_ref, dw_ref, dcb_ref, dba_ref, dbb_ref, ext, dext, ext8, dext8, dwacc):
        i = pl.program_id(0)
        first, last = i == 0, i == nt - 1

        @pl.when(first)
        def _():
            for r in (dcb_ref, dba_ref, dbb_ref, dwacc):
                r[...] = jnp.zeros_like(r)

        ext[H + TB:, :] = jnp.zeros((8, LANES), F32)
        dext[H + TB:, :] = jnp.zeros((8, LANES), F32)

        def blk(cols):
            cols_b = pl.ds(pl.multiple_of(cols.start + D_CONF, LANES), LANES)
            up = (gap_ref[:, cols] + ba_ref[:, cols]) * _sigmoid(gbp_ref[:, cols] + bb_ref[:, cols])
            ext[0:H, :] = jnp.where(first, 0.0, up)
            a = ga_ref[:, cols] + ba_ref[:, cols]
            sb = _sigmoid(gb_ref[:, cols] + bb_ref[:, cols])
            ext[H:H + TB, :] = a * sb
            du = du_ref[:, cols]
            dext[0:TB, :] = du
            dext[TB:TB + H, :] = jnp.where(last, 0.0, dun_ref[:, cols])
            _shifted_copies(ext, ext8)
            _shifted_copies(dext, dext8)
            dcb_ref[:, cols] += jnp.sum(du, axis=0, keepdims=True)
            for r0 in range(0, TB, 64):
                dur = du_ref[pl.ds(r0, 64), cols]
                acc = jnp.zeros((64, LANES), F32)
                for k in range(K_CONF):
                    prod = dur * _shifted(ext8, r0 + H - (K_CONF - 1) + k, 64)
                    dwacc[k * 8:(k + 1) * 8, cols] += prod.reshape(8, 8, LANES).sum(axis=0)
                    acc = acc + w_ref[k:k + 1, cols] * _shifted(dext8, r0 + K_CONF - 1 - k, 64)
                ar, sr = a[r0:r0 + 64], sb[r0:r0 + 64]
                da = acc * sr
                dbv = acc * ar * sr * (1.0 - sr)
                dg_ref[pl.ds(r0, 64), cols] = da.astype(BF16)
                dg_ref[pl.ds(r0, 64), cols_b] = dbv.astype(BF16)
                dba_ref[:, cols] += jnp.sum(da, axis=0, keepdims=True)
                dbb_ref[:, cols] += jnp.sum(dbv, axis=0, keepdims=True)
        _col_loop(D_CONF, blk)

        @pl.when(last)
        def _():
            dw_ref[...] = jnp.sum(dwacc[...].reshape(K_CONF, 8, D_CONF), axis=1)

    return pl.pallas_call(
        body, name="conf_bwd2", grid=(nt,),
        in_specs=[_ANY, _row(TB, 1024, C_GLUA), _prev(TB, H, 1024, C_GLUA), _row(TB, 1024, C_GLUB),
                  _prev(TB, H, 1024, C_GLUB), _row(TB, 1024), _next(TB, H, 1024, nt), _full((K_CONF, 1024)),
                  _full((1, 1024)), _full((1, 1024))],
        out_specs=[_row(TB, 2048), _full((K_CONF, 1024)), _full((1, 1024)), _full((1, 1024)), _full((1, 1024))],
        out_shape=[S(dproj.shape, BF16), S((K_CONF, 1024), F32)] + [S((1, 1024), F32)] * 3,
        input_output_aliases={0: 0},
        scratch_shapes=[pltpu.VMEM((H + TB + 8, LANES), F32), pltpu.VMEM((TB + H + 8, LANES), F32),
                        pltpu.VMEM((8, H + TB, LANES), F32), pltpu.VMEM((8, TB + H, LANES), F32),
                        pltpu.VMEM((K_CONF * 8, D_CONF), F32)],
        compiler_params=_params(("arbitrary",)))(dproj, proj, proj, proj, proj, du1, du1, w, ba, bb)


def _mesh_pos():
    x, y, c = lax.axis_index("x"), lax.axis_index("y"), lax.axis_index("c")
    return x, y, c, 4 * x + 2 * y + c


def _peer(x, y, c, k):
    return (x ^ ((k >> 2) & 1), y ^ ((k >> 1) & 1), c ^ (k & 1))


def _exchange_copies(ins, outs, kinds, send, recv, loc):
    nb = len(ins)
    x, y, c, me = _mesh_pos()
    src = lambda b, d: ins[b].at[d] if kinds[b] == "blocks" else ins[b]
    copies = [pltpu.make_async_copy(src(b, me), outs[b].at[me], loc.at[b]) for b in range(nb)]
    for k in range(1, N_DEV):
        px, py, pc = _peer(x, y, c, k)
        for b in range(nb):
            s = (k - 1) * nb + b
            copies.append(pltpu.make_async_remote_copy(
                src_ref=src(b, 4 * px + 2 * py + pc), dst_ref=outs[b].at[me], send_sem=send.at[s], recv_sem=recv.at[s],
                device_id=(px, py, pc), device_id_type=pl.DeviceIdType.MESH))
    return copies


def _exchange_shapes(bufs, kinds):
    return [S(b.shape if kd == "blocks" else (N_DEV,) + b.shape, b.dtype) for b, kd in zip(bufs, kinds)]


def _exchange_sems(nb):
    n = (N_DEV - 1) * nb
    return [pltpu.SemaphoreType.DMA((n,)), pltpu.SemaphoreType.DMA((n,)), pltpu.SemaphoreType.DMA((nb,))]


def _two_level_gather(ins, outs, send, recv, loc):
    nb = len(ins)
    x, y, c, me = _mesh_pos()
    here, sibling = (x, y, c), (x, y, 1 - c)
    chips = [(1 - x, y), (x, 1 - y), (1 - x, 1 - y)]

    def copy(slot, b, block, to, src=None):
        d = 4 * block[0] + 2 * block[1] + block[2]
        return pltpu.make_async_remote_copy(
            src_ref=outs[b].at[d] if src is None else src, dst_ref=outs[b].at[d],
            send_sem=send.at[slot * nb + b], recv_sem=recv.at[slot * nb + b],
            device_id=to, device_id_type=pl.DeviceIdType.MESH)

    mine = [pltpu.make_async_copy(ins[b], outs[b].at[me], loc.at[b]) for b in range(nb)]
    first = [copy(0, b, here, sibling, src=ins[b]) for b in range(nb)]
    first += [copy(1 + j, b, here, (*chip, c), src=ins[b]) for j, chip in enumerate(chips) for b in range(nb)]

    def start():
        for cp in mine + first:
            cp.start()

    def finish():
        passed = []
        for j, chip in enumerate(chips):
            for b in range(nb):
                copy(1 + j, b, (*chip, c), here).wait_recv()
            onward = [copy(4 + j, b, (*chip, c), sibling) for b in range(nb)]
            for cp in onward:
                cp.start()
            passed += onward
        for b in range(nb):
            copy(0, b, sibling, here).wait_recv()
        for j, chip in enumerate(chips):
            for b in range(nb):
                copy(4 + j, b, (*chip, 1 - c), here).wait_recv()
        for cp in first + passed:
            cp.wait_send()
        for cp in mine:
            cp.wait()

    return start, finish


def _exchange_plan(ins, outs, kinds, send, recv, loc):
    if all(kd == "gather" for kd in kinds):
        return _two_level_gather(ins, outs, send, recv, loc)
    copies = _exchange_copies(ins, outs, kinds, send, recv, loc)

    def start():
        for cp in copies:
            cp.start()

    def finish():
        for cp in copies:
            cp.wait()

    return start, finish


def _exchange(bufs, kinds, name):
    nb = len(bufs)

    def body(*refs):
        start, finish = _exchange_plan(refs[:nb], refs[nb:2 * nb], kinds, *refs[2 * nb:])
        start()
        finish()

    return pl.pallas_call(
        body, name=name, in_specs=[_ANY] * nb, out_specs=[_ANY] * nb,
        out_shape=_exchange_shapes(bufs, kinds), scratch_shapes=_exchange_sems(nb))(*bufs)


def _sum_parts(p_ref):
    acc = p_ref[0].astype(F32)
    for d in range(1, N_DEV):
        acc = acc + p_ref[d].astype(F32)
    return acc


def _adamw_math(g, w, m, v):
    m = ADAM_B1 * m + (1.0 - ADAM_B1) * g
    v = ADAM_B2 * v + (1.0 - ADAM_B2) * (g * g)
    m_hat = m / (1.0 - ADAM_B1 ** ADAM_STEP)
    v_hat = v / (1.0 - ADAM_B2 ** ADAM_STEP)
    return -ADAM_LR * (m_hat / (jnp.sqrt(v_hat) + ADAM_EPS) + ADAM_WD * w), m, v


HEAD_ROWS = 256


def _sum8_adamw(parts, w, m, v, name, head=None):
    _, R, C = w.shape
    tb = HEAD_ROWS if R % HEAD_ROWS == 0 else R
    nb = R // tb
    assert head is None or (tb == HEAD_ROWS and head.shape[1] == HEAD_ROWS and parts.shape[1] == R - HEAD_ROWS)
    skip = 0 if head is None else 1

    def body(*refs):
        p_ref, w_ref, m_ref, v_ref, g_ref, d_ref, mo_ref, vo_ref = refs[skip:]
        g = _sum_parts(p_ref)
        if head is not None:
            g = jnp.where(pl.program_id(0) == nb - 1, _sum_parts(refs[0]), g)
        g_ref[0] = g
        d_ref[0], mo_ref[0], vo_ref[0] = _adamw_math(g, w_ref[0], m_ref[0], v_ref[0])

    first = [] if head is None else [pl.BlockSpec((N_DEV, tb, C), lambda i: (0, 0, 0))]
    own = pl.BlockSpec((1, tb, C), lambda i: (0, i, 0))
    last_part = parts.shape[1] // tb - 1
    return pl.pallas_call(
        body, name=name, grid=(nb,),
        in_specs=first + [pl.BlockSpec((N_DEV, tb, C), lambda i: (0, jnp.minimum(i, last_part), 0))] + [own] * 3,
        out_specs=[own] * 4, out_shape=[S((1, R, C), F32)] * 4,
        compiler_params=_params(("parallel",)))(*([] if head is None else [head]), parts, w, m, v)


SMALL_LAYOUT = (
    ("ln_emb_g", 0, 1024), ("ln_emb_b", 0, 1024), ("ssm_conv_b", 0, 1024), ("ssm_conv_b", 1024, 512),
    ("dt_bias", 0, N_HEADS), ("a_log", 0, N_HEADS), ("d_skip", 0, N_HEADS), ("ssm_norm_g", 0, 1024),
    ("b_glu", 0, 1024), ("b_glu", 1024, 1024), ("conf_conv_b", 0, 1024), ("conf_ln_g", 0, 1024),
    ("conf_ln_b", 0, 1024), ("b_out", 0, 1024), ("ln1_g", 0, 1024), ("ln1_b", 0, 1024), ("ln2_g", 0, 1024),
    ("ln2_b", 0, 1024))
SMALL_ROWS = 24
SMALL = tuple(dict.fromkeys(n for n, _, _ in SMALL_LAYOUT))


LOSS_ROW = len(SMALL_LAYOUT)


def _pack_small(rows, loss):
    def body(*refs):
        o_ref = refs[-1]
        o_ref[...] = jnp.zeros_like(o_ref)
        for r, ref in enumerate(refs[:-2]):
            o_ref[r:r + 1, 0:ref.shape[1]] = ref[...]
        o_ref[LOSS_ROW:LOSS_ROW + 1, 0:LANES] = refs[-2][0:1, :]

    return pl.pallas_call(body, name="pack_small", out_shape=S((SMALL_ROWS, 1024), F32))(*rows, loss)


def _small_update(parts, w, m, v):
    def body(*refs):
        p_ref = refs[0]
        ins = {n: refs[1 + 3 * i:4 + 3 * i] for i, n in enumerate(SMALL)}
        o0 = 1 + 3 * len(SMALL)
        outs = {n: refs[o0 + 4 * i:o0 + 4 * i + 4] for i, n in enumerate(SMALL)}
        gsum = refs[-1]
        gsum[...] = _sum_parts(p_ref)
        refs[-2][...] = gsum[LOSS_ROW:LOSS_ROW + 1, 0:LANES]
        for r, (n, off, wd) in enumerate(SMALL_LAYOUT):
            cs = slice(off, off + wd)
            g = gsum[r:r + 1, 0:wd]
            w_ref, m_ref, v_ref = ins[n]
            g_ref, d_ref, mo_ref, vo_ref = outs[n]
            g_ref[:, cs] = g
            d_ref[:, cs], mo_ref[:, cs], vo_ref[:, cs] = _adamw_math(g, w_ref[:, cs], m_ref[:, cs], v_ref[:, cs])

    args = [parts] + [a for n in SMALL for a in (w[n], m[n], v[n])]
    res = pl.pallas_call(
        body, name="small_update",
        out_shape=[S(w[n].shape, F32) for n in SMALL for _ in range(4)] + [S((1, LANES), F32)],
        scratch_shapes=[pltpu.VMEM((SMALL_ROWS, 1024), F32)])(*args)
    return tuple({n: res[4 * i + j] for i, n in enumerate(SMALL)} for j in range(4)) + (res[-1],)


EARLY = ("w_in", "ssm_conv_w", "conf_conv_w")
LATE = ("w_out", "w_ple_gate", "w_ple_proj")


def _local_step(x, p, tgt, W, shards=None):
    r1 = lambda v: v.reshape(1, -1).astype(F32)
    pad_l = lambda v: jnp.pad(r1(v), ((0, 0), (0, LANES - v.size)))
    late = None if shards is None else [shards[n] for n in LATE]
    if shards is None:
        h0, h0b, h0bt = _ln_emb_fwd(x, r1(W["ln_emb_g"]), r1(W["ln_emb_b"]))
    else:
        h0, h0b, h0bt, *gathered = _ln_emb_fwd(x, r1(W["ln_emb_g"]), r1(W["ln_emb_b"]),
                                               exchange=([shards[n] for n in EARLY], ("gather",) * len(EARLY)))
        W = dict(W, **{n: a if n == "w_in" else _unstack_shards(a, BY_COLS[n]) for n, a in zip(EARLY, gathered)})
    w_main, w_dt = _w_in_to_main(W["w_in"])
    scw, scb = W["ssm_conv_w"], r1(W["ssm_conv_b"])
    wx, wb, bx, bb = scw[:, :1024], scw[:, 1024:], scb[:, :1024], scb[:, 1024:]
    dt_bias, alog = pad_l(W["dt_bias"]), pad_l(W["a_log"])
    dskip_row = jnp.repeat(W["d_skip"].reshape(-1), HEAD).reshape(1, -1)
    norm_g = r1(W["ssm_norm_g"])
    bglu = r1(W["b_glu"])
    ba, bbg = bglu[:, :1024], bglu[:, 1024:]
    ccw, ccb, clg, clb = W["conf_conv_w"], r1(W["conf_conv_b"]), r1(W["conf_ln_g"]), r1(W["conf_ln_b"])

    if late is None:
        proj = _mm(h0b, w_main, "nn", "in_proj", tm=TM_IN_PROJ)
    else:
        proj, *gathered = _mm(h0b, w_main, "nn", "in_proj", tm=TM_IN_PROJ, exchange=(late, ("gather",) * len(LATE)))
        W = dict(W, **{n: _unstack_shards(a, BY_COLS[n]) for n, a in zip(LATE, gathered)})
    xs_c, bc_c, dt, dt_raw = _ssd_pre_fwd(proj, h0b, w_dt, wx, wb, bx, bb, dt_bias)
    mix, ypre, hprev, mixt = _ssd_fwd(xs_c, bc_c, dt, proj, alog, dskip_row, norm_g)
    u1, mix, mixt = _conf_fwd(mix, mixt, proj, ccw, ccb, clg, clb, ba, bbg)
    out, h1, h1b, h1bt = _out_proj_post1(mix, W["w_out"], h0, r1(W["b_out"]), r1(W["ln1_g"]), r1(W["ln1_b"]))
    dh1a, dgp, dple, loss, dln2g, dln2b, pbt = _ple_post2(h1b, W["w_ple_gate"], p, W["w_ple_proj"], h1, tgt,
                                                      r1(W["ln2_g"]), r1(W["ln2_b"]))

    g = {}
    g["w_ple_proj"] = _mm(pbt, dple, "nn", "d_ple_proj", out_dtype=BF16)
    g["w_ple_gate"] = _mm(h1bt, dgp, "nn", "d_ple_gate", out_dtype=BF16)
    dout, dh0a, dln1g, dln1b, dbout = _d_h1_post1_bwd(dgp, W["w_ple_gate"], dh1a, h0, out, r1(W["b_out"]),
                                                      r1(W["ln1_g"]))
    g["w_out"] = _mm(mixt, dout, "nn", "d_w_out", out_dtype=BF16)
    dmix, du1, dproj, dclg, dclb = _d_mix_conf_bwd1(dout, W["w_out"], u1, proj, clg, clb)
    dproj, g["conf_conv_w"], dccb, dba, dbb = _conf_bwd2(dproj, proj, du1, ccw, ba, bbg)
    stack = lambda names: [_stack_shards(g[n], BY_COLS[n]) for n in names]
    dxs_c, dbc_c, ddtr, dproj, dng, ddsk, dalog, ddtb, *recv_a = _ssd_bwd(
        dproj, xs_c, bc_c, dt, dt_raw, dt_bias, proj, ypre, hprev, dmix, alog, dskip_row, norm_g,
        exchange=None if late is None else (stack(LATE), ("blocks",) * len(LATE)))
    dproj, dwx, dbx = _ssd_conv_bwd(dproj, proj, dxs_c, wx, bx, 1024, C_XS, "ssd_conv_bwd_x")
    dproj, dwb, dbb2 = _ssd_conv_bwd(dproj, proj, dbc_c, wb, bb, 512, C_BC, "ssd_conv_bwd_bc")
    g["ssm_conv_w"] = jnp.concatenate([dwx, dwb], axis=1)
    dw_dt = _mm(h0bt, ddtr, "nn", "d_w_dt", out_dtype=BF16)
    last_args = (dproj, w_main, ddtr, w_dt, dh0a, x, r1(W["ln_emb_g"]))
    if late is None:
        g["w_in"] = _w_in_blocks(_mm(h0bt, dproj, "nn", "d_w_in", out_dtype=BF16), dw_dt)
        grad_x, dlng, dlnb = _d_h0_ln_bwd(*last_args)
    else:
        r0 = D - HEAD_ROWS
        head = _w_in_blocks(_mm(h0bt, dproj, "nn", "d_w_in_head", out_dtype=BF16, tk=x.shape[0],
                                a_rows=(r0, HEAD_ROWS)), dw_dt[r0:])
        dw_rest, recv_head = _mm(h0bt, dproj, "nn", "d_w_in", out_dtype=BF16, a_rows=(0, r0),
                                 exchange=([head], ("blocks",)))
        last = ("ssm_conv_w", "conf_conv_w")
        grad_x, dlng, dlnb, *recv_b = _d_h0_ln_bwd(
            *last_args, exchange=([_w_in_blocks(dw_rest, dw_dt[:r0])] + stack(last), ("blocks",) * 3))
        g["recv"] = dict(zip(LATE + ("w_in",) + last, recv_a + recv_b), w_in_head=recv_head)
    g["rows"] = [dlng, dlnb, dbx, dbb2, ddtb, dalog, ddsk, dng, dba, dbb, dccb, dclg, dclb, dbout, dln1g, dln1b,
                 dln2g, dln2b]
    return loss, grad_x, g


W_IN_SEGMENTS = ((0, 2048, 2048), (2048, 5120, 512), (2560, None, N_HEADS), (2576, 0, 2048), (4624, 4096, 1024))


def _w_in_to_main(shards):
    def pieces(p0, width):
        out, p = [], p0
        while p < p0 + width:
            d = p // COLS_PER_DEV
            hi = min(p0 + width, (d + 1) * COLS_PER_DEV)
            out.append(shards[d][:, p - d * COLS_PER_DEV:hi - d * COLS_PER_DEV])
            p = hi
        return out
    main = [s for s in sorted(W_IN_SEGMENTS, key=lambda s: -1 if s[1] is None else s[1]) if s[1] is not None]
    w_main = jnp.concatenate([q for p0, _, width in main for q in pieces(p0, width)], axis=1)
    w_dt = jnp.concatenate(pieces(2560, N_HEADS), axis=1)
    return w_main, jnp.pad(w_dt, ((0, 0), (0, LANES - N_HEADS)))


def _w_in_blocks(dw_main, dw_dt):
    blocks = []
    for d in range(N_DEV):
        lo_d, hi_d = d * COLS_PER_DEV, (d + 1) * COLS_PER_DEV
        parts = []
        for p0, m0, width in W_IN_SEGMENTS:
            lo, hi = max(lo_d, p0), min(hi_d, p0 + width)
            if lo < hi:
                parts.append(dw_dt[:, lo - p0:hi - p0] if m0 is None else dw_main[:, m0 + lo - p0:m0 + hi - p0])
        blocks.append(jnp.concatenate(parts, axis=1))
    return jnp.stack(blocks)


WEIGHTS = ['ln_emb_g', 'ln_emb_b', 'w_in', 'ssm_conv_w', 'ssm_conv_b', 'dt_bias', 'a_log', 'd_skip', 'ssm_norm_g',
           'b_glu', 'conf_conv_w', 'conf_conv_b', 'conf_ln_g', 'conf_ln_b', 'w_out', 'b_out', 'ln1_g', 'ln1_b',
           'w_ple_gate', 'w_ple_proj', 'ln2_g', 'ln2_b']
SHARDED = (("w_in", True), ("w_out", False), ("w_ple_gate", False), ("w_ple_proj", True), ("ssm_conv_w", True),
           ("conf_conv_w", True))
BY_COLS = dict(SHARDED)


def _stack_shards(a, by_cols):
    if by_cols:
        return a.reshape(a.shape[0], N_DEV, a.shape[1] // N_DEV).transpose(1, 0, 2)
    return a.reshape(N_DEV, a.shape[0] // N_DEV, a.shape[1])


def _unstack_shards(a, by_cols):
    if by_cols:
        return a.transpose(1, 0, 2).reshape(a.shape[1], N_DEV * a.shape[2])
    return a.reshape(N_DEV * a.shape[1], a.shape[2])


def kernel(x, p, ln_emb_g, ln_emb_b, w_in, ssm_conv_w, ssm_conv_b, dt_bias, a_log, d_skip, ssm_norm_g, b_glu, conf_conv_w, conf_conv_b, conf_ln_g, conf_ln_b, w_out, b_out, ln1_g, ln1_b, w_ple_gate, w_ple_proj, ln2_g, ln2_b, loss_target, m_ln_emb_g, m_ln_emb_b, m_w_in, m_ssm_conv_w, m_ssm_conv_b, m_dt_bias, m_a_log, m_d_skip, m_ssm_norm_g, m_b_glu, m_conf_conv_w, m_conf_conv_b, m_conf_ln_g, m_conf_ln_b, m_w_out, m_b_out, m_ln1_g, m_ln1_b, m_w_ple_gate, m_w_ple_proj, m_ln2_g, m_ln2_b, v_ln_emb_g, v_ln_emb_b, v_w_in, v_ssm_conv_w, v_ssm_conv_b, v_dt_bias, v_a_log, v_d_skip, v_ssm_norm_g, v_b_glu, v_conf_conv_w, v_conf_conv_b, v_conf_ln_g, v_conf_ln_b, v_w_out, v_b_out, v_ln1_g, v_ln1_b, v_w_ple_gate, v_w_ple_proj, v_ln2_g, v_ln2_b):
    loc = dict(locals())
    w = {n: loc[n] for n in WEIGHTS}
    m = {n: loc["m_" + n] for n in WEIGHTS}
    v = {n: loc["v_" + n] for n in WEIGHTS}
    sharded = [n for n, _ in SHARDED]

    shards = {n: w[n][0].astype(BF16) if n.startswith("w_") else w[n][0] for n in sharded}
    W = {n: w[n].reshape(-1) for n in SMALL}
    loss, grad_x, g = _local_step(x[0], p[0, 0], loss_target[0], W, shards=shards)
    (recv_small,) = _exchange([_pack_small(g["rows"], loss)], ("all",), "small_exchange")

    grads, delta, new_m, new_v = {}, {}, {}, {}
    for n in sharded:
        grads[n], delta[n], new_m[n], new_v[n] = _sum8_adamw(
            g["recv"][n], w[n], m[n], v[n], "adamw_" + n, head=g["recv"]["w_in_head"] if n == "w_in" else None)
    two_d = lambda d: {n: d[n].reshape(1, -1) for n in SMALL}
    *small, loss = _small_update(recv_small, two_d(w), two_d(m), two_d(v))
    for dst, res in zip((grads, delta, new_m, new_v), small):
        for n in SMALL:
            dst[n] = res[n].reshape(w[n].shape)
    return (loss[0, 0], grad_x[None], *[grads[n] for n in WEIGHTS], *[delta[n] for n in WEIGHTS],
            *[new_m[n] for n in WEIGHTS], *[new_v[n] for n in WEIGHTS])
```
